```python
import jax, jax.numpy as jnp
from jax import lax
import numpy as np

D_MODEL = 2048
BATCH = 8
SEQ = 2048
DEPTH = 1

HEAD_DIM = 64
ATTN_HEADS = D_MODEL // 128
KV_HEADS = ATTN_HEADS // 4
Q_DIM = ATTN_HEADS * HEAD_DIM
KV_DIM = KV_HEADS * HEAD_DIM
WINDOW = 128
ATTN_BLOCK = 128
ROPE_THETA = 10000.0
D_INNER = D_MODEL
SSM_HEAD_DIM = 64
SSM_HEADS = D_INNER // SSM_HEAD_DIM
SSM_GROUPS = 4
D_STATE = 128
CONV_WIDTH = 4
CHUNK = 128
CONV_DIM = D_INNER + 2 * SSM_GROUPS * D_STATE
FFN_HIDDEN = -(-(8 * D_MODEL) // (3 * 256)) * 256
PLE_DIM = 256
IN_DIM = Q_DIM + 2 * KV_DIM + D_INNER + CONV_DIM + SSM_HEADS + 2 * D_MODEL
NORM_EPS = 1e-6
SSM_NORM_EPS = 1e-5

kernel_name = "hybrid_swa_sink_ssd_gated_block"


def rmsnorm(x, g, eps=NORM_EPS):
    xf = x.astype(jnp.float32)
    y = xf * lax.rsqrt(jnp.mean(xf * xf, axis=-1, keepdims=True) + eps)
    return (y * g.astype(jnp.float32)).astype(x.dtype)


def apply_rope(t, positions):
    half = HEAD_DIM // 2
    inv_freq = ROPE_THETA ** (-jnp.arange(half, dtype=jnp.float32) * 2.0 / HEAD_DIM)
    ang = positions.astype(jnp.float32)[..., None] * inv_freq
    cos, sin = jnp.cos(ang)[:, :, None, :], jnp.sin(ang)[:, :, None, :]
    t1, t2 = t[..., :half], t[..., half:]
    return jnp.concatenate([t1 * cos - t2 * sin, t2 * cos + t1 * sin], axis=-1)


def sliding_window_sink_attention(q, k, v, sinks):
    b, s = q.shape[0], q.shape[1]
    nb = s // ATTN_BLOCK
    grp = ATTN_HEADS // KV_HEADS
    qb = q.reshape(b, nb, ATTN_BLOCK, KV_HEADS, grp, HEAD_DIM)

    def banded(t):
        tb = t.reshape(b, nb, ATTN_BLOCK, KV_HEADS, HEAD_DIM)
        prev = jnp.pad(tb, ((0, 0), (1, 0), (0, 0), (0, 0), (0, 0)))[:, :-1]
        return jnp.concatenate([prev, tb], axis=2)

    kw, vw = banded(k), banded(v)
    scores = jnp.einsum('bnqhgd,bnkhd->bnhgqk', qb, kw) * (HEAD_DIM ** -0.5)
    qi = jnp.arange(ATTN_BLOCK)[:, None] + ATTN_BLOCK
    kj = jnp.arange(2 * ATTN_BLOCK)[None, :]
    dist = qi - kj
    key_pos = (jnp.arange(nb) * ATTN_BLOCK)[:, None, None] - ATTN_BLOCK + kj[None]
    valid = (dist >= 0)[None] & (dist < WINDOW)[None] & (key_pos >= 0)
    scores = jnp.where(valid[None, :, None, None], scores, -jnp.inf)
    sink = sinks.astype(jnp.float32).reshape(KV_HEADS, grp)[None, None, :, :, None]
    m = jnp.maximum(scores.max(axis=-1), sink)
    e = jnp.exp(scores - m[..., None])
    probs = e / (e.sum(axis=-1) + jnp.exp(sink - m))[..., None]
    out = jnp.einsum('bnhgqk,bnkhd->bnqhgd', probs, vw)
    return out.reshape(b, s, Q_DIM)


def causal_depthwise_conv(x, w, bias):
    out = lax.conv_general_dilated(
        x, w[:, None, :], window_strides=(1,), padding=[(CONV_WIDTH - 1, 0)],
        dimension_numbers=('NWC', 'WIO', 'NWC'), feature_group_count=x.shape[-1])
    return out + bias


def ssd_chunked(xh, dt, a_neg, bm, cm):
    b, s = xh.shape[0], xh.shape[1]
    nc = s // CHUNK
    e_per = SSM_HEADS // SSM_GROUPS
    xd = (xh * dt[..., None]).reshape(b, nc, CHUNK, SSM_GROUPS, e_per, SSM_HEAD_DIM)
    a = jnp.transpose((dt * a_neg).reshape(b, nc, CHUNK, SSM_GROUPS, e_per), (0, 1, 3, 4, 2))
    a_cs = jnp.cumsum(a, axis=-1)
    bc = bm.reshape(b, nc, CHUNK, SSM_GROUPS, D_STATE)
    cc = cm.reshape(b, nc, CHUNK, SSM_GROUPS, D_STATE)
    tril = jnp.tril(jnp.ones((CHUNK, CHUNK), dtype=bool))
    diff = a_cs[..., :, None] - a_cs[..., None, :]
    decay = jnp.where(tril, jnp.exp(jnp.where(tril, diff, 0.0)), 0.0)
    cb = jnp.einsum('bclgn,bcsgn->bcgls', cc, bc)
    y_diag = jnp.einsum('bcgels,bcsgep->bclgep', cb[:, :, :, None] * decay, xd)
    decay_states = jnp.exp(a_cs[..., -1:] - a_cs)
    states = jnp.einsum('bclgn,bcgel,bclgep->bcgepn', bc, decay_states, xd)
    chunk_decay = jnp.exp(a_cs[..., -1])

    def step(carry, inp):
        st, dec = inp
        return carry * dec[..., None, None] + st, carry

    init = jnp.zeros((b, SSM_GROUPS, e_per, SSM_HEAD_DIM, D_STATE), jnp.float32)
    _, prev = lax.scan(step, init, (jnp.moveaxis(states, 1, 0), jnp.moveaxis(chunk_decay, 1, 0)))
    prev = jnp.moveaxis(prev, 0, 1)
    y_off = jnp.einsum('bclgn,bcgepn,bcgel->bclgep', cc, prev, jnp.exp(a_cs))
    return (y_diag + y_off).reshape(b, s, SSM_HEADS, SSM_HEAD_DIM)


def _dense(key, shape, fan_in):
    return jax.random.normal(key, shape, jnp.float32) * (fan_in ** -0.5)


def _fwd_setup_inputs(seed: int = 0) -> dict:
    key = jax.random.key(seed)
    ks = jax.random.split(key, 24)
    L = DEPTH
    ones_noise = lambda k, shape: 1.0 + 0.05 * jax.random.normal(k, shape, jnp.float32)
    dt0 = jnp.exp(jax.random.uniform(ks[5], (L, SSM_HEADS), jnp.float32, np.log(1e-3), np.log(1e-1)))
    return {
        "x": jax.random.normal(ks[0], (BATCH, SEQ, D_MODEL), jnp.float32),
        "p": jax.random.normal(ks[1], (DEPTH, BATCH, SEQ, PLE_DIM), jnp.float32),
        "positions": jnp.tile(jnp.arange(SEQ, dtype=jnp.int32)[None], (BATCH, 1)),
        "g_mix": ones_noise(ks[2], (L, D_MODEL)),
        "w_in": _dense(ks[3], (L, D_MODEL, IN_DIM), D_MODEL),
        "conv_w": _dense(ks[4], (L, CONV_WIDTH, CONV_DIM), CONV_WIDTH),
        "conv_b": 0.02 * jax.random.normal(ks[6], (L, CONV_DIM), jnp.float32),
        "dt_bias": dt0 + jnp.log(-jnp.expm1(-dt0)),
        "a_log": jnp.log(jax.random.uniform(ks[7], (L, SSM_HEADS), jnp.float32, 1.0, 16.0)),
        "d_skip": ones_noise(ks[8], (L, SSM_HEADS)),
        "g_ssd": ones_noise(ks[9], (L, D_INNER)),
        "sinks": 0.5 * jax.random.normal(ks[10], (L, ATTN_HEADS), jnp.float32),
        "w_attn_br": _dense(ks[11], (L, Q_DIM, D_MODEL), Q_DIM),
        "w_ssd_br": _dense(ks[12], (L, D_INNER, D_MODEL), D_INNER),
        "w_o": _dense(ks[13], (L, D_MODEL, D_MODEL), D_MODEL),
        "g_ffn": ones_noise(ks[14], (L, D_MODEL)),
        "w_gate": _dense(ks[15], (L, D_MODEL, FFN_HIDDEN), D_MODEL),
        "w_up": _dense(ks[16], (L, D_MODEL, FFN_HIDDEN), D_MODEL),
        "w_down": _dense(ks[17], (L, FFN_HIDDEN, D_MODEL), FFN_HIDDEN),
        "g_ple": ones_noise(ks[18], (L, D_MODEL)),
        "w_ple_gate": _dense(ks[19], (L, D_MODEL, D_MODEL), D_MODEL),
        "w_ple_proj": _dense(ks[20], (L, PLE_DIM, D_MODEL), PLE_DIM),
        "g_final": ones_noise(ks[21], (D_MODEL,)),
    }


def _fwd_reference(x, p, positions, g_mix, w_in, conv_w, conv_b, dt_bias, a_log, d_skip, g_ssd,
              sinks, w_attn_br, w_ssd_br, w_o, g_ffn, w_gate, w_up, w_down, g_ple,
              w_ple_gate, w_ple_proj, g_final):
    b, s = x.shape[0], x.shape[1]
    f32 = jnp.float32
    sizes = [Q_DIM, KV_DIM, KV_DIM, D_INNER, CONV_DIM, SSM_HEADS, D_MODEL, D_MODEL]
    offsets = [int(o) for o in np.cumsum(sizes)[:-1]]
    h = x
    for i in range(DEPTH):
        u = rmsnorm(h, g_mix[i])
        proj = u @ w_in[i]
        q, k, v, z, xbc, dt_raw, g_a, g_s = jnp.split(proj, offsets, axis=-1)

        q = apply_rope(q.astype(f32).reshape(b, s, ATTN_HEADS, HEAD_DIM), positions)
        k = apply_rope(k.astype(f32).reshape(b, s, KV_HEADS, HEAD_DIM), positions)
        v = v.astype(f32).reshape(b, s, KV_HEADS, HEAD_DIM)
        attn = sliding_window_sink_attention(q, k, v, sinks[i]).astype(x.dtype)
        out_a = attn @ w_attn_br[i]

        xbc = jax.nn.silu(causal_depthwise_conv(xbc, conv_w[i], conv_b[i])).astype(f32)
        xs, bm, cm = jnp.split(xbc, [D_INNER, D_INNER + SSM_GROUPS * D_STATE], axis=-1)
        xh = xs.reshape(b, s, SSM_HEADS, SSM_HEAD_DIM)
        dt = jax.nn.softplus(dt_raw.astype(f32) + dt_bias[i].astype(f32))
        a_neg = -jnp.exp(a_log[i].astype(f32))
        y = ssd_chunked(xh, dt, a_neg,
                        bm.reshape(b, s, SSM_GROUPS, D_STATE),
                        cm.reshape(b, s, SSM_GROUPS, D_STATE))
        y = (y + d_skip[i].astype(f32)[:, None] * xh).reshape(b, s, D_INNER)
        y = rmsnorm(y * jax.nn.silu(z.astype(f32)), g_ssd[i], SSM_NORM_EPS).astype(x.dtype)
        out_s = y @ w_ssd_br[i]

        merged = jax.nn.sigmoid(g_a) * out_a + jax.nn.sigmoid(g_s) * out_s
        h = h + merged @ w_o[i]

        f = rmsnorm(h, g_ffn[i])
        h = h + (jax.nn.silu(f @ w_gate[i]) * (f @ w_up[i])) @ w_down[i]

        gate = jax.nn.sigmoid(rmsnorm(h, g_ple[i]) @ w_ple_gate[i])
        h = h + gate * (p[i] @ w_ple_proj[i])
    return rmsnorm(h, g_final)


import jax as _jax
import jax.numpy as _jnp

TWIN_FORMAT = 'train_step'
FWD_PARAMS = ['x', 'p', 'positions', 'g_mix', 'w_in', 'conv_w', 'conv_b', 'dt_bias', 'a_log', 'd_skip', 'g_ssd', 'sinks', 'w_attn_br', 'w_ssd_br', 'w_o', 'g_ffn', 'w_gate', 'w_up', 'w_down', 'g_ple', 'w_ple_gate', 'w_ple_proj', 'g_final']
TWIN_WEIGHTS = ['g_mix', 'w_in', 'conv_w', 'conv_b', 'dt_bias', 'a_log', 'd_skip', 'g_ssd', 'sinks', 'w_attn_br', 'w_ssd_br', 'w_o', 'g_ffn', 'w_gate', 'w_up', 'w_down', 'g_ple', 'w_ple_gate', 'w_ple_proj', 'g_final']
TWIN_DIFF_INPUT = 'x'
TWIN_INPUTS = ['x', 'p', 'positions', 'g_mix', 'w_in', 'conv_w', 'conv_b', 'dt_bias', 'a_log', 'd_skip', 'g_ssd', 'sinks', 'w_attn_br', 'w_ssd_br', 'w_o', 'g_ffn', 'w_gate', 'w_up', 'w_down', 'g_ple', 'w_ple_gate', 'w_ple_proj', 'g_final', 'loss_target', 'm_g_mix', 'm_w_in', 'm_conv_w', 'm_conv_b', 'm_dt_bias', 'm_a_log', 'm_d_skip', 'm_g_ssd', 'm_sinks', 'm_w_attn_br', 'm_w_ssd_br', 'm_w_o', 'm_g_ffn', 'm_w_gate', 'm_w_up', 'm_w_down', 'm_g_ple', 'm_w_ple_gate', 'm_w_ple_proj', 'm_g_final', 'v_g_mix', 'v_w_in', 'v_conv_w', 'v_conv_b', 'v_dt_bias', 'v_a_log', 'v_d_skip', 'v_g_ssd', 'v_sinks', 'v_w_attn_br', 'v_w_ssd_br', 'v_w_o', 'v_g_ffn', 'v_w_gate', 'v_w_up', 'v_w_down', 'v_g_ple', 'v_w_ple_gate', 'v_w_ple_proj', 'v_g_final']
TWIN_OUTPUTS = ['loss', 'grad_x', 'grad_g_mix', 'grad_w_in', 'grad_conv_w', 'grad_conv_b', 'grad_dt_bias', 'grad_a_log', 'grad_d_skip', 'grad_g_ssd', 'grad_sinks', 'grad_w_attn_br', 'grad_w_ssd_br', 'grad_w_o', 'grad_g_ffn', 'grad_w_gate', 'grad_w_up', 'grad_w_down', 'grad_g_ple', 'grad_w_ple_gate', 'grad_w_ple_proj', 'grad_g_final', 'delta_g_mix', 'delta_w_in', 'delta_conv_w', 'delta_conv_b', 'delta_dt_bias', 'delta_a_log', 'delta_d_skip', 'delta_g_ssd', 'delta_sinks', 'delta_w_attn_br', 'delta_w_ssd_br', 'delta_w_o', 'delta_g_ffn', 'delta_w_gate', 'delta_w_up', 'delta_w_down', 'delta_g_ple', 'delta_w_ple_gate', 'delta_w_ple_proj', 'delta_g_final', 'new_m_g_mix', 'new_m_w_in', 'new_m_conv_w', 'new_m_conv_b', 'new_m_dt_bias', 'new_m_a_log', 'new_m_d_skip', 'new_m_g_ssd', 'new_m_sinks', 'new_m_w_attn_br', 'new_m_w_ssd_br', 'new_m_w_o', 'new_m_g_ffn', 'new_m_w_gate', 'new_m_w_up', 'new_m_w_down', 'new_m_g_ple', 'new_m_w_ple_gate', 'new_m_w_ple_proj', 'new_m_g_final', 'new_v_g_mix', 'new_v_w_in', 'new_v_conv_w', 'new_v_conv_b', 'new_v_dt_bias', 'new_v_a_log', 'new_v_d_skip', 'new_v_g_ssd', 'new_v_sinks', 'new_v_w_attn_br', 'new_v_w_ssd_br', 'new_v_w_o', 'new_v_g_ffn', 'new_v_w_gate', 'new_v_w_up', 'new_v_w_down', 'new_v_g_ple', 'new_v_w_ple_gate', 'new_v_w_ple_proj', 'new_v_g_final']
TWIN_LEAF_KINDS = {'loss': 'loss', 'grad_x': 'grad_x', 'grad_g_mix': 'grad_w', 'grad_w_in': 'grad_w', 'grad_conv_w': 'grad_w', 'grad_conv_b': 'grad_w', 'grad_dt_bias': 'grad_w', 'grad_a_log': 'grad_w', 'grad_d_skip': 'grad_w', 'grad_g_ssd': 'grad_w', 'grad_sinks': 'grad_w', 'grad_w_attn_br': 'grad_w', 'grad_w_ssd_br': 'grad_w', 'grad_w_o': 'grad_w', 'grad_g_ffn': 'grad_w', 'grad_w_gate': 'grad_w', 'grad_w_up': 'grad_w', 'grad_w_down': 'grad_w', 'grad_g_ple': 'grad_w', 'grad_w_ple_gate': 'grad_w', 'grad_w_ple_proj': 'grad_w', 'grad_g_final': 'grad_w', 'delta_g_mix': 'delta_w', 'delta_w_in': 'delta_w', 'delta_conv_w': 'delta_w', 'delta_conv_b': 'delta_w', 'delta_dt_bias': 'delta_w', 'delta_a_log': 'delta_w', 'delta_d_skip': 'delta_w', 'delta_g_ssd': 'delta_w', 'delta_sinks': 'delta_w', 'delta_w_attn_br': 'delta_w', 'delta_w_ssd_br': 'delta_w', 'delta_w_o': 'delta_w', 'delta_g_ffn': 'delta_w', 'delta_w_gate': 'delta_w', 'delta_w_up': 'delta_w', 'delta_w_down': 'delta_w', 'delta_g_ple': 'delta_w', 'delta_w_ple_gate': 'delta_w', 'delta_w_ple_proj': 'delta_w', 'delta_g_final': 'delta_w', 'new_m_g_mix': 'new_m', 'new_m_w_in': 'new_m', 'new_m_conv_w': 'new_m', 'new_m_conv_b': 'new_m', 'new_m_dt_bias': 'new_m', 'new_m_a_log': 'new_m', 'new_m_d_skip': 'new_m', 'new_m_g_ssd': 'new_m', 'new_m_sinks': 'new_m', 'new_m_w_attn_br': 'new_m', 'new_m_w_ssd_br': 'new_m', 'new_m_w_o': 'new_m', 'new_m_g_ffn': 'new_m', 'new_m_w_gate': 'new_m', 'new_m_w_up': 'new_m', 'new_m_w_down': 'new_m', 'new_m_g_ple': 'new_m', 'new_m_w_ple_gate': 'new_m', 'new_m_w_ple_proj': 'new_m', 'new_m_g_final': 'new_m', 'new_v_g_mix': 'new_v', 'new_v_w_in': 'new_v', 'new_v_conv_w': 'new_v', 'new_v_conv_b': 'new_v', 'new_v_dt_bias': 'new_v', 'new_v_a_log': 'new_v', 'new_v_d_skip': 'new_v', 'new_v_g_ssd': 'new_v', 'new_v_sinks': 'new_v', 'new_v_w_attn_br': 'new_v', 'new_v_w_ssd_br': 'new_v', 'new_v_w_o': 'new_v', 'new_v_g_ffn': 'new_v', 'new_v_w_gate': 'new_v', 'new_v_w_up': 'new_v', 'new_v_w_down': 'new_v', 'new_v_g_ple': 'new_v', 'new_v_w_ple_gate': 'new_v', 'new_v_w_ple_proj': 'new_v', 'new_v_g_final': 'new_v'}


def _forward(args):
    return _fwd_reference(*[args[k] for k in FWD_PARAMS])


def _output_shape():
    out = _jax.eval_shape(lambda: _forward(_fwd_setup_inputs(0)))
    return out.shape, out.dtype

N_MICROBATCH = 1
ADAM_LR = 0.001
ADAM_B1 = 0.9
ADAM_B2 = 0.999
ADAM_EPS = 1e-08
ADAM_WD = 0.01
ADAM_STEP = 10
PER_EXAMPLE_BATCH_AXIS = {'x': 0, 'p': 1, 'positions': 0, 'loss_target': 0}
SHARED_INPUTS = []
_WEIGHT_DTYPES = {'g_mix': _jnp.float32, 'w_in': _jnp.float32, 'conv_w': _jnp.float32, 'conv_b': _jnp.float32, 'dt_bias': _jnp.float32, 'a_log': _jnp.float32, 'd_skip': _jnp.float32, 'g_ssd': _jnp.float32, 'sinks': _jnp.float32, 'w_attn_br': _jnp.float32, 'w_ssd_br': _jnp.float32, 'w_o': _jnp.float32, 'g_ffn': _jnp.float32, 'w_gate': _jnp.float32, 'w_up': _jnp.float32, 'w_down': _jnp.float32, 'g_ple': _jnp.float32, 'w_ple_gate': _jnp.float32, 'w_ple_proj': _jnp.float32, 'g_final': _jnp.float32}
MOMENT_SCALE = {'g_mix': 5.147006e-02, 'w_in': 2.194914e-02, 'conv_w': 2.751988e-02, 'conv_b': 3.685245e-02, 'dt_bias': 1.001003e-01, 'a_log': 1.398901e-01, 'd_skip': 1.245472e-01, 'g_ssd': 3.117174e-02, 'sinks': 4.779146e-03, 'w_attn_br': 6.385589e-03, 'w_ssd_br': 3.146052e-02, 'w_o': 3.160683e-02, 'g_ffn': 3.936799e-02, 'w_gate': 1.703747e-02, 'w_up': 1.655751e-02, 'w_down': 2.745244e-02, 'g_ple': 9.955751e-03, 'w_ple_gate': 9.838667e-03, 'w_ple_proj': 2.497740e-02, 'g_final': 8.012516e+00}


def _to_microbatches(a, axis):
    t = _jnp.moveaxis(a, axis, 0)
    t = t.reshape((N_MICROBATCH, t.shape[0] // N_MICROBATCH) + t.shape[1:])
    return _jnp.moveaxis(t, 1, axis + 1)


def setup_inputs(seed: int = 0) -> dict:
    inp = _fwd_setup_inputs(seed)
    key = _jax.random.fold_in(_jax.random.key(seed), 7919)
    shape, _ = _output_shape()
    out = dict(inp)
    out["loss_target"] = _jax.random.normal(_jax.random.fold_in(key, 0), shape, _jnp.float32)
    for i, name in enumerate(TWIN_WEIGHTS):
        w = inp[name].astype(_jnp.float32)
        if MOMENT_SCALE is None:
            s = _jnp.sqrt(_jnp.mean(_jnp.square(w)) + 1e-30)
        else:
            s = MOMENT_SCALE[name]
        km, kv = _jax.random.split(_jax.random.fold_in(key, i + 1))
        out[name] = w
        out["m_" + name] = s * _jax.random.normal(km, w.shape, _jnp.float32)
        out["v_" + name] = (s * s) * _jax.random.uniform(kv, w.shape, _jnp.float32, 0.5, 1.5)
    if N_MICROBATCH > 1:
        for name, axis in PER_EXAMPLE_BATCH_AXIS.items():
            out[name] = _to_microbatches(out[name], axis)
    return {'x': out['x'], 'p': out['p'], 'positions': out['positions'], 'g_mix': out['g_mix'], 'w_in': out['w_in'], 'conv_w': out['conv_w'], 'conv_b': out['conv_b'], 'dt_bias': out['dt_bias'], 'a_log': out['a_log'], 'd_skip': out['d_skip'], 'g_ssd': out['g_ssd'], 'sinks': out['sinks'], 'w_attn_br': out['w_attn_br'], 'w_ssd_br': out['w_ssd_br'], 'w_o': out['w_o'], 'g_ffn': out['g_ffn'], 'w_gate': out['w_gate'], 'w_up': out['w_up'], 'w_down': out['w_down'], 'g_ple': out['g_ple'], 'w_ple_gate': out['w_ple_gate'], 'w_ple_proj': out['w_ple_proj'], 'g_final': out['g_final'], 'loss_target': out['loss_target'], 'm_g_mix': out['m_g_mix'], 'm_w_in': out['m_w_in'], 'm_conv_w': out['m_conv_w'], 'm_conv_b': out['m_conv_b'], 'm_dt_bias': out['m_dt_bias'], 'm_a_log': out['m_a_log'], 'm_d_skip': out['m_d_skip'], 'm_g_ssd': out['m_g_ssd'], 'm_sinks': out['m_sinks'], 'm_w_attn_br': out['m_w_attn_br'], 'm_w_ssd_br': out['m_w_ssd_br'], 'm_w_o': out['m_w_o'], 'm_g_ffn': out['m_g_ffn'], 'm_w_gate': out['m_w_gate'], 'm_w_up': out['m_w_up'], 'm_w_down': out['m_w_down'], 'm_g_ple': out['m_g_ple'], 'm_w_ple_gate': out['m_w_ple_gate'], 'm_w_ple_proj': out['m_w_ple_proj'], 'm_g_final': out['m_g_final'], 'v_g_mix': out['v_g_mix'], 'v_w_in': out['v_w_in'], 'v_conv_w': out['v_conv_w'], 'v_conv_b': out['v_conv_b'], 'v_dt_bias': out['v_dt_bias'], 'v_a_log': out['v_a_log'], 'v_d_skip': out['v_d_skip'], 'v_g_ssd': out['v_g_ssd'], 'v_sinks': out['v_sinks'], 'v_w_attn_br': out['v_w_attn_br'], 'v_w_ssd_br': out['v_w_ssd_br'], 'v_w_o': out['v_w_o'], 'v_g_ffn': out['v_g_ffn'], 'v_w_gate': out['v_w_gate'], 'v_w_up': out['v_w_up'], 'v_w_down': out['v_w_down'], 'v_g_ple': out['v_g_ple'], 'v_w_ple_gate': out['v_w_ple_gate'], 'v_w_ple_proj': out['v_w_ple_proj'], 'v_g_final': out['v_g_final']}


def _loss(weights, diff, rest, loss_target):
    with _jax.named_scope("forward"):
        args = {**rest, TWIN_DIFF_INPUT: diff, **{k: w.astype(_WEIGHT_DTYPES[k]) for k, w in weights.items()}}
        y = _forward(args)
    with _jax.named_scope("loss_head"):
        err = _jnp.square(y.astype(_jnp.float32) - loss_target)
        return 0.5 * _jnp.sum(_jnp.mean(err, axis=-1)) if err.ndim else 0.5 * err


def _adamw(w, g, m, v):
    m = ADAM_B1 * m + (1.0 - ADAM_B1) * g
    v = ADAM_B2 * v + (1.0 - ADAM_B2) * _jnp.square(g)
    m_hat = m / (1.0 - ADAM_B1 ** ADAM_STEP)
    v_hat = v / (1.0 - ADAM_B2 ** ADAM_STEP)
    delta = -ADAM_LR * (m_hat / (_jnp.sqrt(v_hat) + ADAM_EPS) + ADAM_WD * w)
    return delta, m, v


def reference(x, p, positions, g_mix, w_in, conv_w, conv_b, dt_bias, a_log, d_skip, g_ssd, sinks, w_attn_br, w_ssd_br, w_o, g_ffn, w_gate, w_up, w_down, g_ple, w_ple_gate, w_ple_proj, g_final, loss_target, m_g_mix, m_w_in, m_conv_w, m_conv_b, m_dt_bias, m_a_log, m_d_skip, m_g_ssd, m_sinks, m_w_attn_br, m_w_ssd_br, m_w_o, m_g_ffn, m_w_gate, m_w_up, m_w_down, m_g_ple, m_w_ple_gate, m_w_ple_proj, m_g_final, v_g_mix, v_w_in, v_conv_w, v_conv_b, v_dt_bias, v_a_log, v_d_skip, v_g_ssd, v_sinks, v_w_attn_br, v_w_ssd_br, v_w_o, v_g_ffn, v_w_gate, v_w_up, v_w_down, v_g_ple, v_w_ple_gate, v_w_ple_proj, v_g_final):
    given = dict(x=x, p=p, positions=positions, g_mix=g_mix, w_in=w_in, conv_w=conv_w, conv_b=conv_b, dt_bias=dt_bias, a_log=a_log, d_skip=d_skip, g_ssd=g_ssd, sinks=sinks, w_attn_br=w_attn_br, w_ssd_br=w_ssd_br, w_o=w_o, g_ffn=g_ffn, w_gate=w_gate, w_up=w_up, w_down=w_down, g_ple=g_ple, w_ple_gate=w_ple_gate, w_ple_proj=w_ple_proj, g_final=g_final, loss_target=loss_target, m_g_mix=m_g_mix, m_w_in=m_w_in, m_conv_w=m_conv_w, m_conv_b=m_conv_b, m_dt_bias=m_dt_bias, m_a_log=m_a_log, m_d_skip=m_d_skip, m_g_ssd=m_g_ssd, m_sinks=m_sinks, m_w_attn_br=m_w_attn_br, m_w_ssd_br=m_w_ssd_br, m_w_o=m_w_o, m_g_ffn=m_g_ffn, m_w_gate=m_w_gate, m_w_up=m_w_up, m_w_down=m_w_down, m_g_ple=m_g_ple, m_w_ple_gate=m_w_ple_gate, m_w_ple_proj=m_w_ple_proj, m_g_final=m_g_final, v_g_mix=v_g_mix, v_w_in=v_w_in, v_conv_w=v_conv_w, v_conv_b=v_conv_b, v_dt_bias=v_dt_bias, v_a_log=v_a_log, v_d_skip=v_d_skip, v_g_ssd=v_g_ssd, v_sinks=v_sinks, v_w_attn_br=v_w_attn_br, v_w_ssd_br=v_w_ssd_br, v_w_o=v_w_o, v_g_ffn=v_g_ffn, v_w_gate=v_w_gate, v_w_up=v_w_up, v_w_down=v_w_down, v_g_ple=v_g_ple, v_w_ple_gate=v_w_ple_gate, v_w_ple_proj=v_w_ple_proj, v_g_final=v_g_final)
    weights = {n: given[n] for n in TWIN_WEIGHTS}
    shared = {n: given[n] for n in SHARED_INPUTS}
    per_example = {n: given[n] for n in ['x', 'p', 'positions']}
    grad_fn = _jax.value_and_grad(_loss, argnums=(0, 1))

    def one_microbatch(ex, loss_target):
        ex = dict(ex)
        diff = ex.pop(TWIN_DIFF_INPUT)
        return grad_fn(weights, diff, {**shared, **ex}, loss_target)

    if N_MICROBATCH == 1:
        loss, (grad_w, grad_x) = one_microbatch(per_example, given["loss_target"])
    else:
        def body(carry, xs):
            loss_sum, grad_sum = carry
            l_k, (gw_k, gx_k) = one_microbatch(xs[0], xs[1])
            with _jax.named_scope("update"):
                return (loss_sum + l_k, _jax.tree.map(_jnp.add, grad_sum, gw_k)), gx_k

        init = (_jnp.zeros((), _jnp.float32), _jax.tree.map(_jnp.zeros_like, weights))
        (loss, grad_w), grad_x = _jax.lax.scan(body, init, (per_example, given["loss_target"]))
    with _jax.named_scope("update"):
        delta_w, new_m, new_v = {}, {}, {}
        for n in TWIN_WEIGHTS:
            delta_w[n], new_m[n], new_v[n] = _adamw(weights[n], grad_w[n], given["m_" + n], given["v_" + n])
    return (loss, grad_x, *[grad_w[n] for n in TWIN_WEIGHTS], *[delta_w[n] for n in TWIN_WEIGHTS],
            *[new_m[n] for n in TWIN_WEIGHTS], *[new_v[n] for n in TWIN_WEIGHTS])
```

```python
import functools

import numpy as np
import jax
import jax.numpy as jnp
from jax import lax
from jax.experimental import pallas as pl
from jax.experimental.pallas import tpu as pltpu

F32 = jnp.float32
BF16 = jnp.bfloat16

N_DEV = 8
D_MODEL = 2048
HEAD_DIM = 64
ATTN_HEADS = 16
KV_HEADS = 4
Q_DIM = 1024
KV_DIM = 256
BLK = 128
D_INNER = 2048
SSM_HEADS = 32
SSM_GROUPS = 4
HEADS_PER_GROUP = 8
D_STATE = 128
CONV_WIDTH = 4
CONV_DIM = 3072
FFN_HIDDEN = 5632
PLE_DIM = 256
IN_DIM = 10784
NORM_EPS = 1e-6
SSM_NORM_EPS = 1e-5
ROPE_THETA = 10000.0

OFF_Z, OFF_GA, OFF_GS, OFF_XBC, OFF_Q, OFF_K, OFF_V, OFF_DT = 0, 2048, 4096, 6144, 9216, 10240, 10496, 10752
IN_PAD = 10880
DT_PAD = 128
SEG = dict(q=(0, 1024), k=(1024, 256), v=(1280, 256), z=(1536, 2048), xbc=(3584, 3072), dt=(6656, 32),
           ga=(6688, 2048), gs=(8736, 2048))

ADAM_LR, ADAM_B1, ADAM_B2, ADAM_EPS, ADAM_WD, ADAM_STEP = 0.001, 0.9, 0.999, 1e-08, 0.01, 10

VMEM_LIMIT = 56 * 1024 * 1024

NN = (((1,), (0,)), ((), ()))
NT = (((1,), (1,)), ((), ()))
TN = (((0,), (0,)), ((), ()))


def _pcall(body, **kw):
    return pl.pallas_call(body, **kw)


def _cparams(sem=None):
    if sem is None:
        return pltpu.CompilerParams(vmem_limit_bytes=VMEM_LIMIT)
    return pltpu.CompilerParams(vmem_limit_bytes=VMEM_LIMIT, dimension_semantics=sem)


def _dot(a, b, dn):
    return lax.dot_general(a.astype(BF16), b.astype(BF16), dn, preferred_element_type=F32)


def _sigmoid(x):
    return 1.0 / (1.0 + jnp.exp(-x))


def _silu(x):
    return x * _sigmoid(x)


def _dsilu(x):
    s = _sigmoid(x)
    return s * (1.0 + x * (1.0 - s))


def _matmul(name, pairs, pair_specs, dn, grid, out_shapes, out_specs, nred=1, extra=(), extra_specs=(),
            epilogue=None, acc_shape=None):
    n_in = 2 * len(pairs) + len(extra)
    n_out = len(out_shapes)

    def body(*refs):
        ins = refs[:2 * len(pairs)]
        ex = refs[2 * len(pairs):n_in]
        outs = refs[n_in:n_in + n_out]

        def prod():
            s = None
            for p in range(len(pairs)):
                d = _dot(ins[2 * p][...], ins[2 * p + 1][...], dn)
                s = d if s is None else s + d
            return s

        def finish(val):
            if epilogue is None:
                outs[0][...] = val.astype(outs[0].dtype)
            else:
                res = epilogue(val, *[e[...] for e in ex])
                for o, r in zip(outs, res):
                    o[...] = r.astype(o.dtype)

        if nred == 1:
            finish(prod())
        else:
            acc = refs[n_in + n_out]
            k = pl.program_id(len(grid) - 1)

            @pl.when(k == 0)
            def _():
                acc[...] = jnp.zeros_like(acc)

            acc[...] += prod()

            @pl.when(k == nred - 1)
            def _():
                finish(acc[...])

    operands = []
    specs = []
    for (a, b), (sa, sb) in zip(pairs, pair_specs):
        operands += [a, b]
        specs += [sa, sb]
    operands += list(extra)
    specs += list(extra_specs)
    scratch = [pltpu.VMEM(acc_shape, F32)] if nred > 1 else []
    sem = ("arbitrary",) * len(grid)
    res = _pcall(body, name=name, grid=grid, in_specs=specs, out_specs=list(out_specs),
                 out_shape=list(out_shapes), scratch_shapes=scratch, compiler_params=_cparams(sem))(*operands)
    return res


def _sds(shape, dtype):
    return jax.ShapeDtypeStruct(shape, dtype)


def _row_tile(T):
    return min(1024, T)


def mm_nn(name, a, b, tn, out_dtype=F32, residual=None):
    M, K = a.shape
    N = b.shape[1]
    tm = _row_tile(M)
    grid = (M // tm, N // tn)
    extra, especs, epi = (), (), None
    if residual is not None:
        extra = (residual,)
        especs = (pl.BlockSpec((tm, tn), lambda i, n: (i, n)),)
        epi = lambda v, r: (v + r,)
    return _matmul(name, [(a, b)], [(pl.BlockSpec((tm, K), lambda i, n: (i, 0)), pl.BlockSpec((K, tn), lambda i, n: (0, n)))],
                   NN, grid, [_sds((M, N), out_dtype)], [pl.BlockSpec((tm, tn), lambda i, n: (i, n))],
                   extra=extra, extra_specs=especs, epilogue=epi)[0]


def mm_nn_colblk(name, a, b, out_dtype=F32):
    M, K = a.shape
    J, _, nb = b.shape
    tm = _row_tile(M)
    grid = (M // tm, J)
    return _matmul(name, [(a, b)], [(pl.BlockSpec((tm, K), lambda i, j: (i, 0)), pl.BlockSpec((None, K, nb), lambda i, j: (j, 0, 0)))],
                   NN, grid, [_sds((M, J * nb), out_dtype)], [pl.BlockSpec((tm, nb), lambda i, j: (i, j))])[0]


def mm_nt(name, a, w, tr, out_dtype=F32):
    M, C = a.shape
    R = w.shape[0]
    tm = _row_tile(M)
    grid = (M // tm, R // tr)
    return _matmul(name, [(a, w)], [(pl.BlockSpec((tm, C), lambda i, r: (i, 0)), pl.BlockSpec((tr, C), lambda i, r: (r, 0)))],
                   NT, grid, [_sds((M, R), out_dtype)], [pl.BlockSpec((tm, tr), lambda i, r: (i, r))])[0]


def mm_nt_red(name, a, w, tr, tk, out_dtype=F32):
    M, C = a.shape
    R = w.shape[0]
    tm = _row_tile(M)
    nk = C // tk
    grid = (M // tm, R // tr, nk)
    return _matmul(name, [(a, w)], [(pl.BlockSpec((tm, tk), lambda i, r, k: (i, k)), pl.BlockSpec((tr, tk), lambda i, r, k: (r, k)))],
                   NT, grid, [_sds((M, R), out_dtype)], [pl.BlockSpec((tm, tr), lambda i, r, k: (i, r))],
                   nred=nk, acc_shape=(tm, tr))[0]


def mm_tn(name, x, dy, tr, tc, out_dtype=BF16):
    M, R = x.shape
    C = dy.shape[1]
    grid = (R // tr, C // tc)
    return _matmul(name, [(x, dy)], [(pl.BlockSpec((M, tr), lambda r, c: (0, r)), pl.BlockSpec((M, tc), lambda r, c: (0, c)))],
                   TN, grid, [_sds((R, C), out_dtype)], [pl.BlockSpec((tr, tc), lambda r, c: (r, c))])[0]


def mm_tn_colblk(name, x, dy, nb, out_dtype=BF16):
    M, R = x.shape
    J = dy.shape[1] // nb
    grid = (J,)
    return _matmul(name, [(x, dy)], [(pl.BlockSpec((M, R), lambda j: (0, 0)), pl.BlockSpec((M, nb), lambda j: (0, j)))],
                   TN, grid, [_sds((J, R, nb), out_dtype)], [pl.BlockSpec((None, R, nb), lambda j: (j, 0, 0))])[0]


def _rows(T):
    return min(256, T)


def rms_fwd(name, x, g, eps=NORM_EPS):
    T, D = x.shape
    tm = _rows(T)

    def body(x_ref, g_ref, o_ref):
        xv = x_ref[...]
        r = lax.rsqrt(jnp.mean(xv * xv, axis=-1, keepdims=True) + eps)
        o_ref[...] = (xv * r * g_ref[...]).astype(BF16)

    return _pcall(body, name=name, grid=(T // tm,),
                  in_specs=[pl.BlockSpec((tm, D), lambda i: (i, 0)), pl.BlockSpec((1, D), lambda i: (0, 0))],
                  out_specs=pl.BlockSpec((tm, D), lambda i: (i, 0)), out_shape=_sds((T, D), BF16),
                  compiler_params=_cparams(("arbitrary",)))(x, g)


def rms_bwd(name, x, g, dy, dres, eps=NORM_EPS):
    T, D = x.shape
    tm = _rows(T)

    def body(x_ref, g_ref, dy_ref, dr_ref, dx_ref, dxb_ref, dg_ref):
        i = pl.program_id(0)
        xv = x_ref[...]
        r = lax.rsqrt(jnp.mean(xv * xv, axis=-1, keepdims=True) + eps)
        xh = xv * r
        dyv = dy_ref[...]
        gd = dyv * g_ref[...]
        dx = r * (gd - xh * jnp.mean(gd * xh, axis=-1, keepdims=True)) + dr_ref[...]
        dx_ref[...] = dx
        dxb_ref[...] = dx.astype(BF16)

        @pl.when(i == 0)
        def _():
            dg_ref[...] = jnp.zeros_like(dg_ref)

        dg_ref[...] += jnp.sum(dyv * xh, axis=0, keepdims=True)

    row = pl.BlockSpec((tm, D), lambda i: (i, 0))
    vec = pl.BlockSpec((1, D), lambda i: (0, 0))
    return _pcall(body, name=name, grid=(T // tm,), in_specs=[row, vec, row, row], out_specs=[row, row, vec],
                  out_shape=[_sds((T, D), F32), _sds((T, D), BF16), _sds((1, D), F32)],
                  compiler_params=_cparams(("arbitrary",)))(x, g, dy, dres)


def gnorm_fwd(y, projp, g):
    T, D = y.shape
    tm = _rows(T)

    def body(y_ref, z_ref, g_ref, o_ref):
        yz = y_ref[...] * _silu(z_ref[...])
        r = lax.rsqrt(jnp.mean(yz * yz, axis=-1, keepdims=True) + SSM_NORM_EPS)
        o_ref[...] = (yz * r * g_ref[...]).astype(BF16)

    row = pl.BlockSpec((tm, D), lambda i: (i, 0))
    return _pcall(body, name="gnorm_fwd", grid=(T // tm,),
                  in_specs=[row, pl.BlockSpec((tm, D), lambda i: (i, OFF_Z // D)), pl.BlockSpec((1, D), lambda i: (0, 0))],
                  out_specs=row, out_shape=_sds((T, D), BF16), compiler_params=_cparams(("arbitrary",)))(y, projp, g)


def gnorm_bwd(y, projp, g, dyn):
    T, D = y.shape
    tm = _rows(T)

    def body(y_ref, z_ref, g_ref, dyn_ref, dy_ref, dz_ref, dg_ref):
        i = pl.program_id(0)
        yv, zv = y_ref[...], z_ref[...]
        sz = _silu(zv)
        yz = yv * sz
        r = lax.rsqrt(jnp.mean(yz * yz, axis=-1, keepdims=True) + SSM_NORM_EPS)
        xh = yz * r
        dv = dyn_ref[...]
        gd = dv * g_ref[...]
        dyz = r * (gd - xh * jnp.mean(gd * xh, axis=-1, keepdims=True))
        dy_ref[...] = dyz * sz
        dz_ref[...] = (dyz * yv * _dsilu(zv)).astype(BF16)

        @pl.when(i == 0)
        def _():
            dg_ref[...] = jnp.zeros_like(dg_ref)

        dg_ref[...] += jnp.sum(dv * xh, axis=0, keepdims=True)

    row = pl.BlockSpec((tm, D), lambda i: (i, 0))
    vec = pl.BlockSpec((1, D), lambda i: (0, 0))
    return _pcall(body, name="gnorm_bwd", grid=(T // tm,),
                  in_specs=[row, pl.BlockSpec((tm, D), lambda i: (i, OFF_Z // D)), vec, row], out_specs=[row, row, vec],
                  out_shape=[_sds((T, D), F32), _sds((T, D), BF16), _sds((1, D), F32)],
                  compiler_params=_cparams(("arbitrary",)))(y, projp, g, dyn)


def merge_fwd(projp, out_a, out_s):
    T, D = out_a.shape
    tm = _rows(T)

    def body(ga_ref, gs_ref, a_ref, s_ref, o_ref):
        o_ref[...] = (_sigmoid(ga_ref[...]) * a_ref[...] + _sigmoid(gs_ref[...]) * s_ref[...]).astype(BF16)

    row = pl.BlockSpec((tm, D), lambda i: (i, 0))
    return _pcall(body, name="merge_fwd", grid=(T // tm,),
                  in_specs=[pl.BlockSpec((tm, D), lambda i: (i, OFF_GA // D)), pl.BlockSpec((tm, D), lambda i: (i, OFF_GS // D)), row, row],
                  out_specs=row, out_shape=_sds((T, D), BF16), compiler_params=_cparams(("arbitrary",)))(projp, projp, out_a, out_s)


def merge_bwd(projp, out_a, out_s, dmerged):
    T, D = out_a.shape
    tm = _rows(T)

    def body(ga_ref, gs_ref, a_ref, s_ref, dm_ref, da_ref, ds_ref, dga_ref, dgs_ref):
        dm = dm_ref[...]
        sa, ss = _sigmoid(ga_ref[...]), _sigmoid(gs_ref[...])
        da_ref[...] = (dm * sa).astype(BF16)
        ds_ref[...] = (dm * ss).astype(BF16)
        dga_ref[...] = (dm * a_ref[...] * sa * (1.0 - sa)).astype(BF16)
        dgs_ref[...] = (dm * s_ref[...] * ss * (1.0 - ss)).astype(BF16)

    row = pl.BlockSpec((tm, D), lambda i: (i, 0))
    return _pcall(body, name="merge_bwd", grid=(T // tm,),
                  in_specs=[pl.BlockSpec((tm, D), lambda i: (i, OFF_GA // D)), pl.BlockSpec((tm, D), lambda i: (i, OFF_GS // D)), row, row, row],
                  out_specs=[row] * 4, out_shape=[_sds((T, D), BF16)] * 4,
                  compiler_params=_cparams(("arbitrary",)))(projp, projp, out_a, out_s, dmerged)


def head_fwd_bwd(h2, pg, pp, g_final, target):
    T, D = h2.shape
    tm = _rows(T)

    def body(h_ref, pg_ref, pp_ref, g_ref, t_ref, loss_ref, dh_ref, dpg_ref, dpp_ref, dg_ref):
        i = pl.program_id(0)
        s = _sigmoid(pg_ref[...])
        ppv = pp_ref[...]
        h3 = h_ref[...] + s * ppv
        r = lax.rsqrt(jnp.mean(h3 * h3, axis=-1, keepdims=True) + NORM_EPS)
        xh = h3 * r
        gv = g_ref[...]
        e = xh * gv - t_ref[...]
        dyo = e * (1.0 / D)
        gd = dyo * gv
        dh = r * (gd - xh * jnp.mean(gd * xh, axis=-1, keepdims=True))
        dh_ref[...] = dh
        dpg_ref[...] = (dh * ppv * s * (1.0 - s)).astype(BF16)
        dpp_ref[...] = (dh * s).astype(BF16)

        @pl.when(i == 0)
        def _():
            dg_ref[...] = jnp.zeros_like(dg_ref)
            loss_ref[...] = jnp.zeros_like(loss_ref)

        dg_ref[...] += jnp.sum(dyo * xh, axis=0, keepdims=True)
        part = 0.5 * jnp.sum(jnp.mean(e * e, axis=-1, keepdims=True), axis=0, keepdims=True)
        loss_ref[...] += jnp.broadcast_to(part, loss_ref.shape)

    row = pl.BlockSpec((tm, D), lambda i: (i, 0))
    vec = pl.BlockSpec((1, D), lambda i: (0, 0))
    return _pcall(body, name="head_fwd_bwd", grid=(T // tm,), in_specs=[row, row, row, vec, row],
                  out_specs=[pl.BlockSpec((1, 128), lambda i: (0, 0)), row, row, row, vec],
                  out_shape=[_sds((1, 128), F32), _sds((T, D), F32), _sds((T, D), BF16), _sds((T, D), BF16), _sds((1, D), F32)],
                  compiler_params=_cparams(("arbitrary",)))(h2, pg, pp, g_final, target)


def ffn_up(f, wg, wu):
    T, D = f.shape
    J, _, nb = wg.shape
    tm = _row_tile(T)

    def body(f_ref, wg_ref, wu_ref, g_ref, u_ref, a_ref):
        fv = f_ref[...]
        g = _dot(fv, wg_ref[...], NN)
        u = _dot(fv, wu_ref[...], NN)
        g_ref[...] = g
        u_ref[...] = u
        a_ref[...] = (_silu(g) * u).astype(BF16)

    wspec = pl.BlockSpec((None, D, nb), lambda i, j: (j, 0, 0))
    ospec = pl.BlockSpec((None, tm, nb), lambda i, j: (j, i, 0))
    return _pcall(body, name="ffn_up", grid=(T // tm, J), in_specs=[pl.BlockSpec((tm, D), lambda i, j: (i, 0)), wspec, wspec],
                  out_specs=[ospec] * 3, out_shape=[_sds((J, T, nb), F32), _sds((J, T, nb), F32), _sds((J, T, nb), BF16)],
                  compiler_params=_cparams(("arbitrary", "arbitrary")))(f, wg, wu)


def ffn_down(act, wd, h1):
    J, T, nb = act.shape
    D = wd.shape[2]
    tm = _row_tile(T)
    tn = 1024
    grid = (T // tm, D // tn, J)
    return _matmul("ffn_down", [(act, wd)],
                   [(pl.BlockSpec((None, tm, nb), lambda i, n, j: (j, i, 0)), pl.BlockSpec((None, nb, tn), lambda i, n, j: (j, 0, n)))],
                   NN, grid, [_sds((T, D), F32)], [pl.BlockSpec((tm, tn), lambda i, n, j: (i, n))], nred=J, acc_shape=(tm, tn),
                   extra=(h1,), extra_specs=(pl.BlockSpec((tm, tn), lambda i, n, j: (i, n)),), epilogue=lambda v, r: (v + r,))[0]


def ffn_down_bwd(dh2b, wd, gate, up):
    T, D = dh2b.shape
    J, nb, _ = wd.shape
    tm = _row_tile(T)
    ospec = pl.BlockSpec((None, tm, nb), lambda i, j: (j, i, 0))

    def epi(da, g, u):
        return (da * u * _dsilu(g), da * _silu(g))

    return _matmul("ffn_down_bwd", [(dh2b, wd)],
                   [(pl.BlockSpec((tm, D), lambda i, j: (i, 0)), pl.BlockSpec((None, nb, D), lambda i, j: (j, 0, 0)))],
                   NT, (T // tm, J), [_sds((J, T, nb), BF16)] * 2, [ospec, ospec],
                   extra=(gate, up), extra_specs=(ospec, ospec), epilogue=epi)


def ffn_up_bwd(dgate, dup, wg, wu):
    J, T, nb = dgate.shape
    D = wg.shape[1]
    tm = _row_tile(T)
    tr = 1024
    aspec = pl.BlockSpec((None, tm, nb), lambda i, r, j: (j, i, 0))
    wspec = pl.BlockSpec((None, tr, nb), lambda i, r, j: (j, r, 0))
    return _matmul("ffn_up_bwd", [(dgate, wg), (dup, wu)], [(aspec, wspec), (aspec, wspec)], NT, (T // tm, D // tr, J),
                   [_sds((T, D), F32)], [pl.BlockSpec((tm, tr), lambda i, r, j: (i, r))], nred=J, acc_shape=(tm, tr))[0]


def wgrad_rowblk_lhs(name, xb, dy, tc):
    J, T, nb = xb.shape
    C = dy.shape[1]
    return _matmul(name, [(xb, dy)],
                   [(pl.BlockSpec((None, T, nb), lambda j, c: (j, 0, 0)), pl.BlockSpec((T, tc), lambda j, c: (0, c)))],
                   TN, (J, C // tc), [_sds((J, nb, C), BF16)], [pl.BlockSpec((None, nb, tc), lambda j, c: (j, 0, c))])[0]


def wgrad_colblk_rhs(name, x, dyb, tr):
    T, R = x.shape
    J, _, nb = dyb.shape
    return _matmul(name, [(x, dyb)],
                   [(pl.BlockSpec((T, tr), lambda j, r: (0, r)), pl.BlockSpec((None, T, nb), lambda j, r: (j, 0, 0)))],
                   TN, (J, R // tr), [_sds((J, R, nb), BF16)], [pl.BlockSpec((None, tr, nb), lambda j, r: (j, r, 0))])[0]


def attn_br_bwd(dout_a, wab):
    T, D = dout_a.shape
    J, R, nb = wab.shape
    tm = _row_tile(T)
    return _matmul("attn_br_bwd", [(dout_a, wab)],
                   [(pl.BlockSpec((tm, nb), lambda i, j: (i, j)), pl.BlockSpec((None, R, nb), lambda i, j: (j, 0, 0)))],
                   NT, (T // tm, J), [_sds((T, R), BF16)], [pl.BlockSpec((tm, R), lambda i, j: (i, 0))], nred=J, acc_shape=(tm, R))[0]


def _adam_math(w, g, m, v):
    m2 = ADAM_B1 * m + (1.0 - ADAM_B1) * g
    v2 = ADAM_B2 * v + (1.0 - ADAM_B2) * (g * g)
    m_hat = m2 / (1.0 - ADAM_B1 ** ADAM_STEP)
    v_hat = v2 / (1.0 - ADAM_B2 ** ADAM_STEP)
    delta = -ADAM_LR * (m_hat / (jnp.sqrt(v_hat) + ADAM_EPS) + ADAM_WD * w)
    return delta, m2, v2


def adamw(name, parts, w, m, v):
    R, C = w.shape
    tr = R
    for cand in (256, 176, 128, 64, 32, 16, 8):
        if R % cand == 0 and R > cand:
            tr = cand
            break

    def body(p_ref, w_ref, m_ref, v_ref, g_ref, d_ref, m2_ref, v2_ref):
        g = p_ref[0].astype(F32)
        for s in range(1, N_DEV):
            g = g + p_ref[s].astype(F32)
        d, m2, v2 = _adam_math(w_ref[...], g, m_ref[...], v_ref[...])
        g_ref[...] = g
        d_ref[...] = d
        m2_ref[...] = m2
        v2_ref[...] = v2

    blk = pl.BlockSpec((tr, C), lambda i: (i, 0))
    return _pcall(body, name=name, grid=(R // tr,), in_specs=[pl.BlockSpec((N_DEV, tr, C), lambda i: (0, i, 0)), blk, blk, blk],
                  out_specs=[blk] * 4, out_shape=[_sds((R, C), F32)] * 4, compiler_params=_cparams(("arbitrary",)))(parts, w, m, v)


def exchange(name, arrays, mode):
    n = len(arrays)
    out_shapes = []
    for a in arrays:
        shp = a.shape if mode == "scatter" else (N_DEV,) + a.shape
        out_shapes.append(_sds(shp, a.dtype))

    def body(*refs):
        ins, outs = refs[:n], refs[n:2 * n]
        send_sems, recv_sems, local_sems = refs[2 * n:]
        x, y, c = lax.axis_index("x"), lax.axis_index("y"), lax.axis_index("c")
        me = 4 * x + 2 * y + c

        def src(a, dest):
            return ins[a].at[dest] if mode == "scatter" else ins[a]

        local = [pltpu.make_async_copy(src(a, me), outs[a].at[me], local_sems.at[a]) for a in range(n)]
        for cp in local:
            cp.start()
        remote = []
        for k in range(1, N_DEV):
            px = 1 - x if k & 4 else x
            py = 1 - y if k & 2 else y
            pc = 1 - c if k & 1 else c
            peer = 4 * px + 2 * py + pc
            for a in range(n):
                cp = pltpu.make_async_remote_copy(src_ref=src(a, peer), dst_ref=outs[a].at[me],
                                                  send_sem=send_sems.at[a * 7 + k - 1], recv_sem=recv_sems.at[a * 7 + k - 1],
                                                  device_id=(px, py, pc), device_id_type=pl.DeviceIdType.MESH)
                cp.start()
                arrival = pltpu.make_async_remote_copy(src_ref=src(a, peer), dst_ref=outs[a].at[peer],
                                                       send_sem=send_sems.at[a * 7 + k - 1], recv_sem=recv_sems.at[a * 7 + k - 1],
                                                       device_id=(px, py, pc), device_id_type=pl.DeviceIdType.MESH)
                remote.append((cp, arrival))
        for cp, arrival in remote:
            cp.wait_send()
            arrival.wait_recv()
        for cp in local:
            cp.wait()

    any_spec = pl.BlockSpec(memory_space=pl.ANY)
    return _pcall(body, name=name, in_specs=[any_spec] * n, out_specs=[any_spec] * n, out_shape=out_shapes,
                  scratch_shapes=[pltpu.SemaphoreType.DMA((7 * n,)), pltpu.SemaphoreType.DMA((7 * n,)), pltpu.SemaphoreType.DMA((n,))],
                  compiler_params=pltpu.CompilerParams(has_side_effects=True))(*arrays)


def _rope_parts(pos_ref, inv_ref):
    ang = pos_ref[...] * inv_ref[...]
    return jnp.cos(ang), jnp.sin(ang)


def _rot_half(t):
    lane = lax.broadcasted_iota(jnp.int32, t.shape, 1)
    return jnp.where((lane % HEAD_DIM) < HEAD_DIM // 2, -pltpu.roll(t, 128 - HEAD_DIM // 2, 1), pltpu.roll(t, HEAD_DIM // 2, 1))


def _attn_mask(n):
    row = lax.broadcasted_iota(jnp.int32, (BLK, 2 * BLK), 0)
    col = lax.broadcasted_iota(jnp.int32, (BLK, 2 * BLK), 1)
    dist = row + BLK - col
    return (dist >= 0) & (dist < BLK) & ((n * BLK - BLK + col) >= 0)


def _attn_specs(T):
    prev = lambda n: jnp.maximum(n - 1, 0)
    kc = pl.BlockSpec((BLK, KV_DIM), lambda n: (n, OFF_K // KV_DIM))
    kp = pl.BlockSpec((BLK, KV_DIM), lambda n: (prev(n), OFF_K // KV_DIM))
    vc = pl.BlockSpec((BLK, KV_DIM), lambda n: (n, OFF_V // KV_DIM))
    vp = pl.BlockSpec((BLK, KV_DIM), lambda n: (prev(n), OFF_V // KV_DIM))
    pc = pl.BlockSpec((BLK, 1), lambda n: (n, 0))
    pp = pl.BlockSpec((BLK, 1), lambda n: (prev(n), 0))
    inv = pl.BlockSpec((1, 128), lambda n: (0, 0))
    sink = pl.BlockSpec(memory_space=pltpu.SMEM)
    return kc, kp, vc, vp, pc, pp, inv, sink


def _softmax_sink(sc, valid, sink):
    sc = jnp.where(valid, sc * (HEAD_DIM ** -0.5), -1e30)
    m = jnp.maximum(jnp.max(sc, axis=1, keepdims=True), sink)
    e = jnp.exp(sc - m)
    es = jnp.exp(sink - m)
    den = jnp.sum(e, axis=1, keepdims=True) + es
    return e / den, es / den


def attn_fwd(projp, posf, inv128, sinks):
    T = projp.shape[0]
    kc, kp, vc, vp, pc, pp, inv, sink = _attn_specs(T)

    def body(q_ref, kc_ref, kp_ref, vc_ref, vp_ref, pc_ref, pp_ref, inv_ref, sink_ref, o_ref, qr_ref, kr_ref):
        n = pl.program_id(0)
        cos_c, sin_c = _rope_parts(pc_ref, inv_ref)
        cos_p, sin_p = _rope_parts(pp_ref, inv_ref)
        valid = _attn_mask(n)
        k_c, k_p = [], []
        for s in range(KV_DIM // 128):
            t = kc_ref[:, 128 * s:128 * (s + 1)]
            k_c.append((t * cos_c + _rot_half(t) * sin_c).astype(BF16))
            kr_ref[:, 128 * s:128 * (s + 1)] = k_c[s]
            t = kp_ref[:, 128 * s:128 * (s + 1)]
            k_p.append((t * cos_p + _rot_half(t) * sin_p).astype(BF16))
        for s in range(Q_DIM // 128):
            t = q_ref[:, 128 * s:128 * (s + 1)]
            qs = (t * cos_c + _rot_half(t) * sin_c).astype(BF16)
            qr_ref[:, 128 * s:128 * (s + 1)] = qs
            for e in range(2):
                hq = 2 * s + e
                hk = hq // (ATTN_HEADS // KV_HEADS)
                lo = HEAD_DIM * (hk % 2)
                kcat = jnp.concatenate([k_p[hk // 2][:, lo:lo + HEAD_DIM], k_c[hk // 2][:, lo:lo + HEAD_DIM]], axis=0)
                vcat = jnp.concatenate([vp_ref[:, HEAD_DIM * hk:HEAD_DIM * (hk + 1)], vc_ref[:, HEAD_DIM * hk:HEAD_DIM * (hk + 1)]], axis=0)
                sc = _dot(qs[:, HEAD_DIM * e:HEAD_DIM * (e + 1)], kcat, NT)
                p, _ = _softmax_sink(sc, valid, sink_ref[0, hq])
                o_ref[:, HEAD_DIM * hq:HEAD_DIM * (hq + 1)] = _dot(p, vcat, NN).astype(BF16)

    qspec = pl.BlockSpec((BLK, Q_DIM), lambda n: (n, OFF_Q // Q_DIM))
    orow = pl.BlockSpec((BLK, Q_DIM), lambda n: (n, 0))
    krow = pl.BlockSpec((BLK, KV_DIM), lambda n: (n, 0))
    return _pcall(body, name="attn_fwd", grid=(T // BLK,), in_specs=[qspec, kc, kp, vc, vp, pc, pp, inv, sink],
                  out_specs=[orow, orow, krow], out_shape=[_sds((T, Q_DIM), BF16), _sds((T, Q_DIM), BF16), _sds((T, KV_DIM), BF16)],
                  compiler_params=_cparams(("arbitrary",)))(projp, projp, projp, projp, projp, posf, posf, inv128, sinks)


def attn_bwd(qr, kr, projp, dattn, posf, inv128, sinks):
    T = projp.shape[0]
    _, _, vc, vp, pc, pp, inv, sink = _attn_specs(T)
    G = ATTN_HEADS // KV_HEADS

    def body(qr_ref, krc_ref, krp_ref, vc_ref, vp_ref, do_ref, pc_ref, pp_ref, inv_ref, sink_ref, dq_ref, dk_ref, dv_ref, dsk_ref):
        n = pl.program_id(0)

        @pl.when(n == 0)
        def _():
            dk_ref[...] = jnp.zeros_like(dk_ref)
            dv_ref[...] = jnp.zeros_like(dv_ref)
            dsk_ref[...] = jnp.zeros_like(dsk_ref)

        cos_c, sin_c = _rope_parts(pc_ref, inv_ref)
        cos_p, sin_p = _rope_parts(pp_ref, inv_ref)
        valid = _attn_mask(n)
        lane = lax.broadcasted_iota(jnp.int32, (1, 128), 1)
        dsk = jnp.zeros((1, 128), F32)
        dq_heads, dk_heads, dv_heads = [], [], []
        for hk in range(KV_HEADS):
            ksl = slice(HEAD_DIM * hk, HEAD_DIM * (hk + 1))
            kcat = jnp.concatenate([krp_ref[:, ksl], krc_ref[:, ksl]], axis=0)
            vcat = jnp.concatenate([vp_ref[:, ksl], vc_ref[:, ksl]], axis=0)
            dkcat = jnp.zeros((2 * BLK, HEAD_DIM), F32)
            dvcat = jnp.zeros((2 * BLK, HEAD_DIM), F32)
            for g in range(G):
                hq = G * hk + g
                qsl = slice(HEAD_DIM * hq, HEAD_DIM * (hq + 1))
                q_h = qr_ref[:, qsl]
                do_h = do_ref[:, qsl]
                p, psink = _softmax_sink(_dot(q_h, kcat, NT), valid, sink_ref[0, hq])
                dp = _dot(do_h, vcat, NT)
                delta = jnp.sum(p * dp, axis=1, keepdims=True)
                ds = p * (dp - delta) * (HEAD_DIM ** -0.5)
                dsk = dsk + jnp.where(lane == hq, -jnp.sum(psink * delta, axis=0, keepdims=True), 0.0)
                dq_heads.append(_dot(ds, kcat, NN))
                dkcat = dkcat + _dot(ds, q_h, TN)
                dvcat = dvcat + _dot(p, do_h, TN)
            dk_heads.append(dkcat)
            dv_heads.append(dvcat)
        dsk_ref[...] += dsk
        for s in range(Q_DIM // 128):
            t = jnp.concatenate([dq_heads[2 * s], dq_heads[2 * s + 1]], axis=1)
            dq_ref[:, 128 * s:128 * (s + 1)] = (t * cos_c - _rot_half(t) * sin_c).astype(BF16)
        cur = pl.ds(pl.multiple_of(n * BLK, BLK), BLK)
        prv = pl.ds(pl.multiple_of(jnp.maximum(n - 1, 0) * BLK, BLK), BLK)
        for s in range(KV_DIM // 128):
            tc = jnp.concatenate([dk_heads[2 * s][BLK:], dk_heads[2 * s + 1][BLK:]], axis=1)
            tp = jnp.concatenate([dk_heads[2 * s][:BLK], dk_heads[2 * s + 1][:BLK]], axis=1)
            cols = slice(128 * s, 128 * (s + 1))
            dk_ref[cur, cols] += tc * cos_c - _rot_half(tc) * sin_c
            dk_ref[prv, cols] += tp * cos_p - _rot_half(tp) * sin_p
            dv_ref[cur, cols] += jnp.concatenate([dv_heads[2 * s][BLK:], dv_heads[2 * s + 1][BLK:]], axis=1)
            dv_ref[prv, cols] += jnp.concatenate([dv_heads[2 * s][:BLK], dv_heads[2 * s + 1][:BLK]], axis=1)

    qrow = pl.BlockSpec((BLK, Q_DIM), lambda n: (n, 0))
    krc = pl.BlockSpec((BLK, KV_DIM), lambda n: (n, 0))
    krp = pl.BlockSpec((BLK, KV_DIM), lambda n: (jnp.maximum(n - 1, 0), 0))
    whole = pl.BlockSpec((T, KV_DIM), lambda n: (0, 0))
    return _pcall(body, name="attn_bwd", grid=(T // BLK,), in_specs=[qrow, krc, krp, vc, vp, qrow, pc, pp, inv, sink],
                  out_specs=[qrow, whole, whole, pl.BlockSpec((1, 128), lambda n: (0, 0))],
                  out_shape=[_sds((T, Q_DIM), BF16), _sds((T, KV_DIM), F32), _sds((T, KV_DIM), F32), _sds((1, 128), F32)],
                  compiler_params=_cparams(("arbitrary",)))(qr, kr, kr, projp, projp, dattn, posf, posf, inv128, sinks)


CONV_CB = 256


def _shift_down(x, s):
    row = lax.broadcasted_iota(jnp.int32, x.shape, 0)
    return jnp.where(row >= s, pltpu.roll(x, s, 0), 0.0)


def _shift_up(x, s):
    T = x.shape[0]
    row = lax.broadcasted_iota(jnp.int32, x.shape, 0)
    return jnp.where(row < T - s, pltpu.roll(x, T - s, 0), 0.0)


def _conv_pre(x, w_ref, b_ref):
    acc = x * w_ref[CONV_WIDTH - 1:CONV_WIDTH, :] + b_ref[...]
    for s in range(1, CONV_WIDTH):
        acc = acc + _shift_down(x, s) * w_ref[CONV_WIDTH - 1 - s:CONV_WIDTH - s, :]
    return acc


def conv_fwd(projp, conv_w, conv_b):
    T = projp.shape[0]

    def body(x_ref, w_ref, b_ref, o_ref):
        o_ref[...] = _silu(_conv_pre(x_ref[...], w_ref, b_ref))

    return _pcall(body, name="conv_fwd", grid=(CONV_DIM // CONV_CB,),
                  in_specs=[pl.BlockSpec((T, CONV_CB), lambda c: (0, OFF_XBC // CONV_CB + c)),
                            pl.BlockSpec((CONV_WIDTH, CONV_CB), lambda c: (0, c)), pl.BlockSpec((1, CONV_CB), lambda c: (0, c))],
                  out_specs=pl.BlockSpec((T, CONV_CB), lambda c: (0, c)), out_shape=_sds((T, CONV_DIM), F32),
                  compiler_params=_cparams(("arbitrary",)))(projp, conv_w, conv_b)


def conv_bwd(name, projp, dact, conv_w, conv_b, col0):
    T, C = dact.shape
    c0 = col0 // CONV_CB

    def body(x_ref, da_ref, w_ref, b_ref, dx_ref, dw_ref, db_ref):
        x = x_ref[...]
        dpre = da_ref[...] * _dsilu(_conv_pre(x, w_ref, b_ref))
        dx = dpre * w_ref[CONV_WIDTH - 1:CONV_WIDTH, :]
        dw_ref[CONV_WIDTH - 1:CONV_WIDTH, :] = jnp.sum(dpre * x, axis=0, keepdims=True)
        for s in range(1, CONV_WIDTH):
            i = CONV_WIDTH - 1 - s
            dx = dx + _shift_up(dpre, s) * w_ref[i:i + 1, :]
            dw_ref[i:i + 1, :] = jnp.sum(dpre * _shift_down(x, s), axis=0, keepdims=True)
        dx_ref[...] = dx.astype(BF16)
        db_ref[...] = jnp.sum(dpre, axis=0, keepdims=True)

    return _pcall(body, name=name, grid=(C // CONV_CB,),
                  in_specs=[pl.BlockSpec((T, CONV_CB), lambda c: (0, OFF_XBC // CONV_CB + c0 + c)),
                            pl.BlockSpec((T, CONV_CB), lambda c: (0, c)),
                            pl.BlockSpec((CONV_WIDTH, CONV_CB), lambda c: (0, c0 + c)), pl.BlockSpec((1, CONV_CB), lambda c: (0, c0 + c))],
                  out_specs=[pl.BlockSpec((T, CONV_CB), lambda c: (0, c)), pl.BlockSpec((CONV_WIDTH, CONV_CB), lambda c: (0, c)),
                             pl.BlockSpec((1, CONV_CB), lambda c: (0, c))],
                  out_shape=[_sds((T, C), BF16), _sds((CONV_WIDTH, C), F32), _sds((1, C), F32)],
                  compiler_params=_cparams(("arbitrary",)))(projp, dact, conv_w, conv_b)


def _softplus(x):
    return jnp.maximum(x, 0.0) + jnp.log1p(jnp.exp(-jnp.abs(x)))


def _tri(lower):
    r = lax.broadcasted_iota(jnp.int32, (BLK, BLK), 0)
    c = lax.broadcasted_iota(jnp.int32, (BLK, BLK), 1)
    return (r >= c) if lower else (c >= r)


def _ssd_chunk_setup(dt_ref, dtb_ref, alog_ref):
    raw = dt_ref[...] + dtb_ref[...]
    dt = _softplus(raw)
    aneg = -jnp.exp(alog_ref[...])
    a = dt * aneg
    acs = jnp.dot(_tri(True).astype(F32), a, precision=lax.Precision.HIGHEST, preferred_element_type=F32)
    return raw, dt, aneg, acs, acs.T


def _ssd_specs(T, rev):
    nc = T // BLK
    ci = (lambda c: nc - 1 - c) if rev else (lambda c: c)
    xs = pl.BlockSpec((BLK, D_INNER), lambda c: (ci(c), 0))
    bm = pl.BlockSpec((BLK, SSM_GROUPS * D_STATE), lambda c: (ci(c), D_INNER // (SSM_GROUPS * D_STATE)))
    cm = pl.BlockSpec((BLK, SSM_GROUPS * D_STATE), lambda c: (ci(c), D_INNER // (SSM_GROUPS * D_STATE) + 1))
    dt = pl.BlockSpec((BLK, DT_PAD), lambda c: (ci(c), OFF_DT // DT_PAD))
    v128 = pl.BlockSpec((1, 128), lambda c: (0, 0))
    dfull = pl.BlockSpec((1, D_INNER), lambda c: (0, 0))
    st = pl.BlockSpec((None, SSM_HEADS, HEAD_DIM, D_STATE), lambda c: (ci(c), 0, 0, 0))
    return xs, bm, cm, dt, v128, dfull, st, ci


def ssd_fwd(xbc, projp, dtb, alog, dfull):
    T = xbc.shape[0]
    nc = T // BLK
    xs, bm, cm, dts, v128, dfs, st, _ = _ssd_specs(T, False)

    def body(xs_ref, b_ref, c_ref, dt_ref, dtb_ref, alog_ref, d_ref, y_ref, st_ref, h_scr):
        c = pl.program_id(0)

        @pl.when(c == 0)
        def _():
            h_scr[...] = jnp.zeros_like(h_scr)

        _, dt, _, acs, acsT = _ssd_chunk_setup(dt_ref, dtb_ref, alog_ref)
        tril = _tri(True)
        for g in range(SSM_GROUPS):
            B = b_ref[:, D_STATE * g:D_STATE * (g + 1)].astype(BF16)
            C = c_ref[:, D_STATE * g:D_STATE * (g + 1)].astype(BF16)
            cb = _dot(C, B, NT)
            for hh in range(HEADS_PER_GROUP):
                h = HEADS_PER_GROUP * g + hh
                hs = slice(HEAD_DIM * h, HEAD_DIM * (h + 1))
                dt_h, acs_h, acsT_h = dt[:, h:h + 1], acs[:, h:h + 1], acsT[h:h + 1, :]
                alast = acs[BLK - 1:BLK, h:h + 1]
                x_h = xs_ref[:, hs]
                xd = x_h * dt_h
                decay = jnp.where(tril, jnp.exp(jnp.where(tril, acs_h - acsT_h, 0.0)), 0.0)
                y = _dot(cb * decay, xd, NN)
                hp = h_scr[h]
                st_ref[h] = hp
                y = y + jnp.exp(acs_h) * _dot(C, hp, NT)
                h_scr[h] = jnp.exp(alast) * hp + _dot(xd * jnp.exp(alast - acs_h), B, TN)
                y_ref[:, hs] = y + d_ref[:, hs] * x_h

    return _pcall(body, name="ssd_fwd", grid=(nc,), in_specs=[xs, bm, cm, dts, v128, v128, dfs],
                  out_specs=[xs, st], out_shape=[_sds((T, D_INNER), F32), _sds((nc, SSM_HEADS, HEAD_DIM, D_STATE), F32)],
                  scratch_shapes=[pltpu.VMEM((SSM_HEADS, HEAD_DIM, D_STATE), F32)],
                  compiler_params=_cparams(("arbitrary",)))(xbc, xbc, xbc, projp, dtb, alog, dfull)


def ssd_bwd(xbc, projp, dtb, alog, dfull, states, dy):
    T = xbc.shape[0]
    nc = T // BLK
    xs, bm, cm, dts, v128, dfs, st, ci = _ssd_specs(T, True)
    gn = SSM_GROUPS * D_STATE

    def body(xs_ref, b_ref, c_ref, dt_ref, dtb_ref, alog_ref, d_ref, st_ref, dy_ref,
             dxs_ref, dB_ref, dC_ref, ddt_ref, dal_ref, dD_ref, ddtb_ref, dh_scr):
        i = pl.program_id(0)

        @pl.when(i == 0)
        def _():
            dh_scr[...] = jnp.zeros_like(dh_scr)
            dal_ref[...] = jnp.zeros_like(dal_ref)
            dD_ref[...] = jnp.zeros_like(dD_ref)
            ddtb_ref[...] = jnp.zeros_like(ddtb_ref)

        raw, dt, aneg, acs, acsT = _ssd_chunk_setup(dt_ref, dtb_ref, alog_ref)
        tril = _tri(True)
        lane1 = lax.broadcasted_iota(jnp.int32, (1, 128), 1)
        lane = lax.broadcasted_iota(jnp.int32, (BLK, 128), 1)
        sub = lax.broadcasted_iota(jnp.int32, (BLK, 128), 0)
        ds_col = jnp.zeros((BLK, 128), F32)
        ds_row = jnp.zeros((BLK, 128), F32)
        ddt_col = jnp.zeros((BLK, 128), F32)
        ds_last = jnp.zeros((1, 128), F32)
        dD = jnp.zeros((1, 128), F32)

        def total(v):
            return jnp.sum(jnp.sum(v, axis=1, keepdims=True), axis=0, keepdims=True)

        for g in range(SSM_GROUPS):
            gs = slice(D_STATE * g, D_STATE * (g + 1))
            B = b_ref[:, gs].astype(BF16)
            C = c_ref[:, gs].astype(BF16)
            cb = _dot(C, B, NT)
            dG = jnp.zeros((BLK, BLK), F32)
            dB_g = jnp.zeros((BLK, D_STATE), F32)
            dC_g = jnp.zeros((BLK, D_STATE), F32)
            for hh in range(HEADS_PER_GROUP):
                h = HEADS_PER_GROUP * g + hh
                hs = slice(HEAD_DIM * h, HEAD_DIM * (h + 1))
                dt_h, acs_h, acsT_h = dt[:, h:h + 1], acs[:, h:h + 1], acsT[h:h + 1, :]
                alast = acs[BLK - 1:BLK, h:h + 1]
                x_h = xs_ref[:, hs]
                dy_h = dy_ref[:, hs]
                xd = x_h * dt_h
                decay = jnp.where(tril, jnp.exp(jnp.where(tril, acs_h - acsT_h, 0.0)), 0.0)
                M = cb * decay
                hc = st_ref[h]
                dS = dh_scr[h]
                w = jnp.exp(alast - acs_h)
                gamma = jnp.exp(alast)
                dD = dD + jnp.where(lane1 == h, total(dy_h * x_h), 0.0)
                dye = dy_h * jnp.exp(acs_h)
                dH_y = _dot(dye, C, TN)
                dC_g = dC_g + _dot(dye, hc, NN)
                ds_h = jnp.sum(dye * _dot(C, hc, NT), axis=1, keepdims=True)
                dM = _dot(dy_h, xd, NT)
                dxd = _dot(M, dy_h, TN)
                Q = dM * M
                ds_h = ds_h + jnp.sum(Q, axis=1, keepdims=True)
                ds_row = jnp.where(sub == h, jnp.sum(Q, axis=0, keepdims=True), ds_row)
                dG = dG + dM * decay
                dxdw = _dot(B, dS, NT)
                dxd = dxd + w * dxdw
                dww = jnp.sum(xd * dxdw, axis=1, keepdims=True) * w
                ds_h = ds_h - dww
                ds_last = ds_last + jnp.where(lane1 == h, jnp.sum(dww, axis=0, keepdims=True) + total(dS * hc) * gamma, 0.0)
                dB_g = dB_g + _dot(xd * w, dS, NN)
                dh_scr[h] = gamma * dS + dH_y
                dxs_ref[:, hs] = d_ref[:, hs] * dy_h + dxd * dt_h
                ddt_col = jnp.where(lane == h, jnp.sum(dxd * x_h, axis=1, keepdims=True), ddt_col)
                ds_col = jnp.where(lane == h, ds_h, ds_col)
            dC_ref[:, gs] = dC_g + _dot(dG, B, NN)
            dB_ref[:, gs] = dB_g + _dot(dG, C, TN)
        ds_all = ds_col - ds_row.T + jnp.where(sub == BLK - 1, ds_last, 0.0)
        da = jnp.dot(_tri(False).astype(F32), ds_all, precision=lax.Precision.HIGHEST, preferred_element_type=F32)
        ddt = ddt_col + da * aneg
        draw = jnp.where(lane < SSM_HEADS, ddt * _sigmoid(raw), 0.0)
        ddt_ref[...] = draw.astype(BF16)
        dal_ref[...] += jnp.sum(da * dt, axis=0, keepdims=True) * aneg
        ddtb_ref[...] += jnp.sum(draw, axis=0, keepdims=True)
        dD_ref[...] += dD

    gblk = pl.BlockSpec((BLK, gn), lambda c: (ci(c), 0))
    return _pcall(body, name="ssd_bwd", grid=(nc,), in_specs=[xs, bm, cm, dts, v128, v128, dfs, st, xs],
                  out_specs=[xs, gblk, gblk, pl.BlockSpec((BLK, DT_PAD), lambda c: (ci(c), 0)), v128, v128, v128],
                  out_shape=[_sds((T, D_INNER), F32), _sds((T, gn), F32), _sds((T, gn), F32), _sds((T, DT_PAD), BF16),
                             _sds((1, 128), F32), _sds((1, 128), F32), _sds((1, 128), F32)],
                  scratch_shapes=[pltpu.VMEM((SSM_HEADS, HEAD_DIM, D_STATE), F32)],
                  compiler_params=_cparams(("arbitrary",)))(xbc, xbc, xbc, projp, dtb, alog, dfull, states, dy)


_WIN_ORDER = ("z", "ga", "gs", "xbc", "q", "k", "v", "dt")


def _win_to_padded(win_g):
    full = jnp.transpose(win_g, (1, 0, 2)).reshape(D_MODEL, IN_DIM)
    cols = []
    for nm in _WIN_ORDER:
        s, w = SEG[nm]
        cols.append(full[:, s:s + w])
    cols.append(jnp.zeros((D_MODEL, DT_PAD - SEG["dt"][1]), win_g.dtype))
    return jnp.concatenate(cols, axis=1)


def _padded_to_win(dw):
    off = dict(z=OFF_Z, ga=OFF_GA, gs=OFF_GS, xbc=OFF_XBC, q=OFF_Q, k=OFF_K, v=OFF_V, dt=OFF_DT)
    cols = [dw[:, off[nm]:off[nm] + SEG[nm][1]] for nm in ("q", "k", "v", "z", "xbc", "dt", "ga", "gs")]
    full = jnp.concatenate(cols, axis=1)
    return jnp.transpose(full.reshape(D_MODEL, N_DEV, IN_DIM // N_DEV), (1, 0, 2))


def _pad128(v):
    return jnp.pad(v, ((0, 0), (0, 128 - v.shape[1])))


_SMALL = (("loss", 128, 1), ("g_mix", 2048, 2048), ("conv_b", 3072, 3072), ("dt_bias", 128, 32), ("a_log", 128, 32),
          ("d_skip", 128, 32), ("g_ssd", 2048, 2048), ("sinks", 128, 16), ("g_ffn", 2048, 2048), ("g_ple", 2048, 2048),
          ("g_final", 2048, 2048))


def _small_vec(d):
    parts = []
    for nm, pw, w in _SMALL:
        v = d[nm].reshape(1, -1).astype(F32)
        parts.append(jnp.pad(v[:, :min(v.shape[1], pw)], ((0, 0), (0, pw - min(v.shape[1], pw)))))
    return jnp.concatenate(parts, axis=1)


def _small_split(vec):
    out, o = {}, 0
    for nm, pw, w in _SMALL:
        out[nm] = vec[0, o:o + w]
        o += pw
    return out


def kernel(x, p, positions, g_mix, w_in, conv_w, conv_b, dt_bias, a_log, d_skip, g_ssd, sinks, w_attn_br, w_ssd_br, w_o, g_ffn, w_gate, w_up, w_down, g_ple, w_ple_gate, w_ple_proj, g_final, loss_target, m_g_mix, m_w_in, m_conv_w, m_conv_b, m_dt_bias, m_a_log, m_d_skip, m_g_ssd, m_sinks, m_w_attn_br, m_w_ssd_br, m_w_o, m_g_ffn, m_w_gate, m_w_up, m_w_down, m_g_ple, m_w_ple_gate, m_w_ple_proj, m_g_final, v_g_mix, v_w_in, v_conv_w, v_conv_b, v_dt_bias, v_a_log, v_d_skip, v_g_ssd, v_sinks, v_w_attn_br, v_w_ssd_br, v_w_o, v_g_ffn, v_w_gate, v_w_up, v_w_down, v_g_ple, v_w_ple_gate, v_w_ple_proj, v_g_final):
    T = x.shape[1]
    D = D_MODEL
    W = dict(g_mix=g_mix, w_in=w_in, conv_w=conv_w, conv_b=conv_b, dt_bias=dt_bias, a_log=a_log, d_skip=d_skip, g_ssd=g_ssd,
             sinks=sinks, w_attn_br=w_attn_br, w_ssd_br=w_ssd_br, w_o=w_o, g_ffn=g_ffn, w_gate=w_gate, w_up=w_up, w_down=w_down,
             g_ple=g_ple, w_ple_gate=w_ple_gate, w_ple_proj=w_ple_proj, g_final=g_final)
    Mo = dict(g_mix=m_g_mix, w_in=m_w_in, conv_w=m_conv_w, conv_b=m_conv_b, dt_bias=m_dt_bias, a_log=m_a_log, d_skip=m_d_skip,
              g_ssd=m_g_ssd, sinks=m_sinks, w_attn_br=m_w_attn_br, w_ssd_br=m_w_ssd_br, w_o=m_w_o, g_ffn=m_g_ffn, w_gate=m_w_gate,
              w_up=m_w_up, w_down=m_w_down, g_ple=m_g_ple, w_ple_gate=m_w_ple_gate, w_ple_proj=m_w_ple_proj, g_final=m_g_final)
    Vo = dict(g_mix=v_g_mix, w_in=v_w_in, conv_w=v_conv_w, conv_b=v_conv_b, dt_bias=v_dt_bias, a_log=v_a_log, d_skip=v_d_skip,
              g_ssd=v_g_ssd, sinks=v_sinks, w_attn_br=v_w_attn_br, w_ssd_br=v_w_ssd_br, w_o=v_w_o, g_ffn=v_g_ffn, w_gate=v_w_gate,
              w_up=v_w_up, w_down=v_w_down, g_ple=v_g_ple, w_ple_gate=v_w_ple_gate, w_ple_proj=v_w_ple_proj, g_final=v_g_final)
    order = ["g_mix", "w_in", "conv_w", "conv_b", "dt_bias", "a_log", "d_skip", "g_ssd", "sinks", "w_attn_br", "w_ssd_br", "w_o",
             "g_ffn", "w_gate", "w_up", "w_down", "g_ple", "w_ple_gate", "w_ple_proj", "g_final"]
    big = ["w_in", "conv_w", "w_attn_br", "w_ssd_br", "w_o", "w_gate", "w_up", "w_down", "w_ple_gate", "w_ple_proj"]

    x2 = x.reshape(T, D)
    p2 = p.reshape(T, PLE_DIM)
    tgt = loss_target.reshape(T, D)
    posf = positions.reshape(T, 1).astype(F32)
    inv = ROPE_THETA ** (-np.arange(HEAD_DIM // 2, dtype=np.float32) * 2.0 / HEAD_DIM)
    inv128 = jnp.asarray(np.tile(inv, 128 // (HEAD_DIM // 2)).reshape(1, 128).astype(np.float32))
    sh = {n: W[n].reshape(W[n].shape[-2:]) for n in big}

    send = [sh[n] if n == "conv_w" else sh[n].astype(BF16) for n in big]
    gathered = dict(zip(big, exchange("gather_weights", send, "gather")))
    winp = _win_to_padded(gathered["w_in"])
    convw = jnp.transpose(gathered["conv_w"], (1, 0, 2)).reshape(CONV_WIDTH, CONV_DIM)
    wab = gathered["w_attn_br"]
    wsb = gathered["w_ssd_br"].reshape(D, D)
    wo = gathered["w_o"].reshape(D, D)
    wg, wu, wd = gathered["w_gate"], gathered["w_up"], gathered["w_down"]
    wpg = gathered["w_ple_gate"].reshape(D, D)
    wpp = gathered["w_ple_proj"]
    dtb = _pad128(dt_bias)
    alog = _pad128(a_log)
    dfull = jnp.repeat(d_skip.reshape(SSM_HEADS), HEAD_DIM).reshape(1, D_INNER)

    u = rms_fwd("norm_mix", x2, g_mix)
    projp = mm_nn("in_proj", u, winp, 640)
    attn, qr, kr = attn_fwd(projp, posf, inv128, sinks)
    xbc = conv_fwd(projp, convw, conv_b)
    y, states = ssd_fwd(xbc, projp, dtb, alog, dfull)
    yn = gnorm_fwd(y, projp, g_ssd)
    out_a = mm_nn_colblk("attn_br", attn, wab)
    out_s = mm_nn("ssd_br", yn, wsb, 512)
    merged = merge_fwd(projp, out_a, out_s)
    h1 = mm_nn("o_proj", merged, wo, 512, residual=x2)
    f = rms_fwd("norm_ffn", h1, g_ffn)
    gate, up, act = ffn_up(f, wg, wu)
    h2 = ffn_down(act, wd, h1)
    r = rms_fwd("norm_ple", h2, g_ple)
    pg = mm_nn("ple_gate", r, wpg, 512)
    pp = mm_nn_colblk("ple_proj", p2, wpp)
    loss_v, dh3, dpg, dpp, dg_final = head_fwd_bwd(h2, pg, pp, g_final.reshape(1, D), tgt)

    gw = {}
    gw["w_ple_proj"] = mm_tn_colblk("dw_ple_proj", p2, dpp, PLE_DIM)
    dr = mm_nt("d_ple_gate", dpg, wpg, 512)
    gw["w_ple_gate"] = mm_tn("dw_ple_gate", r, dpg, 512, 1024).reshape(N_DEV, D // N_DEV, D)
    dh2, dh2b, dg_ple = rms_bwd("norm_ple_bwd", h2, g_ple, dr, dh3)
    dgate, dup = ffn_down_bwd(dh2b, wd, gate, up)
    gw["w_down"] = wgrad_rowblk_lhs("dw_down", act, dh2b, 1024)
    df = ffn_up_bwd(dgate, dup, wg, wu)
    gw["w_gate"] = wgrad_colblk_rhs("dw_gate", f, dgate, 1024)
    gw["w_up"] = wgrad_colblk_rhs("dw_up", f, dup, 1024)
    dh1, dh1b, dg_ffn = rms_bwd("norm_ffn_bwd", h1, g_ffn, df, dh2)
    dmerged = mm_nt("d_o_proj", dh1b, wo, 512)
    gw["w_o"] = mm_tn("dw_o", merged, dh1b, 512, 1024).reshape(N_DEV, D // N_DEV, D)
    dout_a, dout_s, dga, dgs = merge_bwd(projp, out_a, out_s, dmerged)
    dyn = mm_nt("d_ssd_br", dout_s, wsb, 512)
    gw["w_ssd_br"] = mm_tn("dw_ssd_br", yn, dout_s, 512, 1024).reshape(N_DEV, D // N_DEV, D)
    dattn = attn_br_bwd(dout_a, wab)
    gw["w_attn_br"] = mm_tn_colblk("dw_attn_br", attn, dout_a, D // N_DEV)
    dy, dz, dg_ssd = gnorm_bwd(y, projp, g_ssd, dyn)
    dxs, dbm, dcm, ddt, dal, ddsk, ddtb = ssd_bwd(xbc, projp, dtb, alog, dfull, states, dy)
    dx_x, dwc_x, dbc_x = conv_bwd("conv_bwd_x", projp, dxs, convw, conv_b, 0)
    dx_b, dwc_b, dbc_b = conv_bwd("conv_bwd_b", projp, dbm, convw, conv_b, D_INNER)
    dx_c, dwc_c, dbc_c = conv_bwd("conv_bwd_c", projp, dcm, convw, conv_b, D_INNER + SSM_GROUPS * D_STATE)
    dq, dk, dv, dsk = attn_bwd(qr, kr, projp, dattn, posf, inv128, sinks)
    dproj = jnp.concatenate([dz, dga, dgs, dx_x, dx_b, dx_c, dq, dk.astype(BF16), dv.astype(BF16), ddt], axis=1)
    du = mm_nt_red("d_in_proj", dproj, winp, 1024, 640)
    gw["w_in"] = _padded_to_win(mm_tn("dw_in", u, dproj, 1024, 640))
    gx, _, dg_mix = rms_bwd("norm_mix_bwd", x2, g_mix, du, dh1)
    dconvw = jnp.concatenate([dwc_x, dwc_b, dwc_c], axis=1)
    gw["conv_w"] = jnp.transpose(dconvw.reshape(CONV_WIDTH, N_DEV, CONV_DIM // N_DEV), (1, 0, 2))

    parts = dict(zip(big, exchange("scatter_grads", [gw[n] for n in big], "scatter")))
    res = {}
    for n in big:
        res[n] = adamw("adamw_" + n, parts[n], sh[n], Mo[n].reshape(sh[n].shape), Vo[n].reshape(sh[n].shape))

    small_g = dict(loss=loss_v[:, :1], g_mix=dg_mix, conv_b=jnp.concatenate([dbc_x, dbc_b, dbc_c], axis=1), dt_bias=ddtb,
                   a_log=dal, d_skip=ddsk, g_ssd=dg_ssd, sinks=dsk, g_ffn=dg_ffn, g_ple=dg_ple, g_final=dg_final)
    zero = jnp.zeros((1, 1), F32)
    vec_parts = exchange("gather_small", [_small_vec(small_g)], "gather")[0]
    sres = adamw("adamw_small", vec_parts, _small_vec({**W, "loss": zero}), _small_vec({**Mo, "loss": zero}),
                 _small_vec({**Vo, "loss": zero}))
    ssplit = [_small_split(a) for a in sres]
    loss = ssplit[0]["loss"].reshape(())
    for n in order:
        if n not in res:
            res[n] = tuple(s[n].reshape(W[n].shape) for s in ssplit)
        else:
            res[n] = tuple(a.reshape(W[n].shape) for a in res[n])
    outs = [loss, gx.reshape(x.shape)]
    for k in range(4):
        outs += [res[n][k] for n in order]
    return tuple(outs)
```

```python
import functools

import numpy as np
import jax
import jax.numpy as jnp
from jax import lax
from jax.experimental import pallas as pl
from jax.experimental.pallas import tpu as pltpu

F32 = jnp.float32
BF16 = jnp.bfloat16

N_DEV = 8
D_MODEL = 2048
HEAD_DIM = 64
ATTN_HEADS = 16
KV_HEADS = 4
Q_DIM = 1024
KV_DIM = 256
BLK = 128
D_INNER = 2048
SSM_HEADS = 32
SSM_GROUPS = 4
HEADS_PER_GROUP = 8
D_STATE = 128
CONV_WIDTH = 4
CONV_DIM = 3072
FFN_HIDDEN = 5632
PLE_DIM = 256
IN_DIM = 10784
NORM_EPS = 1e-6
SSM_NORM_EPS = 1e-5
ROPE_THETA = 10000.0

OFF_Z, OFF_GA, OFF_GS, OFF_XBC, OFF_Q, OFF_K, OFF_V, OFF_DT = 0, 2048, 4096, 6144, 9216, 10240, 10496, 10752
IN_PAD = 10880
DT_PAD = 128
SEG = dict(q=(0, 1024), k=(1024, 256), v=(1280, 256), z=(1536, 2048), xbc=(3584, 3072), dt=(6656, 32),
           ga=(6688, 2048), gs=(8736, 2048))

ADAM_LR, ADAM_B1, ADAM_B2, ADAM_EPS, ADAM_WD, ADAM_STEP = 0.001, 0.9, 0.999, 1e-08, 0.01, 10

VMEM_LIMIT = 56 * 1024 * 1024

NN = (((1,), (0,)), ((), ()))
NT = (((1,), (1,)), ((), ()))
TN = (((0,), (0,)), ((), ()))


_PENDING = []


def _raw_call(body, **kw):
    return pl.pallas_call(body, **kw)


def _pcall(body, **kw):
    deps = list(_PENDING)
    del _PENDING[:]
    if not deps:
        return _raw_call(body, **kw)
    n_in = len(kw["in_specs"])

    def tied(*refs):
        return body(*refs[:n_in], *refs[n_in + len(deps):])

    kw["in_specs"] = list(kw["in_specs"]) + [pl.BlockSpec(memory_space=pl.ANY)] * len(deps)
    call = _raw_call(tied, **kw)
    return lambda *ops: call(*ops, *deps)


def _cparams(sem=None):
    if sem is None:
        return pltpu.CompilerParams(vmem_limit_bytes=VMEM_LIMIT)
    return pltpu.CompilerParams(vmem_limit_bytes=VMEM_LIMIT, dimension_semantics=sem)


def _dot(a, b, dn):
    return lax.dot_general(a.astype(BF16), b.astype(BF16), dn, preferred_element_type=F32)


def _sigmoid(x):
    return 1.0 / (1.0 + jnp.exp(-x))


def _silu(x):
    return x * _sigmoid(x)


def _dsilu(x):
    s = _sigmoid(x)
    return s * (1.0 + x * (1.0 - s))


def _matmul(name, pairs, pair_specs, dn, grid, out_shapes, out_specs, nred=1, extra=(), extra_specs=(),
            epilogue=None, acc_shape=None):
    n_in = 2 * len(pairs) + len(extra)
    n_out = len(out_shapes)

    def body(*refs):
        ins = refs[:2 * len(pairs)]
        ex = refs[2 * len(pairs):n_in]
        outs = refs[n_in:n_in + n_out]

        def prod():
            s = None
            for p in range(len(pairs)):
                d = _dot(ins[2 * p][...], ins[2 * p + 1][...], dn)
                s = d if s is None else s + d
            return s

        def finish(val):
            if epilogue is None:
                outs[0][...] = val.astype(outs[0].dtype)
            else:
                res = epilogue(val, *[e[...] for e in ex])
                for o, r in zip(outs, res):
                    o[...] = r.astype(o.dtype)

        if nred == 1:
            finish(prod())
        else:
            acc = refs[n_in + n_out]
            k = pl.program_id(len(grid) - 1)

            @pl.when(k == 0)
            def _():
                acc[...] = jnp.zeros_like(acc)

            acc[...] += prod()

            @pl.when(k == nred - 1)
            def _():
                finish(acc[...])

    operands = []
    specs = []
    for (a, b), (sa, sb) in zip(pairs, pair_specs):
        operands += [a, b]
        specs += [sa, sb]
    operands += list(extra)
    specs += list(extra_specs)
    scratch = [pltpu.VMEM(acc_shape, F32)] if nred > 1 else []
    sem = ("arbitrary",) * len(grid)
    res = _pcall(body, name=name, grid=grid, in_specs=specs, out_specs=list(out_specs),
                 out_shape=list(out_shapes), scratch_shapes=scratch, compiler_params=_cparams(sem))(*operands)
    return res


def _sds(shape, dtype):
    return jax.ShapeDtypeStruct(shape, dtype)


def _row_tile(T):
    return min(1024, T)


def mm_nn(name, a, b, tn, out_dtype=F32, residual=None):
    M, K = a.shape
    N = b.shape[1]
    tm = _row_tile(M)
    grid = (M // tm, N // tn)
    extra, especs, epi = (), (), None
    if residual is not None:
        extra = (residual,)
        especs = (pl.BlockSpec((tm, tn), lambda i, n: (i, n)),)
        epi = lambda v, r: (v + r,)
    return _matmul(name, [(a, b)], [(pl.BlockSpec((tm, K), lambda i, n: (i, 0)), pl.BlockSpec((K, tn), lambda i, n: (0, n)))],
                   NN, grid, [_sds((M, N), out_dtype)], [pl.BlockSpec((tm, tn), lambda i, n: (i, n))],
                   extra=extra, extra_specs=especs, epilogue=epi)[0]


def mm_nn_colblk(name, a, b, out_dtype=F32):
    M, K = a.shape
    J, _, nb = b.shape
    tm = _row_tile(M)
    grid = (M // tm, J)
    return _matmul(name, [(a, b)], [(pl.BlockSpec((tm, K), lambda i, j: (i, 0)), pl.BlockSpec((None, K, nb), lambda i, j: (j, 0, 0)))],
                   NN, grid, [_sds((M, J * nb), out_dtype)], [pl.BlockSpec((tm, nb), lambda i, j: (i, j))])[0]


def mm_nt(name, a, w, tr, out_dtype=F32):
    M, C = a.shape
    R = w.shape[0]
    tm = _row_tile(M)
    grid = (M // tm, R // tr)
    return _matmul(name, [(a, w)], [(pl.BlockSpec((tm, C), lambda i, r: (i, 0)), pl.BlockSpec((tr, C), lambda i, r: (r, 0)))],
                   NT, grid, [_sds((M, R), out_dtype)], [pl.BlockSpec((tm, tr), lambda i, r: (i, r))])[0]


def mm_nt_red(name, a, w, tr, tk, out_dtype=F32):
    M, C = a.shape
    R = w.shape[0]
    tm = _row_tile(M)
    nk = C // tk
    grid = (M // tm, R // tr, nk)
    return _matmul(name, [(a, w)], [(pl.BlockSpec((tm, tk), lambda i, r, k: (i, k)), pl.BlockSpec((tr, tk), lambda i, r, k: (r, k)))],
                   NT, grid, [_sds((M, R), out_dtype)], [pl.BlockSpec((tm, tr), lambda i, r, k: (i, r))],
                   nred=nk, acc_shape=(tm, tr))[0]


def mm_tn(name, x, dy, tr, tc, out_dtype=BF16):
    M, R = x.shape
    C = dy.shape[1]
    grid = (R // tr, C // tc)
    return _matmul(name, [(x, dy)], [(pl.BlockSpec((M, tr), lambda r, c: (0, r)), pl.BlockSpec((M, tc), lambda r, c: (0, c)))],
                   TN, grid, [_sds((R, C), out_dtype)], [pl.BlockSpec((tr, tc), lambda r, c: (r, c))])[0]


def mm_tn_colblk(name, x, dy, nb, out_dtype=BF16):
    M, R = x.shape
    J = dy.shape[1] // nb
    grid = (J,)
    return _matmul(name, [(x, dy)], [(pl.BlockSpec((M, R), lambda j: (0, 0)), pl.BlockSpec((M, nb), lambda j: (0, j)))],
                   TN, grid, [_sds((J, R, nb), out_dtype)], [pl.BlockSpec((None, R, nb), lambda j: (j, 0, 0))])[0]


def _rows(T):
    return min(256, T)


def rms_fwd(name, x, g, eps=NORM_EPS):
    T, D = x.shape
    tm = _rows(T)

    def body(x_ref, g_ref, o_ref):
        xv = x_ref[...]
        r = lax.rsqrt(jnp.mean(xv * xv, axis=-1, keepdims=True) + eps)
        o_ref[...] = (xv * r * g_ref[...]).astype(BF16)

    return _pcall(body, name=name, grid=(T // tm,),
                  in_specs=[pl.BlockSpec((tm, D), lambda i: (i, 0)), pl.BlockSpec((1, D), lambda i: (0, 0))],
                  out_specs=pl.BlockSpec((tm, D), lambda i: (i, 0)), out_shape=_sds((T, D), BF16),
                  compiler_params=_cparams(("arbitrary",)))(x, g)


def rms_bwd(name, x, g, dy, dres, eps=NORM_EPS):
    T, D = x.shape
    tm = _rows(T)

    def body(x_ref, g_ref, dy_ref, dr_ref, dx_ref, dxb_ref, dg_ref):
        i = pl.program_id(0)
        xv = x_ref[...]
        r = lax.rsqrt(jnp.mean(xv * xv, axis=-1, keepdims=True) + eps)
        xh = xv * r
        dyv = dy_ref[...]
        gd = dyv * g_ref[...]
        dx = r * (gd - xh * jnp.mean(gd * xh, axis=-1, keepdims=True)) + dr_ref[...]
        dx_ref[...] = dx
        dxb_ref[...] = dx.astype(BF16)

        @pl.when(i == 0)
        def _():
            dg_ref[...] = jnp.zeros_like(dg_ref)

        dg_ref[...] += jnp.sum(dyv * xh, axis=0, keepdims=True)

    row = pl.BlockSpec((tm, D), lambda i: (i, 0))
    vec = pl.BlockSpec((1, D), lambda i: (0, 0))
    return _pcall(body, name=name, grid=(T // tm,), in_specs=[row, vec, row, row], out_specs=[row, row, vec],
                  out_shape=[_sds((T, D), F32), _sds((T, D), BF16), _sds((1, D), F32)],
                  compiler_params=_cparams(("arbitrary",)))(x, g, dy, dres)


def gnorm_fwd(y, projp, g):
    T, D = y.shape
    tm = _rows(T)

    def body(y_ref, z_ref, g_ref, o_ref):
        yz = y_ref[...] * _silu(z_ref[...])
        r = lax.rsqrt(jnp.mean(yz * yz, axis=-1, keepdims=True) + SSM_NORM_EPS)
        o_ref[...] = (yz * r * g_ref[...]).astype(BF16)

    row = pl.BlockSpec((tm, D), lambda i: (i, 0))
    return _pcall(body, name="gnorm_fwd", grid=(T // tm,),
                  in_specs=[row, pl.BlockSpec((tm, D), lambda i: (i, OFF_Z // D)), pl.BlockSpec((1, D), lambda i: (0, 0))],
                  out_specs=row, out_shape=_sds((T, D), BF16), compiler_params=_cparams(("arbitrary",)))(y, projp, g)


def gnorm_bwd(y, projp, g, dyn):
    T, D = y.shape
    tm = _rows(T)

    def body(y_ref, z_ref, g_ref, dyn_ref, dy_ref, dz_ref, dg_ref):
        i = pl.program_id(0)
        yv, zv = y_ref[...], z_ref[...]
        sz = _silu(zv)
        yz = yv * sz
        r = lax.rsqrt(jnp.mean(yz * yz, axis=-1, keepdims=True) + SSM_NORM_EPS)
        xh = yz * r
        dv = dyn_ref[...]
        gd = dv * g_ref[...]
        dyz = r * (gd - xh * jnp.mean(gd * xh, axis=-1, keepdims=True))
        dy_ref[...] = dyz * sz
        dz_ref[...] = (dyz * yv * _dsilu(zv)).astype(BF16)

        @pl.when(i == 0)
        def _():
            dg_ref[...] = jnp.zeros_like(dg_ref)

        dg_ref[...] += jnp.sum(dv * xh, axis=0, keepdims=True)

    row = pl.BlockSpec((tm, D), lambda i: (i, 0))
    vec = pl.BlockSpec((1, D), lambda i: (0, 0))
    return _pcall(body, name="gnorm_bwd", grid=(T // tm,),
                  in_specs=[row, pl.BlockSpec((tm, D), lambda i: (i, OFF_Z // D)), vec, row], out_specs=[row, row, vec],
                  out_shape=[_sds((T, D), F32), _sds((T, D), BF16), _sds((1, D), F32)],
                  compiler_params=_cparams(("arbitrary",)))(y, projp, g, dyn)


def merge_fwd(projp, out_a, out_s):
    T, D = out_a.shape
    tm = _rows(T)

    def body(ga_ref, gs_ref, a_ref, s_ref, o_ref):
        o_ref[...] = (_sigmoid(ga_ref[...]) * a_ref[...] + _sigmoid(gs_ref[...]) * s_ref[...]).astype(BF16)

    row = pl.BlockSpec((tm, D), lambda i: (i, 0))
    return _pcall(body, name="merge_fwd", grid=(T // tm,),
                  in_specs=[pl.BlockSpec((tm, D), lambda i: (i, OFF_GA // D)), pl.BlockSpec((tm, D), lambda i: (i, OFF_GS // D)), row, row],
                  out_specs=row, out_shape=_sds((T, D), BF16), compiler_params=_cparams(("arbitrary",)))(projp, projp, out_a, out_s)


def merge_bwd(projp, out_a, out_s, dmerged):
    T, D = out_a.shape
    tm = _rows(T)

    def body(ga_ref, gs_ref, a_ref, s_ref, dm_ref, da_ref, ds_ref, dga_ref, dgs_ref):
        dm = dm_ref[...]
        sa, ss = _sigmoid(ga_ref[...]), _sigmoid(gs_ref[...])
        da_ref[...] = (dm * sa).astype(BF16)
        ds_ref[...] = (dm * ss).astype(BF16)
        dga_ref[...] = (dm * a_ref[...] * sa * (1.0 - sa)).astype(BF16)
        dgs_ref[...] = (dm * s_ref[...] * ss * (1.0 - ss)).astype(BF16)

    row = pl.BlockSpec((tm, D), lambda i: (i, 0))
    return _pcall(body, name="merge_bwd", grid=(T // tm,),
                  in_specs=[pl.BlockSpec((tm, D), lambda i: (i, OFF_GA // D)), pl.BlockSpec((tm, D), lambda i: (i, OFF_GS // D)), row, row, row],
                  out_specs=[row] * 4, out_shape=[_sds((T, D), BF16)] * 4,
                  compiler_params=_cparams(("arbitrary",)))(projp, projp, out_a, out_s, dmerged)


def head_fwd_bwd(h2, pg, pp, g_final, target):
    T, D = h2.shape
    tm = _rows(T)

    def body(h_ref, pg_ref, pp_ref, g_ref, t_ref, loss_ref, dh_ref, dpg_ref, dpp_ref, dg_ref):
        i = pl.program_id(0)
        s = _sigmoid(pg_ref[...])
        ppv = pp_ref[...]
        h3 = h_ref[...] + s * ppv
        r = lax.rsqrt(jnp.mean(h3 * h3, axis=-1, keepdims=True) + NORM_EPS)
        xh = h3 * r
        gv = g_ref[...]
        e = xh * gv - t_ref[...]
        dyo = e * (1.0 / D)
        gd = dyo * gv
        dh = r * (gd - xh * jnp.mean(gd * xh, axis=-1, keepdims=True))
        dh_ref[...] = dh
        dpg_ref[...] = (dh * ppv * s * (1.0 - s)).astype(BF16)
        dpp_ref[...] = (dh * s).astype(BF16)

        @pl.when(i == 0)
        def _():
            dg_ref[...] = jnp.zeros_like(dg_ref)
            loss_ref[...] = jnp.zeros_like(loss_ref)

        dg_ref[...] += jnp.sum(dyo * xh, axis=0, keepdims=True)
        part = 0.5 * jnp.sum(jnp.mean(e * e, axis=-1, keepdims=True), axis=0, keepdims=True)
        loss_ref[...] += jnp.broadcast_to(part, loss_ref.shape)

    row = pl.BlockSpec((tm, D), lambda i: (i, 0))
    vec = pl.BlockSpec((1, D), lambda i: (0, 0))
    return _pcall(body, name="head_fwd_bwd", grid=(T // tm,), in_specs=[row, row, row, vec, row],
                  out_specs=[pl.BlockSpec((1, 128), lambda i: (0, 0)), row, row, row, vec],
                  out_shape=[_sds((1, 128), F32), _sds((T, D), F32), _sds((T, D), BF16), _sds((T, D), BF16), _sds((1, D), F32)],
                  compiler_params=_cparams(("arbitrary",)))(h2, pg, pp, g_final, target)


def ffn_up(f, wg, wu):
    T, D = f.shape
    J, _, nb = wg.shape
    tm = _row_tile(T)

    def body(f_ref, wg_ref, wu_ref, g_ref, u_ref, a_ref):
        fv = f_ref[...]
        g = _dot(fv, wg_ref[...], NN)
        u = _dot(fv, wu_ref[...], NN)
        g_ref[...] = g
        u_ref[...] = u
        a_ref[...] = (_silu(g) * u).astype(BF16)

    wspec = pl.BlockSpec((None, D, nb), lambda i, j: (j, 0, 0))
    ospec = pl.BlockSpec((None, tm, nb), lambda i, j: (j, i, 0))
    return _pcall(body, name="ffn_up", grid=(T // tm, J), in_specs=[pl.BlockSpec((tm, D), lambda i, j: (i, 0)), wspec, wspec],
                  out_specs=[ospec] * 3, out_shape=[_sds((J, T, nb), F32), _sds((J, T, nb), F32), _sds((J, T, nb), BF16)],
                  compiler_params=_cparams(("arbitrary", "arbitrary")))(f, wg, wu)


def ffn_down(act, wd, h1):
    J, T, nb = act.shape
    D = wd.shape[2]
    tm = _row_tile(T)
    tn = 1024
    grid = (T // tm, D // tn, J)
    return _matmul("ffn_down", [(act, wd)],
                   [(pl.BlockSpec((None, tm, nb), lambda i, n, j: (j, i, 0)), pl.BlockSpec((None, nb, tn), lambda i, n, j: (j, 0, n)))],
                   NN, grid, [_sds((T, D), F32)], [pl.BlockSpec((tm, tn), lambda i, n, j: (i, n))], nred=J, acc_shape=(tm, tn),
                   extra=(h1,), extra_specs=(pl.BlockSpec((tm, tn), lambda i, n, j: (i, n)),), epilogue=lambda v, r: (v + r,))[0]


def ffn_down_bwd(dh2b, wd, gate, up):
    T, D = dh2b.shape
    J, nb, _ = wd.shape
    tm = _row_tile(T)
    ospec = pl.BlockSpec((None, tm, nb), lambda i, j: (j, i, 0))

    def epi(da, g, u):
        return (da * u * _dsilu(g), da * _silu(g))

    return _matmul("ffn_down_bwd", [(dh2b, wd)],
                   [(pl.BlockSpec((tm, D), lambda i, j: (i, 0)), pl.BlockSpec((None, nb, D), lambda i, j: (j, 0, 0)))],
                   NT, (T // tm, J), [_sds((J, T, nb), BF16)] * 2, [ospec, ospec],
                   extra=(gate, up), extra_specs=(ospec, ospec), epilogue=epi)


def ffn_up_bwd(dgate, dup, wg, wu):
    J, T, nb = dgate.shape
    D = wg.shape[1]
    tm = _row_tile(T)
    tr = 1024
    aspec = pl.BlockSpec((None, tm, nb), lambda i, r, j: (j, i, 0))
    wspec = pl.BlockSpec((None, tr, nb), lambda i, r, j: (j, r, 0))
    return _matmul("ffn_up_bwd", [(dgate, wg), (dup, wu)], [(aspec, wspec), (aspec, wspec)], NT, (T // tm, D // tr, J),
                   [_sds((T, D), F32)], [pl.BlockSpec((tm, tr), lambda i, r, j: (i, r))], nred=J, acc_shape=(tm, tr))[0]


def wgrad_rowblk_lhs(name, xb, dy, tc):
    J, T, nb = xb.shape
    C = dy.shape[1]
    return _matmul(name, [(xb, dy)],
                   [(pl.BlockSpec((None, T, nb), lambda j, c: (j, 0, 0)), pl.BlockSpec((T, tc), lambda j, c: (0, c)))],
                   TN, (J, C // tc), [_sds((J, nb, C), BF16)], [pl.BlockSpec((None, nb, tc), lambda j, c: (j, 0, c))])[0]


def wgrad_colblk_rhs(name, x, dyb, tr):
    T, R = x.shape
    J, _, nb = dyb.shape
    return _matmul(name, [(x, dyb)],
                   [(pl.BlockSpec((T, tr), lambda j, r: (0, r)), pl.BlockSpec((None, T, nb), lambda j, r: (j, 0, 0)))],
                   TN, (J, R // tr), [_sds((J, R, nb), BF16)], [pl.BlockSpec((None, tr, nb), lambda j, r: (j, r, 0))])[0]


def attn_br_bwd(dout_a, wab):
    T, D = dout_a.shape
    J, R, nb = wab.shape
    tm = _row_tile(T)
    return _matmul("attn_br_bwd", [(dout_a, wab)],
                   [(pl.BlockSpec((tm, nb), lambda i, j: (i, j)), pl.BlockSpec((None, R, nb), lambda i, j: (j, 0, 0)))],
                   NT, (T // tm, J), [_sds((T, R), BF16)], [pl.BlockSpec((tm, R), lambda i, j: (i, 0))], nred=J, acc_shape=(tm, R))[0]


def _adam_math(w, g, m, v):
    m2 = ADAM_B1 * m + (1.0 - ADAM_B1) * g
    v2 = ADAM_B2 * v + (1.0 - ADAM_B2) * (g * g)
    m_hat = m2 / (1.0 - ADAM_B1 ** ADAM_STEP)
    v_hat = v2 / (1.0 - ADAM_B2 ** ADAM_STEP)
    delta = -ADAM_LR * (m_hat / (jnp.sqrt(v_hat) + ADAM_EPS) + ADAM_WD * w)
    return delta, m2, v2


def adamw(name, parts, w, m, v, own=None):
    R, C = w.shape
    P = parts.shape[0]
    tr = R
    for cand in (256, 176, 128, 64, 32, 16, 8):
        if R % cand == 0 and R > cand:
            tr = cand
            break

    def body(*refs):
        p_ref = refs[0]
        w_ref, m_ref, v_ref, g_ref, d_ref, m2_ref, v2_ref = refs[-7:]
        g = p_ref[0].astype(F32) if own is None else refs[1][...].astype(F32)
        for s in range(1 if own is None else 0, P):
            g = g + p_ref[s].astype(F32)
        d, m2, v2 = _adam_math(w_ref[...], g, m_ref[...], v_ref[...])
        g_ref[...] = g
        d_ref[...] = d
        m2_ref[...] = m2
        v2_ref[...] = v2

    blk = pl.BlockSpec((tr, C), lambda i: (i, 0))
    ops = [parts] + ([] if own is None else [own]) + [w, m, v]
    specs = [pl.BlockSpec((P, tr, C), lambda i: (0, i, 0))] + [blk] * (len(ops) - 1)
    return _pcall(body, name=name, grid=(R // tr,), in_specs=specs,
                  out_specs=[blk] * 4, out_shape=[_sds((R, C), F32)] * 4, compiler_params=_cparams(("arbitrary",)))(*ops)


def exchange(name, arrays, mode):
    n = len(arrays)
    out_shapes = []
    for a in arrays:
        shp = a.shape if mode == "scatter" else (N_DEV,) + a.shape
        out_shapes.append(_sds(shp, a.dtype))

    def body(*refs):
        ins, outs = refs[:n], refs[n:2 * n]
        send_sems, recv_sems, local_sems = refs[2 * n:]
        x, y, c = lax.axis_index("x"), lax.axis_index("y"), lax.axis_index("c")
        me = 4 * x + 2 * y + c

        def src(a, dest):
            return ins[a].at[dest] if mode == "scatter" else ins[a]

        local = [pltpu.make_async_copy(src(a, me), outs[a].at[me], local_sems.at[a]) for a in range(n)]
        for cp in local:
            cp.start()
        remote = []
        for k in range(1, N_DEV):
            px = 1 - x if k & 4 else x
            py = 1 - y if k & 2 else y
            pc = 1 - c if k & 1 else c
            peer = 4 * px + 2 * py + pc
            for a in range(n):
                cp = pltpu.make_async_remote_copy(src_ref=src(a, peer), dst_ref=outs[a].at[me],
                                                  send_sem=send_sems.at[a * 7 + k - 1], recv_sem=recv_sems.at[a * 7 + k - 1],
                                                  device_id=(px, py, pc), device_id_type=pl.DeviceIdType.MESH)
                cp.start()
                arrival = pltpu.make_async_remote_copy(src_ref=src(a, peer), dst_ref=outs[a].at[peer],
                                                       send_sem=send_sems.at[a * 7 + k - 1], recv_sem=recv_sems.at[a * 7 + k - 1],
                                                       device_id=(px, py, pc), device_id_type=pl.DeviceIdType.MESH)
                remote.append((cp, arrival))
        for cp, arrival in remote:
            cp.wait_send()
            arrival.wait_recv()
        for cp in local:
            cp.wait()

    any_spec = pl.BlockSpec(memory_space=pl.ANY)
    return _pcall(body, name=name, in_specs=[any_spec] * n, out_specs=[any_spec] * n, out_shape=out_shapes,
                  scratch_shapes=[pltpu.SemaphoreType.DMA((7 * n,)), pltpu.SemaphoreType.DMA((7 * n,)), pltpu.SemaphoreType.DMA((n,))],
                  compiler_params=pltpu.CompilerParams(has_side_effects=True))(*arrays)


_HBM = pl.BlockSpec(memory_space=pltpu.HBM)
_SEM = pl.BlockSpec(memory_space=pltpu.SEMAPHORE)
_ANY = pl.BlockSpec(memory_space=pl.ANY)
_SPLIT_PARAMS = dict(compiler_params=pltpu.CompilerParams(has_side_effects=pltpu.SideEffectType.DATAFLOW_SIDE_EFFECTING))
ICI_SAME_CORE = (2, 4, 6)
ALL_PEERS = (1, 2, 3, 4, 5, 6, 7)


def _mesh_pos():
    x, y, c = lax.axis_index("x"), lax.axis_index("y"), lax.axis_index("c")
    return x, y, c, 4 * x + 2 * y + c


def _peer_of(k, x, y, c):
    px = 1 - x if k & 4 else x
    py = 1 - y if k & 2 else y
    pc = 1 - c if k & 1 else c
    return (px, py, pc), 4 * px + 2 * py + pc


def _split_copies(mode, ks, srcs, lands, send_sems, recv_sems):
    x, y, c, me = _mesh_pos()
    pairs = []
    for a in range(len(srcs)):
        for j, k in enumerate(ks):
            dev, peer = _peer_of(k, x, y, c)
            i = a * len(ks) + j
            if mode == "gather":
                s_out, d_out, d_in = srcs[a], lands[a].at[me], lands[a].at[peer]
            else:
                s_out, d_out, d_in = srcs[a].at[peer], lands[a].at[k - 1], lands[a].at[k - 1]
            both = [pltpu.make_async_remote_copy(src_ref=s_out, dst_ref=d, send_sem=send_sems.at[i], recv_sem=recv_sems.at[i],
                                                 device_id=dev, device_id_type=pl.DeviceIdType.MESH) for d in (d_out, d_in)]
            pairs.append(tuple(both))
    return pairs


def split_start(name, mode, ks, srcs, after=None):
    n, nk = len(srcs), len(ks)
    srcs = [pltpu.with_memory_space_constraint(s, pltpu.HBM) for s in srcs]
    shapes = [((N_DEV,) + s.shape) if mode == "gather" else ((N_DEV - 1,) + s.shape[1:]) for s in srcs]
    lands = [pltpu.with_memory_space_constraint(lax.empty(shp, s.dtype), pltpu.HBM) for shp, s in zip(shapes, srcs)]
    extra = [] if after is None else [after]

    def body(*refs):
        src_refs, land_refs = refs[:n], refs[n:2 * n]
        send_sems, recv_sems = refs[2 * n + len(extra)], refs[2 * n + len(extra) + 1]
        token = refs[-1]
        for out, _ in _split_copies(mode, ks, src_refs, land_refs, send_sems, recv_sems):
            out.start()
        token[...] = jnp.zeros_like(token)

    out_shape = (pltpu.SemaphoreType.DMA((n * nk,)), pltpu.SemaphoreType.DMA((n * nk,)),
                 *[pltpu.HBM(s.shape, s.dtype) for s in srcs], *[pltpu.HBM(shp, s.dtype) for shp, s in zip(shapes, srcs)],
                 _sds((8, 128), F32))
    res = _raw_call(body, name=name, out_shape=out_shape, in_specs=[_HBM] * (2 * n) + [_ANY] * len(extra),
                    out_specs=(_SEM, _SEM, *[_HBM] * (2 * n), pl.BlockSpec(memory_space=pltpu.VMEM)),
                    input_output_aliases={i: 2 + i for i in range(2 * n)}, **_SPLIT_PARAMS)(*srcs, *lands, *extra)
    _PENDING.append(res[-1])
    return dict(mode=mode, ks=ks, sems=(res[0], res[1]), srcs=list(res[2:2 + n]), lands=list(res[2 + n:2 + 2 * n]), token=res[-1])


def split_wait(name, h, after):
    n = len(h["srcs"])

    def body(*refs):
        src_refs, land_refs = refs[:n], refs[n:2 * n]
        send_sems, recv_sems = refs[2 * n], refs[2 * n + 1]
        for out, arriving in _split_copies(h["mode"], h["ks"], src_refs, land_refs, send_sems, recv_sems):
            out.wait_send()
            arriving.wait_recv()

    both = h["srcs"] + h["lands"]
    res = _raw_call(body, name=name, out_shape=tuple(pltpu.HBM(a.shape, a.dtype) for a in both),
                    in_specs=[_HBM] * (2 * n) + [_SEM, _SEM, _ANY], out_specs=tuple([_HBM] * (2 * n)),
                    input_output_aliases={i: i for i in range(2 * n)}, **_SPLIT_PARAMS)(*both, *h["sems"], after)
    return list(res[:n]), list(res[n:])


def gather_finish(name, shards, lands):
    n = len(shards)

    def body(*refs):
        sh, land = refs[:n], refs[2 * n:3 * n]
        send_sems, recv_sems, local_sems = refs[3 * n:]
        x, y, c, me = _mesh_pos()
        sib_dev, sib = _peer_of(1, x, y, c)
        local = [pltpu.make_async_copy(sh[a], land[a].at[me], local_sems.at[a]) for a in range(n)]
        for cp in local:
            cp.start()
        pairs = []
        for a in range(n):
            for j, k in enumerate((0,) + ICI_SAME_CORE):
                _, mine = _peer_of(k, x, y, c)
                _, theirs = _peer_of(k | 1, x, y, c)
                src = sh[a] if k == 0 else land[a].at[mine]
                both = [pltpu.make_async_remote_copy(src_ref=src, dst_ref=d, send_sem=send_sems.at[4 * a + j],
                                                     recv_sem=recv_sems.at[4 * a + j], device_id=sib_dev,
                                                     device_id_type=pl.DeviceIdType.MESH) for d in (land[a].at[mine], land[a].at[theirs])]
                both[0].start()
                pairs.append(both)
        for out, arriving in pairs:
            out.wait_send()
            arriving.wait_recv()
        for cp in local:
            cp.wait()

    return _pcall(body, name=name, in_specs=[_ANY] * (2 * n), out_specs=[_ANY] * n,
                  out_shape=[_sds(l.shape, l.dtype) for l in lands], input_output_aliases={n + a: a for a in range(n)},
                  scratch_shapes=[pltpu.SemaphoreType.DMA((4 * n,)), pltpu.SemaphoreType.DMA((4 * n,)), pltpu.SemaphoreType.DMA((n,))],
                  )(*shards, *lands)


def _rope_parts(pos_ref, inv_ref):
    ang = pos_ref[...] * inv_ref[...]
    return jnp.cos(ang), jnp.sin(ang)


def _rot_half(t):
    lane = lax.broadcasted_iota(jnp.int32, t.shape, 1)
    return jnp.where((lane % HEAD_DIM) < HEAD_DIM // 2, -pltpu.roll(t, 128 - HEAD_DIM // 2, 1), pltpu.roll(t, HEAD_DIM // 2, 1))


def _attn_mask(n):
    row = lax.broadcasted_iota(jnp.int32, (BLK, 2 * BLK), 0)
    col = lax.broadcasted_iota(jnp.int32, (BLK, 2 * BLK), 1)
    dist = row + BLK - col
    return (dist >= 0) & (dist < BLK) & ((n * BLK - BLK + col) >= 0)


def _attn_specs(T):
    prev = lambda n: jnp.maximum(n - 1, 0)
    kc = pl.BlockSpec((BLK, KV_DIM), lambda n: (n, OFF_K // KV_DIM))
    kp = pl.BlockSpec((BLK, KV_DIM), lambda n: (prev(n), OFF_K // KV_DIM))
    vc = pl.BlockSpec((BLK, KV_DIM), lambda n: (n, OFF_V // KV_DIM))
    vp = pl.BlockSpec((BLK, KV_DIM), lambda n: (prev(n), OFF_V // KV_DIM))
    pc = pl.BlockSpec((BLK, 1), lambda n: (n, 0))
    pp = pl.BlockSpec((BLK, 1), lambda n: (prev(n), 0))
    inv = pl.BlockSpec((1, 128), lambda n: (0, 0))
    sink = pl.BlockSpec(memory_space=pltpu.SMEM)
    return kc, kp, vc, vp, pc, pp, inv, sink


def _softmax_sink(sc, valid, sink):
    sc = jnp.where(valid, sc * (HEAD_DIM ** -0.5), -1e30)
    m = jnp.maximum(jnp.max(sc, axis=1, keepdims=True), sink)
    e = jnp.exp(sc - m)
    es = jnp.exp(sink - m)
    den = jnp.sum(e, axis=1, keepdims=True) + es
    return e / den, es / den


def attn_fwd(projp, posf, inv128, sinks):
    T = projp.shape[0]
    kc, kp, vc, vp, pc, pp, inv, sink = _attn_specs(T)

    def body(q_ref, kc_ref, kp_ref, vc_ref, vp_ref, pc_ref, pp_ref, inv_ref, sink_ref, o_ref, qr_ref, kr_ref):
        n = pl.program_id(0)
        cos_c, sin_c = _rope_parts(pc_ref, inv_ref)
        cos_p, sin_p = _rope_parts(pp_ref, inv_ref)
        valid = _attn_mask(n)
        k_c, k_p = [], []
        for s in range(KV_DIM // 128):
            t = kc_ref[:, 128 * s:128 * (s + 1)]
            k_c.append((t * cos_c + _rot_half(t) * sin_c).astype(BF16))
            kr_ref[:, 128 * s:128 * (s + 1)] = k_c[s]
            t = kp_ref[:, 128 * s:128 * (s + 1)]
            k_p.append((t * cos_p + _rot_half(t) * sin_p).astype(BF16))
        for s in range(Q_DIM // 128):
            t = q_ref[:, 128 * s:128 * (s + 1)]
            qs = (t * cos_c + _rot_half(t) * sin_c).astype(BF16)
            qr_ref[:, 128 * s:128 * (s + 1)] = qs
            for e in range(2):
                hq = 2 * s + e
                hk = hq // (ATTN_HEADS // KV_HEADS)
                lo = HEAD_DIM * (hk % 2)
                kcat = jnp.concatenate([k_p[hk // 2][:, lo:lo + HEAD_DIM], k_c[hk // 2][:, lo:lo + HEAD_DIM]], axis=0)
                vcat = jnp.concatenate([vp_ref[:, HEAD_DIM * hk:HEAD_DIM * (hk + 1)], vc_ref[:, HEAD_DIM * hk:HEAD_DIM * (hk + 1)]], axis=0)
                sc = _dot(qs[:, HEAD_DIM * e:HEAD_DIM * (e + 1)], kcat, NT)
                p, _ = _softmax_sink(sc, valid, sink_ref[0, hq])
                o_ref[:, HEAD_DIM * hq:HEAD_DIM * (hq + 1)] = _dot(p, vcat, NN).astype(BF16)

    qspec = pl.BlockSpec((BLK, Q_DIM), lambda n: (n, OFF_Q // Q_DIM))
    orow = pl.BlockSpec((BLK, Q_DIM), lambda n: (n, 0))
    krow = pl.BlockSpec((BLK, KV_DIM), lambda n: (n, 0))
    return _pcall(body, name="attn_fwd", grid=(T // BLK,), in_specs=[qspec, kc, kp, vc, vp, pc, pp, inv, sink],
                  out_specs=[orow, orow, krow], out_shape=[_sds((T, Q_DIM), BF16), _sds((T, Q_DIM), BF16), _sds((T, KV_DIM), BF16)],
                  compiler_params=_cparams(("arbitrary",)))(projp, projp, projp, projp, projp, posf, posf, inv128, sinks)


def attn_bwd(qr, kr, projp, dattn, posf, inv128, sinks):
    T = projp.shape[0]
    _, _, vc, vp, pc, pp, inv, sink = _attn_specs(T)
    G = ATTN_HEADS // KV_HEADS

    def body(qr_ref, krc_ref, krp_ref, vc_ref, vp_ref, do_ref, pc_ref, pp_ref, inv_ref, sink_ref, dq_ref, dk_ref, dv_ref, dsk_ref):
        n = pl.program_id(0)

        @pl.when(n == 0)
        def _():
            dk_ref[...] = jnp.zeros_like(dk_ref)
            dv_ref[...] = jnp.zeros_like(dv_ref)
            dsk_ref[...] = jnp.zeros_like(dsk_ref)

        cos_c, sin_c = _rope_parts(pc_ref, inv_ref)
        cos_p, sin_p = _rope_parts(pp_ref, inv_ref)
        valid = _attn_mask(n)
        lane = lax.broadcasted_iota(jnp.int32, (1, 128), 1)
        dsk = jnp.zeros((1, 128), F32)
        dq_heads, dk_heads, dv_heads = [], [], []
        for hk in range(KV_HEADS):
            ksl = slice(HEAD_DIM * hk, HEAD_DIM * (hk + 1))
            kcat = jnp.concatenate([krp_ref[:, ksl], krc_ref[:, ksl]], axis=0)
            vcat = jnp.concatenate([vp_ref[:, ksl], vc_ref[:, ksl]], axis=0)
            dkcat = jnp.zeros((2 * BLK, HEAD_DIM), F32)
            dvcat = jnp.zeros((2 * BLK, HEAD_DIM), F32)
            for g in range(G):
                hq = G * hk + g
                qsl = slice(HEAD_DIM * hq, HEAD_DIM * (hq + 1))
                q_h = qr_ref[:, qsl]
                do_h = do_ref[:, qsl]
                p, psink = _softmax_sink(_dot(q_h, kcat, NT), valid, sink_ref[0, hq])
                dp = _dot(do_h, vcat, NT)
                delta = jnp.sum(p * dp, axis=1, keepdims=True)
                ds = p * (dp - delta) * (HEAD_DIM ** -0.5)
                dsk = dsk + jnp.where(lane == hq, -jnp.sum(psink * delta, axis=0, keepdims=True), 0.0)
                dq_heads.append(_dot(ds, kcat, NN))
                dkcat = dkcat + _dot(ds, q_h, TN)
                dvcat = dvcat + _dot(p, do_h, TN)
            dk_heads.append(dkcat)
            dv_heads.append(dvcat)
        dsk_ref[...] += dsk
        for s in range(Q_DIM // 128):
            t = jnp.concatenate([dq_heads[2 * s], dq_heads[2 * s + 1]], axis=1)
            dq_ref[:, 128 * s:128 * (s + 1)] = (t * cos_c - _rot_half(t) * sin_c).astype(BF16)
        cur = pl.ds(pl.multiple_of(n * BLK, BLK), BLK)
        prv = pl.ds(pl.multiple_of(jnp.maximum(n - 1, 0) * BLK, BLK), BLK)
        for s in range(KV_DIM // 128):
            tc = jnp.concatenate([dk_heads[2 * s][BLK:], dk_heads[2 * s + 1][BLK:]], axis=1)
            tp = jnp.concatenate([dk_heads[2 * s][:BLK], dk_heads[2 * s + 1][:BLK]], axis=1)
            cols = slice(128 * s, 128 * (s + 1))
            dk_ref[cur, cols] += tc * cos_c - _rot_half(tc) * sin_c
            dk_ref[prv, cols] += tp * cos_p - _rot_half(tp) * sin_p
            dv_ref[cur, cols] += jnp.concatenate([dv_heads[2 * s][BLK:], dv_heads[2 * s + 1][BLK:]], axis=1)
            dv_ref[prv, cols] += jnp.concatenate([dv_heads[2 * s][:BLK], dv_heads[2 * s + 1][:BLK]], axis=1)

    qrow = pl.BlockSpec((BLK, Q_DIM), lambda n: (n, 0))
    krc = pl.BlockSpec((BLK, KV_DIM), lambda n: (n, 0))
    krp = pl.BlockSpec((BLK, KV_DIM), lambda n: (jnp.maximum(n - 1, 0), 0))
    whole = pl.BlockSpec((T, KV_DIM), lambda n: (0, 0))
    return _pcall(body, name="attn_bwd", grid=(T // BLK,), in_specs=[qrow, krc, krp, vc, vp, qrow, pc, pp, inv, sink],
                  out_specs=[qrow, whole, whole, pl.BlockSpec((1, 128), lambda n: (0, 0))],
                  out_shape=[_sds((T, Q_DIM), BF16), _sds((T, KV_DIM), F32), _sds((T, KV_DIM), F32), _sds((1, 128), F32)],
                  compiler_params=_cparams(("arbitrary",)))(qr, kr, kr, projp, projp, dattn, posf, posf, inv128, sinks)


CONV_CB = 256


def _shift_down(x, s):
    row = lax.broadcasted_iota(jnp.int32, x.shape, 0)
    return jnp.where(row >= s, pltpu.roll(x, s, 0), 0.0)


def _shift_up(x, s):
    T = x.shape[0]
    row = lax.broadcasted_iota(jnp.int32, x.shape, 0)
    return jnp.where(row < T - s, pltpu.roll(x, T - s, 0), 0.0)


def _conv_pre(x, w_ref, b_ref):
    acc = x * w_ref[CONV_WIDTH - 1:CONV_WIDTH, :] + b_ref[...]
    for s in range(1, CONV_WIDTH):
        acc = acc + _shift_down(x, s) * w_ref[CONV_WIDTH - 1 - s:CONV_WIDTH - s, :]
    return acc


def conv_fwd(projp, conv_w, conv_b):
    T = projp.shape[0]

    def body(x_ref, w_ref, b_ref, o_ref):
        o_ref[...] = _silu(_conv_pre(x_ref[...], w_ref, b_ref))

    return _pcall(body, name="conv_fwd", grid=(CONV_DIM // CONV_CB,),
                  in_specs=[pl.BlockSpec((T, CONV_CB), lambda c: (0, OFF_XBC // CONV_CB + c)),
                            pl.BlockSpec((CONV_WIDTH, CONV_CB), lambda c: (0, c)), pl.BlockSpec((1, CONV_CB), lambda c: (0, c))],
                  out_specs=pl.BlockSpec((T, CONV_CB), lambda c: (0, c)), out_shape=_sds((T, CONV_DIM), F32),
                  compiler_params=_cparams(("arbitrary",)))(projp, conv_w, conv_b)


def conv_bwd(name, projp, dact, conv_w, conv_b, col0):
    T, C = dact.shape
    c0 = col0 // CONV_CB

    def body(x_ref, da_ref, w_ref, b_ref, dx_ref, dw_ref, db_ref):
        x = x_ref[...]
        dpre = da_ref[...] * _dsilu(_conv_pre(x, w_ref, b_ref))
        dx = dpre * w_ref[CONV_WIDTH - 1:CONV_WIDTH, :]
        dw_ref[CONV_WIDTH - 1:CONV_WIDTH, :] = jnp.sum(dpre * x, axis=0, keepdims=True)
        for s in range(1, CONV_WIDTH):
            i = CONV_WIDTH - 1 - s
            dx = dx + _shift_up(dpre, s) * w_ref[i:i + 1, :]
            dw_ref[i:i + 1, :] = jnp.sum(dpre * _shift_down(x, s), axis=0, keepdims=True)
        dx_ref[...] = dx.astype(BF16)
        db_ref[...] = jnp.sum(dpre, axis=0, keepdims=True)

    return _pcall(body, name=name, grid=(C // CONV_CB,),
                  in_specs=[pl.BlockSpec((T, CONV_CB), lambda c: (0, OFF_XBC // CONV_CB + c0 + c)),
                            pl.BlockSpec((T, CONV_CB), lambda c: (0, c)),
                            pl.BlockSpec((CONV_WIDTH, CONV_CB), lambda c: (0, c0 + c)), pl.BlockSpec((1, CONV_CB), lambda c: (0, c0 + c))],
                  out_specs=[pl.BlockSpec((T, CONV_CB), lambda c: (0, c)), pl.BlockSpec((CONV_WIDTH, CONV_CB), lambda c: (0, c)),
                             pl.BlockSpec((1, CONV_CB), lambda c: (0, c))],
                  out_shape=[_sds((T, C), BF16), _sds((CONV_WIDTH, C), F32), _sds((1, C), F32)],
                  compiler_params=_cparams(("arbitrary",)))(projp, dact, conv_w, conv_b)


def _softplus(x):
    return jnp.maximum(x, 0.0) + jnp.log1p(jnp.exp(-jnp.abs(x)))


def _tri(lower):
    r = lax.broadcasted_iota(jnp.int32, (BLK, BLK), 0)
    c = lax.broadcasted_iota(jnp.int32, (BLK, BLK), 1)
    return (r >= c) if lower else (c >= r)


def _ssd_chunk_setup(dt_ref, dtb_ref, alog_ref):
    raw = dt_ref[...] + dtb_ref[...]
    dt = _softplus(raw)
    aneg = -jnp.exp(alog_ref[...])
    a = dt * aneg
    acs = jnp.dot(_tri(True).astype(F32), a, precision=lax.Precision.HIGHEST, preferred_element_type=F32)
    return raw, dt, aneg, acs, acs.T


def _ssd_specs(T, rev):
    nc = T // BLK
    ci = (lambda c: nc - 1 - c) if rev else (lambda c: c)
    xs = pl.BlockSpec((BLK, D_INNER), lambda c: (ci(c), 0))
    bm = pl.BlockSpec((BLK, SSM_GROUPS * D_STATE), lambda c: (ci(c), D_INNER // (SSM_GROUPS * D_STATE)))
    cm = pl.BlockSpec((BLK, SSM_GROUPS * D_STATE), lambda c: (ci(c), D_INNER // (SSM_GROUPS * D_STATE) + 1))
    dt = pl.BlockSpec((BLK, DT_PAD), lambda c: (ci(c), OFF_DT // DT_PAD))
    v128 = pl.BlockSpec((1, 128), lambda c: (0, 0))
    dfull = pl.BlockSpec((1, D_INNER), lambda c: (0, 0))
    st = pl.BlockSpec((None, SSM_HEADS, HEAD_DIM, D_STATE), lambda c: (ci(c), 0, 0, 0))
    return xs, bm, cm, dt, v128, dfull, st, ci


def ssd_fwd(xbc, projp, dtb, alog, dfull):
    T = xbc.shape[0]
    nc = T // BLK
    xs, bm, cm, dts, v128, dfs, st, _ = _ssd_specs(T, False)

    def body(xs_ref, b_ref, c_ref, dt_ref, dtb_ref, alog_ref, d_ref, y_ref, st_ref, h_scr):
        c = pl.program_id(0)

        @pl.when(c == 0)
        def _():
            h_scr[...] = jnp.zeros_like(h_scr)

        _, dt, _, acs, acsT = _ssd_chunk_setup(dt_ref, dtb_ref, alog_ref)
        tril = _tri(True)
        for g in range(SSM_GROUPS):
            B = b_ref[:, D_STATE * g:D_STATE * (g + 1)].astype(BF16)
            C = c_ref[:, D_STATE * g:D_STATE * (g + 1)].astype(BF16)
            cb = _dot(C, B, NT)
            for hh in range(HEADS_PER_GROUP):
                h = HEADS_PER_GROUP * g + hh
                hs = slice(HEAD_DIM * h, HEAD_DIM * (h + 1))
                dt_h, acs_h, acsT_h = dt[:, h:h + 1], acs[:, h:h + 1], acsT[h:h + 1, :]
                alast = acs[BLK - 1:BLK, h:h + 1]
                x_h = xs_ref[:, hs]
                xd = x_h * dt_h
                decay = jnp.where(tril, jnp.exp(jnp.where(tril, acs_h - acsT_h, 0.0)), 0.0)
                y = _dot(cb * decay, xd, NN)
                hp = h_scr[h]
                st_ref[h] = hp
                y = y + jnp.exp(acs_h) * _dot(C, hp, NT)
                h_scr[h] = jnp.exp(alast) * hp + _dot(xd * jnp.exp(alast - acs_h), B, TN)
                y_ref[:, hs] = y + d_ref[:, hs] * x_h

    return _pcall(body, name="ssd_fwd", grid=(nc,), in_specs=[xs, bm, cm, dts, v128, v128, dfs],
                  out_specs=[xs, st], out_shape=[_sds((T, D_INNER), F32), _sds((nc, SSM_HEADS, HEAD_DIM, D_STATE), F32)],
                  scratch_shapes=[pltpu.VMEM((SSM_HEADS, HEAD_DIM, D_STATE), F32)],
                  compiler_params=_cparams(("arbitrary",)))(xbc, xbc, xbc, projp, dtb, alog, dfull)


def ssd_bwd(xbc, projp, dtb, alog, dfull, states, dy):
    T = xbc.shape[0]
    nc = T // BLK
    xs, bm, cm, dts, v128, dfs, st, ci = _ssd_specs(T, True)
    gn = SSM_GROUPS * D_STATE

    def body(xs_ref, b_ref, c_ref, dt_ref, dtb_ref, alog_ref, d_ref, st_ref, dy_ref,
             dxs_ref, dB_ref, dC_ref, ddt_ref, dal_ref, dD_ref, ddtb_ref, dh_scr):
        i = pl.program_id(0)

        @pl.when(i == 0)
        def _():
            dh_scr[...] = jnp.zeros_like(dh_scr)
            dal_ref[...] = jnp.zeros_like(dal_ref)
            dD_ref[...] = jnp.zeros_like(dD_ref)
            ddtb_ref[...] = jnp.zeros_like(ddtb_ref)

        raw, dt, aneg, acs, acsT = _ssd_chunk_setup(dt_ref, dtb_ref, alog_ref)
        tril = _tri(True)
        lane1 = lax.broadcasted_iota(jnp.int32, (1, 128), 1)
        lane = lax.broadcasted_iota(jnp.int32, (BLK, 128), 1)
        sub = lax.broadcasted_iota(jnp.int32, (BLK, 128), 0)
        ds_col = jnp.zeros((BLK, 128), F32)
        ds_row = jnp.zeros((BLK, 128), F32)
        ddt_col = jnp.zeros((BLK, 128), F32)
        ds_last = jnp.zeros((1, 128), F32)
        dD = jnp.zeros((1, 128), F32)

        def total(v):
            return jnp.sum(jnp.sum(v, axis=1, keepdims=True), axis=0, keepdims=True)

        for g in range(SSM_GROUPS):
            gs = slice(D_STATE * g, D_STATE * (g + 1))
            B = b_ref[:, gs].astype(BF16)
            C = c_ref[:, gs].astype(BF16)
            cb = _dot(C, B, NT)
            dG = jnp.zeros((BLK, BLK), F32)
            dB_g = jnp.zeros((BLK, D_STATE), F32)
            dC_g = jnp.zeros((BLK, D_STATE), F32)
            for hh in range(HEADS_PER_GROUP):
                h = HEADS_PER_GROUP * g + hh
                hs = slice(HEAD_DIM * h, HEAD_DIM * (h + 1))
                dt_h, acs_h, acsT_h = dt[:, h:h + 1], acs[:, h:h + 1], acsT[h:h + 1, :]
                alast = acs[BLK - 1:BLK, h:h + 1]
                x_h = xs_ref[:, hs]
                dy_h = dy_ref[:, hs]
                xd = x_h * dt_h
                decay = jnp.where(tril, jnp.exp(jnp.where(tril, acs_h - acsT_h, 0.0)), 0.0)
                M = cb * decay
                hc = st_ref[h]
                dS = dh_scr[h]
                w = jnp.exp(alast - acs_h)
                gamma = jnp.exp(alast)
                dD = dD + jnp.where(lane1 == h, total(dy_h * x_h), 0.0)
                dye = dy_h * jnp.exp(acs_h)
                dH_y = _dot(dye, C, TN)
                dC_g = dC_g + _dot(dye, hc, NN)
                ds_h = jnp.sum(dye * _dot(C, hc, NT), axis=1, keepdims=True)
                dM = _dot(dy_h, xd, NT)
                dxd = _dot(M, dy_h, TN)
                Q = dM * M
                ds_h = ds_h + jnp.sum(Q, axis=1, keepdims=True)
                ds_row = jnp.where(sub == h, jnp.sum(Q, axis=0, keepdims=True), ds_row)
                dG = dG + dM * decay
                dxdw = _dot(B, dS, NT)
                dxd = dxd + w * dxdw
                dww = jnp.sum(xd * dxdw, axis=1, keepdims=True) * w
                ds_h = ds_h - dww
                ds_last = ds_last + jnp.where(lane1 == h, jnp.sum(dww, axis=0, keepdims=True) + total(dS * hc) * gamma, 0.0)
                dB_g = dB_g + _dot(xd * w, dS, NN)
                dh_scr[h] = gamma * dS + dH_y
                dxs_ref[:, hs] = d_ref[:, hs] * dy_h + dxd * dt_h
                ddt_col = jnp.where(lane == h, jnp.sum(dxd * x_h, axis=1, keepdims=True), ddt_col)
                ds_col = jnp.where(lane == h, ds_h, ds_col)
            dC_ref[:, gs] = dC_g + _dot(dG, B, NN)
            dB_ref[:, gs] = dB_g + _dot(dG, C, TN)
        ds_all = ds_col - ds_row.T + jnp.where(sub == BLK - 1, ds_last, 0.0)
        da = jnp.dot(_tri(False).astype(F32), ds_all, precision=lax.Precision.HIGHEST, preferred_element_type=F32)
        ddt = ddt_col + da * aneg
        draw = jnp.where(lane < SSM_HEADS, ddt * _sigmoid(raw), 0.0)
        ddt_ref[...] = draw.astype(BF16)
        dal_ref[...] += jnp.sum(da * dt, axis=0, keepdims=True) * aneg
        ddtb_ref[...] += jnp.sum(draw, axis=0, keepdims=True)
        dD_ref[...] += dD

    gblk = pl.BlockSpec((BLK, gn), lambda c: (ci(c), 0))
    return _pcall(body, name="ssd_bwd", grid=(nc,), in_specs=[xs, bm, cm, dts, v128, v128, dfs, st, xs],
                  out_specs=[xs, gblk, gblk, pl.BlockSpec((BLK, DT_PAD), lambda c: (ci(c), 0)), v128, v128, v128],
                  out_shape=[_sds((T, D_INNER), F32), _sds((T, gn), F32), _sds((T, gn), F32), _sds((T, DT_PAD), BF16),
                             _sds((1, 128), F32), _sds((1, 128), F32), _sds((1, 128), F32)],
                  scratch_shapes=[pltpu.VMEM((SSM_HEADS, HEAD_DIM, D_STATE), F32)],
                  compiler_params=_cparams(("arbitrary",)))(xbc, xbc, xbc, projp, dtb, alog, dfull, states, dy)


_WIN_ORDER = ("z", "ga", "gs", "xbc", "q", "k", "v", "dt")


def _win_to_padded(win_g):
    full = jnp.transpose(win_g, (1, 0, 2)).reshape(D_MODEL, IN_DIM)
    cols = []
    for nm in _WIN_ORDER:
        s, w = SEG[nm]
        cols.append(full[:, s:s + w])
    cols.append(jnp.zeros((D_MODEL, DT_PAD - SEG["dt"][1]), win_g.dtype))
    return jnp.concatenate(cols, axis=1)


def _padded_to_win(dw):
    off = dict(z=OFF_Z, ga=OFF_GA, gs=OFF_GS, xbc=OFF_XBC, q=OFF_Q, k=OFF_K, v=OFF_V, dt=OFF_DT)
    cols = [dw[:, off[nm]:off[nm] + SEG[nm][1]] for nm in ("q", "k", "v", "z", "xbc", "dt", "ga", "gs")]
    full = jnp.concatenate(cols, axis=1)
    return jnp.transpose(full.reshape(D_MODEL, N_DEV, IN_DIM // N_DEV), (1, 0, 2))


def _pad128(v):
    return jnp.pad(v, ((0, 0), (0, 128 - v.shape[1])))


_SMALL = (("loss", 128, 1), ("g_mix", 2048, 2048), ("conv_b", 3072, 3072), ("dt_bias", 128, 32), ("a_log", 128, 32),
          ("d_skip", 128, 32), ("g_ssd", 2048, 2048), ("sinks", 128, 16), ("g_ffn", 2048, 2048), ("g_ple", 2048, 2048),
          ("g_final", 2048, 2048))


def _small_vec(d):
    parts = []
    for nm, pw, w in _SMALL:
        v = d[nm].reshape(1, -1).astype(F32)
        parts.append(jnp.pad(v[:, :min(v.shape[1], pw)], ((0, 0), (0, pw - min(v.shape[1], pw)))))
    return jnp.concatenate(parts, axis=1)


def _small_split(vec):
    out, o = {}, 0
    for nm, pw, w in _SMALL:
        out[nm] = vec[0, o:o + w]
        o += pw
    return out


def kernel(x, p, positions, g_mix, w_in, conv_w, conv_b, dt_bias, a_log, d_skip, g_ssd, sinks, w_attn_br, w_ssd_br, w_o, g_ffn, w_gate, w_up, w_down, g_ple, w_ple_gate, w_ple_proj, g_final, loss_target, m_g_mix, m_w_in, m_conv_w, m_conv_b, m_dt_bias, m_a_log, m_d_skip, m_g_ssd, m_sinks, m_w_attn_br, m_w_ssd_br, m_w_o, m_g_ffn, m_w_gate, m_w_up, m_w_down, m_g_ple, m_w_ple_gate, m_w_ple_proj, m_g_final, v_g_mix, v_w_in, v_conv_w, v_conv_b, v_dt_bias, v_a_log, v_d_skip, v_g_ssd, v_sinks, v_w_attn_br, v_w_ssd_br, v_w_o, v_g_ffn, v_w_gate, v_w_up, v_w_down, v_g_ple, v_w_ple_gate, v_w_ple_proj, v_g_final):
    T = x.shape[1]
    D = D_MODEL
    W = dict(g_mix=g_mix, w_in=w_in, conv_w=conv_w, conv_b=conv_b, dt_bias=dt_bias, a_log=a_log, d_skip=d_skip, g_ssd=g_ssd,
             sinks=sinks, w_attn_br=w_attn_br, w_ssd_br=w_ssd_br, w_o=w_o, g_ffn=g_ffn, w_gate=w_gate, w_up=w_up, w_down=w_down,
             g_ple=g_ple, w_ple_gate=w_ple_gate, w_ple_proj=w_ple_proj, g_final=g_final)
    Mo = dict(g_mix=m_g_mix, w_in=m_w_in, conv_w=m_conv_w, conv_b=m_conv_b, dt_bias=m_dt_bias, a_log=m_a_log, d_skip=m_d_skip,
              g_ssd=m_g_ssd, sinks=m_sinks, w_attn_br=m_w_attn_br, w_ssd_br=m_w_ssd_br, w_o=m_w_o, g_ffn=m_g_ffn, w_gate=m_w_gate,
              w_up=m_w_up, w_down=m_w_down, g_ple=m_g_ple, w_ple_gate=m_w_ple_gate, w_ple_proj=m_w_ple_proj, g_final=m_g_final)
    Vo = dict(g_mix=v_g_mix, w_in=v_w_in, conv_w=v_conv_w, conv_b=v_conv_b, dt_bias=v_dt_bias, a_log=v_a_log, d_skip=v_d_skip,
              g_ssd=v_g_ssd, sinks=v_sinks, w_attn_br=v_w_attn_br, w_ssd_br=v_w_ssd_br, w_o=v_w_o, g_ffn=v_g_ffn, w_gate=v_w_gate,
              w_up=v_w_up, w_down=v_w_down, g_ple=v_g_ple, w_ple_gate=v_w_ple_gate, w_ple_proj=v_w_ple_proj, g_final=v_g_final)
    order = ["g_mix", "w_in", "conv_w", "conv_b", "dt_bias", "a_log", "d_skip", "g_ssd", "sinks", "w_attn_br", "w_ssd_br", "w_o",
             "g_ffn", "w_gate", "w_up", "w_down", "g_ple", "w_ple_gate", "w_ple_proj", "g_final"]
    big = ["w_in", "conv_w", "w_attn_br", "w_ssd_br", "w_o", "w_gate", "w_up", "w_down", "w_ple_gate", "w_ple_proj"]

    x2 = x.reshape(T, D)
    p2 = p.reshape(T, PLE_DIM)
    tgt = loss_target.reshape(T, D)
    posf = positions.reshape(T, 1).astype(F32)
    inv = ROPE_THETA ** (-np.arange(HEAD_DIM // 2, dtype=np.float32) * 2.0 / HEAD_DIM)
    inv128 = jnp.asarray(np.tile(inv, 128 // (HEAD_DIM // 2)).reshape(1, 128).astype(np.float32))
    sh = {n: W[n].reshape(W[n].shape[-2:]) for n in big}

    del _PENDING[:]
    groups = (("w_in",), ("conv_w", "w_attn_br", "w_ssd_br", "w_o"), ("w_gate", "w_up", "w_down"), ("w_ple_gate", "w_ple_proj"))
    send = {n: sh[n] if n == "conv_w" else sh[n].astype(BF16) for n in big}
    started, prev = [], None
    for gi, grp in enumerate(groups):
        h = split_start("gather_start_%d" % gi, "gather", ICI_SAME_CORE, [send[n] for n in grp], after=prev)
        prev = h["token"]
        started.append(h)
    gathered = {}

    def finish_group(gi, after):
        srcs, lands = split_wait("gather_wait_%d" % gi, started[gi], after)
        full = gather_finish("gather_finish_%d" % gi, srcs, lands)
        gathered.update(zip(groups[gi], full))

    u = rms_fwd("norm_mix", x2, g_mix)
    finish_group(0, u)
    winp = _win_to_padded(gathered["w_in"])
    dtb = _pad128(dt_bias)
    alog = _pad128(a_log)
    dfull = jnp.repeat(d_skip.reshape(SSM_HEADS), HEAD_DIM).reshape(1, D_INNER)

    projp = mm_nn("in_proj", u, winp, 640)
    attn, qr, kr = attn_fwd(projp, posf, inv128, sinks)
    finish_group(1, attn)
    convw = jnp.transpose(gathered["conv_w"], (1, 0, 2)).reshape(CONV_WIDTH, CONV_DIM)
    wab = gathered["w_attn_br"]
    wsb = gathered["w_ssd_br"].reshape(D, D)
    wo = gathered["w_o"].reshape(D, D)
    xbc = conv_fwd(projp, convw, conv_b)
    y, states = ssd_fwd(xbc, projp, dtb, alog, dfull)
    yn = gnorm_fwd(y, projp, g_ssd)
    out_a = mm_nn_colblk("attn_br", attn, wab)
    out_s = mm_nn("ssd_br", yn, wsb, 512)
    merged = merge_fwd(projp, out_a, out_s)
    h1 = mm_nn("o_proj", merged, wo, 512, residual=x2)
    f = rms_fwd("norm_ffn", h1, g_ffn)
    finish_group(2, f)
    wg, wu, wd = gathered["w_gate"], gathered["w_up"], gathered["w_down"]
    gate, up, act = ffn_up(f, wg, wu)
    h2 = ffn_down(act, wd, h1)
    r = rms_fwd("norm_ple", h2, g_ple)
    finish_group(3, r)
    wpg = gathered["w_ple_gate"].reshape(D, D)
    wpp = gathered["w_ple_proj"]
    pg = mm_nn("ple_gate", r, wpg, 512)
    pp = mm_nn_colblk("ple_proj", p2, wpp)
    loss_v, dh3, dpg, dpp, dg_final = head_fwd_bwd(h2, pg, pp, g_final.reshape(1, D), tgt)

    gw = {}
    scat = []

    def scatter_start(names):
        scat.append((names, split_start("scatter_start_%d" % len(scat), "scatter", ALL_PEERS, [gw[n] for n in names])))

    gw["w_ple_proj"] = mm_tn_colblk("dw_ple_proj", p2, dpp, PLE_DIM)
    dr = mm_nt("d_ple_gate", dpg, wpg, 512)
    gw["w_ple_gate"] = mm_tn("dw_ple_gate", r, dpg, 512, 1024).reshape(N_DEV, D // N_DEV, D)
    scatter_start(("w_ple_proj", "w_ple_gate"))
    dh2, dh2b, dg_ple = rms_bwd("norm_ple_bwd", h2, g_ple, dr, dh3)
    dgate, dup = ffn_down_bwd(dh2b, wd, gate, up)
    gw["w_down"] = wgrad_rowblk_lhs("dw_down", act, dh2b, 1024)
    gw["w_gate"] = wgrad_colblk_rhs("dw_gate", f, dgate, 1024)
    gw["w_up"] = wgrad_colblk_rhs("dw_up", f, dup, 1024)
    scatter_start(("w_down", "w_gate", "w_up"))
    df = ffn_up_bwd(dgate, dup, wg, wu)
    dh1, dh1b, dg_ffn = rms_bwd("norm_ffn_bwd", h1, g_ffn, df, dh2)
    dmerged = mm_nt("d_o_proj", dh1b, wo, 512)
    gw["w_o"] = mm_tn("dw_o", merged, dh1b, 512, 1024).reshape(N_DEV, D // N_DEV, D)
    dout_a, dout_s, dga, dgs = merge_bwd(projp, out_a, out_s, dmerged)
    gw["w_ssd_br"] = mm_tn("dw_ssd_br", yn, dout_s, 512, 1024).reshape(N_DEV, D // N_DEV, D)
    gw["w_attn_br"] = mm_tn_colblk("dw_attn_br", attn, dout_a, D // N_DEV)
    scatter_start(("w_o", "w_ssd_br", "w_attn_br"))
    dyn = mm_nt("d_ssd_br", dout_s, wsb, 512)
    dattn = attn_br_bwd(dout_a, wab)
    dy, dz, dg_ssd = gnorm_bwd(y, projp, g_ssd, dyn)
    dxs, dbm, dcm, ddt, dal, ddsk, ddtb = ssd_bwd(xbc, projp, dtb, alog, dfull, states, dy)
    dx_x, dwc_x, dbc_x = conv_bwd("conv_bwd_x", projp, dxs, convw, conv_b, 0)
    dx_b, dwc_b, dbc_b = conv_bwd("conv_bwd_b", projp, dbm, convw, conv_b, D_INNER)
    dx_c, dwc_c, dbc_c = conv_bwd("conv_bwd_c", projp, dcm, convw, conv_b, D_INNER + SSM_GROUPS * D_STATE)
    dq, dk, dv, dsk = attn_bwd(qr, kr, projp, dattn, posf, inv128, sinks)
    dproj = jnp.concatenate([dz, dga, dgs, dx_x, dx_b, dx_c, dq, dk.astype(BF16), dv.astype(BF16), ddt], axis=1)
    gw["w_in"] = _padded_to_win(mm_tn("dw_in", u, dproj, 1024, 640))
    dconvw = jnp.concatenate([dwc_x, dwc_b, dwc_c], axis=1)
    gw["conv_w"] = jnp.transpose(dconvw.reshape(CONV_WIDTH, N_DEV, CONV_DIM // N_DEV), (1, 0, 2))
    scatter_start(("w_in", "conv_w"))
    du = mm_nt_red("d_in_proj", dproj, winp, 1024, 640)
    gx, _, dg_mix = rms_bwd("norm_mix_bwd", x2, g_mix, du, dh1)

    me = 4 * lax.axis_index("x") + 2 * lax.axis_index("y") + lax.axis_index("c")
    res = {}
    after = gx
    for si, (names, h) in enumerate(scat):
        srcs, lands = split_wait("scatter_wait_%d" % si, h, after)
        for n, mine, arrived in zip(names, srcs, lands):
            own = lax.dynamic_index_in_dim(mine, me, 0, keepdims=False)
            res[n] = adamw("adamw_" + n, arrived, sh[n], Mo[n].reshape(sh[n].shape), Vo[n].reshape(sh[n].shape), own=own)
        after = res[names[0]][0]

    small_g = dict(loss=loss_v[:, :1], g_mix=dg_mix, conv_b=jnp.concatenate([dbc_x, dbc_b, dbc_c], axis=1), dt_bias=ddtb,
                   a_log=dal, d_skip=ddsk, g_ssd=dg_ssd, sinks=dsk, g_ffn=dg_ffn, g_ple=dg_ple, g_final=dg_final)
    zero = jnp.zeros((1, 1), F32)
    vec_parts = exchange("gather_small", [_small_vec(small_g)], "gather")[0]
    sres = adamw("adamw_small", vec_parts, _small_vec({**W, "loss": zero}), _small_vec({**Mo, "loss": zero}),
                 _small_vec({**Vo, "loss": zero}))
    ssplit = [_small_split(a) for a in sres]
    loss = ssplit[0]["loss"].reshape(())
    for n in order:
        if n not in res:
            res[n] = tuple(s[n].reshape(W[n].shape) for s in ssplit)
        else:
            res[n] = tuple(a.reshape(W[n].shape) for a in res[n])
    outs = [loss, gx.reshape(x.shape)]
    for k in range(4):
        outs += [res[n][k] for n in order]
    return tuple(outs)
```

```python
import functools

import numpy as np
import jax
import jax.numpy as jnp
from jax import lax
from jax.experimental import pallas as pl
from jax.experimental.pallas import tpu as pltpu

F32 = jnp.float32
BF16 = jnp.bfloat16

N_DEV = 8
D_MODEL = 2048
HEAD_DIM = 64
ATTN_HEADS = 16
KV_HEADS = 4
Q_DIM = 1024
KV_DIM = 256
BLK = 128
D_INNER = 2048
SSM_HEADS = 32
SSM_GROUPS = 4
HEADS_PER_GROUP = 8
D_STATE = 128
CONV_WIDTH = 4
CONV_DIM = 3072
FFN_HIDDEN = 5632
PLE_DIM = 256
IN_DIM = 10784
NORM_EPS = 1e-6
SSM_NORM_EPS = 1e-5
ROPE_THETA = 10000.0

OFF_Z, OFF_GA, OFF_GS, OFF_XBC, OFF_Q, OFF_K, OFF_V, OFF_DT = 0, 2048, 4096, 6144, 9216, 10240, 10496, 10752
IN_PAD = 10880
DT_PAD = 128
SEG = dict(q=(0, 1024), k=(1024, 256), v=(1280, 256), z=(1536, 2048), xbc=(3584, 3072), dt=(6656, 32),
           ga=(6688, 2048), gs=(8736, 2048))

ADAM_LR, ADAM_B1, ADAM_B2, ADAM_EPS, ADAM_WD, ADAM_STEP = 0.001, 0.9, 0.999, 1e-08, 0.01, 10

VMEM_LIMIT = 56 * 1024 * 1024

NN = (((1,), (0,)), ((), ()))
NT = (((1,), (1,)), ((), ()))
TN = (((0,), (0,)), ((), ()))


_PENDING = []


def _raw_call(body, **kw):
    return pl.pallas_call(body, **kw)


def _pcall(body, **kw):
    deps = list(_PENDING)
    del _PENDING[:]
    if not deps:
        return _raw_call(body, **kw)
    n_in = len(kw["in_specs"])

    def tied(*refs):
        return body(*refs[:n_in], *refs[n_in + len(deps):])

    kw["in_specs"] = list(kw["in_specs"]) + [pl.BlockSpec(memory_space=pl.ANY)] * len(deps)
    call = _raw_call(tied, **kw)
    return lambda *ops: call(*ops, *deps)


def _cparams(sem=None):
    if sem is None:
        return pltpu.CompilerParams(vmem_limit_bytes=VMEM_LIMIT)
    return pltpu.CompilerParams(vmem_limit_bytes=VMEM_LIMIT, dimension_semantics=sem)


def _dot(a, b, dn):
    return lax.dot_general(a.astype(BF16), b.astype(BF16), dn, preferred_element_type=F32)


def _sigmoid(x):
    return 1.0 / (1.0 + jnp.exp(-x))


def _silu(x):
    return x * _sigmoid(x)


def _dsilu(x):
    s = _sigmoid(x)
    return s * (1.0 + x * (1.0 - s))


def _matmul(name, pairs, pair_specs, dn, grid, out_shapes, out_specs, nred=1, extra=(), extra_specs=(),
            epilogue=None, acc_shape=None):
    n_in = 2 * len(pairs) + len(extra)
    n_out = len(out_shapes)

    def body(*refs):
        ins = refs[:2 * len(pairs)]
        ex = refs[2 * len(pairs):n_in]
        outs = refs[n_in:n_in + n_out]

        def prod():
            s = None
            for p in range(len(pairs)):
                d = _dot(ins[2 * p][...], ins[2 * p + 1][...], dn)
                s = d if s is None else s + d
            return s

        def finish(val):
            if epilogue is None:
                outs[0][...] = val.astype(outs[0].dtype)
            else:
                res = epilogue(val, *[e[...] for e in ex])
                for o, r in zip(outs, res):
                    o[...] = r.astype(o.dtype)

        if nred == 1:
            finish(prod())
        else:
            acc = refs[n_in + n_out]
            k = pl.program_id(len(grid) - 1)

            @pl.when(k == 0)
            def _():
                acc[...] = jnp.zeros_like(acc)

            acc[...] += prod()

            @pl.when(k == nred - 1)
            def _():
                finish(acc[...])

    operands = []
    specs = []
    for (a, b), (sa, sb) in zip(pairs, pair_specs):
        operands += [a, b]
        specs += [sa, sb]
    operands += list(extra)
    specs += list(extra_specs)
    scratch = [pltpu.VMEM(acc_shape, F32)] if nred > 1 else []
    sem = ("arbitrary",) * len(grid)
    res = _pcall(body, name=name, grid=grid, in_specs=specs, out_specs=list(out_specs),
                 out_shape=list(out_shapes), scratch_shapes=scratch, compiler_params=_cparams(sem))(*operands)
    return res


def _sds(shape, dtype):
    return jax.ShapeDtypeStruct(shape, dtype)


def _row_tile(T):
    return min(1024, T)


def mm_nn(name, a, b, tn, out_dtype=F32, residual=None):
    M, K = a.shape
    N = b.shape[1]
    tm = _row_tile(M)
    grid = (M // tm, N // tn)
    extra, especs, epi = (), (), None
    if residual is not None:
        extra = (residual,)
        especs = (pl.BlockSpec((tm, tn), lambda i, n: (i, n)),)
        epi = lambda v, r: (v + r,)
    return _matmul(name, [(a, b)], [(pl.BlockSpec((tm, K), lambda i, n: (i, 0)), pl.BlockSpec((K, tn), lambda i, n: (0, n)))],
                   NN, grid, [_sds((M, N), out_dtype)], [pl.BlockSpec((tm, tn), lambda i, n: (i, n))],
                   extra=extra, extra_specs=especs, epilogue=epi)[0]


def mm_nn_colblk(name, a, b, out_dtype=F32):
    M, K = a.shape
    J, _, nb = b.shape
    tm = _row_tile(M)
    grid = (M // tm, J)
    return _matmul(name, [(a, b)], [(pl.BlockSpec((tm, K), lambda i, j: (i, 0)), pl.BlockSpec((None, K, nb), lambda i, j: (j, 0, 0)))],
                   NN, grid, [_sds((M, J * nb), out_dtype)], [pl.BlockSpec((tm, nb), lambda i, j: (i, j))])[0]


def mm_nt(name, a, w, tr, out_dtype=F32):
    M, C = a.shape
    R = w.shape[0]
    tm = _row_tile(M)
    grid = (M // tm, R // tr)
    return _matmul(name, [(a, w)], [(pl.BlockSpec((tm, C), lambda i, r: (i, 0)), pl.BlockSpec((tr, C), lambda i, r: (r, 0)))],
                   NT, grid, [_sds((M, R), out_dtype)], [pl.BlockSpec((tm, tr), lambda i, r: (i, r))])[0]


def mm_nt_red(name, a, w, tr, tk, out_dtype=F32):
    M, C = a.shape
    R = w.shape[0]
    tm = _row_tile(M)
    nk = C // tk
    grid = (M // tm, R // tr, nk)
    return _matmul(name, [(a, w)], [(pl.BlockSpec((tm, tk), lambda i, r, k: (i, k)), pl.BlockSpec((tr, tk), lambda i, r, k: (r, k)))],
                   NT, grid, [_sds((M, R), out_dtype)], [pl.BlockSpec((tm, tr), lambda i, r, k: (i, r))],
                   nred=nk, acc_shape=(tm, tr))[0]


def mm_nn_red(name, a, b, tn, tk, out_dtype=F32):
    M, K = a.shape
    N = b.shape[1]
    tm = _row_tile(M)
    nk = K // tk
    grid = (M // tm, N // tn, nk)
    return _matmul(name, [(a, b)], [(pl.BlockSpec((tm, tk), lambda i, n, k: (i, k)), pl.BlockSpec((tk, tn), lambda i, n, k: (k, n)))],
                   NN, grid, [_sds((M, N), out_dtype)], [pl.BlockSpec((tm, tn), lambda i, n, k: (i, n))],
                   nred=nk, acc_shape=(tm, tn))[0]


def mm_tn(name, x, dy, tr, tc, out_dtype=BF16):
    M, R = x.shape
    C = dy.shape[1]
    grid = (R // tr, C // tc)
    return _matmul(name, [(x, dy)], [(pl.BlockSpec((M, tr), lambda r, c: (0, r)), pl.BlockSpec((M, tc), lambda r, c: (0, c)))],
                   TN, grid, [_sds((R, C), out_dtype)], [pl.BlockSpec((tr, tc), lambda r, c: (r, c))])[0]


def mm_tn_colblk(name, x, dy, nb, out_dtype=BF16):
    M, R = x.shape
    J = dy.shape[1] // nb
    grid = (J,)
    return _matmul(name, [(x, dy)], [(pl.BlockSpec((M, R), lambda j: (0, 0)), pl.BlockSpec((M, nb), lambda j: (0, j)))],
                   TN, grid, [_sds((J, R, nb), out_dtype)], [pl.BlockSpec((None, R, nb), lambda j: (j, 0, 0))])[0]


def _rows(T):
    return min(256, T)


def rms_fwd(name, x, g, eps=NORM_EPS):
    T, D = x.shape
    tm = _rows(T)

    def body(x_ref, g_ref, o_ref):
        xv = x_ref[...]
        r = lax.rsqrt(jnp.mean(xv * xv, axis=-1, keepdims=True) + eps)
        o_ref[...] = (xv * r * g_ref[...]).astype(BF16)

    return _pcall(body, name=name, grid=(T // tm,),
                  in_specs=[pl.BlockSpec((tm, D), lambda i: (i, 0)), pl.BlockSpec((1, D), lambda i: (0, 0))],
                  out_specs=pl.BlockSpec((tm, D), lambda i: (i, 0)), out_shape=_sds((T, D), BF16),
                  compiler_params=_cparams(("arbitrary",)))(x, g)


def rms_bwd(name, x, g, dy, dres, eps=NORM_EPS):
    T, D = x.shape
    tm = _rows(T)

    def body(x_ref, g_ref, dy_ref, dr_ref, dx_ref, dxb_ref, dg_ref):
        i = pl.program_id(0)
        xv = x_ref[...]
        r = lax.rsqrt(jnp.mean(xv * xv, axis=-1, keepdims=True) + eps)
        xh = xv * r
        dyv = dy_ref[...]
        gd = dyv * g_ref[...]
        dx = r * (gd - xh * jnp.mean(gd * xh, axis=-1, keepdims=True)) + dr_ref[...]
        dx_ref[...] = dx
        dxb_ref[...] = dx.astype(BF16)

        @pl.when(i == 0)
        def _():
            dg_ref[...] = jnp.zeros_like(dg_ref)

        dg_ref[...] += jnp.sum(dyv * xh, axis=0, keepdims=True)

    row = pl.BlockSpec((tm, D), lambda i: (i, 0))
    vec = pl.BlockSpec((1, D), lambda i: (0, 0))
    return _pcall(body, name=name, grid=(T // tm,), in_specs=[row, vec, row, row], out_specs=[row, row, vec],
                  out_shape=[_sds((T, D), F32), _sds((T, D), BF16), _sds((1, D), F32)],
                  compiler_params=_cparams(("arbitrary",)))(x, g, dy, dres)


def gnorm_fwd(y, projp, g):
    T, D = y.shape
    tm = _rows(T)

    def body(y_ref, z_ref, g_ref, o_ref):
        yz = y_ref[...] * _silu(z_ref[...])
        r = lax.rsqrt(jnp.mean(yz * yz, axis=-1, keepdims=True) + SSM_NORM_EPS)
        o_ref[...] = (yz * r * g_ref[...]).astype(BF16)

    row = pl.BlockSpec((tm, D), lambda i: (i, 0))
    return _pcall(body, name="gnorm_fwd", grid=(T // tm,),
                  in_specs=[row, pl.BlockSpec((tm, D), lambda i: (i, OFF_Z // D)), pl.BlockSpec((1, D), lambda i: (0, 0))],
                  out_specs=row, out_shape=_sds((T, D), BF16), compiler_params=_cparams(("arbitrary",)))(y, projp, g)


def gnorm_bwd(y, projp, g, dyn):
    T, D = y.shape
    tm = _rows(T)

    def body(y_ref, z_ref, g_ref, dyn_ref, dy_ref, dz_ref, dg_ref):
        i = pl.program_id(0)
        yv, zv = y_ref[...], z_ref[...]
        sz = _silu(zv)
        yz = yv * sz
        r = lax.rsqrt(jnp.mean(yz * yz, axis=-1, keepdims=True) + SSM_NORM_EPS)
        xh = yz * r
        dv = dyn_ref[...]
        gd = dv * g_ref[...]
        dyz = r * (gd - xh * jnp.mean(gd * xh, axis=-1, keepdims=True))
        dy_ref[...] = dyz * sz
        dz_ref[...] = (dyz * yv * _dsilu(zv)).astype(BF16)

        @pl.when(i == 0)
        def _():
            dg_ref[...] = jnp.zeros_like(dg_ref)

        dg_ref[...] += jnp.sum(dv * xh, axis=0, keepdims=True)

    row = pl.BlockSpec((tm, D), lambda i: (i, 0))
    vec = pl.BlockSpec((1, D), lambda i: (0, 0))
    return _pcall(body, name="gnorm_bwd", grid=(T // tm,),
                  in_specs=[row, pl.BlockSpec((tm, D), lambda i: (i, OFF_Z // D)), vec, row], out_specs=[row, row, vec],
                  out_shape=[_sds((T, D), F32), _sds((T, D), BF16), _sds((1, D), F32)],
                  compiler_params=_cparams(("arbitrary",)))(y, projp, g, dyn)


def merge_fwd(projp, out_a, out_s):
    T, D = out_a.shape
    tm = _rows(T)

    def body(ga_ref, gs_ref, a_ref, s_ref, o_ref):
        o_ref[...] = (_sigmoid(ga_ref[...]) * a_ref[...] + _sigmoid(gs_ref[...]) * s_ref[...]).astype(BF16)

    row = pl.BlockSpec((tm, D), lambda i: (i, 0))
    return _pcall(body, name="merge_fwd", grid=(T // tm,),
                  in_specs=[pl.BlockSpec((tm, D), lambda i: (i, OFF_GA // D)), pl.BlockSpec((tm, D), lambda i: (i, OFF_GS // D)), row, row],
                  out_specs=row, out_shape=_sds((T, D), BF16), compiler_params=_cparams(("arbitrary",)))(projp, projp, out_a, out_s)


def merge_bwd(projp, out_a, out_s, dmerged):
    T, D = out_a.shape
    tm = _rows(T)

    def body(ga_ref, gs_ref, a_ref, s_ref, dm_ref, da_ref, ds_ref, dga_ref, dgs_ref):
        dm = dm_ref[...]
        sa, ss = _sigmoid(ga_ref[...]), _sigmoid(gs_ref[...])
        da_ref[...] = (dm * sa).astype(BF16)
        ds_ref[...] = (dm * ss).astype(BF16)
        dga_ref[...] = (dm * a_ref[...] * sa * (1.0 - sa)).astype(BF16)
        dgs_ref[...] = (dm * s_ref[...] * ss * (1.0 - ss)).astype(BF16)

    row = pl.BlockSpec((tm, D), lambda i: (i, 0))
    return _pcall(body, name="merge_bwd", grid=(T // tm,),
                  in_specs=[pl.BlockSpec((tm, D), lambda i: (i, OFF_GA // D)), pl.BlockSpec((tm, D), lambda i: (i, OFF_GS // D)), row, row, row],
                  out_specs=[row] * 4, out_shape=[_sds((T, D), BF16)] * 4,
                  compiler_params=_cparams(("arbitrary",)))(projp, projp, out_a, out_s, dmerged)


def head_fwd_bwd(h2, pg, pp, g_final, target):
    T, D = h2.shape
    tm = _rows(T)

    def body(h_ref, pg_ref, pp_ref, g_ref, t_ref, loss_ref, dh_ref, dpg_ref, dpp_ref, dg_ref):
        i = pl.program_id(0)
        s = _sigmoid(pg_ref[...])
        ppv = pp_ref[...]
        h3 = h_ref[...] + s * ppv
        r = lax.rsqrt(jnp.mean(h3 * h3, axis=-1, keepdims=True) + NORM_EPS)
        xh = h3 * r
        gv = g_ref[...]
        e = xh * gv - t_ref[...]
        dyo = e * (1.0 / D)
        gd = dyo * gv
        dh = r * (gd - xh * jnp.mean(gd * xh, axis=-1, keepdims=True))
        dh_ref[...] = dh
        dpg_ref[...] = (dh * ppv * s * (1.0 - s)).astype(BF16)
        dpp_ref[...] = (dh * s).astype(BF16)

        @pl.when(i == 0)
        def _():
            dg_ref[...] = jnp.zeros_like(dg_ref)
            loss_ref[...] = jnp.zeros_like(loss_ref)

        dg_ref[...] += jnp.sum(dyo * xh, axis=0, keepdims=True)
        part = 0.5 * jnp.sum(jnp.mean(e * e, axis=-1, keepdims=True), axis=0, keepdims=True)
        loss_ref[...] += jnp.broadcast_to(part, loss_ref.shape)

    row = pl.BlockSpec((tm, D), lambda i: (i, 0))
    vec = pl.BlockSpec((1, D), lambda i: (0, 0))
    return _pcall(body, name="head_fwd_bwd", grid=(T // tm,), in_specs=[row, row, row, vec, row],
                  out_specs=[pl.BlockSpec((1, 128), lambda i: (0, 0)), row, row, row, vec],
                  out_shape=[_sds((1, 128), F32), _sds((T, D), F32), _sds((T, D), BF16), _sds((T, D), BF16), _sds((1, D), F32)],
                  compiler_params=_cparams(("arbitrary",)))(h2, pg, pp, g_final, target)


def ffn_up(f, wg, wu):
    T, D = f.shape
    J, _, nb = wg.shape
    tm = _row_tile(T)

    def body(f_ref, wg_ref, wu_ref, g_ref, u_ref, a_ref):
        fv = f_ref[...]
        g = _dot(fv, wg_ref[...], NN)
        u = _dot(fv, wu_ref[...], NN)
        g_ref[...] = g
        u_ref[...] = u
        a_ref[...] = (_silu(g) * u).astype(BF16)

    wspec = pl.BlockSpec((None, D, nb), lambda i, j: (j, 0, 0))
    ospec = pl.BlockSpec((None, tm, nb), lambda i, j: (j, i, 0))
    return _pcall(body, name="ffn_up", grid=(T // tm, J), in_specs=[pl.BlockSpec((tm, D), lambda i, j: (i, 0)), wspec, wspec],
                  out_specs=[ospec] * 3, out_shape=[_sds((J, T, nb), F32), _sds((J, T, nb), F32), _sds((J, T, nb), BF16)],
                  compiler_params=_cparams(("arbitrary", "arbitrary")))(f, wg, wu)


def ffn_down(act, wd, h1):
    J, T, nb = act.shape
    D = wd.shape[2]
    tm = _row_tile(T)
    tn = 1024
    grid = (T // tm, D // tn, J)
    return _matmul("ffn_down", [(act, wd)],
                   [(pl.BlockSpec((None, tm, nb), lambda i, n, j: (j, i, 0)), pl.BlockSpec((None, nb, tn), lambda i, n, j: (j, 0, n)))],
                   NN, grid, [_sds((T, D), F32)], [pl.BlockSpec((tm, tn), lambda i, n, j: (i, n))], nred=J, acc_shape=(tm, tn),
                   extra=(h1,), extra_specs=(pl.BlockSpec((tm, tn), lambda i, n, j: (i, n)),), epilogue=lambda v, r: (v + r,))[0]


def ffn_down_bwd(dh2b, wd, gate, up):
    T, D = dh2b.shape
    J, nb, _ = wd.shape
    tm = _row_tile(T)
    ospec = pl.BlockSpec((None, tm, nb), lambda i, j: (j, i, 0))

    def epi(da, g, u):
        return (da * u * _dsilu(g), da * _silu(g))

    return _matmul("ffn_down_bwd", [(dh2b, wd)],
                   [(pl.BlockSpec((tm, D), lambda i, j: (i, 0)), pl.BlockSpec((None, nb, D), lambda i, j: (j, 0, 0)))],
                   NT, (T // tm, J), [_sds((J, T, nb), BF16)] * 2, [ospec, ospec],
                   extra=(gate, up), extra_specs=(ospec, ospec), epilogue=epi)


def ffn_up_bwd(dgate, dup, wg, wu):
    J, T, nb = dgate.shape
    D = wg.shape[1]
    tm = _row_tile(T)
    tr = 1024
    aspec = pl.BlockSpec((None, tm, nb), lambda i, r, j: (j, i, 0))
    wspec = pl.BlockSpec((None, tr, nb), lambda i, r, j: (j, r, 0))
    return _matmul("ffn_up_bwd", [(dgate, wg), (dup, wu)], [(aspec, wspec), (aspec, wspec)], NT, (T // tm, D // tr, J),
                   [_sds((T, D), F32)], [pl.BlockSpec((tm, tr), lambda i, r, j: (i, r))], nred=J, acc_shape=(tm, tr))[0]


def wgrad_rowblk_lhs(name, xb, dy, tc):
    J, T, nb = xb.shape
    C = dy.shape[1]
    return _matmul(name, [(xb, dy)],
                   [(pl.BlockSpec((None, T, nb), lambda j, c: (j, 0, 0)), pl.BlockSpec((T, tc), lambda j, c: (0, c)))],
                   TN, (J, C // tc), [_sds((J, nb, C), BF16)], [pl.BlockSpec((None, nb, tc), lambda j, c: (j, 0, c))])[0]


def wgrad_colblk_rhs(name, x, dyb, tr):
    T, R = x.shape
    J, _, nb = dyb.shape
    return _matmul(name, [(x, dyb)],
                   [(pl.BlockSpec((T, tr), lambda j, r: (0, r)), pl.BlockSpec((None, T, nb), lambda j, r: (j, 0, 0)))],
                   TN, (J, R // tr), [_sds((J, R, nb), BF16)], [pl.BlockSpec((None, tr, nb), lambda j, r: (j, r, 0))])[0]


def attn_br_bwd(dout_a, wab):
    T, D = dout_a.shape
    J, R, nb = wab.shape
    tm = _row_tile(T)
    return _matmul("attn_br_bwd", [(dout_a, wab)],
                   [(pl.BlockSpec((tm, nb), lambda i, j: (i, j)), pl.BlockSpec((None, R, nb), lambda i, j: (j, 0, 0)))],
                   NT, (T // tm, J), [_sds((T, R), BF16)], [pl.BlockSpec((tm, R), lambda i, j: (i, 0))], nred=J, acc_shape=(tm, R))[0]


def _adam_math(w, g, m, v):
    m2 = ADAM_B1 * m + (1.0 - ADAM_B1) * g
    v2 = ADAM_B2 * v + (1.0 - ADAM_B2) * (g * g)
    m_hat = m2 / (1.0 - ADAM_B1 ** ADAM_STEP)
    v_hat = v2 / (1.0 - ADAM_B2 ** ADAM_STEP)
    delta = -ADAM_LR * (m_hat / (jnp.sqrt(v_hat) + ADAM_EPS) + ADAM_WD * w)
    return delta, m2, v2


def _sum_partials(own, parts):
    g = None if own is None else own.astype(F32)
    if parts is not None:
        for s in range(parts.shape[0]):
            t = parts[s].astype(F32)
            g = t if g is None else g + t
    return g


def adamw(name, parts, w, m, v, own=None):
    R, C = w.shape
    tr = R
    for cand in (256, 176, 128, 64, 32, 16, 8):
        if R % cand == 0 and R > cand:
            tr = cand
            break
    given = [a for a in (parts, own) if a is not None]

    def body(*refs):
        p_ref = refs[0] if parts is not None else None
        o_ref = refs[len(given) - 1] if own is not None else None
        w_ref, m_ref, v_ref, g_ref, d_ref, m2_ref, v2_ref = refs[-7:]
        g = _sum_partials(None if o_ref is None else o_ref[...], p_ref)
        d, m2, v2 = _adam_math(w_ref[...], g, m_ref[...], v_ref[...])
        g_ref[...] = g
        d_ref[...] = d
        m2_ref[...] = m2
        v2_ref[...] = v2

    blk = pl.BlockSpec((tr, C), lambda i: (i, 0))
    specs = ([] if parts is None else [pl.BlockSpec((parts.shape[0], tr, C), lambda i: (0, i, 0))]) + [blk] * (3 + (own is not None))
    return _pcall(body, name=name, grid=(R // tr,), in_specs=specs,
                  out_specs=[blk] * 4, out_shape=[_sds((R, C), F32)] * 4,
                  compiler_params=_cparams(("arbitrary",)))(*given, w, m, v)


def sum_partials(name, parts, own, tc):
    R, C = own.shape

    def body(p_ref, o_ref, g_ref):
        g_ref[...] = _sum_partials(o_ref[...], p_ref)

    blk = pl.BlockSpec((R, tc), lambda i: (0, i))
    return _pcall(body, name=name, grid=(C // tc,), in_specs=[pl.BlockSpec((parts.shape[0], R, tc), lambda i: (0, 0, i)), blk],
                  out_specs=blk, out_shape=_sds((R, C), F32), compiler_params=_cparams(("arbitrary",)))(parts, own)


def exchange(name, arrays, mode):
    n = len(arrays)
    out_shapes = []
    for a in arrays:
        shp = a.shape if mode == "scatter" else (N_DEV,) + a.shape
        out_shapes.append(_sds(shp, a.dtype))

    def body(*refs):
        ins, outs = refs[:n], refs[n:2 * n]
        send_sems, recv_sems, local_sems = refs[2 * n:]
        x, y, c = lax.axis_index("x"), lax.axis_index("y"), lax.axis_index("c")
        me = 4 * x + 2 * y + c

        def src(a, dest):
            return ins[a].at[dest] if mode == "scatter" else ins[a]

        local = [pltpu.make_async_copy(src(a, me), outs[a].at[me], local_sems.at[a]) for a in range(n)]
        for cp in local:
            cp.start()
        remote = []
        for k in range(1, N_DEV):
            px = 1 - x if k & 4 else x
            py = 1 - y if k & 2 else y
            pc = 1 - c if k & 1 else c
            peer = 4 * px + 2 * py + pc
            for a in range(n):
                cp = pltpu.make_async_remote_copy(src_ref=src(a, peer), dst_ref=outs[a].at[me],
                                                  send_sem=send_sems.at[a * 7 + k - 1], recv_sem=recv_sems.at[a * 7 + k - 1],
                                                  device_id=(px, py, pc), device_id_type=pl.DeviceIdType.MESH)
                cp.start()
                arrival = pltpu.make_async_remote_copy(src_ref=src(a, peer), dst_ref=outs[a].at[peer],
                                                       send_sem=send_sems.at[a * 7 + k - 1], recv_sem=recv_sems.at[a * 7 + k - 1],
                                                       device_id=(px, py, pc), device_id_type=pl.DeviceIdType.MESH)
                remote.append((cp, arrival))
        for cp, arrival in remote:
            cp.wait_send()
            arrival.wait_recv()
        for cp in local:
            cp.wait()

    any_spec = pl.BlockSpec(memory_space=pl.ANY)
    return _pcall(body, name=name, in_specs=[any_spec] * n, out_specs=[any_spec] * n, out_shape=out_shapes,
                  scratch_shapes=[pltpu.SemaphoreType.DMA((7 * n,)), pltpu.SemaphoreType.DMA((7 * n,)), pltpu.SemaphoreType.DMA((n,))],
                  compiler_params=pltpu.CompilerParams(has_side_effects=True))(*arrays)


_HBM = pl.BlockSpec(memory_space=pltpu.HBM)
_SEM = pl.BlockSpec(memory_space=pltpu.SEMAPHORE)
_ANY = pl.BlockSpec(memory_space=pl.ANY)
_SPLIT_PARAMS = dict(compiler_params=pltpu.CompilerParams(has_side_effects=pltpu.SideEffectType.DATAFLOW_SIDE_EFFECTING))
ICI_SAME_CORE = (2, 4, 6)
ALL_PEERS = (1, 2, 3, 4, 5, 6, 7)


def _mesh_pos():
    x, y, c = lax.axis_index("x"), lax.axis_index("y"), lax.axis_index("c")
    return x, y, c, 4 * x + 2 * y + c


def _peer_of(k, x, y, c):
    px = 1 - x if k & 4 else x
    py = 1 - y if k & 2 else y
    pc = 1 - c if k & 1 else c
    return (px, py, pc), 4 * px + 2 * py + pc


def _split_copies(mode, ks, srcs, lands, send_sems, recv_sems):
    x, y, c, me = _mesh_pos()
    pairs = []
    for a in range(len(srcs)):
        for j, k in enumerate(ks):
            dev, peer = _peer_of(k, x, y, c)
            i = a * len(ks) + j
            if mode == "gather":
                s_out, d_out, d_in = srcs[a], lands[a].at[me], lands[a].at[peer]
            else:
                s_out, d_out, d_in = srcs[a].at[peer], lands[a].at[k - 1], lands[a].at[k - 1]
            both = [pltpu.make_async_remote_copy(src_ref=s_out, dst_ref=d, send_sem=send_sems.at[i], recv_sem=recv_sems.at[i],
                                                 device_id=dev, device_id_type=pl.DeviceIdType.MESH) for d in (d_out, d_in)]
            pairs.append(tuple(both))
    return pairs


def split_start(name, mode, ks, srcs, after=None):
    n, nk = len(srcs), len(ks)
    srcs = [pltpu.with_memory_space_constraint(s, pltpu.HBM) for s in srcs]
    shapes = [((N_DEV,) + s.shape) if mode == "gather" else ((N_DEV - 1,) + s.shape[1:]) for s in srcs]
    lands = [pltpu.with_memory_space_constraint(lax.empty(shp, s.dtype), pltpu.HBM) for shp, s in zip(shapes, srcs)]
    extra = [] if after is None else [after]

    def body(*refs):
        src_refs, land_refs = refs[:n], refs[n:2 * n]
        send_sems, recv_sems = refs[2 * n + len(extra)], refs[2 * n + len(extra) + 1]
        token = refs[-1]
        for out, _ in _split_copies(mode, ks, src_refs, land_refs, send_sems, recv_sems):
            out.start()
        token[...] = jnp.zeros_like(token)

    out_shape = (pltpu.SemaphoreType.DMA((n * nk,)), pltpu.SemaphoreType.DMA((n * nk,)),
                 *[pltpu.HBM(s.shape, s.dtype) for s in srcs], *[pltpu.HBM(shp, s.dtype) for shp, s in zip(shapes, srcs)],
                 _sds((8, 128), F32))
    res = _raw_call(body, name=name, out_shape=out_shape, in_specs=[_HBM] * (2 * n) + [_ANY] * len(extra),
                    out_specs=(_SEM, _SEM, *[_HBM] * (2 * n), pl.BlockSpec(memory_space=pltpu.VMEM)),
                    input_output_aliases={i: 2 + i for i in range(2 * n)}, **_SPLIT_PARAMS)(*srcs, *lands, *extra)
    _PENDING.append(res[-1])
    return dict(mode=mode, ks=ks, sems=(res[0], res[1]), srcs=list(res[2:2 + n]), lands=list(res[2 + n:2 + 2 * n]), token=res[-1])


def split_wait(name, h, after):
    n = len(h["srcs"])

    def body(*refs):
        src_refs, land_refs = refs[:n], refs[n:2 * n]
        send_sems, recv_sems = refs[2 * n], refs[2 * n + 1]
        for out, arriving in _split_copies(h["mode"], h["ks"], src_refs, land_refs, send_sems, recv_sems):
            out.wait_send()
            arriving.wait_recv()

    both = h["srcs"] + h["lands"]
    res = _raw_call(body, name=name, out_shape=tuple(pltpu.HBM(a.shape, a.dtype) for a in both),
                    in_specs=[_HBM] * (2 * n) + [_SEM, _SEM, _ANY], out_specs=tuple([_HBM] * (2 * n)),
                    input_output_aliases={i: i for i in range(2 * n)}, **_SPLIT_PARAMS)(*both, *h["sems"], after)
    return list(res[:n]), list(res[n:])


D2D_STREAMS = 8


def _stream_slices(shape):
    rows, cols = shape
    if rows % (16 * D2D_STREAMS) == 0:
        step = rows // D2D_STREAMS
        return [(pl.ds(i * step, step), slice(None)) for i in range(D2D_STREAMS)]
    if cols % (128 * D2D_STREAMS) == 0:
        step = cols // D2D_STREAMS
        return [(slice(None), pl.ds(i * step, step)) for i in range(D2D_STREAMS)]
    return [(slice(None), slice(None))]


def gather_finish(name, shards, lands):
    n = len(shards)
    cuts = [_stream_slices(s.shape) for s in shards]
    base = [0]
    for cs in cuts:
        base.append(base[-1] + 4 * len(cs))

    def body(*refs):
        sh, land = refs[:n], refs[2 * n:3 * n]
        send_sems, recv_sems, local_sems = refs[3 * n:]
        x, y, c, me = _mesh_pos()
        sib_dev, sib = _peer_of(1, x, y, c)
        local = [pltpu.make_async_copy(sh[a], land[a].at[me], local_sems.at[a]) for a in range(n)]
        for cp in local:
            cp.start()
        pairs = []
        for a in range(n):
            for j, k in enumerate((0,) + ICI_SAME_CORE):
                _, mine = _peer_of(k, x, y, c)
                _, theirs = _peer_of(k | 1, x, y, c)
                for q, cut in enumerate(cuts[a]):
                    i = base[a] + j * len(cuts[a]) + q
                    src = sh[a].at[cut] if k == 0 else land[a].at[mine].at[cut]
                    both = [pltpu.make_async_remote_copy(src_ref=src, dst_ref=d.at[cut], send_sem=send_sems.at[i],
                                                         recv_sem=recv_sems.at[i], device_id=sib_dev,
                                                         device_id_type=pl.DeviceIdType.MESH)
                            for d in (land[a].at[mine], land[a].at[theirs])]
                    both[0].start()
                    pairs.append(both)
        for out, arriving in pairs:
            out.wait_send()
            arriving.wait_recv()
        for cp in local:
            cp.wait()

    return _pcall(body, name=name, in_specs=[_ANY] * (2 * n), out_specs=[_ANY] * n,
                  out_shape=[_sds(l.shape, l.dtype) for l in lands], input_output_aliases={n + a: a for a in range(n)},
                  scratch_shapes=[pltpu.SemaphoreType.DMA((base[-1],)), pltpu.SemaphoreType.DMA((base[-1],)),
                                  pltpu.SemaphoreType.DMA((n,))])(*shards, *lands)


def _rope_parts(pos_ref, inv_ref):
    ang = pos_ref[...] * inv_ref[...]
    return jnp.cos(ang), jnp.sin(ang)


def _rot_half(t):
    lane = lax.broadcasted_iota(jnp.int32, t.shape, 1)
    return jnp.where((lane % HEAD_DIM) < HEAD_DIM // 2, -pltpu.roll(t, 128 - HEAD_DIM // 2, 1), pltpu.roll(t, HEAD_DIM // 2, 1))


def _attn_mask(n):
    row = lax.broadcasted_iota(jnp.int32, (BLK, 2 * BLK), 0)
    col = lax.broadcasted_iota(jnp.int32, (BLK, 2 * BLK), 1)
    dist = row + BLK - col
    return (dist >= 0) & (dist < BLK) & ((n * BLK - BLK + col) >= 0)


def _attn_specs(T):
    prev = lambda n: jnp.maximum(n - 1, 0)
    kc = pl.BlockSpec((BLK, KV_DIM), lambda n: (n, OFF_K // KV_DIM))
    kp = pl.BlockSpec((BLK, KV_DIM), lambda n: (prev(n), OFF_K // KV_DIM))
    vc = pl.BlockSpec((BLK, KV_DIM), lambda n: (n, OFF_V // KV_DIM))
    vp = pl.BlockSpec((BLK, KV_DIM), lambda n: (prev(n), OFF_V // KV_DIM))
    pc = pl.BlockSpec((BLK, 1), lambda n: (n, 0))
    pp = pl.BlockSpec((BLK, 1), lambda n: (prev(n), 0))
    inv = pl.BlockSpec((1, 128), lambda n: (0, 0))
    sink = pl.BlockSpec(memory_space=pltpu.SMEM)
    return kc, kp, vc, vp, pc, pp, inv, sink


def _softmax_sink(sc, valid, sink):
    sc = jnp.where(valid, sc * (HEAD_DIM ** -0.5), -1e30)
    m = jnp.maximum(jnp.max(sc, axis=1, keepdims=True), sink)
    e = jnp.exp(sc - m)
    es = jnp.exp(sink - m)
    den = jnp.sum(e, axis=1, keepdims=True) + es
    return e / den, es / den


def attn_fwd(projp, posf, inv128, sinks):
    T = projp.shape[0]
    kc, kp, vc, vp, pc, pp, inv, sink = _attn_specs(T)

    def body(q_ref, kc_ref, kp_ref, vc_ref, vp_ref, pc_ref, pp_ref, inv_ref, sink_ref, o_ref, qr_ref, kr_ref):
        n = pl.program_id(0)
        cos_c, sin_c = _rope_parts(pc_ref, inv_ref)
        cos_p, sin_p = _rope_parts(pp_ref, inv_ref)
        valid = _attn_mask(n)
        k_c, k_p = [], []
        for s in range(KV_DIM // 128):
            t = kc_ref[:, 128 * s:128 * (s + 1)]
            k_c.append((t * cos_c + _rot_half(t) * sin_c).astype(BF16))
            kr_ref[:, 128 * s:128 * (s + 1)] = k_c[s]
            t = kp_ref[:, 128 * s:128 * (s + 1)]
            k_p.append((t * cos_p + _rot_half(t) * sin_p).astype(BF16))
        for s in range(Q_DIM // 128):
            t = q_ref[:, 128 * s:128 * (s + 1)]
            qs = (t * cos_c + _rot_half(t) * sin_c).astype(BF16)
            qr_ref[:, 128 * s:128 * (s + 1)] = qs
            for e in range(2):
                hq = 2 * s + e
                hk = hq // (ATTN_HEADS // KV_HEADS)
                lo = HEAD_DIM * (hk % 2)
                kcat = jnp.concatenate([k_p[hk // 2][:, lo:lo + HEAD_DIM], k_c[hk // 2][:, lo:lo + HEAD_DIM]], axis=0)
                vcat = jnp.concatenate([vp_ref[:, HEAD_DIM * hk:HEAD_DIM * (hk + 1)], vc_ref[:, HEAD_DIM * hk:HEAD_DIM * (hk + 1)]], axis=0)
                sc = _dot(qs[:, HEAD_DIM * e:HEAD_DIM * (e + 1)], kcat, NT)
                p, _ = _softmax_sink(sc, valid, sink_ref[0, hq])
                o_ref[:, HEAD_DIM * hq:HEAD_DIM * (hq + 1)] = _dot(p, vcat, NN).astype(BF16)

    qspec = pl.BlockSpec((BLK, Q_DIM), lambda n: (n, OFF_Q // Q_DIM))
    orow = pl.BlockSpec((BLK, Q_DIM), lambda n: (n, 0))
    krow = pl.BlockSpec((BLK, KV_DIM), lambda n: (n, 0))
    return _pcall(body, name="attn_fwd", grid=(T // BLK,), in_specs=[qspec, kc, kp, vc, vp, pc, pp, inv, sink],
                  out_specs=[orow, orow, krow], out_shape=[_sds((T, Q_DIM), BF16), _sds((T, Q_DIM), BF16), _sds((T, KV_DIM), BF16)],
                  compiler_params=_cparams(("arbitrary",)))(projp, projp, projp, projp, projp, posf, posf, inv128, sinks)


def attn_bwd(qr, kr, projp, dattn, posf, inv128, sinks):
    T = projp.shape[0]
    _, _, vc, vp, pc, pp, inv, sink = _attn_specs(T)
    G = ATTN_HEADS // KV_HEADS

    def body(qr_ref, krc_ref, krp_ref, vc_ref, vp_ref, do_ref, pc_ref, pp_ref, inv_ref, sink_ref, dq_ref, dk_ref, dv_ref, dsk_ref):
        n = pl.program_id(0)

        @pl.when(n == 0)
        def _():
            dk_ref[...] = jnp.zeros_like(dk_ref)
            dv_ref[...] = jnp.zeros_like(dv_ref)
            dsk_ref[...] = jnp.zeros_like(dsk_ref)

        cos_c, sin_c = _rope_parts(pc_ref, inv_ref)
        cos_p, sin_p = _rope_parts(pp_ref, inv_ref)
        valid = _attn_mask(n)
        lane = lax.broadcasted_iota(jnp.int32, (1, 128), 1)
        dsk = jnp.zeros((1, 128), F32)
        dq_heads, dk_heads, dv_heads = [], [], []
        for hk in range(KV_HEADS):
            ksl = slice(HEAD_DIM * hk, HEAD_DIM * (hk + 1))
            kcat = jnp.concatenate([krp_ref[:, ksl], krc_ref[:, ksl]], axis=0)
            vcat = jnp.concatenate([vp_ref[:, ksl], vc_ref[:, ksl]], axis=0)
            dkcat = jnp.zeros((2 * BLK, HEAD_DIM), F32)
            dvcat = jnp.zeros((2 * BLK, HEAD_DIM), F32)
            for g in range(G):
                hq = G * hk + g
                qsl = slice(HEAD_DIM * hq, HEAD_DIM * (hq + 1))
                q_h = qr_ref[:, qsl]
                do_h = do_ref[:, qsl]
                p, psink = _softmax_sink(_dot(q_h, kcat, NT), valid, sink_ref[0, hq])
                dp = _dot(do_h, vcat, NT)
                delta = jnp.sum(p * dp, axis=1, keepdims=True)
                ds = p * (dp - delta) * (HEAD_DIM ** -0.5)
                dsk = dsk + jnp.where(lane == hq, -jnp.sum(psink * delta, axis=0, keepdims=True), 0.0)
                dq_heads.append(_dot(ds, kcat, NN))
                dkcat = dkcat + _dot(ds, q_h, TN)
                dvcat = dvcat + _dot(p, do_h, TN)
            dk_heads.append(dkcat)
            dv_heads.append(dvcat)
        dsk_ref[...] += dsk
        for s in range(Q_DIM // 128):
            t = jnp.concatenate([dq_heads[2 * s], dq_heads[2 * s + 1]], axis=1)
            dq_ref[:, 128 * s:128 * (s + 1)] = (t * cos_c - _rot_half(t) * sin_c).astype(BF16)
        cur = pl.ds(pl.multiple_of(n * BLK, BLK), BLK)
        prv = pl.ds(pl.multiple_of(jnp.maximum(n - 1, 0) * BLK, BLK), BLK)
        for s in range(KV_DIM // 128):
            tc = jnp.concatenate([dk_heads[2 * s][BLK:], dk_heads[2 * s + 1][BLK:]], axis=1)
            tp = jnp.concatenate([dk_heads[2 * s][:BLK], dk_heads[2 * s + 1][:BLK]], axis=1)
            cols = slice(128 * s, 128 * (s + 1))
            dk_ref[cur, cols] += tc * cos_c - _rot_half(tc) * sin_c
            dk_ref[prv, cols] += tp * cos_p - _rot_half(tp) * sin_p
            dv_ref[cur, cols] += jnp.concatenate([dv_heads[2 * s][BLK:], dv_heads[2 * s + 1][BLK:]], axis=1)
            dv_ref[prv, cols] += jnp.concatenate([dv_heads[2 * s][:BLK], dv_heads[2 * s + 1][:BLK]], axis=1)

    qrow = pl.BlockSpec((BLK, Q_DIM), lambda n: (n, 0))
    krc = pl.BlockSpec((BLK, KV_DIM), lambda n: (n, 0))
    krp = pl.BlockSpec((BLK, KV_DIM), lambda n: (jnp.maximum(n - 1, 0), 0))
    whole = pl.BlockSpec((T, KV_DIM), lambda n: (0, 0))
    return _pcall(body, name="attn_bwd", grid=(T // BLK,), in_specs=[qrow, krc, krp, vc, vp, qrow, pc, pp, inv, sink],
                  out_specs=[qrow, whole, whole, pl.BlockSpec((1, 128), lambda n: (0, 0))],
                  out_shape=[_sds((T, Q_DIM), BF16), _sds((T, KV_DIM), F32), _sds((T, KV_DIM), F32), _sds((1, 128), F32)],
                  compiler_params=_cparams(("arbitrary",)))(qr, kr, kr, projp, projp, dattn, posf, posf, inv128, sinks)


CONV_CB = 256


def _shift_down(x, s):
    row = lax.broadcasted_iota(jnp.int32, x.shape, 0)
    return jnp.where(row >= s, pltpu.roll(x, s, 0), 0.0)


def _shift_up(x, s):
    T = x.shape[0]
    row = lax.broadcasted_iota(jnp.int32, x.shape, 0)
    return jnp.where(row < T - s, pltpu.roll(x, T - s, 0), 0.0)


def _conv_pre(x, w_ref, b_ref):
    acc = x * w_ref[CONV_WIDTH - 1:CONV_WIDTH, :] + b_ref[...]
    for s in range(1, CONV_WIDTH):
        acc = acc + _shift_down(x, s) * w_ref[CONV_WIDTH - 1 - s:CONV_WIDTH - s, :]
    return acc


def conv_fwd(projp, conv_w, conv_b):
    T = projp.shape[0]

    def body(x_ref, w_ref, b_ref, o_ref):
        o_ref[...] = _silu(_conv_pre(x_ref[...], w_ref, b_ref))

    return _pcall(body, name="conv_fwd", grid=(CONV_DIM // CONV_CB,),
                  in_specs=[pl.BlockSpec((T, CONV_CB), lambda c: (0, OFF_XBC // CONV_CB + c)),
                            pl.BlockSpec((CONV_WIDTH, CONV_CB), lambda c: (0, c)), pl.BlockSpec((1, CONV_CB), lambda c: (0, c))],
                  out_specs=pl.BlockSpec((T, CONV_CB), lambda c: (0, c)), out_shape=_sds((T, CONV_DIM), F32),
                  compiler_params=_cparams(("arbitrary",)))(projp, conv_w, conv_b)


def conv_bwd(name, projp, dact, conv_w, conv_b, col0):
    T, C = dact.shape
    c0 = col0 // CONV_CB

    def body(x_ref, da_ref, w_ref, b_ref, dx_ref, dw_ref, db_ref):
        x = x_ref[...]
        dpre = da_ref[...] * _dsilu(_conv_pre(x, w_ref, b_ref))
        dx = dpre * w_ref[CONV_WIDTH - 1:CONV_WIDTH, :]
        dw_ref[CONV_WIDTH - 1:CONV_WIDTH, :] = jnp.sum(dpre * x, axis=0, keepdims=True)
        for s in range(1, CONV_WIDTH):
            i = CONV_WIDTH - 1 - s
            dx = dx + _shift_up(dpre, s) * w_ref[i:i + 1, :]
            dw_ref[i:i + 1, :] = jnp.sum(dpre * _shift_down(x, s), axis=0, keepdims=True)
        dx_ref[...] = dx.astype(BF16)
        db_ref[...] = jnp.sum(dpre, axis=0, keepdims=True)

    return _pcall(body, name=name, grid=(C // CONV_CB,),
                  in_specs=[pl.BlockSpec((T, CONV_CB), lambda c: (0, OFF_XBC // CONV_CB + c0 + c)),
                            pl.BlockSpec((T, CONV_CB), lambda c: (0, c)),
                            pl.BlockSpec((CONV_WIDTH, CONV_CB), lambda c: (0, c0 + c)), pl.BlockSpec((1, CONV_CB), lambda c: (0, c0 + c))],
                  out_specs=[pl.BlockSpec((T, CONV_CB), lambda c: (0, c)), pl.BlockSpec((CONV_WIDTH, CONV_CB), lambda c: (0, c)),
                             pl.BlockSpec((1, CONV_CB), lambda c: (0, c))],
                  out_shape=[_sds((T, C), BF16), _sds((CONV_WIDTH, C), F32), _sds((1, C), F32)],
                  compiler_params=_cparams(("arbitrary",)))(projp, dact, conv_w, conv_b)


def _softplus(x):
    return jnp.maximum(x, 0.0) + jnp.log1p(jnp.exp(-jnp.abs(x)))


def _tri(lower):
    r = lax.broadcasted_iota(jnp.int32, (BLK, BLK), 0)
    c = lax.broadcasted_iota(jnp.int32, (BLK, BLK), 1)
    return (r >= c) if lower else (c >= r)


def _ssd_chunk_setup(dt_ref, dtb_ref, alog_ref):
    raw = dt_ref[...] + dtb_ref[...]
    dt = _softplus(raw)
    aneg = -jnp.exp(alog_ref[...])
    a = dt * aneg
    acs = jnp.dot(_tri(True).astype(F32), a, precision=lax.Precision.HIGHEST, preferred_element_type=F32)
    return raw, dt, aneg, acs, acs.T


def _ssd_specs(T, rev):
    nc = T // BLK
    ci = (lambda c: nc - 1 - c) if rev else (lambda c: c)
    xs = pl.BlockSpec((BLK, D_INNER), lambda c: (ci(c), 0))
    bm = pl.BlockSpec((BLK, SSM_GROUPS * D_STATE), lambda c: (ci(c), D_INNER // (SSM_GROUPS * D_STATE)))
    cm = pl.BlockSpec((BLK, SSM_GROUPS * D_STATE), lambda c: (ci(c), D_INNER // (SSM_GROUPS * D_STATE) + 1))
    dt = pl.BlockSpec((BLK, DT_PAD), lambda c: (ci(c), OFF_DT // DT_PAD))
    v128 = pl.BlockSpec((1, 128), lambda c: (0, 0))
    dfull = pl.BlockSpec((1, D_INNER), lambda c: (0, 0))
    st = pl.BlockSpec((None, SSM_HEADS, HEAD_DIM, D_STATE), lambda c: (ci(c), 0, 0, 0))
    return xs, bm, cm, dt, v128, dfull, st, ci


def ssd_fwd(xbc, projp, dtb, alog, dfull):
    T = xbc.shape[0]
    nc = T // BLK
    xs, bm, cm, dts, v128, dfs, st, _ = _ssd_specs(T, False)

    def body(xs_ref, b_ref, c_ref, dt_ref, dtb_ref, alog_ref, d_ref, y_ref, st_ref, h_scr):
        c = pl.program_id(0)

        @pl.when(c == 0)
        def _():
            h_scr[...] = jnp.zeros_like(h_scr)

        _, dt, _, acs, acsT = _ssd_chunk_setup(dt_ref, dtb_ref, alog_ref)
        tril = _tri(True)
        for g in range(SSM_GROUPS):
            B = b_ref[:, D_STATE * g:D_STATE * (g + 1)].astype(BF16)
            C = c_ref[:, D_STATE * g:D_STATE * (g + 1)].astype(BF16)
            cb = _dot(C, B, NT)
            for hh in range(HEADS_PER_GROUP):
                h = HEADS_PER_GROUP * g + hh
                hs = slice(HEAD_DIM * h, HEAD_DIM * (h + 1))
                dt_h, acs_h, acsT_h = dt[:, h:h + 1], acs[:, h:h + 1], acsT[h:h + 1, :]
                alast = acs[BLK - 1:BLK, h:h + 1]
                x_h = xs_ref[:, hs]
                xd = x_h * dt_h
                decay = jnp.where(tril, jnp.exp(jnp.where(tril, acs_h - acsT_h, 0.0)), 0.0)
                y = _dot(cb * decay, xd, NN)
                hp = h_scr[h]
                st_ref[h] = hp
                y = y + jnp.exp(acs_h) * _dot(C, hp, NT)
                h_scr[h] = jnp.exp(alast) * hp + _dot(xd * jnp.exp(alast - acs_h), B, TN)
                y_ref[:, hs] = y + d_ref[:, hs] * x_h

    return _pcall(body, name="ssd_fwd", grid=(nc,), in_specs=[xs, bm, cm, dts, v128, v128, dfs],
                  out_specs=[xs, st], out_shape=[_sds((T, D_INNER), F32), _sds((nc, SSM_HEADS, HEAD_DIM, D_STATE), F32)],
                  scratch_shapes=[pltpu.VMEM((SSM_HEADS, HEAD_DIM, D_STATE), F32)],
                  compiler_params=_cparams(("arbitrary",)))(xbc, xbc, xbc, projp, dtb, alog, dfull)


def ssd_bwd(xbc, projp, dtb, alog, dfull, states, dy):
    T = xbc.shape[0]
    nc = T // BLK
    xs, bm, cm, dts, v128, dfs, st, ci = _ssd_specs(T, True)
    gn = SSM_GROUPS * D_STATE

    def body(xs_ref, b_ref, c_ref, dt_ref, dtb_ref, alog_ref, d_ref, st_ref, dy_ref,
             dxs_ref, dB_ref, dC_ref, ddt_ref, dal_ref, dD_ref, ddtb_ref, dh_scr):
        i = pl.program_id(0)

        @pl.when(i == 0)
        def _():
            dh_scr[...] = jnp.zeros_like(dh_scr)
            dal_ref[...] = jnp.zeros_like(dal_ref)
            dD_ref[...] = jnp.zeros_like(dD_ref)
            ddtb_ref[...] = jnp.zeros_like(ddtb_ref)

        raw, dt, aneg, acs, acsT = _ssd_chunk_setup(dt_ref, dtb_ref, alog_ref)
        tril = _tri(True)
        lane1 = lax.broadcasted_iota(jnp.int32, (1, 128), 1)
        lane = lax.broadcasted_iota(jnp.int32, (BLK, 128), 1)
        sub = lax.broadcasted_iota(jnp.int32, (BLK, 128), 0)
        ds_col = jnp.zeros((BLK, 128), F32)
        ds_row = jnp.zeros((BLK, 128), F32)
        ddt_col = jnp.zeros((BLK, 128), F32)
        ds_last = jnp.zeros((1, 128), F32)
        dD = jnp.zeros((1, 128), F32)

        def total(v):
            return jnp.sum(jnp.sum(v, axis=1, keepdims=True), axis=0, keepdims=True)

        for g in range(SSM_GROUPS):
            gs = slice(D_STATE * g, D_STATE * (g + 1))
            B = b_ref[:, gs].astype(BF16)
            C = c_ref[:, gs].astype(BF16)
            cb = _dot(C, B, NT)
            dG = jnp.zeros((BLK, BLK), F32)
            dB_g = jnp.zeros((BLK, D_STATE), F32)
            dC_g = jnp.zeros((BLK, D_STATE), F32)
            for hh in range(HEADS_PER_GROUP):
                h = HEADS_PER_GROUP * g + hh
                hs = slice(HEAD_DIM * h, HEAD_DIM * (h + 1))
                dt_h, acs_h, acsT_h = dt[:, h:h + 1], acs[:, h:h + 1], acsT[h:h + 1, :]
                alast = acs[BLK - 1:BLK, h:h + 1]
                x_h = xs_ref[:, hs]
                dy_h = dy_ref[:, hs]
                xd = x_h * dt_h
                decay = jnp.where(tril, jnp.exp(jnp.where(tril, acs_h - acsT_h, 0.0)), 0.0)
                M = cb * decay
                hc = st_ref[h]
                dS = dh_scr[h]
                w = jnp.exp(alast - acs_h)
                gamma = jnp.exp(alast)
                dD = dD + jnp.where(lane1 == h, total(dy_h * x_h), 0.0)
                dye = dy_h * jnp.exp(acs_h)
                dH_y = _dot(dye, C, TN)
                dC_g = dC_g + _dot(dye, hc, NN)
                ds_h = jnp.sum(dye * _dot(C, hc, NT), axis=1, keepdims=True)
                dM = _dot(dy_h, xd, NT)
                dxd = _dot(M, dy_h, TN)
                Q = dM * M
                ds_h = ds_h + jnp.sum(Q, axis=1, keepdims=True)
                ds_row = jnp.where(sub == h, jnp.sum(Q, axis=0, keepdims=True), ds_row)
                dG = dG + dM * decay
                dxdw = _dot(B, dS, NT)
                dxd = dxd + w * dxdw
                dww = jnp.sum(xd * dxdw, axis=1, keepdims=True) * w
                ds_h = ds_h - dww
                ds_last = ds_last + jnp.where(lane1 == h, jnp.sum(dww, axis=0, keepdims=True) + total(dS * hc) * gamma, 0.0)
                dB_g = dB_g + _dot(xd * w, dS, NN)
                dh_scr[h] = gamma * dS + dH_y
                dxs_ref[:, hs] = d_ref[:, hs] * dy_h + dxd * dt_h
                ddt_col = jnp.where(lane == h, jnp.sum(dxd * x_h, axis=1, keepdims=True), ddt_col)
                ds_col = jnp.where(lane == h, ds_h, ds_col)
            dC_ref[:, gs] = dC_g + _dot(dG, B, NN)
            dB_ref[:, gs] = dB_g + _dot(dG, C, TN)
        ds_all = ds_col - ds_row.T + jnp.where(sub == BLK - 1, ds_last, 0.0)
        da = jnp.dot(_tri(False).astype(F32), ds_all, precision=lax.Precision.HIGHEST, preferred_element_type=F32)
        ddt = ddt_col + da * aneg
        draw = jnp.where(lane < SSM_HEADS, ddt * _sigmoid(raw), 0.0)
        ddt_ref[...] = draw.astype(BF16)
        dal_ref[...] += jnp.sum(da * dt, axis=0, keepdims=True) * aneg
        ddtb_ref[...] += jnp.sum(draw, axis=0, keepdims=True)
        dD_ref[...] += dD

    gblk = pl.BlockSpec((BLK, gn), lambda c: (ci(c), 0))
    return _pcall(body, name="ssd_bwd", grid=(nc,), in_specs=[xs, bm, cm, dts, v128, v128, dfs, st, xs],
                  out_specs=[xs, gblk, gblk, pl.BlockSpec((BLK, DT_PAD), lambda c: (ci(c), 0)), v128, v128, v128],
                  out_shape=[_sds((T, D_INNER), F32), _sds((T, gn), F32), _sds((T, gn), F32), _sds((T, DT_PAD), BF16),
                             _sds((1, 128), F32), _sds((1, 128), F32), _sds((1, 128), F32)],
                  scratch_shapes=[pltpu.VMEM((SSM_HEADS, HEAD_DIM, D_STATE), F32)],
                  compiler_params=_cparams(("arbitrary",)))(xbc, xbc, xbc, projp, dtb, alog, dfull, states, dy)


_WIN_ORDER = ("z", "ga", "gs", "xbc", "q", "k", "v", "dt")


def _win_to_padded(win_g):
    full = win_g.reshape(IN_DIM, D_MODEL)
    rows = []
    for nm in _WIN_ORDER:
        s, w = SEG[nm]
        rows.append(full[s:s + w])
    rows.append(jnp.zeros((DT_PAD - SEG["dt"][1], D_MODEL), win_g.dtype))
    return jnp.concatenate(rows, axis=0)


def _padded_to_win(dw):
    off = dict(z=OFF_Z, ga=OFF_GA, gs=OFF_GS, xbc=OFF_XBC, q=OFF_Q, k=OFF_K, v=OFF_V, dt=OFF_DT)
    rows = [dw[off[nm]:off[nm] + SEG[nm][1]] for nm in ("q", "k", "v", "z", "xbc", "dt", "ga", "gs")]
    return jnp.concatenate(rows, axis=0).reshape(N_DEV, IN_DIM // N_DEV, D_MODEL)


def _pad128(v):
    return jnp.pad(v, ((0, 0), (0, 128 - v.shape[1])))


_SMALL = (("loss", 128, 1), ("g_mix", 2048, 2048), ("conv_b", 3072, 3072), ("dt_bias", 128, 32), ("a_log", 128, 32),
          ("d_skip", 128, 32), ("g_ssd", 2048, 2048), ("sinks", 128, 16), ("g_ffn", 2048, 2048), ("g_ple", 2048, 2048),
          ("g_final", 2048, 2048))


def _small_vec(d):
    parts = []
    for nm, pw, w in _SMALL:
        v = d[nm].reshape(1, -1).astype(F32)
        parts.append(jnp.pad(v[:, :min(v.shape[1], pw)], ((0, 0), (0, pw - min(v.shape[1], pw)))))
    return jnp.concatenate(parts, axis=1)


def _small_split(vec):
    out, o = {}, 0
    for nm, pw, w in _SMALL:
        out[nm] = vec[0, o:o + w]
        o += pw
    return out


def kernel(x, p, positions, g_mix, w_in, conv_w, conv_b, dt_bias, a_log, d_skip, g_ssd, sinks, w_attn_br, w_ssd_br, w_o, g_ffn, w_gate, w_up, w_down, g_ple, w_ple_gate, w_ple_proj, g_final, loss_target, m_g_mix, m_w_in, m_conv_w, m_conv_b, m_dt_bias, m_a_log, m_d_skip, m_g_ssd, m_sinks, m_w_attn_br, m_w_ssd_br, m_w_o, m_g_ffn, m_w_gate, m_w_up, m_w_down, m_g_ple, m_w_ple_gate, m_w_ple_proj, m_g_final, v_g_mix, v_w_in, v_conv_w, v_conv_b, v_dt_bias, v_a_log, v_d_skip, v_g_ssd, v_sinks, v_w_attn_br, v_w_ssd_br, v_w_o, v_g_ffn, v_w_gate, v_w_up, v_w_down, v_g_ple, v_w_ple_gate, v_w_ple_proj, v_g_final):
    T = x.shape[1]
    D = D_MODEL
    W = dict(g_mix=g_mix, w_in=w_in, conv_w=conv_w, conv_b=conv_b, dt_bias=dt_bias, a_log=a_log, d_skip=d_skip, g_ssd=g_ssd,
             sinks=sinks, w_attn_br=w_attn_br, w_ssd_br=w_ssd_br, w_o=w_o, g_ffn=g_ffn, w_gate=w_gate, w_up=w_up, w_down=w_down,
             g_ple=g_ple, w_ple_gate=w_ple_gate, w_ple_proj=w_ple_proj, g_final=g_final)
    Mo = dict(g_mix=m_g_mix, w_in=m_w_in, conv_w=m_conv_w, conv_b=m_conv_b, dt_bias=m_dt_bias, a_log=m_a_log, d_skip=m_d_skip,
              g_ssd=m_g_ssd, sinks=m_sinks, w_attn_br=m_w_attn_br, w_ssd_br=m_w_ssd_br, w_o=m_w_o, g_ffn=m_g_ffn, w_gate=m_w_gate,
              w_up=m_w_up, w_down=m_w_down, g_ple=m_g_ple, w_ple_gate=m_w_ple_gate, w_ple_proj=m_w_ple_proj, g_final=m_g_final)
    Vo = dict(g_mix=v_g_mix, w_in=v_w_in, conv_w=v_conv_w, conv_b=v_conv_b, dt_bias=v_dt_bias, a_log=v_a_log, d_skip=v_d_skip,
              g_ssd=v_g_ssd, sinks=v_sinks, w_attn_br=v_w_attn_br, w_ssd_br=v_w_ssd_br, w_o=v_w_o, g_ffn=v_g_ffn, w_gate=v_w_gate,
              w_up=v_w_up, w_down=v_w_down, g_ple=v_g_ple, w_ple_gate=v_w_ple_gate, w_ple_proj=v_w_ple_proj, g_final=v_g_final)
    order = ["g_mix", "w_in", "conv_w", "conv_b", "dt_bias", "a_log", "d_skip", "g_ssd", "sinks", "w_attn_br", "w_ssd_br", "w_o",
             "g_ffn", "w_gate", "w_up", "w_down", "g_ple", "w_ple_gate", "w_ple_proj", "g_final"]
    big = ["w_in", "conv_w", "w_attn_br", "w_ssd_br", "w_o", "w_gate", "w_up", "w_down", "w_ple_gate", "w_ple_proj"]

    x2 = x.reshape(T, D)
    p2 = p.reshape(T, PLE_DIM)
    tgt = loss_target.reshape(T, D)
    posf = positions.reshape(T, 1).astype(F32)
    inv = ROPE_THETA ** (-np.arange(HEAD_DIM // 2, dtype=np.float32) * 2.0 / HEAD_DIM)
    inv128 = jnp.asarray(np.tile(inv, 128 // (HEAD_DIM // 2)).reshape(1, 128).astype(np.float32))
    sh = {n: W[n].reshape(W[n].shape[-2:]) for n in big}

    del _PENDING[:]
    groups = (("w_in",), ("conv_w", "w_attn_br", "w_ssd_br", "w_o"), ("w_gate", "w_up", "w_down"), ("w_ple_gate", "w_ple_proj"))
    send = {n: sh[n] if n == "conv_w" else sh[n].astype(BF16) for n in big}
    send["w_in"] = sh["w_in"].T.astype(BF16)
    started, prev = [], None
    for gi, grp in enumerate(groups):
        h = split_start("gather_start_%d" % gi, "gather", ICI_SAME_CORE, [send[n] for n in grp], after=prev)
        prev = h["token"]
        started.append(h)
    gathered = {}

    def finish_group(gi, after):
        srcs, lands = split_wait("gather_wait_%d" % gi, started[gi], after)
        full = gather_finish("gather_finish_%d" % gi, srcs, lands)
        gathered.update(zip(groups[gi], full))

    u = rms_fwd("norm_mix", x2, g_mix)
    finish_group(0, u)
    winp = _win_to_padded(gathered["w_in"])
    dtb = _pad128(dt_bias)
    alog = _pad128(a_log)
    dfull = jnp.repeat(d_skip.reshape(SSM_HEADS), HEAD_DIM).reshape(1, D_INNER)

    projp = mm_nt("in_proj", u, winp, 640)
    attn, qr, kr = attn_fwd(projp, posf, inv128, sinks)
    finish_group(1, attn)
    convw = jnp.transpose(gathered["conv_w"], (1, 0, 2)).reshape(CONV_WIDTH, CONV_DIM)
    wab = gathered["w_attn_br"]
    wsb = gathered["w_ssd_br"].reshape(D, D)
    wo = gathered["w_o"].reshape(D, D)
    xbc = conv_fwd(projp, convw, conv_b)
    y, states = ssd_fwd(xbc, projp, dtb, alog, dfull)
    yn = gnorm_fwd(y, projp, g_ssd)
    out_a = mm_nn_colblk("attn_br", attn, wab)
    out_s = mm_nn("ssd_br", yn, wsb, 512)
    merged = merge_fwd(projp, out_a, out_s)
    h1 = mm_nn("o_proj", merged, wo, 512, residual=x2)
    f = rms_fwd("norm_ffn", h1, g_ffn)
    finish_group(2, f)
    wg, wu, wd = gathered["w_gate"], gathered["w_up"], gathered["w_down"]
    gate, up, act = ffn_up(f, wg, wu)
    h2 = ffn_down(act, wd, h1)
    r = rms_fwd("norm_ple", h2, g_ple)
    finish_group(3, r)
    wpg = gathered["w_ple_gate"].reshape(D, D)
    wpp = gathered["w_ple_proj"]
    pg = mm_nn("ple_gate", r, wpg, 512)
    pp = mm_nn_colblk("ple_proj", p2, wpp)
    loss_v, dh3, dpg, dpp, dg_final = head_fwd_bwd(h2, pg, pp, g_final.reshape(1, D), tgt)

    gw = {}
    scat = []

    def scatter_start(names):
        scat.append((names, split_start("scatter_start_%d" % len(scat), "scatter", ALL_PEERS, [gw[n] for n in names])))

    gw["w_ple_proj"] = mm_tn_colblk("dw_ple_proj", p2, dpp, PLE_DIM)
    dr = mm_nt("d_ple_gate", dpg, wpg, 512)
    gw["w_ple_gate"] = mm_tn("dw_ple_gate", r, dpg, 512, 1024).reshape(N_DEV, D // N_DEV, D)
    scatter_start(("w_ple_proj", "w_ple_gate"))
    dh2, dh2b, dg_ple = rms_bwd("norm_ple_bwd", h2, g_ple, dr, dh3)
    dgate, dup = ffn_down_bwd(dh2b, wd, gate, up)
    gw["w_down"] = wgrad_rowblk_lhs("dw_down", act, dh2b, 1024)
    gw["w_gate"] = wgrad_colblk_rhs("dw_gate", f, dgate, 1024)
    gw["w_up"] = wgrad_colblk_rhs("dw_up", f, dup, 1024)
    scatter_start(("w_down", "w_gate", "w_up"))
    df = ffn_up_bwd(dgate, dup, wg, wu)
    dh1, dh1b, dg_ffn = rms_bwd("norm_ffn_bwd", h1, g_ffn, df, dh2)
    dmerged = mm_nt("d_o_proj", dh1b, wo, 512)
    gw["w_o"] = mm_tn("dw_o", merged, dh1b, 512, 1024).reshape(N_DEV, D // N_DEV, D)
    dout_a, dout_s, dga, dgs = merge_bwd(projp, out_a, out_s, dmerged)
    gw["w_ssd_br"] = mm_tn("dw_ssd_br", yn, dout_s, 512, 1024).reshape(N_DEV, D // N_DEV, D)
    gw["w_attn_br"] = mm_tn_colblk("dw_attn_br", attn, dout_a, D // N_DEV)
    scatter_start(("w_o", "w_ssd_br", "w_attn_br"))
    dyn = mm_nt("d_ssd_br", dout_s, wsb, 512)
    dattn = attn_br_bwd(dout_a, wab)
    dy, dz, dg_ssd = gnorm_bwd(y, projp, g_ssd, dyn)
    dxs, dbm, dcm, ddt, dal, ddsk, ddtb = ssd_bwd(xbc, projp, dtb, alog, dfull, states, dy)
    dx_x, dwc_x, dbc_x = conv_bwd("conv_bwd_x", projp, dxs, convw, conv_b, 0)
    dx_b, dwc_b, dbc_b = conv_bwd("conv_bwd_b", projp, dbm, convw, conv_b, D_INNER)
    dx_c, dwc_c, dbc_c = conv_bwd("conv_bwd_c", projp, dcm, convw, conv_b, D_INNER + SSM_GROUPS * D_STATE)
    dq, dk, dv, dsk = attn_bwd(qr, kr, projp, dattn, posf, inv128, sinks)
    dproj = jnp.concatenate([dz, dga, dgs, dx_x, dx_b, dx_c, dq, dk.astype(BF16), dv.astype(BF16), ddt], axis=1)
    gw["w_in"] = _padded_to_win(mm_tn("dw_in", dproj, u, 640, 1024))
    dconvw = jnp.concatenate([dwc_x, dwc_b, dwc_c], axis=1)
    gw["conv_w"] = jnp.transpose(dconvw.reshape(CONV_WIDTH, N_DEV, CONV_DIM // N_DEV), (1, 0, 2))
    scatter_start(("w_in", "conv_w"))
    du = mm_nn_red("d_in_proj", dproj, winp, 1024, 640)
    gx, _, dg_mix = rms_bwd("norm_mix_bwd", x2, g_mix, du, dh1)

    me = 4 * lax.axis_index("x") + 2 * lax.axis_index("y") + lax.axis_index("c")
    res = {}
    after = gx
    for si, (names, h) in enumerate(scat):
        srcs, lands = split_wait("scatter_wait_%d" % si, h, after)
        for n, mine, arrived in zip(names, srcs, lands):
            own = lax.dynamic_index_in_dim(mine, me, 0, keepdims=False)
            if n == "w_in":
                arrived, own = None, sum_partials("sum_w_in", arrived, own, 256).T
            res[n] = adamw("adamw_" + n, arrived, sh[n], Mo[n].reshape(sh[n].shape), Vo[n].reshape(sh[n].shape), own=own)
        after = res[names[0]][0]

    small_g = dict(loss=loss_v[:, :1], g_mix=dg_mix, conv_b=jnp.concatenate([dbc_x, dbc_b, dbc_c], axis=1), dt_bias=ddtb,
                   a_log=dal, d_skip=ddsk, g_ssd=dg_ssd, sinks=dsk, g_ffn=dg_ffn, g_ple=dg_ple, g_final=dg_final)
    zero = jnp.zeros((1, 1), F32)
    vec_parts = exchange("gather_small", [_small_vec(small_g)], "gather")[0]
    sres = adamw("adamw_small", vec_parts, _small_vec({**W, "loss": zero}), _small_vec({**Mo, "loss": zero}),
                 _small_vec({**Vo, "loss": zero}))
    ssplit = [_small_split(a) for a in sres]
    loss = ssplit[0]["loss"].reshape(())
    for n in order:
        if n not in res:
            res[n] = tuple(s[n].reshape(W[n].shape) for s in ssplit)
        else:
            res[n] = tuple(a.reshape(W[n].shape) for a in res[n])
    outs = [loss, gx.reshape(x.shape)]
    for k in range(4):
        outs += [res[n][k] for n in order]
    return tuple(outs)
```

```python
import functools

import numpy as np
import jax
import jax.numpy as jnp
from jax import lax
from jax.experimental import pallas as pl
from jax.experimental.pallas import tpu as pltpu

F32 = jnp.float32
BF16 = jnp.bfloat16

N_DEV = 8
D_MODEL = 2048
HEAD_DIM = 64
ATTN_HEADS = 16
KV_HEADS = 4
Q_DIM = 1024
KV_DIM = 256
BLK = 128
D_INNER = 2048
SSM_HEADS = 32
SSM_GROUPS = 4
HEADS_PER_GROUP = 8
D_STATE = 128
CONV_WIDTH = 4
CONV_DIM = 3072
FFN_HIDDEN = 5632
PLE_DIM = 256
IN_DIM = 10784
NORM_EPS = 1e-6
SSM_NORM_EPS = 1e-5
ROPE_THETA = 10000.0

OFF_Z, OFF_GA, OFF_GS, OFF_XBC, OFF_Q, OFF_K, OFF_V, OFF_DT = 0, 2048, 4096, 6144, 9216, 10240, 10496, 10752
IN_PAD = 10880
DT_PAD = 128
SEG = dict(q=(0, 1024), k=(1024, 256), v=(1280, 256), z=(1536, 2048), xbc=(3584, 3072), dt=(6656, 32),
           ga=(6688, 2048), gs=(8736, 2048))

ADAM_LR, ADAM_B1, ADAM_B2, ADAM_EPS, ADAM_WD, ADAM_STEP = 0.001, 0.9, 0.999, 1e-08, 0.01, 10

VMEM_LIMIT = 56 * 1024 * 1024

NN = (((1,), (0,)), ((), ()))
NT = (((1,), (1,)), ((), ()))
TN = (((0,), (0,)), ((), ()))


_PENDING = []


def _raw_call(body, **kw):
    return pl.pallas_call(body, **kw)


def _pcall(body, **kw):
    deps = list(_PENDING)
    del _PENDING[:]
    if not deps:
        return _raw_call(body, **kw)
    n_in = len(kw["in_specs"])

    def tied(*refs):
        return body(*refs[:n_in], *refs[n_in + len(deps):])

    kw["in_specs"] = list(kw["in_specs"]) + [pl.BlockSpec(memory_space=pl.ANY)] * len(deps)
    call = _raw_call(tied, **kw)
    return lambda *ops: call(*ops, *deps)


def _cparams(sem=None):
    if sem is None:
        return pltpu.CompilerParams(vmem_limit_bytes=VMEM_LIMIT)
    return pltpu.CompilerParams(vmem_limit_bytes=VMEM_LIMIT, dimension_semantics=sem)


def _dot(a, b, dn):
    return lax.dot_general(a.astype(BF16), b.astype(BF16), dn, preferred_element_type=F32)


def _sigmoid(x):
    return 1.0 / (1.0 + jnp.exp(-x))


def _silu(x):
    return x * _sigmoid(x)


def _dsilu(x):
    s = _sigmoid(x)
    return s * (1.0 + x * (1.0 - s))


def _matmul(name, pairs, pair_specs, dn, grid, out_shapes, out_specs, nred=1, extra=(), extra_specs=(),
            epilogue=None, acc_shape=None):
    n_in = 2 * len(pairs) + len(extra)
    n_out = len(out_shapes)

    def body(*refs):
        ins = refs[:2 * len(pairs)]
        ex = refs[2 * len(pairs):n_in]
        outs = refs[n_in:n_in + n_out]

        def prod():
            s = None
            for p in range(len(pairs)):
                d = _dot(ins[2 * p][...], ins[2 * p + 1][...], dn)
                s = d if s is None else s + d
            return s

        def finish(val):
            if epilogue is None:
                outs[0][...] = val.astype(outs[0].dtype)
            else:
                res = epilogue(val, *[e[...] for e in ex])
                for o, r in zip(outs, res):
                    o[...] = r.astype(o.dtype)

        if nred == 1:
            finish(prod())
        else:
            acc = refs[n_in + n_out]
            k = pl.program_id(len(grid) - 1)

            @pl.when(k == 0)
            def _():
                acc[...] = jnp.zeros_like(acc)

            acc[...] += prod()

            @pl.when(k == nred - 1)
            def _():
                finish(acc[...])

    operands = []
    specs = []
    for (a, b), (sa, sb) in zip(pairs, pair_specs):
        operands += [a, b]
        specs += [sa, sb]
    operands += list(extra)
    specs += list(extra_specs)
    scratch = [pltpu.VMEM(acc_shape, F32)] if nred > 1 else []
    sem = ("arbitrary",) * len(grid)
    res = _pcall(body, name=name, grid=grid, in_specs=specs, out_specs=list(out_specs),
                 out_shape=list(out_shapes), scratch_shapes=scratch, compiler_params=_cparams(sem))(*operands)
    return res


def _sds(shape, dtype):
    return jax.ShapeDtypeStruct(shape, dtype)


def _row_tile(T):
    return min(1024, T)


def mm_nn(name, a, b, tn, out_dtype=F32, residual=None):
    M, K = a.shape
    N = b.shape[1]
    tm = _row_tile(M)
    grid = (M // tm, N // tn)
    extra, especs, epi = (), (), None
    if residual is not None:
        extra = (residual,)
        especs = (pl.BlockSpec((tm, tn), lambda i, n: (i, n)),)
        epi = lambda v, r: (v + r,)
    return _matmul(name, [(a, b)], [(pl.BlockSpec((tm, K), lambda i, n: (i, 0)), pl.BlockSpec((K, tn), lambda i, n: (0, n)))],
                   NN, grid, [_sds((M, N), out_dtype)], [pl.BlockSpec((tm, tn), lambda i, n: (i, n))],
                   extra=extra, extra_specs=especs, epilogue=epi)[0]


def mm_nn_colblk(name, a, b, out_dtype=F32):
    M, K = a.shape
    J, _, nb = b.shape
    tm = _row_tile(M)
    grid = (M // tm, J)
    return _matmul(name, [(a, b)], [(pl.BlockSpec((tm, K), lambda i, j: (i, 0)), pl.BlockSpec((None, K, nb), lambda i, j: (j, 0, 0)))],
                   NN, grid, [_sds((M, J * nb), out_dtype)], [pl.BlockSpec((tm, nb), lambda i, j: (i, j))])[0]


def mm_nt(name, a, w, tr, out_dtype=F32):
    M, C = a.shape
    R = w.shape[0]
    tm = _row_tile(M)
    grid = (M // tm, R // tr)
    return _matmul(name, [(a, w)], [(pl.BlockSpec((tm, C), lambda i, r: (i, 0)), pl.BlockSpec((tr, C), lambda i, r: (r, 0)))],
                   NT, grid, [_sds((M, R), out_dtype)], [pl.BlockSpec((tm, tr), lambda i, r: (i, r))])[0]


def mm_nt_red(name, a, w, tr, tk, out_dtype=F32):
    M, C = a.shape
    R = w.shape[0]
    tm = _row_tile(M)
    nk = C // tk
    grid = (M // tm, R // tr, nk)
    return _matmul(name, [(a, w)], [(pl.BlockSpec((tm, tk), lambda i, r, k: (i, k)), pl.BlockSpec((tr, tk), lambda i, r, k: (r, k)))],
                   NT, grid, [_sds((M, R), out_dtype)], [pl.BlockSpec((tm, tr), lambda i, r, k: (i, r))],
                   nred=nk, acc_shape=(tm, tr))[0]


def mm_nn_red(name, a, b, tn, tk, out_dtype=F32):
    M, K = a.shape
    N = b.shape[1]
    tm = _row_tile(M)
    nk = K // tk
    grid = (M // tm, N // tn, nk)
    return _matmul(name, [(a, b)], [(pl.BlockSpec((tm, tk), lambda i, n, k: (i, k)), pl.BlockSpec((tk, tn), lambda i, n, k: (k, n)))],
                   NN, grid, [_sds((M, N), out_dtype)], [pl.BlockSpec((tm, tn), lambda i, n, k: (i, n))],
                   nred=nk, acc_shape=(tm, tn))[0]


def mm_tn(name, x, dy, tr, tc, out_dtype=BF16):
    M, R = x.shape
    C = dy.shape[1]
    grid = (R // tr, C // tc)
    return _matmul(name, [(x, dy)], [(pl.BlockSpec((M, tr), lambda r, c: (0, r)), pl.BlockSpec((M, tc), lambda r, c: (0, c)))],
                   TN, grid, [_sds((R, C), out_dtype)], [pl.BlockSpec((tr, tc), lambda r, c: (r, c))])[0]


def mm_tn_colblk(name, x, dy, nb, out_dtype=BF16):
    M, R = x.shape
    J = dy.shape[1] // nb
    grid = (J,)
    return _matmul(name, [(x, dy)], [(pl.BlockSpec((M, R), lambda j: (0, 0)), pl.BlockSpec((M, nb), lambda j: (0, j)))],
                   TN, grid, [_sds((J, R, nb), out_dtype)], [pl.BlockSpec((None, R, nb), lambda j: (j, 0, 0))])[0]


def _rows(T):
    return min(256, T)


def rms_fwd(name, x, g, eps=NORM_EPS):
    T, D = x.shape
    tm = _rows(T)

    def body(x_ref, g_ref, o_ref):
        xv = x_ref[...]
        r = lax.rsqrt(jnp.mean(xv * xv, axis=-1, keepdims=True) + eps)
        o_ref[...] = (xv * r * g_ref[...]).astype(BF16)

    return _pcall(body, name=name, grid=(T // tm,),
                  in_specs=[pl.BlockSpec((tm, D), lambda i: (i, 0)), pl.BlockSpec((1, D), lambda i: (0, 0))],
                  out_specs=pl.BlockSpec((tm, D), lambda i: (i, 0)), out_shape=_sds((T, D), BF16),
                  compiler_params=_cparams(("arbitrary",)))(x, g)


def rms_bwd(name, x, g, dy, dres, eps=NORM_EPS):
    T, D = x.shape
    tm = _rows(T)

    def body(x_ref, g_ref, dy_ref, dr_ref, dx_ref, dxb_ref, dg_ref):
        i = pl.program_id(0)
        xv = x_ref[...]
        r = lax.rsqrt(jnp.mean(xv * xv, axis=-1, keepdims=True) + eps)
        xh = xv * r
        dyv = dy_ref[...]
        gd = dyv * g_ref[...]
        dx = r * (gd - xh * jnp.mean(gd * xh, axis=-1, keepdims=True)) + dr_ref[...]
        dx_ref[...] = dx
        dxb_ref[...] = dx.astype(BF16)

        @pl.when(i == 0)
        def _():
            dg_ref[...] = jnp.zeros_like(dg_ref)

        dg_ref[...] += jnp.sum(dyv * xh, axis=0, keepdims=True)

    row = pl.BlockSpec((tm, D), lambda i: (i, 0))
    vec = pl.BlockSpec((1, D), lambda i: (0, 0))
    return _pcall(body, name=name, grid=(T // tm,), in_specs=[row, vec, row, row], out_specs=[row, row, vec],
                  out_shape=[_sds((T, D), F32), _sds((T, D), BF16), _sds((1, D), F32)],
                  compiler_params=_cparams(("arbitrary",)))(x, g, dy, dres)


def gnorm_fwd(y, projp, g):
    T, D = y.shape
    tm = _rows(T)

    def body(y_ref, z_ref, g_ref, o_ref):
        yz = y_ref[...] * _silu(z_ref[...])
        r = lax.rsqrt(jnp.mean(yz * yz, axis=-1, keepdims=True) + SSM_NORM_EPS)
        o_ref[...] = (yz * r * g_ref[...]).astype(BF16)

    row = pl.BlockSpec((tm, D), lambda i: (i, 0))
    return _pcall(body, name="gnorm_fwd", grid=(T // tm,),
                  in_specs=[row, pl.BlockSpec((tm, D), lambda i: (i, OFF_Z // D)), pl.BlockSpec((1, D), lambda i: (0, 0))],
                  out_specs=row, out_shape=_sds((T, D), BF16), compiler_params=_cparams(("arbitrary",)))(y, projp, g)


def gnorm_bwd(y, projp, g, dyn):
    T, D = y.shape
    tm = _rows(T)

    def body(y_ref, z_ref, g_ref, dyn_ref, dy_ref, dz_ref, dg_ref):
        i = pl.program_id(0)
        yv, zv = y_ref[...], z_ref[...]
        sz = _silu(zv)
        yz = yv * sz
        r = lax.rsqrt(jnp.mean(yz * yz, axis=-1, keepdims=True) + SSM_NORM_EPS)
        xh = yz * r
        dv = dyn_ref[...]
        gd = dv * g_ref[...]
        dyz = r * (gd - xh * jnp.mean(gd * xh, axis=-1, keepdims=True))
        dy_ref[...] = dyz * sz
        dz_ref[...] = (dyz * yv * _dsilu(zv)).astype(BF16)

        @pl.when(i == 0)
        def _():
            dg_ref[...] = jnp.zeros_like(dg_ref)

        dg_ref[...] += jnp.sum(dv * xh, axis=0, keepdims=True)

    row = pl.BlockSpec((tm, D), lambda i: (i, 0))
    vec = pl.BlockSpec((1, D), lambda i: (0, 0))
    return _pcall(body, name="gnorm_bwd", grid=(T // tm,),
                  in_specs=[row, pl.BlockSpec((tm, D), lambda i: (i, OFF_Z // D)), vec, row], out_specs=[row, row, vec],
                  out_shape=[_sds((T, D), F32), _sds((T, D), BF16), _sds((1, D), F32)],
                  compiler_params=_cparams(("arbitrary",)))(y, projp, g, dyn)


def merge_fwd(projp, out_a, out_s):
    T, D = out_a.shape
    tm = _rows(T)

    def body(ga_ref, gs_ref, a_ref, s_ref, o_ref):
        o_ref[...] = (_sigmoid(ga_ref[...]) * a_ref[...] + _sigmoid(gs_ref[...]) * s_ref[...]).astype(BF16)

    row = pl.BlockSpec((tm, D), lambda i: (i, 0))
    return _pcall(body, name="merge_fwd", grid=(T // tm,),
                  in_specs=[pl.BlockSpec((tm, D), lambda i: (i, OFF_GA // D)), pl.BlockSpec((tm, D), lambda i: (i, OFF_GS // D)), row, row],
                  out_specs=row, out_shape=_sds((T, D), BF16), compiler_params=_cparams(("arbitrary",)))(projp, projp, out_a, out_s)


def merge_bwd(projp, out_a, out_s, dmerged):
    T, D = out_a.shape
    tm = _rows(T)

    def body(ga_ref, gs_ref, a_ref, s_ref, dm_ref, da_ref, ds_ref, dga_ref, dgs_ref):
        dm = dm_ref[...]
        sa, ss = _sigmoid(ga_ref[...]), _sigmoid(gs_ref[...])
        da_ref[...] = (dm * sa).astype(BF16)
        ds_ref[...] = (dm * ss).astype(BF16)
        dga_ref[...] = (dm * a_ref[...] * sa * (1.0 - sa)).astype(BF16)
        dgs_ref[...] = (dm * s_ref[...] * ss * (1.0 - ss)).astype(BF16)

    row = pl.BlockSpec((tm, D), lambda i: (i, 0))
    return _pcall(body, name="merge_bwd", grid=(T // tm,),
                  in_specs=[pl.BlockSpec((tm, D), lambda i: (i, OFF_GA // D)), pl.BlockSpec((tm, D), lambda i: (i, OFF_GS // D)), row, row, row],
                  out_specs=[row] * 4, out_shape=[_sds((T, D), BF16)] * 4,
                  compiler_params=_cparams(("arbitrary",)))(projp, projp, out_a, out_s, dmerged)


def head_fwd_bwd(h2, pg, pp, g_final, target):
    T, D = h2.shape
    tm = _rows(T)

    def body(h_ref, pg_ref, pp_ref, g_ref, t_ref, loss_ref, dh_ref, dpg_ref, dpp_ref, dg_ref):
        i = pl.program_id(0)
        s = _sigmoid(pg_ref[...])
        ppv = pp_ref[...]
        h3 = h_ref[...] + s * ppv
        r = lax.rsqrt(jnp.mean(h3 * h3, axis=-1, keepdims=True) + NORM_EPS)
        xh = h3 * r
        gv = g_ref[...]
        e = xh * gv - t_ref[...]
        dyo = e * (1.0 / D)
        gd = dyo * gv
        dh = r * (gd - xh * jnp.mean(gd * xh, axis=-1, keepdims=True))
        dh_ref[...] = dh
        dpg_ref[...] = (dh * ppv * s * (1.0 - s)).astype(BF16)
        dpp_ref[...] = (dh * s).astype(BF16)

        @pl.when(i == 0)
        def _():
            dg_ref[...] = jnp.zeros_like(dg_ref)
            loss_ref[...] = jnp.zeros_like(loss_ref)

        dg_ref[...] += jnp.sum(dyo * xh, axis=0, keepdims=True)
        part = 0.5 * jnp.sum(jnp.mean(e * e, axis=-1, keepdims=True), axis=0, keepdims=True)
        loss_ref[...] += jnp.broadcast_to(part, loss_ref.shape)

    row = pl.BlockSpec((tm, D), lambda i: (i, 0))
    vec = pl.BlockSpec((1, D), lambda i: (0, 0))
    return _pcall(body, name="head_fwd_bwd", grid=(T // tm,), in_specs=[row, row, row, vec, row],
                  out_specs=[pl.BlockSpec((1, 128), lambda i: (0, 0)), row, row, row, vec],
                  out_shape=[_sds((1, 128), F32), _sds((T, D), F32), _sds((T, D), BF16), _sds((T, D), BF16), _sds((1, D), F32)],
                  compiler_params=_cparams(("arbitrary",)))(h2, pg, pp, g_final, target)


def ffn_up(f, wg, wu):
    T, D = f.shape
    J, _, nb = wg.shape
    tm = _row_tile(T)

    def body(f_ref, wg_ref, wu_ref, g_ref, u_ref, a_ref):
        fv = f_ref[...]
        g = _dot(fv, wg_ref[...], NN)
        u = _dot(fv, wu_ref[...], NN)
        g_ref[...] = g
        u_ref[...] = u
        a_ref[...] = (_silu(g) * u).astype(BF16)

    wspec = pl.BlockSpec((None, D, nb), lambda i, j: (j, 0, 0))
    ospec = pl.BlockSpec((None, tm, nb), lambda i, j: (j, i, 0))
    return _pcall(body, name="ffn_up", grid=(T // tm, J), in_specs=[pl.BlockSpec((tm, D), lambda i, j: (i, 0)), wspec, wspec],
                  out_specs=[ospec] * 3, out_shape=[_sds((J, T, nb), F32), _sds((J, T, nb), F32), _sds((J, T, nb), BF16)],
                  compiler_params=_cparams(("arbitrary", "arbitrary")))(f, wg, wu)


def ffn_down(act, wd, h1):
    J, T, nb = act.shape
    D = wd.shape[2]
    tm = _row_tile(T)
    tn = 1024
    grid = (T // tm, D // tn, J)
    return _matmul("ffn_down", [(act, wd)],
                   [(pl.BlockSpec((None, tm, nb), lambda i, n, j: (j, i, 0)), pl.BlockSpec((None, nb, tn), lambda i, n, j: (j, 0, n)))],
                   NN, grid, [_sds((T, D), F32)], [pl.BlockSpec((tm, tn), lambda i, n, j: (i, n))], nred=J, acc_shape=(tm, tn),
                   extra=(h1,), extra_specs=(pl.BlockSpec((tm, tn), lambda i, n, j: (i, n)),), epilogue=lambda v, r: (v + r,))[0]


def ffn_down_bwd(dh2b, wd, gate, up):
    T, D = dh2b.shape
    J, nb, _ = wd.shape
    tm = _row_tile(T)
    ospec = pl.BlockSpec((None, tm, nb), lambda i, j: (j, i, 0))

    def epi(da, g, u):
        return (da * u * _dsilu(g), da * _silu(g))

    return _matmul("ffn_down_bwd", [(dh2b, wd)],
                   [(pl.BlockSpec((tm, D), lambda i, j: (i, 0)), pl.BlockSpec((None, nb, D), lambda i, j: (j, 0, 0)))],
                   NT, (T // tm, J), [_sds((J, T, nb), BF16)] * 2, [ospec, ospec],
                   extra=(gate, up), extra_specs=(ospec, ospec), epilogue=epi)


def ffn_up_bwd(dgate, dup, wg, wu):
    J, T, nb = dgate.shape
    D = wg.shape[1]
    tm = _row_tile(T)
    tr = 1024
    aspec = pl.BlockSpec((None, tm, nb), lambda i, r, j: (j, i, 0))
    wspec = pl.BlockSpec((None, tr, nb), lambda i, r, j: (j, r, 0))
    return _matmul("ffn_up_bwd", [(dgate, wg), (dup, wu)], [(aspec, wspec), (aspec, wspec)], NT, (T // tm, D // tr, J),
                   [_sds((T, D), F32)], [pl.BlockSpec((tm, tr), lambda i, r, j: (i, r))], nred=J, acc_shape=(tm, tr))[0]


def wgrad_rowblk_lhs(name, xb, dy, tc):
    J, T, nb = xb.shape
    C = dy.shape[1]
    return _matmul(name, [(xb, dy)],
                   [(pl.BlockSpec((None, T, nb), lambda j, c: (j, 0, 0)), pl.BlockSpec((T, tc), lambda j, c: (0, c)))],
                   TN, (J, C // tc), [_sds((J, nb, C), BF16)], [pl.BlockSpec((None, nb, tc), lambda j, c: (j, 0, c))])[0]


def wgrad_colblk_rhs(name, x, dyb, tr):
    T, R = x.shape
    J, _, nb = dyb.shape
    return _matmul(name, [(x, dyb)],
                   [(pl.BlockSpec((T, tr), lambda j, r: (0, r)), pl.BlockSpec((None, T, nb), lambda j, r: (j, 0, 0)))],
                   TN, (J, R // tr), [_sds((J, R, nb), BF16)], [pl.BlockSpec((None, tr, nb), lambda j, r: (j, r, 0))])[0]


def attn_br_bwd(dout_a, wab):
    T, D = dout_a.shape
    J, R, nb = wab.shape
    tm = _row_tile(T)
    return _matmul("attn_br_bwd", [(dout_a, wab)],
                   [(pl.BlockSpec((tm, nb), lambda i, j: (i, j)), pl.BlockSpec((None, R, nb), lambda i, j: (j, 0, 0)))],
                   NT, (T // tm, J), [_sds((T, R), BF16)], [pl.BlockSpec((tm, R), lambda i, j: (i, 0))], nred=J, acc_shape=(tm, R))[0]


def _adam_math(w, g, m, v):
    m2 = ADAM_B1 * m + (1.0 - ADAM_B1) * g
    v2 = ADAM_B2 * v + (1.0 - ADAM_B2) * (g * g)
    m_hat = m2 / (1.0 - ADAM_B1 ** ADAM_STEP)
    v_hat = v2 / (1.0 - ADAM_B2 ** ADAM_STEP)
    delta = -ADAM_LR * (m_hat / (jnp.sqrt(v_hat) + ADAM_EPS) + ADAM_WD * w)
    return delta, m2, v2


def _sum_partials(own, parts):
    g = None if own is None else own.astype(F32)
    if parts is not None:
        for s in range(parts.shape[0]):
            t = parts[s].astype(F32)
            g = t if g is None else g + t
    return g


def adamw(name, parts, w, m, v, own=None):
    R, C = w.shape
    tr = R
    for cand in (256, 176, 128, 64, 32, 16, 8):
        if R % cand == 0 and R > cand:
            tr = cand
            break
    given = [a for a in (parts, own) if a is not None]

    def body(*refs):
        p_ref = refs[0] if parts is not None else None
        o_ref = refs[len(given) - 1] if own is not None else None
        w_ref, m_ref, v_ref, g_ref, d_ref, m2_ref, v2_ref = refs[-7:]
        g = _sum_partials(None if o_ref is None else o_ref[...], p_ref)
        d, m2, v2 = _adam_math(w_ref[...], g, m_ref[...], v_ref[...])
        g_ref[...] = g
        d_ref[...] = d
        m2_ref[...] = m2
        v2_ref[...] = v2

    blk = pl.BlockSpec((tr, C), lambda i: (i, 0))
    specs = ([] if parts is None else [pl.BlockSpec((parts.shape[0], tr, C), lambda i: (0, i, 0))]) + [blk] * (3 + (own is not None))
    return _pcall(body, name=name, grid=(R // tr,), in_specs=specs,
                  out_specs=[blk] * 4, out_shape=[_sds((R, C), F32)] * 4,
                  compiler_params=_cparams(("arbitrary",)))(*given, w, m, v)


def sum_partials(name, parts, own, tc):
    R, C = own.shape

    def body(p_ref, o_ref, g_ref):
        g_ref[...] = _sum_partials(o_ref[...], p_ref)

    blk = pl.BlockSpec((R, tc), lambda i: (0, i))
    return _pcall(body, name=name, grid=(C // tc,), in_specs=[pl.BlockSpec((parts.shape[0], R, tc), lambda i: (0, 0, i)), blk],
                  out_specs=blk, out_shape=_sds((R, C), F32), compiler_params=_cparams(("arbitrary",)))(parts, own)


def exchange(name, arrays, mode):
    n = len(arrays)
    out_shapes = []
    for a in arrays:
        shp = a.shape if mode == "scatter" else (N_DEV,) + a.shape
        out_shapes.append(_sds(shp, a.dtype))

    def body(*refs):
        ins, outs = refs[:n], refs[n:2 * n]
        send_sems, recv_sems, local_sems = refs[2 * n:]
        x, y, c = lax.axis_index("x"), lax.axis_index("y"), lax.axis_index("c")
        me = 4 * x + 2 * y + c

        def src(a, dest):
            return ins[a].at[dest] if mode == "scatter" else ins[a]

        local = [pltpu.make_async_copy(src(a, me), outs[a].at[me], local_sems.at[a]) for a in range(n)]
        for cp in local:
            cp.start()
        remote = []
        for k in range(1, N_DEV):
            px = 1 - x if k & 4 else x
            py = 1 - y if k & 2 else y
            pc = 1 - c if k & 1 else c
            peer = 4 * px + 2 * py + pc
            for a in range(n):
                cp = pltpu.make_async_remote_copy(src_ref=src(a, peer), dst_ref=outs[a].at[me],
                                                  send_sem=send_sems.at[a * 7 + k - 1], recv_sem=recv_sems.at[a * 7 + k - 1],
                                                  device_id=(px, py, pc), device_id_type=pl.DeviceIdType.MESH)
                cp.start()
                arrival = pltpu.make_async_remote_copy(src_ref=src(a, peer), dst_ref=outs[a].at[peer],
                                                       send_sem=send_sems.at[a * 7 + k - 1], recv_sem=recv_sems.at[a * 7 + k - 1],
                                                       device_id=(px, py, pc), device_id_type=pl.DeviceIdType.MESH)
                remote.append((cp, arrival))
        for cp, arrival in remote:
            cp.wait_send()
            arrival.wait_recv()
        for cp in local:
            cp.wait()

    any_spec = pl.BlockSpec(memory_space=pl.ANY)
    return _pcall(body, name=name, in_specs=[any_spec] * n, out_specs=[any_spec] * n, out_shape=out_shapes,
                  scratch_shapes=[pltpu.SemaphoreType.DMA((7 * n,)), pltpu.SemaphoreType.DMA((7 * n,)), pltpu.SemaphoreType.DMA((n,))],
                  compiler_params=pltpu.CompilerParams(has_side_effects=True))(*arrays)


_HBM = pl.BlockSpec(memory_space=pltpu.HBM)
_SEM = pl.BlockSpec(memory_space=pltpu.SEMAPHORE)
_ANY = pl.BlockSpec(memory_space=pl.ANY)
_SPLIT_PARAMS = dict(compiler_params=pltpu.CompilerParams(has_side_effects=pltpu.SideEffectType.DATAFLOW_SIDE_EFFECTING))
ICI_SAME_CORE = (2, 4, 6)
ALL_PEERS = (1, 2, 3, 4, 5, 6, 7)


def _mesh_pos():
    x, y, c = lax.axis_index("x"), lax.axis_index("y"), lax.axis_index("c")
    return x, y, c, 4 * x + 2 * y + c


def _peer_of(k, x, y, c):
    px = 1 - x if k & 4 else x
    py = 1 - y if k & 2 else y
    pc = 1 - c if k & 1 else c
    return (px, py, pc), 4 * px + 2 * py + pc


def _split_copies(mode, ks, srcs, lands, send_sems, recv_sems):
    x, y, c, me = _mesh_pos()
    pairs = []
    for a in range(len(lands)):
        for j, k in enumerate(ks):
            dev, peer = _peer_of(k, x, y, c)
            i = a * len(ks) + j
            if mode == "gather":
                s_out, d_out, d_in = srcs[a], lands[a].at[me], lands[a].at[peer]
            elif mode == "scatter":
                s_out, d_out, d_in = srcs[a].at[peer], lands[a].at[k - 1], lands[a].at[k - 1]
            else:
                dev, _ = _peer_of(1, x, y, c)
                _, theirs = _peer_of(k | 1, x, y, c)
                s_out, d_out, d_in = lands[a].at[peer], lands[a].at[peer], lands[a].at[theirs]
            both = [pltpu.make_async_remote_copy(src_ref=s_out, dst_ref=d, send_sem=send_sems.at[i], recv_sem=recv_sems.at[i],
                                                 device_id=dev, device_id_type=pl.DeviceIdType.MESH) for d in (d_out, d_in)]
            pairs.append(tuple(both))
    return pairs


def split_start(name, mode, ks, srcs, lands=None, after=None):
    n, nk = len(srcs) if lands is None else len(lands), len(ks)
    srcs = [pltpu.with_memory_space_constraint(s, pltpu.HBM) for s in srcs]
    if lands is None:
        shapes = [((N_DEV,) + s.shape) if mode == "gather" else ((N_DEV - 1,) + s.shape[1:]) for s in srcs]
        lands = [lax.empty(shp, s.dtype) for shp, s in zip(shapes, srcs)]
    lands = [pltpu.with_memory_space_constraint(l, pltpu.HBM) for l in lands]
    both = srcs + lands
    extra = [] if after is None else [after]

    def body(*refs):
        src_refs, land_refs = refs[:len(srcs)], refs[len(srcs):len(both)]
        send_sems, recv_sems = refs[len(both) + len(extra)], refs[len(both) + len(extra) + 1]
        token = refs[-1]
        for out, _ in _split_copies(mode, ks, src_refs, land_refs, send_sems, recv_sems):
            out.start()
        token[...] = jnp.zeros_like(token)

    out_shape = (pltpu.SemaphoreType.DMA((n * nk,)), pltpu.SemaphoreType.DMA((n * nk,)),
                 *[pltpu.HBM(a.shape, a.dtype) for a in both], _sds((8, 128), F32))
    res = _raw_call(body, name=name, out_shape=out_shape, in_specs=[_HBM] * len(both) + [_ANY] * len(extra),
                    out_specs=(_SEM, _SEM, *[_HBM] * len(both), pl.BlockSpec(memory_space=pltpu.VMEM)),
                    input_output_aliases={i: 2 + i for i in range(len(both))}, **_SPLIT_PARAMS)(*both, *extra)
    _PENDING.append(res[-1])
    return dict(mode=mode, ks=ks, sems=(res[0], res[1]), srcs=list(res[2:2 + len(srcs)]),
                lands=list(res[2 + len(srcs):2 + len(both)]), token=res[-1])


def split_wait(name, h, after):
    ns = len(h["srcs"])
    both = h["srcs"] + h["lands"]

    def body(*refs):
        src_refs, land_refs = refs[:ns], refs[ns:len(both)]
        send_sems, recv_sems = refs[len(both)], refs[len(both) + 1]
        for out, arriving in _split_copies(h["mode"], h["ks"], src_refs, land_refs, send_sems, recv_sems):
            out.wait_send()
            arriving.wait_recv()

    res = _raw_call(body, name=name, out_shape=tuple(pltpu.HBM(a.shape, a.dtype) for a in both),
                    in_specs=[_HBM] * len(both) + [_SEM, _SEM, _ANY], out_specs=tuple([_HBM] * len(both)),
                    input_output_aliases={i: i for i in range(len(both))}, **_SPLIT_PARAMS)(*both, *h["sems"], after)
    return list(res[:ns]), list(res[ns:])


FORWARD_BLOCKS = (0, 2, 4, 6)


def place_own(name, shards, lands):
    n = len(shards)

    def body(*refs):
        sh, land, sems = refs[:n], refs[2 * n:3 * n], refs[3 * n]
        _, _, _, me = _mesh_pos()
        copies = [pltpu.make_async_copy(sh[a], land[a].at[me], sems.at[a]) for a in range(n)]
        for cp in copies:
            cp.start()
        for cp in copies:
            cp.wait()

    return _pcall(body, name=name, in_specs=[_ANY] * (2 * n), out_specs=[_ANY] * n,
                  out_shape=[_sds(l.shape, l.dtype) for l in lands], input_output_aliases={n + a: a for a in range(n)},
                  scratch_shapes=[pltpu.SemaphoreType.DMA((n,))])(*shards, *lands)


def _rope_parts(pos_ref, inv_ref):
    ang = pos_ref[...] * inv_ref[...]
    return jnp.cos(ang), jnp.sin(ang)


def _rot_half(t):
    lane = lax.broadcasted_iota(jnp.int32, t.shape, 1)
    return jnp.where((lane % HEAD_DIM) < HEAD_DIM // 2, -pltpu.roll(t, 128 - HEAD_DIM // 2, 1), pltpu.roll(t, HEAD_DIM // 2, 1))


def _attn_mask(n):
    row = lax.broadcasted_iota(jnp.int32, (BLK, 2 * BLK), 0)
    col = lax.broadcasted_iota(jnp.int32, (BLK, 2 * BLK), 1)
    dist = row + BLK - col
    return (dist >= 0) & (dist < BLK) & ((n * BLK - BLK + col) >= 0)


def _attn_specs(T):
    prev = lambda n: jnp.maximum(n - 1, 0)
    kc = pl.BlockSpec((BLK, KV_DIM), lambda n: (n, OFF_K // KV_DIM))
    kp = pl.BlockSpec((BLK, KV_DIM), lambda n: (prev(n), OFF_K // KV_DIM))
    vc = pl.BlockSpec((BLK, KV_DIM), lambda n: (n, OFF_V // KV_DIM))
    vp = pl.BlockSpec((BLK, KV_DIM), lambda n: (prev(n), OFF_V // KV_DIM))
    pc = pl.BlockSpec((BLK, 1), lambda n: (n, 0))
    pp = pl.BlockSpec((BLK, 1), lambda n: (prev(n), 0))
    inv = pl.BlockSpec((1, 128), lambda n: (0, 0))
    sink = pl.BlockSpec(memory_space=pltpu.SMEM)
    return kc, kp, vc, vp, pc, pp, inv, sink


def _softmax_sink(sc, valid, sink):
    sc = jnp.where(valid, sc * (HEAD_DIM ** -0.5), -1e30)
    m = jnp.maximum(jnp.max(sc, axis=1, keepdims=True), sink)
    e = jnp.exp(sc - m)
    es = jnp.exp(sink - m)
    den = jnp.sum(e, axis=1, keepdims=True) + es
    return e / den, es / den


def attn_fwd(projp, posf, inv128, sinks):
    T = projp.shape[0]
    kc, kp, vc, vp, pc, pp, inv, sink = _attn_specs(T)

    def body(q_ref, kc_ref, kp_ref, vc_ref, vp_ref, pc_ref, pp_ref, inv_ref, sink_ref, o_ref, qr_ref, kr_ref):
        n = pl.program_id(0)
        cos_c, sin_c = _rope_parts(pc_ref, inv_ref)
        cos_p, sin_p = _rope_parts(pp_ref, inv_ref)
        valid = _attn_mask(n)
        k_c, k_p = [], []
        for s in range(KV_DIM // 128):
            t = kc_ref[:, 128 * s:128 * (s + 1)]
            k_c.append((t * cos_c + _rot_half(t) * sin_c).astype(BF16))
            kr_ref[:, 128 * s:128 * (s + 1)] = k_c[s]
            t = kp_ref[:, 128 * s:128 * (s + 1)]
            k_p.append((t * cos_p + _rot_half(t) * sin_p).astype(BF16))
        for s in range(Q_DIM // 128):
            t = q_ref[:, 128 * s:128 * (s + 1)]
            qs = (t * cos_c + _rot_half(t) * sin_c).astype(BF16)
            qr_ref[:, 128 * s:128 * (s + 1)] = qs
            for e in range(2):
                hq = 2 * s + e
                hk = hq // (ATTN_HEADS // KV_HEADS)
                lo = HEAD_DIM * (hk % 2)
                kcat = jnp.concatenate([k_p[hk // 2][:, lo:lo + HEAD_DIM], k_c[hk // 2][:, lo:lo + HEAD_DIM]], axis=0)
                vcat = jnp.concatenate([vp_ref[:, HEAD_DIM * hk:HEAD_DIM * (hk + 1)], vc_ref[:, HEAD_DIM * hk:HEAD_DIM * (hk + 1)]], axis=0)
                sc = _dot(qs[:, HEAD_DIM * e:HEAD_DIM * (e + 1)], kcat, NT)
                p, _ = _softmax_sink(sc, valid, sink_ref[0, hq])
                o_ref[:, HEAD_DIM * hq:HEAD_DIM * (hq + 1)] = _dot(p, vcat, NN).astype(BF16)

    qspec = pl.BlockSpec((BLK, Q_DIM), lambda n: (n, OFF_Q // Q_DIM))
    orow = pl.BlockSpec((BLK, Q_DIM), lambda n: (n, 0))
    krow = pl.BlockSpec((BLK, KV_DIM), lambda n: (n, 0))
    return _pcall(body, name="attn_fwd", grid=(T // BLK,), in_specs=[qspec, kc, kp, vc, vp, pc, pp, inv, sink],
                  out_specs=[orow, orow, krow], out_shape=[_sds((T, Q_DIM), BF16), _sds((T, Q_DIM), BF16), _sds((T, KV_DIM), BF16)],
                  compiler_params=_cparams(("arbitrary",)))(projp, projp, projp, projp, projp, posf, posf, inv128, sinks)


def attn_bwd(qr, kr, projp, dattn, posf, inv128, sinks):
    T = projp.shape[0]
    _, _, vc, vp, pc, pp, inv, sink = _attn_specs(T)
    G = ATTN_HEADS // KV_HEADS

    def body(qr_ref, krc_ref, krp_ref, vc_ref, vp_ref, do_ref, pc_ref, pp_ref, inv_ref, sink_ref, dq_ref, dk_ref, dv_ref, dsk_ref):
        n = pl.program_id(0)

        @pl.when(n == 0)
        def _():
            dk_ref[...] = jnp.zeros_like(dk_ref)
            dv_ref[...] = jnp.zeros_like(dv_ref)
            dsk_ref[...] = jnp.zeros_like(dsk_ref)

        cos_c, sin_c = _rope_parts(pc_ref, inv_ref)
        cos_p, sin_p = _rope_parts(pp_ref, inv_ref)
        valid = _attn_mask(n)
        lane = lax.broadcasted_iota(jnp.int32, (1, 128), 1)
        dsk = jnp.zeros((1, 128), F32)
        dq_heads, dk_heads, dv_heads = [], [], []
        for hk in range(KV_HEADS):
            ksl = slice(HEAD_DIM * hk, HEAD_DIM * (hk + 1))
            kcat = jnp.concatenate([krp_ref[:, ksl], krc_ref[:, ksl]], axis=0)
            vcat = jnp.concatenate([vp_ref[:, ksl], vc_ref[:, ksl]], axis=0)
            dkcat = jnp.zeros((2 * BLK, HEAD_DIM), F32)
            dvcat = jnp.zeros((2 * BLK, HEAD_DIM), F32)
            for g in range(G):
                hq = G * hk + g
                qsl = slice(HEAD_DIM * hq, HEAD_DIM * (hq + 1))
                q_h = qr_ref[:, qsl]
                do_h = do_ref[:, qsl]
                p, psink = _softmax_sink(_dot(q_h, kcat, NT), valid, sink_ref[0, hq])
                dp = _dot(do_h, vcat, NT)
                delta = jnp.sum(p * dp, axis=1, keepdims=True)
                ds = p * (dp - delta) * (HEAD_DIM ** -0.5)
                dsk = dsk + jnp.where(lane == hq, -jnp.sum(psink * delta, axis=0, keepdims=True), 0.0)
                dq_heads.append(_dot(ds, kcat, NN))
                dkcat = dkcat + _dot(ds, q_h, TN)
                dvcat = dvcat + _dot(p, do_h, TN)
            dk_heads.append(dkcat)
            dv_heads.append(dvcat)
        dsk_ref[...] += dsk
        for s in range(Q_DIM // 128):
            t = jnp.concatenate([dq_heads[2 * s], dq_heads[2 * s + 1]], axis=1)
            dq_ref[:, 128 * s:128 * (s + 1)] = (t * cos_c - _rot_half(t) * sin_c).astype(BF16)
        cur = pl.ds(pl.multiple_of(n * BLK, BLK), BLK)
        prv = pl.ds(pl.multiple_of(jnp.maximum(n - 1, 0) * BLK, BLK), BLK)
        for s in range(KV_DIM // 128):
            tc = jnp.concatenate([dk_heads[2 * s][BLK:], dk_heads[2 * s + 1][BLK:]], axis=1)
            tp = jnp.concatenate([dk_heads[2 * s][:BLK], dk_heads[2 * s + 1][:BLK]], axis=1)
            cols = slice(128 * s, 128 * (s + 1))
            dk_ref[cur, cols] += tc * cos_c - _rot_half(tc) * sin_c
            dk_ref[prv, cols] += tp * cos_p - _rot_half(tp) * sin_p
            dv_ref[cur, cols] += jnp.concatenate([dv_heads[2 * s][BLK:], dv_heads[2 * s + 1][BLK:]], axis=1)
            dv_ref[prv, cols] += jnp.concatenate([dv_heads[2 * s][:BLK], dv_heads[2 * s + 1][:BLK]], axis=1)

    qrow = pl.BlockSpec((BLK, Q_DIM), lambda n: (n, 0))
    krc = pl.BlockSpec((BLK, KV_DIM), lambda n: (n, 0))
    krp = pl.BlockSpec((BLK, KV_DIM), lambda n: (jnp.maximum(n - 1, 0), 0))
    whole = pl.BlockSpec((T, KV_DIM), lambda n: (0, 0))
    return _pcall(body, name="attn_bwd", grid=(T // BLK,), in_specs=[qrow, krc, krp, vc, vp, qrow, pc, pp, inv, sink],
                  out_specs=[qrow, whole, whole, pl.BlockSpec((1, 128), lambda n: (0, 0))],
                  out_shape=[_sds((T, Q_DIM), BF16), _sds((T, KV_DIM), F32), _sds((T, KV_DIM), F32), _sds((1, 128), F32)],
                  compiler_params=_cparams(("arbitrary",)))(qr, kr, kr, projp, projp, dattn, posf, posf, inv128, sinks)


CONV_CB = 256


def _shift_down(x, s):
    row = lax.broadcasted_iota(jnp.int32, x.shape, 0)
    return jnp.where(row >= s, pltpu.roll(x, s, 0), 0.0)


def _shift_up(x, s):
    T = x.shape[0]
    row = lax.broadcasted_iota(jnp.int32, x.shape, 0)
    return jnp.where(row < T - s, pltpu.roll(x, T - s, 0), 0.0)


def _conv_pre(x, w_ref, b_ref):
    acc = x * w_ref[CONV_WIDTH - 1:CONV_WIDTH, :] + b_ref[...]
    for s in range(1, CONV_WIDTH):
        acc = acc + _shift_down(x, s) * w_ref[CONV_WIDTH - 1 - s:CONV_WIDTH - s, :]
    return acc


def conv_fwd(projp, conv_w, conv_b):
    T = projp.shape[0]

    def body(x_ref, w_ref, b_ref, o_ref):
        o_ref[...] = _silu(_conv_pre(x_ref[...], w_ref, b_ref))

    return _pcall(body, name="conv_fwd", grid=(CONV_DIM // CONV_CB,),
                  in_specs=[pl.BlockSpec((T, CONV_CB), lambda c: (0, OFF_XBC // CONV_CB + c)),
                            pl.BlockSpec((CONV_WIDTH, CONV_CB), lambda c: (0, c)), pl.BlockSpec((1, CONV_CB), lambda c: (0, c))],
                  out_specs=pl.BlockSpec((T, CONV_CB), lambda c: (0, c)), out_shape=_sds((T, CONV_DIM), F32),
                  compiler_params=_cparams(("arbitrary",)))(projp, conv_w, conv_b)


def conv_bwd(name, projp, dact, conv_w, conv_b, col0):
    T, C = dact.shape
    c0 = col0 // CONV_CB

    def body(x_ref, da_ref, w_ref, b_ref, dx_ref, dw_ref, db_ref):
        x = x_ref[...]
        dpre = da_ref[...] * _dsilu(_conv_pre(x, w_ref, b_ref))
        dx = dpre * w_ref[CONV_WIDTH - 1:CONV_WIDTH, :]
        dw_ref[CONV_WIDTH - 1:CONV_WIDTH, :] = jnp.sum(dpre * x, axis=0, keepdims=True)
        for s in range(1, CONV_WIDTH):
            i = CONV_WIDTH - 1 - s
            dx = dx + _shift_up(dpre, s) * w_ref[i:i + 1, :]
            dw_ref[i:i + 1, :] = jnp.sum(dpre * _shift_down(x, s), axis=0, keepdims=True)
        dx_ref[...] = dx.astype(BF16)
        db_ref[...] = jnp.sum(dpre, axis=0, keepdims=True)

    return _pcall(body, name=name, grid=(C // CONV_CB,),
                  in_specs=[pl.BlockSpec((T, CONV_CB), lambda c: (0, OFF_XBC // CONV_CB + c0 + c)),
                            pl.BlockSpec((T, CONV_CB), lambda c: (0, c)),
                            pl.BlockSpec((CONV_WIDTH, CONV_CB), lambda c: (0, c0 + c)), pl.BlockSpec((1, CONV_CB), lambda c: (0, c0 + c))],
                  out_specs=[pl.BlockSpec((T, CONV_CB), lambda c: (0, c)), pl.BlockSpec((CONV_WIDTH, CONV_CB), lambda c: (0, c)),
                             pl.BlockSpec((1, CONV_CB), lambda c: (0, c))],
                  out_shape=[_sds((T, C), BF16), _sds((CONV_WIDTH, C), F32), _sds((1, C), F32)],
                  compiler_params=_cparams(("arbitrary",)))(projp, dact, conv_w, conv_b)


def _softplus(x):
    return jnp.maximum(x, 0.0) + jnp.log1p(jnp.exp(-jnp.abs(x)))


def _tri(lower):
    r = lax.broadcasted_iota(jnp.int32, (BLK, BLK), 0)
    c = lax.broadcasted_iota(jnp.int32, (BLK, BLK), 1)
    return (r >= c) if lower else (c >= r)


def _ssd_chunk_setup(dt_ref, dtb_ref, alog_ref):
    raw = dt_ref[...] + dtb_ref[...]
    dt = _softplus(raw)
    aneg = -jnp.exp(alog_ref[...])
    a = dt * aneg
    acs = jnp.dot(_tri(True).astype(F32), a, precision=lax.Precision.HIGHEST, preferred_element_type=F32)
    return raw, dt, aneg, acs, acs.T


def _ssd_specs(T, rev):
    nc = T // BLK
    ci = (lambda c: nc - 1 - c) if rev else (lambda c: c)
    xs = pl.BlockSpec((BLK, D_INNER), lambda c: (ci(c), 0))
    bm = pl.BlockSpec((BLK, SSM_GROUPS * D_STATE), lambda c: (ci(c), D_INNER // (SSM_GROUPS * D_STATE)))
    cm = pl.BlockSpec((BLK, SSM_GROUPS * D_STATE), lambda c: (ci(c), D_INNER // (SSM_GROUPS * D_STATE) + 1))
    dt = pl.BlockSpec((BLK, DT_PAD), lambda c: (ci(c), OFF_DT // DT_PAD))
    v128 = pl.BlockSpec((1, 128), lambda c: (0, 0))
    dfull = pl.BlockSpec((1, D_INNER), lambda c: (0, 0))
    st = pl.BlockSpec((None, SSM_HEADS, HEAD_DIM, D_STATE), lambda c: (ci(c), 0, 0, 0))
    return xs, bm, cm, dt, v128, dfull, st, ci


def ssd_fwd(xbc, projp, dtb, alog, dfull):
    T = xbc.shape[0]
    nc = T // BLK
    xs, bm, cm, dts, v128, dfs, st, _ = _ssd_specs(T, False)

    def body(xs_ref, b_ref, c_ref, dt_ref, dtb_ref, alog_ref, d_ref, y_ref, st_ref, h_scr):
        c = pl.program_id(0)

        @pl.when(c == 0)
        def _():
            h_scr[...] = jnp.zeros_like(h_scr)

        _, dt, _, acs, acsT = _ssd_chunk_setup(dt_ref, dtb_ref, alog_ref)
        tril = _tri(True)
        for g in range(SSM_GROUPS):
            B = b_ref[:, D_STATE * g:D_STATE * (g + 1)].astype(BF16)
            C = c_ref[:, D_STATE * g:D_STATE * (g + 1)].astype(BF16)
            cb = _dot(C, B, NT)
            for hh in range(HEADS_PER_GROUP):
                h = HEADS_PER_GROUP * g + hh
                hs = slice(HEAD_DIM * h, HEAD_DIM * (h + 1))
                dt_h, acs_h, acsT_h = dt[:, h:h + 1], acs[:, h:h + 1], acsT[h:h + 1, :]
                alast = acs[BLK - 1:BLK, h:h + 1]
                x_h = xs_ref[:, hs]
                xd = x_h * dt_h
                decay = jnp.where(tril, jnp.exp(jnp.where(tril, acs_h - acsT_h, 0.0)), 0.0)
                y = _dot(cb * decay, xd, NN)
                hp = h_scr[h]
                st_ref[h] = hp
                y = y + jnp.exp(acs_h) * _dot(C, hp, NT)
                h_scr[h] = jnp.exp(alast) * hp + _dot(xd * jnp.exp(alast - acs_h), B, TN)
                y_ref[:, hs] = y + d_ref[:, hs] * x_h

    return _pcall(body, name="ssd_fwd", grid=(nc,), in_specs=[xs, bm, cm, dts, v128, v128, dfs],
                  out_specs=[xs, st], out_shape=[_sds((T, D_INNER), F32), _sds((nc, SSM_HEADS, HEAD_DIM, D_STATE), F32)],
                  scratch_shapes=[pltpu.VMEM((SSM_HEADS, HEAD_DIM, D_STATE), F32)],
                  compiler_params=_cparams(("arbitrary",)))(xbc, xbc, xbc, projp, dtb, alog, dfull)


def ssd_bwd(xbc, projp, dtb, alog, dfull, states, dy):
    T = xbc.shape[0]
    nc = T // BLK
    xs, bm, cm, dts, v128, dfs, st, ci = _ssd_specs(T, True)
    gn = SSM_GROUPS * D_STATE

    def body(xs_ref, b_ref, c_ref, dt_ref, dtb_ref, alog_ref, d_ref, st_ref, dy_ref,
             dxs_ref, dB_ref, dC_ref, ddt_ref, dal_ref, dD_ref, ddtb_ref, dh_scr):
        i = pl.program_id(0)

        @pl.when(i == 0)
        def _():
            dh_scr[...] = jnp.zeros_like(dh_scr)
            dal_ref[...] = jnp.zeros_like(dal_ref)
            dD_ref[...] = jnp.zeros_like(dD_ref)
            ddtb_ref[...] = jnp.zeros_like(ddtb_ref)

        raw, dt, aneg, acs, acsT = _ssd_chunk_setup(dt_ref, dtb_ref, alog_ref)
        tril = _tri(True)
        lane1 = lax.broadcasted_iota(jnp.int32, (1, 128), 1)
        lane = lax.broadcasted_iota(jnp.int32, (BLK, 128), 1)
        sub = lax.broadcasted_iota(jnp.int32, (BLK, 128), 0)
        ds_col = jnp.zeros((BLK, 128), F32)
        ds_row = jnp.zeros((BLK, 128), F32)
        ddt_col = jnp.zeros((BLK, 128), F32)
        ds_last = jnp.zeros((1, 128), F32)
        dD = jnp.zeros((1, 128), F32)

        def total(v):
            return jnp.sum(jnp.sum(v, axis=1, keepdims=True), axis=0, keepdims=True)

        for g in range(SSM_GROUPS):
            gs = slice(D_STATE * g, D_STATE * (g + 1))
            B = b_ref[:, gs].astype(BF16)
            C = c_ref[:, gs].astype(BF16)
            cb = _dot(C, B, NT)
            dG = jnp.zeros((BLK, BLK), F32)
            dB_g = jnp.zeros((BLK, D_STATE), F32)
            dC_g = jnp.zeros((BLK, D_STATE), F32)
            for hh in range(HEADS_PER_GROUP):
                h = HEADS_PER_GROUP * g + hh
                hs = slice(HEAD_DIM * h, HEAD_DIM * (h + 1))
                dt_h, acs_h, acsT_h = dt[:, h:h + 1], acs[:, h:h + 1], acsT[h:h + 1, :]
                alast = acs[BLK - 1:BLK, h:h + 1]
                x_h = xs_ref[:, hs]
                dy_h = dy_ref[:, hs]
                xd = x_h * dt_h
                decay = jnp.where(tril, jnp.exp(jnp.where(tril, acs_h - acsT_h, 0.0)), 0.0)
                M = cb * decay
                hc = st_ref[h]
                dS = dh_scr[h]
                w = jnp.exp(alast - acs_h)
                gamma = jnp.exp(alast)
                dD = dD + jnp.where(lane1 == h, total(dy_h * x_h), 0.0)
                dye = dy_h * jnp.exp(acs_h)
                dH_y = _dot(dye, C, TN)
                dC_g = dC_g + _dot(dye, hc, NN)
                ds_h = jnp.sum(dye * _dot(C, hc, NT), axis=1, keepdims=True)
                dM = _dot(dy_h, xd, NT)
                dxd = _dot(M, dy_h, TN)
                Q = dM * M
                ds_h = ds_h + jnp.sum(Q, axis=1, keepdims=True)
                ds_row = jnp.where(sub == h, jnp.sum(Q, axis=0, keepdims=True), ds_row)
                dG = dG + dM * decay
                dxdw = _dot(B, dS, NT)
                dxd = dxd + w * dxdw
                dww = jnp.sum(xd * dxdw, axis=1, keepdims=True) * w
                ds_h = ds_h - dww
                ds_last = ds_last + jnp.where(lane1 == h, jnp.sum(dww, axis=0, keepdims=True) + total(dS * hc) * gamma, 0.0)
                dB_g = dB_g + _dot(xd * w, dS, NN)
                dh_scr[h] = gamma * dS + dH_y
                dxs_ref[:, hs] = d_ref[:, hs] * dy_h + dxd * dt_h
                ddt_col = jnp.where(lane == h, jnp.sum(dxd * x_h, axis=1, keepdims=True), ddt_col)
                ds_col = jnp.where(lane == h, ds_h, ds_col)
            dC_ref[:, gs] = dC_g + _dot(dG, B, NN)
            dB_ref[:, gs] = dB_g + _dot(dG, C, TN)
        ds_all = ds_col - ds_row.T + jnp.where(sub == BLK - 1, ds_last, 0.0)
        da = jnp.dot(_tri(False).astype(F32), ds_all, precision=lax.Precision.HIGHEST, preferred_element_type=F32)
        ddt = ddt_col + da * aneg
        draw = jnp.where(lane < SSM_HEADS, ddt * _sigmoid(raw), 0.0)
        ddt_ref[...] = draw.astype(BF16)
        dal_ref[...] += jnp.sum(da * dt, axis=0, keepdims=True) * aneg
        ddtb_ref[...] += jnp.sum(draw, axis=0, keepdims=True)
        dD_ref[...] += dD

    gblk = pl.BlockSpec((BLK, gn), lambda c: (ci(c), 0))
    return _pcall(body, name="ssd_bwd", grid=(nc,), in_specs=[xs, bm, cm, dts, v128, v128, dfs, st, xs],
                  out_specs=[xs, gblk, gblk, pl.BlockSpec((BLK, DT_PAD), lambda c: (ci(c), 0)), v128, v128, v128],
                  out_shape=[_sds((T, D_INNER), F32), _sds((T, gn), F32), _sds((T, gn), F32), _sds((T, DT_PAD), BF16),
                             _sds((1, 128), F32), _sds((1, 128), F32), _sds((1, 128), F32)],
                  scratch_shapes=[pltpu.VMEM((SSM_HEADS, HEAD_DIM, D_STATE), F32)],
                  compiler_params=_cparams(("arbitrary",)))(xbc, xbc, xbc, projp, dtb, alog, dfull, states, dy)


_WIN_ORDER = ("z", "ga", "gs", "xbc", "q", "k", "v", "dt")


def _win_to_padded(win_g):
    full = win_g.reshape(IN_DIM, D_MODEL)
    rows = []
    for nm in _WIN_ORDER:
        s, w = SEG[nm]
        rows.append(full[s:s + w])
    rows.append(jnp.zeros((DT_PAD - SEG["dt"][1], D_MODEL), win_g.dtype))
    return jnp.concatenate(rows, axis=0)


def _padded_to_win(dw):
    off = dict(z=OFF_Z, ga=OFF_GA, gs=OFF_GS, xbc=OFF_XBC, q=OFF_Q, k=OFF_K, v=OFF_V, dt=OFF_DT)
    rows = [dw[off[nm]:off[nm] + SEG[nm][1]] for nm in ("q", "k", "v", "z", "xbc", "dt", "ga", "gs")]
    return jnp.concatenate(rows, axis=0).reshape(N_DEV, IN_DIM // N_DEV, D_MODEL)


def _pad128(v):
    return jnp.pad(v, ((0, 0), (0, 128 - v.shape[1])))


_SMALL = (("loss", 128, 1), ("g_mix", 2048, 2048), ("conv_b", 3072, 3072), ("dt_bias", 128, 32), ("a_log", 128, 32),
          ("d_skip", 128, 32), ("g_ssd", 2048, 2048), ("sinks", 128, 16), ("g_ffn", 2048, 2048), ("g_ple", 2048, 2048),
          ("g_final", 2048, 2048))


def _small_vec(d):
    parts = []
    for nm, pw, w in _SMALL:
        v = d[nm].reshape(1, -1).astype(F32)
        parts.append(jnp.pad(v[:, :min(v.shape[1], pw)], ((0, 0), (0, pw - min(v.shape[1], pw)))))
    return jnp.concatenate(parts, axis=1)


def _small_split(vec):
    out, o = {}, 0
    for nm, pw, w in _SMALL:
        out[nm] = vec[0, o:o + w]
        o += pw
    return out


def kernel(x, p, positions, g_mix, w_in, conv_w, conv_b, dt_bias, a_log, d_skip, g_ssd, sinks, w_attn_br, w_ssd_br, w_o, g_ffn, w_gate, w_up, w_down, g_ple, w_ple_gate, w_ple_proj, g_final, loss_target, m_g_mix, m_w_in, m_conv_w, m_conv_b, m_dt_bias, m_a_log, m_d_skip, m_g_ssd, m_sinks, m_w_attn_br, m_w_ssd_br, m_w_o, m_g_ffn, m_w_gate, m_w_up, m_w_down, m_g_ple, m_w_ple_gate, m_w_ple_proj, m_g_final, v_g_mix, v_w_in, v_conv_w, v_conv_b, v_dt_bias, v_a_log, v_d_skip, v_g_ssd, v_sinks, v_w_attn_br, v_w_ssd_br, v_w_o, v_g_ffn, v_w_gate, v_w_up, v_w_down, v_g_ple, v_w_ple_gate, v_w_ple_proj, v_g_final):
    T = x.shape[1]
    D = D_MODEL
    W = dict(g_mix=g_mix, w_in=w_in, conv_w=conv_w, conv_b=conv_b, dt_bias=dt_bias, a_log=a_log, d_skip=d_skip, g_ssd=g_ssd,
             sinks=sinks, w_attn_br=w_attn_br, w_ssd_br=w_ssd_br, w_o=w_o, g_ffn=g_ffn, w_gate=w_gate, w_up=w_up, w_down=w_down,
             g_ple=g_ple, w_ple_gate=w_ple_gate, w_ple_proj=w_ple_proj, g_final=g_final)
    Mo = dict(g_mix=m_g_mix, w_in=m_w_in, conv_w=m_conv_w, conv_b=m_conv_b, dt_bias=m_dt_bias, a_log=m_a_log, d_skip=m_d_skip,
              g_ssd=m_g_ssd, sinks=m_sinks, w_attn_br=m_w_attn_br, w_ssd_br=m_w_ssd_br, w_o=m_w_o, g_ffn=m_g_ffn, w_gate=m_w_gate,
              w_up=m_w_up, w_down=m_w_down, g_ple=m_g_ple, w_ple_gate=m_w_ple_gate, w_ple_proj=m_w_ple_proj, g_final=m_g_final)
    Vo = dict(g_mix=v_g_mix, w_in=v_w_in, conv_w=v_conv_w, conv_b=v_conv_b, dt_bias=v_dt_bias, a_log=v_a_log, d_skip=v_d_skip,
              g_ssd=v_g_ssd, sinks=v_sinks, w_attn_br=v_w_attn_br, w_ssd_br=v_w_ssd_br, w_o=v_w_o, g_ffn=v_g_ffn, w_gate=v_w_gate,
              w_up=v_w_up, w_down=v_w_down, g_ple=v_g_ple, w_ple_gate=v_w_ple_gate, w_ple_proj=v_w_ple_proj, g_final=v_g_final)
    order = ["g_mix", "w_in", "conv_w", "conv_b", "dt_bias", "a_log", "d_skip", "g_ssd", "sinks", "w_attn_br", "w_ssd_br", "w_o",
             "g_ffn", "w_gate", "w_up", "w_down", "g_ple", "w_ple_gate", "w_ple_proj", "g_final"]
    big = ["w_in", "conv_w", "w_attn_br", "w_ssd_br", "w_o", "w_gate", "w_up", "w_down", "w_ple_gate", "w_ple_proj"]

    x2 = x.reshape(T, D)
    p2 = p.reshape(T, PLE_DIM)
    tgt = loss_target.reshape(T, D)
    posf = positions.reshape(T, 1).astype(F32)
    inv = ROPE_THETA ** (-np.arange(HEAD_DIM // 2, dtype=np.float32) * 2.0 / HEAD_DIM)
    inv128 = jnp.asarray(np.tile(inv, 128 // (HEAD_DIM // 2)).reshape(1, 128).astype(np.float32))
    sh = {n: W[n].reshape(W[n].shape[-2:]) for n in big}

    del _PENDING[:]
    groups = (("w_in",), ("conv_w", "w_attn_br", "w_ssd_br", "w_o"), ("w_gate", "w_up", "w_down"), ("w_ple_gate", "w_ple_proj"))
    send = {n: sh[n] if n == "conv_w" else sh[n].astype(BF16) for n in big}
    send["w_in"] = sh["w_in"].T.astype(BF16)
    started, prev = [], None
    for gi, grp in enumerate(groups):
        h = split_start("gather_start_%d" % gi, "gather", ICI_SAME_CORE, [send[n] for n in grp], after=prev)
        prev = h["token"]
        started.append(h)
    gathered, fwd = {}, {}

    def forward_start(gi, after):
        srcs, lands = split_wait("gather_wait_%d" % gi, started[gi], after)
        lands = place_own("gather_own_%d" % gi, srcs, lands)
        fwd[gi] = split_start("forward_start_%d" % gi, "forward", FORWARD_BLOCKS, [], lands=lands)

    def forward_wait(gi, after):
        _, full = split_wait("forward_wait_%d" % gi, fwd[gi], after)
        gathered.update(zip(groups[gi], full))

    u = rms_fwd("norm_mix", x2, g_mix)
    forward_start(0, u)
    forward_wait(0, u)
    forward_start(1, u)
    winp = _win_to_padded(gathered["w_in"])
    dtb = _pad128(dt_bias)
    alog = _pad128(a_log)
    dfull = jnp.repeat(d_skip.reshape(SSM_HEADS), HEAD_DIM).reshape(1, D_INNER)

    projp = mm_nt("in_proj", u, winp, 640)
    attn, qr, kr = attn_fwd(projp, posf, inv128, sinks)
    forward_wait(1, attn)
    convw = jnp.transpose(gathered["conv_w"], (1, 0, 2)).reshape(CONV_WIDTH, CONV_DIM)
    wab = gathered["w_attn_br"]
    wsb = gathered["w_ssd_br"].reshape(D, D)
    wo = gathered["w_o"].reshape(D, D)
    xbc = conv_fwd(projp, convw, conv_b)
    y, states = ssd_fwd(xbc, projp, dtb, alog, dfull)
    forward_start(2, y)
    yn = gnorm_fwd(y, projp, g_ssd)
    out_a = mm_nn_colblk("attn_br", attn, wab)
    out_s = mm_nn("ssd_br", yn, wsb, 512)
    merged = merge_fwd(projp, out_a, out_s)
    h1 = mm_nn("o_proj", merged, wo, 512, residual=x2)
    f = rms_fwd("norm_ffn", h1, g_ffn)
    forward_wait(2, f)
    wg, wu, wd = gathered["w_gate"], gathered["w_up"], gathered["w_down"]
    gate, up, act = ffn_up(f, wg, wu)
    forward_start(3, act)
    h2 = ffn_down(act, wd, h1)
    r = rms_fwd("norm_ple", h2, g_ple)
    forward_wait(3, r)
    wpg = gathered["w_ple_gate"].reshape(D, D)
    wpp = gathered["w_ple_proj"]
    pg = mm_nn("ple_gate", r, wpg, 512)
    pp = mm_nn_colblk("ple_proj", p2, wpp)
    loss_v, dh3, dpg, dpp, dg_final = head_fwd_bwd(h2, pg, pp, g_final.reshape(1, D), tgt)

    gw = {}
    scat = []

    def scatter_start(names):
        scat.append((names, split_start("scatter_start_%d" % len(scat), "scatter", ALL_PEERS, [gw[n] for n in names])))

    gw["w_ple_proj"] = mm_tn_colblk("dw_ple_proj", p2, dpp, PLE_DIM)
    dr = mm_nt("d_ple_gate", dpg, wpg, 512)
    gw["w_ple_gate"] = mm_tn("dw_ple_gate", r, dpg, 512, 1024).reshape(N_DEV, D // N_DEV, D)
    scatter_start(("w_ple_proj", "w_ple_gate"))
    dh2, dh2b, dg_ple = rms_bwd("norm_ple_bwd", h2, g_ple, dr, dh3)
    dgate, dup = ffn_down_bwd(dh2b, wd, gate, up)
    gw["w_down"] = wgrad_rowblk_lhs("dw_down", act, dh2b, 1024)
    gw["w_gate"] = wgrad_colblk_rhs("dw_gate", f, dgate, 1024)
    gw["w_up"] = wgrad_colblk_rhs("dw_up", f, dup, 1024)
    scatter_start(("w_down", "w_gate", "w_up"))
    df = ffn_up_bwd(dgate, dup, wg, wu)
    dh1, dh1b, dg_ffn = rms_bwd("norm_ffn_bwd", h1, g_ffn, df, dh2)
    dmerged = mm_nt("d_o_proj", dh1b, wo, 512)
    gw["w_o"] = mm_tn("dw_o", merged, dh1b, 512, 1024).reshape(N_DEV, D // N_DEV, D)
    dout_a, dout_s, dga, dgs = merge_bwd(projp, out_a, out_s, dmerged)
    gw["w_ssd_br"] = mm_tn("dw_ssd_br", yn, dout_s, 512, 1024).reshape(N_DEV, D // N_DEV, D)
    gw["w_attn_br"] = mm_tn_colblk("dw_attn_br", attn, dout_a, D // N_DEV)
    scatter_start(("w_o", "w_ssd_br", "w_attn_br"))
    dyn = mm_nt("d_ssd_br", dout_s, wsb, 512)
    dattn = attn_br_bwd(dout_a, wab)
    dy, dz, dg_ssd = gnorm_bwd(y, projp, g_ssd, dyn)
    dxs, dbm, dcm, ddt, dal, ddsk, ddtb = ssd_bwd(xbc, projp, dtb, alog, dfull, states, dy)
    dx_x, dwc_x, dbc_x = conv_bwd("conv_bwd_x", projp, dxs, convw, conv_b, 0)
    dx_b, dwc_b, dbc_b = conv_bwd("conv_bwd_b", projp, dbm, convw, conv_b, D_INNER)
    dx_c, dwc_c, dbc_c = conv_bwd("conv_bwd_c", projp, dcm, convw, conv_b, D_INNER + SSM_GROUPS * D_STATE)
    dq, dk, dv, dsk = attn_bwd(qr, kr, projp, dattn, posf, inv128, sinks)
    dproj = jnp.concatenate([dz, dga, dgs, dx_x, dx_b, dx_c, dq, dk.astype(BF16), dv.astype(BF16), ddt], axis=1)
    gw["w_in"] = _padded_to_win(mm_tn("dw_in", dproj, u, 640, 1024))
    dconvw = jnp.concatenate([dwc_x, dwc_b, dwc_c], axis=1)
    gw["conv_w"] = jnp.transpose(dconvw.reshape(CONV_WIDTH, N_DEV, CONV_DIM // N_DEV), (1, 0, 2))
    scatter_start(("w_in", "conv_w"))
    du = mm_nn_red("d_in_proj", dproj, winp, 1024, 640)
    gx, _, dg_mix = rms_bwd("norm_mix_bwd", x2, g_mix, du, dh1)

    me = 4 * lax.axis_index("x") + 2 * lax.axis_index("y") + lax.axis_index("c")
    res = {}
    after = gx
    for si, (names, h) in enumerate(scat):
        srcs, lands = split_wait("scatter_wait_%d" % si, h, after)
        for n, mine, arrived in zip(names, srcs, lands):
            own = lax.dynamic_index_in_dim(mine, me, 0, keepdims=False)
            if n == "w_in":
                arrived, own = None, sum_partials("sum_w_in", arrived, own, 256).T
            res[n] = adamw("adamw_" + n, arrived, sh[n], Mo[n].reshape(sh[n].shape), Vo[n].reshape(sh[n].shape), own=own)
        after = res[names[0]][0]

    small_g = dict(loss=loss_v[:, :1], g_mix=dg_mix, conv_b=jnp.concatenate([dbc_x, dbc_b, dbc_c], axis=1), dt_bias=ddtb,
                   a_log=dal, d_skip=ddsk, g_ssd=dg_ssd, sinks=dsk, g_ffn=dg_ffn, g_ple=dg_ple, g_final=dg_final)
    zero = jnp.zeros((1, 1), F32)
    vec_parts = exchange("gather_small", [_small_vec(small_g)], "gather")[0]
    sres = adamw("adamw_small", vec_parts, _small_vec({**W, "loss": zero}), _small_vec({**Mo, "loss": zero}),
                 _small_vec({**Vo, "loss": zero}))
    ssplit = [_small_split(a) for a in sres]
    loss = ssplit[0]["loss"].reshape(())
    for n in order:
        if n not in res:
            res[n] = tuple(s[n].reshape(W[n].shape) for s in ssplit)
        else:
            res[n] = tuple(a.reshape(W[n].shape) for a in res[n])
    outs = [loss, gx.reshape(x.shape)]
    for k in range(4):
        outs += [res[n][k] for n in order]
    return tuple(outs)
```

```python
import functools

import numpy as np
import jax
import jax.numpy as jnp
from jax import lax
from jax.experimental import pallas as pl
from jax.experimental.pallas import tpu as pltpu

F32 = jnp.float32
BF16 = jnp.bfloat16

N_DEV = 8
D_MODEL = 2048
HEAD_DIM = 64
ATTN_HEADS = 16
KV_HEADS = 4
Q_DIM = 1024
KV_DIM = 256
BLK = 128
D_INNER = 2048
SSM_HEADS = 32
SSM_GROUPS = 4
HEADS_PER_GROUP = 8
D_STATE = 128
CONV_WIDTH = 4
CONV_DIM = 3072
FFN_HIDDEN = 5632
PLE_DIM = 256
IN_DIM = 10784
NORM_EPS = 1e-6
SSM_NORM_EPS = 1e-5
ROPE_THETA = 10000.0

OFF_Z, OFF_GA, OFF_GS, OFF_XBC, OFF_Q, OFF_K, OFF_V, OFF_DT = 0, 2048, 4096, 6144, 9216, 10240, 10496, 10752
IN_PAD = 10880
DT_PAD = 128
SEG = dict(q=(0, 1024), k=(1024, 256), v=(1280, 256), z=(1536, 2048), xbc=(3584, 3072), dt=(6656, 32),
           ga=(6688, 2048), gs=(8736, 2048))

ADAM_LR, ADAM_B1, ADAM_B2, ADAM_EPS, ADAM_WD, ADAM_STEP = 0.001, 0.9, 0.999, 1e-08, 0.01, 10

VMEM_LIMIT = 56 * 1024 * 1024

NN = (((1,), (0,)), ((), ()))
NT = (((1,), (1,)), ((), ()))
TN = (((0,), (0,)), ((), ()))


_PENDING = []


def _raw_call(body, **kw):
    return pl.pallas_call(body, **kw)


def _pcall(body, **kw):
    deps = list(_PENDING)
    del _PENDING[:]
    if not deps:
        return _raw_call(body, **kw)
    n_in = len(kw["in_specs"])

    def tied(*refs):
        return body(*refs[:n_in], *refs[n_in + len(deps):])

    kw["in_specs"] = list(kw["in_specs"]) + [pl.BlockSpec(memory_space=pl.ANY)] * len(deps)
    call = _raw_call(tied, **kw)
    return lambda *ops: call(*ops, *deps)


def _cparams(sem=None):
    if sem is None:
        return pltpu.CompilerParams(vmem_limit_bytes=VMEM_LIMIT)
    return pltpu.CompilerParams(vmem_limit_bytes=VMEM_LIMIT, dimension_semantics=sem)


def _dot(a, b, dn):
    return lax.dot_general(a.astype(BF16), b.astype(BF16), dn, preferred_element_type=F32)


def _sigmoid(x):
    return 1.0 / (1.0 + jnp.exp(-x))


def _silu(x):
    return x * _sigmoid(x)


def _dsilu(x):
    s = _sigmoid(x)
    return s * (1.0 + x * (1.0 - s))


def _matmul(name, pairs, pair_specs, dn, grid, out_shapes, out_specs, nred=1, extra=(), extra_specs=(),
            epilogue=None, acc_shape=None):
    n_in = 2 * len(pairs) + len(extra)
    n_out = len(out_shapes)

    def body(*refs):
        ins = refs[:2 * len(pairs)]
        ex = refs[2 * len(pairs):n_in]
        outs = refs[n_in:n_in + n_out]

        def prod():
            s = None
            for p in range(len(pairs)):
                d = _dot(ins[2 * p][...], ins[2 * p + 1][...], dn)
                s = d if s is None else s + d
            return s

        def finish(val):
            if epilogue is None:
                outs[0][...] = val.astype(outs[0].dtype)
            else:
                res = epilogue(val, *[e[...] for e in ex])
                for o, r in zip(outs, res):
                    o[...] = r.astype(o.dtype)

        if nred == 1:
            finish(prod())
        else:
            acc = refs[n_in + n_out]
            k = pl.program_id(len(grid) - 1)

            @pl.when(k == 0)
            def _():
                acc[...] = jnp.zeros_like(acc)

            acc[...] += prod()

            @pl.when(k == nred - 1)
            def _():
                finish(acc[...])

    operands = []
    specs = []
    for (a, b), (sa, sb) in zip(pairs, pair_specs):
        operands += [a, b]
        specs += [sa, sb]
    operands += list(extra)
    specs += list(extra_specs)
    scratch = [pltpu.VMEM(acc_shape, F32)] if nred > 1 else []
    sem = ("arbitrary",) * len(grid)
    res = _pcall(body, name=name, grid=grid, in_specs=specs, out_specs=list(out_specs),
                 out_shape=list(out_shapes), scratch_shapes=scratch, compiler_params=_cparams(sem))(*operands)
    return res


def _sds(shape, dtype):
    return jax.ShapeDtypeStruct(shape, dtype)


def _row_tile(T):
    return min(1024, T)


def mm_nn(name, a, b, tn, out_dtype=F32, residual=None):
    M, K = a.shape
    N = b.shape[1]
    tm = _row_tile(M)
    grid = (M // tm, N // tn)
    extra, especs, epi = (), (), None
    if residual is not None:
        extra = (residual,)
        especs = (pl.BlockSpec((tm, tn), lambda i, n: (i, n)),)
        epi = lambda v, r: (v + r,)
    return _matmul(name, [(a, b)], [(pl.BlockSpec((tm, K), lambda i, n: (i, 0)), pl.BlockSpec((K, tn), lambda i, n: (0, n)))],
                   NN, grid, [_sds((M, N), out_dtype)], [pl.BlockSpec((tm, tn), lambda i, n: (i, n))],
                   extra=extra, extra_specs=especs, epilogue=epi)[0]


def mm_nn_colblk(name, a, b, out_dtype=F32):
    M, K = a.shape
    J, _, nb = b.shape
    tm = _row_tile(M)
    grid = (M // tm, J)
    return _matmul(name, [(a, b)], [(pl.BlockSpec((tm, K), lambda i, j: (i, 0)), pl.BlockSpec((None, K, nb), lambda i, j: (j, 0, 0)))],
                   NN, grid, [_sds((M, J * nb), out_dtype)], [pl.BlockSpec((tm, nb), lambda i, j: (i, j))])[0]


def mm_nt(name, a, w, tr, out_dtype=F32):
    M, C = a.shape
    R = w.shape[0]
    tm = _row_tile(M)
    grid = (M // tm, R // tr)
    return _matmul(name, [(a, w)], [(pl.BlockSpec((tm, C), lambda i, r: (i, 0)), pl.BlockSpec((tr, C), lambda i, r: (r, 0)))],
                   NT, grid, [_sds((M, R), out_dtype)], [pl.BlockSpec((tm, tr), lambda i, r: (i, r))])[0]


def mm_nt_red(name, a, w, tr, tk, out_dtype=F32):
    M, C = a.shape
    R = w.shape[0]
    tm = _row_tile(M)
    nk = C // tk
    grid = (M // tm, R // tr, nk)
    return _matmul(name, [(a, w)], [(pl.BlockSpec((tm, tk), lambda i, r, k: (i, k)), pl.BlockSpec((tr, tk), lambda i, r, k: (r, k)))],
                   NT, grid, [_sds((M, R), out_dtype)], [pl.BlockSpec((tm, tr), lambda i, r, k: (i, r))],
                   nred=nk, acc_shape=(tm, tr))[0]


def mm_nn_red(name, a, b, tn, tk, out_dtype=F32):
    M, K = a.shape
    N = b.shape[1]
    tm = _row_tile(M)
    nk = K // tk
    grid = (M // tm, N // tn, nk)
    return _matmul(name, [(a, b)], [(pl.BlockSpec((tm, tk), lambda i, n, k: (i, k)), pl.BlockSpec((tk, tn), lambda i, n, k: (k, n)))],
                   NN, grid, [_sds((M, N), out_dtype)], [pl.BlockSpec((tm, tn), lambda i, n, k: (i, n))],
                   nred=nk, acc_shape=(tm, tn))[0]


def mm_tn(name, x, dy, tr, tc, out_dtype=BF16):
    M, R = x.shape
    C = dy.shape[1]
    grid = (R // tr, C // tc)
    return _matmul(name, [(x, dy)], [(pl.BlockSpec((M, tr), lambda r, c: (0, r)), pl.BlockSpec((M, tc), lambda r, c: (0, c)))],
                   TN, grid, [_sds((R, C), out_dtype)], [pl.BlockSpec((tr, tc), lambda r, c: (r, c))])[0]


def mm_tn_colblk(name, x, dy, nb, out_dtype=BF16):
    M, R = x.shape
    J = dy.shape[1] // nb
    grid = (J,)
    return _matmul(name, [(x, dy)], [(pl.BlockSpec((M, R), lambda j: (0, 0)), pl.BlockSpec((M, nb), lambda j: (0, j)))],
                   TN, grid, [_sds((J, R, nb), out_dtype)], [pl.BlockSpec((None, R, nb), lambda j: (j, 0, 0))])[0]


def _rows(T):
    return min(256, T)


def rms_fwd(name, x, g, eps=NORM_EPS):
    T, D = x.shape
    tm = _rows(T)

    def body(x_ref, g_ref, o_ref):
        xv = x_ref[...]
        r = lax.rsqrt(jnp.mean(xv * xv, axis=-1, keepdims=True) + eps)
        o_ref[...] = (xv * r * g_ref[...]).astype(BF16)

    return _pcall(body, name=name, grid=(T // tm,),
                  in_specs=[pl.BlockSpec((tm, D), lambda i: (i, 0)), pl.BlockSpec((1, D), lambda i: (0, 0))],
                  out_specs=pl.BlockSpec((tm, D), lambda i: (i, 0)), out_shape=_sds((T, D), BF16),
                  compiler_params=_cparams(("arbitrary",)))(x, g)


def rms_bwd(name, x, g, dy, dres, eps=NORM_EPS):
    T, D = x.shape
    tm = _rows(T)

    def body(x_ref, g_ref, dy_ref, dr_ref, dx_ref, dxb_ref, dg_ref):
        i = pl.program_id(0)
        xv = x_ref[...]
        r = lax.rsqrt(jnp.mean(xv * xv, axis=-1, keepdims=True) + eps)
        xh = xv * r
        dyv = dy_ref[...]
        gd = dyv * g_ref[...]
        dx = r * (gd - xh * jnp.mean(gd * xh, axis=-1, keepdims=True)) + dr_ref[...]
        dx_ref[...] = dx
        dxb_ref[...] = dx.astype(BF16)

        @pl.when(i == 0)
        def _():
            dg_ref[...] = jnp.zeros_like(dg_ref)

        dg_ref[...] += jnp.sum(dyv * xh, axis=0, keepdims=True)

    row = pl.BlockSpec((tm, D), lambda i: (i, 0))
    vec = pl.BlockSpec((1, D), lambda i: (0, 0))
    return _pcall(body, name=name, grid=(T // tm,), in_specs=[row, vec, row, row], out_specs=[row, row, vec],
                  out_shape=[_sds((T, D), F32), _sds((T, D), BF16), _sds((1, D), F32)],
                  compiler_params=_cparams(("arbitrary",)))(x, g, dy, dres)


def gnorm_fwd(y, projp, g):
    T, D = y.shape
    tm = _rows(T)

    def body(y_ref, z_ref, g_ref, o_ref):
        yz = y_ref[...] * _silu(z_ref[...])
        r = lax.rsqrt(jnp.mean(yz * yz, axis=-1, keepdims=True) + SSM_NORM_EPS)
        o_ref[...] = (yz * r * g_ref[...]).astype(BF16)

    row = pl.BlockSpec((tm, D), lambda i: (i, 0))
    return _pcall(body, name="gnorm_fwd", grid=(T // tm,),
                  in_specs=[row, pl.BlockSpec((tm, D), lambda i: (i, OFF_Z // D)), pl.BlockSpec((1, D), lambda i: (0, 0))],
                  out_specs=row, out_shape=_sds((T, D), BF16), compiler_params=_cparams(("arbitrary",)))(y, projp, g)


def gnorm_bwd(y, projp, g, dyn):
    T, D = y.shape
    tm = _rows(T)

    def body(y_ref, z_ref, g_ref, dyn_ref, dy_ref, dz_ref, dg_ref):
        i = pl.program_id(0)
        yv, zv = y_ref[...], z_ref[...]
        sz = _silu(zv)
        yz = yv * sz
        r = lax.rsqrt(jnp.mean(yz * yz, axis=-1, keepdims=True) + SSM_NORM_EPS)
        xh = yz * r
        dv = dyn_ref[...]
        gd = dv * g_ref[...]
        dyz = r * (gd - xh * jnp.mean(gd * xh, axis=-1, keepdims=True))
        dy_ref[...] = dyz * sz
        dz_ref[...] = (dyz * yv * _dsilu(zv)).astype(BF16)

        @pl.when(i == 0)
        def _():
            dg_ref[...] = jnp.zeros_like(dg_ref)

        dg_ref[...] += jnp.sum(dv * xh, axis=0, keepdims=True)

    row = pl.BlockSpec((tm, D), lambda i: (i, 0))
    vec = pl.BlockSpec((1, D), lambda i: (0, 0))
    return _pcall(body, name="gnorm_bwd", grid=(T // tm,),
                  in_specs=[row, pl.BlockSpec((tm, D), lambda i: (i, OFF_Z // D)), vec, row], out_specs=[row, row, vec],
                  out_shape=[_sds((T, D), F32), _sds((T, D), BF16), _sds((1, D), F32)],
                  compiler_params=_cparams(("arbitrary",)))(y, projp, g, dyn)


def merge_fwd(projp, out_a, out_s):
    T, D = out_a.shape
    tm = _rows(T)

    def body(ga_ref, gs_ref, a_ref, s_ref, o_ref):
        o_ref[...] = (_sigmoid(ga_ref[...]) * a_ref[...] + _sigmoid(gs_ref[...]) * s_ref[...]).astype(BF16)

    row = pl.BlockSpec((tm, D), lambda i: (i, 0))
    return _pcall(body, name="merge_fwd", grid=(T // tm,),
                  in_specs=[pl.BlockSpec((tm, D), lambda i: (i, OFF_GA // D)), pl.BlockSpec((tm, D), lambda i: (i, OFF_GS // D)), row, row],
                  out_specs=row, out_shape=_sds((T, D), BF16), compiler_params=_cparams(("arbitrary",)))(projp, projp, out_a, out_s)


def merge_bwd(projp, out_a, out_s, dmerged):
    T, D = out_a.shape
    tm = _rows(T)

    def body(ga_ref, gs_ref, a_ref, s_ref, dm_ref, da_ref, ds_ref, dga_ref, dgs_ref):
        dm = dm_ref[...]
        sa, ss = _sigmoid(ga_ref[...]), _sigmoid(gs_ref[...])
        da_ref[...] = (dm * sa).astype(BF16)
        ds_ref[...] = (dm * ss).astype(BF16)
        dga_ref[...] = (dm * a_ref[...] * sa * (1.0 - sa)).astype(BF16)
        dgs_ref[...] = (dm * s_ref[...] * ss * (1.0 - ss)).astype(BF16)

    row = pl.BlockSpec((tm, D), lambda i: (i, 0))
    return _pcall(body, name="merge_bwd", grid=(T // tm,),
                  in_specs=[pl.BlockSpec((tm, D), lambda i: (i, OFF_GA // D)), pl.BlockSpec((tm, D), lambda i: (i, OFF_GS // D)), row, row, row],
                  out_specs=[row] * 4, out_shape=[_sds((T, D), BF16)] * 4,
                  compiler_params=_cparams(("arbitrary",)))(projp, projp, out_a, out_s, dmerged)


def head_fwd_bwd(h2, pg, pp, g_final, target):
    T, D = h2.shape
    tm = _rows(T)

    def body(h_ref, pg_ref, pp_ref, g_ref, t_ref, loss_ref, dh_ref, dpg_ref, dpp_ref, dg_ref):
        i = pl.program_id(0)
        s = _sigmoid(pg_ref[...])
        ppv = pp_ref[...]
        h3 = h_ref[...] + s * ppv
        r = lax.rsqrt(jnp.mean(h3 * h3, axis=-1, keepdims=True) + NORM_EPS)
        xh = h3 * r
        gv = g_ref[...]
        e = xh * gv - t_ref[...]
        dyo = e * (1.0 / D)
        gd = dyo * gv
        dh = r * (gd - xh * jnp.mean(gd * xh, axis=-1, keepdims=True))
        dh_ref[...] = dh
        dpg_ref[...] = (dh * ppv * s * (1.0 - s)).astype(BF16)
        dpp_ref[...] = (dh * s).astype(BF16)

        @pl.when(i == 0)
        def _():
            dg_ref[...] = jnp.zeros_like(dg_ref)
            loss_ref[...] = jnp.zeros_like(loss_ref)

        dg_ref[...] += jnp.sum(dyo * xh, axis=0, keepdims=True)
        part = 0.5 * jnp.sum(jnp.mean(e * e, axis=-1, keepdims=True), axis=0, keepdims=True)
        loss_ref[...] += jnp.broadcast_to(part, loss_ref.shape)

    row = pl.BlockSpec((tm, D), lambda i: (i, 0))
    vec = pl.BlockSpec((1, D), lambda i: (0, 0))
    return _pcall(body, name="head_fwd_bwd", grid=(T // tm,), in_specs=[row, row, row, vec, row],
                  out_specs=[pl.BlockSpec((1, 128), lambda i: (0, 0)), row, row, row, vec],
                  out_shape=[_sds((1, 128), F32), _sds((T, D), F32), _sds((T, D), BF16), _sds((T, D), BF16), _sds((1, D), F32)],
                  compiler_params=_cparams(("arbitrary",)))(h2, pg, pp, g_final, target)


def ffn_up(f, wg, wu):
    T, D = f.shape
    J, _, nb = wg.shape
    tm = _row_tile(T)

    def body(f_ref, wg_ref, wu_ref, g_ref, u_ref, a_ref):
        fv = f_ref[...]
        g = _dot(fv, wg_ref[...], NN)
        u = _dot(fv, wu_ref[...], NN)
        g_ref[...] = g
        u_ref[...] = u
        a_ref[...] = (_silu(g) * u).astype(BF16)

    wspec = pl.BlockSpec((None, D, nb), lambda i, j: (j, 0, 0))
    ospec = pl.BlockSpec((None, tm, nb), lambda i, j: (j, i, 0))
    return _pcall(body, name="ffn_up", grid=(T // tm, J), in_specs=[pl.BlockSpec((tm, D), lambda i, j: (i, 0)), wspec, wspec],
                  out_specs=[ospec] * 3, out_shape=[_sds((J, T, nb), F32), _sds((J, T, nb), F32), _sds((J, T, nb), BF16)],
                  compiler_params=_cparams(("arbitrary", "arbitrary")))(f, wg, wu)


def ffn_down(act, wd, h1):
    J, T, nb = act.shape
    D = wd.shape[2]
    tm = _row_tile(T)
    tn = 1024
    grid = (T // tm, D // tn, J)
    return _matmul("ffn_down", [(act, wd)],
                   [(pl.BlockSpec((None, tm, nb), lambda i, n, j: (j, i, 0)), pl.BlockSpec((None, nb, tn), lambda i, n, j: (j, 0, n)))],
                   NN, grid, [_sds((T, D), F32)], [pl.BlockSpec((tm, tn), lambda i, n, j: (i, n))], nred=J, acc_shape=(tm, tn),
                   extra=(h1,), extra_specs=(pl.BlockSpec((tm, tn), lambda i, n, j: (i, n)),), epilogue=lambda v, r: (v + r,))[0]


def ffn_down_bwd(dh2b, wd, gate, up):
    T, D = dh2b.shape
    J, nb, _ = wd.shape
    tm = _row_tile(T)
    ospec = pl.BlockSpec((None, tm, nb), lambda i, j: (j, i, 0))

    def epi(da, g, u):
        return (da * u * _dsilu(g), da * _silu(g))

    return _matmul("ffn_down_bwd", [(dh2b, wd)],
                   [(pl.BlockSpec((tm, D), lambda i, j: (i, 0)), pl.BlockSpec((None, nb, D), lambda i, j: (j, 0, 0)))],
                   NT, (T // tm, J), [_sds((J, T, nb), BF16)] * 2, [ospec, ospec],
                   extra=(gate, up), extra_specs=(ospec, ospec), epilogue=epi)


def ffn_up_bwd(dgate, dup, wg, wu):
    J, T, nb = dgate.shape
    D = wg.shape[1]
    tm = _row_tile(T)
    tr = 1024
    aspec = pl.BlockSpec((None, tm, nb), lambda i, r, j: (j, i, 0))
    wspec = pl.BlockSpec((None, tr, nb), lambda i, r, j: (j, r, 0))
    return _matmul("ffn_up_bwd", [(dgate, wg), (dup, wu)], [(aspec, wspec), (aspec, wspec)], NT, (T // tm, D // tr, J),
                   [_sds((T, D), F32)], [pl.BlockSpec((tm, tr), lambda i, r, j: (i, r))], nred=J, acc_shape=(tm, tr))[0]


def wgrad_rowblk_lhs(name, xb, dy, tc):
    J, T, nb = xb.shape
    C = dy.shape[1]
    return _matmul(name, [(xb, dy)],
                   [(pl.BlockSpec((None, T, nb), lambda j, c: (j, 0, 0)), pl.BlockSpec((T, tc), lambda j, c: (0, c)))],
                   TN, (J, C // tc), [_sds((J, nb, C), BF16)], [pl.BlockSpec((None, nb, tc), lambda j, c: (j, 0, c))])[0]


def wgrad_colblk_rhs(name, x, dyb, tr):
    T, R = x.shape
    J, _, nb = dyb.shape
    return _matmul(name, [(x, dyb)],
                   [(pl.BlockSpec((T, tr), lambda j, r: (0, r)), pl.BlockSpec((None, T, nb), lambda j, r: (j, 0, 0)))],
                   TN, (J, R // tr), [_sds((J, R, nb), BF16)], [pl.BlockSpec((None, tr, nb), lambda j, r: (j, r, 0))])[0]


def attn_br_bwd(dout_a, wab):
    T, D = dout_a.shape
    J, R, nb = wab.shape
    tm = _row_tile(T)
    return _matmul("attn_br_bwd", [(dout_a, wab)],
                   [(pl.BlockSpec((tm, nb), lambda i, j: (i, j)), pl.BlockSpec((None, R, nb), lambda i, j: (j, 0, 0)))],
                   NT, (T // tm, J), [_sds((T, R), BF16)], [pl.BlockSpec((tm, R), lambda i, j: (i, 0))], nred=J, acc_shape=(tm, R))[0]


def _adam_math(w, g, m, v):
    m2 = ADAM_B1 * m + (1.0 - ADAM_B1) * g
    v2 = ADAM_B2 * v + (1.0 - ADAM_B2) * (g * g)
    m_hat = m2 / (1.0 - ADAM_B1 ** ADAM_STEP)
    v_hat = v2 / (1.0 - ADAM_B2 ** ADAM_STEP)
    delta = -ADAM_LR * (m_hat / (jnp.sqrt(v_hat) + ADAM_EPS) + ADAM_WD * w)
    return delta, m2, v2


def _sum_partials(own, parts):
    g = None if own is None else own.astype(F32)
    if parts is not None:
        for s in range(parts.shape[0]):
            t = parts[s].astype(F32)
            g = t if g is None else g + t
    return g


def adamw(name, parts, w, m, v, own=None):
    R, C = w.shape
    tr = R
    for cand in (256, 176, 128, 64, 32, 16, 8):
        if R % cand == 0 and R > cand:
            tr = cand
            break
    given = [a for a in (parts, own) if a is not None]

    def body(*refs):
        p_ref = refs[0] if parts is not None else None
        o_ref = refs[len(given) - 1] if own is not None else None
        w_ref, m_ref, v_ref, g_ref, d_ref, m2_ref, v2_ref = refs[-7:]
        g = _sum_partials(None if o_ref is None else o_ref[...], p_ref)
        d, m2, v2 = _adam_math(w_ref[...], g, m_ref[...], v_ref[...])
        g_ref[...] = g
        d_ref[...] = d
        m2_ref[...] = m2
        v2_ref[...] = v2

    blk = pl.BlockSpec((tr, C), lambda i: (i, 0))
    specs = ([] if parts is None else [pl.BlockSpec((parts.shape[0], tr, C), lambda i: (0, i, 0))]) + [blk] * (3 + (own is not None))
    return _pcall(body, name=name, grid=(R // tr,), in_specs=specs,
                  out_specs=[blk] * 4, out_shape=[_sds((R, C), F32)] * 4,
                  compiler_params=_cparams(("arbitrary",)))(*given, w, m, v)


def sum_partials(name, parts, own, tc):
    R, C = own.shape

    def body(p_ref, o_ref, g_ref):
        g_ref[...] = _sum_partials(o_ref[...], p_ref)

    blk = pl.BlockSpec((R, tc), lambda i: (0, i))
    return _pcall(body, name=name, grid=(C // tc,), in_specs=[pl.BlockSpec((parts.shape[0], R, tc), lambda i: (0, 0, i)), blk],
                  out_specs=blk, out_shape=_sds((R, C), F32), compiler_params=_cparams(("arbitrary",)))(parts, own)


def exchange(name, arrays, mode):
    n = len(arrays)
    out_shapes = []
    for a in arrays:
        shp = a.shape if mode == "scatter" else (N_DEV,) + a.shape
        out_shapes.append(_sds(shp, a.dtype))

    def body(*refs):
        ins, outs = refs[:n], refs[n:2 * n]
        send_sems, recv_sems, local_sems = refs[2 * n:]
        x, y, c = lax.axis_index("x"), lax.axis_index("y"), lax.axis_index("c")
        me = 4 * x + 2 * y + c

        def src(a, dest):
            return ins[a].at[dest] if mode == "scatter" else ins[a]

        local = [pltpu.make_async_copy(src(a, me), outs[a].at[me], local_sems.at[a]) for a in range(n)]
        for cp in local:
            cp.start()
        remote = []
        for k in range(1, N_DEV):
            px = 1 - x if k & 4 else x
            py = 1 - y if k & 2 else y
            pc = 1 - c if k & 1 else c
            peer = 4 * px + 2 * py + pc
            for a in range(n):
                cp = pltpu.make_async_remote_copy(src_ref=src(a, peer), dst_ref=outs[a].at[me],
                                                  send_sem=send_sems.at[a * 7 + k - 1], recv_sem=recv_sems.at[a * 7 + k - 1],
                                                  device_id=(px, py, pc), device_id_type=pl.DeviceIdType.MESH)
                cp.start()
                arrival = pltpu.make_async_remote_copy(src_ref=src(a, peer), dst_ref=outs[a].at[peer],
                                                       send_sem=send_sems.at[a * 7 + k - 1], recv_sem=recv_sems.at[a * 7 + k - 1],
                                                       device_id=(px, py, pc), device_id_type=pl.DeviceIdType.MESH)
                remote.append((cp, arrival))
        for cp, arrival in remote:
            cp.wait_send()
            arrival.wait_recv()
        for cp in local:
            cp.wait()

    any_spec = pl.BlockSpec(memory_space=pl.ANY)
    return _pcall(body, name=name, in_specs=[any_spec] * n, out_specs=[any_spec] * n, out_shape=out_shapes,
                  scratch_shapes=[pltpu.SemaphoreType.DMA((7 * n,)), pltpu.SemaphoreType.DMA((7 * n,)), pltpu.SemaphoreType.DMA((n,))],
                  compiler_params=pltpu.CompilerParams(has_side_effects=True))(*arrays)


_HBM = pl.BlockSpec(memory_space=pltpu.HBM)
_SEM = pl.BlockSpec(memory_space=pltpu.SEMAPHORE)
_ANY = pl.BlockSpec(memory_space=pl.ANY)
_SPLIT_PARAMS = dict(compiler_params=pltpu.CompilerParams(has_side_effects=pltpu.SideEffectType.DATAFLOW_SIDE_EFFECTING))
ICI_SAME_CORE = (2, 4, 6)
ALL_PEERS = (1, 2, 3, 4, 5, 6, 7)


def _mesh_pos():
    x, y, c = lax.axis_index("x"), lax.axis_index("y"), lax.axis_index("c")
    return x, y, c, 4 * x + 2 * y + c


def _peer_of(k, x, y, c):
    px = 1 - x if k & 4 else x
    py = 1 - y if k & 2 else y
    pc = 1 - c if k & 1 else c
    return (px, py, pc), 4 * px + 2 * py + pc


def _split_copies(mode, ks, srcs, lands, send_sems, recv_sems):
    x, y, c, me = _mesh_pos()
    pairs = []
    for a in range(len(lands)):
        for j, k in enumerate(ks):
            dev, peer = _peer_of(k, x, y, c)
            i = a * len(ks) + j
            if mode == "gather":
                s_out, d_out, d_in = lands[a].at[me], lands[a].at[me], lands[a].at[peer]
            elif mode == "scatter":
                s_out, d_out, d_in = srcs[a].at[peer], lands[a].at[k - 1], lands[a].at[k - 1]
            else:
                dev, _ = _peer_of(1, x, y, c)
                _, theirs = _peer_of(k | 1, x, y, c)
                s_out, d_out, d_in = lands[a].at[peer], lands[a].at[peer], lands[a].at[theirs]
            both = [pltpu.make_async_remote_copy(src_ref=s_out, dst_ref=d, send_sem=send_sems.at[i], recv_sem=recv_sems.at[i],
                                                 device_id=dev, device_id_type=pl.DeviceIdType.MESH) for d in (d_out, d_in)]
            pairs.append(tuple(both))
    return pairs


def split_start(name, mode, ks, srcs, lands=None, after=None):
    n, nk = len(srcs) if lands is None else len(lands), len(ks)
    srcs = [pltpu.with_memory_space_constraint(s, pltpu.HBM) for s in srcs]
    if lands is None:
        shapes = [((N_DEV,) + s.shape) if mode == "gather" else ((N_DEV - 1,) + s.shape[1:]) for s in srcs]
        lands = [lax.empty(shp, s.dtype) for shp, s in zip(shapes, srcs)]
    lands = [pltpu.with_memory_space_constraint(l, pltpu.HBM) for l in lands]
    both = srcs + lands
    extra = [] if after is None else [after]

    def body(*refs):
        src_refs, land_refs = refs[:len(srcs)], refs[len(srcs):len(both)]
        send_sems, recv_sems = refs[len(both) + len(extra)], refs[len(both) + len(extra) + 1]
        token = refs[-1]
        for out, _ in _split_copies(mode, ks, src_refs, land_refs, send_sems, recv_sems):
            out.start()
        token[...] = jnp.zeros_like(token)

    out_shape = (pltpu.SemaphoreType.DMA((n * nk,)), pltpu.SemaphoreType.DMA((n * nk,)),
                 *[pltpu.HBM(a.shape, a.dtype) for a in both], _sds((8, 128), F32))
    res = _raw_call(body, name=name, out_shape=out_shape, in_specs=[_HBM] * len(both) + [_ANY] * len(extra),
                    out_specs=(_SEM, _SEM, *[_HBM] * len(both), pl.BlockSpec(memory_space=pltpu.VMEM)),
                    input_output_aliases={i: 2 + i for i in range(len(both))}, **_SPLIT_PARAMS)(*both, *extra)
    _PENDING.append(res[-1])
    return dict(mode=mode, ks=ks, sems=(res[0], res[1]), srcs=list(res[2:2 + len(srcs)]),
                lands=list(res[2 + len(srcs):2 + len(both)]), token=res[-1])


def split_wait(name, h, after):
    ns = len(h["srcs"])
    both = h["srcs"] + h["lands"]

    def body(*refs):
        src_refs, land_refs = refs[:ns], refs[ns:len(both)]
        send_sems, recv_sems = refs[len(both)], refs[len(both) + 1]
        for out, arriving in _split_copies(h["mode"], h["ks"], src_refs, land_refs, send_sems, recv_sems):
            out.wait_send()
            arriving.wait_recv()

    res = _raw_call(body, name=name, out_shape=tuple(pltpu.HBM(a.shape, a.dtype) for a in both),
                    in_specs=[_HBM] * len(both) + [_SEM, _SEM, _ANY], out_specs=tuple([_HBM] * len(both)),
                    input_output_aliases={i: i for i in range(len(both))}, **_SPLIT_PARAMS)(*both, *h["sems"], after)
    return list(res[:ns]), list(res[ns:])


FORWARD_BLOCKS = (0, 2, 4, 6)


def _rope_parts(pos_ref, inv_ref):
    ang = pos_ref[...] * inv_ref[...]
    return jnp.cos(ang), jnp.sin(ang)


def _rot_half(t):
    lane = lax.broadcasted_iota(jnp.int32, t.shape, 1)
    return jnp.where((lane % HEAD_DIM) < HEAD_DIM // 2, -pltpu.roll(t, 128 - HEAD_DIM // 2, 1), pltpu.roll(t, HEAD_DIM // 2, 1))


def _attn_mask(n):
    row = lax.broadcasted_iota(jnp.int32, (BLK, 2 * BLK), 0)
    col = lax.broadcasted_iota(jnp.int32, (BLK, 2 * BLK), 1)
    dist = row + BLK - col
    return (dist >= 0) & (dist < BLK) & ((n * BLK - BLK + col) >= 0)


def _attn_specs(T):
    prev = lambda n: jnp.maximum(n - 1, 0)
    kc = pl.BlockSpec((BLK, KV_DIM), lambda n: (n, OFF_K // KV_DIM))
    kp = pl.BlockSpec((BLK, KV_DIM), lambda n: (prev(n), OFF_K // KV_DIM))
    vc = pl.BlockSpec((BLK, KV_DIM), lambda n: (n, OFF_V // KV_DIM))
    vp = pl.BlockSpec((BLK, KV_DIM), lambda n: (prev(n), OFF_V // KV_DIM))
    pc = pl.BlockSpec((BLK, 1), lambda n: (n, 0))
    pp = pl.BlockSpec((BLK, 1), lambda n: (prev(n), 0))
    inv = pl.BlockSpec((1, 128), lambda n: (0, 0))
    sink = pl.BlockSpec(memory_space=pltpu.SMEM)
    return kc, kp, vc, vp, pc, pp, inv, sink


def _softmax_sink(sc, valid, sink):
    sc = jnp.where(valid, sc * (HEAD_DIM ** -0.5), -1e30)
    m = jnp.maximum(jnp.max(sc, axis=1, keepdims=True), sink)
    e = jnp.exp(sc - m)
    es = jnp.exp(sink - m)
    den = jnp.sum(e, axis=1, keepdims=True) + es
    return e / den, es / den


def attn_fwd(projp, posf, inv128, sinks):
    T = projp.shape[0]
    kc, kp, vc, vp, pc, pp, inv, sink = _attn_specs(T)

    def body(q_ref, kc_ref, kp_ref, vc_ref, vp_ref, pc_ref, pp_ref, inv_ref, sink_ref, o_ref, qr_ref, kr_ref):
        n = pl.program_id(0)
        cos_c, sin_c = _rope_parts(pc_ref, inv_ref)
        cos_p, sin_p = _rope_parts(pp_ref, inv_ref)
        valid = _attn_mask(n)
        k_c, k_p = [], []
        for s in range(KV_DIM // 128):
            t = kc_ref[:, 128 * s:128 * (s + 1)]
            k_c.append((t * cos_c + _rot_half(t) * sin_c).astype(BF16))
            kr_ref[:, 128 * s:128 * (s + 1)] = k_c[s]
            t = kp_ref[:, 128 * s:128 * (s + 1)]
            k_p.append((t * cos_p + _rot_half(t) * sin_p).astype(BF16))
        for s in range(Q_DIM // 128):
            t = q_ref[:, 128 * s:128 * (s + 1)]
            qs = (t * cos_c + _rot_half(t) * sin_c).astype(BF16)
            qr_ref[:, 128 * s:128 * (s + 1)] = qs
            for e in range(2):
                hq = 2 * s + e
                hk = hq // (ATTN_HEADS // KV_HEADS)
                lo = HEAD_DIM * (hk % 2)
                kcat = jnp.concatenate([k_p[hk // 2][:, lo:lo + HEAD_DIM], k_c[hk // 2][:, lo:lo + HEAD_DIM]], axis=0)
                vcat = jnp.concatenate([vp_ref[:, HEAD_DIM * hk:HEAD_DIM * (hk + 1)], vc_ref[:, HEAD_DIM * hk:HEAD_DIM * (hk + 1)]], axis=0)
                sc = _dot(qs[:, HEAD_DIM * e:HEAD_DIM * (e + 1)], kcat, NT)
                p, _ = _softmax_sink(sc, valid, sink_ref[0, hq])
                o_ref[:, HEAD_DIM * hq:HEAD_DIM * (hq + 1)] = _dot(p, vcat, NN).astype(BF16)

    qspec = pl.BlockSpec((BLK, Q_DIM), lambda n: (n, OFF_Q // Q_DIM))
    orow = pl.BlockSpec((BLK, Q_DIM), lambda n: (n, 0))
    krow = pl.BlockSpec((BLK, KV_DIM), lambda n: (n, 0))
    return _pcall(body, name="attn_fwd", grid=(T // BLK,), in_specs=[qspec, kc, kp, vc, vp, pc, pp, inv, sink],
                  out_specs=[orow, orow, krow], out_shape=[_sds((T, Q_DIM), BF16), _sds((T, Q_DIM), BF16), _sds((T, KV_DIM), BF16)],
                  compiler_params=_cparams(("arbitrary",)))(projp, projp, projp, projp, projp, posf, posf, inv128, sinks)


def attn_bwd(qr, kr, projp, dattn, posf, inv128, sinks):
    T = projp.shape[0]
    _, _, vc, vp, pc, pp, inv, sink = _attn_specs(T)
    G = ATTN_HEADS // KV_HEADS

    def body(qr_ref, krc_ref, krp_ref, vc_ref, vp_ref, do_ref, pc_ref, pp_ref, inv_ref, sink_ref, dq_ref, dk_ref, dv_ref, dsk_ref):
        n = pl.program_id(0)

        @pl.when(n == 0)
        def _():
            dk_ref[...] = jnp.zeros_like(dk_ref)
            dv_ref[...] = jnp.zeros_like(dv_ref)
            dsk_ref[...] = jnp.zeros_like(dsk_ref)

        cos_c, sin_c = _rope_parts(pc_ref, inv_ref)
        cos_p, sin_p = _rope_parts(pp_ref, inv_ref)
        valid = _attn_mask(n)
        lane = lax.broadcasted_iota(jnp.int32, (1, 128), 1)
        dsk = jnp.zeros((1, 128), F32)
        dq_heads, dk_heads, dv_heads = [], [], []
        for hk in range(KV_HEADS):
            ksl = slice(HEAD_DIM * hk, HEAD_DIM * (hk + 1))
            kcat = jnp.concatenate([krp_ref[:, ksl], krc_ref[:, ksl]], axis=0)
            vcat = jnp.concatenate([vp_ref[:, ksl], vc_ref[:, ksl]], axis=0)
            dkcat = jnp.zeros((2 * BLK, HEAD_DIM), F32)
            dvcat = jnp.zeros((2 * BLK, HEAD_DIM), F32)
            for g in range(G):
                hq = G * hk + g
                qsl = slice(HEAD_DIM * hq, HEAD_DIM * (hq + 1))
                q_h = qr_ref[:, qsl]
                do_h = do_ref[:, qsl]
                p, psink = _softmax_sink(_dot(q_h, kcat, NT), valid, sink_ref[0, hq])
                dp = _dot(do_h, vcat, NT)
                delta = jnp.sum(p * dp, axis=1, keepdims=True)
                ds = p * (dp - delta) * (HEAD_DIM ** -0.5)
                dsk = dsk + jnp.where(lane == hq, -jnp.sum(psink * delta, axis=0, keepdims=True), 0.0)
                dq_heads.append(_dot(ds, kcat, NN))
                dkcat = dkcat + _dot(ds, q_h, TN)
                dvcat = dvcat + _dot(p, do_h, TN)
            dk_heads.append(dkcat)
            dv_heads.append(dvcat)
        dsk_ref[...] += dsk
        for s in range(Q_DIM // 128):
            t = jnp.concatenate([dq_heads[2 * s], dq_heads[2 * s + 1]], axis=1)
            dq_ref[:, 128 * s:128 * (s + 1)] = (t * cos_c - _rot_half(t) * sin_c).astype(BF16)
        cur = pl.ds(pl.multiple_of(n * BLK, BLK), BLK)
        prv = pl.ds(pl.multiple_of(jnp.maximum(n - 1, 0) * BLK, BLK), BLK)
        for s in range(KV_DIM // 128):
            tc = jnp.concatenate([dk_heads[2 * s][BLK:], dk_heads[2 * s + 1][BLK:]], axis=1)
            tp = jnp.concatenate([dk_heads[2 * s][:BLK], dk_heads[2 * s + 1][:BLK]], axis=1)
            cols = slice(128 * s, 128 * (s + 1))
            dk_ref[cur, cols] += tc * cos_c - _rot_half(tc) * sin_c
            dk_ref[prv, cols] += tp * cos_p - _rot_half(tp) * sin_p
            dv_ref[cur, cols] += jnp.concatenate([dv_heads[2 * s][BLK:], dv_heads[2 * s + 1][BLK:]], axis=1)
            dv_ref[prv, cols] += jnp.concatenate([dv_heads[2 * s][:BLK], dv_heads[2 * s + 1][:BLK]], axis=1)

    qrow = pl.BlockSpec((BLK, Q_DIM), lambda n: (n, 0))
    krc = pl.BlockSpec((BLK, KV_DIM), lambda n: (n, 0))
    krp = pl.BlockSpec((BLK, KV_DIM), lambda n: (jnp.maximum(n - 1, 0), 0))
    whole = pl.BlockSpec((T, KV_DIM), lambda n: (0, 0))
    return _pcall(body, name="attn_bwd", grid=(T // BLK,), in_specs=[qrow, krc, krp, vc, vp, qrow, pc, pp, inv, sink],
                  out_specs=[qrow, whole, whole, pl.BlockSpec((1, 128), lambda n: (0, 0))],
                  out_shape=[_sds((T, Q_DIM), BF16), _sds((T, KV_DIM), F32), _sds((T, KV_DIM), F32), _sds((1, 128), F32)],
                  compiler_params=_cparams(("arbitrary",)))(qr, kr, kr, projp, projp, dattn, posf, posf, inv128, sinks)


CONV_CB = 256


def _shift_down(x, s):
    row = lax.broadcasted_iota(jnp.int32, x.shape, 0)
    return jnp.where(row >= s, pltpu.roll(x, s, 0), 0.0)


def _shift_up(x, s):
    T = x.shape[0]
    row = lax.broadcasted_iota(jnp.int32, x.shape, 0)
    return jnp.where(row < T - s, pltpu.roll(x, T - s, 0), 0.0)


def _conv_pre(x, w_ref, b_ref):
    acc = x * w_ref[CONV_WIDTH - 1:CONV_WIDTH, :] + b_ref[...]
    for s in range(1, CONV_WIDTH):
        acc = acc + _shift_down(x, s) * w_ref[CONV_WIDTH - 1 - s:CONV_WIDTH - s, :]
    return acc


def conv_fwd(projp, conv_w, conv_b):
    T = projp.shape[0]

    def body(x_ref, w_ref, b_ref, o_ref):
        o_ref[...] = _silu(_conv_pre(x_ref[...], w_ref, b_ref))

    return _pcall(body, name="conv_fwd", grid=(CONV_DIM // CONV_CB,),
                  in_specs=[pl.BlockSpec((T, CONV_CB), lambda c: (0, OFF_XBC // CONV_CB + c)),
                            pl.BlockSpec((CONV_WIDTH, CONV_CB), lambda c: (0, c)), pl.BlockSpec((1, CONV_CB), lambda c: (0, c))],
                  out_specs=pl.BlockSpec((T, CONV_CB), lambda c: (0, c)), out_shape=_sds((T, CONV_DIM), F32),
                  compiler_params=_cparams(("arbitrary",)))(projp, conv_w, conv_b)


def conv_bwd(name, projp, dact, conv_w, conv_b, col0):
    T, C = dact.shape
    c0 = col0 // CONV_CB

    def body(x_ref, da_ref, w_ref, b_ref, dx_ref, dw_ref, db_ref):
        x = x_ref[...]
        dpre = da_ref[...] * _dsilu(_conv_pre(x, w_ref, b_ref))
        dx = dpre * w_ref[CONV_WIDTH - 1:CONV_WIDTH, :]
        dw_ref[CONV_WIDTH - 1:CONV_WIDTH, :] = jnp.sum(dpre * x, axis=0, keepdims=True)
        for s in range(1, CONV_WIDTH):
            i = CONV_WIDTH - 1 - s
            dx = dx + _shift_up(dpre, s) * w_ref[i:i + 1, :]
            dw_ref[i:i + 1, :] = jnp.sum(dpre * _shift_down(x, s), axis=0, keepdims=True)
        dx_ref[...] = dx.astype(BF16)
        db_ref[...] = jnp.sum(dpre, axis=0, keepdims=True)

    return _pcall(body, name=name, grid=(C // CONV_CB,),
                  in_specs=[pl.BlockSpec((T, CONV_CB), lambda c: (0, OFF_XBC // CONV_CB + c0 + c)),
                            pl.BlockSpec((T, CONV_CB), lambda c: (0, c)),
                            pl.BlockSpec((CONV_WIDTH, CONV_CB), lambda c: (0, c0 + c)), pl.BlockSpec((1, CONV_CB), lambda c: (0, c0 + c))],
                  out_specs=[pl.BlockSpec((T, CONV_CB), lambda c: (0, c)), pl.BlockSpec((CONV_WIDTH, CONV_CB), lambda c: (0, c)),
                             pl.BlockSpec((1, CONV_CB), lambda c: (0, c))],
                  out_shape=[_sds((T, C), BF16), _sds((CONV_WIDTH, C), F32), _sds((1, C), F32)],
                  compiler_params=_cparams(("arbitrary",)))(projp, dact, conv_w, conv_b)


def _softplus(x):
    return jnp.maximum(x, 0.0) + jnp.log1p(jnp.exp(-jnp.abs(x)))


def _tri(lower):
    r = lax.broadcasted_iota(jnp.int32, (BLK, BLK), 0)
    c = lax.broadcasted_iota(jnp.int32, (BLK, BLK), 1)
    return (r >= c) if lower else (c >= r)


def _ssd_chunk_setup(dt_ref, dtb_ref, alog_ref):
    raw = dt_ref[...] + dtb_ref[...]
    dt = _softplus(raw)
    aneg = -jnp.exp(alog_ref[...])
    a = dt * aneg
    acs = jnp.dot(_tri(True).astype(F32), a, precision=lax.Precision.HIGHEST, preferred_element_type=F32)
    return raw, dt, aneg, acs, acs.T


def _ssd_specs(T, rev):
    nc = T // BLK
    ci = (lambda c: nc - 1 - c) if rev else (lambda c: c)
    xs = pl.BlockSpec((BLK, D_INNER), lambda c: (ci(c), 0))
    bm = pl.BlockSpec((BLK, SSM_GROUPS * D_STATE), lambda c: (ci(c), D_INNER // (SSM_GROUPS * D_STATE)))
    cm = pl.BlockSpec((BLK, SSM_GROUPS * D_STATE), lambda c: (ci(c), D_INNER // (SSM_GROUPS * D_STATE) + 1))
    dt = pl.BlockSpec((BLK, DT_PAD), lambda c: (ci(c), OFF_DT // DT_PAD))
    v128 = pl.BlockSpec((1, 128), lambda c: (0, 0))
    dfull = pl.BlockSpec((1, D_INNER), lambda c: (0, 0))
    st = pl.BlockSpec((None, SSM_HEADS, HEAD_DIM, D_STATE), lambda c: (ci(c), 0, 0, 0))
    return xs, bm, cm, dt, v128, dfull, st, ci


def ssd_fwd(xbc, projp, dtb, alog, dfull):
    T = xbc.shape[0]
    nc = T // BLK
    xs, bm, cm, dts, v128, dfs, st, _ = _ssd_specs(T, False)

    def body(xs_ref, b_ref, c_ref, dt_ref, dtb_ref, alog_ref, d_ref, y_ref, st_ref, h_scr):
        c = pl.program_id(0)

        @pl.when(c == 0)
        def _():
            h_scr[...] = jnp.zeros_like(h_scr)

        _, dt, _, acs, acsT = _ssd_chunk_setup(dt_ref, dtb_ref, alog_ref)
        tril = _tri(True)
        for g in range(SSM_GROUPS):
            B = b_ref[:, D_STATE * g:D_STATE * (g + 1)].astype(BF16)
            C = c_ref[:, D_STATE * g:D_STATE * (g + 1)].astype(BF16)
            cb = _dot(C, B, NT)
            for hh in range(HEADS_PER_GROUP):
                h = HEADS_PER_GROUP * g + hh
                hs = slice(HEAD_DIM * h, HEAD_DIM * (h + 1))
                dt_h, acs_h, acsT_h = dt[:, h:h + 1], acs[:, h:h + 1], acsT[h:h + 1, :]
                alast = acs[BLK - 1:BLK, h:h + 1]
                x_h = xs_ref[:, hs]
                xd = x_h * dt_h
                decay = jnp.where(tril, jnp.exp(jnp.where(tril, acs_h - acsT_h, 0.0)), 0.0)
                y = _dot(cb * decay, xd, NN)
                hp = h_scr[h]
                st_ref[h] = hp
                y = y + jnp.exp(acs_h) * _dot(C, hp, NT)
                h_scr[h] = jnp.exp(alast) * hp + _dot(xd * jnp.exp(alast - acs_h), B, TN)
                y_ref[:, hs] = y + d_ref[:, hs] * x_h

    return _pcall(body, name="ssd_fwd", grid=(nc,), in_specs=[xs, bm, cm, dts, v128, v128, dfs],
                  out_specs=[xs, st], out_shape=[_sds((T, D_INNER), F32), _sds((nc, SSM_HEADS, HEAD_DIM, D_STATE), F32)],
                  scratch_shapes=[pltpu.VMEM((SSM_HEADS, HEAD_DIM, D_STATE), F32)],
                  compiler_params=_cparams(("arbitrary",)))(xbc, xbc, xbc, projp, dtb, alog, dfull)


def ssd_bwd(xbc, projp, dtb, alog, dfull, states, dy):
    T = xbc.shape[0]
    nc = T // BLK
    xs, bm, cm, dts, v128, dfs, st, ci = _ssd_specs(T, True)
    gn = SSM_GROUPS * D_STATE

    def body(xs_ref, b_ref, c_ref, dt_ref, dtb_ref, alog_ref, d_ref, st_ref, dy_ref,
             dxs_ref, dB_ref, dC_ref, ddt_ref, dal_ref, dD_ref, ddtb_ref, dh_scr):
        i = pl.program_id(0)

        @pl.when(i == 0)
        def _():
            dh_scr[...] = jnp.zeros_like(dh_scr)
            dal_ref[...] = jnp.zeros_like(dal_ref)
            dD_ref[...] = jnp.zeros_like(dD_ref)
            ddtb_ref[...] = jnp.zeros_like(ddtb_ref)

        raw, dt, aneg, acs, acsT = _ssd_chunk_setup(dt_ref, dtb_ref, alog_ref)
        tril = _tri(True)
        lane1 = lax.broadcasted_iota(jnp.int32, (1, 128), 1)
        lane = lax.broadcasted_iota(jnp.int32, (BLK, 128), 1)
        sub = lax.broadcasted_iota(jnp.int32, (BLK, 128), 0)
        ds_col = jnp.zeros((BLK, 128), F32)
        ds_row = jnp.zeros((BLK, 128), F32)
        ddt_col = jnp.zeros((BLK, 128), F32)
        ds_last = jnp.zeros((1, 128), F32)
        dD = jnp.zeros((1, 128), F32)

        def total(v):
            return jnp.sum(jnp.sum(v, axis=1, keepdims=True), axis=0, keepdims=True)

        for g in range(SSM_GROUPS):
            gs = slice(D_STATE * g, D_STATE * (g + 1))
            B = b_ref[:, gs].astype(BF16)
            C = c_ref[:, gs].astype(BF16)
            cb = _dot(C, B, NT)
            dG = jnp.zeros((BLK, BLK), F32)
            dB_g = jnp.zeros((BLK, D_STATE), F32)
            dC_g = jnp.zeros((BLK, D_STATE), F32)
            for hh in range(HEADS_PER_GROUP):
                h = HEADS_PER_GROUP * g + hh
                hs = slice(HEAD_DIM * h, HEAD_DIM * (h + 1))
                dt_h, acs_h, acsT_h = dt[:, h:h + 1], acs[:, h:h + 1], acsT[h:h + 1, :]
                alast = acs[BLK - 1:BLK, h:h + 1]
                x_h = xs_ref[:, hs]
                dy_h = dy_ref[:, hs]
                xd = x_h * dt_h
                decay = jnp.where(tril, jnp.exp(jnp.where(tril, acs_h - acsT_h, 0.0)), 0.0)
                M = cb * decay
                hc = st_ref[h]
                dS = dh_scr[h]
                w = jnp.exp(alast - acs_h)
                gamma = jnp.exp(alast)
                dD = dD + jnp.where(lane1 == h, total(dy_h * x_h), 0.0)
                dye = dy_h * jnp.exp(acs_h)
                dH_y = _dot(dye, C, TN)
                dC_g = dC_g + _dot(dye, hc, NN)
                ds_h = jnp.sum(dye * _dot(C, hc, NT), axis=1, keepdims=True)
                dM = _dot(dy_h, xd, NT)
                dxd = _dot(M, dy_h, TN)
                Q = dM * M
                ds_h = ds_h + jnp.sum(Q, axis=1, keepdims=True)
                ds_row = jnp.where(sub == h, jnp.sum(Q, axis=0, keepdims=True), ds_row)
                dG = dG + dM * decay
                dxdw = _dot(B, dS, NT)
                dxd = dxd + w * dxdw
                dww = jnp.sum(xd * dxdw, axis=1, keepdims=True) * w
                ds_h = ds_h - dww
                ds_last = ds_last + jnp.where(lane1 == h, jnp.sum(dww, axis=0, keepdims=True) + total(dS * hc) * gamma, 0.0)
                dB_g = dB_g + _dot(xd * w, dS, NN)
                dh_scr[h] = gamma * dS + dH_y
                dxs_ref[:, hs] = d_ref[:, hs] * dy_h + dxd * dt_h
                ddt_col = jnp.where(lane == h, jnp.sum(dxd * x_h, axis=1, keepdims=True), ddt_col)
                ds_col = jnp.where(lane == h, ds_h, ds_col)
            dC_ref[:, gs] = dC_g + _dot(dG, B, NN)
            dB_ref[:, gs] = dB_g + _dot(dG, C, TN)
        ds_all = ds_col - ds_row.T + jnp.where(sub == BLK - 1, ds_last, 0.0)
        da = jnp.dot(_tri(False).astype(F32), ds_all, precision=lax.Precision.HIGHEST, preferred_element_type=F32)
        ddt = ddt_col + da * aneg
        draw = jnp.where(lane < SSM_HEADS, ddt * _sigmoid(raw), 0.0)
        ddt_ref[...] = draw.astype(BF16)
        dal_ref[...] += jnp.sum(da * dt, axis=0, keepdims=True) * aneg
        ddtb_ref[...] += jnp.sum(draw, axis=0, keepdims=True)
        dD_ref[...] += dD

    gblk = pl.BlockSpec((BLK, gn), lambda c: (ci(c), 0))
    return _pcall(body, name="ssd_bwd", grid=(nc,), in_specs=[xs, bm, cm, dts, v128, v128, dfs, st, xs],
                  out_specs=[xs, gblk, gblk, pl.BlockSpec((BLK, DT_PAD), lambda c: (ci(c), 0)), v128, v128, v128],
                  out_shape=[_sds((T, D_INNER), F32), _sds((T, gn), F32), _sds((T, gn), F32), _sds((T, DT_PAD), BF16),
                             _sds((1, 128), F32), _sds((1, 128), F32), _sds((1, 128), F32)],
                  scratch_shapes=[pltpu.VMEM((SSM_HEADS, HEAD_DIM, D_STATE), F32)],
                  compiler_params=_cparams(("arbitrary",)))(xbc, xbc, xbc, projp, dtb, alog, dfull, states, dy)


_WIN_ORDER = ("z", "ga", "gs", "xbc", "q", "k", "v", "dt")


def _win_to_padded(win_g):
    per = IN_DIM // N_DEV
    rows = []
    for nm in _WIN_ORDER:
        s, w = SEG[nm]
        while w > 0:
            j, r = divmod(s, per)
            n = min(w, per - r)
            rows.append(win_g[j, r:r + n])
            s, w = s + n, w - n
    rows.append(jnp.zeros((DT_PAD - SEG["dt"][1], D_MODEL), win_g.dtype))
    return jnp.concatenate(rows, axis=0)


def _padded_to_win(dw):
    off = dict(z=OFF_Z, ga=OFF_GA, gs=OFF_GS, xbc=OFF_XBC, q=OFF_Q, k=OFF_K, v=OFF_V, dt=OFF_DT)
    per = IN_DIM // N_DEV
    blocks = []
    for j in range(N_DEV):
        lo, hi, rows = j * per, (j + 1) * per, []
        for nm in ("q", "k", "v", "z", "xbc", "dt", "ga", "gs"):
            s, w = SEG[nm]
            a, b = max(lo, s), min(hi, s + w)
            if a < b:
                rows.append(dw[off[nm] + a - s:off[nm] + b - s])
        blocks.append(jnp.concatenate(rows, axis=0))
    return jnp.stack(blocks)


def _pad128(v):
    return jnp.pad(v, ((0, 0), (0, 128 - v.shape[1])))


_SMALL = (("loss", 128, 1), ("g_mix", 2048, 2048), ("conv_b", 3072, 3072), ("dt_bias", 128, 32), ("a_log", 128, 32),
          ("d_skip", 128, 32), ("g_ssd", 2048, 2048), ("sinks", 128, 16), ("g_ffn", 2048, 2048), ("g_ple", 2048, 2048),
          ("g_final", 2048, 2048))


def _small_vec(d):
    parts = []
    for nm, pw, w in _SMALL:
        v = d[nm].reshape(1, -1).astype(F32)
        parts.append(jnp.pad(v[:, :min(v.shape[1], pw)], ((0, 0), (0, pw - min(v.shape[1], pw)))))
    return jnp.concatenate(parts, axis=1)


def _small_split(vec):
    out, o = {}, 0
    for nm, pw, w in _SMALL:
        out[nm] = vec[0, o:o + w]
        o += pw
    return out


def kernel(x, p, positions, g_mix, w_in, conv_w, conv_b, dt_bias, a_log, d_skip, g_ssd, sinks, w_attn_br, w_ssd_br, w_o, g_ffn, w_gate, w_up, w_down, g_ple, w_ple_gate, w_ple_proj, g_final, loss_target, m_g_mix, m_w_in, m_conv_w, m_conv_b, m_dt_bias, m_a_log, m_d_skip, m_g_ssd, m_sinks, m_w_attn_br, m_w_ssd_br, m_w_o, m_g_ffn, m_w_gate, m_w_up, m_w_down, m_g_ple, m_w_ple_gate, m_w_ple_proj, m_g_final, v_g_mix, v_w_in, v_conv_w, v_conv_b, v_dt_bias, v_a_log, v_d_skip, v_g_ssd, v_sinks, v_w_attn_br, v_w_ssd_br, v_w_o, v_g_ffn, v_w_gate, v_w_up, v_w_down, v_g_ple, v_w_ple_gate, v_w_ple_proj, v_g_final):
    T = x.shape[1]
    D = D_MODEL
    W = dict(g_mix=g_mix, w_in=w_in, conv_w=conv_w, conv_b=conv_b, dt_bias=dt_bias, a_log=a_log, d_skip=d_skip, g_ssd=g_ssd,
             sinks=sinks, w_attn_br=w_attn_br, w_ssd_br=w_ssd_br, w_o=w_o, g_ffn=g_ffn, w_gate=w_gate, w_up=w_up, w_down=w_down,
             g_ple=g_ple, w_ple_gate=w_ple_gate, w_ple_proj=w_ple_proj, g_final=g_final)
    Mo = dict(g_mix=m_g_mix, w_in=m_w_in, conv_w=m_conv_w, conv_b=m_conv_b, dt_bias=m_dt_bias, a_log=m_a_log, d_skip=m_d_skip,
              g_ssd=m_g_ssd, sinks=m_sinks, w_attn_br=m_w_attn_br, w_ssd_br=m_w_ssd_br, w_o=m_w_o, g_ffn=m_g_ffn, w_gate=m_w_gate,
              w_up=m_w_up, w_down=m_w_down, g_ple=m_g_ple, w_ple_gate=m_w_ple_gate, w_ple_proj=m_w_ple_proj, g_final=m_g_final)
    Vo = dict(g_mix=v_g_mix, w_in=v_w_in, conv_w=v_conv_w, conv_b=v_conv_b, dt_bias=v_dt_bias, a_log=v_a_log, d_skip=v_d_skip,
              g_ssd=v_g_ssd, sinks=v_sinks, w_attn_br=v_w_attn_br, w_ssd_br=v_w_ssd_br, w_o=v_w_o, g_ffn=v_g_ffn, w_gate=v_w_gate,
              w_up=v_w_up, w_down=v_w_down, g_ple=v_g_ple, w_ple_gate=v_w_ple_gate, w_ple_proj=v_w_ple_proj, g_final=v_g_final)
    order = ["g_mix", "w_in", "conv_w", "conv_b", "dt_bias", "a_log", "d_skip", "g_ssd", "sinks", "w_attn_br", "w_ssd_br", "w_o",
             "g_ffn", "w_gate", "w_up", "w_down", "g_ple", "w_ple_gate", "w_ple_proj", "g_final"]
    big = ["w_in", "conv_w", "w_attn_br", "w_ssd_br", "w_o", "w_gate", "w_up", "w_down", "w_ple_gate", "w_ple_proj"]

    x2 = x.reshape(T, D)
    p2 = p.reshape(T, PLE_DIM)
    tgt = loss_target.reshape(T, D)
    posf = positions.reshape(T, 1).astype(F32)
    inv = ROPE_THETA ** (-np.arange(HEAD_DIM // 2, dtype=np.float32) * 2.0 / HEAD_DIM)
    inv128 = jnp.asarray(np.tile(inv, 128 // (HEAD_DIM // 2)).reshape(1, 128).astype(np.float32))
    sh = {n: W[n].reshape(W[n].shape[-2:]) for n in big}

    del _PENDING[:]
    me = 4 * lax.axis_index("x") + 2 * lax.axis_index("y") + lax.axis_index("c")
    groups = (("w_in",), ("conv_w", "w_attn_br", "w_ssd_br", "w_o"), ("w_gate", "w_up", "w_down"), ("w_ple_gate", "w_ple_proj"))
    send = {n: sh[n] if n == "conv_w" else sh[n].astype(BF16) for n in big}
    send["w_in"] = sh["w_in"].T.astype(BF16)
    started, prev = [], None
    for gi, grp in enumerate(groups):
        zones = [lax.dynamic_update_index_in_dim(lax.empty((N_DEV,) + send[n].shape, send[n].dtype), send[n], me, 0) for n in grp]
        h = split_start("gather_start_%d" % gi, "gather", ICI_SAME_CORE, [], lands=zones, after=prev)
        prev = h["token"]
        started.append(h)
    gathered, fwd = {}, {}

    def forward_start(gi, after):
        _, lands = split_wait("gather_wait_%d" % gi, started[gi], after)
        fwd[gi] = split_start("forward_start_%d" % gi, "forward", FORWARD_BLOCKS, [], lands=lands)

    def forward_wait(gi, after):
        _, full = split_wait("forward_wait_%d" % gi, fwd[gi], after)
        gathered.update(zip(groups[gi], full))

    u = rms_fwd("norm_mix", x2, g_mix)
    forward_start(0, u)
    forward_wait(0, u)
    forward_start(1, u)
    winp = _win_to_padded(gathered["w_in"])
    dtb = _pad128(dt_bias)
    alog = _pad128(a_log)
    dfull = jnp.repeat(d_skip.reshape(SSM_HEADS), HEAD_DIM).reshape(1, D_INNER)

    projp = mm_nt("in_proj", u, winp, 640)
    attn, qr, kr = attn_fwd(projp, posf, inv128, sinks)
    forward_wait(1, attn)
    convw = jnp.transpose(gathered["conv_w"], (1, 0, 2)).reshape(CONV_WIDTH, CONV_DIM)
    wab = gathered["w_attn_br"]
    wsb = gathered["w_ssd_br"].reshape(D, D)
    wo = gathered["w_o"].reshape(D, D)
    xbc = conv_fwd(projp, convw, conv_b)
    y, states = ssd_fwd(xbc, projp, dtb, alog, dfull)
    forward_start(2, y)
    yn = gnorm_fwd(y, projp, g_ssd)
    out_a = mm_nn_colblk("attn_br", attn, wab)
    out_s = mm_nn("ssd_br", yn, wsb, 512)
    merged = merge_fwd(projp, out_a, out_s)
    h1 = mm_nn("o_proj", merged, wo, 512, residual=x2)
    f = rms_fwd("norm_ffn", h1, g_ffn)
    forward_wait(2, f)
    wg, wu, wd = gathered["w_gate"], gathered["w_up"], gathered["w_down"]
    gate, up, act = ffn_up(f, wg, wu)
    forward_start(3, act)
    h2 = ffn_down(act, wd, h1)
    r = rms_fwd("norm_ple", h2, g_ple)
    forward_wait(3, r)
    wpg = gathered["w_ple_gate"].reshape(D, D)
    wpp = gathered["w_ple_proj"]
    pg = mm_nn("ple_gate", r, wpg, 512)
    pp = mm_nn_colblk("ple_proj", p2, wpp)
    loss_v, dh3, dpg, dpp, dg_final = head_fwd_bwd(h2, pg, pp, g_final.reshape(1, D), tgt)

    gw = {}
    scat = []

    def scatter_start(names):
        scat.append((names, split_start("scatter_start_%d" % len(scat), "scatter", ALL_PEERS, [gw[n] for n in names])))

    gw["w_ple_proj"] = mm_tn_colblk("dw_ple_proj", p2, dpp, PLE_DIM)
    dr = mm_nt("d_ple_gate", dpg, wpg, 512)
    gw["w_ple_gate"] = mm_tn("dw_ple_gate", r, dpg, 512, 1024).reshape(N_DEV, D // N_DEV, D)
    scatter_start(("w_ple_proj", "w_ple_gate"))
    dh2, dh2b, dg_ple = rms_bwd("norm_ple_bwd", h2, g_ple, dr, dh3)
    dgate, dup = ffn_down_bwd(dh2b, wd, gate, up)
    gw["w_down"] = wgrad_rowblk_lhs("dw_down", act, dh2b, 1024)
    gw["w_gate"] = wgrad_colblk_rhs("dw_gate", f, dgate, 1024)
    gw["w_up"] = wgrad_colblk_rhs("dw_up", f, dup, 1024)
    scatter_start(("w_down", "w_gate", "w_up"))
    df = ffn_up_bwd(dgate, dup, wg, wu)
    dh1, dh1b, dg_ffn = rms_bwd("norm_ffn_bwd", h1, g_ffn, df, dh2)
    dmerged = mm_nt("d_o_proj", dh1b, wo, 512)
    gw["w_o"] = mm_tn("dw_o", merged, dh1b, 512, 1024).reshape(N_DEV, D // N_DEV, D)
    dout_a, dout_s, dga, dgs = merge_bwd(projp, out_a, out_s, dmerged)
    gw["w_ssd_br"] = mm_tn("dw_ssd_br", yn, dout_s, 512, 1024).reshape(N_DEV, D // N_DEV, D)
    gw["w_attn_br"] = mm_tn_colblk("dw_attn_br", attn, dout_a, D // N_DEV)
    scatter_start(("w_o", "w_ssd_br", "w_attn_br"))
    dyn = mm_nt("d_ssd_br", dout_s, wsb, 512)
    dattn = attn_br_bwd(dout_a, wab)
    dy, dz, dg_ssd = gnorm_bwd(y, projp, g_ssd, dyn)
    dxs, dbm, dcm, ddt, dal, ddsk, ddtb = ssd_bwd(xbc, projp, dtb, alog, dfull, states, dy)
    dx_x, dwc_x, dbc_x = conv_bwd("conv_bwd_x", projp, dxs, convw, conv_b, 0)
    dx_b, dwc_b, dbc_b = conv_bwd("conv_bwd_b", projp, dbm, convw, conv_b, D_INNER)
    dx_c, dwc_c, dbc_c = conv_bwd("conv_bwd_c", projp, dcm, convw, conv_b, D_INNER + SSM_GROUPS * D_STATE)
    dq, dk, dv, dsk = attn_bwd(qr, kr, projp, dattn, posf, inv128, sinks)
    dproj = jnp.concatenate([dz, dga, dgs, dx_x, dx_b, dx_c, dq, dk.astype(BF16), dv.astype(BF16), ddt], axis=1)
    gw["w_in"] = _padded_to_win(mm_tn("dw_in", dproj, u, 640, 1024))
    dconvw = jnp.concatenate([dwc_x, dwc_b, dwc_c], axis=1)
    gw["conv_w"] = jnp.transpose(dconvw.reshape(CONV_WIDTH, N_DEV, CONV_DIM // N_DEV), (1, 0, 2))
    scatter_start(("w_in", "conv_w"))
    du = mm_nn_red("d_in_proj", dproj, winp, 1024, 640)
    gx, _, dg_mix = rms_bwd("norm_mix_bwd", x2, g_mix, du, dh1)

    res = {}
    after = gx
    for si, (names, h) in enumerate(scat):
        srcs, lands = split_wait("scatter_wait_%d" % si, h, after)
        for n, mine, arrived in zip(names, srcs, lands):
            own = lax.dynamic_index_in_dim(mine, me, 0, keepdims=False)
            if n == "w_in":
                arrived, own = None, sum_partials("sum_w_in", arrived, own, 256).T
            res[n] = adamw("adamw_" + n, arrived, sh[n], Mo[n].reshape(sh[n].shape), Vo[n].reshape(sh[n].shape), own=own)
        after = res[names[0]][0]

    small_g = dict(loss=loss_v[:, :1], g_mix=dg_mix, conv_b=jnp.concatenate([dbc_x, dbc_b, dbc_c], axis=1), dt_bias=ddtb,
                   a_log=dal, d_skip=ddsk, g_ssd=dg_ssd, sinks=dsk, g_ffn=dg_ffn, g_ple=dg_ple, g_final=dg_final)
    zero = jnp.zeros((1, 1), F32)
    vec_parts = exchange("gather_small", [_small_vec(small_g)], "gather")[0]
    sres = adamw("adamw_small", vec_parts, _small_vec({**W, "loss": zero}), _small_vec({**Mo, "loss": zero}),
                 _small_vec({**Vo, "loss": zero}))
    ssplit = [_small_split(a) for a in sres]
    loss = ssplit[0]["loss"].reshape(())
    for n in order:
        if n not in res:
            res[n] = tuple(s[n].reshape(W[n].shape) for s in ssplit)
        else:
            res[n] = tuple(a.reshape(W[n].shape) for a in res[n])
    outs = [loss, gx.reshape(x.shape)]
    for k in range(4):
        outs += [res[n][k] for n in order]
    return tuple(outs)
```

```python
import functools

import numpy as np
import jax
import jax.numpy as jnp
from jax import lax
from jax.experimental import pallas as pl
from jax.experimental.pallas import tpu as pltpu

F32 = jnp.float32
BF16 = jnp.bfloat16

N_DEV = 8
D_MODEL = 2048
HEAD_DIM = 64
ATTN_HEADS = 16
KV_HEADS = 4
Q_DIM = 1024
KV_DIM = 256
BLK = 128
D_INNER = 2048
SSM_HEADS = 32
SSM_GROUPS = 4
HEADS_PER_GROUP = 8
D_STATE = 128
CONV_WIDTH = 4
CONV_DIM = 3072
FFN_HIDDEN = 5632
PLE_DIM = 256
IN_DIM = 10784
NORM_EPS = 1e-6
SSM_NORM_EPS = 1e-5
ROPE_THETA = 10000.0

OFF_Z, OFF_GA, OFF_GS, OFF_XBC, OFF_Q, OFF_K, OFF_V, OFF_DT = 0, 2048, 4096, 6144, 9216, 10240, 10496, 10752
IN_PAD = 10880
DT_PAD = 128
SEG = dict(q=(0, 1024), k=(1024, 256), v=(1280, 256), z=(1536, 2048), xbc=(3584, 3072), dt=(6656, 32),
           ga=(6688, 2048), gs=(8736, 2048))

ADAM_LR, ADAM_B1, ADAM_B2, ADAM_EPS, ADAM_WD, ADAM_STEP = 0.001, 0.9, 0.999, 1e-08, 0.01, 10

VMEM_LIMIT = 56 * 1024 * 1024

NN = (((1,), (0,)), ((), ()))
NT = (((1,), (1,)), ((), ()))
TN = (((0,), (0,)), ((), ()))


_PENDING = []


def _raw_call(body, **kw):
    return pl.pallas_call(body, **kw)


def _pcall(body, **kw):
    deps = list(_PENDING)
    del _PENDING[:]
    if not deps:
        return _raw_call(body, **kw)
    n_in = len(kw["in_specs"])

    def tied(*refs):
        return body(*refs[:n_in], *refs[n_in + len(deps):])

    kw["in_specs"] = list(kw["in_specs"]) + [pl.BlockSpec(memory_space=pl.ANY)] * len(deps)
    call = _raw_call(tied, **kw)
    return lambda *ops: call(*ops, *deps)


def _cparams(sem=None):
    if sem is None:
        return pltpu.CompilerParams(vmem_limit_bytes=VMEM_LIMIT)
    return pltpu.CompilerParams(vmem_limit_bytes=VMEM_LIMIT, dimension_semantics=sem)


def _dot(a, b, dn):
    return lax.dot_general(a.astype(BF16), b.astype(BF16), dn, preferred_element_type=F32)


def _sigmoid(x):
    return 1.0 / (1.0 + jnp.exp(-x))


def _silu(x):
    return x * _sigmoid(x)


def _dsilu(x):
    s = _sigmoid(x)
    return s * (1.0 + x * (1.0 - s))


def _matmul(name, pairs, pair_specs, dn, grid, out_shapes, out_specs, nred=1, extra=(), extra_specs=(),
            epilogue=None, acc_shape=None):
    n_in = 2 * len(pairs) + len(extra)
    n_out = len(out_shapes)

    def body(*refs):
        ins = refs[:2 * len(pairs)]
        ex = refs[2 * len(pairs):n_in]
        outs = refs[n_in:n_in + n_out]

        def prod():
            s = None
            for p in range(len(pairs)):
                d = _dot(ins[2 * p][...], ins[2 * p + 1][...], dn)
                s = d if s is None else s + d
            return s

        def finish(val):
            if epilogue is None:
                outs[0][...] = val.astype(outs[0].dtype)
            else:
                res = epilogue(val, *[e[...] for e in ex])
                for o, r in zip(outs, res):
                    o[...] = r.astype(o.dtype)

        if nred == 1:
            finish(prod())
        else:
            acc = refs[n_in + n_out]
            k = pl.program_id(len(grid) - 1)

            @pl.when(k == 0)
            def _():
                acc[...] = jnp.zeros_like(acc)

            acc[...] += prod()

            @pl.when(k == nred - 1)
            def _():
                finish(acc[...])

    operands = []
    specs = []
    for (a, b), (sa, sb) in zip(pairs, pair_specs):
        operands += [a, b]
        specs += [sa, sb]
    operands += list(extra)
    specs += list(extra_specs)
    scratch = [pltpu.VMEM(acc_shape, F32)] if nred > 1 else []
    sem = ("arbitrary",) * len(grid)
    res = _pcall(body, name=name, grid=grid, in_specs=specs, out_specs=list(out_specs),
                 out_shape=list(out_shapes), scratch_shapes=scratch, compiler_params=_cparams(sem))(*operands)
    return res


def _sds(shape, dtype):
    return jax.ShapeDtypeStruct(shape, dtype)


def _row_tile(T):
    return min(1024, T)


def mm_nn(name, a, b, tn, out_dtype=F32, residual=None):
    M, K = a.shape
    N = b.shape[1]
    tm = _row_tile(M)
    grid = (M // tm, N // tn)
    extra, especs, epi = (), (), None
    if residual is not None:
        extra = (residual,)
        especs = (pl.BlockSpec((tm, tn), lambda i, n: (i, n)),)
        epi = lambda v, r: (v + r,)
    return _matmul(name, [(a, b)], [(pl.BlockSpec((tm, K), lambda i, n: (i, 0)), pl.BlockSpec((K, tn), lambda i, n: (0, n)))],
                   NN, grid, [_sds((M, N), out_dtype)], [pl.BlockSpec((tm, tn), lambda i, n: (i, n))],
                   extra=extra, extra_specs=especs, epilogue=epi)[0]


def mm_nn_colblk(name, a, b, out_dtype=F32):
    M, K = a.shape
    J, _, nb = b.shape
    tm = _row_tile(M)
    grid = (M // tm, J)
    return _matmul(name, [(a, b)], [(pl.BlockSpec((tm, K), lambda i, j: (i, 0)), pl.BlockSpec((None, K, nb), lambda i, j: (j, 0, 0)))],
                   NN, grid, [_sds((M, J * nb), out_dtype)], [pl.BlockSpec((tm, nb), lambda i, j: (i, j))])[0]


def mm_nt(name, a, w, tr, out_dtype=F32):
    M, C = a.shape
    R = w.shape[0]
    tm = _row_tile(M)
    grid = (M // tm, R // tr)
    return _matmul(name, [(a, w)], [(pl.BlockSpec((tm, C), lambda i, r: (i, 0)), pl.BlockSpec((tr, C), lambda i, r: (r, 0)))],
                   NT, grid, [_sds((M, R), out_dtype)], [pl.BlockSpec((tm, tr), lambda i, r: (i, r))])[0]


def mm_nt_red(name, a, w, tr, tk, out_dtype=F32):
    M, C = a.shape
    R = w.shape[0]
    tm = _row_tile(M)
    nk = C // tk
    grid = (M // tm, R // tr, nk)
    return _matmul(name, [(a, w)], [(pl.BlockSpec((tm, tk), lambda i, r, k: (i, k)), pl.BlockSpec((tr, tk), lambda i, r, k: (r, k)))],
                   NT, grid, [_sds((M, R), out_dtype)], [pl.BlockSpec((tm, tr), lambda i, r, k: (i, r))],
                   nred=nk, acc_shape=(tm, tr))[0]


def mm_nn_red(name, a, b, tn, tk, out_dtype=F32):
    M, K = a.shape
    N = b.shape[1]
    tm = _row_tile(M)
    nk = K // tk
    grid = (M // tm, N // tn, nk)
    return _matmul(name, [(a, b)], [(pl.BlockSpec((tm, tk), lambda i, n, k: (i, k)), pl.BlockSpec((tk, tn), lambda i, n, k: (k, n)))],
                   NN, grid, [_sds((M, N), out_dtype)], [pl.BlockSpec((tm, tn), lambda i, n, k: (i, n))],
                   nred=nk, acc_shape=(tm, tn))[0]


def mm_tn(name, x, dy, tr, tc, out_dtype=BF16):
    M, R = x.shape
    C = dy.shape[1]
    grid = (R // tr, C // tc)
    return _matmul(name, [(x, dy)], [(pl.BlockSpec((M, tr), lambda r, c: (0, r)), pl.BlockSpec((M, tc), lambda r, c: (0, c)))],
                   TN, grid, [_sds((R, C), out_dtype)], [pl.BlockSpec((tr, tc), lambda r, c: (r, c))])[0]


def mm_tn_colblk(name, x, dy, nb, out_dtype=BF16):
    M, R = x.shape
    J = dy.shape[1] // nb
    grid = (J,)
    return _matmul(name, [(x, dy)], [(pl.BlockSpec((M, R), lambda j: (0, 0)), pl.BlockSpec((M, nb), lambda j: (0, j)))],
                   TN, grid, [_sds((J, R, nb), out_dtype)], [pl.BlockSpec((None, R, nb), lambda j: (j, 0, 0))])[0]


def _rows(T):
    return min(256, T)


def rms_fwd(name, x, g, eps=NORM_EPS):
    T, D = x.shape
    tm = _rows(T)

    def body(x_ref, g_ref, o_ref):
        xv = x_ref[...]
        r = lax.rsqrt(jnp.mean(xv * xv, axis=-1, keepdims=True) + eps)
        o_ref[...] = (xv * r * g_ref[...]).astype(BF16)

    return _pcall(body, name=name, grid=(T // tm,),
                  in_specs=[pl.BlockSpec((tm, D), lambda i: (i, 0)), pl.BlockSpec((1, D), lambda i: (0, 0))],
                  out_specs=pl.BlockSpec((tm, D), lambda i: (i, 0)), out_shape=_sds((T, D), BF16),
                  compiler_params=_cparams(("arbitrary",)))(x, g)


def rms_bwd(name, x, g, dy, dres, eps=NORM_EPS):
    T, D = x.shape
    tm = _rows(T)

    def body(x_ref, g_ref, dy_ref, dr_ref, dx_ref, dxb_ref, dg_ref):
        i = pl.program_id(0)
        xv = x_ref[...]
        r = lax.rsqrt(jnp.mean(xv * xv, axis=-1, keepdims=True) + eps)
        xh = xv * r
        dyv = dy_ref[...]
        gd = dyv * g_ref[...]
        dx = r * (gd - xh * jnp.mean(gd * xh, axis=-1, keepdims=True)) + dr_ref[...]
        dx_ref[...] = dx
        dxb_ref[...] = dx.astype(BF16)

        @pl.when(i == 0)
        def _():
            dg_ref[...] = jnp.zeros_like(dg_ref)

        dg_ref[...] += jnp.sum(dyv * xh, axis=0, keepdims=True)

    row = pl.BlockSpec((tm, D), lambda i: (i, 0))
    vec = pl.BlockSpec((1, D), lambda i: (0, 0))
    return _pcall(body, name=name, grid=(T // tm,), in_specs=[row, vec, row, row], out_specs=[row, row, vec],
                  out_shape=[_sds((T, D), F32), _sds((T, D), BF16), _sds((1, D), F32)],
                  compiler_params=_cparams(("arbitrary",)))(x, g, dy, dres)


def gnorm_fwd(y, projp, g):
    T, D = y.shape
    tm = _rows(T)

    def body(y_ref, z_ref, g_ref, o_ref):
        yz = y_ref[...] * _silu(z_ref[...])
        r = lax.rsqrt(jnp.mean(yz * yz, axis=-1, keepdims=True) + SSM_NORM_EPS)
        o_ref[...] = (yz * r * g_ref[...]).astype(BF16)

    row = pl.BlockSpec((tm, D), lambda i: (i, 0))
    return _pcall(body, name="gnorm_fwd", grid=(T // tm,),
                  in_specs=[row, pl.BlockSpec((tm, D), lambda i: (i, OFF_Z // D)), pl.BlockSpec((1, D), lambda i: (0, 0))],
                  out_specs=row, out_shape=_sds((T, D), BF16), compiler_params=_cparams(("arbitrary",)))(y, projp, g)


def gnorm_bwd(y, projp, g, dyn):
    T, D = y.shape
    tm = _rows(T)

    def body(y_ref, z_ref, g_ref, dyn_ref, dy_ref, dz_ref, dg_ref):
        i = pl.program_id(0)
        yv, zv = y_ref[...], z_ref[...]
        sz = _silu(zv)
        yz = yv * sz
        r = lax.rsqrt(jnp.mean(yz * yz, axis=-1, keepdims=True) + SSM_NORM_EPS)
        xh = yz * r
        dv = dyn_ref[...]
        gd = dv * g_ref[...]
        dyz = r * (gd - xh * jnp.mean(gd * xh, axis=-1, keepdims=True))
        dy_ref[...] = dyz * sz
        dz_ref[...] = (dyz * yv * _dsilu(zv)).astype(BF16)

        @pl.when(i == 0)
        def _():
            dg_ref[...] = jnp.zeros_like(dg_ref)

        dg_ref[...] += jnp.sum(dv * xh, axis=0, keepdims=True)

    row = pl.BlockSpec((tm, D), lambda i: (i, 0))
    vec = pl.BlockSpec((1, D), lambda i: (0, 0))
    return _pcall(body, name="gnorm_bwd", grid=(T // tm,),
                  in_specs=[row, pl.BlockSpec((tm, D), lambda i: (i, OFF_Z // D)), vec, row], out_specs=[row, row, vec],
                  out_shape=[_sds((T, D), F32), _sds((T, D), BF16), _sds((1, D), F32)],
                  compiler_params=_cparams(("arbitrary",)))(y, projp, g, dyn)


def merge_fwd(projp, out_a, out_s):
    T, D = out_a.shape
    tm = _rows(T)

    def body(ga_ref, gs_ref, a_ref, s_ref, o_ref):
        o_ref[...] = (_sigmoid(ga_ref[...]) * a_ref[...] + _sigmoid(gs_ref[...]) * s_ref[...]).astype(BF16)

    row = pl.BlockSpec((tm, D), lambda i: (i, 0))
    return _pcall(body, name="merge_fwd", grid=(T // tm,),
                  in_specs=[pl.BlockSpec((tm, D), lambda i: (i, OFF_GA // D)), pl.BlockSpec((tm, D), lambda i: (i, OFF_GS // D)), row, row],
                  out_specs=row, out_shape=_sds((T, D), BF16), compiler_params=_cparams(("arbitrary",)))(projp, projp, out_a, out_s)


def merge_bwd(projp, out_a, out_s, dmerged):
    T, D = out_a.shape
    tm = _rows(T)

    def body(ga_ref, gs_ref, a_ref, s_ref, dm_ref, da_ref, ds_ref, dga_ref, dgs_ref):
        dm = dm_ref[...]
        sa, ss = _sigmoid(ga_ref[...]), _sigmoid(gs_ref[...])
        da_ref[...] = (dm * sa).astype(BF16)
        ds_ref[...] = (dm * ss).astype(BF16)
        dga_ref[...] = (dm * a_ref[...] * sa * (1.0 - sa)).astype(BF16)
        dgs_ref[...] = (dm * s_ref[...] * ss * (1.0 - ss)).astype(BF16)

    row = pl.BlockSpec((tm, D), lambda i: (i, 0))
    return _pcall(body, name="merge_bwd", grid=(T // tm,),
                  in_specs=[pl.BlockSpec((tm, D), lambda i: (i, OFF_GA // D)), pl.BlockSpec((tm, D), lambda i: (i, OFF_GS // D)), row, row, row],
                  out_specs=[row] * 4, out_shape=[_sds((T, D), BF16)] * 4,
                  compiler_params=_cparams(("arbitrary",)))(projp, projp, out_a, out_s, dmerged)


def head_fwd_bwd(h2, pg, pp, g_final, target):
    T, D = h2.shape
    tm = _rows(T)

    def body(h_ref, pg_ref, pp_ref, g_ref, t_ref, loss_ref, dh_ref, dpg_ref, dpp_ref, dg_ref):
        i = pl.program_id(0)
        s = _sigmoid(pg_ref[...])
        ppv = pp_ref[...]
        h3 = h_ref[...] + s * ppv
        r = lax.rsqrt(jnp.mean(h3 * h3, axis=-1, keepdims=True) + NORM_EPS)
        xh = h3 * r
        gv = g_ref[...]
        e = xh * gv - t_ref[...]
        dyo = e * (1.0 / D)
        gd = dyo * gv
        dh = r * (gd - xh * jnp.mean(gd * xh, axis=-1, keepdims=True))
        dh_ref[...] = dh
        dpg_ref[...] = (dh * ppv * s * (1.0 - s)).astype(BF16)
        dpp_ref[...] = (dh * s).astype(BF16)

        @pl.when(i == 0)
        def _():
            dg_ref[...] = jnp.zeros_like(dg_ref)
            loss_ref[...] = jnp.zeros_like(loss_ref)

        dg_ref[...] += jnp.sum(dyo * xh, axis=0, keepdims=True)
        part = 0.5 * jnp.sum(jnp.mean(e * e, axis=-1, keepdims=True), axis=0, keepdims=True)
        loss_ref[...] += jnp.broadcast_to(part, loss_ref.shape)

    row = pl.BlockSpec((tm, D), lambda i: (i, 0))
    vec = pl.BlockSpec((1, D), lambda i: (0, 0))
    return _pcall(body, name="head_fwd_bwd", grid=(T // tm,), in_specs=[row, row, row, vec, row],
                  out_specs=[pl.BlockSpec((1, 128), lambda i: (0, 0)), row, row, row, vec],
                  out_shape=[_sds((1, 128), F32), _sds((T, D), F32), _sds((T, D), BF16), _sds((T, D), BF16), _sds((1, D), F32)],
                  compiler_params=_cparams(("arbitrary",)))(h2, pg, pp, g_final, target)


def ffn_up(f, wg, wu):
    T, D = f.shape
    J, _, nb = wg.shape
    tm = _row_tile(T)

    def body(f_ref, wg_ref, wu_ref, g_ref, u_ref, a_ref):
        fv = f_ref[...]
        g = _dot(fv, wg_ref[...], NN)
        u = _dot(fv, wu_ref[...], NN)
        g_ref[...] = g
        u_ref[...] = u
        a_ref[...] = (_silu(g) * u).astype(BF16)

    wspec = pl.BlockSpec((None, D, nb), lambda i, j: (j, 0, 0))
    ospec = pl.BlockSpec((None, tm, nb), lambda i, j: (j, i, 0))
    return _pcall(body, name="ffn_up", grid=(T // tm, J), in_specs=[pl.BlockSpec((tm, D), lambda i, j: (i, 0)), wspec, wspec],
                  out_specs=[ospec] * 3, out_shape=[_sds((J, T, nb), F32), _sds((J, T, nb), F32), _sds((J, T, nb), BF16)],
                  compiler_params=_cparams(("arbitrary", "arbitrary")))(f, wg, wu)


def ffn_down(act, wd, h1):
    J, T, nb = act.shape
    D = wd.shape[2]
    tm = _row_tile(T)
    tn = 1024
    grid = (T // tm, D // tn, J)
    return _matmul("ffn_down", [(act, wd)],
                   [(pl.BlockSpec((None, tm, nb), lambda i, n, j: (j, i, 0)), pl.BlockSpec((None, nb, tn), lambda i, n, j: (j, 0, n)))],
                   NN, grid, [_sds((T, D), F32)], [pl.BlockSpec((tm, tn), lambda i, n, j: (i, n))], nred=J, acc_shape=(tm, tn),
                   extra=(h1,), extra_specs=(pl.BlockSpec((tm, tn), lambda i, n, j: (i, n)),), epilogue=lambda v, r: (v + r,))[0]


def ffn_down_bwd(dh2b, wd, gate, up):
    T, D = dh2b.shape
    J, nb, _ = wd.shape
    tm = _row_tile(T)
    ospec = pl.BlockSpec((None, tm, nb), lambda i, j: (j, i, 0))

    def epi(da, g, u):
        return (da * u * _dsilu(g), da * _silu(g))

    return _matmul("ffn_down_bwd", [(dh2b, wd)],
                   [(pl.BlockSpec((tm, D), lambda i, j: (i, 0)), pl.BlockSpec((None, nb, D), lambda i, j: (j, 0, 0)))],
                   NT, (T // tm, J), [_sds((J, T, nb), BF16)] * 2, [ospec, ospec],
                   extra=(gate, up), extra_specs=(ospec, ospec), epilogue=epi)


def ffn_up_bwd(dgate, dup, wg, wu):
    J, T, nb = dgate.shape
    D = wg.shape[1]
    tm = _row_tile(T)
    tr = 1024
    aspec = pl.BlockSpec((None, tm, nb), lambda i, r, j: (j, i, 0))
    wspec = pl.BlockSpec((None, tr, nb), lambda i, r, j: (j, r, 0))
    return _matmul("ffn_up_bwd", [(dgate, wg), (dup, wu)], [(aspec, wspec), (aspec, wspec)], NT, (T // tm, D // tr, J),
                   [_sds((T, D), F32)], [pl.BlockSpec((tm, tr), lambda i, r, j: (i, r))], nred=J, acc_shape=(tm, tr))[0]


def wgrad_rowblk_lhs(name, xb, dy, tc):
    J, T, nb = xb.shape
    C = dy.shape[1]
    return _matmul(name, [(xb, dy)],
                   [(pl.BlockSpec((None, T, nb), lambda j, c: (j, 0, 0)), pl.BlockSpec((T, tc), lambda j, c: (0, c)))],
                   TN, (J, C // tc), [_sds((J, nb, C), BF16)], [pl.BlockSpec((None, nb, tc), lambda j, c: (j, 0, c))])[0]


def wgrad_colblk_rhs(name, x, dyb, tr):
    T, R = x.shape
    J, _, nb = dyb.shape
    return _matmul(name, [(x, dyb)],
                   [(pl.BlockSpec((T, tr), lambda j, r: (0, r)), pl.BlockSpec((None, T, nb), lambda j, r: (j, 0, 0)))],
                   TN, (J, R // tr), [_sds((J, R, nb), BF16)], [pl.BlockSpec((None, tr, nb), lambda j, r: (j, r, 0))])[0]


def attn_br_bwd(dout_a, wab):
    T, D = dout_a.shape
    J, R, nb = wab.shape
    tm = _row_tile(T)
    return _matmul("attn_br_bwd", [(dout_a, wab)],
                   [(pl.BlockSpec((tm, nb), lambda i, j: (i, j)), pl.BlockSpec((None, R, nb), lambda i, j: (j, 0, 0)))],
                   NT, (T // tm, J), [_sds((T, R), BF16)], [pl.BlockSpec((tm, R), lambda i, j: (i, 0))], nred=J, acc_shape=(tm, R))[0]


def _adam_math(w, g, m, v):
    m2 = ADAM_B1 * m + (1.0 - ADAM_B1) * g
    v2 = ADAM_B2 * v + (1.0 - ADAM_B2) * (g * g)
    m_hat = m2 / (1.0 - ADAM_B1 ** ADAM_STEP)
    v_hat = v2 / (1.0 - ADAM_B2 ** ADAM_STEP)
    delta = -ADAM_LR * (m_hat / (jnp.sqrt(v_hat) + ADAM_EPS) + ADAM_WD * w)
    return delta, m2, v2


def _sum_partials(own, parts):
    g = None if own is None else own.astype(F32)
    if parts is not None:
        for s in range(parts.shape[0]):
            t = parts[s].astype(F32)
            g = t if g is None else g + t
    return g


def adamw(name, parts, w, m, v, own=None):
    R, C = w.shape
    tr, tc = R, C
    for cand in (256, 176, 128, 64, 32, 16, 8):
        if R % cand == 0 and R > cand:
            tr = cand
            break
    if tr == R and R > 256:
        tc = 256
    given = [a for a in (parts, own) if a is not None]

    def body(*refs):
        p_ref = refs[0] if parts is not None else None
        o_ref = refs[len(given) - 1] if own is not None else None
        w_ref, m_ref, v_ref, g_ref, d_ref, m2_ref, v2_ref = refs[-7:]
        g = _sum_partials(None if o_ref is None else o_ref[...], p_ref)
        d, m2, v2 = _adam_math(w_ref[...], g, m_ref[...], v_ref[...])
        g_ref[...] = g
        d_ref[...] = d
        m2_ref[...] = m2
        v2_ref[...] = v2

    blk = pl.BlockSpec((tr, tc), lambda i, j: (i, j))
    specs = ([] if parts is None else [pl.BlockSpec((parts.shape[0], tr, tc), lambda i, j: (0, i, j))]) + [blk] * (3 + (own is not None))
    return _pcall(body, name=name, grid=(R // tr, C // tc), in_specs=specs,
                  out_specs=[blk] * 4, out_shape=[_sds((R, C), F32)] * 4,
                  compiler_params=_cparams(("arbitrary", "arbitrary")))(*given, w, m, v)


def exchange(name, arrays, mode):
    n = len(arrays)
    out_shapes = []
    for a in arrays:
        shp = a.shape if mode == "scatter" else (N_DEV,) + a.shape
        out_shapes.append(_sds(shp, a.dtype))

    def body(*refs):
        ins, outs = refs[:n], refs[n:2 * n]
        send_sems, recv_sems, local_sems = refs[2 * n:]
        x, y, c = lax.axis_index("x"), lax.axis_index("y"), lax.axis_index("c")
        me = 4 * x + 2 * y + c

        def src(a, dest):
            return ins[a].at[dest] if mode == "scatter" else ins[a]

        local = [pltpu.make_async_copy(src(a, me), outs[a].at[me], local_sems.at[a]) for a in range(n)]
        for cp in local:
            cp.start()
        remote = []
        for k in range(1, N_DEV):
            px = 1 - x if k & 4 else x
            py = 1 - y if k & 2 else y
            pc = 1 - c if k & 1 else c
            peer = 4 * px + 2 * py + pc
            for a in range(n):
                cp = pltpu.make_async_remote_copy(src_ref=src(a, peer), dst_ref=outs[a].at[me],
                                                  send_sem=send_sems.at[a * 7 + k - 1], recv_sem=recv_sems.at[a * 7 + k - 1],
                                                  device_id=(px, py, pc), device_id_type=pl.DeviceIdType.MESH)
                cp.start()
                arrival = pltpu.make_async_remote_copy(src_ref=src(a, peer), dst_ref=outs[a].at[peer],
                                                       send_sem=send_sems.at[a * 7 + k - 1], recv_sem=recv_sems.at[a * 7 + k - 1],
                                                       device_id=(px, py, pc), device_id_type=pl.DeviceIdType.MESH)
                remote.append((cp, arrival))
        for cp, arrival in remote:
            cp.wait_send()
            arrival.wait_recv()
        for cp in local:
            cp.wait()

    any_spec = pl.BlockSpec(memory_space=pl.ANY)
    return _pcall(body, name=name, in_specs=[any_spec] * n, out_specs=[any_spec] * n, out_shape=out_shapes,
                  scratch_shapes=[pltpu.SemaphoreType.DMA((7 * n,)), pltpu.SemaphoreType.DMA((7 * n,)), pltpu.SemaphoreType.DMA((n,))],
                  compiler_params=pltpu.CompilerParams(has_side_effects=True))(*arrays)


_HBM = pl.BlockSpec(memory_space=pltpu.HBM)
_SEM = pl.BlockSpec(memory_space=pltpu.SEMAPHORE)
_ANY = pl.BlockSpec(memory_space=pl.ANY)
_SPLIT_PARAMS = dict(compiler_params=pltpu.CompilerParams(has_side_effects=pltpu.SideEffectType.DATAFLOW_SIDE_EFFECTING))
ICI_SAME_CORE = (2, 4, 6)
ALL_PEERS = (1, 2, 3, 4, 5, 6, 7)


def _mesh_pos():
    x, y, c = lax.axis_index("x"), lax.axis_index("y"), lax.axis_index("c")
    return x, y, c, 4 * x + 2 * y + c


def _peer_of(k, x, y, c):
    px = 1 - x if k & 4 else x
    py = 1 - y if k & 2 else y
    pc = 1 - c if k & 1 else c
    return (px, py, pc), 4 * px + 2 * py + pc


def _split_copies(mode, ks, srcs, lands, send_sems, recv_sems):
    x, y, c, me = _mesh_pos()
    pairs = []
    for a in range(len(lands)):
        for j, k in enumerate(ks):
            dev, peer = _peer_of(k, x, y, c)
            i = a * len(ks) + j
            if mode == "gather":
                s_out, d_out, d_in = lands[a].at[me], lands[a].at[me], lands[a].at[peer]
            elif mode == "scatter":
                s_out, d_out, d_in = srcs[a].at[peer], lands[a].at[k - 1], lands[a].at[k - 1]
            else:
                dev, _ = _peer_of(1, x, y, c)
                _, theirs = _peer_of(k | 1, x, y, c)
                s_out, d_out, d_in = lands[a].at[peer], lands[a].at[peer], lands[a].at[theirs]
            both = [pltpu.make_async_remote_copy(src_ref=s_out, dst_ref=d, send_sem=send_sems.at[i], recv_sem=recv_sems.at[i],
                                                 device_id=dev, device_id_type=pl.DeviceIdType.MESH) for d in (d_out, d_in)]
            pairs.append(tuple(both))
    return pairs


def split_start(name, mode, ks, srcs, lands=None, after=None):
    n, nk = len(srcs) if lands is None else len(lands), len(ks)
    srcs = [pltpu.with_memory_space_constraint(s, pltpu.HBM) for s in srcs]
    if lands is None:
        shapes = [((N_DEV,) + s.shape) if mode == "gather" else ((N_DEV - 1,) + s.shape[1:]) for s in srcs]
        lands = [lax.empty(shp, s.dtype) for shp, s in zip(shapes, srcs)]
    lands = [pltpu.with_memory_space_constraint(l, pltpu.HBM) for l in lands]
    both = srcs + lands
    extra = [] if after is None else [after]

    def body(*refs):
        src_refs, land_refs = refs[:len(srcs)], refs[len(srcs):len(both)]
        send_sems, recv_sems = refs[len(both) + len(extra)], refs[len(both) + len(extra) + 1]
        token = refs[-1]
        for out, _ in _split_copies(mode, ks, src_refs, land_refs, send_sems, recv_sems):
            out.start()
        token[...] = jnp.zeros_like(token)

    out_shape = (pltpu.SemaphoreType.DMA((n * nk,)), pltpu.SemaphoreType.DMA((n * nk,)),
                 *[pltpu.HBM(a.shape, a.dtype) for a in both], _sds((8, 128), F32))
    res = _raw_call(body, name=name, out_shape=out_shape, in_specs=[_HBM] * len(both) + [_ANY] * len(extra),
                    out_specs=(_SEM, _SEM, *[_HBM] * len(both), pl.BlockSpec(memory_space=pltpu.VMEM)),
                    input_output_aliases={i: 2 + i for i in range(len(both))}, **_SPLIT_PARAMS)(*both, *extra)
    _PENDING.append(res[-1])
    return dict(mode=mode, ks=ks, sems=(res[0], res[1]), srcs=list(res[2:2 + len(srcs)]),
                lands=list(res[2 + len(srcs):2 + len(both)]), token=res[-1])


def split_wait(name, h, after):
    ns = len(h["srcs"])
    both = h["srcs"] + h["lands"]

    def body(*refs):
        src_refs, land_refs = refs[:ns], refs[ns:len(both)]
        send_sems, recv_sems = refs[len(both)], refs[len(both) + 1]
        for out, arriving in _split_copies(h["mode"], h["ks"], src_refs, land_refs, send_sems, recv_sems):
            out.wait_send()
            arriving.wait_recv()

    res = _raw_call(body, name=name, out_shape=tuple(pltpu.HBM(a.shape, a.dtype) for a in both),
                    in_specs=[_HBM] * len(both) + [_SEM, _SEM, _ANY], out_specs=tuple([_HBM] * len(both)),
                    input_output_aliases={i: i for i in range(len(both))}, **_SPLIT_PARAMS)(*both, *h["sems"], after)
    return list(res[:ns]), list(res[ns:])


FORWARD_BLOCKS = (0, 2, 4, 6)


def _rope_parts(pos_ref, inv_ref):
    ang = pos_ref[...] * inv_ref[...]
    return jnp.cos(ang), jnp.sin(ang)


def _rot_half(t):
    lane = lax.broadcasted_iota(jnp.int32, t.shape, 1)
    return jnp.where((lane % HEAD_DIM) < HEAD_DIM // 2, -pltpu.roll(t, 128 - HEAD_DIM // 2, 1), pltpu.roll(t, HEAD_DIM // 2, 1))


def _attn_mask(n):
    row = lax.broadcasted_iota(jnp.int32, (BLK, 2 * BLK), 0)
    col = lax.broadcasted_iota(jnp.int32, (BLK, 2 * BLK), 1)
    dist = row + BLK - col
    return (dist >= 0) & (dist < BLK) & ((n * BLK - BLK + col) >= 0)


def _attn_specs(T):
    prev = lambda n: jnp.maximum(n - 1, 0)
    kc = pl.BlockSpec((BLK, KV_DIM), lambda n: (n, OFF_K // KV_DIM))
    kp = pl.BlockSpec((BLK, KV_DIM), lambda n: (prev(n), OFF_K // KV_DIM))
    vc = pl.BlockSpec((BLK, KV_DIM), lambda n: (n, OFF_V // KV_DIM))
    vp = pl.BlockSpec((BLK, KV_DIM), lambda n: (prev(n), OFF_V // KV_DIM))
    pc = pl.BlockSpec((BLK, 1), lambda n: (n, 0))
    pp = pl.BlockSpec((BLK, 1), lambda n: (prev(n), 0))
    inv = pl.BlockSpec((1, 128), lambda n: (0, 0))
    sink = pl.BlockSpec(memory_space=pltpu.SMEM)
    return kc, kp, vc, vp, pc, pp, inv, sink


def _softmax_sink(sc, valid, sink):
    sc = jnp.where(valid, sc * (HEAD_DIM ** -0.5), -1e30)
    m = jnp.maximum(jnp.max(sc, axis=1, keepdims=True), sink)
    e = jnp.exp(sc - m)
    es = jnp.exp(sink - m)
    den = jnp.sum(e, axis=1, keepdims=True) + es
    return e / den, es / den


def attn_fwd(projp, posf, inv128, sinks):
    T = projp.shape[0]
    kc, kp, vc, vp, pc, pp, inv, sink = _attn_specs(T)

    def body(q_ref, kc_ref, kp_ref, vc_ref, vp_ref, pc_ref, pp_ref, inv_ref, sink_ref, o_ref, qr_ref, kr_ref):
        n = pl.program_id(0)
        cos_c, sin_c = _rope_parts(pc_ref, inv_ref)
        cos_p, sin_p = _rope_parts(pp_ref, inv_ref)
        valid = _attn_mask(n)
        k_c, k_p = [], []
        for s in range(KV_DIM // 128):
            t = kc_ref[:, 128 * s:128 * (s + 1)]
            k_c.append((t * cos_c + _rot_half(t) * sin_c).astype(BF16))
            kr_ref[:, 128 * s:128 * (s + 1)] = k_c[s]
            t = kp_ref[:, 128 * s:128 * (s + 1)]
            k_p.append((t * cos_p + _rot_half(t) * sin_p).astype(BF16))
        kcat, vcat = [], []
        for hk in range(KV_HEADS):
            lo = HEAD_DIM * (hk % 2)
            kcat.append(jnp.concatenate([k_p[hk // 2][:, lo:lo + HEAD_DIM], k_c[hk // 2][:, lo:lo + HEAD_DIM]], axis=0))
            vcat.append(jnp.concatenate([vp_ref[:, HEAD_DIM * hk:HEAD_DIM * (hk + 1)], vc_ref[:, HEAD_DIM * hk:HEAD_DIM * (hk + 1)]], axis=0)
                        .astype(BF16))
        q_heads = []
        for s in range(Q_DIM // 128):
            t = q_ref[:, 128 * s:128 * (s + 1)]
            qs = (t * cos_c + _rot_half(t) * sin_c).astype(BF16)
            qr_ref[:, 128 * s:128 * (s + 1)] = qs
            q_heads += [qs[:, :HEAD_DIM], qs[:, HEAD_DIM:]]
        G = ATTN_HEADS // KV_HEADS
        scores = [_dot(q_heads[hq], kcat[hq // G], NT) for hq in range(ATTN_HEADS)]
        probs = [_softmax_sink(scores[hq], valid, sink_ref[0, hq])[0] for hq in range(ATTN_HEADS)]
        outs = [_dot(probs[hq], vcat[hq // G], NN) for hq in range(ATTN_HEADS)]
        for s in range(Q_DIM // 128):
            o_ref[:, 128 * s:128 * (s + 1)] = jnp.concatenate([outs[2 * s], outs[2 * s + 1]], axis=1).astype(BF16)

    qspec = pl.BlockSpec((BLK, Q_DIM), lambda n: (n, OFF_Q // Q_DIM))
    orow = pl.BlockSpec((BLK, Q_DIM), lambda n: (n, 0))
    krow = pl.BlockSpec((BLK, KV_DIM), lambda n: (n, 0))
    return _pcall(body, name="attn_fwd", grid=(T // BLK,), in_specs=[qspec, kc, kp, vc, vp, pc, pp, inv, sink],
                  out_specs=[orow, orow, krow], out_shape=[_sds((T, Q_DIM), BF16), _sds((T, Q_DIM), BF16), _sds((T, KV_DIM), BF16)],
                  compiler_params=_cparams(("arbitrary",)))(projp, projp, projp, projp, projp, posf, posf, inv128, sinks)


def attn_bwd(qr, kr, projp, dattn, posf, inv128, sinks):
    T = projp.shape[0]
    _, _, vc, vp, pc, pp, inv, sink = _attn_specs(T)
    G = ATTN_HEADS // KV_HEADS

    def body(qr_ref, krc_ref, krp_ref, vc_ref, vp_ref, do_ref, pc_ref, pp_ref, inv_ref, sink_ref, dq_ref, dk_ref, dv_ref, dsk_ref):
        n = pl.program_id(0)

        @pl.when(n == 0)
        def _():
            dk_ref[...] = jnp.zeros_like(dk_ref)
            dv_ref[...] = jnp.zeros_like(dv_ref)
            dsk_ref[...] = jnp.zeros_like(dsk_ref)

        cos_c, sin_c = _rope_parts(pc_ref, inv_ref)
        cos_p, sin_p = _rope_parts(pp_ref, inv_ref)
        valid = _attn_mask(n)
        lane = lax.broadcasted_iota(jnp.int32, (1, 128), 1)
        kcat, vcat = [], []
        for hk in range(KV_HEADS):
            ksl = slice(HEAD_DIM * hk, HEAD_DIM * (hk + 1))
            kcat.append(jnp.concatenate([krp_ref[:, ksl], krc_ref[:, ksl]], axis=0))
            vcat.append(jnp.concatenate([vp_ref[:, ksl], vc_ref[:, ksl]], axis=0).astype(BF16))
        H = range(ATTN_HEADS)
        q_heads = [qr_ref[:, HEAD_DIM * hq:HEAD_DIM * (hq + 1)] for hq in H]
        do_heads = [do_ref[:, HEAD_DIM * hq:HEAD_DIM * (hq + 1)] for hq in H]
        soft = [_softmax_sink(_dot(q_heads[hq], kcat[hq // G], NT), valid, sink_ref[0, hq]) for hq in H]
        dps = [_dot(do_heads[hq], vcat[hq // G], NT) for hq in H]
        deltas = [jnp.sum(soft[hq][0] * dps[hq], axis=1, keepdims=True) for hq in H]
        dss = [(soft[hq][0] * (dps[hq] - deltas[hq]) * (HEAD_DIM ** -0.5)).astype(BF16) for hq in H]
        pbs = [soft[hq][0].astype(BF16) for hq in H]
        dsk = jnp.zeros((1, 128), F32)
        for hq in H:
            dsk = dsk + jnp.where(lane == hq, -jnp.sum(soft[hq][1] * deltas[hq], axis=0, keepdims=True), 0.0)
        dsk_ref[...] += dsk
        dq_heads = [_dot(dss[hq], kcat[hq // G], NN) for hq in H]
        dk_parts = [_dot(dss[hq], q_heads[hq], TN) for hq in H]
        dv_parts = [_dot(pbs[hq], do_heads[hq], TN) for hq in H]
        dk_heads = [sum(dk_parts[G * hk + 1:G * (hk + 1)], dk_parts[G * hk]) for hk in range(KV_HEADS)]
        dv_heads = [sum(dv_parts[G * hk + 1:G * (hk + 1)], dv_parts[G * hk]) for hk in range(KV_HEADS)]
        for s in range(Q_DIM // 128):
            t = jnp.concatenate([dq_heads[2 * s], dq_heads[2 * s + 1]], axis=1)
            dq_ref[:, 128 * s:128 * (s + 1)] = (t * cos_c - _rot_half(t) * sin_c).astype(BF16)
        cur = pl.ds(pl.multiple_of(n * BLK, BLK), BLK)
        prv = pl.ds(pl.multiple_of(jnp.maximum(n - 1, 0) * BLK, BLK), BLK)
        for s in range(KV_DIM // 128):
            tc = jnp.concatenate([dk_heads[2 * s][BLK:], dk_heads[2 * s + 1][BLK:]], axis=1)
            tp = jnp.concatenate([dk_heads[2 * s][:BLK], dk_heads[2 * s + 1][:BLK]], axis=1)
            cols = slice(128 * s, 128 * (s + 1))
            dk_ref[cur, cols] += tc * cos_c - _rot_half(tc) * sin_c
            dk_ref[prv, cols] += tp * cos_p - _rot_half(tp) * sin_p
            dv_ref[cur, cols] += jnp.concatenate([dv_heads[2 * s][BLK:], dv_heads[2 * s + 1][BLK:]], axis=1)
            dv_ref[prv, cols] += jnp.concatenate([dv_heads[2 * s][:BLK], dv_heads[2 * s + 1][:BLK]], axis=1)

    qrow = pl.BlockSpec((BLK, Q_DIM), lambda n: (n, 0))
    krc = pl.BlockSpec((BLK, KV_DIM), lambda n: (n, 0))
    krp = pl.BlockSpec((BLK, KV_DIM), lambda n: (jnp.maximum(n - 1, 0), 0))
    whole = pl.BlockSpec((T, KV_DIM), lambda n: (0, 0))
    return _pcall(body, name="attn_bwd", grid=(T // BLK,), in_specs=[qrow, krc, krp, vc, vp, qrow, pc, pp, inv, sink],
                  out_specs=[qrow, whole, whole, pl.BlockSpec((1, 128), lambda n: (0, 0))],
                  out_shape=[_sds((T, Q_DIM), BF16), _sds((T, KV_DIM), F32), _sds((T, KV_DIM), F32), _sds((1, 128), F32)],
                  compiler_params=_cparams(("arbitrary",)))(qr, kr, kr, projp, projp, dattn, posf, posf, inv128, sinks)


CONV_CB = 256


def _shift_down(x, s):
    row = lax.broadcasted_iota(jnp.int32, x.shape, 0)
    return jnp.where(row >= s, pltpu.roll(x, s, 0), 0.0)


def _shift_up(x, s):
    T = x.shape[0]
    row = lax.broadcasted_iota(jnp.int32, x.shape, 0)
    return jnp.where(row < T - s, pltpu.roll(x, T - s, 0), 0.0)


def _conv_pre(x, w_ref, b_ref):
    acc = x * w_ref[CONV_WIDTH - 1:CONV_WIDTH, :] + b_ref[...]
    for s in range(1, CONV_WIDTH):
        acc = acc + _shift_down(x, s) * w_ref[CONV_WIDTH - 1 - s:CONV_WIDTH - s, :]
    return acc


def conv_fwd(projp, conv_w, conv_b):
    T = projp.shape[0]

    def body(x_ref, w_ref, b_ref, o_ref):
        o_ref[...] = _silu(_conv_pre(x_ref[...], w_ref, b_ref))

    return _pcall(body, name="conv_fwd", grid=(CONV_DIM // CONV_CB,),
                  in_specs=[pl.BlockSpec((T, CONV_CB), lambda c: (0, OFF_XBC // CONV_CB + c)),
                            pl.BlockSpec((CONV_WIDTH, CONV_CB), lambda c: (0, c)), pl.BlockSpec((1, CONV_CB), lambda c: (0, c))],
                  out_specs=pl.BlockSpec((T, CONV_CB), lambda c: (0, c)), out_shape=_sds((T, CONV_DIM), F32),
                  compiler_params=_cparams(("arbitrary",)))(projp, conv_w, conv_b)


def conv_bwd(name, projp, dact, conv_w, conv_b, col0):
    T, C = dact.shape
    c0 = col0 // CONV_CB

    def body(x_ref, da_ref, w_ref, b_ref, dx_ref, dw_ref, db_ref):
        x = x_ref[...]
        dpre = da_ref[...] * _dsilu(_conv_pre(x, w_ref, b_ref))
        dx = dpre * w_ref[CONV_WIDTH - 1:CONV_WIDTH, :]
        dw_ref[CONV_WIDTH - 1:CONV_WIDTH, :] = jnp.sum(dpre * x, axis=0, keepdims=True)
        for s in range(1, CONV_WIDTH):
            i = CONV_WIDTH - 1 - s
            dx = dx + _shift_up(dpre, s) * w_ref[i:i + 1, :]
            dw_ref[i:i + 1, :] = jnp.sum(dpre * _shift_down(x, s), axis=0, keepdims=True)
        dx_ref[...] = dx.astype(BF16)
        db_ref[...] = jnp.sum(dpre, axis=0, keepdims=True)

    return _pcall(body, name=name, grid=(C // CONV_CB,),
                  in_specs=[pl.BlockSpec((T, CONV_CB), lambda c: (0, OFF_XBC // CONV_CB + c0 + c)),
                            pl.BlockSpec((T, CONV_CB), lambda c: (0, c)),
                            pl.BlockSpec((CONV_WIDTH, CONV_CB), lambda c: (0, c0 + c)), pl.BlockSpec((1, CONV_CB), lambda c: (0, c0 + c))],
                  out_specs=[pl.BlockSpec((T, CONV_CB), lambda c: (0, c)), pl.BlockSpec((CONV_WIDTH, CONV_CB), lambda c: (0, c)),
                             pl.BlockSpec((1, CONV_CB), lambda c: (0, c))],
                  out_shape=[_sds((T, C), BF16), _sds((CONV_WIDTH, C), F32), _sds((1, C), F32)],
                  compiler_params=_cparams(("arbitrary",)))(projp, dact, conv_w, conv_b)


def _softplus(x):
    return jnp.maximum(x, 0.0) + jnp.log1p(jnp.exp(-jnp.abs(x)))


def _tri(lower):
    r = lax.broadcasted_iota(jnp.int32, (BLK, BLK), 0)
    c = lax.broadcasted_iota(jnp.int32, (BLK, BLK), 1)
    return (r >= c) if lower else (c >= r)


def _ssd_chunk_setup(dt_ref, dtb_ref, alog_ref):
    raw = dt_ref[...] + dtb_ref[...]
    dt = _softplus(raw)
    aneg = -jnp.exp(alog_ref[...])
    a = dt * aneg
    acs = jnp.dot(_tri(True).astype(F32), a, precision=lax.Precision.HIGHEST, preferred_element_type=F32)
    return raw, dt, aneg, acs, acs.T


def _ssd_specs(T, rev):
    nc = T // BLK
    ci = (lambda c: nc - 1 - c) if rev else (lambda c: c)
    xs = pl.BlockSpec((BLK, D_INNER), lambda c: (ci(c), 0))
    bm = pl.BlockSpec((BLK, SSM_GROUPS * D_STATE), lambda c: (ci(c), D_INNER // (SSM_GROUPS * D_STATE)))
    cm = pl.BlockSpec((BLK, SSM_GROUPS * D_STATE), lambda c: (ci(c), D_INNER // (SSM_GROUPS * D_STATE) + 1))
    dt = pl.BlockSpec((BLK, DT_PAD), lambda c: (ci(c), OFF_DT // DT_PAD))
    v128 = pl.BlockSpec((1, 128), lambda c: (0, 0))
    dfull = pl.BlockSpec((1, D_INNER), lambda c: (0, 0))
    st = pl.BlockSpec((None, SSM_HEADS, HEAD_DIM, D_STATE), lambda c: (ci(c), 0, 0, 0))
    return xs, bm, cm, dt, v128, dfull, st, ci


GW = HEADS_PER_GROUP * HEAD_DIM


def _expanders():
    e = np.zeros((SSM_GROUPS, 128, GW), np.float32)
    for g in range(SSM_GROUPS):
        for hh in range(HEADS_PER_GROUP):
            e[g, HEADS_PER_GROUP * g + hh, HEAD_DIM * hh:HEAD_DIM * (hh + 1)] = 1.0
    return jnp.asarray(e), jnp.asarray(np.transpose(e, (0, 2, 1)).copy())


def _dotx(a, b):
    return jnp.dot(a, b, precision=lax.Precision.HIGHEST, preferred_element_type=F32)


def _decay(acs, acsT, h, tril):
    return jnp.where(tril, jnp.exp(jnp.where(tril, acs[:, h:h + 1] - acsT[h:h + 1, :], 0.0)), 0.0)


def ssd_fwd(xbc, projp, dtb, alog, dfull):
    T = xbc.shape[0]
    nc = T // BLK
    xs, bm, cm, dts, v128, dfs, st, _ = _ssd_specs(T, False)
    E, _ = _expanders()

    def body(xs_ref, b_ref, c_ref, dt_ref, dtb_ref, alog_ref, d_ref, e_ref, y_ref, st_ref, h_scr):
        c = pl.program_id(0)

        @pl.when(c == 0)
        def _():
            h_scr[...] = jnp.zeros_like(h_scr)

        _, dt, _, acs, acsT = _ssd_chunk_setup(dt_ref, dtb_ref, alog_ref)
        tril = _tri(True)
        alast = acs[BLK - 1:BLK, :]
        eacs = jnp.exp(acs)
        wmat = jnp.exp(alast - acs)
        gam = jnp.exp(alast)
        for g in range(SSM_GROUPS):
            gl = slice(GW * g, GW * (g + 1))
            hsl = slice(HEADS_PER_GROUP * g, HEADS_PER_GROUP * (g + 1))
            heads = [HEADS_PER_GROUP * g + hh for hh in range(HEADS_PER_GROUP)]
            Eg = e_ref[g]
            B = b_ref[:, D_STATE * g:D_STATE * (g + 1)].astype(BF16)
            C = c_ref[:, D_STATE * g:D_STATE * (g + 1)].astype(BF16)
            cb = _dot(C, B, NT)
            x_g = xs_ref[:, gl]
            xd_g = x_g * _dotx(dt, Eg)
            hold = h_scr[hsl]
            st_ref[hsl] = hold
            hcat = hold.reshape(GW, D_STATE)
            yoff = _dotx(eacs, Eg) * _dot(C, hcat, NT)
            S = _dot(xd_g * _dotx(wmat, Eg), B, TN)
            Ms = [cb * _decay(acs, acsT, h, tril) for h in heads]
            ys = [_dot(Ms[hh], xd_g[:, HEAD_DIM * hh:HEAD_DIM * (hh + 1)], NN) for hh in range(HEADS_PER_GROUP)]
            for hh, h in enumerate(heads):
                h_scr[h] = gam[:, h:h + 1] * hold[hh] + S[HEAD_DIM * hh:HEAD_DIM * (hh + 1)]
            y_ref[:, gl] = jnp.concatenate(ys, axis=1) + yoff + d_ref[:, gl] * x_g

    espec = pl.BlockSpec((SSM_GROUPS, 128, GW), lambda c: (0, 0, 0))
    return _pcall(body, name="ssd_fwd", grid=(nc,), in_specs=[xs, bm, cm, dts, v128, v128, dfs, espec],
                  out_specs=[xs, st], out_shape=[_sds((T, D_INNER), F32), _sds((nc, SSM_HEADS, HEAD_DIM, D_STATE), F32)],
                  scratch_shapes=[pltpu.VMEM((SSM_HEADS, HEAD_DIM, D_STATE), F32)],
                  compiler_params=_cparams(("arbitrary",)))(xbc, xbc, xbc, projp, dtb, alog, dfull, E)


def ssd_bwd(xbc, projp, dtb, alog, dfull, states, dy):
    T = xbc.shape[0]
    nc = T // BLK
    xs, bm, cm, dts, v128, dfs, st, ci = _ssd_specs(T, True)
    gn = SSM_GROUPS * D_STATE
    E, ET = _expanders()

    def body(xs_ref, b_ref, c_ref, dt_ref, dtb_ref, alog_ref, d_ref, st_ref, dy_ref, e_ref, et_ref,
             dxs_ref, dB_ref, dC_ref, ddt_ref, dal_ref, dD_ref, ddtb_ref, dh_scr):
        i = pl.program_id(0)

        @pl.when(i == 0)
        def _():
            dh_scr[...] = jnp.zeros_like(dh_scr)
            dal_ref[...] = jnp.zeros_like(dal_ref)
            dD_ref[...] = jnp.zeros_like(dD_ref)
            ddtb_ref[...] = jnp.zeros_like(ddtb_ref)

        raw, dt, aneg, acs, acsT = _ssd_chunk_setup(dt_ref, dtb_ref, alog_ref)
        tril = _tri(True)
        lane = lax.broadcasted_iota(jnp.int32, (BLK, 128), 1)
        sub = lax.broadcasted_iota(jnp.int32, (BLK, 128), 0)
        alast = acs[BLK - 1:BLK, :]
        eacs = jnp.exp(acs)
        wmat = jnp.exp(alast - acs)
        gam = jnp.exp(alast)
        gcol = jnp.exp(acsT[:, BLK - 1:BLK])
        ds_col = jnp.zeros((BLK, 128), F32)
        ds_row = jnp.zeros((BLK, 128), F32)
        ddt_col = jnp.zeros((BLK, 128), F32)
        dDm = jnp.zeros((BLK, 128), F32)
        hl = [slice(HEAD_DIM * hh, HEAD_DIM * (hh + 1)) for hh in range(HEADS_PER_GROUP)]
        for g in range(SSM_GROUPS):
            gl = slice(GW * g, GW * (g + 1))
            gs = slice(D_STATE * g, D_STATE * (g + 1))
            hsl = slice(HEADS_PER_GROUP * g, HEADS_PER_GROUP * (g + 1))
            heads = [HEADS_PER_GROUP * g + hh for hh in range(HEADS_PER_GROUP)]
            Eg, ETg = e_ref[g], et_ref[g]
            B = b_ref[:, gs].astype(BF16)
            C = c_ref[:, gs].astype(BF16)
            cb = _dot(C, B, NT)
            x_g, dy_g = xs_ref[:, gl], dy_ref[:, gl]
            dt_x, w_x = _dotx(dt, Eg), _dotx(wmat, Eg)
            xd_g = x_g * dt_x
            dye = dy_g * _dotx(eacs, Eg)
            hcat = st_ref[hsl].reshape(GW, D_STATE)
            dSv = dh_scr[hsl]
            dScat = dSv.reshape(GW, D_STATE)
            dDm = dDm + _dotx(dy_g * x_g, ETg)
            dH_y = _dot(dye, C, TN)
            dC_g = _dot(dye, hcat, NN)
            ds_col = ds_col + _dotx(dye * _dot(C, hcat, NT), ETg)
            dxdw = _dot(B, dScat, NT)
            dB_g = _dot(xd_g * w_x, dScat, NN)
            dww = _dotx(xd_g * dxdw, ETg) * wmat
            ds_col = ds_col - dww + jnp.where(sub == BLK - 1, jnp.sum(dww, axis=0, keepdims=True), 0.0)
            hd = jnp.sum(_dotx(Eg, dScat * hcat), axis=1, keepdims=True) * gcol
            ds_row = ds_row - jnp.where(lane == BLK - 1, hd, 0.0)
            decays = [_decay(acs, acsT, h, tril) for h in heads]
            Ms = [cb * d for d in decays]
            dMs = [_dot(dy_g[:, hl[hh]], xd_g[:, hl[hh]], NT) for hh in range(HEADS_PER_GROUP)]
            dxd1 = [_dot(Ms[hh], dy_g[:, hl[hh]], TN) for hh in range(HEADS_PER_GROUP)]
            dG = jnp.zeros((BLK, BLK), F32)
            for hh, h in enumerate(heads):
                Q = dMs[hh] * Ms[hh]
                ds_col = ds_col + jnp.where(lane == h, jnp.sum(Q, axis=1, keepdims=True), 0.0)
                ds_row = ds_row + jnp.where(sub == h, jnp.sum(Q, axis=0, keepdims=True), 0.0)
                dG = dG + dMs[hh] * decays[hh]
            dxd_g = jnp.concatenate(dxd1, axis=1) + w_x * dxdw
            dxs_ref[:, gl] = d_ref[:, gl] * dy_g + dxd_g * dt_x
            ddt_col = ddt_col + _dotx(dxd_g * x_g, ETg)
            dC_ref[:, gs] = dC_g + _dot(dG, B, NN)
            dB_ref[:, gs] = dB_g + _dot(dG, C, TN)
            for hh, h in enumerate(heads):
                dh_scr[h] = gam[:, h:h + 1] * dSv[hh] + dH_y[hl[hh]]
        ds_all = ds_col - ds_row.T
        da = jnp.dot(_tri(False).astype(F32), ds_all, precision=lax.Precision.HIGHEST, preferred_element_type=F32)
        ddt = ddt_col + da * aneg
        draw = jnp.where(lane < SSM_HEADS, ddt * _sigmoid(raw), 0.0)
        ddt_ref[...] = draw.astype(BF16)
        dal_ref[...] += jnp.sum(da * dt, axis=0, keepdims=True) * aneg
        ddtb_ref[...] += jnp.sum(draw, axis=0, keepdims=True)
        dD_ref[...] += jnp.sum(dDm, axis=0, keepdims=True)

    gblk = pl.BlockSpec((BLK, gn), lambda c: (ci(c), 0))
    espec = pl.BlockSpec((SSM_GROUPS, 128, GW), lambda c: (0, 0, 0))
    etspec = pl.BlockSpec((SSM_GROUPS, GW, 128), lambda c: (0, 0, 0))
    return _pcall(body, name="ssd_bwd", grid=(nc,), in_specs=[xs, bm, cm, dts, v128, v128, dfs, st, xs, espec, etspec],
                  out_specs=[xs, gblk, gblk, pl.BlockSpec((BLK, DT_PAD), lambda c: (ci(c), 0)), v128, v128, v128],
                  out_shape=[_sds((T, D_INNER), F32), _sds((T, gn), F32), _sds((T, gn), F32), _sds((T, DT_PAD), BF16),
                             _sds((1, 128), F32), _sds((1, 128), F32), _sds((1, 128), F32)],
                  scratch_shapes=[pltpu.VMEM((SSM_HEADS, HEAD_DIM, D_STATE), F32)],
                  compiler_params=_cparams(("arbitrary",)))(xbc, xbc, xbc, projp, dtb, alog, dfull, states, dy, E, ET)


_WIN_ORDER = ("z", "ga", "gs", "xbc", "q", "k", "v", "dt")


def _win_to_padded(win_g):
    full = win_g.reshape(IN_DIM, D_MODEL)
    rows = []
    for nm in _WIN_ORDER:
        s, w = SEG[nm]
        rows.append(full[s:s + w])
    rows.append(jnp.zeros((DT_PAD - SEG["dt"][1], D_MODEL), win_g.dtype))
    return jnp.concatenate(rows, axis=0)


def _padded_to_win(dw):
    off = dict(z=OFF_Z, ga=OFF_GA, gs=OFF_GS, xbc=OFF_XBC, q=OFF_Q, k=OFF_K, v=OFF_V, dt=OFF_DT)
    per = IN_DIM // N_DEV
    blocks = []
    for j in range(N_DEV):
        lo, hi, rows = j * per, (j + 1) * per, []
        for nm in ("q", "k", "v", "z", "xbc", "dt", "ga", "gs"):
            s, w = SEG[nm]
            a, b = max(lo, s), min(hi, s + w)
            if a < b:
                rows.append(dw[off[nm] + a - s:off[nm] + b - s])
        blocks.append(jnp.concatenate(rows, axis=0))
    return jnp.stack(blocks)


def _pad128(v):
    return jnp.pad(v, ((0, 0), (0, 128 - v.shape[1])))


_SMALL = (("loss", 128, 1), ("g_mix", 2048, 2048), ("conv_b", 3072, 3072), ("dt_bias", 128, 32), ("a_log", 128, 32),
          ("d_skip", 128, 32), ("g_ssd", 2048, 2048), ("sinks", 128, 16), ("g_ffn", 2048, 2048), ("g_ple", 2048, 2048),
          ("g_final", 2048, 2048))


def _small_vec(d):
    parts = []
    for nm, pw, w in _SMALL:
        v = d[nm].reshape(1, -1).astype(F32)
        parts.append(jnp.pad(v[:, :min(v.shape[1], pw)], ((0, 0), (0, pw - min(v.shape[1], pw)))))
    return jnp.concatenate(parts, axis=1)


def _small_split(vec):
    out, o = {}, 0
    for nm, pw, w in _SMALL:
        out[nm] = vec[0, o:o + w]
        o += pw
    return out


def kernel(x, p, positions, g_mix, w_in, conv_w, conv_b, dt_bias, a_log, d_skip, g_ssd, sinks, w_attn_br, w_ssd_br, w_o, g_ffn, w_gate, w_up, w_down, g_ple, w_ple_gate, w_ple_proj, g_final, loss_target, m_g_mix, m_w_in, m_conv_w, m_conv_b, m_dt_bias, m_a_log, m_d_skip, m_g_ssd, m_sinks, m_w_attn_br, m_w_ssd_br, m_w_o, m_g_ffn, m_w_gate, m_w_up, m_w_down, m_g_ple, m_w_ple_gate, m_w_ple_proj, m_g_final, v_g_mix, v_w_in, v_conv_w, v_conv_b, v_dt_bias, v_a_log, v_d_skip, v_g_ssd, v_sinks, v_w_attn_br, v_w_ssd_br, v_w_o, v_g_ffn, v_w_gate, v_w_up, v_w_down, v_g_ple, v_w_ple_gate, v_w_ple_proj, v_g_final):
    T = x.shape[1]
    D = D_MODEL
    W = dict(g_mix=g_mix, w_in=w_in, conv_w=conv_w, conv_b=conv_b, dt_bias=dt_bias, a_log=a_log, d_skip=d_skip, g_ssd=g_ssd,
             sinks=sinks, w_attn_br=w_attn_br, w_ssd_br=w_ssd_br, w_o=w_o, g_ffn=g_ffn, w_gate=w_gate, w_up=w_up, w_down=w_down,
             g_ple=g_ple, w_ple_gate=w_ple_gate, w_ple_proj=w_ple_proj, g_final=g_final)
    Mo = dict(g_mix=m_g_mix, w_in=m_w_in, conv_w=m_conv_w, conv_b=m_conv_b, dt_bias=m_dt_bias, a_log=m_a_log, d_skip=m_d_skip,
              g_ssd=m_g_ssd, sinks=m_sinks, w_attn_br=m_w_attn_br, w_ssd_br=m_w_ssd_br, w_o=m_w_o, g_ffn=m_g_ffn, w_gate=m_w_gate,
              w_up=m_w_up, w_down=m_w_down, g_ple=m_g_ple, w_ple_gate=m_w_ple_gate, w_ple_proj=m_w_ple_proj, g_final=m_g_final)
    Vo = dict(g_mix=v_g_mix, w_in=v_w_in, conv_w=v_conv_w, conv_b=v_conv_b, dt_bias=v_dt_bias, a_log=v_a_log, d_skip=v_d_skip,
              g_ssd=v_g_ssd, sinks=v_sinks, w_attn_br=v_w_attn_br, w_ssd_br=v_w_ssd_br, w_o=v_w_o, g_ffn=v_g_ffn, w_gate=v_w_gate,
              w_up=v_w_up, w_down=v_w_down, g_ple=v_g_ple, w_ple_gate=v_w_ple_gate, w_ple_proj=v_w_ple_proj, g_final=v_g_final)
    order = ["g_mix", "w_in", "conv_w", "conv_b", "dt_bias", "a_log", "d_skip", "g_ssd", "sinks", "w_attn_br", "w_ssd_br", "w_o",
             "g_ffn", "w_gate", "w_up", "w_down", "g_ple", "w_ple_gate", "w_ple_proj", "g_final"]
    big = ["w_in", "conv_w", "w_attn_br", "w_ssd_br", "w_o", "w_gate", "w_up", "w_down", "w_ple_gate", "w_ple_proj"]

    x2 = x.reshape(T, D)
    p2 = p.reshape(T, PLE_DIM)
    tgt = loss_target.reshape(T, D)
    posf = positions.reshape(T, 1).astype(F32)
    inv = ROPE_THETA ** (-np.arange(HEAD_DIM // 2, dtype=np.float32) * 2.0 / HEAD_DIM)
    inv128 = jnp.asarray(np.tile(inv, 128 // (HEAD_DIM // 2)).reshape(1, 128).astype(np.float32))
    transposed = ("w_in",)

    def shard2d(a, n):
        a = a.reshape(a.shape[-2:])
        return a.T if n in transposed else a

    sh = {n: shard2d(W[n], n) for n in big}

    del _PENDING[:]
    me = 4 * lax.axis_index("x") + 2 * lax.axis_index("y") + lax.axis_index("c")
    groups = (("w_in",), ("conv_w", "w_attn_br", "w_ssd_br", "w_o"), ("w_gate", "w_up", "w_down"), ("w_ple_gate", "w_ple_proj"))
    send = {n: sh[n] if n == "conv_w" else sh[n].astype(BF16) for n in big}
    started, prev = [], None
    for gi, grp in enumerate(groups):
        zones = [lax.dynamic_update_index_in_dim(lax.empty((N_DEV,) + send[n].shape, send[n].dtype), send[n], me, 0) for n in grp]
        h = split_start("gather_start_%d" % gi, "gather", ICI_SAME_CORE, [], lands=zones, after=prev)
        prev = h["token"]
        started.append(h)
    gathered, fwd = {}, {}

    def forward_start(gi, after):
        _, lands = split_wait("gather_wait_%d" % gi, started[gi], after)
        fwd[gi] = split_start("forward_start_%d" % gi, "forward", FORWARD_BLOCKS, [], lands=lands)

    def forward_wait(gi, after):
        _, full = split_wait("forward_wait_%d" % gi, fwd[gi], after)
        gathered.update(zip(groups[gi], full))

    u = rms_fwd("norm_mix", x2, g_mix)
    forward_start(0, u)
    forward_wait(0, u)
    forward_start(1, u)
    winp = _win_to_padded(gathered["w_in"])
    dtb = _pad128(dt_bias)
    alog = _pad128(a_log)
    dfull = jnp.repeat(d_skip.reshape(SSM_HEADS), HEAD_DIM).reshape(1, D_INNER)

    projp = mm_nt("in_proj", u, winp, 640)
    attn, qr, kr = attn_fwd(projp, posf, inv128, sinks)
    forward_wait(1, attn)
    convw = jnp.transpose(gathered["conv_w"], (1, 0, 2)).reshape(CONV_WIDTH, CONV_DIM)
    wab = gathered["w_attn_br"]
    wsb = gathered["w_ssd_br"].reshape(D, D)
    wo = gathered["w_o"].reshape(D, D)
    xbc = conv_fwd(projp, convw, conv_b)
    y, states = ssd_fwd(xbc, projp, dtb, alog, dfull)
    forward_start(2, y)
    yn = gnorm_fwd(y, projp, g_ssd)
    out_a = mm_nn_colblk("attn_br", attn, wab)
    out_s = mm_nn("ssd_br", yn, wsb, 512)
    merged = merge_fwd(projp, out_a, out_s)
    h1 = mm_nn("o_proj", merged, wo, 512, residual=x2)
    f = rms_fwd("norm_ffn", h1, g_ffn)
    forward_wait(2, f)
    wg, wu, wd = gathered["w_gate"], gathered["w_up"], gathered["w_down"]
    gate, up, act = ffn_up(f, wg, wu)
    forward_start(3, act)
    h2 = ffn_down(act, wd, h1)
    r = rms_fwd("norm_ple", h2, g_ple)
    forward_wait(3, r)
    wpg = gathered["w_ple_gate"].reshape(D, D)
    wpp = gathered["w_ple_proj"]
    pg = mm_nn("ple_gate", r, wpg, 512)
    pp = mm_nn_colblk("ple_proj", p2, wpp)
    loss_v, dh3, dpg, dpp, dg_final = head_fwd_bwd(h2, pg, pp, g_final.reshape(1, D), tgt)

    gw = {}
    scat = []

    def scatter_start(names):
        scat.append((names, split_start("scatter_start_%d" % len(scat), "scatter", ALL_PEERS, [gw[n] for n in names])))

    gw["w_ple_proj"] = mm_tn_colblk("dw_ple_proj", p2, dpp, PLE_DIM)
    dr = mm_nt("d_ple_gate", dpg, wpg, 512)
    gw["w_ple_gate"] = mm_tn("dw_ple_gate", r, dpg, 512, 1024).reshape(N_DEV, D // N_DEV, D)
    scatter_start(("w_ple_proj", "w_ple_gate"))
    dh2, dh2b, dg_ple = rms_bwd("norm_ple_bwd", h2, g_ple, dr, dh3)
    dgate, dup = ffn_down_bwd(dh2b, wd, gate, up)
    gw["w_down"] = wgrad_rowblk_lhs("dw_down", act, dh2b, 1024)
    gw["w_gate"] = wgrad_colblk_rhs("dw_gate", f, dgate, 1024)
    gw["w_up"] = wgrad_colblk_rhs("dw_up", f, dup, 1024)
    scatter_start(("w_down", "w_gate", "w_up"))
    df = ffn_up_bwd(dgate, dup, wg, wu)
    dh1, dh1b, dg_ffn = rms_bwd("norm_ffn_bwd", h1, g_ffn, df, dh2)
    dmerged = mm_nt("d_o_proj", dh1b, wo, 512)
    gw["w_o"] = mm_tn("dw_o", merged, dh1b, 512, 1024).reshape(N_DEV, D // N_DEV, D)
    dout_a, dout_s, dga, dgs = merge_bwd(projp, out_a, out_s, dmerged)
    gw["w_ssd_br"] = mm_tn("dw_ssd_br", yn, dout_s, 512, 1024).reshape(N_DEV, D // N_DEV, D)
    gw["w_attn_br"] = mm_tn_colblk("dw_attn_br", attn, dout_a, D // N_DEV)
    scatter_start(("w_o", "w_ssd_br", "w_attn_br"))
    dyn = mm_nt("d_ssd_br", dout_s, wsb, 512)
    dattn = attn_br_bwd(dout_a, wab)
    dy, dz, dg_ssd = gnorm_bwd(y, projp, g_ssd, dyn)
    dxs, dbm, dcm, ddt, dal, ddsk, ddtb = ssd_bwd(xbc, projp, dtb, alog, dfull, states, dy)
    dx_x, dwc_x, dbc_x = conv_bwd("conv_bwd_x", projp, dxs, convw, conv_b, 0)
    dx_b, dwc_b, dbc_b = conv_bwd("conv_bwd_b", projp, dbm, convw, conv_b, D_INNER)
    dx_c, dwc_c, dbc_c = conv_bwd("conv_bwd_c", projp, dcm, convw, conv_b, D_INNER + SSM_GROUPS * D_STATE)
    dq, dk, dv, dsk = attn_bwd(qr, kr, projp, dattn, posf, inv128, sinks)
    dproj = jnp.concatenate([dz, dga, dgs, dx_x, dx_b, dx_c, dq, dk.astype(BF16), dv.astype(BF16), ddt], axis=1)
    gw["w_in"] = _padded_to_win(mm_tn("dw_in", dproj, u, 640, 1024))
    dconvw = jnp.concatenate([dwc_x, dwc_b, dwc_c], axis=1)
    gw["conv_w"] = jnp.transpose(dconvw.reshape(CONV_WIDTH, N_DEV, CONV_DIM // N_DEV), (1, 0, 2))
    scatter_start(("w_in", "conv_w"))
    du = mm_nn_red("d_in_proj", dproj, winp, 1024, 640)
    gx, _, dg_mix = rms_bwd("norm_mix_bwd", x2, g_mix, du, dh1)

    res = {}
    after = gx
    for si, (names, h) in enumerate(scat):
        srcs, lands = split_wait("scatter_wait_%d" % si, h, after)
        for n, mine, arrived in zip(names, srcs, lands):
            own = lax.dynamic_index_in_dim(mine, me, 0, keepdims=False)
            res[n] = adamw("adamw_" + n, arrived, sh[n], shard2d(Mo[n], n), shard2d(Vo[n], n), own=own)
        after = res[names[0]][0]

    small_g = dict(loss=loss_v[:, :1], g_mix=dg_mix, conv_b=jnp.concatenate([dbc_x, dbc_b, dbc_c], axis=1), dt_bias=ddtb,
                   a_log=dal, d_skip=ddsk, g_ssd=dg_ssd, sinks=dsk, g_ffn=dg_ffn, g_ple=dg_ple, g_final=dg_final)
    zero = jnp.zeros((1, 1), F32)
    vec_parts = exchange("gather_small", [_small_vec(small_g)], "gather")[0]
    sres = adamw("adamw_small", vec_parts, _small_vec({**W, "loss": zero}), _small_vec({**Mo, "loss": zero}),
                 _small_vec({**Vo, "loss": zero}))
    ssplit = [_small_split(a) for a in sres]
    loss = ssplit[0]["loss"].reshape(())
    for n in order:
        if n not in res:
            res[n] = tuple(s[n].reshape(W[n].shape) for s in ssplit)
        else:
            res[n] = tuple((a.T if n in transposed else a).reshape(W[n].shape) for a in res[n])
    outs = [loss, gx.reshape(x.shape)]
    for k in range(4):
        outs += [res[n][k] for n in order]
    return tuple(outs)
```

```python
import functools

import numpy as np
import jax
import jax.numpy as jnp
from jax import lax
from jax.experimental import pallas as pl
from jax.experimental.pallas import tpu as pltpu

F32 = jnp.float32
BF16 = jnp.bfloat16

N_DEV = 8
D_MODEL = 2048
HEAD_DIM = 64
ATTN_HEADS = 16
KV_HEADS = 4
Q_DIM = 1024
KV_DIM = 256
BLK = 128
D_INNER = 2048
SSM_HEADS = 32
SSM_GROUPS = 4
HEADS_PER_GROUP = 8
D_STATE = 128
CONV_WIDTH = 4
CONV_DIM = 3072
FFN_HIDDEN = 5632
PLE_DIM = 256
IN_DIM = 10784
NORM_EPS = 1e-6
SSM_NORM_EPS = 1e-5
ROPE_THETA = 10000.0

OFF_Z, OFF_GA, OFF_GS, OFF_XBC, OFF_Q, OFF_K, OFF_V, OFF_DT = 0, 2048, 4096, 6144, 9216, 10240, 10496, 10752
IN_PAD = 10880
DT_PAD = 128
SEG = dict(q=(0, 1024), k=(1024, 256), v=(1280, 256), z=(1536, 2048), xbc=(3584, 3072), dt=(6656, 32),
           ga=(6688, 2048), gs=(8736, 2048))

ADAM_LR, ADAM_B1, ADAM_B2, ADAM_EPS, ADAM_WD, ADAM_STEP = 0.001, 0.9, 0.999, 1e-08, 0.01, 10

VMEM_LIMIT = 56 * 1024 * 1024

NN = (((1,), (0,)), ((), ()))
NT = (((1,), (1,)), ((), ()))
TN = (((0,), (0,)), ((), ()))


_PENDING = []


def _raw_call(body, **kw):
    return pl.pallas_call(body, **kw)


def _pcall(body, **kw):
    deps = list(_PENDING)
    del _PENDING[:]
    if not deps:
        return _raw_call(body, **kw)
    n_in = len(kw["in_specs"])

    def tied(*refs):
        return body(*refs[:n_in], *refs[n_in + len(deps):])

    kw["in_specs"] = list(kw["in_specs"]) + [pl.BlockSpec(memory_space=pl.ANY)] * len(deps)
    call = _raw_call(tied, **kw)
    return lambda *ops: call(*ops, *deps)


def _cparams(sem=None):
    if sem is None:
        return pltpu.CompilerParams(vmem_limit_bytes=VMEM_LIMIT)
    return pltpu.CompilerParams(vmem_limit_bytes=VMEM_LIMIT, dimension_semantics=sem)


def _dot(a, b, dn):
    return lax.dot_general(a.astype(BF16), b.astype(BF16), dn, preferred_element_type=F32)


def _sigmoid(x):
    return 1.0 / (1.0 + jnp.exp(-x))


def _silu(x):
    return x * _sigmoid(x)


def _dsilu(x):
    s = _sigmoid(x)
    return s * (1.0 + x * (1.0 - s))


def _matmul(name, pairs, pair_specs, dn, grid, out_shapes, out_specs, nred=1, extra=(), extra_specs=(),
            epilogue=None, acc_shape=None):
    n_in = 2 * len(pairs) + len(extra)
    n_out = len(out_shapes)

    def body(*refs):
        ins = refs[:2 * len(pairs)]
        ex = refs[2 * len(pairs):n_in]
        outs = refs[n_in:n_in + n_out]

        def prod():
            s = None
            for p in range(len(pairs)):
                d = _dot(ins[2 * p][...], ins[2 * p + 1][...], dn)
                s = d if s is None else s + d
            return s

        def finish(val):
            if epilogue is None:
                outs[0][...] = val.astype(outs[0].dtype)
            else:
                res = epilogue(val, *[e[...] for e in ex])
                for o, r in zip(outs, res):
                    o[...] = r.astype(o.dtype)

        if nred == 1:
            finish(prod())
        else:
            acc = refs[n_in + n_out]
            k = pl.program_id(len(grid) - 1)

            @pl.when(k == 0)
            def _():
                acc[...] = jnp.zeros_like(acc)

            acc[...] += prod()

            @pl.when(k == nred - 1)
            def _():
                finish(acc[...])

    operands = []
    specs = []
    for (a, b), (sa, sb) in zip(pairs, pair_specs):
        operands += [a, b]
        specs += [sa, sb]
    operands += list(extra)
    specs += list(extra_specs)
    scratch = [pltpu.VMEM(acc_shape, F32)] if nred > 1 else []
    sem = ("arbitrary",) * len(grid)
    res = _pcall(body, name=name, grid=grid, in_specs=specs, out_specs=list(out_specs),
                 out_shape=list(out_shapes), scratch_shapes=scratch, compiler_params=_cparams(sem))(*operands)
    return res


def _sds(shape, dtype):
    return jax.ShapeDtypeStruct(shape, dtype)


def _row_tile(T):
    return min(1024, T)


def mm_nn(name, a, b, tn, out_dtype=F32, residual=None):
    M, K = a.shape
    N = b.shape[1]
    tm = _row_tile(M)
    grid = (M // tm, N // tn)
    extra, especs, epi = (), (), None
    if residual is not None:
        extra = (residual,)
        especs = (pl.BlockSpec((tm, tn), lambda i, n: (i, n)),)
        epi = lambda v, r: (v + r,)
    return _matmul(name, [(a, b)], [(pl.BlockSpec((tm, K), lambda i, n: (i, 0)), pl.BlockSpec((K, tn), lambda i, n: (0, n)))],
                   NN, grid, [_sds((M, N), out_dtype)], [pl.BlockSpec((tm, tn), lambda i, n: (i, n))],
                   extra=extra, extra_specs=especs, epilogue=epi)[0]


def mm_nn_colblk(name, a, b, out_dtype=F32):
    M, K = a.shape
    J, _, nb = b.shape
    tm = _row_tile(M)
    grid = (M // tm, J)
    return _matmul(name, [(a, b)], [(pl.BlockSpec((tm, K), lambda i, j: (i, 0)), pl.BlockSpec((None, K, nb), lambda i, j: (j, 0, 0)))],
                   NN, grid, [_sds((M, J * nb), out_dtype)], [pl.BlockSpec((tm, nb), lambda i, j: (i, j))])[0]


def mm_nt(name, a, w, tr, out_dtype=F32):
    M, C = a.shape
    R = w.shape[0]
    tm = _row_tile(M)
    grid = (M // tm, R // tr)
    return _matmul(name, [(a, w)], [(pl.BlockSpec((tm, C), lambda i, r: (i, 0)), pl.BlockSpec((tr, C), lambda i, r: (r, 0)))],
                   NT, grid, [_sds((M, R), out_dtype)], [pl.BlockSpec((tm, tr), lambda i, r: (i, r))])[0]


def mm_nt_red(name, a, w, tr, tk, out_dtype=F32):
    M, C = a.shape
    R = w.shape[0]
    tm = _row_tile(M)
    nk = C // tk
    grid = (M // tm, R // tr, nk)
    return _matmul(name, [(a, w)], [(pl.BlockSpec((tm, tk), lambda i, r, k: (i, k)), pl.BlockSpec((tr, tk), lambda i, r, k: (r, k)))],
                   NT, grid, [_sds((M, R), out_dtype)], [pl.BlockSpec((tm, tr), lambda i, r, k: (i, r))],
                   nred=nk, acc_shape=(tm, tr))[0]


def mm_nn_red(name, a, b, tn, tk, out_dtype=F32):
    M, K = a.shape
    N = b.shape[1]
    tm = _row_tile(M)
    nk = K // tk
    grid = (M // tm, N // tn, nk)
    return _matmul(name, [(a, b)], [(pl.BlockSpec((tm, tk), lambda i, n, k: (i, k)), pl.BlockSpec((tk, tn), lambda i, n, k: (k, n)))],
                   NN, grid, [_sds((M, N), out_dtype)], [pl.BlockSpec((tm, tn), lambda i, n, k: (i, n))],
                   nred=nk, acc_shape=(tm, tn))[0]


def mm_tn(name, x, dy, tr, tc, out_dtype=BF16):
    M, R = x.shape
    C = dy.shape[1]
    grid = (R // tr, C // tc)
    return _matmul(name, [(x, dy)], [(pl.BlockSpec((M, tr), lambda r, c: (0, r)), pl.BlockSpec((M, tc), lambda r, c: (0, c)))],
                   TN, grid, [_sds((R, C), out_dtype)], [pl.BlockSpec((tr, tc), lambda r, c: (r, c))])[0]


def mm_tn_colblk(name, x, dy, nb, out_dtype=BF16):
    M, R = x.shape
    J = dy.shape[1] // nb
    grid = (J,)
    return _matmul(name, [(x, dy)], [(pl.BlockSpec((M, R), lambda j: (0, 0)), pl.BlockSpec((M, nb), lambda j: (0, j)))],
                   TN, grid, [_sds((J, R, nb), out_dtype)], [pl.BlockSpec((None, R, nb), lambda j: (j, 0, 0))])[0]


def _rows(T):
    return min(256, T)


def rms_fwd(name, x, g, eps=NORM_EPS):
    T, D = x.shape
    tm = _rows(T)

    def body(x_ref, g_ref, o_ref):
        xv = x_ref[...]
        r = lax.rsqrt(jnp.mean(xv * xv, axis=-1, keepdims=True) + eps)
        o_ref[...] = (xv * r * g_ref[...]).astype(BF16)

    return _pcall(body, name=name, grid=(T // tm,),
                  in_specs=[pl.BlockSpec((tm, D), lambda i: (i, 0)), pl.BlockSpec((1, D), lambda i: (0, 0))],
                  out_specs=pl.BlockSpec((tm, D), lambda i: (i, 0)), out_shape=_sds((T, D), BF16),
                  compiler_params=_cparams(("arbitrary",)))(x, g)


def rms_bwd(name, x, g, dy, dres, eps=NORM_EPS):
    T, D = x.shape
    tm = _rows(T)

    def body(x_ref, g_ref, dy_ref, dr_ref, dx_ref, dxb_ref, dg_ref):
        i = pl.program_id(0)
        xv = x_ref[...]
        r = lax.rsqrt(jnp.mean(xv * xv, axis=-1, keepdims=True) + eps)
        xh = xv * r
        dyv = dy_ref[...]
        gd = dyv * g_ref[...]
        dx = r * (gd - xh * jnp.mean(gd * xh, axis=-1, keepdims=True)) + dr_ref[...]
        dx_ref[...] = dx
        dxb_ref[...] = dx.astype(BF16)

        @pl.when(i == 0)
        def _():
            dg_ref[...] = jnp.zeros_like(dg_ref)

        dg_ref[...] += jnp.sum(dyv * xh, axis=0, keepdims=True)

    row = pl.BlockSpec((tm, D), lambda i: (i, 0))
    vec = pl.BlockSpec((1, D), lambda i: (0, 0))
    return _pcall(body, name=name, grid=(T // tm,), in_specs=[row, vec, row, row], out_specs=[row, row, vec],
                  out_shape=[_sds((T, D), F32), _sds((T, D), BF16), _sds((1, D), F32)],
                  compiler_params=_cparams(("arbitrary",)))(x, g, dy, dres)


def gnorm_fwd(y, projp, g):
    T, D = y.shape
    tm = _rows(T)

    def body(y_ref, z_ref, g_ref, o_ref):
        yz = y_ref[...] * _silu(z_ref[...])
        r = lax.rsqrt(jnp.mean(yz * yz, axis=-1, keepdims=True) + SSM_NORM_EPS)
        o_ref[...] = (yz * r * g_ref[...]).astype(BF16)

    row = pl.BlockSpec((tm, D), lambda i: (i, 0))
    return _pcall(body, name="gnorm_fwd", grid=(T // tm,),
                  in_specs=[row, pl.BlockSpec((tm, D), lambda i: (i, OFF_Z // D)), pl.BlockSpec((1, D), lambda i: (0, 0))],
                  out_specs=row, out_shape=_sds((T, D), BF16), compiler_params=_cparams(("arbitrary",)))(y, projp, g)


def gnorm_bwd(y, projp, g, dyn):
    T, D = y.shape
    tm = _rows(T)

    def body(y_ref, z_ref, g_ref, dyn_ref, dy_ref, dz_ref, dg_ref):
        i = pl.program_id(0)
        yv, zv = y_ref[...], z_ref[...]
        sz = _silu(zv)
        yz = yv * sz
        r = lax.rsqrt(jnp.mean(yz * yz, axis=-1, keepdims=True) + SSM_NORM_EPS)
        xh = yz * r
        dv = dyn_ref[...]
        gd = dv * g_ref[...]
        dyz = r * (gd - xh * jnp.mean(gd * xh, axis=-1, keepdims=True))
        dy_ref[...] = dyz * sz
        dz_ref[...] = (dyz * yv * _dsilu(zv)).astype(BF16)

        @pl.when(i == 0)
        def _():
            dg_ref[...] = jnp.zeros_like(dg_ref)

        dg_ref[...] += jnp.sum(dv * xh, axis=0, keepdims=True)

    row = pl.BlockSpec((tm, D), lambda i: (i, 0))
    vec = pl.BlockSpec((1, D), lambda i: (0, 0))
    return _pcall(body, name="gnorm_bwd", grid=(T // tm,),
                  in_specs=[row, pl.BlockSpec((tm, D), lambda i: (i, OFF_Z // D)), vec, row], out_specs=[row, row, vec],
                  out_shape=[_sds((T, D), F32), _sds((T, D), BF16), _sds((1, D), F32)],
                  compiler_params=_cparams(("arbitrary",)))(y, projp, g, dyn)


def merge_fwd(projp, out_a, out_s):
    T, D = out_a.shape
    tm = _rows(T)

    def body(ga_ref, gs_ref, a_ref, s_ref, o_ref):
        o_ref[...] = (_sigmoid(ga_ref[...]) * a_ref[...] + _sigmoid(gs_ref[...]) * s_ref[...]).astype(BF16)

    row = pl.BlockSpec((tm, D), lambda i: (i, 0))
    return _pcall(body, name="merge_fwd", grid=(T // tm,),
                  in_specs=[pl.BlockSpec((tm, D), lambda i: (i, OFF_GA // D)), pl.BlockSpec((tm, D), lambda i: (i, OFF_GS // D)), row, row],
                  out_specs=row, out_shape=_sds((T, D), BF16), compiler_params=_cparams(("arbitrary",)))(projp, projp, out_a, out_s)


def merge_bwd(projp, out_a, out_s, dmerged):
    T, D = out_a.shape
    tm = _rows(T)

    def body(ga_ref, gs_ref, a_ref, s_ref, dm_ref, da_ref, ds_ref, dga_ref, dgs_ref):
        dm = dm_ref[...]
        sa, ss = _sigmoid(ga_ref[...]), _sigmoid(gs_ref[...])
        da_ref[...] = (dm * sa).astype(BF16)
        ds_ref[...] = (dm * ss).astype(BF16)
        dga_ref[...] = (dm * a_ref[...] * sa * (1.0 - sa)).astype(BF16)
        dgs_ref[...] = (dm * s_ref[...] * ss * (1.0 - ss)).astype(BF16)

    row = pl.BlockSpec((tm, D), lambda i: (i, 0))
    return _pcall(body, name="merge_bwd", grid=(T // tm,),
                  in_specs=[pl.BlockSpec((tm, D), lambda i: (i, OFF_GA // D)), pl.BlockSpec((tm, D), lambda i: (i, OFF_GS // D)), row, row, row],
                  out_specs=[row] * 4, out_shape=[_sds((T, D), BF16)] * 4,
                  compiler_params=_cparams(("arbitrary",)))(projp, projp, out_a, out_s, dmerged)


def head_fwd_bwd(h2, pg, pp, g_final, target):
    T, D = h2.shape
    tm = _rows(T)

    def body(h_ref, pg_ref, pp_ref, g_ref, t_ref, loss_ref, dh_ref, dpg_ref, dpp_ref, dg_ref):
        i = pl.program_id(0)
        s = _sigmoid(pg_ref[...])
        ppv = pp_ref[...]
        h3 = h_ref[...] + s * ppv
        r = lax.rsqrt(jnp.mean(h3 * h3, axis=-1, keepdims=True) + NORM_EPS)
        xh = h3 * r
        gv = g_ref[...]
        e = xh * gv - t_ref[...]
        dyo = e * (1.0 / D)
        gd = dyo * gv
        dh = r * (gd - xh * jnp.mean(gd * xh, axis=-1, keepdims=True))
        dh_ref[...] = dh
        dpg_ref[...] = (dh * ppv * s * (1.0 - s)).astype(BF16)
        dpp_ref[...] = (dh * s).astype(BF16)

        @pl.when(i == 0)
        def _():
            dg_ref[...] = jnp.zeros_like(dg_ref)
            loss_ref[...] = jnp.zeros_like(loss_ref)

        dg_ref[...] += jnp.sum(dyo * xh, axis=0, keepdims=True)
        part = 0.5 * jnp.sum(jnp.mean(e * e, axis=-1, keepdims=True), axis=0, keepdims=True)
        loss_ref[...] += jnp.broadcast_to(part, loss_ref.shape)

    row = pl.BlockSpec((tm, D), lambda i: (i, 0))
    vec = pl.BlockSpec((1, D), lambda i: (0, 0))
    return _pcall(body, name="head_fwd_bwd", grid=(T // tm,), in_specs=[row, row, row, vec, row],
                  out_specs=[pl.BlockSpec((1, 128), lambda i: (0, 0)), row, row, row, vec],
                  out_shape=[_sds((1, 128), F32), _sds((T, D), F32), _sds((T, D), BF16), _sds((T, D), BF16), _sds((1, D), F32)],
                  compiler_params=_cparams(("arbitrary",)))(h2, pg, pp, g_final, target)


def ffn_up(f, wg, wu):
    T, D = f.shape
    J, _, nb = wg.shape
    tm = _row_tile(T)

    def body(f_ref, wg_ref, wu_ref, g_ref, u_ref, a_ref):
        fv = f_ref[...]
        g = _dot(fv, wg_ref[...], NN)
        u = _dot(fv, wu_ref[...], NN)
        g_ref[...] = g
        u_ref[...] = u
        a_ref[...] = (_silu(g) * u).astype(BF16)

    wspec = pl.BlockSpec((None, D, nb), lambda i, j: (j, 0, 0))
    ospec = pl.BlockSpec((None, tm, nb), lambda i, j: (j, i, 0))
    return _pcall(body, name="ffn_up", grid=(T // tm, J), in_specs=[pl.BlockSpec((tm, D), lambda i, j: (i, 0)), wspec, wspec],
                  out_specs=[ospec] * 3, out_shape=[_sds((J, T, nb), F32), _sds((J, T, nb), F32), _sds((J, T, nb), BF16)],
                  compiler_params=_cparams(("arbitrary", "arbitrary")))(f, wg, wu)


def ffn_down(act, wd, h1):
    J, T, nb = act.shape
    D = wd.shape[2]
    tm = _row_tile(T)
    tn = 1024
    grid = (T // tm, D // tn, J)
    return _matmul("ffn_down", [(act, wd)],
                   [(pl.BlockSpec((None, tm, nb), lambda i, n, j: (j, i, 0)), pl.BlockSpec((None, nb, tn), lambda i, n, j: (j, 0, n)))],
                   NN, grid, [_sds((T, D), F32)], [pl.BlockSpec((tm, tn), lambda i, n, j: (i, n))], nred=J, acc_shape=(tm, tn),
                   extra=(h1,), extra_specs=(pl.BlockSpec((tm, tn), lambda i, n, j: (i, n)),), epilogue=lambda v, r: (v + r,))[0]


def ffn_down_bwd(dh2b, wd, gate, up):
    T, D = dh2b.shape
    J, nb, _ = wd.shape
    tm = _row_tile(T)
    ospec = pl.BlockSpec((None, tm, nb), lambda i, j: (j, i, 0))

    def epi(da, g, u):
        return (da * u * _dsilu(g), da * _silu(g))

    return _matmul("ffn_down_bwd", [(dh2b, wd)],
                   [(pl.BlockSpec((tm, D), lambda i, j: (i, 0)), pl.BlockSpec((None, nb, D), lambda i, j: (j, 0, 0)))],
                   NT, (T // tm, J), [_sds((J, T, nb), BF16)] * 2, [ospec, ospec],
                   extra=(gate, up), extra_specs=(ospec, ospec), epilogue=epi)


def ffn_up_bwd(dgate, dup, wg, wu):
    J, T, nb = dgate.shape
    D = wg.shape[1]
    tm = _row_tile(T)
    tr = 1024
    aspec = pl.BlockSpec((None, tm, nb), lambda i, r, j: (j, i, 0))
    wspec = pl.BlockSpec((None, tr, nb), lambda i, r, j: (j, r, 0))
    return _matmul("ffn_up_bwd", [(dgate, wg), (dup, wu)], [(aspec, wspec), (aspec, wspec)], NT, (T // tm, D // tr, J),
                   [_sds((T, D), F32)], [pl.BlockSpec((tm, tr), lambda i, r, j: (i, r))], nred=J, acc_shape=(tm, tr))[0]


def wgrad_rowblk_lhs(name, xb, dy, tc):
    J, T, nb = xb.shape
    C = dy.shape[1]
    return _matmul(name, [(xb, dy)],
                   [(pl.BlockSpec((None, T, nb), lambda j, c: (j, 0, 0)), pl.BlockSpec((T, tc), lambda j, c: (0, c)))],
                   TN, (J, C // tc), [_sds((J, nb, C), BF16)], [pl.BlockSpec((None, nb, tc), lambda j, c: (j, 0, c))])[0]


def wgrad_colblk_rhs(name, x, dyb, tr):
    T, R = x.shape
    J, _, nb = dyb.shape
    return _matmul(name, [(x, dyb)],
                   [(pl.BlockSpec((T, tr), lambda j, r: (0, r)), pl.BlockSpec((None, T, nb), lambda j, r: (j, 0, 0)))],
                   TN, (J, R // tr), [_sds((J, R, nb), BF16)], [pl.BlockSpec((None, tr, nb), lambda j, r: (j, r, 0))])[0]


def attn_br_bwd(dout_a, wab):
    T, D = dout_a.shape
    J, R, nb = wab.shape
    tm = _row_tile(T)
    return _matmul("attn_br_bwd", [(dout_a, wab)],
                   [(pl.BlockSpec((tm, nb), lambda i, j: (i, j)), pl.BlockSpec((None, R, nb), lambda i, j: (j, 0, 0)))],
                   NT, (T // tm, J), [_sds((T, R), BF16)], [pl.BlockSpec((tm, R), lambda i, j: (i, 0))], nred=J, acc_shape=(tm, R))[0]


def _adam_math(w, g, m, v):
    m2 = ADAM_B1 * m + (1.0 - ADAM_B1) * g
    v2 = ADAM_B2 * v + (1.0 - ADAM_B2) * (g * g)
    m_hat = m2 / (1.0 - ADAM_B1 ** ADAM_STEP)
    v_hat = v2 / (1.0 - ADAM_B2 ** ADAM_STEP)
    delta = -ADAM_LR * (m_hat / (jnp.sqrt(v_hat) + ADAM_EPS) + ADAM_WD * w)
    return delta, m2, v2


def _sum_partials(own, parts):
    g = None if own is None else own.astype(F32)
    if parts is not None:
        for s in range(parts.shape[0]):
            t = parts[s].astype(F32)
            g = t if g is None else g + t
    return g


def adamw(name, parts, w, m, v, own=None):
    R, C = w.shape
    tr, tc = R, C
    for cand in (256, 176, 128, 64, 32, 16, 8):
        if R % cand == 0 and R > cand:
            tr = cand
            break
    if tr == R and R > 256:
        tc = 256
    given = [a for a in (parts, own) if a is not None]

    def body(*refs):
        p_ref = refs[0] if parts is not None else None
        o_ref = refs[len(given) - 1] if own is not None else None
        w_ref, m_ref, v_ref, g_ref, d_ref, m2_ref, v2_ref = refs[-7:]
        g = _sum_partials(None if o_ref is None else o_ref[...], p_ref)
        d, m2, v2 = _adam_math(w_ref[...], g, m_ref[...], v_ref[...])
        g_ref[...] = g
        d_ref[...] = d
        m2_ref[...] = m2
        v2_ref[...] = v2

    blk = pl.BlockSpec((tr, tc), lambda i, j: (i, j))
    specs = ([] if parts is None else [pl.BlockSpec((parts.shape[0], tr, tc), lambda i, j: (0, i, j))]) + [blk] * (3 + (own is not None))
    return _pcall(body, name=name, grid=(R // tr, C // tc), in_specs=specs,
                  out_specs=[blk] * 4, out_shape=[_sds((R, C), F32)] * 4,
                  compiler_params=_cparams(("arbitrary", "arbitrary")))(*given, w, m, v)


def exchange(name, arrays, mode):
    n = len(arrays)
    out_shapes = []
    for a in arrays:
        shp = a.shape if mode == "scatter" else (N_DEV,) + a.shape
        out_shapes.append(_sds(shp, a.dtype))

    def body(*refs):
        ins, outs = refs[:n], refs[n:2 * n]
        send_sems, recv_sems, local_sems = refs[2 * n:]
        x, y, c = lax.axis_index("x"), lax.axis_index("y"), lax.axis_index("c")
        me = 4 * x + 2 * y + c

        def src(a, dest):
            return ins[a].at[dest] if mode == "scatter" else ins[a]

        local = [pltpu.make_async_copy(src(a, me), outs[a].at[me], local_sems.at[a]) for a in range(n)]
        for cp in local:
            cp.start()
        remote = []
        for k in range(1, N_DEV):
            px = 1 - x if k & 4 else x
            py = 1 - y if k & 2 else y
            pc = 1 - c if k & 1 else c
            peer = 4 * px + 2 * py + pc
            for a in range(n):
                cp = pltpu.make_async_remote_copy(src_ref=src(a, peer), dst_ref=outs[a].at[me],
                                                  send_sem=send_sems.at[a * 7 + k - 1], recv_sem=recv_sems.at[a * 7 + k - 1],
                                                  device_id=(px, py, pc), device_id_type=pl.DeviceIdType.MESH)
                cp.start()
                arrival = pltpu.make_async_remote_copy(src_ref=src(a, peer), dst_ref=outs[a].at[peer],
                                                       send_sem=send_sems.at[a * 7 + k - 1], recv_sem=recv_sems.at[a * 7 + k - 1],
                                                       device_id=(px, py, pc), device_id_type=pl.DeviceIdType.MESH)
                remote.append((cp, arrival))
        for cp, arrival in remote:
            cp.wait_send()
            arrival.wait_recv()
        for cp in local:
            cp.wait()

    any_spec = pl.BlockSpec(memory_space=pl.ANY)
    return _pcall(body, name=name, in_specs=[any_spec] * n, out_specs=[any_spec] * n, out_shape=out_shapes,
                  scratch_shapes=[pltpu.SemaphoreType.DMA((7 * n,)), pltpu.SemaphoreType.DMA((7 * n,)), pltpu.SemaphoreType.DMA((n,))],
                  compiler_params=pltpu.CompilerParams(has_side_effects=True))(*arrays)


_HBM = pl.BlockSpec(memory_space=pltpu.HBM)
_SEM = pl.BlockSpec(memory_space=pltpu.SEMAPHORE)
_ANY = pl.BlockSpec(memory_space=pl.ANY)
_SPLIT_PARAMS = dict(compiler_params=pltpu.CompilerParams(has_side_effects=pltpu.SideEffectType.DATAFLOW_SIDE_EFFECTING))
ICI_SAME_CORE = (2, 4, 6)
ALL_PEERS = (1, 2, 3, 4, 5, 6, 7)
LAND_SLOTS = {"gather": N_DEV, "scatter": N_DEV - 1, "pair": 4, "scatter_core": 3}


def _mesh_pos():
    x, y, c = lax.axis_index("x"), lax.axis_index("y"), lax.axis_index("c")
    return x, y, c, 4 * x + 2 * y + c


def _peer_of(k, x, y, c):
    px = 1 - x if k & 4 else x
    py = 1 - y if k & 2 else y
    pc = 1 - c if k & 1 else c
    return (px, py, pc), 4 * px + 2 * py + pc


def _split_copies(mode, ks, srcs, lands, send_sems, recv_sems):
    x, y, c, me = _mesh_pos()
    pairs = []
    for a in range(len(lands)):
        for j, k in enumerate(ks):
            dev, peer = _peer_of(k, x, y, c)
            i = a * len(ks) + j
            if mode == "gather":
                s_out, d_out, d_in = lands[a].at[me], lands[a].at[me], lands[a].at[peer]
            elif mode == "scatter":
                s_out, d_out, d_in = srcs[a].at[peer], lands[a].at[k - 1], lands[a].at[k - 1]
            elif mode == "pair":
                dev, _ = _peer_of(1, x, y, c)
                _, theirs = _peer_of(k | 1, x, y, c)
                s_out, d_out, d_in = srcs[a].at[theirs], lands[a].at[j], lands[a].at[j]
            elif mode == "scatter_core":
                s_out, d_out, d_in = srcs[a].at[j + 1], lands[a].at[j], lands[a].at[j]
            else:
                dev, _ = _peer_of(1, x, y, c)
                _, theirs = _peer_of(k | 1, x, y, c)
                s_out, d_out, d_in = lands[a].at[peer], lands[a].at[peer], lands[a].at[theirs]
            both = [pltpu.make_async_remote_copy(src_ref=s_out, dst_ref=d, send_sem=send_sems.at[i], recv_sem=recv_sems.at[i],
                                                 device_id=dev, device_id_type=pl.DeviceIdType.MESH) for d in (d_out, d_in)]
            pairs.append(tuple(both))
    return pairs


def split_start(name, mode, ks, srcs, lands=None, after=None):
    n, nk = len(srcs) if lands is None else len(lands), len(ks)
    srcs = [pltpu.with_memory_space_constraint(s, pltpu.HBM) for s in srcs]
    if lands is None:
        shapes = [((N_DEV,) + s.shape) if mode == "gather" else ((LAND_SLOTS[mode],) + s.shape[1:]) for s in srcs]
        lands = [lax.empty(shp, s.dtype) for shp, s in zip(shapes, srcs)]
    lands = [pltpu.with_memory_space_constraint(l, pltpu.HBM) for l in lands]
    both = srcs + lands
    extra = [] if after is None else [after]

    def body(*refs):
        src_refs, land_refs = refs[:len(srcs)], refs[len(srcs):len(both)]
        send_sems, recv_sems = refs[len(both) + len(extra)], refs[len(both) + len(extra) + 1]
        token = refs[-1]
        for out, _ in _split_copies(mode, ks, src_refs, land_refs, send_sems, recv_sems):
            out.start()
        token[...] = jnp.zeros_like(token)

    out_shape = (pltpu.SemaphoreType.DMA((n * nk,)), pltpu.SemaphoreType.DMA((n * nk,)),
                 *[pltpu.HBM(a.shape, a.dtype) for a in both], _sds((8, 128), F32))
    res = _raw_call(body, name=name, out_shape=out_shape, in_specs=[_HBM] * len(both) + [_ANY] * len(extra),
                    out_specs=(_SEM, _SEM, *[_HBM] * len(both), pl.BlockSpec(memory_space=pltpu.VMEM)),
                    input_output_aliases={i: 2 + i for i in range(len(both))}, **_SPLIT_PARAMS)(*both, *extra)
    _PENDING.append(res[-1])
    return dict(mode=mode, ks=ks, sems=(res[0], res[1]), srcs=list(res[2:2 + len(srcs)]),
                lands=list(res[2 + len(srcs):2 + len(both)]), token=res[-1])


def split_wait(name, h, after):
    ns = len(h["srcs"])
    both = h["srcs"] + h["lands"]

    def body(*refs):
        src_refs, land_refs = refs[:ns], refs[ns:len(both)]
        send_sems, recv_sems = refs[len(both)], refs[len(both) + 1]
        for out, arriving in _split_copies(h["mode"], h["ks"], src_refs, land_refs, send_sems, recv_sems):
            out.wait_send()
            arriving.wait_recv()

    res = _raw_call(body, name=name, out_shape=tuple(pltpu.HBM(a.shape, a.dtype) for a in both),
                    in_specs=[_HBM] * len(both) + [_SEM, _SEM, _ANY], out_specs=tuple([_HBM] * len(both)),
                    input_output_aliases={i: i for i in range(len(both))}, **_SPLIT_PARAMS)(*both, *h["sems"], after)
    return list(res[:ns]), list(res[ns:])


FORWARD_BLOCKS = (0, 2, 4, 6)


def pair_sum(name, mine, theirs):
    P, R, C = mine.shape
    tc = 256

    def body(a_ref, b_ref, o_ref):
        o_ref[...] = (a_ref[...].astype(F32) + b_ref[...].astype(F32)).astype(o_ref.dtype)

    blk = pl.BlockSpec((P, R, tc), lambda i: (0, 0, i))
    return _pcall(body, name=name, grid=(C // tc,), in_specs=[blk, blk], out_specs=blk, out_shape=_sds((P, R, C), mine.dtype),
                  compiler_params=_cparams(("arbitrary",)))(mine, theirs)


def _rope_parts(pos_ref, inv_ref):
    ang = pos_ref[...] * inv_ref[...]
    return jnp.cos(ang), jnp.sin(ang)


def _rot_half(t):
    lane = lax.broadcasted_iota(jnp.int32, t.shape, 1)
    return jnp.where((lane % HEAD_DIM) < HEAD_DIM // 2, -pltpu.roll(t, 128 - HEAD_DIM // 2, 1), pltpu.roll(t, HEAD_DIM // 2, 1))


def _attn_mask(n):
    row = lax.broadcasted_iota(jnp.int32, (BLK, 2 * BLK), 0)
    col = lax.broadcasted_iota(jnp.int32, (BLK, 2 * BLK), 1)
    dist = row + BLK - col
    return (dist >= 0) & (dist < BLK) & ((n * BLK - BLK + col) >= 0)


def _attn_specs(T):
    prev = lambda n: jnp.maximum(n - 1, 0)
    kc = pl.BlockSpec((BLK, KV_DIM), lambda n: (n, OFF_K // KV_DIM))
    kp = pl.BlockSpec((BLK, KV_DIM), lambda n: (prev(n), OFF_K // KV_DIM))
    vc = pl.BlockSpec((BLK, KV_DIM), lambda n: (n, OFF_V // KV_DIM))
    vp = pl.BlockSpec((BLK, KV_DIM), lambda n: (prev(n), OFF_V // KV_DIM))
    pc = pl.BlockSpec((BLK, 1), lambda n: (n, 0))
    pp = pl.BlockSpec((BLK, 1), lambda n: (prev(n), 0))
    inv = pl.BlockSpec((1, 128), lambda n: (0, 0))
    sink = pl.BlockSpec(memory_space=pltpu.SMEM)
    return kc, kp, vc, vp, pc, pp, inv, sink


def _softmax_sink(sc, valid, sink):
    sc = jnp.where(valid, sc * (HEAD_DIM ** -0.5), -1e30)
    m = jnp.maximum(jnp.max(sc, axis=1, keepdims=True), sink)
    e = jnp.exp(sc - m)
    es = jnp.exp(sink - m)
    den = jnp.sum(e, axis=1, keepdims=True) + es
    return e / den, es / den


def attn_fwd(projp, posf, inv128, sinks):
    T = projp.shape[0]
    kc, kp, vc, vp, pc, pp, inv, sink = _attn_specs(T)

    def body(q_ref, kc_ref, kp_ref, vc_ref, vp_ref, pc_ref, pp_ref, inv_ref, sink_ref, o_ref, qr_ref, kr_ref):
        n = pl.program_id(0)
        cos_c, sin_c = _rope_parts(pc_ref, inv_ref)
        cos_p, sin_p = _rope_parts(pp_ref, inv_ref)
        valid = _attn_mask(n)
        k_c, k_p = [], []
        for s in range(KV_DIM // 128):
            t = kc_ref[:, 128 * s:128 * (s + 1)]
            k_c.append((t * cos_c + _rot_half(t) * sin_c).astype(BF16))
            kr_ref[:, 128 * s:128 * (s + 1)] = k_c[s]
            t = kp_ref[:, 128 * s:128 * (s + 1)]
            k_p.append((t * cos_p + _rot_half(t) * sin_p).astype(BF16))
        kcat, vcat = [], []
        for hk in range(KV_HEADS):
            lo = HEAD_DIM * (hk % 2)
            kcat.append(jnp.concatenate([k_p[hk // 2][:, lo:lo + HEAD_DIM], k_c[hk // 2][:, lo:lo + HEAD_DIM]], axis=0))
            vcat.append(jnp.concatenate([vp_ref[:, HEAD_DIM * hk:HEAD_DIM * (hk + 1)], vc_ref[:, HEAD_DIM * hk:HEAD_DIM * (hk + 1)]], axis=0)
                        .astype(BF16))
        q_heads = []
        for s in range(Q_DIM // 128):
            t = q_ref[:, 128 * s:128 * (s + 1)]
            qs = (t * cos_c + _rot_half(t) * sin_c).astype(BF16)
            qr_ref[:, 128 * s:128 * (s + 1)] = qs
            q_heads += [qs[:, :HEAD_DIM], qs[:, HEAD_DIM:]]
        G = ATTN_HEADS // KV_HEADS
        scores = [_dot(q_heads[hq], kcat[hq // G], NT) for hq in range(ATTN_HEADS)]
        probs = [_softmax_sink(scores[hq], valid, sink_ref[0, hq])[0] for hq in range(ATTN_HEADS)]
        outs = [_dot(probs[hq], vcat[hq // G], NN) for hq in range(ATTN_HEADS)]
        for s in range(Q_DIM // 128):
            o_ref[:, 128 * s:128 * (s + 1)] = jnp.concatenate([outs[2 * s], outs[2 * s + 1]], axis=1).astype(BF16)

    qspec = pl.BlockSpec((BLK, Q_DIM), lambda n: (n, OFF_Q // Q_DIM))
    orow = pl.BlockSpec((BLK, Q_DIM), lambda n: (n, 0))
    krow = pl.BlockSpec((BLK, KV_DIM), lambda n: (n, 0))
    return _pcall(body, name="attn_fwd", grid=(T // BLK,), in_specs=[qspec, kc, kp, vc, vp, pc, pp, inv, sink],
                  out_specs=[orow, orow, krow], out_shape=[_sds((T, Q_DIM), BF16), _sds((T, Q_DIM), BF16), _sds((T, KV_DIM), BF16)],
                  compiler_params=_cparams(("arbitrary",)))(projp, projp, projp, projp, projp, posf, posf, inv128, sinks)


def attn_bwd(qr, kr, projp, dattn, posf, inv128, sinks):
    T = projp.shape[0]
    _, _, vc, vp, pc, pp, inv, sink = _attn_specs(T)
    G = ATTN_HEADS // KV_HEADS

    def body(qr_ref, krc_ref, krp_ref, vc_ref, vp_ref, do_ref, pc_ref, pp_ref, inv_ref, sink_ref, dq_ref, dk_ref, dv_ref, dsk_ref):
        n = pl.program_id(0)

        @pl.when(n == 0)
        def _():
            dk_ref[...] = jnp.zeros_like(dk_ref)
            dv_ref[...] = jnp.zeros_like(dv_ref)
            dsk_ref[...] = jnp.zeros_like(dsk_ref)

        cos_c, sin_c = _rope_parts(pc_ref, inv_ref)
        cos_p, sin_p = _rope_parts(pp_ref, inv_ref)
        valid = _attn_mask(n)
        lane = lax.broadcasted_iota(jnp.int32, (1, 128), 1)
        kcat, vcat = [], []
        for hk in range(KV_HEADS):
            ksl = slice(HEAD_DIM * hk, HEAD_DIM * (hk + 1))
            kcat.append(jnp.concatenate([krp_ref[:, ksl], krc_ref[:, ksl]], axis=0))
            vcat.append(jnp.concatenate([vp_ref[:, ksl], vc_ref[:, ksl]], axis=0).astype(BF16))
        H = range(ATTN_HEADS)
        q_heads = [qr_ref[:, HEAD_DIM * hq:HEAD_DIM * (hq + 1)] for hq in H]
        do_heads = [do_ref[:, HEAD_DIM * hq:HEAD_DIM * (hq + 1)] for hq in H]
        soft = [_softmax_sink(_dot(q_heads[hq], kcat[hq // G], NT), valid, sink_ref[0, hq]) for hq in H]
        dps = [_dot(do_heads[hq], vcat[hq // G], NT) for hq in H]
        deltas = [jnp.sum(soft[hq][0] * dps[hq], axis=1, keepdims=True) for hq in H]
        dss = [(soft[hq][0] * (dps[hq] - deltas[hq]) * (HEAD_DIM ** -0.5)).astype(BF16) for hq in H]
        pbs = [soft[hq][0].astype(BF16) for hq in H]
        dsk = jnp.zeros((1, 128), F32)
        for hq in H:
            dsk = dsk + jnp.where(lane == hq, -jnp.sum(soft[hq][1] * deltas[hq], axis=0, keepdims=True), 0.0)
        dsk_ref[...] += dsk
        dq_heads = [_dot(dss[hq], kcat[hq // G], NN) for hq in H]
        dk_parts = [_dot(dss[hq], q_heads[hq], TN) for hq in H]
        dv_parts = [_dot(pbs[hq], do_heads[hq], TN) for hq in H]
        dk_heads = [sum(dk_parts[G * hk + 1:G * (hk + 1)], dk_parts[G * hk]) for hk in range(KV_HEADS)]
        dv_heads = [sum(dv_parts[G * hk + 1:G * (hk + 1)], dv_parts[G * hk]) for hk in range(KV_HEADS)]
        for s in range(Q_DIM // 128):
            t = jnp.concatenate([dq_heads[2 * s], dq_heads[2 * s + 1]], axis=1)
            dq_ref[:, 128 * s:128 * (s + 1)] = (t * cos_c - _rot_half(t) * sin_c).astype(BF16)
        cur = pl.ds(pl.multiple_of(n * BLK, BLK), BLK)
        prv = pl.ds(pl.multiple_of(jnp.maximum(n - 1, 0) * BLK, BLK), BLK)
        for s in range(KV_DIM // 128):
            tc = jnp.concatenate([dk_heads[2 * s][BLK:], dk_heads[2 * s + 1][BLK:]], axis=1)
            tp = jnp.concatenate([dk_heads[2 * s][:BLK], dk_heads[2 * s + 1][:BLK]], axis=1)
            cols = slice(128 * s, 128 * (s + 1))
            dk_ref[cur, cols] += tc * cos_c - _rot_half(tc) * sin_c
            dk_ref[prv, cols] += tp * cos_p - _rot_half(tp) * sin_p
            dv_ref[cur, cols] += jnp.concatenate([dv_heads[2 * s][BLK:], dv_heads[2 * s + 1][BLK:]], axis=1)
            dv_ref[prv, cols] += jnp.concatenate([dv_heads[2 * s][:BLK], dv_heads[2 * s + 1][:BLK]], axis=1)

    qrow = pl.BlockSpec((BLK, Q_DIM), lambda n: (n, 0))
    krc = pl.BlockSpec((BLK, KV_DIM), lambda n: (n, 0))
    krp = pl.BlockSpec((BLK, KV_DIM), lambda n: (jnp.maximum(n - 1, 0), 0))
    whole = pl.BlockSpec((T, KV_DIM), lambda n: (0, 0))
    return _pcall(body, name="attn_bwd", grid=(T // BLK,), in_specs=[qrow, krc, krp, vc, vp, qrow, pc, pp, inv, sink],
                  out_specs=[qrow, whole, whole, pl.BlockSpec((1, 128), lambda n: (0, 0))],
                  out_shape=[_sds((T, Q_DIM), BF16), _sds((T, KV_DIM), F32), _sds((T, KV_DIM), F32), _sds((1, 128), F32)],
                  compiler_params=_cparams(("arbitrary",)))(qr, kr, kr, projp, projp, dattn, posf, posf, inv128, sinks)


CONV_CB = 256


def _shift_down(x, s):
    row = lax.broadcasted_iota(jnp.int32, x.shape, 0)
    return jnp.where(row >= s, pltpu.roll(x, s, 0), 0.0)


def _shift_up(x, s):
    T = x.shape[0]
    row = lax.broadcasted_iota(jnp.int32, x.shape, 0)
    return jnp.where(row < T - s, pltpu.roll(x, T - s, 0), 0.0)


def _conv_pre(x, w_ref, b_ref):
    acc = x * w_ref[CONV_WIDTH - 1:CONV_WIDTH, :] + b_ref[...]
    for s in range(1, CONV_WIDTH):
        acc = acc + _shift_down(x, s) * w_ref[CONV_WIDTH - 1 - s:CONV_WIDTH - s, :]
    return acc


def conv_fwd(projp, conv_w, conv_b):
    T = projp.shape[0]

    def body(x_ref, w_ref, b_ref, o_ref):
        o_ref[...] = _silu(_conv_pre(x_ref[...], w_ref, b_ref))

    return _pcall(body, name="conv_fwd", grid=(CONV_DIM // CONV_CB,),
                  in_specs=[pl.BlockSpec((T, CONV_CB), lambda c: (0, OFF_XBC // CONV_CB + c)),
                            pl.BlockSpec((CONV_WIDTH, CONV_CB), lambda c: (0, c)), pl.BlockSpec((1, CONV_CB), lambda c: (0, c))],
                  out_specs=pl.BlockSpec((T, CONV_CB), lambda c: (0, c)), out_shape=_sds((T, CONV_DIM), F32),
                  compiler_params=_cparams(("arbitrary",)))(projp, conv_w, conv_b)


def conv_bwd(name, projp, dact, conv_w, conv_b, col0):
    T, C = dact.shape
    c0 = col0 // CONV_CB

    def body(x_ref, da_ref, w_ref, b_ref, dx_ref, dw_ref, db_ref):
        x = x_ref[...]
        dpre = da_ref[...] * _dsilu(_conv_pre(x, w_ref, b_ref))
        dx = dpre * w_ref[CONV_WIDTH - 1:CONV_WIDTH, :]
        dw_ref[CONV_WIDTH - 1:CONV_WIDTH, :] = jnp.sum(dpre * x, axis=0, keepdims=True)
        for s in range(1, CONV_WIDTH):
            i = CONV_WIDTH - 1 - s
            dx = dx + _shift_up(dpre, s) * w_ref[i:i + 1, :]
            dw_ref[i:i + 1, :] = jnp.sum(dpre * _shift_down(x, s), axis=0, keepdims=True)
        dx_ref[...] = dx.astype(BF16)
        db_ref[...] = jnp.sum(dpre, axis=0, keepdims=True)

    return _pcall(body, name=name, grid=(C // CONV_CB,),
                  in_specs=[pl.BlockSpec((T, CONV_CB), lambda c: (0, OFF_XBC // CONV_CB + c0 + c)),
                            pl.BlockSpec((T, CONV_CB), lambda c: (0, c)),
                            pl.BlockSpec((CONV_WIDTH, CONV_CB), lambda c: (0, c0 + c)), pl.BlockSpec((1, CONV_CB), lambda c: (0, c0 + c))],
                  out_specs=[pl.BlockSpec((T, CONV_CB), lambda c: (0, c)), pl.BlockSpec((CONV_WIDTH, CONV_CB), lambda c: (0, c)),
                             pl.BlockSpec((1, CONV_CB), lambda c: (0, c))],
                  out_shape=[_sds((T, C), BF16), _sds((CONV_WIDTH, C), F32), _sds((1, C), F32)],
                  compiler_params=_cparams(("arbitrary",)))(projp, dact, conv_w, conv_b)


def _softplus(x):
    return jnp.maximum(x, 0.0) + jnp.log1p(jnp.exp(-jnp.abs(x)))


def _tri(lower):
    r = lax.broadcasted_iota(jnp.int32, (BLK, BLK), 0)
    c = lax.broadcasted_iota(jnp.int32, (BLK, BLK), 1)
    return (r >= c) if lower else (c >= r)


def _ssd_chunk_setup(dt_ref, dtb_ref, alog_ref):
    raw = dt_ref[...] + dtb_ref[...]
    dt = _softplus(raw)
    aneg = -jnp.exp(alog_ref[...])
    a = dt * aneg
    acs = jnp.dot(_tri(True).astype(F32), a, precision=lax.Precision.HIGHEST, preferred_element_type=F32)
    return raw, dt, aneg, acs, acs.T


def _ssd_specs(T, rev):
    nc = T // BLK
    ci = (lambda c: nc - 1 - c) if rev else (lambda c: c)
    xs = pl.BlockSpec((BLK, D_INNER), lambda c: (ci(c), 0))
    bm = pl.BlockSpec((BLK, SSM_GROUPS * D_STATE), lambda c: (ci(c), D_INNER // (SSM_GROUPS * D_STATE)))
    cm = pl.BlockSpec((BLK, SSM_GROUPS * D_STATE), lambda c: (ci(c), D_INNER // (SSM_GROUPS * D_STATE) + 1))
    dt = pl.BlockSpec((BLK, DT_PAD), lambda c: (ci(c), OFF_DT // DT_PAD))
    v128 = pl.BlockSpec((1, 128), lambda c: (0, 0))
    dfull = pl.BlockSpec((1, D_INNER), lambda c: (0, 0))
    st = pl.BlockSpec((None, SSM_HEADS, HEAD_DIM, D_STATE), lambda c: (ci(c), 0, 0, 0))
    return xs, bm, cm, dt, v128, dfull, st, ci


GW = HEADS_PER_GROUP * HEAD_DIM


def _expanders():
    e = np.zeros((SSM_GROUPS, 128, GW), np.float32)
    for g in range(SSM_GROUPS):
        for hh in range(HEADS_PER_GROUP):
            e[g, HEADS_PER_GROUP * g + hh, HEAD_DIM * hh:HEAD_DIM * (hh + 1)] = 1.0
    return jnp.asarray(e), jnp.asarray(np.transpose(e, (0, 2, 1)).copy())


def _dotx(a, b):
    return jnp.dot(a, b, precision=lax.Precision.HIGHEST, preferred_element_type=F32)


def _decay(acs, acsT, h, tril):
    return jnp.where(tril, jnp.exp(jnp.where(tril, acs[:, h:h + 1] - acsT[h:h + 1, :], 0.0)), 0.0)


def ssd_fwd(xbc, projp, dtb, alog, dfull):
    T = xbc.shape[0]
    nc = T // BLK
    xs, bm, cm, dts, v128, dfs, st, _ = _ssd_specs(T, False)
    E, _ = _expanders()

    def body(xs_ref, b_ref, c_ref, dt_ref, dtb_ref, alog_ref, d_ref, e_ref, y_ref, st_ref, h_scr):
        c = pl.program_id(0)

        @pl.when(c == 0)
        def _():
            h_scr[...] = jnp.zeros_like(h_scr)

        _, dt, _, acs, acsT = _ssd_chunk_setup(dt_ref, dtb_ref, alog_ref)
        tril = _tri(True)
        alast = acs[BLK - 1:BLK, :]
        eacs = jnp.exp(acs)
        wmat = jnp.exp(alast - acs)
        gam = jnp.exp(alast)
        for g in range(SSM_GROUPS):
            gl = slice(GW * g, GW * (g + 1))
            hsl = slice(HEADS_PER_GROUP * g, HEADS_PER_GROUP * (g + 1))
            heads = [HEADS_PER_GROUP * g + hh for hh in range(HEADS_PER_GROUP)]
            Eg = e_ref[g]
            B = b_ref[:, D_STATE * g:D_STATE * (g + 1)].astype(BF16)
            C = c_ref[:, D_STATE * g:D_STATE * (g + 1)].astype(BF16)
            cb = _dot(C, B, NT)
            x_g = xs_ref[:, gl]
            xd_g = x_g * _dotx(dt, Eg)
            hold = h_scr[hsl]
            st_ref[hsl] = hold
            hcat = hold.reshape(GW, D_STATE)
            yoff = _dotx(eacs, Eg) * _dot(C, hcat, NT)
            S = _dot(xd_g * _dotx(wmat, Eg), B, TN)
            Ms = [cb * _decay(acs, acsT, h, tril) for h in heads]
            ys = [_dot(Ms[hh], xd_g[:, HEAD_DIM * hh:HEAD_DIM * (hh + 1)], NN) for hh in range(HEADS_PER_GROUP)]
            for hh, h in enumerate(heads):
                h_scr[h] = gam[:, h:h + 1] * hold[hh] + S[HEAD_DIM * hh:HEAD_DIM * (hh + 1)]
            y_ref[:, gl] = jnp.concatenate(ys, axis=1) + yoff + d_ref[:, gl] * x_g

    espec = pl.BlockSpec((SSM_GROUPS, 128, GW), lambda c: (0, 0, 0))
    return _pcall(body, name="ssd_fwd", grid=(nc,), in_specs=[xs, bm, cm, dts, v128, v128, dfs, espec],
                  out_specs=[xs, st], out_shape=[_sds((T, D_INNER), F32), _sds((nc, SSM_HEADS, HEAD_DIM, D_STATE), F32)],
                  scratch_shapes=[pltpu.VMEM((SSM_HEADS, HEAD_DIM, D_STATE), F32)],
                  compiler_params=_cparams(("arbitrary",)))(xbc, xbc, xbc, projp, dtb, alog, dfull, E)


def ssd_bwd(xbc, projp, dtb, alog, dfull, states, dy):
    T = xbc.shape[0]
    nc = T // BLK
    xs, bm, cm, dts, v128, dfs, st, ci = _ssd_specs(T, True)
    gn = SSM_GROUPS * D_STATE
    E, ET = _expanders()

    def body(xs_ref, b_ref, c_ref, dt_ref, dtb_ref, alog_ref, d_ref, st_ref, dy_ref, e_ref, et_ref,
             dxs_ref, dB_ref, dC_ref, ddt_ref, dal_ref, dD_ref, ddtb_ref, dh_scr):
        i = pl.program_id(0)

        @pl.when(i == 0)
        def _():
            dh_scr[...] = jnp.zeros_like(dh_scr)
            dal_ref[...] = jnp.zeros_like(dal_ref)
            dD_ref[...] = jnp.zeros_like(dD_ref)
            ddtb_ref[...] = jnp.zeros_like(ddtb_ref)

        raw, dt, aneg, acs, acsT = _ssd_chunk_setup(dt_ref, dtb_ref, alog_ref)
        tril = _tri(True)
        lane = lax.broadcasted_iota(jnp.int32, (BLK, 128), 1)
        sub = lax.broadcasted_iota(jnp.int32, (BLK, 128), 0)
        alast = acs[BLK - 1:BLK, :]
        eacs = jnp.exp(acs)
        wmat = jnp.exp(alast - acs)
        gam = jnp.exp(alast)
        gcol = jnp.exp(acsT[:, BLK - 1:BLK])
        ds_col = jnp.zeros((BLK, 128), F32)
        ds_row = jnp.zeros((BLK, 128), F32)
        ddt_col = jnp.zeros((BLK, 128), F32)
        dDm = jnp.zeros((BLK, 128), F32)
        hl = [slice(HEAD_DIM * hh, HEAD_DIM * (hh + 1)) for hh in range(HEADS_PER_GROUP)]
        for g in range(SSM_GROUPS):
            gl = slice(GW * g, GW * (g + 1))
            gs = slice(D_STATE * g, D_STATE * (g + 1))
            hsl = slice(HEADS_PER_GROUP * g, HEADS_PER_GROUP * (g + 1))
            heads = [HEADS_PER_GROUP * g + hh for hh in range(HEADS_PER_GROUP)]
            Eg, ETg = e_ref[g], et_ref[g]
            B = b_ref[:, gs].astype(BF16)
            C = c_ref[:, gs].astype(BF16)
            cb = _dot(C, B, NT)
            x_g, dy_g = xs_ref[:, gl], dy_ref[:, gl]
            dt_x, w_x = _dotx(dt, Eg), _dotx(wmat, Eg)
            xd_g = x_g * dt_x
            dye = dy_g * _dotx(eacs, Eg)
            hcat = st_ref[hsl].reshape(GW, D_STATE)
            dSv = dh_scr[hsl]
            dScat = dSv.reshape(GW, D_STATE)
            dDm = dDm + _dotx(dy_g * x_g, ETg)
            dH_y = _dot(dye, C, TN)
            dC_g = _dot(dye, hcat, NN)
            ds_col = ds_col + _dotx(dye * _dot(C, hcat, NT), ETg)
            dxdw = _dot(B, dScat, NT)
            dB_g = _dot(xd_g * w_x, dScat, NN)
            dww = _dotx(xd_g * dxdw, ETg) * wmat
            ds_col = ds_col - dww + jnp.where(sub == BLK - 1, jnp.sum(dww, axis=0, keepdims=True), 0.0)
            hd = jnp.sum(_dotx(Eg, dScat * hcat), axis=1, keepdims=True) * gcol
            ds_row = ds_row - jnp.where(lane == BLK - 1, hd, 0.0)
            decays = [_decay(acs, acsT, h, tril) for h in heads]
            Ms = [cb * d for d in decays]
            dMs = [_dot(dy_g[:, hl[hh]], xd_g[:, hl[hh]], NT) for hh in range(HEADS_PER_GROUP)]
            dxd1 = [_dot(Ms[hh], dy_g[:, hl[hh]], TN) for hh in range(HEADS_PER_GROUP)]
            dG = jnp.zeros((BLK, BLK), F32)
            for hh, h in enumerate(heads):
                Q = dMs[hh] * Ms[hh]
                ds_col = ds_col + jnp.where(lane == h, jnp.sum(Q, axis=1, keepdims=True), 0.0)
                ds_row = ds_row + jnp.where(sub == h, jnp.sum(Q, axis=0, keepdims=True), 0.0)
                dG = dG + dMs[hh] * decays[hh]
            dxd_g = jnp.concatenate(dxd1, axis=1) + w_x * dxdw
            dxs_ref[:, gl] = d_ref[:, gl] * dy_g + dxd_g * dt_x
            ddt_col = ddt_col + _dotx(dxd_g * x_g, ETg)
            dC_ref[:, gs] = dC_g + _dot(dG, B, NN)
            dB_ref[:, gs] = dB_g + _dot(dG, C, TN)
            for hh, h in enumerate(heads):
                dh_scr[h] = gam[:, h:h + 1] * dSv[hh] + dH_y[hl[hh]]
        ds_all = ds_col - ds_row.T
        da = jnp.dot(_tri(False).astype(F32), ds_all, precision=lax.Precision.HIGHEST, preferred_element_type=F32)
        ddt = ddt_col + da * aneg
        draw = jnp.where(lane < SSM_HEADS, ddt * _sigmoid(raw), 0.0)
        ddt_ref[...] = draw.astype(BF16)
        dal_ref[...] += jnp.sum(da * dt, axis=0, keepdims=True) * aneg
        ddtb_ref[...] += jnp.sum(draw, axis=0, keepdims=True)
        dD_ref[...] += jnp.sum(dDm, axis=0, keepdims=True)

    gblk = pl.BlockSpec((BLK, gn), lambda c: (ci(c), 0))
    espec = pl.BlockSpec((SSM_GROUPS, 128, GW), lambda c: (0, 0, 0))
    etspec = pl.BlockSpec((SSM_GROUPS, GW, 128), lambda c: (0, 0, 0))
    return _pcall(body, name="ssd_bwd", grid=(nc,), in_specs=[xs, bm, cm, dts, v128, v128, dfs, st, xs, espec, etspec],
                  out_specs=[xs, gblk, gblk, pl.BlockSpec((BLK, DT_PAD), lambda c: (ci(c), 0)), v128, v128, v128],
                  out_shape=[_sds((T, D_INNER), F32), _sds((T, gn), F32), _sds((T, gn), F32), _sds((T, DT_PAD), BF16),
                             _sds((1, 128), F32), _sds((1, 128), F32), _sds((1, 128), F32)],
                  scratch_shapes=[pltpu.VMEM((SSM_HEADS, HEAD_DIM, D_STATE), F32)],
                  compiler_params=_cparams(("arbitrary",)))(xbc, xbc, xbc, projp, dtb, alog, dfull, states, dy, E, ET)


_WIN_ORDER = ("z", "ga", "gs", "xbc", "q", "k", "v", "dt")


def _win_to_padded(win_g):
    full = win_g.reshape(IN_DIM, D_MODEL)
    rows = []
    for nm in _WIN_ORDER:
        s, w = SEG[nm]
        rows.append(full[s:s + w])
    rows.append(jnp.zeros((DT_PAD - SEG["dt"][1], D_MODEL), win_g.dtype))
    return jnp.concatenate(rows, axis=0)


def _padded_to_win(dw):
    off = dict(z=OFF_Z, ga=OFF_GA, gs=OFF_GS, xbc=OFF_XBC, q=OFF_Q, k=OFF_K, v=OFF_V, dt=OFF_DT)
    per = IN_DIM // N_DEV
    blocks = []
    for j in range(N_DEV):
        lo, hi, rows = j * per, (j + 1) * per, []
        for nm in ("q", "k", "v", "z", "xbc", "dt", "ga", "gs"):
            s, w = SEG[nm]
            a, b = max(lo, s), min(hi, s + w)
            if a < b:
                rows.append(dw[off[nm] + a - s:off[nm] + b - s])
        blocks.append(jnp.concatenate(rows, axis=0))
    return jnp.stack(blocks)


def _pad128(v):
    return jnp.pad(v, ((0, 0), (0, 128 - v.shape[1])))


_SMALL = (("loss", 128, 1), ("g_mix", 2048, 2048), ("conv_b", 3072, 3072), ("dt_bias", 128, 32), ("a_log", 128, 32),
          ("d_skip", 128, 32), ("g_ssd", 2048, 2048), ("sinks", 128, 16), ("g_ffn", 2048, 2048), ("g_ple", 2048, 2048),
          ("g_final", 2048, 2048))


def _small_vec(d):
    parts = []
    for nm, pw, w in _SMALL:
        v = d[nm].reshape(1, -1).astype(F32)
        parts.append(jnp.pad(v[:, :min(v.shape[1], pw)], ((0, 0), (0, pw - min(v.shape[1], pw)))))
    return jnp.concatenate(parts, axis=1)


def _small_split(vec):
    out, o = {}, 0
    for nm, pw, w in _SMALL:
        out[nm] = vec[0, o:o + w]
        o += pw
    return out


def kernel(x, p, positions, g_mix, w_in, conv_w, conv_b, dt_bias, a_log, d_skip, g_ssd, sinks, w_attn_br, w_ssd_br, w_o, g_ffn, w_gate, w_up, w_down, g_ple, w_ple_gate, w_ple_proj, g_final, loss_target, m_g_mix, m_w_in, m_conv_w, m_conv_b, m_dt_bias, m_a_log, m_d_skip, m_g_ssd, m_sinks, m_w_attn_br, m_w_ssd_br, m_w_o, m_g_ffn, m_w_gate, m_w_up, m_w_down, m_g_ple, m_w_ple_gate, m_w_ple_proj, m_g_final, v_g_mix, v_w_in, v_conv_w, v_conv_b, v_dt_bias, v_a_log, v_d_skip, v_g_ssd, v_sinks, v_w_attn_br, v_w_ssd_br, v_w_o, v_g_ffn, v_w_gate, v_w_up, v_w_down, v_g_ple, v_w_ple_gate, v_w_ple_proj, v_g_final):
    T = x.shape[1]
    D = D_MODEL
    W = dict(g_mix=g_mix, w_in=w_in, conv_w=conv_w, conv_b=conv_b, dt_bias=dt_bias, a_log=a_log, d_skip=d_skip, g_ssd=g_ssd,
             sinks=sinks, w_attn_br=w_attn_br, w_ssd_br=w_ssd_br, w_o=w_o, g_ffn=g_ffn, w_gate=w_gate, w_up=w_up, w_down=w_down,
             g_ple=g_ple, w_ple_gate=w_ple_gate, w_ple_proj=w_ple_proj, g_final=g_final)
    Mo = dict(g_mix=m_g_mix, w_in=m_w_in, conv_w=m_conv_w, conv_b=m_conv_b, dt_bias=m_dt_bias, a_log=m_a_log, d_skip=m_d_skip,
              g_ssd=m_g_ssd, sinks=m_sinks, w_attn_br=m_w_attn_br, w_ssd_br=m_w_ssd_br, w_o=m_w_o, g_ffn=m_g_ffn, w_gate=m_w_gate,
              w_up=m_w_up, w_down=m_w_down, g_ple=m_g_ple, w_ple_gate=m_w_ple_gate, w_ple_proj=m_w_ple_proj, g_final=m_g_final)
    Vo = dict(g_mix=v_g_mix, w_in=v_w_in, conv_w=v_conv_w, conv_b=v_conv_b, dt_bias=v_dt_bias, a_log=v_a_log, d_skip=v_d_skip,
              g_ssd=v_g_ssd, sinks=v_sinks, w_attn_br=v_w_attn_br, w_ssd_br=v_w_ssd_br, w_o=v_w_o, g_ffn=v_g_ffn, w_gate=v_w_gate,
              w_up=v_w_up, w_down=v_w_down, g_ple=v_g_ple, w_ple_gate=v_w_ple_gate, w_ple_proj=v_w_ple_proj, g_final=v_g_final)
    order = ["g_mix", "w_in", "conv_w", "conv_b", "dt_bias", "a_log", "d_skip", "g_ssd", "sinks", "w_attn_br", "w_ssd_br", "w_o",
             "g_ffn", "w_gate", "w_up", "w_down", "g_ple", "w_ple_gate", "w_ple_proj", "g_final"]
    big = ["w_in", "conv_w", "w_attn_br", "w_ssd_br", "w_o", "w_gate", "w_up", "w_down", "w_ple_gate", "w_ple_proj"]

    x2 = x.reshape(T, D)
    p2 = p.reshape(T, PLE_DIM)
    tgt = loss_target.reshape(T, D)
    posf = positions.reshape(T, 1).astype(F32)
    inv = ROPE_THETA ** (-np.arange(HEAD_DIM // 2, dtype=np.float32) * 2.0 / HEAD_DIM)
    inv128 = jnp.asarray(np.tile(inv, 128 // (HEAD_DIM // 2)).reshape(1, 128).astype(np.float32))
    transposed = ("w_in",)

    def shard2d(a, n):
        a = a.reshape(a.shape[-2:])
        return a.T if n in transposed else a

    sh = {n: shard2d(W[n], n) for n in big}

    del _PENDING[:]
    me = 4 * lax.axis_index("x") + 2 * lax.axis_index("y") + lax.axis_index("c")
    groups = (("w_in",), ("conv_w", "w_attn_br", "w_ssd_br", "w_o"), ("w_gate", "w_up", "w_down"), ("w_ple_gate", "w_ple_proj"))
    send = {n: sh[n] if n == "conv_w" else sh[n].astype(BF16) for n in big}
    started, prev = [], None
    for gi, grp in enumerate(groups):
        zones = [lax.dynamic_update_index_in_dim(lax.empty((N_DEV,) + send[n].shape, send[n].dtype), send[n], me, 0) for n in grp]
        h = split_start("gather_start_%d" % gi, "gather", ICI_SAME_CORE, [], lands=zones, after=prev)
        prev = h["token"]
        started.append(h)
    gathered, fwd = {}, {}

    def forward_start(gi, after):
        _, lands = split_wait("gather_wait_%d" % gi, started[gi], after)
        fwd[gi] = split_start("forward_start_%d" % gi, "forward", FORWARD_BLOCKS, [], lands=lands)

    def forward_wait(gi, after):
        _, full = split_wait("forward_wait_%d" % gi, fwd[gi], after)
        gathered.update(zip(groups[gi], full))

    u = rms_fwd("norm_mix", x2, g_mix)
    forward_start(0, u)
    forward_wait(0, u)
    forward_start(1, u)
    winp = _win_to_padded(gathered["w_in"])
    dtb = _pad128(dt_bias)
    alog = _pad128(a_log)
    dfull = jnp.repeat(d_skip.reshape(SSM_HEADS), HEAD_DIM).reshape(1, D_INNER)

    projp = mm_nt("in_proj", u, winp, 640)
    attn, qr, kr = attn_fwd(projp, posf, inv128, sinks)
    forward_wait(1, attn)
    convw = jnp.transpose(gathered["conv_w"], (1, 0, 2)).reshape(CONV_WIDTH, CONV_DIM)
    wab = gathered["w_attn_br"]
    wsb = gathered["w_ssd_br"].reshape(D, D)
    wo = gathered["w_o"].reshape(D, D)
    xbc = conv_fwd(projp, convw, conv_b)
    y, states = ssd_fwd(xbc, projp, dtb, alog, dfull)
    yn = gnorm_fwd(y, projp, g_ssd)
    out_a = mm_nn_colblk("attn_br", attn, wab)
    out_s = mm_nn("ssd_br", yn, wsb, 512)
    forward_start(2, out_s)
    merged = merge_fwd(projp, out_a, out_s)
    h1 = mm_nn("o_proj", merged, wo, 512, residual=x2)
    f = rms_fwd("norm_ffn", h1, g_ffn)
    forward_wait(2, f)
    wg, wu, wd = gathered["w_gate"], gathered["w_up"], gathered["w_down"]
    gate, up, act = ffn_up(f, wg, wu)
    forward_start(3, act)
    h2 = ffn_down(act, wd, h1)
    r = rms_fwd("norm_ple", h2, g_ple)
    forward_wait(3, r)
    wpg = gathered["w_ple_gate"].reshape(D, D)
    wpp = gathered["w_ple_proj"]
    pg = mm_nn("ple_gate", r, wpg, 512)
    pp = mm_nn_colblk("ple_proj", p2, wpp)
    loss_v, dh3, dpg, dpp, dg_final = head_fwd_bwd(h2, pg, pp, g_final.reshape(1, D), tgt)

    gw = {}
    scat = []

    def scatter_start(names):
        scat.append((names, split_start("scatter_start_%d" % len(scat), "scatter", ALL_PEERS, [gw[n] for n in names])))

    gw["w_ple_proj"] = mm_tn_colblk("dw_ple_proj", p2, dpp, PLE_DIM)
    dr = mm_nt("d_ple_gate", dpg, wpg, 512)
    gw["w_ple_gate"] = mm_tn("dw_ple_gate", r, dpg, 512, 1024).reshape(N_DEV, D // N_DEV, D)
    scatter_start(("w_ple_proj", "w_ple_gate"))
    dh2, dh2b, dg_ple = rms_bwd("norm_ple_bwd", h2, g_ple, dr, dh3)
    dgate, dup = ffn_down_bwd(dh2b, wd, gate, up)
    gw["w_down"] = wgrad_rowblk_lhs("dw_down", act, dh2b, 1024)
    gw["w_gate"] = wgrad_colblk_rhs("dw_gate", f, dgate, 1024)
    gw["w_up"] = wgrad_colblk_rhs("dw_up", f, dup, 1024)
    scatter_start(("w_down", "w_gate", "w_up"))
    df = ffn_up_bwd(dgate, dup, wg, wu)
    dh1, dh1b, dg_ffn = rms_bwd("norm_ffn_bwd", h1, g_ffn, df, dh2)
    dmerged = mm_nt("d_o_proj", dh1b, wo, 512)
    gw["w_o"] = mm_tn("dw_o", merged, dh1b, 512, 1024).reshape(N_DEV, D // N_DEV, D)
    dout_a, dout_s, dga, dgs = merge_bwd(projp, out_a, out_s, dmerged)
    gw["w_ssd_br"] = mm_tn("dw_ssd_br", yn, dout_s, 512, 1024).reshape(N_DEV, D // N_DEV, D)
    gw["w_attn_br"] = mm_tn_colblk("dw_attn_br", attn, dout_a, D // N_DEV)
    scatter_start(("w_o", "w_ssd_br", "w_attn_br"))
    dyn = mm_nt("d_ssd_br", dout_s, wsb, 512)
    dattn = attn_br_bwd(dout_a, wab)
    dy, dz, dg_ssd = gnorm_bwd(y, projp, g_ssd, dyn)
    dxs, dbm, dcm, ddt, dal, ddsk, ddtb = ssd_bwd(xbc, projp, dtb, alog, dfull, states, dy)
    dx_x, dwc_x, dbc_x = conv_bwd("conv_bwd_x", projp, dxs, convw, conv_b, 0)
    dx_b, dwc_b, dbc_b = conv_bwd("conv_bwd_b", projp, dbm, convw, conv_b, D_INNER)
    dx_c, dwc_c, dbc_c = conv_bwd("conv_bwd_c", projp, dcm, convw, conv_b, D_INNER + SSM_GROUPS * D_STATE)
    dq, dk, dv, dsk = attn_bwd(qr, kr, projp, dattn, posf, inv128, sinks)
    dproj = jnp.concatenate([dz, dga, dgs, dx_x, dx_b, dx_c, dq, dk.astype(BF16), dv.astype(BF16), ddt], axis=1)
    gw_in = _padded_to_win(mm_tn("dw_in", dproj, u, 640, 1024))
    pair = split_start("pair_start", "pair", FORWARD_BLOCKS, [gw_in])
    dconvw = jnp.concatenate([dwc_x, dwc_b, dwc_c], axis=1)
    gw["conv_w"] = jnp.transpose(dconvw.reshape(CONV_WIDTH, N_DEV, CONV_DIM // N_DEV), (1, 0, 2))
    scatter_start(("conv_w",))
    du = mm_nn_red("d_in_proj", dproj, winp, 1024, 640)
    (gw_in,), (sibling_part,) = split_wait("pair_wait", pair, du)
    my_part = jnp.stack([lax.dynamic_index_in_dim(gw_in, jnp.bitwise_xor(me, k), 0, keepdims=False) for k in FORWARD_BLOCKS])
    core = split_start("core_start", "scatter_core", ICI_SAME_CORE, [pair_sum("pair_sum_w_in", my_part, sibling_part)])
    gx, _, dg_mix = rms_bwd("norm_mix_bwd", x2, g_mix, du, dh1)

    res = {}
    after = gx
    for si, (names, h) in enumerate(scat):
        srcs, lands = split_wait("scatter_wait_%d" % si, h, after)
        for n, mine, arrived in zip(names, srcs, lands):
            own = lax.dynamic_index_in_dim(mine, me, 0, keepdims=False)
            res[n] = adamw("adamw_" + n, arrived, sh[n], shard2d(Mo[n], n), shard2d(Vo[n], n), own=own)
        after = res[names[0]][0]
    (pair_sums,), (arrived,) = split_wait("core_wait", core, after)
    res["w_in"] = adamw("adamw_w_in", arrived, sh["w_in"], shard2d(Mo["w_in"], "w_in"), shard2d(Vo["w_in"], "w_in"), own=pair_sums[0])

    small_g = dict(loss=loss_v[:, :1], g_mix=dg_mix, conv_b=jnp.concatenate([dbc_x, dbc_b, dbc_c], axis=1), dt_bias=ddtb,
                   a_log=dal, d_skip=ddsk, g_ssd=dg_ssd, sinks=dsk, g_ffn=dg_ffn, g_ple=dg_ple, g_final=dg_final)
    zero = jnp.zeros((1, 1), F32)
    vec_parts = exchange("gather_small", [_small_vec(small_g)], "gather")[0]
    sres = adamw("adamw_small", vec_parts, _small_vec({**W, "loss": zero}), _small_vec({**Mo, "loss": zero}),
                 _small_vec({**Vo, "loss": zero}))
    ssplit = [_small_split(a) for a in sres]
    loss = ssplit[0]["loss"].reshape(())
    for n in order:
        if n not in res:
            res[n] = tuple(s[n].reshape(W[n].shape) for s in ssplit)
        else:
            res[n] = tuple((a.T if n in transposed else a).reshape(W[n].shape) for a in res[n])
    outs = [loss, gx.reshape(x.shape)]
    for k in range(4):
        outs += [res[n][k] for n in order]
    return tuple(outs)
```

```python
import functools

import numpy as np
import jax
import jax.numpy as jnp
from jax import lax
from jax.experimental import pallas as pl
from jax.experimental.pallas import tpu as pltpu

F32 = jnp.float32
BF16 = jnp.bfloat16

N_DEV = 8
D_MODEL = 2048
HEAD_DIM = 64
ATTN_HEADS = 16
KV_HEADS = 4
Q_DIM = 1024
KV_DIM = 256
BLK = 128
D_INNER = 2048
SSM_HEADS = 32
SSM_GROUPS = 4
HEADS_PER_GROUP = 8
D_STATE = 128
CONV_WIDTH = 4
CONV_DIM = 3072
FFN_HIDDEN = 5632
PLE_DIM = 256
IN_DIM = 10784
NORM_EPS = 1e-6
SSM_NORM_EPS = 1e-5
ROPE_THETA = 10000.0

OFF_Z, OFF_GA, OFF_GS, OFF_XBC, OFF_Q, OFF_K, OFF_V, OFF_DT = 0, 2048, 4096, 6144, 9216, 10240, 10496, 10752
IN_PAD = 10880
DT_PAD = 128
SEG = dict(q=(0, 1024), k=(1024, 256), v=(1280, 256), z=(1536, 2048), xbc=(3584, 3072), dt=(6656, 32),
           ga=(6688, 2048), gs=(8736, 2048))

ADAM_LR, ADAM_B1, ADAM_B2, ADAM_EPS, ADAM_WD, ADAM_STEP = 0.001, 0.9, 0.999, 1e-08, 0.01, 10

VMEM_LIMIT = 56 * 1024 * 1024

NN = (((1,), (0,)), ((), ()))
NT = (((1,), (1,)), ((), ()))
TN = (((0,), (0,)), ((), ()))


_PENDING = []


def _raw_call(body, **kw):
    return pl.pallas_call(body, **kw)


def _pcall(body, **kw):
    if "in_specs" not in kw:
        return _raw_call(body, **kw)
    deps = list(_PENDING)
    del _PENDING[:]
    if not deps:
        return _raw_call(body, **kw)
    n_in = len(kw["in_specs"])

    def tied(*refs):
        return body(*refs[:n_in], *refs[n_in + len(deps):])

    kw["in_specs"] = list(kw["in_specs"]) + [pl.BlockSpec(memory_space=pl.ANY)] * len(deps)
    call = _raw_call(tied, **kw)
    return lambda *ops: call(*ops, *deps)


def _cparams(sem=None):
    if sem is None:
        return pltpu.CompilerParams(vmem_limit_bytes=VMEM_LIMIT)
    return pltpu.CompilerParams(vmem_limit_bytes=VMEM_LIMIT, dimension_semantics=sem)


def _dot(a, b, dn):
    return lax.dot_general(a.astype(BF16), b.astype(BF16), dn, preferred_element_type=F32)


def _sigmoid(x):
    return 1.0 / (1.0 + jnp.exp(-x))


def _silu(x):
    return x * _sigmoid(x)


def _dsilu(x):
    s = _sigmoid(x)
    return s * (1.0 + x * (1.0 - s))


def _matmul(name, pairs, pair_specs, dn, grid, out_shapes, out_specs, nred=1, extra=(), extra_specs=(),
            epilogue=None, acc_shape=None):
    n_in = 2 * len(pairs) + len(extra)
    n_out = len(out_shapes)

    def body(*refs):
        ins = refs[:2 * len(pairs)]
        ex = refs[2 * len(pairs):n_in]
        outs = refs[n_in:n_in + n_out]

        def prod():
            s = None
            for p in range(len(pairs)):
                d = _dot(ins[2 * p][...], ins[2 * p + 1][...], dn)
                s = d if s is None else s + d
            return s

        def finish(val):
            if epilogue is None:
                outs[0][...] = val.astype(outs[0].dtype)
            else:
                res = epilogue(val, *[e[...] for e in ex])
                for o, r in zip(outs, res):
                    o[...] = r.astype(o.dtype)

        if nred == 1:
            finish(prod())
        else:
            acc = refs[n_in + n_out]
            k = pl.program_id(len(grid) - 1)

            @pl.when(k == 0)
            def _():
                acc[...] = jnp.zeros_like(acc)

            acc[...] += prod()

            @pl.when(k == nred - 1)
            def _():
                finish(acc[...])

    operands = []
    specs = []
    for (a, b), (sa, sb) in zip(pairs, pair_specs):
        operands += [a, b]
        specs += [sa, sb]
    operands += list(extra)
    specs += list(extra_specs)
    scratch = [pltpu.VMEM(acc_shape, F32)] if nred > 1 else []
    sem = ("arbitrary",) * len(grid)
    res = _pcall(body, name=name, grid=grid, in_specs=specs, out_specs=list(out_specs),
                 out_shape=list(out_shapes), scratch_shapes=scratch, compiler_params=_cparams(sem))(*operands)
    return res


def _sds(shape, dtype):
    return jax.ShapeDtypeStruct(shape, dtype)


def _row_tile(T):
    return min(1024, T)


def mm_nn(name, a, b, tn, out_dtype=F32, residual=None):
    M, K = a.shape
    N = b.shape[1]
    tm = _row_tile(M)
    grid = (M // tm, N // tn)
    extra, especs, epi = (), (), None
    if residual is not None:
        extra = (residual,)
        especs = (pl.BlockSpec((tm, tn), lambda i, n: (i, n)),)
        epi = lambda v, r: (v + r,)
    return _matmul(name, [(a, b)], [(pl.BlockSpec((tm, K), lambda i, n: (i, 0)), pl.BlockSpec((K, tn), lambda i, n: (0, n)))],
                   NN, grid, [_sds((M, N), out_dtype)], [pl.BlockSpec((tm, tn), lambda i, n: (i, n))],
                   extra=extra, extra_specs=especs, epilogue=epi)[0]


def mm_nn_colblk(name, a, b, out_dtype=F32):
    M, K = a.shape
    J, _, nb = b.shape
    tm = _row_tile(M)
    grid = (M // tm, J)
    return _matmul(name, [(a, b)], [(pl.BlockSpec((tm, K), lambda i, j: (i, 0)), pl.BlockSpec((None, K, nb), lambda i, j: (j, 0, 0)))],
                   NN, grid, [_sds((M, J * nb), out_dtype)], [pl.BlockSpec((tm, nb), lambda i, j: (i, j))])[0]


def mm_nt(name, a, w, tr, out_dtype=F32):
    M, C = a.shape
    R = w.shape[0]
    tm = _row_tile(M)
    grid = (M // tm, R // tr)
    return _matmul(name, [(a, w)], [(pl.BlockSpec((tm, C), lambda i, r: (i, 0)), pl.BlockSpec((tr, C), lambda i, r: (r, 0)))],
                   NT, grid, [_sds((M, R), out_dtype)], [pl.BlockSpec((tm, tr), lambda i, r: (i, r))])[0]


def mm_nt_red(name, a, w, tr, tk, out_dtype=F32):
    M, C = a.shape
    R = w.shape[0]
    tm = _row_tile(M)
    nk = C // tk
    grid = (M // tm, R // tr, nk)
    return _matmul(name, [(a, w)], [(pl.BlockSpec((tm, tk), lambda i, r, k: (i, k)), pl.BlockSpec((tr, tk), lambda i, r, k: (r, k)))],
                   NT, grid, [_sds((M, R), out_dtype)], [pl.BlockSpec((tm, tr), lambda i, r, k: (i, r))],
                   nred=nk, acc_shape=(tm, tr))[0]


def mm_nn_red(name, a, b, tn, tk, out_dtype=F32):
    M, K = a.shape
    N = b.shape[1]
    tm = _row_tile(M)
    nk = K // tk
    grid = (M // tm, N // tn, nk)
    return _matmul(name, [(a, b)], [(pl.BlockSpec((tm, tk), lambda i, n, k: (i, k)), pl.BlockSpec((tk, tn), lambda i, n, k: (k, n)))],
                   NN, grid, [_sds((M, N), out_dtype)], [pl.BlockSpec((tm, tn), lambda i, n, k: (i, n))],
                   nred=nk, acc_shape=(tm, tn))[0]


def mm_tn(name, x, dy, tr, tc, out_dtype=BF16):
    M, R = x.shape
    C = dy.shape[1]
    grid = (R // tr, C // tc)
    return _matmul(name, [(x, dy)], [(pl.BlockSpec((M, tr), lambda r, c: (0, r)), pl.BlockSpec((M, tc), lambda r, c: (0, c)))],
                   TN, grid, [_sds((R, C), out_dtype)], [pl.BlockSpec((tr, tc), lambda r, c: (r, c))])[0]


def mm_tn_colblk(name, x, dy, nb, out_dtype=BF16):
    M, R = x.shape
    J = dy.shape[1] // nb
    grid = (J,)
    return _matmul(name, [(x, dy)], [(pl.BlockSpec((M, R), lambda j: (0, 0)), pl.BlockSpec((M, nb), lambda j: (0, j)))],
                   TN, grid, [_sds((J, R, nb), out_dtype)], [pl.BlockSpec((None, R, nb), lambda j: (j, 0, 0))])[0]


def _rows(T):
    return min(256, T)


def rms_fwd(name, x, g, eps=NORM_EPS):
    T, D = x.shape
    tm = _rows(T)

    def body(x_ref, g_ref, o_ref):
        xv = x_ref[...]
        r = lax.rsqrt(jnp.mean(xv * xv, axis=-1, keepdims=True) + eps)
        o_ref[...] = (xv * r * g_ref[...]).astype(BF16)

    return _pcall(body, name=name, grid=(T // tm,),
                  in_specs=[pl.BlockSpec((tm, D), lambda i: (i, 0)), pl.BlockSpec((1, D), lambda i: (0, 0))],
                  out_specs=pl.BlockSpec((tm, D), lambda i: (i, 0)), out_shape=_sds((T, D), BF16),
                  compiler_params=_cparams(("arbitrary",)))(x, g)


def rms_bwd(name, x, g, dy, dres, eps=NORM_EPS):
    T, D = x.shape
    tm = _rows(T)

    def body(x_ref, g_ref, dy_ref, dr_ref, dx_ref, dxb_ref, dg_ref):
        i = pl.program_id(0)
        xv = x_ref[...]
        r = lax.rsqrt(jnp.mean(xv * xv, axis=-1, keepdims=True) + eps)
        xh = xv * r
        dyv = dy_ref[...]
        gd = dyv * g_ref[...]
        dx = r * (gd - xh * jnp.mean(gd * xh, axis=-1, keepdims=True)) + dr_ref[...]
        dx_ref[...] = dx
        dxb_ref[...] = dx.astype(BF16)

        @pl.when(i == 0)
        def _():
            dg_ref[...] = jnp.zeros_like(dg_ref)

        dg_ref[...] += jnp.sum(dyv * xh, axis=0, keepdims=True)

    row = pl.BlockSpec((tm, D), lambda i: (i, 0))
    vec = pl.BlockSpec((1, D), lambda i: (0, 0))
    return _pcall(body, name=name, grid=(T // tm,), in_specs=[row, vec, row, row], out_specs=[row, row, vec],
                  out_shape=[_sds((T, D), F32), _sds((T, D), BF16), _sds((1, D), F32)],
                  compiler_params=_cparams(("arbitrary",)))(x, g, dy, dres)


def gnorm_fwd(y, projp, g):
    T, D = y.shape
    tm = _rows(T)

    def body(y_ref, z_ref, g_ref, o_ref):
        yz = y_ref[...] * _silu(z_ref[...])
        r = lax.rsqrt(jnp.mean(yz * yz, axis=-1, keepdims=True) + SSM_NORM_EPS)
        o_ref[...] = (yz * r * g_ref[...]).astype(BF16)

    row = pl.BlockSpec((tm, D), lambda i: (i, 0))
    return _pcall(body, name="gnorm_fwd", grid=(T // tm,),
                  in_specs=[row, pl.BlockSpec((tm, D), lambda i: (i, OFF_Z // D)), pl.BlockSpec((1, D), lambda i: (0, 0))],
                  out_specs=row, out_shape=_sds((T, D), BF16), compiler_params=_cparams(("arbitrary",)))(y, projp, g)


def gnorm_bwd(y, projp, g, dyn):
    T, D = y.shape
    tm = _rows(T)

    def body(y_ref, z_ref, g_ref, dyn_ref, dy_ref, dz_ref, dg_ref):
        i = pl.program_id(0)
        yv, zv = y_ref[...], z_ref[...]
        sz = _silu(zv)
        yz = yv * sz
        r = lax.rsqrt(jnp.mean(yz * yz, axis=-1, keepdims=True) + SSM_NORM_EPS)
        xh = yz * r
        dv = dyn_ref[...]
        gd = dv * g_ref[...]
        dyz = r * (gd - xh * jnp.mean(gd * xh, axis=-1, keepdims=True))
        dy_ref[...] = dyz * sz
        dz_ref[...] = (dyz * yv * _dsilu(zv)).astype(BF16)

        @pl.when(i == 0)
        def _():
            dg_ref[...] = jnp.zeros_like(dg_ref)

        dg_ref[...] += jnp.sum(dv * xh, axis=0, keepdims=True)

    row = pl.BlockSpec((tm, D), lambda i: (i, 0))
    vec = pl.BlockSpec((1, D), lambda i: (0, 0))
    return _pcall(body, name="gnorm_bwd", grid=(T // tm,),
                  in_specs=[row, pl.BlockSpec((tm, D), lambda i: (i, OFF_Z // D)), vec, row], out_specs=[row, row, vec],
                  out_shape=[_sds((T, D), F32), _sds((T, D), BF16), _sds((1, D), F32)],
                  compiler_params=_cparams(("arbitrary",)))(y, projp, g, dyn)


def merge_fwd(projp, out_a, out_s):
    T, D = out_a.shape
    tm = _rows(T)

    def body(ga_ref, gs_ref, a_ref, s_ref, o_ref):
        o_ref[...] = (_sigmoid(ga_ref[...]) * a_ref[...] + _sigmoid(gs_ref[...]) * s_ref[...]).astype(BF16)

    row = pl.BlockSpec((tm, D), lambda i: (i, 0))
    return _pcall(body, name="merge_fwd", grid=(T // tm,),
                  in_specs=[pl.BlockSpec((tm, D), lambda i: (i, OFF_GA // D)), pl.BlockSpec((tm, D), lambda i: (i, OFF_GS // D)), row, row],
                  out_specs=row, out_shape=_sds((T, D), BF16), compiler_params=_cparams(("arbitrary",)))(projp, projp, out_a, out_s)


def merge_bwd(projp, out_a, out_s, dmerged):
    T, D = out_a.shape
    tm = _rows(T)

    def body(ga_ref, gs_ref, a_ref, s_ref, dm_ref, da_ref, ds_ref, dga_ref, dgs_ref):
        dm = dm_ref[...]
        sa, ss = _sigmoid(ga_ref[...]), _sigmoid(gs_ref[...])
        da_ref[...] = (dm * sa).astype(BF16)
        ds_ref[...] = (dm * ss).astype(BF16)
        dga_ref[...] = (dm * a_ref[...] * sa * (1.0 - sa)).astype(BF16)
        dgs_ref[...] = (dm * s_ref[...] * ss * (1.0 - ss)).astype(BF16)

    row = pl.BlockSpec((tm, D), lambda i: (i, 0))
    return _pcall(body, name="merge_bwd", grid=(T // tm,),
                  in_specs=[pl.BlockSpec((tm, D), lambda i: (i, OFF_GA // D)), pl.BlockSpec((tm, D), lambda i: (i, OFF_GS // D)), row, row, row],
                  out_specs=[row] * 4, out_shape=[_sds((T, D), BF16)] * 4,
                  compiler_params=_cparams(("arbitrary",)))(projp, projp, out_a, out_s, dmerged)


def head_fwd_bwd(h2, pg, pp, g_final, target):
    T, D = h2.shape
    tm = _rows(T)

    def body(h_ref, pg_ref, pp_ref, g_ref, t_ref, loss_ref, dh_ref, dpg_ref, dpp_ref, dg_ref):
        i = pl.program_id(0)
        s = _sigmoid(pg_ref[...])
        ppv = pp_ref[...]
        h3 = h_ref[...] + s * ppv
        r = lax.rsqrt(jnp.mean(h3 * h3, axis=-1, keepdims=True) + NORM_EPS)
        xh = h3 * r
        gv = g_ref[...]
        e = xh * gv - t_ref[...]
        dyo = e * (1.0 / D)
        gd = dyo * gv
        dh = r * (gd - xh * jnp.mean(gd * xh, axis=-1, keepdims=True))
        dh_ref[...] = dh
        dpg_ref[...] = (dh * ppv * s * (1.0 - s)).astype(BF16)
        dpp_ref[...] = (dh * s).astype(BF16)

        @pl.when(i == 0)
        def _():
            dg_ref[...] = jnp.zeros_like(dg_ref)
            loss_ref[...] = jnp.zeros_like(loss_ref)

        dg_ref[...] += jnp.sum(dyo * xh, axis=0, keepdims=True)
        part = 0.5 * jnp.sum(jnp.mean(e * e, axis=-1, keepdims=True), axis=0, keepdims=True)
        loss_ref[...] += jnp.broadcast_to(part, loss_ref.shape)

    row = pl.BlockSpec((tm, D), lambda i: (i, 0))
    vec = pl.BlockSpec((1, D), lambda i: (0, 0))
    return _pcall(body, name="head_fwd_bwd", grid=(T // tm,), in_specs=[row, row, row, vec, row],
                  out_specs=[pl.BlockSpec((1, 128), lambda i: (0, 0)), row, row, row, vec],
                  out_shape=[_sds((1, 128), F32), _sds((T, D), F32), _sds((T, D), BF16), _sds((T, D), BF16), _sds((1, D), F32)],
                  compiler_params=_cparams(("arbitrary",)))(h2, pg, pp, g_final, target)


def ffn_up(f, wg, wu):
    T, D = f.shape
    J, _, nb = wg.shape
    tm = _row_tile(T)

    def body(f_ref, wg_ref, wu_ref, g_ref, u_ref, a_ref):
        fv = f_ref[...]
        g = _dot(fv, wg_ref[...], NN)
        u = _dot(fv, wu_ref[...], NN)
        g_ref[...] = g
        u_ref[...] = u
        a_ref[...] = (_silu(g) * u).astype(BF16)

    wspec = pl.BlockSpec((None, D, nb), lambda i, j: (j, 0, 0))
    ospec = pl.BlockSpec((None, tm, nb), lambda i, j: (j, i, 0))
    return _pcall(body, name="ffn_up", grid=(T // tm, J), in_specs=[pl.BlockSpec((tm, D), lambda i, j: (i, 0)), wspec, wspec],
                  out_specs=[ospec] * 3, out_shape=[_sds((J, T, nb), F32), _sds((J, T, nb), F32), _sds((J, T, nb), BF16)],
                  compiler_params=_cparams(("arbitrary", "arbitrary")))(f, wg, wu)


def ffn_down(act, wd, h1):
    J, T, nb = act.shape
    D = wd.shape[2]
    tm = _row_tile(T)
    tn = 1024
    grid = (T // tm, D // tn, J)
    return _matmul("ffn_down", [(act, wd)],
                   [(pl.BlockSpec((None, tm, nb), lambda i, n, j: (j, i, 0)), pl.BlockSpec((None, nb, tn), lambda i, n, j: (j, 0, n)))],
                   NN, grid, [_sds((T, D), F32)], [pl.BlockSpec((tm, tn), lambda i, n, j: (i, n))], nred=J, acc_shape=(tm, tn),
                   extra=(h1,), extra_specs=(pl.BlockSpec((tm, tn), lambda i, n, j: (i, n)),), epilogue=lambda v, r: (v + r,))[0]


def ffn_down_bwd(dh2b, wd, gate, up):
    T, D = dh2b.shape
    J, nb, _ = wd.shape
    tm = _row_tile(T)
    ospec = pl.BlockSpec((None, tm, nb), lambda i, j: (j, i, 0))

    def epi(da, g, u):
        return (da * u * _dsilu(g), da * _silu(g))

    return _matmul("ffn_down_bwd", [(dh2b, wd)],
                   [(pl.BlockSpec((tm, D), lambda i, j: (i, 0)), pl.BlockSpec((None, nb, D), lambda i, j: (j, 0, 0)))],
                   NT, (T // tm, J), [_sds((J, T, nb), BF16)] * 2, [ospec, ospec],
                   extra=(gate, up), extra_specs=(ospec, ospec), epilogue=epi)


def ffn_up_bwd(dgate, dup, wg, wu):
    J, T, nb = dgate.shape
    D = wg.shape[1]
    tm = _row_tile(T)
    tr = 1024
    aspec = pl.BlockSpec((None, tm, nb), lambda i, r, j: (j, i, 0))
    wspec = pl.BlockSpec((None, tr, nb), lambda i, r, j: (j, r, 0))
    return _matmul("ffn_up_bwd", [(dgate, wg), (dup, wu)], [(aspec, wspec), (aspec, wspec)], NT, (T // tm, D // tr, J),
                   [_sds((T, D), F32)], [pl.BlockSpec((tm, tr), lambda i, r, j: (i, r))], nred=J, acc_shape=(tm, tr))[0]


def wgrad_rowblk_lhs(name, xb, dy, tc):
    J, T, nb = xb.shape
    C = dy.shape[1]
    return _matmul(name, [(xb, dy)],
                   [(pl.BlockSpec((None, T, nb), lambda j, c: (j, 0, 0)), pl.BlockSpec((T, tc), lambda j, c: (0, c)))],
                   TN, (J, C // tc), [_sds((J, nb, C), BF16)], [pl.BlockSpec((None, nb, tc), lambda j, c: (j, 0, c))])[0]


def wgrad_colblk_rhs(name, x, dyb, tr):
    T, R = x.shape
    J, _, nb = dyb.shape
    return _matmul(name, [(x, dyb)],
                   [(pl.BlockSpec((T, tr), lambda j, r: (0, r)), pl.BlockSpec((None, T, nb), lambda j, r: (j, 0, 0)))],
                   TN, (J, R // tr), [_sds((J, R, nb), BF16)], [pl.BlockSpec((None, tr, nb), lambda j, r: (j, r, 0))])[0]


def attn_br_bwd(dout_a, wab):
    T, D = dout_a.shape
    J, R, nb = wab.shape
    tm = _row_tile(T)
    return _matmul("attn_br_bwd", [(dout_a, wab)],
                   [(pl.BlockSpec((tm, nb), lambda i, j: (i, j)), pl.BlockSpec((None, R, nb), lambda i, j: (j, 0, 0)))],
                   NT, (T // tm, J), [_sds((T, R), BF16)], [pl.BlockSpec((tm, R), lambda i, j: (i, 0))], nred=J, acc_shape=(tm, R))[0]


def _adam_math(w, g, m, v):
    m2 = ADAM_B1 * m + (1.0 - ADAM_B1) * g
    v2 = ADAM_B2 * v + (1.0 - ADAM_B2) * (g * g)
    m_hat = m2 / (1.0 - ADAM_B1 ** ADAM_STEP)
    v_hat = v2 / (1.0 - ADAM_B2 ** ADAM_STEP)
    delta = -ADAM_LR * (m_hat / (jnp.sqrt(v_hat) + ADAM_EPS) + ADAM_WD * w)
    return delta, m2, v2


def _sum_partials(own, parts):
    g = None if own is None else own.astype(F32)
    if parts is not None:
        for s in range(parts.shape[0]):
            t = parts[s].astype(F32)
            g = t if g is None else g + t
    return g


def adamw(name, parts, w, m, v, own=None, own_slot=None):
    R, C = w.shape
    tr, tc = R, C
    for cand in (256, 176, 128, 64, 32, 16, 8):
        if R % cand == 0 and R > cand:
            tr = cand
            break
    if tr == R and R > 256:
        tc = 256
    given = [a for a in (parts, own) if a is not None]
    pre = own_slot is not None

    def body(*refs):
        refs = refs[1:] if pre else refs
        p_ref = refs[0] if parts is not None else None
        o_ref = refs[len(given) - 1] if own is not None else None
        w_ref, m_ref, v_ref, g_ref, d_ref, m2_ref, v2_ref = refs[-7:]
        g = _sum_partials(None if o_ref is None else o_ref[...], p_ref)
        d, m2, v2 = _adam_math(w_ref[...], g, m_ref[...], v_ref[...])
        g_ref[...] = g
        d_ref[...] = d
        m2_ref[...] = m2
        v2_ref[...] = v2

    blk = pl.BlockSpec((tr, tc), lambda i, j, *s: (i, j))
    specs = [] if parts is None else [pl.BlockSpec((parts.shape[0], tr, tc), lambda i, j, *s: (0, i, j))]
    if own is not None:
        specs.append(pl.BlockSpec((None, tr, tc), lambda i, j, s: (s[0], i, j)) if pre else blk)
    specs += [blk] * 3
    grid = (R // tr, C // tc)
    out_shape = [_sds((R, C), F32)] * 4
    params = _cparams(("arbitrary", "arbitrary"))
    if not pre:
        return _pcall(body, name=name, grid=grid, in_specs=specs, out_specs=[blk] * 4, out_shape=out_shape,
                      compiler_params=params)(*given, w, m, v)
    spec = pltpu.PrefetchScalarGridSpec(num_scalar_prefetch=1, grid=grid, in_specs=specs, out_specs=[blk] * 4)
    return _pcall(body, name=name, grid_spec=spec, out_shape=out_shape,
                  compiler_params=params)(jnp.asarray(own_slot, jnp.int32).reshape(1), *given, w, m, v)


def exchange(name, arrays, mode):
    n = len(arrays)
    out_shapes = []
    for a in arrays:
        shp = a.shape if mode == "scatter" else (N_DEV,) + a.shape
        out_shapes.append(_sds(shp, a.dtype))

    def body(*refs):
        ins, outs = refs[:n], refs[n:2 * n]
        send_sems, recv_sems, local_sems = refs[2 * n:]
        x, y, c = lax.axis_index("x"), lax.axis_index("y"), lax.axis_index("c")
        me = 4 * x + 2 * y + c

        def src(a, dest):
            return ins[a].at[dest] if mode == "scatter" else ins[a]

        local = [pltpu.make_async_copy(src(a, me), outs[a].at[me], local_sems.at[a]) for a in range(n)]
        for cp in local:
            cp.start()
        remote = []
        for k in range(1, N_DEV):
            px = 1 - x if k & 4 else x
            py = 1 - y if k & 2 else y
            pc = 1 - c if k & 1 else c
            peer = 4 * px + 2 * py + pc
            for a in range(n):
                cp = pltpu.make_async_remote_copy(src_ref=src(a, peer), dst_ref=outs[a].at[me],
                                                  send_sem=send_sems.at[a * 7 + k - 1], recv_sem=recv_sems.at[a * 7 + k - 1],
                                                  device_id=(px, py, pc), device_id_type=pl.DeviceIdType.MESH)
                cp.start()
                arrival = pltpu.make_async_remote_copy(src_ref=src(a, peer), dst_ref=outs[a].at[peer],
                                                       send_sem=send_sems.at[a * 7 + k - 1], recv_sem=recv_sems.at[a * 7 + k - 1],
                                                       device_id=(px, py, pc), device_id_type=pl.DeviceIdType.MESH)
                remote.append((cp, arrival))
        for cp, arrival in remote:
            cp.wait_send()
            arrival.wait_recv()
        for cp in local:
            cp.wait()

    any_spec = pl.BlockSpec(memory_space=pl.ANY)
    return _pcall(body, name=name, in_specs=[any_spec] * n, out_specs=[any_spec] * n, out_shape=out_shapes,
                  scratch_shapes=[pltpu.SemaphoreType.DMA((7 * n,)), pltpu.SemaphoreType.DMA((7 * n,)), pltpu.SemaphoreType.DMA((n,))],
                  compiler_params=pltpu.CompilerParams(has_side_effects=True))(*arrays)


_HBM = pl.BlockSpec(memory_space=pltpu.HBM)
_SEM = pl.BlockSpec(memory_space=pltpu.SEMAPHORE)
_ANY = pl.BlockSpec(memory_space=pl.ANY)
_SPLIT_PARAMS = dict(compiler_params=pltpu.CompilerParams(has_side_effects=pltpu.SideEffectType.DATAFLOW_SIDE_EFFECTING))
ICI_SAME_CORE = (2, 4, 6)
ALL_PEERS = (1, 2, 3, 4, 5, 6, 7)
LAND_SLOTS = {"gather": N_DEV, "scatter": N_DEV - 1, "pair": 4, "scatter_core": 3}


def _mesh_pos():
    x, y, c = lax.axis_index("x"), lax.axis_index("y"), lax.axis_index("c")
    return x, y, c, 4 * x + 2 * y + c


def _peer_of(k, x, y, c):
    px = 1 - x if k & 4 else x
    py = 1 - y if k & 2 else y
    pc = 1 - c if k & 1 else c
    return (px, py, pc), 4 * px + 2 * py + pc


def _split_copies(mode, ks, srcs, lands, send_sems, recv_sems):
    x, y, c, me = _mesh_pos()
    pairs = []
    for a in range(len(lands)):
        for j, k in enumerate(ks):
            dev, peer = _peer_of(k, x, y, c)
            i = a * len(ks) + j
            if mode == "gather":
                s_out, d_out, d_in = lands[a].at[me], lands[a].at[me], lands[a].at[peer]
            elif mode == "scatter":
                s_out, d_out, d_in = srcs[a].at[peer], lands[a].at[k - 1], lands[a].at[k - 1]
            elif mode == "pair":
                dev, _ = _peer_of(1, x, y, c)
                _, theirs = _peer_of(k | 1, x, y, c)
                s_out, d_out, d_in = srcs[a].at[theirs], lands[a].at[j], lands[a].at[j]
            elif mode == "scatter_core":
                s_out, d_out, d_in = srcs[a].at[j + 1], lands[a].at[j], lands[a].at[j]
            else:
                dev, _ = _peer_of(1, x, y, c)
                _, theirs = _peer_of(k | 1, x, y, c)
                s_out, d_out, d_in = lands[a].at[peer], lands[a].at[peer], lands[a].at[theirs]
            both = [pltpu.make_async_remote_copy(src_ref=s_out, dst_ref=d, send_sem=send_sems.at[i], recv_sem=recv_sems.at[i],
                                                 device_id=dev, device_id_type=pl.DeviceIdType.MESH) for d in (d_out, d_in)]
            pairs.append(tuple(both))
    return pairs


def split_start(name, mode, ks, srcs, lands=None, after=None):
    n, nk = len(srcs) if lands is None else len(lands), len(ks)
    srcs = [pltpu.with_memory_space_constraint(s, pltpu.HBM) for s in srcs]
    if lands is None:
        shapes = [((N_DEV,) + s.shape) if mode == "gather" else ((LAND_SLOTS[mode],) + s.shape[1:]) for s in srcs]
        lands = [lax.empty(shp, s.dtype) for shp, s in zip(shapes, srcs)]
    lands = [pltpu.with_memory_space_constraint(l, pltpu.HBM) for l in lands]
    both = srcs + lands
    extra = [] if after is None else [after]

    def body(*refs):
        src_refs, land_refs = refs[:len(srcs)], refs[len(srcs):len(both)]
        send_sems, recv_sems = refs[len(both) + len(extra)], refs[len(both) + len(extra) + 1]
        token = refs[-1]
        for out, _ in _split_copies(mode, ks, src_refs, land_refs, send_sems, recv_sems):
            out.start()
        token[...] = jnp.zeros_like(token)

    out_shape = (pltpu.SemaphoreType.DMA((n * nk,)), pltpu.SemaphoreType.DMA((n * nk,)),
                 *[pltpu.HBM(a.shape, a.dtype) for a in both], _sds((8, 128), F32))
    res = _raw_call(body, name=name, out_shape=out_shape, in_specs=[_HBM] * len(both) + [_ANY] * len(extra),
                    out_specs=(_SEM, _SEM, *[_HBM] * len(both), pl.BlockSpec(memory_space=pltpu.VMEM)),
                    input_output_aliases={i: 2 + i for i in range(len(both))}, **_SPLIT_PARAMS)(*both, *extra)
    _PENDING.append(res[-1])
    return dict(mode=mode, ks=ks, sems=(res[0], res[1]), srcs=list(res[2:2 + len(srcs)]),
                lands=list(res[2 + len(srcs):2 + len(both)]), token=res[-1])


def split_wait(name, h, after):
    ns = len(h["srcs"])
    both = h["srcs"] + h["lands"]
    after = list(after) if isinstance(after, (list, tuple)) else [after]

    def body(*refs):
        src_refs, land_refs = refs[:ns], refs[ns:len(both)]
        send_sems, recv_sems = refs[len(both)], refs[len(both) + 1]
        for out, arriving in _split_copies(h["mode"], h["ks"], src_refs, land_refs, send_sems, recv_sems):
            out.wait_send()
            arriving.wait_recv()

    res = _raw_call(body, name=name, out_shape=tuple(pltpu.HBM(a.shape, a.dtype) for a in both),
                    in_specs=[_HBM] * len(both) + [_SEM, _SEM] + [_ANY] * len(after), out_specs=tuple([_HBM] * len(both)),
                    input_output_aliases={i: i for i in range(len(both))}, **_SPLIT_PARAMS)(*both, *h["sems"], *after)
    return list(res[:ns]), list(res[ns:])


FORWARD_BLOCKS = (0, 2, 4, 6)


def pair_sum(name, mine, slots, theirs):
    P, R, C = theirs.shape
    tc = 512

    def body(s_ref, a_ref, b_ref, o_ref):
        o_ref[...] = (a_ref[...].astype(F32) + b_ref[...].astype(F32)).astype(o_ref.dtype)

    blk = pl.BlockSpec((None, R, tc), lambda p, i, s: (p, 0, i))
    spec = pltpu.PrefetchScalarGridSpec(num_scalar_prefetch=1, grid=(P, C // tc),
                                        in_specs=[pl.BlockSpec((None, R, tc), lambda p, i, s: (s[p], 0, i)), blk], out_specs=blk)
    return _pcall(body, name=name, grid_spec=spec, out_shape=_sds((P, R, C), theirs.dtype),
                  compiler_params=_cparams(("arbitrary", "arbitrary")))(slots, mine, theirs)


def _rope_parts(pos_ref, inv_ref):
    ang = pos_ref[...] * inv_ref[...]
    return jnp.cos(ang), jnp.sin(ang)


def _rot_half(t):
    lane = lax.broadcasted_iota(jnp.int32, t.shape, 1)
    return jnp.where((lane % HEAD_DIM) < HEAD_DIM // 2, -pltpu.roll(t, 128 - HEAD_DIM // 2, 1), pltpu.roll(t, HEAD_DIM // 2, 1))


def _attn_mask(n):
    row = lax.broadcasted_iota(jnp.int32, (BLK, 2 * BLK), 0)
    col = lax.broadcasted_iota(jnp.int32, (BLK, 2 * BLK), 1)
    dist = row + BLK - col
    return (dist >= 0) & (dist < BLK) & ((n * BLK - BLK + col) >= 0)


def _attn_specs(T):
    prev = lambda n: jnp.maximum(n - 1, 0)
    kc = pl.BlockSpec((BLK, KV_DIM), lambda n: (n, OFF_K // KV_DIM))
    kp = pl.BlockSpec((BLK, KV_DIM), lambda n: (prev(n), OFF_K // KV_DIM))
    vc = pl.BlockSpec((BLK, KV_DIM), lambda n: (n, OFF_V // KV_DIM))
    vp = pl.BlockSpec((BLK, KV_DIM), lambda n: (prev(n), OFF_V // KV_DIM))
    pc = pl.BlockSpec((BLK, 1), lambda n: (n, 0))
    pp = pl.BlockSpec((BLK, 1), lambda n: (prev(n), 0))
    inv = pl.BlockSpec((1, 128), lambda n: (0, 0))
    sink = pl.BlockSpec(memory_space=pltpu.SMEM)
    return kc, kp, vc, vp, pc, pp, inv, sink


def _softmax_sink(sc, valid, sink):
    sc = jnp.where(valid, sc * (HEAD_DIM ** -0.5), -1e30)
    m = jnp.maximum(jnp.max(sc, axis=1, keepdims=True), sink)
    e = jnp.exp(sc - m)
    es = jnp.exp(sink - m)
    den = jnp.sum(e, axis=1, keepdims=True) + es
    return e / den, es / den


def attn_fwd(projp, posf, inv128, sinks):
    T = projp.shape[0]
    kc, kp, vc, vp, pc, pp, inv, sink = _attn_specs(T)

    def body(q_ref, kc_ref, kp_ref, vc_ref, vp_ref, pc_ref, pp_ref, inv_ref, sink_ref, o_ref, qr_ref, kr_ref):
        n = pl.program_id(0)
        cos_c, sin_c = _rope_parts(pc_ref, inv_ref)
        cos_p, sin_p = _rope_parts(pp_ref, inv_ref)
        valid = _attn_mask(n)
        k_c, k_p = [], []
        for s in range(KV_DIM // 128):
            t = kc_ref[:, 128 * s:128 * (s + 1)]
            k_c.append((t * cos_c + _rot_half(t) * sin_c).astype(BF16))
            kr_ref[:, 128 * s:128 * (s + 1)] = k_c[s]
            t = kp_ref[:, 128 * s:128 * (s + 1)]
            k_p.append((t * cos_p + _rot_half(t) * sin_p).astype(BF16))
        kcat, vcat = [], []
        for hk in range(KV_HEADS):
            lo = HEAD_DIM * (hk % 2)
            kcat.append(jnp.concatenate([k_p[hk // 2][:, lo:lo + HEAD_DIM], k_c[hk // 2][:, lo:lo + HEAD_DIM]], axis=0))
            vcat.append(jnp.concatenate([vp_ref[:, HEAD_DIM * hk:HEAD_DIM * (hk + 1)], vc_ref[:, HEAD_DIM * hk:HEAD_DIM * (hk + 1)]], axis=0)
                        .astype(BF16))
        q_heads = []
        for s in range(Q_DIM // 128):
            t = q_ref[:, 128 * s:128 * (s + 1)]
            qs = (t * cos_c + _rot_half(t) * sin_c).astype(BF16)
            qr_ref[:, 128 * s:128 * (s + 1)] = qs
            q_heads += [qs[:, :HEAD_DIM], qs[:, HEAD_DIM:]]
        G = ATTN_HEADS // KV_HEADS
        scores = [_dot(q_heads[hq], kcat[hq // G], NT) for hq in range(ATTN_HEADS)]
        probs = [_softmax_sink(scores[hq], valid, sink_ref[0, hq])[0] for hq in range(ATTN_HEADS)]
        outs = [_dot(probs[hq], vcat[hq // G], NN) for hq in range(ATTN_HEADS)]
        for s in range(Q_DIM // 128):
            o_ref[:, 128 * s:128 * (s + 1)] = jnp.concatenate([outs[2 * s], outs[2 * s + 1]], axis=1).astype(BF16)

    qspec = pl.BlockSpec((BLK, Q_DIM), lambda n: (n, OFF_Q // Q_DIM))
    orow = pl.BlockSpec((BLK, Q_DIM), lambda n: (n, 0))
    krow = pl.BlockSpec((BLK, KV_DIM), lambda n: (n, 0))
    return _pcall(body, name="attn_fwd", grid=(T // BLK,), in_specs=[qspec, kc, kp, vc, vp, pc, pp, inv, sink],
                  out_specs=[orow, orow, krow], out_shape=[_sds((T, Q_DIM), BF16), _sds((T, Q_DIM), BF16), _sds((T, KV_DIM), BF16)],
                  compiler_params=_cparams(("arbitrary",)))(projp, projp, projp, projp, projp, posf, posf, inv128, sinks)


def attn_bwd(qr, kr, projp, dattn, posf, inv128, sinks):
    T = projp.shape[0]
    _, _, vc, vp, pc, pp, inv, sink = _attn_specs(T)
    G = ATTN_HEADS // KV_HEADS

    def body(qr_ref, krc_ref, krp_ref, vc_ref, vp_ref, do_ref, pc_ref, pp_ref, inv_ref, sink_ref, dq_ref, dk_ref, dv_ref, dsk_ref):
        n = pl.program_id(0)

        @pl.when(n == 0)
        def _():
            dk_ref[...] = jnp.zeros_like(dk_ref)
            dv_ref[...] = jnp.zeros_like(dv_ref)
            dsk_ref[...] = jnp.zeros_like(dsk_ref)

        cos_c, sin_c = _rope_parts(pc_ref, inv_ref)
        cos_p, sin_p = _rope_parts(pp_ref, inv_ref)
        valid = _attn_mask(n)
        lane = lax.broadcasted_iota(jnp.int32, (1, 128), 1)
        kcat, vcat = [], []
        for hk in range(KV_HEADS):
            ksl = slice(HEAD_DIM * hk, HEAD_DIM * (hk + 1))
            kcat.append(jnp.concatenate([krp_ref[:, ksl], krc_ref[:, ksl]], axis=0))
            vcat.append(jnp.concatenate([vp_ref[:, ksl], vc_ref[:, ksl]], axis=0).astype(BF16))
        H = range(ATTN_HEADS)
        q_heads = [qr_ref[:, HEAD_DIM * hq:HEAD_DIM * (hq + 1)] for hq in H]
        do_heads = [do_ref[:, HEAD_DIM * hq:HEAD_DIM * (hq + 1)] for hq in H]
        soft = [_softmax_sink(_dot(q_heads[hq], kcat[hq // G], NT), valid, sink_ref[0, hq]) for hq in H]
        dps = [_dot(do_heads[hq], vcat[hq // G], NT) for hq in H]
        deltas = [jnp.sum(soft[hq][0] * dps[hq], axis=1, keepdims=True) for hq in H]
        dss = [(soft[hq][0] * (dps[hq] - deltas[hq]) * (HEAD_DIM ** -0.5)).astype(BF16) for hq in H]
        pbs = [soft[hq][0].astype(BF16) for hq in H]
        dsk = jnp.zeros((1, 128), F32)
        for hq in H:
            dsk = dsk + jnp.where(lane == hq, -jnp.sum(soft[hq][1] * deltas[hq], axis=0, keepdims=True), 0.0)
        dsk_ref[...] += dsk
        dq_heads = [_dot(dss[hq], kcat[hq // G], NN) for hq in H]
        dk_parts = [_dot(dss[hq], q_heads[hq], TN) for hq in H]
        dv_parts = [_dot(pbs[hq], do_heads[hq], TN) for hq in H]
        dk_heads = [sum(dk_parts[G * hk + 1:G * (hk + 1)], dk_parts[G * hk]) for hk in range(KV_HEADS)]
        dv_heads = [sum(dv_parts[G * hk + 1:G * (hk + 1)], dv_parts[G * hk]) for hk in range(KV_HEADS)]
        for s in range(Q_DIM // 128):
            t = jnp.concatenate([dq_heads[2 * s], dq_heads[2 * s + 1]], axis=1)
            dq_ref[:, 128 * s:128 * (s + 1)] = (t * cos_c - _rot_half(t) * sin_c).astype(BF16)
        cur = pl.ds(pl.multiple_of(n * BLK, BLK), BLK)
        prv = pl.ds(pl.multiple_of(jnp.maximum(n - 1, 0) * BLK, BLK), BLK)
        for s in range(KV_DIM // 128):
            tc = jnp.concatenate([dk_heads[2 * s][BLK:], dk_heads[2 * s + 1][BLK:]], axis=1)
            tp = jnp.concatenate([dk_heads[2 * s][:BLK], dk_heads[2 * s + 1][:BLK]], axis=1)
            cols = slice(128 * s, 128 * (s + 1))
            dk_ref[cur, cols] += tc * cos_c - _rot_half(tc) * sin_c
            dk_ref[prv, cols] += tp * cos_p - _rot_half(tp) * sin_p
            dv_ref[cur, cols] += jnp.concatenate([dv_heads[2 * s][BLK:], dv_heads[2 * s + 1][BLK:]], axis=1)
            dv_ref[prv, cols] += jnp.concatenate([dv_heads[2 * s][:BLK], dv_heads[2 * s + 1][:BLK]], axis=1)

    qrow = pl.BlockSpec((BLK, Q_DIM), lambda n: (n, 0))
    krc = pl.BlockSpec((BLK, KV_DIM), lambda n: (n, 0))
    krp = pl.BlockSpec((BLK, KV_DIM), lambda n: (jnp.maximum(n - 1, 0), 0))
    whole = pl.BlockSpec((T, KV_DIM), lambda n: (0, 0))
    return _pcall(body, name="attn_bwd", grid=(T // BLK,), in_specs=[qrow, krc, krp, vc, vp, qrow, pc, pp, inv, sink],
                  out_specs=[qrow, whole, whole, pl.BlockSpec((1, 128), lambda n: (0, 0))],
                  out_shape=[_sds((T, Q_DIM), BF16), _sds((T, KV_DIM), F32), _sds((T, KV_DIM), F32), _sds((1, 128), F32)],
                  compiler_params=_cparams(("arbitrary",)))(qr, kr, kr, projp, projp, dattn, posf, posf, inv128, sinks)


CONV_CB = 256


def _shift_down(x, s):
    row = lax.broadcasted_iota(jnp.int32, x.shape, 0)
    return jnp.where(row >= s, pltpu.roll(x, s, 0), 0.0)


def _shift_up(x, s):
    T = x.shape[0]
    row = lax.broadcasted_iota(jnp.int32, x.shape, 0)
    return jnp.where(row < T - s, pltpu.roll(x, T - s, 0), 0.0)


def _conv_pre(x, w_ref, b_ref):
    acc = x * w_ref[CONV_WIDTH - 1:CONV_WIDTH, :] + b_ref[...]
    for s in range(1, CONV_WIDTH):
        acc = acc + _shift_down(x, s) * w_ref[CONV_WIDTH - 1 - s:CONV_WIDTH - s, :]
    return acc


def conv_fwd(projp, conv_w, conv_b):
    T = projp.shape[0]

    def body(x_ref, w_ref, b_ref, o_ref):
        o_ref[...] = _silu(_conv_pre(x_ref[...], w_ref, b_ref))

    return _pcall(body, name="conv_fwd", grid=(CONV_DIM // CONV_CB,),
                  in_specs=[pl.BlockSpec((T, CONV_CB), lambda c: (0, OFF_XBC // CONV_CB + c)),
                            pl.BlockSpec((CONV_WIDTH, CONV_CB), lambda c: (0, c)), pl.BlockSpec((1, CONV_CB), lambda c: (0, c))],
                  out_specs=pl.BlockSpec((T, CONV_CB), lambda c: (0, c)), out_shape=_sds((T, CONV_DIM), F32),
                  compiler_params=_cparams(("arbitrary",)))(projp, conv_w, conv_b)


def conv_bwd(name, projp, dact, conv_w, conv_b, col0):
    T, C = dact.shape
    c0 = col0 // CONV_CB

    def body(x_ref, da_ref, w_ref, b_ref, dx_ref, dw_ref, db_ref):
        x = x_ref[...]
        dpre = da_ref[...] * _dsilu(_conv_pre(x, w_ref, b_ref))
        dx = dpre * w_ref[CONV_WIDTH - 1:CONV_WIDTH, :]
        dw_ref[CONV_WIDTH - 1:CONV_WIDTH, :] = jnp.sum(dpre * x, axis=0, keepdims=True)
        for s in range(1, CONV_WIDTH):
            i = CONV_WIDTH - 1 - s
            dx = dx + _shift_up(dpre, s) * w_ref[i:i + 1, :]
            dw_ref[i:i + 1, :] = jnp.sum(dpre * _shift_down(x, s), axis=0, keepdims=True)
        dx_ref[...] = dx.astype(BF16)
        db_ref[...] = jnp.sum(dpre, axis=0, keepdims=True)

    return _pcall(body, name=name, grid=(C // CONV_CB,),
                  in_specs=[pl.BlockSpec((T, CONV_CB), lambda c: (0, OFF_XBC // CONV_CB + c0 + c)),
                            pl.BlockSpec((T, CONV_CB), lambda c: (0, c)),
                            pl.BlockSpec((CONV_WIDTH, CONV_CB), lambda c: (0, c0 + c)), pl.BlockSpec((1, CONV_CB), lambda c: (0, c0 + c))],
                  out_specs=[pl.BlockSpec((T, CONV_CB), lambda c: (0, c)), pl.BlockSpec((CONV_WIDTH, CONV_CB), lambda c: (0, c)),
                             pl.BlockSpec((1, CONV_CB), lambda c: (0, c))],
                  out_shape=[_sds((T, C), BF16), _sds((CONV_WIDTH, C), F32), _sds((1, C), F32)],
                  compiler_params=_cparams(("arbitrary",)))(projp, dact, conv_w, conv_b)


def _softplus(x):
    return jnp.maximum(x, 0.0) + jnp.log1p(jnp.exp(-jnp.abs(x)))


def _tri(lower):
    r = lax.broadcasted_iota(jnp.int32, (BLK, BLK), 0)
    c = lax.broadcasted_iota(jnp.int32, (BLK, BLK), 1)
    return (r >= c) if lower else (c >= r)


def _ssd_chunk_setup(dt_ref, dtb_ref, alog_ref):
    raw = dt_ref[...] + dtb_ref[...]
    dt = _softplus(raw)
    aneg = -jnp.exp(alog_ref[...])
    a = dt * aneg
    acs = jnp.dot(_tri(True).astype(F32), a, precision=lax.Precision.HIGHEST, preferred_element_type=F32)
    return raw, dt, aneg, acs, acs.T


def _ssd_specs(T, rev):
    nc = T // BLK
    ci = (lambda c: nc - 1 - c) if rev else (lambda c: c)
    xs = pl.BlockSpec((BLK, D_INNER), lambda c: (ci(c), 0))
    bm = pl.BlockSpec((BLK, SSM_GROUPS * D_STATE), lambda c: (ci(c), D_INNER // (SSM_GROUPS * D_STATE)))
    cm = pl.BlockSpec((BLK, SSM_GROUPS * D_STATE), lambda c: (ci(c), D_INNER // (SSM_GROUPS * D_STATE) + 1))
    dt = pl.BlockSpec((BLK, DT_PAD), lambda c: (ci(c), OFF_DT // DT_PAD))
    v128 = pl.BlockSpec((1, 128), lambda c: (0, 0))
    dfull = pl.BlockSpec((1, D_INNER), lambda c: (0, 0))
    st = pl.BlockSpec((None, SSM_HEADS, HEAD_DIM, D_STATE), lambda c: (ci(c), 0, 0, 0))
    return xs, bm, cm, dt, v128, dfull, st, ci


GW = HEADS_PER_GROUP * HEAD_DIM


def _expanders():
    e = np.zeros((SSM_GROUPS, 128, GW), np.float32)
    for g in range(SSM_GROUPS):
        for hh in range(HEADS_PER_GROUP):
            e[g, HEADS_PER_GROUP * g + hh, HEAD_DIM * hh:HEAD_DIM * (hh + 1)] = 1.0
    return jnp.asarray(e), jnp.asarray(np.transpose(e, (0, 2, 1)).copy())


def _dotx(a, b):
    return jnp.dot(a, b, precision=lax.Precision.HIGHEST, preferred_element_type=F32)


def _decay(acs, acsT, h, tril):
    return jnp.where(tril, jnp.exp(jnp.where(tril, acs[:, h:h + 1] - acsT[h:h + 1, :], 0.0)), 0.0)


def ssd_fwd(xbc, projp, dtb, alog, dfull):
    T = xbc.shape[0]
    nc = T // BLK
    xs, bm, cm, dts, v128, dfs, st, _ = _ssd_specs(T, False)
    E, _ = _expanders()

    def body(xs_ref, b_ref, c_ref, dt_ref, dtb_ref, alog_ref, d_ref, e_ref, y_ref, st_ref, h_scr):
        c = pl.program_id(0)

        @pl.when(c == 0)
        def _():
            h_scr[...] = jnp.zeros_like(h_scr)

        _, dt, _, acs, acsT = _ssd_chunk_setup(dt_ref, dtb_ref, alog_ref)
        tril = _tri(True)
        alast = acs[BLK - 1:BLK, :]
        eacs = jnp.exp(acs)
        wmat = jnp.exp(alast - acs)
        gam = jnp.exp(alast)
        for g in range(SSM_GROUPS):
            gl = slice(GW * g, GW * (g + 1))
            hsl = slice(HEADS_PER_GROUP * g, HEADS_PER_GROUP * (g + 1))
            heads = [HEADS_PER_GROUP * g + hh for hh in range(HEADS_PER_GROUP)]
            Eg = e_ref[g]
            B = b_ref[:, D_STATE * g:D_STATE * (g + 1)].astype(BF16)
            C = c_ref[:, D_STATE * g:D_STATE * (g + 1)].astype(BF16)
            cb = _dot(C, B, NT)
            x_g = xs_ref[:, gl]
            xd_g = x_g * _dotx(dt, Eg)
            hold = h_scr[hsl]
            st_ref[hsl] = hold
            hcat = hold.reshape(GW, D_STATE)
            yoff = _dotx(eacs, Eg) * _dot(C, hcat, NT)
            S = _dot(xd_g * _dotx(wmat, Eg), B, TN)
            Ms = [cb * _decay(acs, acsT, h, tril) for h in heads]
            ys = [_dot(Ms[hh], xd_g[:, HEAD_DIM * hh:HEAD_DIM * (hh + 1)], NN) for hh in range(HEADS_PER_GROUP)]
            for hh, h in enumerate(heads):
                h_scr[h] = gam[:, h:h + 1] * hold[hh] + S[HEAD_DIM * hh:HEAD_DIM * (hh + 1)]
            y_ref[:, gl] = jnp.concatenate(ys, axis=1) + yoff + d_ref[:, gl] * x_g

    espec = pl.BlockSpec((SSM_GROUPS, 128, GW), lambda c: (0, 0, 0))
    return _pcall(body, name="ssd_fwd", grid=(nc,), in_specs=[xs, bm, cm, dts, v128, v128, dfs, espec],
                  out_specs=[xs, st], out_shape=[_sds((T, D_INNER), F32), _sds((nc, SSM_HEADS, HEAD_DIM, D_STATE), F32)],
                  scratch_shapes=[pltpu.VMEM((SSM_HEADS, HEAD_DIM, D_STATE), F32)],
                  compiler_params=_cparams(("arbitrary",)))(xbc, xbc, xbc, projp, dtb, alog, dfull, E)


def ssd_bwd(xbc, projp, dtb, alog, dfull, states, dy):
    T = xbc.shape[0]
    nc = T // BLK
    xs, bm, cm, dts, v128, dfs, st, ci = _ssd_specs(T, True)
    gn = SSM_GROUPS * D_STATE
    E, ET = _expanders()

    def body(xs_ref, b_ref, c_ref, dt_ref, dtb_ref, alog_ref, d_ref, st_ref, dy_ref, e_ref, et_ref,
             dxs_ref, dB_ref, dC_ref, ddt_ref, dal_ref, dD_ref, ddtb_ref, dh_scr):
        i = pl.program_id(0)

        @pl.when(i == 0)
        def _():
            dh_scr[...] = jnp.zeros_like(dh_scr)
            dal_ref[...] = jnp.zeros_like(dal_ref)
            dD_ref[...] = jnp.zeros_like(dD_ref)
            ddtb_ref[...] = jnp.zeros_like(ddtb_ref)

        raw, dt, aneg, acs, acsT = _ssd_chunk_setup(dt_ref, dtb_ref, alog_ref)
        tril = _tri(True)
        lane = lax.broadcasted_iota(jnp.int32, (BLK, 128), 1)
        sub = lax.broadcasted_iota(jnp.int32, (BLK, 128), 0)
        alast = acs[BLK - 1:BLK, :]
        eacs = jnp.exp(acs)
        wmat = jnp.exp(alast - acs)
        gam = jnp.exp(alast)
        gcol = jnp.exp(acsT[:, BLK - 1:BLK])
        ds_col = jnp.zeros((BLK, 128), F32)
        ds_row = jnp.zeros((BLK, 128), F32)
        ddt_col = jnp.zeros((BLK, 128), F32)
        dDm = jnp.zeros((BLK, 128), F32)
        hl = [slice(HEAD_DIM * hh, HEAD_DIM * (hh + 1)) for hh in range(HEADS_PER_GROUP)]
        for g in range(SSM_GROUPS):
            gl = slice(GW * g, GW * (g + 1))
            gs = slice(D_STATE * g, D_STATE * (g + 1))
            hsl = slice(HEADS_PER_GROUP * g, HEADS_PER_GROUP * (g + 1))
            heads = [HEADS_PER_GROUP * g + hh for hh in range(HEADS_PER_GROUP)]
            Eg, ETg = e_ref[g], et_ref[g]
            B = b_ref[:, gs].astype(BF16)
            C = c_ref[:, gs].astype(BF16)
            cb = _dot(C, B, NT)
            x_g, dy_g = xs_ref[:, gl], dy_ref[:, gl]
            dt_x, w_x = _dotx(dt, Eg), _dotx(wmat, Eg)
            xd_g = x_g * dt_x
            dye = dy_g * _dotx(eacs, Eg)
            hcat = st_ref[hsl].reshape(GW, D_STATE)
            dSv = dh_scr[hsl]
            dScat = dSv.reshape(GW, D_STATE)
            dDm = dDm + _dotx(dy_g * x_g, ETg)
            dH_y = _dot(dye, C, TN)
            dC_g = _dot(dye, hcat, NN)
            ds_col = ds_col + _dotx(dye * _dot(C, hcat, NT), ETg)
            dxdw = _dot(B, dScat, NT)
            dB_g = _dot(xd_g * w_x, dScat, NN)
            dww = _dotx(xd_g * dxdw, ETg) * wmat
            ds_col = ds_col - dww + jnp.where(sub == BLK - 1, jnp.sum(dww, axis=0, keepdims=True), 0.0)
            hd = jnp.sum(_dotx(Eg, dScat * hcat), axis=1, keepdims=True) * gcol
            ds_row = ds_row - jnp.where(lane == BLK - 1, hd, 0.0)
            decays = [_decay(acs, acsT, h, tril) for h in heads]
            Ms = [cb * d for d in decays]
            dMs = [_dot(dy_g[:, hl[hh]], xd_g[:, hl[hh]], NT) for hh in range(HEADS_PER_GROUP)]
            dxd1 = [_dot(Ms[hh], dy_g[:, hl[hh]], TN) for hh in range(HEADS_PER_GROUP)]
            dG = jnp.zeros((BLK, BLK), F32)
            for hh, h in enumerate(heads):
                Q = dMs[hh] * Ms[hh]
                ds_col = ds_col + jnp.where(lane == h, jnp.sum(Q, axis=1, keepdims=True), 0.0)
                ds_row = ds_row + jnp.where(sub == h, jnp.sum(Q, axis=0, keepdims=True), 0.0)
                dG = dG + dMs[hh] * decays[hh]
            dxd_g = jnp.concatenate(dxd1, axis=1) + w_x * dxdw
            dxs_ref[:, gl] = d_ref[:, gl] * dy_g + dxd_g * dt_x
            ddt_col = ddt_col + _dotx(dxd_g * x_g, ETg)
            dC_ref[:, gs] = dC_g + _dot(dG, B, NN)
            dB_ref[:, gs] = dB_g + _dot(dG, C, TN)
            for hh, h in enumerate(heads):
                dh_scr[h] = gam[:, h:h + 1] * dSv[hh] + dH_y[hl[hh]]
        ds_all = ds_col - ds_row.T
        da = jnp.dot(_tri(False).astype(F32), ds_all, precision=lax.Precision.HIGHEST, preferred_element_type=F32)
        ddt = ddt_col + da * aneg
        draw = jnp.where(lane < SSM_HEADS, ddt * _sigmoid(raw), 0.0)
        ddt_ref[...] = draw.astype(BF16)
        dal_ref[...] += jnp.sum(da * dt, axis=0, keepdims=True) * aneg
        ddtb_ref[...] += jnp.sum(draw, axis=0, keepdims=True)
        dD_ref[...] += jnp.sum(dDm, axis=0, keepdims=True)

    gblk = pl.BlockSpec((BLK, gn), lambda c: (ci(c), 0))
    espec = pl.BlockSpec((SSM_GROUPS, 128, GW), lambda c: (0, 0, 0))
    etspec = pl.BlockSpec((SSM_GROUPS, GW, 128), lambda c: (0, 0, 0))
    return _pcall(body, name="ssd_bwd", grid=(nc,), in_specs=[xs, bm, cm, dts, v128, v128, dfs, st, xs, espec, etspec],
                  out_specs=[xs, gblk, gblk, pl.BlockSpec((BLK, DT_PAD), lambda c: (ci(c), 0)), v128, v128, v128],
                  out_shape=[_sds((T, D_INNER), F32), _sds((T, gn), F32), _sds((T, gn), F32), _sds((T, DT_PAD), BF16),
                             _sds((1, 128), F32), _sds((1, 128), F32), _sds((1, 128), F32)],
                  scratch_shapes=[pltpu.VMEM((SSM_HEADS, HEAD_DIM, D_STATE), F32)],
                  compiler_params=_cparams(("arbitrary",)))(xbc, xbc, xbc, projp, dtb, alog, dfull, states, dy, E, ET)


_WIN_ORDER = ("z", "ga", "gs", "xbc", "q", "k", "v", "dt")


def _win_to_padded(win_g):
    full = win_g.reshape(IN_DIM, D_MODEL)
    rows = []
    for nm in _WIN_ORDER:
        s, w = SEG[nm]
        rows.append(full[s:s + w])
    rows.append(jnp.zeros((DT_PAD - SEG["dt"][1], D_MODEL), win_g.dtype))
    return jnp.concatenate(rows, axis=0)


def _padded_to_win(dw):
    off = dict(z=OFF_Z, ga=OFF_GA, gs=OFF_GS, xbc=OFF_XBC, q=OFF_Q, k=OFF_K, v=OFF_V, dt=OFF_DT)
    per = IN_DIM // N_DEV
    blocks = []
    for j in range(N_DEV):
        lo, hi, rows = j * per, (j + 1) * per, []
        for nm in ("q", "k", "v", "z", "xbc", "dt", "ga", "gs"):
            s, w = SEG[nm]
            a, b = max(lo, s), min(hi, s + w)
            if a < b:
                rows.append(dw[off[nm] + a - s:off[nm] + b - s])
        blocks.append(jnp.concatenate(rows, axis=0))
    return jnp.stack(blocks)


def _pad128(v):
    return jnp.pad(v, ((0, 0), (0, 128 - v.shape[1])))


_SMALL = (("loss", 128, 1), ("g_mix", 2048, 2048), ("conv_b", 3072, 3072), ("dt_bias", 128, 32), ("a_log", 128, 32),
          ("d_skip", 128, 32), ("g_ssd", 2048, 2048), ("sinks", 128, 16), ("g_ffn", 2048, 2048), ("g_ple", 2048, 2048),
          ("g_final", 2048, 2048))


def _small_vec(d):
    parts = []
    for nm, pw, w in _SMALL:
        v = d[nm].reshape(1, -1).astype(F32)
        parts.append(jnp.pad(v[:, :min(v.shape[1], pw)], ((0, 0), (0, pw - min(v.shape[1], pw)))))
    return jnp.concatenate(parts, axis=1)


def _small_split(vec):
    out, o = {}, 0
    for nm, pw, w in _SMALL:
        out[nm] = vec[0, o:o + w]
        o += pw
    return out


def kernel(x, p, positions, g_mix, w_in, conv_w, conv_b, dt_bias, a_log, d_skip, g_ssd, sinks, w_attn_br, w_ssd_br, w_o, g_ffn, w_gate, w_up, w_down, g_ple, w_ple_gate, w_ple_proj, g_final, loss_target, m_g_mix, m_w_in, m_conv_w, m_conv_b, m_dt_bias, m_a_log, m_d_skip, m_g_ssd, m_sinks, m_w_attn_br, m_w_ssd_br, m_w_o, m_g_ffn, m_w_gate, m_w_up, m_w_down, m_g_ple, m_w_ple_gate, m_w_ple_proj, m_g_final, v_g_mix, v_w_in, v_conv_w, v_conv_b, v_dt_bias, v_a_log, v_d_skip, v_g_ssd, v_sinks, v_w_attn_br, v_w_ssd_br, v_w_o, v_g_ffn, v_w_gate, v_w_up, v_w_down, v_g_ple, v_w_ple_gate, v_w_ple_proj, v_g_final):
    T = x.shape[1]
    D = D_MODEL
    W = dict(g_mix=g_mix, w_in=w_in, conv_w=conv_w, conv_b=conv_b, dt_bias=dt_bias, a_log=a_log, d_skip=d_skip, g_ssd=g_ssd,
             sinks=sinks, w_attn_br=w_attn_br, w_ssd_br=w_ssd_br, w_o=w_o, g_ffn=g_ffn, w_gate=w_gate, w_up=w_up, w_down=w_down,
             g_ple=g_ple, w_ple_gate=w_ple_gate, w_ple_proj=w_ple_proj, g_final=g_final)
    Mo = dict(g_mix=m_g_mix, w_in=m_w_in, conv_w=m_conv_w, conv_b=m_conv_b, dt_bias=m_dt_bias, a_log=m_a_log, d_skip=m_d_skip,
              g_ssd=m_g_ssd, sinks=m_sinks, w_attn_br=m_w_attn_br, w_ssd_br=m_w_ssd_br, w_o=m_w_o, g_ffn=m_g_ffn, w_gate=m_w_gate,
              w_up=m_w_up, w_down=m_w_down, g_ple=m_g_ple, w_ple_gate=m_w_ple_gate, w_ple_proj=m_w_ple_proj, g_final=m_g_final)
    Vo = dict(g_mix=v_g_mix, w_in=v_w_in, conv_w=v_conv_w, conv_b=v_conv_b, dt_bias=v_dt_bias, a_log=v_a_log, d_skip=v_d_skip,
              g_ssd=v_g_ssd, sinks=v_sinks, w_attn_br=v_w_attn_br, w_ssd_br=v_w_ssd_br, w_o=v_w_o, g_ffn=v_g_ffn, w_gate=v_w_gate,
              w_up=v_w_up, w_down=v_w_down, g_ple=v_g_ple, w_ple_gate=v_w_ple_gate, w_ple_proj=v_w_ple_proj, g_final=v_g_final)
    order = ["g_mix", "w_in", "conv_w", "conv_b", "dt_bias", "a_log", "d_skip", "g_ssd", "sinks", "w_attn_br", "w_ssd_br", "w_o",
             "g_ffn", "w_gate", "w_up", "w_down", "g_ple", "w_ple_gate", "w_ple_proj", "g_final"]
    big = ["w_in", "conv_w", "w_attn_br", "w_ssd_br", "w_o", "w_gate", "w_up", "w_down", "w_ple_gate", "w_ple_proj"]

    x2 = x.reshape(T, D)
    p2 = p.reshape(T, PLE_DIM)
    tgt = loss_target.reshape(T, D)
    posf = positions.reshape(T, 1).astype(F32)
    inv = ROPE_THETA ** (-np.arange(HEAD_DIM // 2, dtype=np.float32) * 2.0 / HEAD_DIM)
    inv128 = jnp.asarray(np.tile(inv, 128 // (HEAD_DIM // 2)).reshape(1, 128).astype(np.float32))
    transposed = ("w_in",)

    def shard2d(a, n):
        a = a.reshape(a.shape[-2:])
        return a.T if n in transposed else a

    sh = {n: shard2d(W[n], n) for n in big}

    del _PENDING[:]
    me = 4 * lax.axis_index("x") + 2 * lax.axis_index("y") + lax.axis_index("c")
    groups = (("w_in",), ("conv_w", "w_attn_br", "w_ssd_br", "w_o"), ("w_gate", "w_up", "w_down"), ("w_ple_gate", "w_ple_proj"))
    send = {n: sh[n] if n == "conv_w" else sh[n].astype(BF16) for n in big}
    started, prev = [], None
    for gi, grp in enumerate(groups):
        zones = [lax.dynamic_update_index_in_dim(lax.empty((N_DEV,) + send[n].shape, send[n].dtype), send[n], me, 0) for n in grp]
        h = split_start("gather_start_%d" % gi, "gather", ICI_SAME_CORE, [], lands=zones, after=prev)
        prev = h["token"]
        started.append(h)
    gathered, fwd = {}, {}

    def forward_start(gi, after):
        _, lands = split_wait("gather_wait_%d" % gi, started[gi], after)
        fwd[gi] = split_start("forward_start_%d" % gi, "forward", FORWARD_BLOCKS, [], lands=lands)

    def forward_wait(gi, after):
        _, full = split_wait("forward_wait_%d" % gi, fwd[gi], after)
        gathered.update(zip(groups[gi], full))

    u = rms_fwd("norm_mix", x2, g_mix)
    forward_start(0, u)
    forward_wait(0, u)
    forward_start(1, u)
    winp = _win_to_padded(gathered["w_in"])
    dtb = _pad128(dt_bias)
    alog = _pad128(a_log)
    dfull = jnp.repeat(d_skip.reshape(SSM_HEADS), HEAD_DIM).reshape(1, D_INNER)

    projp = mm_nt("in_proj", u, winp, 640)
    attn, qr, kr = attn_fwd(projp, posf, inv128, sinks)
    forward_wait(1, attn)
    convw = jnp.transpose(gathered["conv_w"], (1, 0, 2)).reshape(CONV_WIDTH, CONV_DIM)
    wab = gathered["w_attn_br"]
    wsb = gathered["w_ssd_br"].reshape(D, D)
    wo = gathered["w_o"].reshape(D, D)
    xbc = conv_fwd(projp, convw, conv_b)
    y, states = ssd_fwd(xbc, projp, dtb, alog, dfull)
    yn = gnorm_fwd(y, projp, g_ssd)
    out_a = mm_nn_colblk("attn_br", attn, wab)
    out_s = mm_nn("ssd_br", yn, wsb, 512)
    forward_start(2, out_s)
    merged = merge_fwd(projp, out_a, out_s)
    h1 = mm_nn("o_proj", merged, wo, 512, residual=x2)
    f = rms_fwd("norm_ffn", h1, g_ffn)
    forward_wait(2, f)
    wg, wu, wd = gathered["w_gate"], gathered["w_up"], gathered["w_down"]
    gate, up, act = ffn_up(f, wg, wu)
    forward_start(3, act)
    h2 = ffn_down(act, wd, h1)
    r = rms_fwd("norm_ple", h2, g_ple)
    forward_wait(3, r)
    wpg = gathered["w_ple_gate"].reshape(D, D)
    wpp = gathered["w_ple_proj"]
    pg = mm_nn("ple_gate", r, wpg, 512)
    pp = mm_nn_colblk("ple_proj", p2, wpp)
    loss_v, dh3, dpg, dpp, dg_final = head_fwd_bwd(h2, pg, pp, g_final.reshape(1, D), tgt)

    gw = {}
    scat = []

    def scatter_start(names):
        scat.append((names, split_start("scatter_start_%d" % len(scat), "scatter", ALL_PEERS, [gw[n] for n in names])))

    gw["w_ple_proj"] = mm_tn_colblk("dw_ple_proj", p2, dpp, PLE_DIM)
    dr = mm_nt("d_ple_gate", dpg, wpg, 512)
    gw["w_ple_gate"] = mm_tn("dw_ple_gate", r, dpg, 512, 1024).reshape(N_DEV, D // N_DEV, D)
    scatter_start(("w_ple_proj", "w_ple_gate"))
    dh2, dh2b, dg_ple = rms_bwd("norm_ple_bwd", h2, g_ple, dr, dh3)
    dgate, dup = ffn_down_bwd(dh2b, wd, gate, up)
    gw["w_down"] = wgrad_rowblk_lhs("dw_down", act, dh2b, 1024)
    gw["w_gate"] = wgrad_colblk_rhs("dw_gate", f, dgate, 1024)
    gw["w_up"] = wgrad_colblk_rhs("dw_up", f, dup, 1024)
    scatter_start(("w_down", "w_gate", "w_up"))
    df = ffn_up_bwd(dgate, dup, wg, wu)
    dh1, dh1b, dg_ffn = rms_bwd("norm_ffn_bwd", h1, g_ffn, df, dh2)
    dmerged = mm_nt("d_o_proj", dh1b, wo, 512)
    gw["w_o"] = mm_tn("dw_o", merged, dh1b, 512, 1024).reshape(N_DEV, D // N_DEV, D)
    dout_a, dout_s, dga, dgs = merge_bwd(projp, out_a, out_s, dmerged)
    gw["w_ssd_br"] = mm_tn("dw_ssd_br", yn, dout_s, 512, 1024).reshape(N_DEV, D // N_DEV, D)
    gw["w_attn_br"] = mm_tn_colblk("dw_attn_br", attn, dout_a, D // N_DEV)
    scatter_start(("w_o", "w_ssd_br", "w_attn_br"))
    dyn = mm_nt("d_ssd_br", dout_s, wsb, 512)
    dattn = attn_br_bwd(dout_a, wab)
    dy, dz, dg_ssd = gnorm_bwd(y, projp, g_ssd, dyn)
    dxs, dbm, dcm, ddt, dal, ddsk, ddtb = ssd_bwd(xbc, projp, dtb, alog, dfull, states, dy)
    dx_x, dwc_x, dbc_x = conv_bwd("conv_bwd_x", projp, dxs, convw, conv_b, 0)
    dx_b, dwc_b, dbc_b = conv_bwd("conv_bwd_b", projp, dbm, convw, conv_b, D_INNER)
    dx_c, dwc_c, dbc_c = conv_bwd("conv_bwd_c", projp, dcm, convw, conv_b, D_INNER + SSM_GROUPS * D_STATE)
    dq, dk, dv, dsk = attn_bwd(qr, kr, projp, dattn, posf, inv128, sinks)
    dproj = jnp.concatenate([dz, dga, dgs, dx_x, dx_b, dx_c, dq, dk.astype(BF16), dv.astype(BF16), ddt], axis=1)
    gw_in = _padded_to_win(mm_tn("dw_in", dproj, u, 640, 1024))
    pair = split_start("pair_start", "pair", FORWARD_BLOCKS, [gw_in])
    dconvw = jnp.concatenate([dwc_x, dwc_b, dwc_c], axis=1)
    gw["conv_w"] = jnp.transpose(dconvw.reshape(CONV_WIDTH, N_DEV, CONV_DIM // N_DEV), (1, 0, 2))
    scatter_start(("conv_w",))
    du = mm_nn_red("d_in_proj", dproj, winp, 1024, 640)
    (gw_in,), (sibling_part,) = split_wait("pair_wait", pair, du)
    pair_slots = jnp.stack([jnp.bitwise_xor(me, k) for k in FORWARD_BLOCKS]).astype(jnp.int32)
    core = split_start("core_start", "scatter_core", ICI_SAME_CORE, [pair_sum("pair_sum_w_in", gw_in, pair_slots, sibling_part)])
    gx, _, dg_mix = rms_bwd("norm_mix_bwd", x2, g_mix, du, dh1)

    res = {}
    after = [gx]
    for si, (names, h) in enumerate(scat):
        srcs, lands = split_wait("scatter_wait_%d" % si, h, after)
        for n, mine, arrived in zip(names, srcs, lands):
            res[n] = adamw("adamw_" + n, arrived, sh[n], shard2d(Mo[n], n), shard2d(Vo[n], n), own=mine, own_slot=me)
        after = [res[n][0] for n in names]

    small_g = dict(loss=loss_v[:, :1], g_mix=dg_mix, conv_b=jnp.concatenate([dbc_x, dbc_b, dbc_c], axis=1), dt_bias=ddtb,
                   a_log=dal, d_skip=ddsk, g_ssd=dg_ssd, sinks=dsk, g_ffn=dg_ffn, g_ple=dg_ple, g_final=dg_final)
    zero = jnp.zeros((1, 1), F32)
    vec_parts = exchange("gather_small", [_small_vec(small_g)], "gather")[0]
    sres = adamw("adamw_small", vec_parts, _small_vec({**W, "loss": zero}), _small_vec({**Mo, "loss": zero}),
                 _small_vec({**Vo, "loss": zero}))
    ssplit = [_small_split(a) for a in sres]

    (pair_sums,), (arrived,) = split_wait("core_wait", core, [res[n][0] for n in res] + [sres[0]])
    res["w_in"] = adamw("adamw_w_in", arrived, sh["w_in"], shard2d(Mo["w_in"], "w_in"), shard2d(Vo["w_in"], "w_in"),
                        own=pair_sums, own_slot=0)
    loss = ssplit[0]["loss"].reshape(())
    for n in order:
        if n not in res:
            res[n] = tuple(s[n].reshape(W[n].shape) for s in ssplit)
        else:
            res[n] = tuple((a.T if n in transposed else a).reshape(W[n].shape) for a in res[n])
    outs = [loss, gx.reshape(x.shape)]
    for k in range(4):
        outs += [res[n][k] for n in order]
    return tuple(outs)
```

```python
import functools

import numpy as np
import jax
import jax.numpy as jnp
from jax import lax
from jax.experimental import pallas as pl
from jax.experimental.pallas import tpu as pltpu

F32 = jnp.float32
BF16 = jnp.bfloat16

N_DEV = 8
D_MODEL = 2048
HEAD_DIM = 64
ATTN_HEADS = 16
KV_HEADS = 4
Q_DIM = 1024
KV_DIM = 256
BLK = 128
D_INNER = 2048
SSM_HEADS = 32
SSM_GROUPS = 4
HEADS_PER_GROUP = 8
D_STATE = 128
CONV_WIDTH = 4
CONV_DIM = 3072
FFN_HIDDEN = 5632
PLE_DIM = 256
IN_DIM = 10784
NORM_EPS = 1e-6
SSM_NORM_EPS = 1e-5
ROPE_THETA = 10000.0

OFF_Z, OFF_GA, OFF_GS, OFF_XBC, OFF_Q, OFF_K, OFF_V, OFF_DT = 0, 2048, 4096, 6144, 9216, 10240, 10496, 10752
IN_PAD = 10880
DT_PAD = 128
SEG = dict(q=(0, 1024), k=(1024, 256), v=(1280, 256), z=(1536, 2048), xbc=(3584, 3072), dt=(6656, 32),
           ga=(6688, 2048), gs=(8736, 2048))

ADAM_LR, ADAM_B1, ADAM_B2, ADAM_EPS, ADAM_WD, ADAM_STEP = 0.001, 0.9, 0.999, 1e-08, 0.01, 10

VMEM_LIMIT = 56 * 1024 * 1024

NN = (((1,), (0,)), ((), ()))
NT = (((1,), (1,)), ((), ()))
TN = (((0,), (0,)), ((), ()))


_PENDING = []


def _raw_call(body, **kw):
    return pl.pallas_call(body, **kw)


def _pcall(body, **kw):
    if "in_specs" not in kw:
        return _raw_call(body, **kw)
    deps = list(_PENDING)
    del _PENDING[:]
    if not deps:
        return _raw_call(body, **kw)
    n_in = len(kw["in_specs"])

    def tied(*refs):
        return body(*refs[:n_in], *refs[n_in + len(deps):])

    kw["in_specs"] = list(kw["in_specs"]) + [pl.BlockSpec(memory_space=pl.ANY)] * len(deps)
    call = _raw_call(tied, **kw)
    return lambda *ops: call(*ops, *deps)


def _cparams(sem=None):
    if sem is None:
        return pltpu.CompilerParams(vmem_limit_bytes=VMEM_LIMIT)
    return pltpu.CompilerParams(vmem_limit_bytes=VMEM_LIMIT, dimension_semantics=sem)


def _dot(a, b, dn):
    return lax.dot_general(a.astype(BF16), b.astype(BF16), dn, preferred_element_type=F32)


def _sigmoid(x):
    return 1.0 / (1.0 + jnp.exp(-x))


def _silu(x):
    return x * _sigmoid(x)


def _dsilu(x):
    s = _sigmoid(x)
    return s * (1.0 + x * (1.0 - s))


def _matmul(name, pairs, pair_specs, dn, grid, out_shapes, out_specs, nred=1, extra=(), extra_specs=(),
            epilogue=None, acc_shape=None, alias=None):
    n_in = 2 * len(pairs) + len(extra)
    n_out = len(out_shapes)

    def body(*refs):
        ins = refs[:2 * len(pairs)]
        ex = [r for r, sp in zip(refs[2 * len(pairs):n_in], extra_specs) if sp.memory_space != pl.ANY]
        outs = refs[n_in:n_in + n_out]

        def prod():
            s = None
            for p in range(len(pairs)):
                d = _dot(ins[2 * p][...], ins[2 * p + 1][...], dn)
                s = d if s is None else s + d
            return s

        def finish(val):
            if epilogue is None:
                outs[0][...] = val.astype(outs[0].dtype)
            else:
                res = epilogue(val, *[e[...] for e in ex])
                for o, r in zip(outs, res):
                    o[...] = r.astype(o.dtype)

        if nred == 1:
            finish(prod())
        else:
            acc = refs[n_in + n_out]
            k = pl.program_id(len(grid) - 1)

            @pl.when(k == 0)
            def _():
                acc[...] = jnp.zeros_like(acc)

            acc[...] += prod()

            @pl.when(k == nred - 1)
            def _():
                finish(acc[...])

    operands = []
    specs = []
    for (a, b), (sa, sb) in zip(pairs, pair_specs):
        operands += [a, b]
        specs += [sa, sb]
    operands += list(extra)
    specs += list(extra_specs)
    scratch = [pltpu.VMEM(acc_shape, F32)] if nred > 1 else []
    sem = ("arbitrary",) * len(grid)
    res = _pcall(body, name=name, grid=grid, in_specs=specs, out_specs=list(out_specs), input_output_aliases=dict(alias or {}),
                 out_shape=list(out_shapes), scratch_shapes=scratch, compiler_params=_cparams(sem))(*operands)
    return res


def _sds(shape, dtype):
    return jax.ShapeDtypeStruct(shape, dtype)


def _row_tile(T):
    return min(1024, T)


def mm_nn(name, a, b, tn, out_dtype=F32, residual=None):
    M, K = a.shape
    N = b.shape[1]
    tm = _row_tile(M)
    grid = (M // tm, N // tn)
    extra, especs, epi = (), (), None
    if residual is not None:
        extra = (residual,)
        especs = (pl.BlockSpec((tm, tn), lambda i, n: (i, n)),)
        epi = lambda v, r: (v + r,)
    return _matmul(name, [(a, b)], [(pl.BlockSpec((tm, K), lambda i, n: (i, 0)), pl.BlockSpec((K, tn), lambda i, n: (0, n)))],
                   NN, grid, [_sds((M, N), out_dtype)], [pl.BlockSpec((tm, tn), lambda i, n: (i, n))],
                   extra=extra, extra_specs=especs, epilogue=epi)[0]


def mm_nn_colblk(name, a, b, out_dtype=F32):
    M, K = a.shape
    J, _, nb = b.shape
    tm = _row_tile(M)
    grid = (M // tm, J)
    return _matmul(name, [(a, b)], [(pl.BlockSpec((tm, K), lambda i, j: (i, 0)), pl.BlockSpec((None, K, nb), lambda i, j: (j, 0, 0)))],
                   NN, grid, [_sds((M, J * nb), out_dtype)], [pl.BlockSpec((tm, nb), lambda i, j: (i, j))])[0]


def mm_nt(name, a, w, tr, out_dtype=F32):
    M, C = a.shape
    R = w.shape[0]
    tm = _row_tile(M)
    grid = (M // tm, R // tr)
    return _matmul(name, [(a, w)], [(pl.BlockSpec((tm, C), lambda i, r: (i, 0)), pl.BlockSpec((tr, C), lambda i, r: (r, 0)))],
                   NT, grid, [_sds((M, R), out_dtype)], [pl.BlockSpec((tm, tr), lambda i, r: (i, r))])[0]


def mm_nt_red(name, a, w, tr, tk, out_dtype=F32):
    M, C = a.shape
    R = w.shape[0]
    tm = _row_tile(M)
    nk = C // tk
    grid = (M // tm, R // tr, nk)
    return _matmul(name, [(a, w)], [(pl.BlockSpec((tm, tk), lambda i, r, k: (i, k)), pl.BlockSpec((tr, tk), lambda i, r, k: (r, k)))],
                   NT, grid, [_sds((M, R), out_dtype)], [pl.BlockSpec((tm, tr), lambda i, r, k: (i, r))],
                   nred=nk, acc_shape=(tm, tr))[0]


def mm_nn_red(name, a, b, tn, tk, out_dtype=F32, residual=None, rows=None, prev=None):
    M, K = a.shape
    N = b.shape[1]
    tm = _row_tile(M)
    nk = K // tk
    i0, ni = (0, M // tm) if rows is None else rows
    grid = (ni, N // tn, nk)
    ospec = pl.BlockSpec((tm, tn), lambda i, n, k: (i + i0, n))
    extra, especs, epi = [], [], None
    if residual is not None:
        extra, especs, epi = [residual], [ospec], (lambda v, r, *_: (v + r,))
    alias = {}
    if prev is not None:
        alias = {2 + len(extra): 0}
        extra, especs = extra + [prev], especs + [_ANY]
        epi = epi or (lambda v, *_: (v,))
    return _matmul(name, [(a, b)], [(pl.BlockSpec((tm, tk), lambda i, n, k: (i + i0, k)), pl.BlockSpec((tk, tn), lambda i, n, k: (k, n)))],
                   NN, grid, [_sds((M, N), out_dtype)], [ospec], nred=nk, acc_shape=(tm, tn),
                   extra=extra, extra_specs=especs, epilogue=epi, alias=alias)[0]


def mm_tn(name, x, dy, tr, tc, out_dtype=BF16):
    M, R = x.shape
    C = dy.shape[1]
    grid = (R // tr, C // tc)
    return _matmul(name, [(x, dy)], [(pl.BlockSpec((M, tr), lambda r, c: (0, r)), pl.BlockSpec((M, tc), lambda r, c: (0, c)))],
                   TN, grid, [_sds((R, C), out_dtype)], [pl.BlockSpec((tr, tc), lambda r, c: (r, c))])[0]


def mm_tn_colblk(name, x, dy, nb, out_dtype=BF16):
    M, R = x.shape
    J = dy.shape[1] // nb
    grid = (J,)
    return _matmul(name, [(x, dy)], [(pl.BlockSpec((M, R), lambda j: (0, 0)), pl.BlockSpec((M, nb), lambda j: (0, j)))],
                   TN, grid, [_sds((J, R, nb), out_dtype)], [pl.BlockSpec((None, R, nb), lambda j: (j, 0, 0))])[0]


def _rows(T):
    return min(256, T)


def rms_fwd(name, x, g, eps=NORM_EPS):
    T, D = x.shape
    tm = _rows(T)

    def body(x_ref, g_ref, o_ref):
        xv = x_ref[...]
        r = lax.rsqrt(jnp.mean(xv * xv, axis=-1, keepdims=True) + eps)
        o_ref[...] = (xv * r * g_ref[...]).astype(BF16)

    return _pcall(body, name=name, grid=(T // tm,),
                  in_specs=[pl.BlockSpec((tm, D), lambda i: (i, 0)), pl.BlockSpec((1, D), lambda i: (0, 0))],
                  out_specs=pl.BlockSpec((tm, D), lambda i: (i, 0)), out_shape=_sds((T, D), BF16),
                  compiler_params=_cparams(("arbitrary",)))(x, g)


def rms_bwd(name, x, g, dy, dres, eps=NORM_EPS):
    T, D = x.shape
    tm = _rows(T)

    def body(x_ref, g_ref, dy_ref, dr_ref, dx_ref, dxb_ref, dg_ref):
        i = pl.program_id(0)
        xv = x_ref[...]
        r = lax.rsqrt(jnp.mean(xv * xv, axis=-1, keepdims=True) + eps)
        xh = xv * r
        dyv = dy_ref[...]
        gd = dyv * g_ref[...]
        dx = r * (gd - xh * jnp.mean(gd * xh, axis=-1, keepdims=True)) + dr_ref[...]
        dx_ref[...] = dx
        dxb_ref[...] = dx.astype(BF16)

        @pl.when(i == 0)
        def _():
            dg_ref[...] = jnp.zeros_like(dg_ref)

        dg_ref[...] += jnp.sum(dyv * xh, axis=0, keepdims=True)

    row = pl.BlockSpec((tm, D), lambda i: (i, 0))
    vec = pl.BlockSpec((1, D), lambda i: (0, 0))
    return _pcall(body, name=name, grid=(T // tm,), in_specs=[row, vec, row, row], out_specs=[row, row, vec],
                  out_shape=[_sds((T, D), F32), _sds((T, D), BF16), _sds((1, D), F32)],
                  compiler_params=_cparams(("arbitrary",)))(x, g, dy, dres)


def gnorm_fwd(y, projp, g):
    T, D = y.shape
    tm = _rows(T)

    def body(y_ref, z_ref, g_ref, o_ref):
        yz = y_ref[...] * _silu(z_ref[...])
        r = lax.rsqrt(jnp.mean(yz * yz, axis=-1, keepdims=True) + SSM_NORM_EPS)
        o_ref[...] = (yz * r * g_ref[...]).astype(BF16)

    row = pl.BlockSpec((tm, D), lambda i: (i, 0))
    return _pcall(body, name="gnorm_fwd", grid=(T // tm,),
                  in_specs=[row, pl.BlockSpec((tm, D), lambda i: (i, OFF_Z // D)), pl.BlockSpec((1, D), lambda i: (0, 0))],
                  out_specs=row, out_shape=_sds((T, D), BF16), compiler_params=_cparams(("arbitrary",)))(y, projp, g)


def gnorm_bwd(y, projp, g, dyn):
    T, D = y.shape
    tm = _rows(T)

    def body(y_ref, z_ref, g_ref, dyn_ref, dy_ref, dz_ref, dg_ref):
        i = pl.program_id(0)
        yv, zv = y_ref[...], z_ref[...]
        sz = _silu(zv)
        yz = yv * sz
        r = lax.rsqrt(jnp.mean(yz * yz, axis=-1, keepdims=True) + SSM_NORM_EPS)
        xh = yz * r
        dv = dyn_ref[...]
        gd = dv * g_ref[...]
        dyz = r * (gd - xh * jnp.mean(gd * xh, axis=-1, keepdims=True))
        dy_ref[...] = dyz * sz
        dz_ref[...] = (dyz * yv * _dsilu(zv)).astype(BF16)

        @pl.when(i == 0)
        def _():
            dg_ref[...] = jnp.zeros_like(dg_ref)

        dg_ref[...] += jnp.sum(dv * xh, axis=0, keepdims=True)

    row = pl.BlockSpec((tm, D), lambda i: (i, 0))
    vec = pl.BlockSpec((1, D), lambda i: (0, 0))
    return _pcall(body, name="gnorm_bwd", grid=(T // tm,),
                  in_specs=[row, pl.BlockSpec((tm, D), lambda i: (i, OFF_Z // D)), vec, row], out_specs=[row, row, vec],
                  out_shape=[_sds((T, D), F32), _sds((T, D), BF16), _sds((1, D), F32)],
                  compiler_params=_cparams(("arbitrary",)))(y, projp, g, dyn)


def merge_fwd(projp, out_a, out_s):
    T, D = out_a.shape
    tm = _rows(T)

    def body(ga_ref, gs_ref, a_ref, s_ref, o_ref):
        o_ref[...] = (_sigmoid(ga_ref[...]) * a_ref[...] + _sigmoid(gs_ref[...]) * s_ref[...]).astype(BF16)

    row = pl.BlockSpec((tm, D), lambda i: (i, 0))
    return _pcall(body, name="merge_fwd", grid=(T // tm,),
                  in_specs=[pl.BlockSpec((tm, D), lambda i: (i, OFF_GA // D)), pl.BlockSpec((tm, D), lambda i: (i, OFF_GS // D)), row, row],
                  out_specs=row, out_shape=_sds((T, D), BF16), compiler_params=_cparams(("arbitrary",)))(projp, projp, out_a, out_s)


def merge_bwd(projp, out_a, out_s, dmerged):
    T, D = out_a.shape
    tm = _rows(T)

    def body(ga_ref, gs_ref, a_ref, s_ref, dm_ref, da_ref, ds_ref, dga_ref, dgs_ref):
        dm = dm_ref[...]
        sa, ss = _sigmoid(ga_ref[...]), _sigmoid(gs_ref[...])
        da_ref[...] = (dm * sa).astype(BF16)
        ds_ref[...] = (dm * ss).astype(BF16)
        dga_ref[...] = (dm * a_ref[...] * sa * (1.0 - sa)).astype(BF16)
        dgs_ref[...] = (dm * s_ref[...] * ss * (1.0 - ss)).astype(BF16)

    row = pl.BlockSpec((tm, D), lambda i: (i, 0))
    return _pcall(body, name="merge_bwd", grid=(T // tm,),
                  in_specs=[pl.BlockSpec((tm, D), lambda i: (i, OFF_GA // D)), pl.BlockSpec((tm, D), lambda i: (i, OFF_GS // D)), row, row, row],
                  out_specs=[row] * 4, out_shape=[_sds((T, D), BF16)] * 4,
                  compiler_params=_cparams(("arbitrary",)))(projp, projp, out_a, out_s, dmerged)


def head_fwd_bwd(h2, pg, pp, g_final, target):
    T, D = h2.shape
    tm = _rows(T)

    def body(h_ref, pg_ref, pp_ref, g_ref, t_ref, loss_ref, dh_ref, dpg_ref, dpp_ref, dg_ref):
        i = pl.program_id(0)
        s = _sigmoid(pg_ref[...])
        ppv = pp_ref[...]
        h3 = h_ref[...] + s * ppv
        r = lax.rsqrt(jnp.mean(h3 * h3, axis=-1, keepdims=True) + NORM_EPS)
        xh = h3 * r
        gv = g_ref[...]
        e = xh * gv - t_ref[...]
        dyo = e * (1.0 / D)
        gd = dyo * gv
        dh = r * (gd - xh * jnp.mean(gd * xh, axis=-1, keepdims=True))
        dh_ref[...] = dh
        dpg_ref[...] = (dh * ppv * s * (1.0 - s)).astype(BF16)
        dpp_ref[...] = (dh * s).astype(BF16)

        @pl.when(i == 0)
        def _():
            dg_ref[...] = jnp.zeros_like(dg_ref)
            loss_ref[...] = jnp.zeros_like(loss_ref)

        dg_ref[...] += jnp.sum(dyo * xh, axis=0, keepdims=True)
        part = 0.5 * jnp.sum(jnp.mean(e * e, axis=-1, keepdims=True), axis=0, keepdims=True)
        loss_ref[...] += jnp.broadcast_to(part, loss_ref.shape)

    row = pl.BlockSpec((tm, D), lambda i: (i, 0))
    vec = pl.BlockSpec((1, D), lambda i: (0, 0))
    return _pcall(body, name="head_fwd_bwd", grid=(T // tm,), in_specs=[row, row, row, vec, row],
                  out_specs=[pl.BlockSpec((1, 128), lambda i: (0, 0)), row, row, row, vec],
                  out_shape=[_sds((1, 128), F32), _sds((T, D), F32), _sds((T, D), BF16), _sds((T, D), BF16), _sds((1, D), F32)],
                  compiler_params=_cparams(("arbitrary",)))(h2, pg, pp, g_final, target)


FFN_TILE = 512


def ffn_up(f, wgt, wut):
    T, D = f.shape
    H = wgt.shape[0]
    tm = _row_tile(T)

    def body(f_ref, wg_ref, wu_ref, g_ref, u_ref, a_ref):
        fv = f_ref[...]
        g = _dot(fv, wg_ref[...], NT)
        u = _dot(fv, wu_ref[...], NT)
        g_ref[...] = g
        u_ref[...] = u
        a_ref[...] = (_silu(g) * u).astype(BF16)

    wspec = pl.BlockSpec((FFN_TILE, D), lambda i, j: (j, 0))
    ospec = pl.BlockSpec((tm, FFN_TILE), lambda i, j: (i, j))
    return _pcall(body, name="ffn_up", grid=(T // tm, H // FFN_TILE), in_specs=[pl.BlockSpec((tm, D), lambda i, j: (i, 0)), wspec, wspec],
                  out_specs=[ospec] * 3, out_shape=[_sds((T, H), F32), _sds((T, H), F32), _sds((T, H), BF16)],
                  compiler_params=_cparams(("arbitrary", "arbitrary")))(f, wgt, wut)


def ffn_down_bwd(dh2b, wd, gate, up):
    T, D = dh2b.shape
    H = wd.shape[0]
    tm = _row_tile(T)
    ospec = pl.BlockSpec((tm, FFN_TILE), lambda i, j: (i, j))

    def epi(da, g, u):
        return (da * u * _dsilu(g), da * _silu(g))

    return _matmul("ffn_down_bwd", [(dh2b, wd)],
                   [(pl.BlockSpec((tm, D), lambda i, j: (i, 0)), pl.BlockSpec((FFN_TILE, D), lambda i, j: (j, 0)))],
                   NT, (T // tm, H // FFN_TILE), [_sds((T, H), BF16)] * 2, [ospec, ospec],
                   extra=(gate, up), extra_specs=(ospec, ospec), epilogue=epi)


def ffn_up_bwd(dgate, dup, wgt, wut):
    T, H = dgate.shape
    D = wgt.shape[1]
    tm = _row_tile(T)
    tn = 1024
    aspec = pl.BlockSpec((tm, FFN_TILE), lambda i, n, k: (i, k))
    wspec = pl.BlockSpec((FFN_TILE, tn), lambda i, n, k: (k, n))
    return _matmul("ffn_up_bwd", [(dgate, wgt), (dup, wut)], [(aspec, wspec), (aspec, wspec)], NN, (T // tm, D // tn, H // FFN_TILE),
                   [_sds((T, D), F32)], [pl.BlockSpec((tm, tn), lambda i, n, k: (i, n))], nred=H // FFN_TILE, acc_shape=(tm, tn))[0]


def attn_br_bwd(dout_a, wab):
    T, D = dout_a.shape
    J, R, nb = wab.shape
    tm = _row_tile(T)
    return _matmul("attn_br_bwd", [(dout_a, wab)],
                   [(pl.BlockSpec((tm, nb), lambda i, j: (i, j)), pl.BlockSpec((None, R, nb), lambda i, j: (j, 0, 0)))],
                   NT, (T // tm, J), [_sds((T, R), BF16)], [pl.BlockSpec((tm, R), lambda i, j: (i, 0))], nred=J, acc_shape=(tm, R))[0]


def _adam_math(w, g, m, v):
    m2 = ADAM_B1 * m + (1.0 - ADAM_B1) * g
    v2 = ADAM_B2 * v + (1.0 - ADAM_B2) * (g * g)
    m_hat = m2 / (1.0 - ADAM_B1 ** ADAM_STEP)
    v_hat = v2 / (1.0 - ADAM_B2 ** ADAM_STEP)
    delta = -ADAM_LR * (m_hat / (jnp.sqrt(v_hat) + ADAM_EPS) + ADAM_WD * w)
    return delta, m2, v2


def _sum_partials(own, parts):
    g = None if own is None else own.astype(F32)
    if parts is not None:
        for s in range(parts.shape[0]):
            t = parts[s].astype(F32)
            g = t if g is None else g + t
    return g


def adamw(name, parts, w, m, v, own=None, own_slot=None):
    R, C = w.shape
    tr, tc = R, C
    for cand in (256, 176, 128, 64, 32, 16, 8):
        if R % cand == 0 and R > cand:
            tr = cand
            break
    if tr == R and R > 256:
        tc = 256
    given = [a for a in (parts, own) if a is not None]
    pre = own_slot is not None

    def body(*refs):
        refs = refs[1:] if pre else refs
        p_ref = refs[0] if parts is not None else None
        o_ref = refs[len(given) - 1] if own is not None else None
        w_ref, m_ref, v_ref, g_ref, d_ref, m2_ref, v2_ref = refs[-7:]
        g = _sum_partials(None if o_ref is None else o_ref[...], p_ref)
        d, m2, v2 = _adam_math(w_ref[...], g, m_ref[...], v_ref[...])
        g_ref[...] = g
        d_ref[...] = d
        m2_ref[...] = m2
        v2_ref[...] = v2

    blk = pl.BlockSpec((tr, tc), lambda i, j, *s: (i, j))
    specs = [] if parts is None else [pl.BlockSpec((parts.shape[0], tr, tc), lambda i, j, *s: (0, i, j))]
    if own is not None:
        specs.append(pl.BlockSpec((None, tr, tc), lambda i, j, s: (s[0], i, j)) if pre else blk)
    specs += [blk] * 3
    grid = (R // tr, C // tc)
    out_shape = [_sds((R, C), F32)] * 4
    params = _cparams(("arbitrary", "arbitrary"))
    if not pre:
        return _pcall(body, name=name, grid=grid, in_specs=specs, out_specs=[blk] * 4, out_shape=out_shape,
                      compiler_params=params)(*given, w, m, v)
    spec = pltpu.PrefetchScalarGridSpec(num_scalar_prefetch=1, grid=grid, in_specs=specs, out_specs=[blk] * 4)
    return _pcall(body, name=name, grid_spec=spec, out_shape=out_shape,
                  compiler_params=params)(jnp.asarray(own_slot, jnp.int32).reshape(1), *given, w, m, v)


_HBM = pl.BlockSpec(memory_space=pltpu.HBM)
_SEM = pl.BlockSpec(memory_space=pltpu.SEMAPHORE)
_ANY = pl.BlockSpec(memory_space=pl.ANY)
_SPLIT_PARAMS = dict(compiler_params=pltpu.CompilerParams(has_side_effects=pltpu.SideEffectType.DATAFLOW_SIDE_EFFECTING))
ICI_SAME_CORE = (2, 4, 6)
ALL_PEERS = (1, 2, 3, 4, 5, 6, 7)
LAND_SLOTS = {"gather": N_DEV, "scatter": N_DEV - 1, "pair": 4, "scatter_core": 3}


def _mesh_pos():
    x, y, c = lax.axis_index("x"), lax.axis_index("y"), lax.axis_index("c")
    return x, y, c, 4 * x + 2 * y + c


def _peer_of(k, x, y, c):
    px = 1 - x if k & 4 else x
    py = 1 - y if k & 2 else y
    pc = 1 - c if k & 1 else c
    return (px, py, pc), 4 * px + 2 * py + pc


def _split_copies(mode, ks, srcs, lands, send_sems, recv_sems):
    x, y, c, me = _mesh_pos()
    pairs = []
    for a in range(len(lands)):
        for j, k in enumerate(ks):
            dev, peer = _peer_of(k, x, y, c)
            i = a * len(ks) + j
            if mode == "gather":
                s_out, d_out, d_in = lands[a].at[me], lands[a].at[me], lands[a].at[peer]
            elif mode == "scatter":
                s_out, d_out, d_in = srcs[a].at[peer], lands[a].at[k - 1], lands[a].at[k - 1]
            elif mode == "pair":
                dev, _ = _peer_of(1, x, y, c)
                _, theirs = _peer_of(k | 1, x, y, c)
                s_out, d_out, d_in = srcs[a].at[theirs], lands[a].at[j], lands[a].at[j]
            elif mode == "scatter_core":
                s_out, d_out, d_in = srcs[a].at[j + 1], lands[a].at[j], lands[a].at[j]
            else:
                dev, _ = _peer_of(1, x, y, c)
                _, theirs = _peer_of(k | 1, x, y, c)
                s_out, d_out, d_in = lands[a].at[peer], lands[a].at[peer], lands[a].at[theirs]
            both = [pltpu.make_async_remote_copy(src_ref=s_out, dst_ref=d, send_sem=send_sems.at[i], recv_sem=recv_sems.at[i],
                                                 device_id=dev, device_id_type=pl.DeviceIdType.MESH) for d in (d_out, d_in)]
            pairs.append(tuple(both))
    return pairs


def split_start(name, mode, ks, srcs, lands=None, after=None):
    n, nk = len(srcs) if lands is None else len(lands), len(ks)
    srcs = [pltpu.with_memory_space_constraint(s, pltpu.HBM) for s in srcs]
    if lands is None:
        shapes = [((N_DEV,) + s.shape) if mode == "gather" else ((LAND_SLOTS[mode],) + s.shape[1:]) for s in srcs]
        lands = [lax.empty(shp, s.dtype) for shp, s in zip(shapes, srcs)]
    lands = [pltpu.with_memory_space_constraint(l, pltpu.HBM) for l in lands]
    both = srcs + lands
    extra = [] if after is None else [after]

    def body(*refs):
        src_refs, land_refs = refs[:len(srcs)], refs[len(srcs):len(both)]
        send_sems, recv_sems = refs[len(both) + len(extra)], refs[len(both) + len(extra) + 1]
        token = refs[-1]
        for out, _ in _split_copies(mode, ks, src_refs, land_refs, send_sems, recv_sems):
            out.start()
        token[...] = jnp.zeros_like(token)

    out_shape = (pltpu.SemaphoreType.DMA((n * nk,)), pltpu.SemaphoreType.DMA((n * nk,)),
                 *[pltpu.HBM(a.shape, a.dtype) for a in both], _sds((8, 128), F32))
    res = _raw_call(body, name=name, out_shape=out_shape, in_specs=[_HBM] * len(both) + [_ANY] * len(extra),
                    out_specs=(_SEM, _SEM, *[_HBM] * len(both), pl.BlockSpec(memory_space=pltpu.VMEM)),
                    input_output_aliases={i: 2 + i for i in range(len(both))}, **_SPLIT_PARAMS)(*both, *extra)
    _PENDING.append(res[-1])
    return dict(mode=mode, ks=ks, sems=(res[0], res[1]), srcs=list(res[2:2 + len(srcs)]),
                lands=list(res[2 + len(srcs):2 + len(both)]), token=res[-1])


def split_wait(name, h, after):
    ns = len(h["srcs"])
    both = h["srcs"] + h["lands"]
    after = list(after) if isinstance(after, (list, tuple)) else [after]

    def body(*refs):
        src_refs, land_refs = refs[:ns], refs[ns:len(both)]
        send_sems, recv_sems = refs[len(both)], refs[len(both) + 1]
        for out, arriving in _split_copies(h["mode"], h["ks"], src_refs, land_refs, send_sems, recv_sems):
            out.wait_send()
            arriving.wait_recv()

    res = _raw_call(body, name=name, out_shape=tuple(pltpu.HBM(a.shape, a.dtype) for a in both),
                    in_specs=[_HBM] * len(both) + [_SEM, _SEM] + [_ANY] * len(after), out_specs=tuple([_HBM] * len(both)),
                    input_output_aliases={i: i for i in range(len(both))}, **_SPLIT_PARAMS)(*both, *h["sems"], *after)
    return list(res[:ns]), list(res[ns:])


FORWARD_BLOCKS = (0, 2, 4, 6)


def pair_sum(name, mine, slots, theirs):
    P, R, C = theirs.shape
    tc = 512

    def body(s_ref, a_ref, b_ref, o_ref):
        o_ref[...] = (a_ref[...].astype(F32) + b_ref[...].astype(F32)).astype(o_ref.dtype)

    blk = pl.BlockSpec((None, R, tc), lambda p, i, s: (p, 0, i))
    spec = pltpu.PrefetchScalarGridSpec(num_scalar_prefetch=1, grid=(P, C // tc),
                                        in_specs=[pl.BlockSpec((None, R, tc), lambda p, i, s: (s[p], 0, i)), blk], out_specs=blk)
    return _pcall(body, name=name, grid_spec=spec, out_shape=_sds((P, R, C), theirs.dtype),
                  compiler_params=_cparams(("arbitrary", "arbitrary")))(slots, mine, theirs)


def _rope_parts(pos_ref, inv_ref):
    ang = pos_ref[...] * inv_ref[...]
    return jnp.cos(ang), jnp.sin(ang)


def _rot_half(t):
    lane = lax.broadcasted_iota(jnp.int32, t.shape, 1)
    return jnp.where((lane % HEAD_DIM) < HEAD_DIM // 2, -pltpu.roll(t, 128 - HEAD_DIM // 2, 1), pltpu.roll(t, HEAD_DIM // 2, 1))


def _attn_mask(n):
    row = lax.broadcasted_iota(jnp.int32, (BLK, 2 * BLK), 0)
    col = lax.broadcasted_iota(jnp.int32, (BLK, 2 * BLK), 1)
    dist = row + BLK - col
    return (dist >= 0) & (dist < BLK) & ((n * BLK - BLK + col) >= 0)


def _attn_specs(T):
    prev = lambda n: jnp.maximum(n - 1, 0)
    kc = pl.BlockSpec((BLK, KV_DIM), lambda n: (n, OFF_K // KV_DIM))
    kp = pl.BlockSpec((BLK, KV_DIM), lambda n: (prev(n), OFF_K // KV_DIM))
    vc = pl.BlockSpec((BLK, KV_DIM), lambda n: (n, OFF_V // KV_DIM))
    vp = pl.BlockSpec((BLK, KV_DIM), lambda n: (prev(n), OFF_V // KV_DIM))
    pc = pl.BlockSpec((BLK, 1), lambda n: (n, 0))
    pp = pl.BlockSpec((BLK, 1), lambda n: (prev(n), 0))
    inv = pl.BlockSpec((1, 128), lambda n: (0, 0))
    sink = pl.BlockSpec(memory_space=pltpu.SMEM)
    return kc, kp, vc, vp, pc, pp, inv, sink


def _softmax_sink(sc, valid, sink):
    sc = jnp.where(valid, sc * (HEAD_DIM ** -0.5), -1e30)
    m = jnp.maximum(jnp.max(sc, axis=1, keepdims=True), sink)
    e = jnp.exp(sc - m)
    es = jnp.exp(sink - m)
    den = jnp.sum(e, axis=1, keepdims=True) + es
    return e / den, es / den


def attn_fwd(projp, posf, inv128, sinks):
    T = projp.shape[0]
    kc, kp, vc, vp, pc, pp, inv, sink = _attn_specs(T)

    def body(q_ref, kc_ref, kp_ref, vc_ref, vp_ref, pc_ref, pp_ref, inv_ref, sink_ref, o_ref, qr_ref, kr_ref):
        n = pl.program_id(0)
        cos_c, sin_c = _rope_parts(pc_ref, inv_ref)
        cos_p, sin_p = _rope_parts(pp_ref, inv_ref)
        valid = _attn_mask(n)
        k_c, k_p = [], []
        for s in range(KV_DIM // 128):
            t = kc_ref[:, 128 * s:128 * (s + 1)]
            k_c.append((t * cos_c + _rot_half(t) * sin_c).astype(BF16))
            kr_ref[:, 128 * s:128 * (s + 1)] = k_c[s]
            t = kp_ref[:, 128 * s:128 * (s + 1)]
            k_p.append((t * cos_p + _rot_half(t) * sin_p).astype(BF16))
        kcat, vcat = [], []
        for hk in range(KV_HEADS):
            lo = HEAD_DIM * (hk % 2)
            kcat.append(jnp.concatenate([k_p[hk // 2][:, lo:lo + HEAD_DIM], k_c[hk // 2][:, lo:lo + HEAD_DIM]], axis=0))
            vcat.append(jnp.concatenate([vp_ref[:, HEAD_DIM * hk:HEAD_DIM * (hk + 1)], vc_ref[:, HEAD_DIM * hk:HEAD_DIM * (hk + 1)]], axis=0)
                        .astype(BF16))
        q_heads = []
        for s in range(Q_DIM // 128):
            t = q_ref[:, 128 * s:128 * (s + 1)]
            qs = (t * cos_c + _rot_half(t) * sin_c).astype(BF16)
            qr_ref[:, 128 * s:128 * (s + 1)] = qs
            q_heads += [qs[:, :HEAD_DIM], qs[:, HEAD_DIM:]]
        G = ATTN_HEADS // KV_HEADS
        scores = [_dot(q_heads[hq], kcat[hq // G], NT) for hq in range(ATTN_HEADS)]
        probs = [_softmax_sink(scores[hq], valid, sink_ref[0, hq])[0] for hq in range(ATTN_HEADS)]
        outs = [_dot(probs[hq], vcat[hq // G], NN) for hq in range(ATTN_HEADS)]
        for s in range(Q_DIM // 128):
            o_ref[:, 128 * s:128 * (s + 1)] = jnp.concatenate([outs[2 * s], outs[2 * s + 1]], axis=1).astype(BF16)

    qspec = pl.BlockSpec((BLK, Q_DIM), lambda n: (n, OFF_Q // Q_DIM))
    orow = pl.BlockSpec((BLK, Q_DIM), lambda n: (n, 0))
    krow = pl.BlockSpec((BLK, KV_DIM), lambda n: (n, 0))
    return _pcall(body, name="attn_fwd", grid=(T // BLK,), in_specs=[qspec, kc, kp, vc, vp, pc, pp, inv, sink],
                  out_specs=[orow, orow, krow], out_shape=[_sds((T, Q_DIM), BF16), _sds((T, Q_DIM), BF16), _sds((T, KV_DIM), BF16)],
                  compiler_params=_cparams(("arbitrary",)))(projp, projp, projp, projp, projp, posf, posf, inv128, sinks)


def attn_bwd(qr, kr, projp, dattn, posf, inv128, sinks):
    T = projp.shape[0]
    _, _, vc, vp, pc, pp, inv, sink = _attn_specs(T)
    G = ATTN_HEADS // KV_HEADS

    def body(qr_ref, krc_ref, krp_ref, vc_ref, vp_ref, do_ref, pc_ref, pp_ref, inv_ref, sink_ref, dq_ref, dk_ref, dv_ref, dsk_ref):
        n = pl.program_id(0)

        @pl.when(n == 0)
        def _():
            dk_ref[...] = jnp.zeros_like(dk_ref)
            dv_ref[...] = jnp.zeros_like(dv_ref)
            dsk_ref[...] = jnp.zeros_like(dsk_ref)

        cos_c, sin_c = _rope_parts(pc_ref, inv_ref)
        cos_p, sin_p = _rope_parts(pp_ref, inv_ref)
        valid = _attn_mask(n)
        lane = lax.broadcasted_iota(jnp.int32, (1, 128), 1)
        kcat, vcat = [], []
        for hk in range(KV_HEADS):
            ksl = slice(HEAD_DIM * hk, HEAD_DIM * (hk + 1))
            kcat.append(jnp.concatenate([krp_ref[:, ksl], krc_ref[:, ksl]], axis=0))
            vcat.append(jnp.concatenate([vp_ref[:, ksl], vc_ref[:, ksl]], axis=0).astype(BF16))
        H = range(ATTN_HEADS)
        q_heads = [qr_ref[:, HEAD_DIM * hq:HEAD_DIM * (hq + 1)] for hq in H]
        do_heads = [do_ref[:, HEAD_DIM * hq:HEAD_DIM * (hq + 1)] for hq in H]
        soft = [_softmax_sink(_dot(q_heads[hq], kcat[hq // G], NT), valid, sink_ref[0, hq]) for hq in H]
        dps = [_dot(do_heads[hq], vcat[hq // G], NT) for hq in H]
        deltas = [jnp.sum(soft[hq][0] * dps[hq], axis=1, keepdims=True) for hq in H]
        dss = [(soft[hq][0] * (dps[hq] - deltas[hq]) * (HEAD_DIM ** -0.5)).astype(BF16) for hq in H]
        pbs = [soft[hq][0].astype(BF16) for hq in H]
        dsk = jnp.zeros((1, 128), F32)
        for hq in H:
            dsk = dsk + jnp.where(lane == hq, -jnp.sum(soft[hq][1] * deltas[hq], axis=0, keepdims=True), 0.0)
        dsk_ref[...] += dsk
        dq_heads = [_dot(dss[hq], kcat[hq // G], NN) for hq in H]
        dk_parts = [_dot(dss[hq], q_heads[hq], TN) for hq in H]
        dv_parts = [_dot(pbs[hq], do_heads[hq], TN) for hq in H]
        dk_heads = [sum(dk_parts[G * hk + 1:G * (hk + 1)], dk_parts[G * hk]) for hk in range(KV_HEADS)]
        dv_heads = [sum(dv_parts[G * hk + 1:G * (hk + 1)], dv_parts[G * hk]) for hk in range(KV_HEADS)]
        for s in range(Q_DIM // 128):
            t = jnp.concatenate([dq_heads[2 * s], dq_heads[2 * s + 1]], axis=1)
            dq_ref[:, 128 * s:128 * (s + 1)] = (t * cos_c - _rot_half(t) * sin_c).astype(BF16)
        cur = pl.ds(pl.multiple_of(n * BLK, BLK), BLK)
        prv = pl.ds(pl.multiple_of(jnp.maximum(n - 1, 0) * BLK, BLK), BLK)
        for s in range(KV_DIM // 128):
            tc = jnp.concatenate([dk_heads[2 * s][BLK:], dk_heads[2 * s + 1][BLK:]], axis=1)
            tp = jnp.concatenate([dk_heads[2 * s][:BLK], dk_heads[2 * s + 1][:BLK]], axis=1)
            cols = slice(128 * s, 128 * (s + 1))
            dk_ref[cur, cols] += tc * cos_c - _rot_half(tc) * sin_c
            dk_ref[prv, cols] += tp * cos_p - _rot_half(tp) * sin_p
            dv_ref[cur, cols] += jnp.concatenate([dv_heads[2 * s][BLK:], dv_heads[2 * s + 1][BLK:]], axis=1)
            dv_ref[prv, cols] += jnp.concatenate([dv_heads[2 * s][:BLK], dv_heads[2 * s + 1][:BLK]], axis=1)

    qrow = pl.BlockSpec((BLK, Q_DIM), lambda n: (n, 0))
    krc = pl.BlockSpec((BLK, KV_DIM), lambda n: (n, 0))
    krp = pl.BlockSpec((BLK, KV_DIM), lambda n: (jnp.maximum(n - 1, 0), 0))
    whole = pl.BlockSpec((T, KV_DIM), lambda n: (0, 0))
    return _pcall(body, name="attn_bwd", grid=(T // BLK,), in_specs=[qrow, krc, krp, vc, vp, qrow, pc, pp, inv, sink],
                  out_specs=[qrow, whole, whole, pl.BlockSpec((1, 128), lambda n: (0, 0))],
                  out_shape=[_sds((T, Q_DIM), BF16), _sds((T, KV_DIM), F32), _sds((T, KV_DIM), F32), _sds((1, 128), F32)],
                  compiler_params=_cparams(("arbitrary",)))(qr, kr, kr, projp, projp, dattn, posf, posf, inv128, sinks)


CONV_CB = 256


def _shift_down(x, s):
    row = lax.broadcasted_iota(jnp.int32, x.shape, 0)
    return jnp.where(row >= s, pltpu.roll(x, s, 0), 0.0)


def _shift_up(x, s):
    T = x.shape[0]
    row = lax.broadcasted_iota(jnp.int32, x.shape, 0)
    return jnp.where(row < T - s, pltpu.roll(x, T - s, 0), 0.0)


def _conv_pre(x, w_ref, b_ref):
    acc = x * w_ref[CONV_WIDTH - 1:CONV_WIDTH, :] + b_ref[...]
    for s in range(1, CONV_WIDTH):
        acc = acc + _shift_down(x, s) * w_ref[CONV_WIDTH - 1 - s:CONV_WIDTH - s, :]
    return acc


def conv_fwd(projp, conv_w, conv_b):
    T = projp.shape[0]

    def body(x_ref, w_ref, b_ref, o_ref):
        o_ref[...] = _silu(_conv_pre(x_ref[...], w_ref, b_ref))

    return _pcall(body, name="conv_fwd", grid=(CONV_DIM // CONV_CB,),
                  in_specs=[pl.BlockSpec((T, CONV_CB), lambda c: (0, OFF_XBC // CONV_CB + c)),
                            pl.BlockSpec((CONV_WIDTH, CONV_CB), lambda c: (0, c)), pl.BlockSpec((1, CONV_CB), lambda c: (0, c))],
                  out_specs=pl.BlockSpec((T, CONV_CB), lambda c: (0, c)), out_shape=_sds((T, CONV_DIM), F32),
                  compiler_params=_cparams(("arbitrary",)))(projp, conv_w, conv_b)


def conv_bwd(name, projp, dact, conv_w, conv_b, col0):
    T, C = dact.shape
    c0 = col0 // CONV_CB

    def body(x_ref, da_ref, w_ref, b_ref, dx_ref, dw_ref, db_ref):
        x = x_ref[...]
        dpre = da_ref[...] * _dsilu(_conv_pre(x, w_ref, b_ref))
        dx = dpre * w_ref[CONV_WIDTH - 1:CONV_WIDTH, :]
        dw_ref[CONV_WIDTH - 1:CONV_WIDTH, :] = jnp.sum(dpre * x, axis=0, keepdims=True)
        for s in range(1, CONV_WIDTH):
            i = CONV_WIDTH - 1 - s
            dx = dx + _shift_up(dpre, s) * w_ref[i:i + 1, :]
            dw_ref[i:i + 1, :] = jnp.sum(dpre * _shift_down(x, s), axis=0, keepdims=True)
        dx_ref[...] = dx.astype(BF16)
        db_ref[...] = jnp.sum(dpre, axis=0, keepdims=True)

    return _pcall(body, name=name, grid=(C // CONV_CB,),
                  in_specs=[pl.BlockSpec((T, CONV_CB), lambda c: (0, OFF_XBC // CONV_CB + c0 + c)),
                            pl.BlockSpec((T, CONV_CB), lambda c: (0, c)),
                            pl.BlockSpec((CONV_WIDTH, CONV_CB), lambda c: (0, c0 + c)), pl.BlockSpec((1, CONV_CB), lambda c: (0, c0 + c))],
                  out_specs=[pl.BlockSpec((T, CONV_CB), lambda c: (0, c)), pl.BlockSpec((CONV_WIDTH, CONV_CB), lambda c: (0, c)),
                             pl.BlockSpec((1, CONV_CB), lambda c: (0, c))],
                  out_shape=[_sds((T, C), BF16), _sds((CONV_WIDTH, C), F32), _sds((1, C), F32)],
                  compiler_params=_cparams(("arbitrary",)))(projp, dact, conv_w, conv_b)


def _softplus(x):
    return jnp.maximum(x, 0.0) + jnp.log1p(jnp.exp(-jnp.abs(x)))


def _tri(lower):
    r = lax.broadcasted_iota(jnp.int32, (BLK, BLK), 0)
    c = lax.broadcasted_iota(jnp.int32, (BLK, BLK), 1)
    return (r >= c) if lower else (c >= r)


def _ssd_chunk_setup(dt_ref, dtb_ref, alog_ref):
    raw = dt_ref[...] + dtb_ref[...]
    dt = _softplus(raw)
    aneg = -jnp.exp(alog_ref[...])
    a = dt * aneg
    acs = jnp.dot(_tri(True).astype(F32), a, precision=lax.Precision.HIGHEST, preferred_element_type=F32)
    return raw, dt, aneg, acs, acs.T


def _ssd_specs(T, rev):
    nc = T // BLK
    ci = (lambda c: nc - 1 - c) if rev else (lambda c: c)
    xs = pl.BlockSpec((BLK, D_INNER), lambda c: (ci(c), 0))
    bm = pl.BlockSpec((BLK, SSM_GROUPS * D_STATE), lambda c: (ci(c), D_INNER // (SSM_GROUPS * D_STATE)))
    cm = pl.BlockSpec((BLK, SSM_GROUPS * D_STATE), lambda c: (ci(c), D_INNER // (SSM_GROUPS * D_STATE) + 1))
    dt = pl.BlockSpec((BLK, DT_PAD), lambda c: (ci(c), OFF_DT // DT_PAD))
    v128 = pl.BlockSpec((1, 128), lambda c: (0, 0))
    dfull = pl.BlockSpec((1, D_INNER), lambda c: (0, 0))
    st = pl.BlockSpec((None, SSM_HEADS, HEAD_DIM, D_STATE), lambda c: (ci(c), 0, 0, 0))
    return xs, bm, cm, dt, v128, dfull, st, ci


GW = HEADS_PER_GROUP * HEAD_DIM


def _expanders():
    e = np.zeros((SSM_GROUPS, 128, GW), np.float32)
    for g in range(SSM_GROUPS):
        for hh in range(HEADS_PER_GROUP):
            e[g, HEADS_PER_GROUP * g + hh, HEAD_DIM * hh:HEAD_DIM * (hh + 1)] = 1.0
    return jnp.asarray(e), jnp.asarray(np.transpose(e, (0, 2, 1)).copy())


def _dotx(a, b):
    return jnp.dot(a, b, precision=lax.Precision.HIGHEST, preferred_element_type=F32)


def _decay(acs, acsT, h, tril):
    return jnp.where(tril, jnp.exp(jnp.where(tril, acs[:, h:h + 1] - acsT[h:h + 1, :], 0.0)), 0.0)


def ssd_fwd(xbc, projp, dtb, alog, dfull):
    T = xbc.shape[0]
    nc = T // BLK
    xs, bm, cm, dts, v128, dfs, st, _ = _ssd_specs(T, False)
    E, _ = _expanders()

    def body(xs_ref, b_ref, c_ref, dt_ref, dtb_ref, alog_ref, d_ref, e_ref, y_ref, st_ref, h_scr):
        c = pl.program_id(0)

        @pl.when(c == 0)
        def _():
            h_scr[...] = jnp.zeros_like(h_scr)

        _, dt, _, acs, acsT = _ssd_chunk_setup(dt_ref, dtb_ref, alog_ref)
        tril = _tri(True)
        alast = acs[BLK - 1:BLK, :]
        eacs = jnp.exp(acs)
        wmat = jnp.exp(alast - acs)
        gam = jnp.exp(alast)
        for g in range(SSM_GROUPS):
            gl = slice(GW * g, GW * (g + 1))
            hsl = slice(HEADS_PER_GROUP * g, HEADS_PER_GROUP * (g + 1))
            heads = [HEADS_PER_GROUP * g + hh for hh in range(HEADS_PER_GROUP)]
            Eg = e_ref[g]
            B = b_ref[:, D_STATE * g:D_STATE * (g + 1)].astype(BF16)
            C = c_ref[:, D_STATE * g:D_STATE * (g + 1)].astype(BF16)
            cb = _dot(C, B, NT)
            x_g = xs_ref[:, gl]
            xd_g = x_g * _dotx(dt, Eg)
            hold = h_scr[hsl]
            st_ref[hsl] = hold
            hcat = hold.reshape(GW, D_STATE)
            yoff = _dotx(eacs, Eg) * _dot(C, hcat, NT)
            S = _dot(xd_g * _dotx(wmat, Eg), B, TN)
            Ms = [cb * _decay(acs, acsT, h, tril) for h in heads]
            ys = [_dot(Ms[hh], xd_g[:, HEAD_DIM * hh:HEAD_DIM * (hh + 1)], NN) for hh in range(HEADS_PER_GROUP)]
            for hh, h in enumerate(heads):
                h_scr[h] = gam[:, h:h + 1] * hold[hh] + S[HEAD_DIM * hh:HEAD_DIM * (hh + 1)]
            y_ref[:, gl] = jnp.concatenate(ys, axis=1) + yoff + d_ref[:, gl] * x_g

    espec = pl.BlockSpec((SSM_GROUPS, 128, GW), lambda c: (0, 0, 0))
    return _pcall(body, name="ssd_fwd", grid=(nc,), in_specs=[xs, bm, cm, dts, v128, v128, dfs, espec],
                  out_specs=[xs, st], out_shape=[_sds((T, D_INNER), F32), _sds((nc, SSM_HEADS, HEAD_DIM, D_STATE), F32)],
                  scratch_shapes=[pltpu.VMEM((SSM_HEADS, HEAD_DIM, D_STATE), F32)],
                  compiler_params=_cparams(("arbitrary",)))(xbc, xbc, xbc, projp, dtb, alog, dfull, E)


def ssd_bwd(xbc, projp, dtb, alog, dfull, states, dy):
    T = xbc.shape[0]
    nc = T // BLK
    xs, bm, cm, dts, v128, dfs, st, ci = _ssd_specs(T, True)
    gn = SSM_GROUPS * D_STATE
    E, ET = _expanders()

    def body(xs_ref, b_ref, c_ref, dt_ref, dtb_ref, alog_ref, d_ref, st_ref, dy_ref, e_ref, et_ref,
             dxs_ref, dB_ref, dC_ref, ddt_ref, dal_ref, dD_ref, ddtb_ref, dh_scr):
        i = pl.program_id(0)

        @pl.when(i == 0)
        def _():
            dh_scr[...] = jnp.zeros_like(dh_scr)
            dal_ref[...] = jnp.zeros_like(dal_ref)
            dD_ref[...] = jnp.zeros_like(dD_ref)
            ddtb_ref[...] = jnp.zeros_like(ddtb_ref)

        raw, dt, aneg, acs, acsT = _ssd_chunk_setup(dt_ref, dtb_ref, alog_ref)
        tril = _tri(True)
        lane = lax.broadcasted_iota(jnp.int32, (BLK, 128), 1)
        sub = lax.broadcasted_iota(jnp.int32, (BLK, 128), 0)
        alast = acs[BLK - 1:BLK, :]
        eacs = jnp.exp(acs)
        wmat = jnp.exp(alast - acs)
        gam = jnp.exp(alast)
        gcol = jnp.exp(acsT[:, BLK - 1:BLK])
        ds_col = jnp.zeros((BLK, 128), F32)
        ds_row = jnp.zeros((BLK, 128), F32)
        ddt_col = jnp.zeros((BLK, 128), F32)
        dDm = jnp.zeros((BLK, 128), F32)
        hl = [slice(HEAD_DIM * hh, HEAD_DIM * (hh + 1)) for hh in range(HEADS_PER_GROUP)]
        for g in range(SSM_GROUPS):
            gl = slice(GW * g, GW * (g + 1))
            gs = slice(D_STATE * g, D_STATE * (g + 1))
            hsl = slice(HEADS_PER_GROUP * g, HEADS_PER_GROUP * (g + 1))
            heads = [HEADS_PER_GROUP * g + hh for hh in range(HEADS_PER_GROUP)]
            Eg, ETg = e_ref[g], et_ref[g]
            B = b_ref[:, gs].astype(BF16)
            C = c_ref[:, gs].astype(BF16)
            cb = _dot(C, B, NT)
            x_g, dy_g = xs_ref[:, gl], dy_ref[:, gl]
            dt_x, w_x = _dotx(dt, Eg), _dotx(wmat, Eg)
            xd_g = x_g * dt_x
            dye = dy_g * _dotx(eacs, Eg)
            hcat = st_ref[hsl].reshape(GW, D_STATE)
            dSv = dh_scr[hsl]
            dScat = dSv.reshape(GW, D_STATE)
            dDm = dDm + _dotx(dy_g * x_g, ETg)
            dH_y = _dot(dye, C, TN)
            dC_g = _dot(dye, hcat, NN)
            ds_col = ds_col + _dotx(dye * _dot(C, hcat, NT), ETg)
            dxdw = _dot(B, dScat, NT)
            dB_g = _dot(xd_g * w_x, dScat, NN)
            dww = _dotx(xd_g * dxdw, ETg) * wmat
            ds_col = ds_col - dww + jnp.where(sub == BLK - 1, jnp.sum(dww, axis=0, keepdims=True), 0.0)
            hd = jnp.sum(_dotx(Eg, dScat * hcat), axis=1, keepdims=True) * gcol
            ds_row = ds_row - jnp.where(lane == BLK - 1, hd, 0.0)
            decays = [_decay(acs, acsT, h, tril) for h in heads]
            Ms = [cb * d for d in decays]
            dMs = [_dot(dy_g[:, hl[hh]], xd_g[:, hl[hh]], NT) for hh in range(HEADS_PER_GROUP)]
            dxd1 = [_dot(Ms[hh], dy_g[:, hl[hh]], TN) for hh in range(HEADS_PER_GROUP)]
            dG = jnp.zeros((BLK, BLK), F32)
            for hh, h in enumerate(heads):
                Q = dMs[hh] * Ms[hh]
                ds_col = ds_col + jnp.where(lane == h, jnp.sum(Q, axis=1, keepdims=True), 0.0)
                ds_row = ds_row + jnp.where(sub == h, jnp.sum(Q, axis=0, keepdims=True), 0.0)
                dG = dG + dMs[hh] * decays[hh]
            dxd_g = jnp.concatenate(dxd1, axis=1) + w_x * dxdw
            dxs_ref[:, gl] = d_ref[:, gl] * dy_g + dxd_g * dt_x
            ddt_col = ddt_col + _dotx(dxd_g * x_g, ETg)
            dC_ref[:, gs] = dC_g + _dot(dG, B, NN)
            dB_ref[:, gs] = dB_g + _dot(dG, C, TN)
            for hh, h in enumerate(heads):
                dh_scr[h] = gam[:, h:h + 1] * dSv[hh] + dH_y[hl[hh]]
        ds_all = ds_col - ds_row.T
        da = jnp.dot(_tri(False).astype(F32), ds_all, precision=lax.Precision.HIGHEST, preferred_element_type=F32)
        ddt = ddt_col + da * aneg
        draw = jnp.where(lane < SSM_HEADS, ddt * _sigmoid(raw), 0.0)
        ddt_ref[...] = draw.astype(BF16)
        dal_ref[...] += jnp.sum(da * dt, axis=0, keepdims=True) * aneg
        ddtb_ref[...] += jnp.sum(draw, axis=0, keepdims=True)
        dD_ref[...] += jnp.sum(dDm, axis=0, keepdims=True)

    gblk = pl.BlockSpec((BLK, gn), lambda c: (ci(c), 0))
    espec = pl.BlockSpec((SSM_GROUPS, 128, GW), lambda c: (0, 0, 0))
    etspec = pl.BlockSpec((SSM_GROUPS, GW, 128), lambda c: (0, 0, 0))
    return _pcall(body, name="ssd_bwd", grid=(nc,), in_specs=[xs, bm, cm, dts, v128, v128, dfs, st, xs, espec, etspec],
                  out_specs=[xs, gblk, gblk, pl.BlockSpec((BLK, DT_PAD), lambda c: (ci(c), 0)), v128, v128, v128],
                  out_shape=[_sds((T, D_INNER), F32), _sds((T, gn), F32), _sds((T, gn), F32), _sds((T, DT_PAD), BF16),
                             _sds((1, 128), F32), _sds((1, 128), F32), _sds((1, 128), F32)],
                  scratch_shapes=[pltpu.VMEM((SSM_HEADS, HEAD_DIM, D_STATE), F32)],
                  compiler_params=_cparams(("arbitrary",)))(xbc, xbc, xbc, projp, dtb, alog, dfull, states, dy, E, ET)


_WIN_ORDER = ("z", "ga", "gs", "xbc", "q", "k", "v", "dt")


def _win_to_padded(win_g):
    full = win_g.reshape(IN_DIM, D_MODEL)
    rows = []
    for nm in _WIN_ORDER:
        s, w = SEG[nm]
        rows.append(full[s:s + w])
    rows.append(jnp.zeros((DT_PAD - SEG["dt"][1], D_MODEL), win_g.dtype))
    return jnp.concatenate(rows, axis=0)


def _padded_to_win(dw):
    off = dict(z=OFF_Z, ga=OFF_GA, gs=OFF_GS, xbc=OFF_XBC, q=OFF_Q, k=OFF_K, v=OFF_V, dt=OFF_DT)
    per = IN_DIM // N_DEV
    blocks = []
    for j in range(N_DEV):
        lo, hi, rows = j * per, (j + 1) * per, []
        for nm in ("q", "k", "v", "z", "xbc", "dt", "ga", "gs"):
            s, w = SEG[nm]
            a, b = max(lo, s), min(hi, s + w)
            if a < b:
                rows.append(dw[off[nm] + a - s:off[nm] + b - s])
        blocks.append(jnp.concatenate(rows, axis=0))
    return jnp.stack(blocks)


def _pad128(v):
    return jnp.pad(v, ((0, 0), (0, 128 - v.shape[1])))


_SMALL = (("loss", 128, 1), ("g_mix", 2048, 2048), ("conv_b", 3072, 3072), ("dt_bias", 128, 32), ("a_log", 128, 32),
          ("d_skip", 128, 32), ("g_ssd", 2048, 2048), ("sinks", 128, 16), ("g_ffn", 2048, 2048), ("g_ple", 2048, 2048),
          ("g_final", 2048, 2048))


def _small_vec(d):
    parts = []
    for nm, pw, w in _SMALL:
        v = d[nm].reshape(1, -1).astype(F32)
        parts.append(jnp.pad(v[:, :min(v.shape[1], pw)], ((0, 0), (0, pw - min(v.shape[1], pw)))))
    return jnp.concatenate(parts, axis=1)


def _small_split(vec):
    out, o = {}, 0
    for nm, pw, w in _SMALL:
        out[nm] = vec[0, o:o + w]
        o += pw
    return out


def kernel(x, p, positions, g_mix, w_in, conv_w, conv_b, dt_bias, a_log, d_skip, g_ssd, sinks, w_attn_br, w_ssd_br, w_o, g_ffn, w_gate, w_up, w_down, g_ple, w_ple_gate, w_ple_proj, g_final, loss_target, m_g_mix, m_w_in, m_conv_w, m_conv_b, m_dt_bias, m_a_log, m_d_skip, m_g_ssd, m_sinks, m_w_attn_br, m_w_ssd_br, m_w_o, m_g_ffn, m_w_gate, m_w_up, m_w_down, m_g_ple, m_w_ple_gate, m_w_ple_proj, m_g_final, v_g_mix, v_w_in, v_conv_w, v_conv_b, v_dt_bias, v_a_log, v_d_skip, v_g_ssd, v_sinks, v_w_attn_br, v_w_ssd_br, v_w_o, v_g_ffn, v_w_gate, v_w_up, v_w_down, v_g_ple, v_w_ple_gate, v_w_ple_proj, v_g_final):
    T = x.shape[1]
    D = D_MODEL
    W = dict(g_mix=g_mix, w_in=w_in, conv_w=conv_w, conv_b=conv_b, dt_bias=dt_bias, a_log=a_log, d_skip=d_skip, g_ssd=g_ssd,
             sinks=sinks, w_attn_br=w_attn_br, w_ssd_br=w_ssd_br, w_o=w_o, g_ffn=g_ffn, w_gate=w_gate, w_up=w_up, w_down=w_down,
             g_ple=g_ple, w_ple_gate=w_ple_gate, w_ple_proj=w_ple_proj, g_final=g_final)
    Mo = dict(g_mix=m_g_mix, w_in=m_w_in, conv_w=m_conv_w, conv_b=m_conv_b, dt_bias=m_dt_bias, a_log=m_a_log, d_skip=m_d_skip,
              g_ssd=m_g_ssd, sinks=m_sinks, w_attn_br=m_w_attn_br, w_ssd_br=m_w_ssd_br, w_o=m_w_o, g_ffn=m_g_ffn, w_gate=m_w_gate,
              w_up=m_w_up, w_down=m_w_down, g_ple=m_g_ple, w_ple_gate=m_w_ple_gate, w_ple_proj=m_w_ple_proj, g_final=m_g_final)
    Vo = dict(g_mix=v_g_mix, w_in=v_w_in, conv_w=v_conv_w, conv_b=v_conv_b, dt_bias=v_dt_bias, a_log=v_a_log, d_skip=v_d_skip,
              g_ssd=v_g_ssd, sinks=v_sinks, w_attn_br=v_w_attn_br, w_ssd_br=v_w_ssd_br, w_o=v_w_o, g_ffn=v_g_ffn, w_gate=v_w_gate,
              w_up=v_w_up, w_down=v_w_down, g_ple=v_g_ple, w_ple_gate=v_w_ple_gate, w_ple_proj=v_w_ple_proj, g_final=v_g_final)
    order = ["g_mix", "w_in", "conv_w", "conv_b", "dt_bias", "a_log", "d_skip", "g_ssd", "sinks", "w_attn_br", "w_ssd_br", "w_o",
             "g_ffn", "w_gate", "w_up", "w_down", "g_ple", "w_ple_gate", "w_ple_proj", "g_final"]
    big = ["w_in", "conv_w", "w_attn_br", "w_ssd_br", "w_o", "w_gate", "w_up", "w_down", "w_ple_gate", "w_ple_proj"]

    x2 = x.reshape(T, D)
    p2 = p.reshape(T, PLE_DIM)
    tgt = loss_target.reshape(T, D)
    posf = positions.reshape(T, 1).astype(F32)
    inv = ROPE_THETA ** (-np.arange(HEAD_DIM // 2, dtype=np.float32) * 2.0 / HEAD_DIM)
    inv128 = jnp.asarray(np.tile(inv, 128 // (HEAD_DIM // 2)).reshape(1, 128).astype(np.float32))
    transposed = ("w_in", "w_gate", "w_up")

    def shard2d(a, n):
        a = a.reshape(a.shape[-2:])
        return a.T if n in transposed else a

    sh = {n: shard2d(W[n], n) for n in big}

    del _PENDING[:]
    me = 4 * lax.axis_index("x") + 2 * lax.axis_index("y") + lax.axis_index("c")
    groups = (("w_in",), ("conv_w", "w_attn_br", "w_ssd_br", "w_o"), ("w_gate", "w_up", "w_down"), ("w_ple_gate", "w_ple_proj"))
    send = {n: sh[n] if n == "conv_w" else sh[n].astype(BF16) for n in big}
    started, prev = [], None
    for gi, grp in enumerate(groups):
        zones = [lax.dynamic_update_index_in_dim(lax.empty((N_DEV,) + send[n].shape, send[n].dtype), send[n], me, 0) for n in grp]
        h = split_start("gather_start_%d" % gi, "gather", ICI_SAME_CORE, [], lands=zones, after=prev)
        prev = h["token"]
        started.append(h)
    gathered, fwd = {}, {}

    def forward_start(gi, after):
        _, lands = split_wait("gather_wait_%d" % gi, started[gi], after)
        fwd[gi] = split_start("forward_start_%d" % gi, "forward", FORWARD_BLOCKS, [], lands=lands)

    def forward_wait(gi, after):
        _, full = split_wait("forward_wait_%d" % gi, fwd[gi], after)
        gathered.update(zip(groups[gi], full))

    u = rms_fwd("norm_mix", x2, g_mix)
    forward_start(0, u)
    forward_wait(0, u)
    forward_start(1, u)
    winp = _win_to_padded(gathered["w_in"])
    dtb = _pad128(dt_bias)
    alog = _pad128(a_log)
    dfull = jnp.repeat(d_skip.reshape(SSM_HEADS), HEAD_DIM).reshape(1, D_INNER)

    projp = mm_nt("in_proj", u, winp, 640)
    attn, qr, kr = attn_fwd(projp, posf, inv128, sinks)
    forward_wait(1, attn)
    convw = jnp.transpose(gathered["conv_w"], (1, 0, 2)).reshape(CONV_WIDTH, CONV_DIM)
    wab = gathered["w_attn_br"]
    wsb = gathered["w_ssd_br"].reshape(D, D)
    wo = gathered["w_o"].reshape(D, D)
    xbc = conv_fwd(projp, convw, conv_b)
    y, states = ssd_fwd(xbc, projp, dtb, alog, dfull)
    yn = gnorm_fwd(y, projp, g_ssd)
    out_a = mm_nn_colblk("attn_br", attn, wab)
    out_s = mm_nn("ssd_br", yn, wsb, 512)
    forward_start(2, out_s)
    merged = merge_fwd(projp, out_a, out_s)
    h1 = mm_nn("o_proj", merged, wo, 512, residual=x2)
    f = rms_fwd("norm_ffn", h1, g_ffn)
    forward_wait(2, f)
    wgt, wut, wd = (gathered[n].reshape(FFN_HIDDEN, D) for n in ("w_gate", "w_up", "w_down"))
    gate, up, act = ffn_up(f, wgt, wut)
    forward_start(3, act)
    h2 = mm_nn_red("ffn_down", act, wd, 1024, FFN_TILE, residual=h1)
    r = rms_fwd("norm_ple", h2, g_ple)
    forward_wait(3, r)
    wpg = gathered["w_ple_gate"].reshape(D, D)
    wpp = gathered["w_ple_proj"]
    pg = mm_nn("ple_gate", r, wpg, 512)
    pp = mm_nn_colblk("ple_proj", p2, wpp)
    loss_v, dh3, dpg, dpp, dg_final = head_fwd_bwd(h2, pg, pp, g_final.reshape(1, D), tgt)

    gw = {}
    scat = []

    def scatter_start(names):
        scat.append((names, split_start("scatter_start_%d" % len(scat), "scatter", ALL_PEERS, [gw[n] for n in names])))

    gw["w_ple_proj"] = mm_tn_colblk("dw_ple_proj", p2, dpp, PLE_DIM)
    dr = mm_nt("d_ple_gate", dpg, wpg, 512)
    gw["w_ple_gate"] = mm_tn("dw_ple_gate", r, dpg, 512, 1024).reshape(N_DEV, D // N_DEV, D)
    scatter_start(("w_ple_proj", "w_ple_gate"))
    dh2, dh2b, dg_ple = rms_bwd("norm_ple_bwd", h2, g_ple, dr, dh3)
    dgate, dup = ffn_down_bwd(dh2b, wd, gate, up)
    per = FFN_HIDDEN // N_DEV
    gw["w_down"] = mm_tn("dw_down", act, dh2b, FFN_TILE, 1024).reshape(N_DEV, per, D)
    gw["w_gate"] = mm_tn("dw_gate", dgate, f, FFN_TILE, 1024).reshape(N_DEV, per, D)
    gw["w_up"] = mm_tn("dw_up", dup, f, FFN_TILE, 1024).reshape(N_DEV, per, D)
    scatter_start(("w_down", "w_gate", "w_up"))
    df = ffn_up_bwd(dgate, dup, wgt, wut)
    dh1, dh1b, dg_ffn = rms_bwd("norm_ffn_bwd", h1, g_ffn, df, dh2)
    dmerged = mm_nt("d_o_proj", dh1b, wo, 512)
    gw["w_o"] = mm_tn("dw_o", merged, dh1b, 512, 1024).reshape(N_DEV, D // N_DEV, D)
    dout_a, dout_s, dga, dgs = merge_bwd(projp, out_a, out_s, dmerged)
    gw["w_ssd_br"] = mm_tn("dw_ssd_br", yn, dout_s, 512, 1024).reshape(N_DEV, D // N_DEV, D)
    gw["w_attn_br"] = mm_tn_colblk("dw_attn_br", attn, dout_a, D // N_DEV)
    scatter_start(("w_o", "w_ssd_br", "w_attn_br"))
    dyn = mm_nt("d_ssd_br", dout_s, wsb, 512)
    dattn = attn_br_bwd(dout_a, wab)
    dy, dz, dg_ssd = gnorm_bwd(y, projp, g_ssd, dyn)
    dxs, dbm, dcm, ddt, dal, ddsk, ddtb = ssd_bwd(xbc, projp, dtb, alog, dfull, states, dy)
    dx_x, dwc_x, dbc_x = conv_bwd("conv_bwd_x", projp, dxs, convw, conv_b, 0)
    dx_b, dwc_b, dbc_b = conv_bwd("conv_bwd_b", projp, dbm, convw, conv_b, D_INNER)
    dx_c, dwc_c, dbc_c = conv_bwd("conv_bwd_c", projp, dcm, convw, conv_b, D_INNER + SSM_GROUPS * D_STATE)
    dq, dk, dv, dsk = attn_bwd(qr, kr, projp, dattn, posf, inv128, sinks)
    dproj = jnp.concatenate([dz, dga, dgs, dx_x, dx_b, dx_c, dq, dk.astype(BF16), dv.astype(BF16), ddt], axis=1)
    gw_in = _padded_to_win(mm_tn("dw_in", dproj, u, 640, 1024))
    pair = split_start("pair_start", "pair", FORWARD_BLOCKS, [gw_in])
    dconvw = jnp.concatenate([dwc_x, dwc_b, dwc_c], axis=1)
    gw["conv_w"] = jnp.transpose(dconvw.reshape(CONV_WIDTH, N_DEV, CONV_DIM // N_DEV), (1, 0, 2))
    scatter_start(("conv_w",))
    tiles = T // _row_tile(T)
    first = max(tiles // 2, 1)
    du = mm_nn_red("d_in_proj_a", dproj, winp, 1024, 640, rows=(0, first))
    (gw_in,), (sibling_part,) = split_wait("pair_wait", pair, du)
    pair_slots = jnp.stack([jnp.bitwise_xor(me, k) for k in FORWARD_BLOCKS]).astype(jnp.int32)
    core = split_start("core_start", "scatter_core", ICI_SAME_CORE, [pair_sum("pair_sum_w_in", gw_in, pair_slots, sibling_part)])
    if first < tiles:
        du = mm_nn_red("d_in_proj_b", dproj, winp, 1024, 640, rows=(first, tiles - first), prev=du)
    gx, _, dg_mix = rms_bwd("norm_mix_bwd", x2, g_mix, du, dh1)

    small_g = dict(loss=loss_v[:, :1], g_mix=dg_mix, conv_b=jnp.concatenate([dbc_x, dbc_b, dbc_c], axis=1), dt_bias=ddtb,
                   a_log=dal, d_skip=ddsk, g_ssd=dg_ssd, sinks=dsk, g_ffn=dg_ffn, g_ple=dg_ple, g_final=dg_final)
    vec = _small_vec(small_g)
    small = split_start("small_start", "gather", ALL_PEERS, [],
                        lands=[lax.dynamic_update_index_in_dim(lax.empty((N_DEV,) + vec.shape, F32), vec, me, 0)])

    res = {}
    after = [gx]
    for si, (names, h) in enumerate(scat):
        srcs, lands = split_wait("scatter_wait_%d" % si, h, after)
        for n, mine, arrived in zip(names, srcs, lands):
            res[n] = adamw("adamw_" + n, arrived, sh[n], shard2d(Mo[n], n), shard2d(Vo[n], n), own=mine, own_slot=me)
        after = [res[n][0] for n in names]
    zero = jnp.zeros((1, 1), F32)
    _, (vec_parts,) = split_wait("small_wait", small, [res[n][0] for n in res])
    sres = adamw("adamw_small", vec_parts, _small_vec({**W, "loss": zero}), _small_vec({**Mo, "loss": zero}),
                 _small_vec({**Vo, "loss": zero}))
    ssplit = [_small_split(a) for a in sres]

    (pair_sums,), (arrived,) = split_wait("core_wait", core, [sres[0]])
    res["w_in"] = adamw("adamw_w_in", arrived, sh["w_in"], shard2d(Mo["w_in"], "w_in"), shard2d(Vo["w_in"], "w_in"),
                        own=pair_sums, own_slot=0)
    loss = ssplit[0]["loss"].reshape(())
    for n in order:
        if n not in res:
            res[n] = tuple(s[n].reshape(W[n].shape) for s in ssplit)
        else:
            res[n] = tuple((a.T if n in transposed else a).reshape(W[n].shape) for a in res[n])
    outs = [loss, gx.reshape(x.shape)]
    for k in range(4):
        outs += [res[n][k] for n in order]
    return tuple(outs)
```

```python
import functools

import numpy as np
import jax
import jax.numpy as jnp
from jax import lax
from jax.experimental import pallas as pl
from jax.experimental.pallas import tpu as pltpu

F32 = jnp.float32
BF16 = jnp.bfloat16

N_DEV = 8
D_MODEL = 2048
HEAD_DIM = 64
ATTN_HEADS = 16
KV_HEADS = 4
Q_DIM = 1024
KV_DIM = 256
BLK = 128
D_INNER = 2048
SSM_HEADS = 32
SSM_GROUPS = 4
HEADS_PER_GROUP = 8
D_STATE = 128
CONV_WIDTH = 4
CONV_DIM = 3072
FFN_HIDDEN = 5632
PLE_DIM = 256
IN_DIM = 10784
NORM_EPS = 1e-6
SSM_NORM_EPS = 1e-5
ROPE_THETA = 10000.0

OFF_Z, OFF_GA, OFF_GS, OFF_XBC, OFF_Q, OFF_K, OFF_V, OFF_DT = 0, 2048, 4096, 6144, 9216, 10240, 10496, 10752
IN_PAD = 10880
DT_PAD = 128
SEG = dict(q=(0, 1024), k=(1024, 256), v=(1280, 256), z=(1536, 2048), xbc=(3584, 3072), dt=(6656, 32),
           ga=(6688, 2048), gs=(8736, 2048))

ADAM_LR, ADAM_B1, ADAM_B2, ADAM_EPS, ADAM_WD, ADAM_STEP = 0.001, 0.9, 0.999, 1e-08, 0.01, 10

VMEM_LIMIT = 56 * 1024 * 1024

NN = (((1,), (0,)), ((), ()))
NT = (((1,), (1,)), ((), ()))
TN = (((0,), (0,)), ((), ()))


_PENDING = []


def _raw_call(body, **kw):
    return pl.pallas_call(body, **kw)


def _pcall(body, **kw):
    if "in_specs" not in kw:
        return _raw_call(body, **kw)
    deps = list(_PENDING)
    del _PENDING[:]
    if not deps:
        return _raw_call(body, **kw)
    n_in = len(kw["in_specs"])

    def tied(*refs):
        return body(*refs[:n_in], *refs[n_in + len(deps):])

    kw["in_specs"] = list(kw["in_specs"]) + [pl.BlockSpec(memory_space=pl.ANY)] * len(deps)
    call = _raw_call(tied, **kw)
    return lambda *ops: call(*ops, *deps)


def _cparams(sem=None):
    if sem is None:
        return pltpu.CompilerParams(vmem_limit_bytes=VMEM_LIMIT)
    return pltpu.CompilerParams(vmem_limit_bytes=VMEM_LIMIT, dimension_semantics=sem)


def _dot(a, b, dn):
    return lax.dot_general(a.astype(BF16), b.astype(BF16), dn, preferred_element_type=F32)


def _sigmoid(x):
    return 1.0 / (1.0 + jnp.exp(-x))


def _silu(x):
    return x * _sigmoid(x)


def _dsilu(x):
    s = _sigmoid(x)
    return s * (1.0 + x * (1.0 - s))


def _matmul(name, pairs, pair_specs, dn, grid, out_shapes, out_specs, nred=1, extra=(), extra_specs=(),
            epilogue=None, acc_shape=None, alias=None):
    n_in = 2 * len(pairs) + len(extra)
    n_out = len(out_shapes)

    def body(*refs):
        ins = refs[:2 * len(pairs)]
        ex = [r for r, sp in zip(refs[2 * len(pairs):n_in], extra_specs) if sp.memory_space != pl.ANY]
        outs = refs[n_in:n_in + n_out]

        def prod():
            s = None
            for p in range(len(pairs)):
                d = _dot(ins[2 * p][...], ins[2 * p + 1][...], dn)
                s = d if s is None else s + d
            return s

        def finish(val):
            if epilogue is None:
                outs[0][...] = val.astype(outs[0].dtype)
            else:
                res = epilogue(val, *[e[...] for e in ex])
                for o, r in zip(outs, res):
                    o[...] = r.astype(o.dtype)

        if nred == 1:
            finish(prod())
        else:
            acc = refs[n_in + n_out]
            k = pl.program_id(len(grid) - 1)

            @pl.when(k == 0)
            def _():
                acc[...] = jnp.zeros_like(acc)

            acc[...] += prod()

            @pl.when(k == nred - 1)
            def _():
                finish(acc[...])

    operands = []
    specs = []
    for (a, b), (sa, sb) in zip(pairs, pair_specs):
        operands += [a, b]
        specs += [sa, sb]
    operands += list(extra)
    specs += list(extra_specs)
    scratch = [pltpu.VMEM(acc_shape, F32)] if nred > 1 else []
    sem = ("arbitrary",) * len(grid)
    res = _pcall(body, name=name, grid=grid, in_specs=specs, out_specs=list(out_specs), input_output_aliases=dict(alias or {}),
                 out_shape=list(out_shapes), scratch_shapes=scratch, compiler_params=_cparams(sem))(*operands)
    return res


def _sds(shape, dtype):
    return jax.ShapeDtypeStruct(shape, dtype)


def _row_tile(T):
    return min(1024, T)


def mm_nn(name, a, b, tn, out_dtype=F32, residual=None):
    M, K = a.shape
    N = b.shape[1]
    tm = _row_tile(M)
    grid = (M // tm, N // tn)
    extra, especs, epi = (), (), None
    if residual is not None:
        extra = (residual,)
        especs = (pl.BlockSpec((tm, tn), lambda i, n: (i, n)),)
        epi = lambda v, r: (v + r,)
    return _matmul(name, [(a, b)], [(pl.BlockSpec((tm, K), lambda i, n: (i, 0)), pl.BlockSpec((K, tn), lambda i, n: (0, n)))],
                   NN, grid, [_sds((M, N), out_dtype)], [pl.BlockSpec((tm, tn), lambda i, n: (i, n))],
                   extra=extra, extra_specs=especs, epilogue=epi)[0]


def mm_nn_colblk(name, a, b, out_dtype=F32):
    M, K = a.shape
    J, _, nb = b.shape
    tm = _row_tile(M)
    grid = (M // tm, J)
    return _matmul(name, [(a, b)], [(pl.BlockSpec((tm, K), lambda i, j: (i, 0)), pl.BlockSpec((None, K, nb), lambda i, j: (j, 0, 0)))],
                   NN, grid, [_sds((M, J * nb), out_dtype)], [pl.BlockSpec((tm, nb), lambda i, j: (i, j))])[0]


def mm_nt(name, a, w, tr, out_dtype=F32):
    M, C = a.shape
    R = w.shape[0]
    tm = _row_tile(M)
    grid = (M // tm, R // tr)
    return _matmul(name, [(a, w)], [(pl.BlockSpec((tm, C), lambda i, r: (i, 0)), pl.BlockSpec((tr, C), lambda i, r: (r, 0)))],
                   NT, grid, [_sds((M, R), out_dtype)], [pl.BlockSpec((tm, tr), lambda i, r: (i, r))])[0]


def mm_nt_red(name, a, w, tr, tk, out_dtype=F32):
    M, C = a.shape
    R = w.shape[0]
    tm = _row_tile(M)
    nk = C // tk
    grid = (M // tm, R // tr, nk)
    return _matmul(name, [(a, w)], [(pl.BlockSpec((tm, tk), lambda i, r, k: (i, k)), pl.BlockSpec((tr, tk), lambda i, r, k: (r, k)))],
                   NT, grid, [_sds((M, R), out_dtype)], [pl.BlockSpec((tm, tr), lambda i, r, k: (i, r))],
                   nred=nk, acc_shape=(tm, tr))[0]


def mm_nn_red(name, a, b, tn, tk, out_dtype=F32, residual=None, rows=None, prev=None, tm=None):
    M, K = a.shape
    N = b.shape[1]
    tm = min(tm or _row_tile(M), M)
    nk = K // tk
    i0, ni = (0, M // tm) if rows is None else rows
    grid = (ni, N // tn, nk)
    ospec = pl.BlockSpec((tm, tn), lambda i, n, k: (i + i0, n))
    extra, especs, epi = [], [], None
    if residual is not None:
        extra, especs, epi = [residual], [ospec], (lambda v, r, *_: (v + r,))
    alias = {}
    if prev is not None:
        alias = {2 + len(extra): 0}
        extra, especs = extra + [prev], especs + [_ANY]
        epi = epi or (lambda v, *_: (v,))
    return _matmul(name, [(a, b)], [(pl.BlockSpec((tm, tk), lambda i, n, k: (i + i0, k)), pl.BlockSpec((tk, tn), lambda i, n, k: (k, n)))],
                   NN, grid, [_sds((M, N), out_dtype)], [ospec], nred=nk, acc_shape=(tm, tn),
                   extra=extra, extra_specs=especs, epilogue=epi, alias=alias)[0]


def mm_tn(name, x, dy, tr, tc, out_dtype=BF16):
    M, R = x.shape
    C = dy.shape[1]
    grid = (R // tr, C // tc)
    return _matmul(name, [(x, dy)], [(pl.BlockSpec((M, tr), lambda r, c: (0, r)), pl.BlockSpec((M, tc), lambda r, c: (0, c)))],
                   TN, grid, [_sds((R, C), out_dtype)], [pl.BlockSpec((tr, tc), lambda r, c: (r, c))])[0]


def mm_tn_colblk(name, x, dy, nb, out_dtype=BF16):
    M, R = x.shape
    J = dy.shape[1] // nb
    grid = (J,)
    return _matmul(name, [(x, dy)], [(pl.BlockSpec((M, R), lambda j: (0, 0)), pl.BlockSpec((M, nb), lambda j: (0, j)))],
                   TN, grid, [_sds((J, R, nb), out_dtype)], [pl.BlockSpec((None, R, nb), lambda j: (j, 0, 0))])[0]


def _rows(T):
    return min(256, T)


def rms_fwd(name, x, g, eps=NORM_EPS):
    T, D = x.shape
    tm = _rows(T)

    def body(x_ref, g_ref, o_ref):
        xv = x_ref[...]
        r = lax.rsqrt(jnp.mean(xv * xv, axis=-1, keepdims=True) + eps)
        o_ref[...] = (xv * r * g_ref[...]).astype(BF16)

    return _pcall(body, name=name, grid=(T // tm,),
                  in_specs=[pl.BlockSpec((tm, D), lambda i: (i, 0)), pl.BlockSpec((1, D), lambda i: (0, 0))],
                  out_specs=pl.BlockSpec((tm, D), lambda i: (i, 0)), out_shape=_sds((T, D), BF16),
                  compiler_params=_cparams(("arbitrary",)))(x, g)


def rms_bwd(name, x, g, dy, dres, eps=NORM_EPS):
    T, D = x.shape
    tm = _rows(T)

    def body(x_ref, g_ref, dy_ref, dr_ref, dx_ref, dxb_ref, dg_ref):
        i = pl.program_id(0)
        xv = x_ref[...]
        r = lax.rsqrt(jnp.mean(xv * xv, axis=-1, keepdims=True) + eps)
        xh = xv * r
        dyv = dy_ref[...]
        gd = dyv * g_ref[...]
        dx = r * (gd - xh * jnp.mean(gd * xh, axis=-1, keepdims=True)) + dr_ref[...]
        dx_ref[...] = dx
        dxb_ref[...] = dx.astype(BF16)

        @pl.when(i == 0)
        def _():
            dg_ref[...] = jnp.zeros_like(dg_ref)

        dg_ref[...] += jnp.sum(dyv * xh, axis=0, keepdims=True)

    row = pl.BlockSpec((tm, D), lambda i: (i, 0))
    vec = pl.BlockSpec((1, D), lambda i: (0, 0))
    return _pcall(body, name=name, grid=(T // tm,), in_specs=[row, vec, row, row], out_specs=[row, row, vec],
                  out_shape=[_sds((T, D), F32), _sds((T, D), BF16), _sds((1, D), F32)],
                  compiler_params=_cparams(("arbitrary",)))(x, g, dy, dres)


def gnorm_fwd(y, projp, g):
    T, D = y.shape
    tm = _rows(T)

    def body(y_ref, z_ref, g_ref, o_ref):
        yz = y_ref[...] * _silu(z_ref[...])
        r = lax.rsqrt(jnp.mean(yz * yz, axis=-1, keepdims=True) + SSM_NORM_EPS)
        o_ref[...] = (yz * r * g_ref[...]).astype(BF16)

    row = pl.BlockSpec((tm, D), lambda i: (i, 0))
    return _pcall(body, name="gnorm_fwd", grid=(T // tm,),
                  in_specs=[row, pl.BlockSpec((tm, D), lambda i: (i, OFF_Z // D)), pl.BlockSpec((1, D), lambda i: (0, 0))],
                  out_specs=row, out_shape=_sds((T, D), BF16), compiler_params=_cparams(("arbitrary",)))(y, projp, g)


def gnorm_bwd(y, projp, g, dyn):
    T, D = y.shape
    tm = _rows(T)

    def body(y_ref, z_ref, g_ref, dyn_ref, dy_ref, dz_ref, dg_ref):
        i = pl.program_id(0)
        yv, zv = y_ref[...], z_ref[...]
        sz = _silu(zv)
        yz = yv * sz
        r = lax.rsqrt(jnp.mean(yz * yz, axis=-1, keepdims=True) + SSM_NORM_EPS)
        xh = yz * r
        dv = dyn_ref[...]
        gd = dv * g_ref[...]
        dyz = r * (gd - xh * jnp.mean(gd * xh, axis=-1, keepdims=True))
        dy_ref[...] = dyz * sz
        dz_ref[...] = (dyz * yv * _dsilu(zv)).astype(BF16)

        @pl.when(i == 0)
        def _():
            dg_ref[...] = jnp.zeros_like(dg_ref)

        dg_ref[...] += jnp.sum(dv * xh, axis=0, keepdims=True)

    row = pl.BlockSpec((tm, D), lambda i: (i, 0))
    vec = pl.BlockSpec((1, D), lambda i: (0, 0))
    return _pcall(body, name="gnorm_bwd", grid=(T // tm,),
                  in_specs=[row, pl.BlockSpec((tm, D), lambda i: (i, OFF_Z // D)), vec, row], out_specs=[row, row, vec],
                  out_shape=[_sds((T, D), F32), _sds((T, D), BF16), _sds((1, D), F32)],
                  compiler_params=_cparams(("arbitrary",)))(y, projp, g, dyn)


def merge_fwd(projp, out_a, out_s):
    T, D = out_a.shape
    tm = _rows(T)

    def body(ga_ref, gs_ref, a_ref, s_ref, o_ref):
        o_ref[...] = (_sigmoid(ga_ref[...]) * a_ref[...] + _sigmoid(gs_ref[...]) * s_ref[...]).astype(BF16)

    row = pl.BlockSpec((tm, D), lambda i: (i, 0))
    return _pcall(body, name="merge_fwd", grid=(T // tm,),
                  in_specs=[pl.BlockSpec((tm, D), lambda i: (i, OFF_GA // D)), pl.BlockSpec((tm, D), lambda i: (i, OFF_GS // D)), row, row],
                  out_specs=row, out_shape=_sds((T, D), BF16), compiler_params=_cparams(("arbitrary",)))(projp, projp, out_a, out_s)


def merge_bwd(projp, out_a, out_s, dmerged):
    T, D = out_a.shape
    tm = _rows(T)

    def body(ga_ref, gs_ref, a_ref, s_ref, dm_ref, da_ref, ds_ref, dga_ref, dgs_ref):
        dm = dm_ref[...]
        sa, ss = _sigmoid(ga_ref[...]), _sigmoid(gs_ref[...])
        da_ref[...] = (dm * sa).astype(BF16)
        ds_ref[...] = (dm * ss).astype(BF16)
        dga_ref[...] = (dm * a_ref[...] * sa * (1.0 - sa)).astype(BF16)
        dgs_ref[...] = (dm * s_ref[...] * ss * (1.0 - ss)).astype(BF16)

    row = pl.BlockSpec((tm, D), lambda i: (i, 0))
    return _pcall(body, name="merge_bwd", grid=(T // tm,),
                  in_specs=[pl.BlockSpec((tm, D), lambda i: (i, OFF_GA // D)), pl.BlockSpec((tm, D), lambda i: (i, OFF_GS // D)), row, row, row],
                  out_specs=[row] * 4, out_shape=[_sds((T, D), BF16)] * 4,
                  compiler_params=_cparams(("arbitrary",)))(projp, projp, out_a, out_s, dmerged)


def head_fwd_bwd(h2, pg, pp, g_final, target):
    T, D = h2.shape
    tm = _rows(T)

    def body(h_ref, pg_ref, pp_ref, g_ref, t_ref, loss_ref, dh_ref, dpg_ref, dpp_ref, dg_ref):
        i = pl.program_id(0)
        s = _sigmoid(pg_ref[...])
        ppv = pp_ref[...]
        h3 = h_ref[...] + s * ppv
        r = lax.rsqrt(jnp.mean(h3 * h3, axis=-1, keepdims=True) + NORM_EPS)
        xh = h3 * r
        gv = g_ref[...]
        e = xh * gv - t_ref[...]
        dyo = e * (1.0 / D)
        gd = dyo * gv
        dh = r * (gd - xh * jnp.mean(gd * xh, axis=-1, keepdims=True))
        dh_ref[...] = dh
        dpg_ref[...] = (dh * ppv * s * (1.0 - s)).astype(BF16)
        dpp_ref[...] = (dh * s).astype(BF16)

        @pl.when(i == 0)
        def _():
            dg_ref[...] = jnp.zeros_like(dg_ref)
            loss_ref[...] = jnp.zeros_like(loss_ref)

        dg_ref[...] += jnp.sum(dyo * xh, axis=0, keepdims=True)
        part = 0.5 * jnp.sum(jnp.mean(e * e, axis=-1, keepdims=True), axis=0, keepdims=True)
        loss_ref[...] += jnp.broadcast_to(part, loss_ref.shape)

    row = pl.BlockSpec((tm, D), lambda i: (i, 0))
    vec = pl.BlockSpec((1, D), lambda i: (0, 0))
    return _pcall(body, name="head_fwd_bwd", grid=(T // tm,), in_specs=[row, row, row, vec, row],
                  out_specs=[pl.BlockSpec((1, 128), lambda i: (0, 0)), row, row, row, vec],
                  out_shape=[_sds((1, 128), F32), _sds((T, D), F32), _sds((T, D), BF16), _sds((T, D), BF16), _sds((1, D), F32)],
                  compiler_params=_cparams(("arbitrary",)))(h2, pg, pp, g_final, target)


FFN_TILE = 512


def ffn_up(f, wgt, wut):
    T, D = f.shape
    H = wgt.shape[0]
    tm = _row_tile(T)

    def body(f_ref, wg_ref, wu_ref, g_ref, u_ref, a_ref):
        fv = f_ref[...]
        g = _dot(fv, wg_ref[...], NT)
        u = _dot(fv, wu_ref[...], NT)
        g_ref[...] = g
        u_ref[...] = u
        a_ref[...] = (_silu(g) * u).astype(BF16)

    wspec = pl.BlockSpec((FFN_TILE, D), lambda i, j: (j, 0))
    ospec = pl.BlockSpec((tm, FFN_TILE), lambda i, j: (i, j))
    return _pcall(body, name="ffn_up", grid=(T // tm, H // FFN_TILE), in_specs=[pl.BlockSpec((tm, D), lambda i, j: (i, 0)), wspec, wspec],
                  out_specs=[ospec] * 3, out_shape=[_sds((T, H), F32), _sds((T, H), F32), _sds((T, H), BF16)],
                  compiler_params=_cparams(("arbitrary", "arbitrary")))(f, wgt, wut)


def ffn_down_bwd(dh2b, wd, gate, up):
    T, D = dh2b.shape
    H = wd.shape[0]
    tm = _row_tile(T)
    ospec = pl.BlockSpec((tm, FFN_TILE), lambda i, j: (i, j))

    def epi(da, g, u):
        return (da * u * _dsilu(g), da * _silu(g))

    return _matmul("ffn_down_bwd", [(dh2b, wd)],
                   [(pl.BlockSpec((tm, D), lambda i, j: (i, 0)), pl.BlockSpec((FFN_TILE, D), lambda i, j: (j, 0)))],
                   NT, (T // tm, H // FFN_TILE), [_sds((T, H), BF16)] * 2, [ospec, ospec],
                   extra=(gate, up), extra_specs=(ospec, ospec), epilogue=epi)


def ffn_up_bwd(dgate, dup, wgt, wut):
    T, H = dgate.shape
    D = wgt.shape[1]
    tm = min(512, T)
    tn = 512
    aspec = pl.BlockSpec((tm, H), lambda i, n: (i, 0))
    wspec = pl.BlockSpec((H, tn), lambda i, n: (0, n))
    return _matmul("ffn_up_bwd", [(dgate, wgt), (dup, wut)], [(aspec, wspec), (aspec, wspec)], NN, (T // tm, D // tn),
                   [_sds((T, D), F32)], [pl.BlockSpec((tm, tn), lambda i, n: (i, n))])[0]


def attn_br_bwd(dout_a, wab):
    T, D = dout_a.shape
    J, R, nb = wab.shape
    tm = _row_tile(T)
    return _matmul("attn_br_bwd", [(dout_a, wab)],
                   [(pl.BlockSpec((tm, nb), lambda i, j: (i, j)), pl.BlockSpec((None, R, nb), lambda i, j: (j, 0, 0)))],
                   NT, (T // tm, J), [_sds((T, R), BF16)], [pl.BlockSpec((tm, R), lambda i, j: (i, 0))], nred=J, acc_shape=(tm, R))[0]


def _adam_math(w, g, m, v):
    m2 = ADAM_B1 * m + (1.0 - ADAM_B1) * g
    v2 = ADAM_B2 * v + (1.0 - ADAM_B2) * (g * g)
    m_hat = m2 / (1.0 - ADAM_B1 ** ADAM_STEP)
    v_hat = v2 / (1.0 - ADAM_B2 ** ADAM_STEP)
    delta = -ADAM_LR * (m_hat / (jnp.sqrt(v_hat) + ADAM_EPS) + ADAM_WD * w)
    return delta, m2, v2


def _sum_partials(own, parts):
    g = None if own is None else own.astype(F32)
    if parts is not None:
        for s in range(parts.shape[0]):
            t = parts[s].astype(F32)
            g = t if g is None else g + t
    return g


def adamw(name, parts, w, m, v, own=None, own_slot=None):
    R, C = w.shape
    tr, tc = R, C
    for cand in (256, 176, 128, 64, 32, 16, 8):
        if R % cand == 0 and R > cand:
            tr = cand
            break
    if tr == R and R > 256:
        tc = 256
    given = [a for a in (parts, own) if a is not None]
    pre = own_slot is not None

    def body(*refs):
        refs = refs[1:] if pre else refs
        p_ref = refs[0] if parts is not None else None
        o_ref = refs[len(given) - 1] if own is not None else None
        w_ref, m_ref, v_ref, g_ref, d_ref, m2_ref, v2_ref = refs[-7:]
        g = _sum_partials(None if o_ref is None else o_ref[...], p_ref)
        d, m2, v2 = _adam_math(w_ref[...], g, m_ref[...], v_ref[...])
        g_ref[...] = g
        d_ref[...] = d
        m2_ref[...] = m2
        v2_ref[...] = v2

    blk = pl.BlockSpec((tr, tc), lambda i, j, *s: (i, j))
    specs = [] if parts is None else [pl.BlockSpec((parts.shape[0], tr, tc), lambda i, j, *s: (0, i, j))]
    if own is not None:
        specs.append(pl.BlockSpec((None, tr, tc), lambda i, j, s: (s[0], i, j)) if pre else blk)
    specs += [blk] * 3
    grid = (R // tr, C // tc)
    out_shape = [_sds((R, C), F32)] * 4
    params = _cparams(("arbitrary", "arbitrary"))
    if not pre:
        return _pcall(body, name=name, grid=grid, in_specs=specs, out_specs=[blk] * 4, out_shape=out_shape,
                      compiler_params=params)(*given, w, m, v)
    spec = pltpu.PrefetchScalarGridSpec(num_scalar_prefetch=1, grid=grid, in_specs=specs, out_specs=[blk] * 4)
    return _pcall(body, name=name, grid_spec=spec, out_shape=out_shape,
                  compiler_params=params)(jnp.asarray(own_slot, jnp.int32).reshape(1), *given, w, m, v)


_HBM = pl.BlockSpec(memory_space=pltpu.HBM)
_SEM = pl.BlockSpec(memory_space=pltpu.SEMAPHORE)
_ANY = pl.BlockSpec(memory_space=pl.ANY)
_SPLIT_PARAMS = dict(compiler_params=pltpu.CompilerParams(has_side_effects=pltpu.SideEffectType.DATAFLOW_SIDE_EFFECTING))
ICI_SAME_CORE = (2, 4, 6)
ALL_PEERS = (1, 2, 3, 4, 5, 6, 7)
LAND_SLOTS = {"gather": N_DEV, "scatter": N_DEV - 1, "pair": 4, "scatter_core": 3}


def _mesh_pos():
    x, y, c = lax.axis_index("x"), lax.axis_index("y"), lax.axis_index("c")
    return x, y, c, 4 * x + 2 * y + c


def _peer_of(k, x, y, c):
    px = 1 - x if k & 4 else x
    py = 1 - y if k & 2 else y
    pc = 1 - c if k & 1 else c
    return (px, py, pc), 4 * px + 2 * py + pc


def _split_copies(mode, ks, srcs, lands, send_sems, recv_sems):
    x, y, c, me = _mesh_pos()
    pairs = []
    for a in range(len(lands)):
        for j, k in enumerate(ks):
            dev, peer = _peer_of(k, x, y, c)
            i = a * len(ks) + j
            if mode == "gather":
                s_out, d_out, d_in = lands[a].at[me], lands[a].at[me], lands[a].at[peer]
            elif mode == "scatter":
                s_out, d_out, d_in = srcs[a].at[peer], lands[a].at[k - 1], lands[a].at[k - 1]
            elif mode == "pair":
                dev, _ = _peer_of(1, x, y, c)
                _, theirs = _peer_of(k | 1, x, y, c)
                s_out, d_out, d_in = srcs[a].at[theirs], lands[a].at[j], lands[a].at[j]
            elif mode == "scatter_core":
                s_out, d_out, d_in = srcs[a].at[j + 1], lands[a].at[j], lands[a].at[j]
            else:
                dev, _ = _peer_of(1, x, y, c)
                _, theirs = _peer_of(k | 1, x, y, c)
                s_out, d_out, d_in = lands[a].at[peer], lands[a].at[peer], lands[a].at[theirs]
            both = [pltpu.make_async_remote_copy(src_ref=s_out, dst_ref=d, send_sem=send_sems.at[i], recv_sem=recv_sems.at[i],
                                                 device_id=dev, device_id_type=pl.DeviceIdType.MESH) for d in (d_out, d_in)]
            pairs.append(tuple(both))
    return pairs


def split_start(name, mode, ks, srcs, lands=None, after=None):
    n, nk = len(srcs) if lands is None else len(lands), len(ks)
    srcs = [pltpu.with_memory_space_constraint(s, pltpu.HBM) for s in srcs]
    if lands is None:
        shapes = [((N_DEV,) + s.shape) if mode == "gather" else ((LAND_SLOTS[mode],) + s.shape[1:]) for s in srcs]
        lands = [lax.empty(shp, s.dtype) for shp, s in zip(shapes, srcs)]
    lands = [pltpu.with_memory_space_constraint(l, pltpu.HBM) for l in lands]
    both = srcs + lands
    extra = [] if after is None else [after]

    def body(*refs):
        src_refs, land_refs = refs[:len(srcs)], refs[len(srcs):len(both)]
        send_sems, recv_sems = refs[len(both) + len(extra)], refs[len(both) + len(extra) + 1]
        token = refs[-1]
        for out, _ in _split_copies(mode, ks, src_refs, land_refs, send_sems, recv_sems):
            out.start()
        token[...] = jnp.zeros_like(token)

    out_shape = (pltpu.SemaphoreType.DMA((n * nk,)), pltpu.SemaphoreType.DMA((n * nk,)),
                 *[pltpu.HBM(a.shape, a.dtype) for a in both], _sds((8, 128), F32))
    res = _raw_call(body, name=name, out_shape=out_shape, in_specs=[_HBM] * len(both) + [_ANY] * len(extra),
                    out_specs=(_SEM, _SEM, *[_HBM] * len(both), pl.BlockSpec(memory_space=pltpu.VMEM)),
                    input_output_aliases={i: 2 + i for i in range(len(both))}, **_SPLIT_PARAMS)(*both, *extra)
    _PENDING.append(res[-1])
    return dict(mode=mode, ks=ks, sems=(res[0], res[1]), srcs=list(res[2:2 + len(srcs)]),
                lands=list(res[2 + len(srcs):2 + len(both)]), token=res[-1])


def split_wait(name, h, after):
    ns = len(h["srcs"])
    both = h["srcs"] + h["lands"]
    after = list(after) if isinstance(after, (list, tuple)) else [after]

    def body(*refs):
        src_refs, land_refs = refs[:ns], refs[ns:len(both)]
        send_sems, recv_sems = refs[len(both)], refs[len(both) + 1]
        for out, arriving in _split_copies(h["mode"], h["ks"], src_refs, land_refs, send_sems, recv_sems):
            out.wait_send()
            arriving.wait_recv()

    res = _raw_call(body, name=name, out_shape=tuple(pltpu.HBM(a.shape, a.dtype) for a in both),
                    in_specs=[_HBM] * len(both) + [_SEM, _SEM] + [_ANY] * len(after), out_specs=tuple([_HBM] * len(both)),
                    input_output_aliases={i: i for i in range(len(both))}, **_SPLIT_PARAMS)(*both, *h["sems"], *after)
    return list(res[:ns]), list(res[ns:])


FORWARD_BLOCKS = (0, 2, 4, 6)


def pair_sum(name, mine, slots, theirs):
    P, R, C = theirs.shape
    tc = 512

    def body(s_ref, a_ref, b_ref, o_ref):
        o_ref[...] = (a_ref[...].astype(F32) + b_ref[...].astype(F32)).astype(o_ref.dtype)

    blk = pl.BlockSpec((None, R, tc), lambda p, i, s: (p, 0, i))
    spec = pltpu.PrefetchScalarGridSpec(num_scalar_prefetch=1, grid=(P, C // tc),
                                        in_specs=[pl.BlockSpec((None, R, tc), lambda p, i, s: (s[p], 0, i)), blk], out_specs=blk)
    return _pcall(body, name=name, grid_spec=spec, out_shape=_sds((P, R, C), theirs.dtype),
                  compiler_params=_cparams(("arbitrary", "arbitrary")))(slots, mine, theirs)


def _rope_parts(pos_ref, inv_ref):
    ang = pos_ref[...] * inv_ref[...]
    return jnp.cos(ang), jnp.sin(ang)


def _rot_half(t):
    lane = lax.broadcasted_iota(jnp.int32, t.shape, 1)
    return jnp.where((lane % HEAD_DIM) < HEAD_DIM // 2, -pltpu.roll(t, 128 - HEAD_DIM // 2, 1), pltpu.roll(t, HEAD_DIM // 2, 1))


def _attn_mask(n):
    row = lax.broadcasted_iota(jnp.int32, (BLK, 2 * BLK), 0)
    col = lax.broadcasted_iota(jnp.int32, (BLK, 2 * BLK), 1)
    dist = row + BLK - col
    return (dist >= 0) & (dist < BLK) & ((n * BLK - BLK + col) >= 0)


def _attn_specs(T):
    prev = lambda n: jnp.maximum(n - 1, 0)
    kc = pl.BlockSpec((BLK, KV_DIM), lambda n: (n, OFF_K // KV_DIM))
    kp = pl.BlockSpec((BLK, KV_DIM), lambda n: (prev(n), OFF_K // KV_DIM))
    vc = pl.BlockSpec((BLK, KV_DIM), lambda n: (n, OFF_V // KV_DIM))
    vp = pl.BlockSpec((BLK, KV_DIM), lambda n: (prev(n), OFF_V // KV_DIM))
    pc = pl.BlockSpec((BLK, 1), lambda n: (n, 0))
    pp = pl.BlockSpec((BLK, 1), lambda n: (prev(n), 0))
    inv = pl.BlockSpec((1, 128), lambda n: (0, 0))
    sink = pl.BlockSpec(memory_space=pltpu.SMEM)
    return kc, kp, vc, vp, pc, pp, inv, sink


def _softmax_sink(sc, valid, sink):
    sc = jnp.where(valid, sc * (HEAD_DIM ** -0.5), -1e30)
    m = jnp.maximum(jnp.max(sc, axis=1, keepdims=True), sink)
    e = jnp.exp(sc - m)
    es = jnp.exp(sink - m)
    den = jnp.sum(e, axis=1, keepdims=True) + es
    return e / den, es / den


def attn_fwd(projp, posf, inv128, sinks):
    T = projp.shape[0]
    kc, kp, vc, vp, pc, pp, inv, sink = _attn_specs(T)

    def body(q_ref, kc_ref, kp_ref, vc_ref, vp_ref, pc_ref, pp_ref, inv_ref, sink_ref, o_ref, qr_ref, kr_ref):
        n = pl.program_id(0)
        cos_c, sin_c = _rope_parts(pc_ref, inv_ref)
        cos_p, sin_p = _rope_parts(pp_ref, inv_ref)
        valid = _attn_mask(n)
        k_c, k_p = [], []
        for s in range(KV_DIM // 128):
            t = kc_ref[:, 128 * s:128 * (s + 1)]
            k_c.append((t * cos_c + _rot_half(t) * sin_c).astype(BF16))
            kr_ref[:, 128 * s:128 * (s + 1)] = k_c[s]
            t = kp_ref[:, 128 * s:128 * (s + 1)]
            k_p.append((t * cos_p + _rot_half(t) * sin_p).astype(BF16))
        kcat, vcat = [], []
        for hk in range(KV_HEADS):
            lo = HEAD_DIM * (hk % 2)
            kcat.append(jnp.concatenate([k_p[hk // 2][:, lo:lo + HEAD_DIM], k_c[hk // 2][:, lo:lo + HEAD_DIM]], axis=0))
            vcat.append(jnp.concatenate([vp_ref[:, HEAD_DIM * hk:HEAD_DIM * (hk + 1)], vc_ref[:, HEAD_DIM * hk:HEAD_DIM * (hk + 1)]], axis=0)
                        .astype(BF16))
        q_heads = []
        for s in range(Q_DIM // 128):
            t = q_ref[:, 128 * s:128 * (s + 1)]
            qs = (t * cos_c + _rot_half(t) * sin_c).astype(BF16)
            qr_ref[:, 128 * s:128 * (s + 1)] = qs
            q_heads += [qs[:, :HEAD_DIM], qs[:, HEAD_DIM:]]
        G = ATTN_HEADS // KV_HEADS
        scores = [_dot(q_heads[hq], kcat[hq // G], NT) for hq in range(ATTN_HEADS)]
        probs = [_softmax_sink(scores[hq], valid, sink_ref[0, hq])[0] for hq in range(ATTN_HEADS)]
        outs = [_dot(probs[hq], vcat[hq // G], NN) for hq in range(ATTN_HEADS)]
        for s in range(Q_DIM // 128):
            o_ref[:, 128 * s:128 * (s + 1)] = jnp.concatenate([outs[2 * s], outs[2 * s + 1]], axis=1).astype(BF16)

    qspec = pl.BlockSpec((BLK, Q_DIM), lambda n: (n, OFF_Q // Q_DIM))
    orow = pl.BlockSpec((BLK, Q_DIM), lambda n: (n, 0))
    krow = pl.BlockSpec((BLK, KV_DIM), lambda n: (n, 0))
    return _pcall(body, name="attn_fwd", grid=(T // BLK,), in_specs=[qspec, kc, kp, vc, vp, pc, pp, inv, sink],
                  out_specs=[orow, orow, krow], out_shape=[_sds((T, Q_DIM), BF16), _sds((T, Q_DIM), BF16), _sds((T, KV_DIM), BF16)],
                  compiler_params=_cparams(("arbitrary",)))(projp, projp, projp, projp, projp, posf, posf, inv128, sinks)


def attn_bwd(qr, kr, projp, dattn, posf, inv128, sinks):
    T = projp.shape[0]
    _, _, vc, vp, pc, pp, inv, sink = _attn_specs(T)
    G = ATTN_HEADS // KV_HEADS

    def body(qr_ref, krc_ref, krp_ref, vc_ref, vp_ref, do_ref, pc_ref, pp_ref, inv_ref, sink_ref, dq_ref, dk_ref, dv_ref, dsk_ref):
        n = pl.program_id(0)

        @pl.when(n == 0)
        def _():
            dk_ref[...] = jnp.zeros_like(dk_ref)
            dv_ref[...] = jnp.zeros_like(dv_ref)
            dsk_ref[...] = jnp.zeros_like(dsk_ref)

        cos_c, sin_c = _rope_parts(pc_ref, inv_ref)
        cos_p, sin_p = _rope_parts(pp_ref, inv_ref)
        valid = _attn_mask(n)
        lane = lax.broadcasted_iota(jnp.int32, (1, 128), 1)
        kcat, vcat = [], []
        for hk in range(KV_HEADS):
            ksl = slice(HEAD_DIM * hk, HEAD_DIM * (hk + 1))
            kcat.append(jnp.concatenate([krp_ref[:, ksl], krc_ref[:, ksl]], axis=0))
            vcat.append(jnp.concatenate([vp_ref[:, ksl], vc_ref[:, ksl]], axis=0).astype(BF16))
        H = range(ATTN_HEADS)
        q_heads = [qr_ref[:, HEAD_DIM * hq:HEAD_DIM * (hq + 1)] for hq in H]
        do_heads = [do_ref[:, HEAD_DIM * hq:HEAD_DIM * (hq + 1)] for hq in H]
        soft = [_softmax_sink(_dot(q_heads[hq], kcat[hq // G], NT), valid, sink_ref[0, hq]) for hq in H]
        dps = [_dot(do_heads[hq], vcat[hq // G], NT) for hq in H]
        deltas = [jnp.sum(soft[hq][0] * dps[hq], axis=1, keepdims=True) for hq in H]
        dss = [(soft[hq][0] * (dps[hq] - deltas[hq]) * (HEAD_DIM ** -0.5)).astype(BF16) for hq in H]
        pbs = [soft[hq][0].astype(BF16) for hq in H]
        dsk = jnp.zeros((1, 128), F32)
        for hq in H:
            dsk = dsk + jnp.where(lane == hq, -jnp.sum(soft[hq][1] * deltas[hq], axis=0, keepdims=True), 0.0)
        dsk_ref[...] += dsk
        dq_heads = [_dot(dss[hq], kcat[hq // G], NN) for hq in H]
        dk_parts = [_dot(dss[hq], q_heads[hq], TN) for hq in H]
        dv_parts = [_dot(pbs[hq], do_heads[hq], TN) for hq in H]
        dk_heads = [sum(dk_parts[G * hk + 1:G * (hk + 1)], dk_parts[G * hk]) for hk in range(KV_HEADS)]
        dv_heads = [sum(dv_parts[G * hk + 1:G * (hk + 1)], dv_parts[G * hk]) for hk in range(KV_HEADS)]
        for s in range(Q_DIM // 128):
            t = jnp.concatenate([dq_heads[2 * s], dq_heads[2 * s + 1]], axis=1)
            dq_ref[:, 128 * s:128 * (s + 1)] = (t * cos_c - _rot_half(t) * sin_c).astype(BF16)
        cur = pl.ds(pl.multiple_of(n * BLK, BLK), BLK)
        prv = pl.ds(pl.multiple_of(jnp.maximum(n - 1, 0) * BLK, BLK), BLK)
        for s in range(KV_DIM // 128):
            tc = jnp.concatenate([dk_heads[2 * s][BLK:], dk_heads[2 * s + 1][BLK:]], axis=1)
            tp = jnp.concatenate([dk_heads[2 * s][:BLK], dk_heads[2 * s + 1][:BLK]], axis=1)
            cols = slice(128 * s, 128 * (s + 1))
            dk_ref[cur, cols] += tc * cos_c - _rot_half(tc) * sin_c
            dk_ref[prv, cols] += tp * cos_p - _rot_half(tp) * sin_p
            dv_ref[cur, cols] += jnp.concatenate([dv_heads[2 * s][BLK:], dv_heads[2 * s + 1][BLK:]], axis=1)
            dv_ref[prv, cols] += jnp.concatenate([dv_heads[2 * s][:BLK], dv_heads[2 * s + 1][:BLK]], axis=1)

    qrow = pl.BlockSpec((BLK, Q_DIM), lambda n: (n, 0))
    krc = pl.BlockSpec((BLK, KV_DIM), lambda n: (n, 0))
    krp = pl.BlockSpec((BLK, KV_DIM), lambda n: (jnp.maximum(n - 1, 0), 0))
    whole = pl.BlockSpec((T, KV_DIM), lambda n: (0, 0))
    return _pcall(body, name="attn_bwd", grid=(T // BLK,), in_specs=[qrow, krc, krp, vc, vp, qrow, pc, pp, inv, sink],
                  out_specs=[qrow, whole, whole, pl.BlockSpec((1, 128), lambda n: (0, 0))],
                  out_shape=[_sds((T, Q_DIM), BF16), _sds((T, KV_DIM), F32), _sds((T, KV_DIM), F32), _sds((1, 128), F32)],
                  compiler_params=_cparams(("arbitrary",)))(qr, kr, kr, projp, projp, dattn, posf, posf, inv128, sinks)


CONV_CB = 256


def _shift_down(x, s):
    row = lax.broadcasted_iota(jnp.int32, x.shape, 0)
    return jnp.where(row >= s, pltpu.roll(x, s, 0), 0.0)


def _shift_up(x, s):
    T = x.shape[0]
    row = lax.broadcasted_iota(jnp.int32, x.shape, 0)
    return jnp.where(row < T - s, pltpu.roll(x, T - s, 0), 0.0)


def _conv_pre(x, w_ref, b_ref):
    acc = x * w_ref[CONV_WIDTH - 1:CONV_WIDTH, :] + b_ref[...]
    for s in range(1, CONV_WIDTH):
        acc = acc + _shift_down(x, s) * w_ref[CONV_WIDTH - 1 - s:CONV_WIDTH - s, :]
    return acc


def conv_fwd(projp, conv_w, conv_b):
    T = projp.shape[0]

    def body(x_ref, w_ref, b_ref, o_ref):
        o_ref[...] = _silu(_conv_pre(x_ref[...], w_ref, b_ref))

    return _pcall(body, name="conv_fwd", grid=(CONV_DIM // CONV_CB,),
                  in_specs=[pl.BlockSpec((T, CONV_CB), lambda c: (0, OFF_XBC // CONV_CB + c)),
                            pl.BlockSpec((CONV_WIDTH, CONV_CB), lambda c: (0, c)), pl.BlockSpec((1, CONV_CB), lambda c: (0, c))],
                  out_specs=pl.BlockSpec((T, CONV_CB), lambda c: (0, c)), out_shape=_sds((T, CONV_DIM), F32),
                  compiler_params=_cparams(("arbitrary",)))(projp, conv_w, conv_b)


def conv_bwd(name, projp, dact, conv_w, conv_b, col0):
    T, C = dact.shape
    c0 = col0 // CONV_CB

    def body(x_ref, da_ref, w_ref, b_ref, dx_ref, dw_ref, db_ref):
        x = x_ref[...]
        dpre = da_ref[...] * _dsilu(_conv_pre(x, w_ref, b_ref))
        dx = dpre * w_ref[CONV_WIDTH - 1:CONV_WIDTH, :]
        dw_ref[CONV_WIDTH - 1:CONV_WIDTH, :] = jnp.sum(dpre * x, axis=0, keepdims=True)
        for s in range(1, CONV_WIDTH):
            i = CONV_WIDTH - 1 - s
            dx = dx + _shift_up(dpre, s) * w_ref[i:i + 1, :]
            dw_ref[i:i + 1, :] = jnp.sum(dpre * _shift_down(x, s), axis=0, keepdims=True)
        dx_ref[...] = dx.astype(BF16)
        db_ref[...] = jnp.sum(dpre, axis=0, keepdims=True)

    return _pcall(body, name=name, grid=(C // CONV_CB,),
                  in_specs=[pl.BlockSpec((T, CONV_CB), lambda c: (0, OFF_XBC // CONV_CB + c0 + c)),
                            pl.BlockSpec((T, CONV_CB), lambda c: (0, c)),
                            pl.BlockSpec((CONV_WIDTH, CONV_CB), lambda c: (0, c0 + c)), pl.BlockSpec((1, CONV_CB), lambda c: (0, c0 + c))],
                  out_specs=[pl.BlockSpec((T, CONV_CB), lambda c: (0, c)), pl.BlockSpec((CONV_WIDTH, CONV_CB), lambda c: (0, c)),
                             pl.BlockSpec((1, CONV_CB), lambda c: (0, c))],
                  out_shape=[_sds((T, C), BF16), _sds((CONV_WIDTH, C), F32), _sds((1, C), F32)],
                  compiler_params=_cparams(("arbitrary",)))(projp, dact, conv_w, conv_b)


def _softplus(x):
    return jnp.maximum(x, 0.0) + jnp.log1p(jnp.exp(-jnp.abs(x)))


def _tri(lower):
    r = lax.broadcasted_iota(jnp.int32, (BLK, BLK), 0)
    c = lax.broadcasted_iota(jnp.int32, (BLK, BLK), 1)
    return (r >= c) if lower else (c >= r)


def _ssd_chunk_setup(dt_ref, dtb_ref, alog_ref):
    raw = dt_ref[...] + dtb_ref[...]
    dt = _softplus(raw)
    aneg = -jnp.exp(alog_ref[...])
    a = dt * aneg
    acs = jnp.dot(_tri(True).astype(F32), a, precision=lax.Precision.HIGHEST, preferred_element_type=F32)
    return raw, dt, aneg, acs, acs.T


def _ssd_specs(T, rev):
    nc = T // BLK
    ci = (lambda c: nc - 1 - c) if rev else (lambda c: c)
    xs = pl.BlockSpec((BLK, D_INNER), lambda c: (ci(c), 0))
    bm = pl.BlockSpec((BLK, SSM_GROUPS * D_STATE), lambda c: (ci(c), D_INNER // (SSM_GROUPS * D_STATE)))
    cm = pl.BlockSpec((BLK, SSM_GROUPS * D_STATE), lambda c: (ci(c), D_INNER // (SSM_GROUPS * D_STATE) + 1))
    dt = pl.BlockSpec((BLK, DT_PAD), lambda c: (ci(c), OFF_DT // DT_PAD))
    v128 = pl.BlockSpec((1, 128), lambda c: (0, 0))
    dfull = pl.BlockSpec((1, D_INNER), lambda c: (0, 0))
    st = pl.BlockSpec((None, SSM_HEADS, HEAD_DIM, D_STATE), lambda c: (ci(c), 0, 0, 0))
    return xs, bm, cm, dt, v128, dfull, st, ci


GW = HEADS_PER_GROUP * HEAD_DIM


def _expanders():
    e = np.zeros((SSM_GROUPS, 128, GW), np.float32)
    for g in range(SSM_GROUPS):
        for hh in range(HEADS_PER_GROUP):
            e[g, HEADS_PER_GROUP * g + hh, HEAD_DIM * hh:HEAD_DIM * (hh + 1)] = 1.0
    return jnp.asarray(e), jnp.asarray(np.transpose(e, (0, 2, 1)).copy())


def _dotx(a, b):
    return jnp.dot(a, b, precision=lax.Precision.HIGHEST, preferred_element_type=F32)


def _decay(acs, acsT, h, tril):
    return jnp.where(tril, jnp.exp(jnp.where(tril, acs[:, h:h + 1] - acsT[h:h + 1, :], 0.0)), 0.0)


def ssd_fwd(xbc, projp, dtb, alog, dfull):
    T = xbc.shape[0]
    nc = T // BLK
    xs, bm, cm, dts, v128, dfs, st, _ = _ssd_specs(T, False)
    E, _ = _expanders()

    def body(xs_ref, b_ref, c_ref, dt_ref, dtb_ref, alog_ref, d_ref, e_ref, y_ref, st_ref, h_scr):
        c = pl.program_id(0)

        @pl.when(c == 0)
        def _():
            h_scr[...] = jnp.zeros_like(h_scr)

        _, dt, _, acs, acsT = _ssd_chunk_setup(dt_ref, dtb_ref, alog_ref)
        tril = _tri(True)
        alast = acs[BLK - 1:BLK, :]
        eacs = jnp.exp(acs)
        wmat = jnp.exp(alast - acs)
        gam = jnp.exp(alast)
        for g in range(SSM_GROUPS):
            gl = slice(GW * g, GW * (g + 1))
            hsl = slice(HEADS_PER_GROUP * g, HEADS_PER_GROUP * (g + 1))
            heads = [HEADS_PER_GROUP * g + hh for hh in range(HEADS_PER_GROUP)]
            Eg = e_ref[g]
            B = b_ref[:, D_STATE * g:D_STATE * (g + 1)].astype(BF16)
            C = c_ref[:, D_STATE * g:D_STATE * (g + 1)].astype(BF16)
            cb = _dot(C, B, NT)
            x_g = xs_ref[:, gl]
            xd_g = x_g * _dotx(dt, Eg)
            hold = h_scr[hsl]
            st_ref[hsl] = hold
            hcat = hold.reshape(GW, D_STATE)
            yoff = _dotx(eacs, Eg) * _dot(C, hcat, NT)
            S = _dot(xd_g * _dotx(wmat, Eg), B, TN)
            Ms = [cb * _decay(acs, acsT, h, tril) for h in heads]
            ys = [_dot(Ms[hh], xd_g[:, HEAD_DIM * hh:HEAD_DIM * (hh + 1)], NN) for hh in range(HEADS_PER_GROUP)]
            for hh, h in enumerate(heads):
                h_scr[h] = gam[:, h:h + 1] * hold[hh] + S[HEAD_DIM * hh:HEAD_DIM * (hh + 1)]
            y_ref[:, gl] = jnp.concatenate(ys, axis=1) + yoff + d_ref[:, gl] * x_g

    espec = pl.BlockSpec((SSM_GROUPS, 128, GW), lambda c: (0, 0, 0))
    return _pcall(body, name="ssd_fwd", grid=(nc,), in_specs=[xs, bm, cm, dts, v128, v128, dfs, espec],
                  out_specs=[xs, st], out_shape=[_sds((T, D_INNER), F32), _sds((nc, SSM_HEADS, HEAD_DIM, D_STATE), F32)],
                  scratch_shapes=[pltpu.VMEM((SSM_HEADS, HEAD_DIM, D_STATE), F32)],
                  compiler_params=_cparams(("arbitrary",)))(xbc, xbc, xbc, projp, dtb, alog, dfull, E)


def ssd_bwd(xbc, projp, dtb, alog, dfull, states, dy):
    T = xbc.shape[0]
    nc = T // BLK
    xs, bm, cm, dts, v128, dfs, st, ci = _ssd_specs(T, True)
    gn = SSM_GROUPS * D_STATE
    E, ET = _expanders()

    def body(xs_ref, b_ref, c_ref, dt_ref, dtb_ref, alog_ref, d_ref, st_ref, dy_ref, e_ref, et_ref,
             dxs_ref, dB_ref, dC_ref, ddt_ref, dal_ref, dD_ref, ddtb_ref, dh_scr):
        i = pl.program_id(0)

        @pl.when(i == 0)
        def _():
            dh_scr[...] = jnp.zeros_like(dh_scr)
            dal_ref[...] = jnp.zeros_like(dal_ref)
            dD_ref[...] = jnp.zeros_like(dD_ref)
            ddtb_ref[...] = jnp.zeros_like(ddtb_ref)

        raw, dt, aneg, acs, acsT = _ssd_chunk_setup(dt_ref, dtb_ref, alog_ref)
        tril = _tri(True)
        lane = lax.broadcasted_iota(jnp.int32, (BLK, 128), 1)
        sub = lax.broadcasted_iota(jnp.int32, (BLK, 128), 0)
        alast = acs[BLK - 1:BLK, :]
        eacs = jnp.exp(acs)
        wmat = jnp.exp(alast - acs)
        gam = jnp.exp(alast)
        gcol = jnp.exp(acsT[:, BLK - 1:BLK])
        ds_col = jnp.zeros((BLK, 128), F32)
        ds_row = jnp.zeros((BLK, 128), F32)
        ddt_col = jnp.zeros((BLK, 128), F32)
        dDm = jnp.zeros((BLK, 128), F32)
        hl = [slice(HEAD_DIM * hh, HEAD_DIM * (hh + 1)) for hh in range(HEADS_PER_GROUP)]
        for g in range(SSM_GROUPS):
            gl = slice(GW * g, GW * (g + 1))
            gs = slice(D_STATE * g, D_STATE * (g + 1))
            hsl = slice(HEADS_PER_GROUP * g, HEADS_PER_GROUP * (g + 1))
            heads = [HEADS_PER_GROUP * g + hh for hh in range(HEADS_PER_GROUP)]
            Eg, ETg = e_ref[g], et_ref[g]
            B = b_ref[:, gs].astype(BF16)
            C = c_ref[:, gs].astype(BF16)
            cb = _dot(C, B, NT)
            x_g, dy_g = xs_ref[:, gl], dy_ref[:, gl]
            dt_x, w_x = _dotx(dt, Eg), _dotx(wmat, Eg)
            xd_g = x_g * dt_x
            dye = dy_g * _dotx(eacs, Eg)
            hcat = st_ref[hsl].reshape(GW, D_STATE)
            dSv = dh_scr[hsl]
            dScat = dSv.reshape(GW, D_STATE)
            dDm = dDm + _dotx(dy_g * x_g, ETg)
            dH_y = _dot(dye, C, TN)
            dC_g = _dot(dye, hcat, NN)
            ds_col = ds_col + _dotx(dye * _dot(C, hcat, NT), ETg)
            dxdw = _dot(B, dScat, NT)
            dB_g = _dot(xd_g * w_x, dScat, NN)
            dww = _dotx(xd_g * dxdw, ETg) * wmat
            ds_col = ds_col - dww + jnp.where(sub == BLK - 1, jnp.sum(dww, axis=0, keepdims=True), 0.0)
            hd = jnp.sum(_dotx(Eg, dScat * hcat), axis=1, keepdims=True) * gcol
            ds_row = ds_row - jnp.where(lane == BLK - 1, hd, 0.0)
            decays = [_decay(acs, acsT, h, tril) for h in heads]
            Ms = [cb * d for d in decays]
            dMs = [_dot(dy_g[:, hl[hh]], xd_g[:, hl[hh]], NT) for hh in range(HEADS_PER_GROUP)]
            dxd1 = [_dot(Ms[hh], dy_g[:, hl[hh]], TN) for hh in range(HEADS_PER_GROUP)]
            dG = jnp.zeros((BLK, BLK), F32)
            for hh, h in enumerate(heads):
                Q = dMs[hh] * Ms[hh]
                ds_col = ds_col + jnp.where(lane == h, jnp.sum(Q, axis=1, keepdims=True), 0.0)
                ds_row = ds_row + jnp.where(sub == h, jnp.sum(Q, axis=0, keepdims=True), 0.0)
                dG = dG + dMs[hh] * decays[hh]
            dxd_g = jnp.concatenate(dxd1, axis=1) + w_x * dxdw
            dxs_ref[:, gl] = d_ref[:, gl] * dy_g + dxd_g * dt_x
            ddt_col = ddt_col + _dotx(dxd_g * x_g, ETg)
            dC_ref[:, gs] = dC_g + _dot(dG, B, NN)
            dB_ref[:, gs] = dB_g + _dot(dG, C, TN)
            for hh, h in enumerate(heads):
                dh_scr[h] = gam[:, h:h + 1] * dSv[hh] + dH_y[hl[hh]]
        ds_all = ds_col - ds_row.T
        da = jnp.dot(_tri(False).astype(F32), ds_all, precision=lax.Precision.HIGHEST, preferred_element_type=F32)
        ddt = ddt_col + da * aneg
        draw = jnp.where(lane < SSM_HEADS, ddt * _sigmoid(raw), 0.0)
        ddt_ref[...] = draw.astype(BF16)
        dal_ref[...] += jnp.sum(da * dt, axis=0, keepdims=True) * aneg
        ddtb_ref[...] += jnp.sum(draw, axis=0, keepdims=True)
        dD_ref[...] += jnp.sum(dDm, axis=0, keepdims=True)

    gblk = pl.BlockSpec((BLK, gn), lambda c: (ci(c), 0))
    espec = pl.BlockSpec((SSM_GROUPS, 128, GW), lambda c: (0, 0, 0))
    etspec = pl.BlockSpec((SSM_GROUPS, GW, 128), lambda c: (0, 0, 0))
    return _pcall(body, name="ssd_bwd", grid=(nc,), in_specs=[xs, bm, cm, dts, v128, v128, dfs, st, xs, espec, etspec],
                  out_specs=[xs, gblk, gblk, pl.BlockSpec((BLK, DT_PAD), lambda c: (ci(c), 0)), v128, v128, v128],
                  out_shape=[_sds((T, D_INNER), F32), _sds((T, gn), F32), _sds((T, gn), F32), _sds((T, DT_PAD), BF16),
                             _sds((1, 128), F32), _sds((1, 128), F32), _sds((1, 128), F32)],
                  scratch_shapes=[pltpu.VMEM((SSM_HEADS, HEAD_DIM, D_STATE), F32)],
                  compiler_params=_cparams(("arbitrary",)))(xbc, xbc, xbc, projp, dtb, alog, dfull, states, dy, E, ET)


_WIN_ORDER = ("z", "ga", "gs", "xbc", "q", "k", "v", "dt")


def _win_to_padded(win_g):
    full = win_g.reshape(IN_DIM, D_MODEL)
    rows = []
    for nm in _WIN_ORDER:
        s, w = SEG[nm]
        rows.append(full[s:s + w])
    rows.append(jnp.zeros((DT_PAD - SEG["dt"][1], D_MODEL), win_g.dtype))
    return jnp.concatenate(rows, axis=0)


def _padded_to_win(dw):
    off = dict(z=OFF_Z, ga=OFF_GA, gs=OFF_GS, xbc=OFF_XBC, q=OFF_Q, k=OFF_K, v=OFF_V, dt=OFF_DT)
    per = IN_DIM // N_DEV
    blocks = []
    for j in range(N_DEV):
        lo, hi, rows = j * per, (j + 1) * per, []
        for nm in ("q", "k", "v", "z", "xbc", "dt", "ga", "gs"):
            s, w = SEG[nm]
            a, b = max(lo, s), min(hi, s + w)
            if a < b:
                rows.append(dw[off[nm] + a - s:off[nm] + b - s])
        blocks.append(jnp.concatenate(rows, axis=0))
    return jnp.stack(blocks)


def _pad128(v):
    return jnp.pad(v, ((0, 0), (0, 128 - v.shape[1])))


_SMALL = (("loss", 128, 1), ("g_mix", 2048, 2048), ("conv_b", 3072, 3072), ("dt_bias", 128, 32), ("a_log", 128, 32),
          ("d_skip", 128, 32), ("g_ssd", 2048, 2048), ("sinks", 128, 16), ("g_ffn", 2048, 2048), ("g_ple", 2048, 2048),
          ("g_final", 2048, 2048))


def _small_vec(d):
    parts = []
    for nm, pw, w in _SMALL:
        v = d[nm].reshape(1, -1).astype(F32)
        parts.append(jnp.pad(v[:, :min(v.shape[1], pw)], ((0, 0), (0, pw - min(v.shape[1], pw)))))
    return jnp.concatenate(parts, axis=1)


def _small_split(vec):
    out, o = {}, 0
    for nm, pw, w in _SMALL:
        out[nm] = vec[0, o:o + w]
        o += pw
    return out


def kernel(x, p, positions, g_mix, w_in, conv_w, conv_b, dt_bias, a_log, d_skip, g_ssd, sinks, w_attn_br, w_ssd_br, w_o, g_ffn, w_gate, w_up, w_down, g_ple, w_ple_gate, w_ple_proj, g_final, loss_target, m_g_mix, m_w_in, m_conv_w, m_conv_b, m_dt_bias, m_a_log, m_d_skip, m_g_ssd, m_sinks, m_w_attn_br, m_w_ssd_br, m_w_o, m_g_ffn, m_w_gate, m_w_up, m_w_down, m_g_ple, m_w_ple_gate, m_w_ple_proj, m_g_final, v_g_mix, v_w_in, v_conv_w, v_conv_b, v_dt_bias, v_a_log, v_d_skip, v_g_ssd, v_sinks, v_w_attn_br, v_w_ssd_br, v_w_o, v_g_ffn, v_w_gate, v_w_up, v_w_down, v_g_ple, v_w_ple_gate, v_w_ple_proj, v_g_final):
    T = x.shape[1]
    D = D_MODEL
    W = dict(g_mix=g_mix, w_in=w_in, conv_w=conv_w, conv_b=conv_b, dt_bias=dt_bias, a_log=a_log, d_skip=d_skip, g_ssd=g_ssd,
             sinks=sinks, w_attn_br=w_attn_br, w_ssd_br=w_ssd_br, w_o=w_o, g_ffn=g_ffn, w_gate=w_gate, w_up=w_up, w_down=w_down,
             g_ple=g_ple, w_ple_gate=w_ple_gate, w_ple_proj=w_ple_proj, g_final=g_final)
    Mo = dict(g_mix=m_g_mix, w_in=m_w_in, conv_w=m_conv_w, conv_b=m_conv_b, dt_bias=m_dt_bias, a_log=m_a_log, d_skip=m_d_skip,
              g_ssd=m_g_ssd, sinks=m_sinks, w_attn_br=m_w_attn_br, w_ssd_br=m_w_ssd_br, w_o=m_w_o, g_ffn=m_g_ffn, w_gate=m_w_gate,
              w_up=m_w_up, w_down=m_w_down, g_ple=m_g_ple, w_ple_gate=m_w_ple_gate, w_ple_proj=m_w_ple_proj, g_final=m_g_final)
    Vo = dict(g_mix=v_g_mix, w_in=v_w_in, conv_w=v_conv_w, conv_b=v_conv_b, dt_bias=v_dt_bias, a_log=v_a_log, d_skip=v_d_skip,
              g_ssd=v_g_ssd, sinks=v_sinks, w_attn_br=v_w_attn_br, w_ssd_br=v_w_ssd_br, w_o=v_w_o, g_ffn=v_g_ffn, w_gate=v_w_gate,
              w_up=v_w_up, w_down=v_w_down, g_ple=v_g_ple, w_ple_gate=v_w_ple_gate, w_ple_proj=v_w_ple_proj, g_final=v_g_final)
    order = ["g_mix", "w_in", "conv_w", "conv_b", "dt_bias", "a_log", "d_skip", "g_ssd", "sinks", "w_attn_br", "w_ssd_br", "w_o",
             "g_ffn", "w_gate", "w_up", "w_down", "g_ple", "w_ple_gate", "w_ple_proj", "g_final"]
    big = ["w_in", "conv_w", "w_attn_br", "w_ssd_br", "w_o", "w_gate", "w_up", "w_down", "w_ple_gate", "w_ple_proj"]

    x2 = x.reshape(T, D)
    p2 = p.reshape(T, PLE_DIM)
    tgt = loss_target.reshape(T, D)
    posf = positions.reshape(T, 1).astype(F32)
    inv = ROPE_THETA ** (-np.arange(HEAD_DIM // 2, dtype=np.float32) * 2.0 / HEAD_DIM)
    inv128 = jnp.asarray(np.tile(inv, 128 // (HEAD_DIM // 2)).reshape(1, 128).astype(np.float32))
    transposed = ("w_in", "w_gate", "w_up")

    def shard2d(a, n):
        a = a.reshape(a.shape[-2:])
        return a.T if n in transposed else a

    sh = {n: shard2d(W[n], n) for n in big}

    del _PENDING[:]
    me = 4 * lax.axis_index("x") + 2 * lax.axis_index("y") + lax.axis_index("c")
    groups = (("w_in",), ("conv_w", "w_attn_br", "w_ssd_br", "w_o"), ("w_gate", "w_up", "w_down"), ("w_ple_gate", "w_ple_proj"))
    send = {n: sh[n] if n == "conv_w" else sh[n].astype(BF16) for n in big}
    started, prev = [], None
    for gi, grp in enumerate(groups):
        zones = [lax.dynamic_update_index_in_dim(lax.empty((N_DEV,) + send[n].shape, send[n].dtype), send[n], me, 0) for n in grp]
        h = split_start("gather_start_%d" % gi, "gather", ICI_SAME_CORE, [], lands=zones, after=prev)
        prev = h["token"]
        started.append(h)
    gathered, fwd = {}, {}

    def forward_start(gi, after):
        _, lands = split_wait("gather_wait_%d" % gi, started[gi], after)
        fwd[gi] = split_start("forward_start_%d" % gi, "forward", FORWARD_BLOCKS, [], lands=lands)

    def forward_wait(gi, after):
        _, full = split_wait("forward_wait_%d" % gi, fwd[gi], after)
        gathered.update(zip(groups[gi], full))

    u = rms_fwd("norm_mix", x2, g_mix)
    forward_start(0, u)
    forward_wait(0, u)
    forward_start(1, u)
    winp = _win_to_padded(gathered["w_in"])
    dtb = _pad128(dt_bias)
    alog = _pad128(a_log)
    dfull = jnp.repeat(d_skip.reshape(SSM_HEADS), HEAD_DIM).reshape(1, D_INNER)

    projp = mm_nt("in_proj", u, winp, 640)
    attn, qr, kr = attn_fwd(projp, posf, inv128, sinks)
    forward_wait(1, attn)
    convw = jnp.transpose(gathered["conv_w"], (1, 0, 2)).reshape(CONV_WIDTH, CONV_DIM)
    wab = gathered["w_attn_br"]
    wsb = gathered["w_ssd_br"].reshape(D, D)
    wo = gathered["w_o"].reshape(D, D)
    xbc = conv_fwd(projp, convw, conv_b)
    y, states = ssd_fwd(xbc, projp, dtb, alog, dfull)
    yn = gnorm_fwd(y, projp, g_ssd)
    out_a = mm_nn_colblk("attn_br", attn, wab)
    out_s = mm_nn("ssd_br", yn, wsb, 512)
    forward_start(2, out_s)
    merged = merge_fwd(projp, out_a, out_s)
    h1 = mm_nn("o_proj", merged, wo, 512, residual=x2)
    f = rms_fwd("norm_ffn", h1, g_ffn)
    forward_wait(2, f)
    wgt, wut, wd = (gathered[n].reshape(FFN_HIDDEN, D) for n in ("w_gate", "w_up", "w_down"))
    gate, up, act = ffn_up(f, wgt, wut)
    forward_start(3, act)
    h2 = mm_nn_red("ffn_down", act, wd, 512, FFN_HIDDEN, residual=h1)
    r = rms_fwd("norm_ple", h2, g_ple)
    forward_wait(3, r)
    wpg = gathered["w_ple_gate"].reshape(D, D)
    wpp = gathered["w_ple_proj"]
    pg = mm_nn("ple_gate", r, wpg, 512)
    pp = mm_nn_colblk("ple_proj", p2, wpp)
    loss_v, dh3, dpg, dpp, dg_final = head_fwd_bwd(h2, pg, pp, g_final.reshape(1, D), tgt)

    gw = {}
    scat = []

    def scatter_start(names):
        scat.append((names, split_start("scatter_start_%d" % len(scat), "scatter", ALL_PEERS, [gw[n] for n in names])))

    gw["w_ple_proj"] = mm_tn_colblk("dw_ple_proj", p2, dpp, PLE_DIM)
    dr = mm_nt("d_ple_gate", dpg, wpg, 512)
    gw["w_ple_gate"] = mm_tn("dw_ple_gate", r, dpg, 512, 1024).reshape(N_DEV, D // N_DEV, D)
    scatter_start(("w_ple_proj", "w_ple_gate"))
    dh2, dh2b, dg_ple = rms_bwd("norm_ple_bwd", h2, g_ple, dr, dh3)
    dgate, dup = ffn_down_bwd(dh2b, wd, gate, up)
    per = FFN_HIDDEN // N_DEV
    gw["w_down"] = mm_tn("dw_down", act, dh2b, FFN_TILE, 1024).reshape(N_DEV, per, D)
    gw["w_gate"] = mm_tn("dw_gate", dgate, f, FFN_TILE, 1024).reshape(N_DEV, per, D)
    gw["w_up"] = mm_tn("dw_up", dup, f, FFN_TILE, 1024).reshape(N_DEV, per, D)
    scatter_start(("w_down", "w_gate", "w_up"))
    df = ffn_up_bwd(dgate, dup, wgt, wut)
    dh1, dh1b, dg_ffn = rms_bwd("norm_ffn_bwd", h1, g_ffn, df, dh2)
    dmerged = mm_nt("d_o_proj", dh1b, wo, 512)
    gw["w_o"] = mm_tn("dw_o", merged, dh1b, 512, 1024).reshape(N_DEV, D // N_DEV, D)
    dout_a, dout_s, dga, dgs = merge_bwd(projp, out_a, out_s, dmerged)
    gw["w_ssd_br"] = mm_tn("dw_ssd_br", yn, dout_s, 512, 1024).reshape(N_DEV, D // N_DEV, D)
    gw["w_attn_br"] = mm_tn_colblk("dw_attn_br", attn, dout_a, D // N_DEV)
    scatter_start(("w_o", "w_ssd_br", "w_attn_br"))
    dyn = mm_nt("d_ssd_br", dout_s, wsb, 512)
    dattn = attn_br_bwd(dout_a, wab)
    dy, dz, dg_ssd = gnorm_bwd(y, projp, g_ssd, dyn)
    dxs, dbm, dcm, ddt, dal, ddsk, ddtb = ssd_bwd(xbc, projp, dtb, alog, dfull, states, dy)
    dx_x, dwc_x, dbc_x = conv_bwd("conv_bwd_x", projp, dxs, convw, conv_b, 0)
    dx_b, dwc_b, dbc_b = conv_bwd("conv_bwd_b", projp, dbm, convw, conv_b, D_INNER)
    dx_c, dwc_c, dbc_c = conv_bwd("conv_bwd_c", projp, dcm, convw, conv_b, D_INNER + SSM_GROUPS * D_STATE)
    dq, dk, dv, dsk = attn_bwd(qr, kr, projp, dattn, posf, inv128, sinks)
    dproj = jnp.concatenate([dz, dga, dgs, dx_x, dx_b, dx_c, dq, dk.astype(BF16), dv.astype(BF16), ddt], axis=1)
    gw_in = _padded_to_win(mm_tn("dw_in", dproj, u, 640, 1024))
    pair = split_start("pair_start", "pair", FORWARD_BLOCKS, [gw_in])
    dconvw = jnp.concatenate([dwc_x, dwc_b, dwc_c], axis=1)
    gw["conv_w"] = jnp.transpose(dconvw.reshape(CONV_WIDTH, N_DEV, CONV_DIM // N_DEV), (1, 0, 2))
    scatter_start(("conv_w",))
    du_tm = min(512, T)
    tiles = T // du_tm
    first = max(tiles // 2, 1)
    du = mm_nn_red("d_in_proj_a", dproj, winp, 512, IN_PAD, rows=(0, first), tm=du_tm)
    (gw_in,), (sibling_part,) = split_wait("pair_wait", pair, du)
    pair_slots = jnp.stack([jnp.bitwise_xor(me, k) for k in FORWARD_BLOCKS]).astype(jnp.int32)
    core = split_start("core_start", "scatter_core", ICI_SAME_CORE, [pair_sum("pair_sum_w_in", gw_in, pair_slots, sibling_part)])
    if first < tiles:
        du = mm_nn_red("d_in_proj_b", dproj, winp, 512, IN_PAD, rows=(first, tiles - first), prev=du, tm=du_tm)
    gx, _, dg_mix = rms_bwd("norm_mix_bwd", x2, g_mix, du, dh1)

    small_g = dict(loss=loss_v[:, :1], g_mix=dg_mix, conv_b=jnp.concatenate([dbc_x, dbc_b, dbc_c], axis=1), dt_bias=ddtb,
                   a_log=dal, d_skip=ddsk, g_ssd=dg_ssd, sinks=dsk, g_ffn=dg_ffn, g_ple=dg_ple, g_final=dg_final)
    vec = _small_vec(small_g)
    small = split_start("small_start", "gather", ALL_PEERS, [],
                        lands=[lax.dynamic_update_index_in_dim(lax.empty((N_DEV,) + vec.shape, F32), vec, me, 0)])

    res = {}
    after = [gx]
    for si, (names, h) in enumerate(scat):
        srcs, lands = split_wait("scatter_wait_%d" % si, h, after)
        for n, mine, arrived in zip(names, srcs, lands):
            res[n] = adamw("adamw_" + n, arrived, sh[n], shard2d(Mo[n], n), shard2d(Vo[n], n), own=mine, own_slot=me)
        after = [res[n][0] for n in names]
    zero = jnp.zeros((1, 1), F32)
    _, (vec_parts,) = split_wait("small_wait", small, [res[n][0] for n in res])
    sres = adamw("adamw_small", vec_parts, _small_vec({**W, "loss": zero}), _small_vec({**Mo, "loss": zero}),
                 _small_vec({**Vo, "loss": zero}))
    ssplit = [_small_split(a) for a in sres]

    (pair_sums,), (arrived,) = split_wait("core_wait", core, [sres[0]])
    res["w_in"] = adamw("adamw_w_in", arrived, sh["w_in"], shard2d(Mo["w_in"], "w_in"), shard2d(Vo["w_in"], "w_in"),
                        own=pair_sums, own_slot=0)
    loss = ssplit[0]["loss"].reshape(())
    for n in order:
        if n not in res:
            res[n] = tuple(s[n].reshape(W[n].shape) for s in ssplit)
        else:
            res[n] = tuple((a.T if n in transposed else a).reshape(W[n].shape) for a in res[n])
    outs = [loss, gx.reshape(x.shape)]
    for k in range(4):
        outs += [res[n][k] for n in order]
    return tuple(outs)
```

```python
import functools

import numpy as np
import jax
import jax.numpy as jnp
from jax import lax
from jax.experimental import pallas as pl
from jax.experimental.pallas import tpu as pltpu

F32 = jnp.float32
BF16 = jnp.bfloat16

N_DEV = 8
D_MODEL = 2048
HEAD_DIM = 64
ATTN_HEADS = 16
KV_HEADS = 4
Q_DIM = 1024
KV_DIM = 256
BLK = 128
D_INNER = 2048
SSM_HEADS = 32
SSM_GROUPS = 4
HEADS_PER_GROUP = 8
D_STATE = 128
CONV_WIDTH = 4
CONV_DIM = 3072
FFN_HIDDEN = 5632
PLE_DIM = 256
IN_DIM = 10784
NORM_EPS = 1e-6
SSM_NORM_EPS = 1e-5
ROPE_THETA = 10000.0

OFF_Z, OFF_GA, OFF_GS, OFF_XBC, OFF_Q, OFF_K, OFF_V, OFF_DT = 0, 2048, 4096, 6144, 9216, 10240, 10496, 10752
IN_PAD = 10880
DT_PAD = 128
SEG = dict(q=(0, 1024), k=(1024, 256), v=(1280, 256), z=(1536, 2048), xbc=(3584, 3072), dt=(6656, 32),
           ga=(6688, 2048), gs=(8736, 2048))

ADAM_LR, ADAM_B1, ADAM_B2, ADAM_EPS, ADAM_WD, ADAM_STEP = 0.001, 0.9, 0.999, 1e-08, 0.01, 10

VMEM_LIMIT = 56 * 1024 * 1024

NN = (((1,), (0,)), ((), ()))
NT = (((1,), (1,)), ((), ()))
TN = (((0,), (0,)), ((), ()))


_PENDING = []


def _raw_call(body, **kw):
    return pl.pallas_call(body, **kw)


def _pcall(body, **kw):
    if "in_specs" not in kw:
        return _raw_call(body, **kw)
    deps = list(_PENDING)
    del _PENDING[:]
    if not deps:
        return _raw_call(body, **kw)
    n_in = len(kw["in_specs"])

    def tied(*refs):
        return body(*refs[:n_in], *refs[n_in + len(deps):])

    kw["in_specs"] = list(kw["in_specs"]) + [pl.BlockSpec(memory_space=pl.ANY)] * len(deps)
    call = _raw_call(tied, **kw)
    return lambda *ops: call(*ops, *deps)


def _cparams(sem=None):
    if sem is None:
        return pltpu.CompilerParams(vmem_limit_bytes=VMEM_LIMIT)
    return pltpu.CompilerParams(vmem_limit_bytes=VMEM_LIMIT, dimension_semantics=sem)


def _dot(a, b, dn):
    return lax.dot_general(a.astype(BF16), b.astype(BF16), dn, preferred_element_type=F32)


def _sigmoid(x):
    return 1.0 / (1.0 + jnp.exp(-x))


def _silu(x):
    return x * _sigmoid(x)


def _dsilu(x):
    s = _sigmoid(x)
    return s * (1.0 + x * (1.0 - s))


def _matmul(name, pairs, pair_specs, dn, grid, out_shapes, out_specs, nred=1, extra=(), extra_specs=(),
            epilogue=None, acc_shape=None, alias=None):
    n_in = 2 * len(pairs) + len(extra)
    n_out = len(out_shapes)

    def body(*refs):
        ins = refs[:2 * len(pairs)]
        ex = [r for r, sp in zip(refs[2 * len(pairs):n_in], extra_specs) if sp.memory_space != pl.ANY]
        outs = refs[n_in:n_in + n_out]

        def prod():
            s = None
            for p in range(len(pairs)):
                d = _dot(ins[2 * p][...], ins[2 * p + 1][...], dn)
                s = d if s is None else s + d
            return s

        def finish(val):
            if epilogue is None:
                outs[0][...] = val.astype(outs[0].dtype)
            else:
                res = epilogue(val, *[e[...] for e in ex])
                for o, r in zip(outs, res):
                    o[...] = r.astype(o.dtype)

        if nred == 1:
            finish(prod())
        else:
            acc = refs[n_in + n_out]
            k = pl.program_id(len(grid) - 1)

            @pl.when(k == 0)
            def _():
                acc[...] = jnp.zeros_like(acc)

            acc[...] += prod()

            @pl.when(k == nred - 1)
            def _():
                finish(acc[...])

    operands = []
    specs = []
    for (a, b), (sa, sb) in zip(pairs, pair_specs):
        operands += [a, b]
        specs += [sa, sb]
    operands += list(extra)
    specs += list(extra_specs)
    scratch = [pltpu.VMEM(acc_shape, F32)] if nred > 1 else []
    sem = ("arbitrary",) * len(grid)
    res = _pcall(body, name=name, grid=grid, in_specs=specs, out_specs=list(out_specs), input_output_aliases=dict(alias or {}),
                 out_shape=list(out_shapes), scratch_shapes=scratch, compiler_params=_cparams(sem))(*operands)
    return res


def _sds(shape, dtype):
    return jax.ShapeDtypeStruct(shape, dtype)


def _row_tile(T):
    return min(1024, T)


def mm_nn(name, a, b, tn, out_dtype=F32, residual=None):
    M, K = a.shape
    N = b.shape[1]
    tm = _row_tile(M)
    grid = (M // tm, N // tn)
    extra, especs, epi = (), (), None
    if residual is not None:
        extra = (residual,)
        especs = (pl.BlockSpec((tm, tn), lambda i, n: (i, n)),)
        epi = lambda v, r: (v + r,)
    return _matmul(name, [(a, b)], [(pl.BlockSpec((tm, K), lambda i, n: (i, 0)), pl.BlockSpec((K, tn), lambda i, n: (0, n)))],
                   NN, grid, [_sds((M, N), out_dtype)], [pl.BlockSpec((tm, tn), lambda i, n: (i, n))],
                   extra=extra, extra_specs=especs, epilogue=epi)[0]


def mm_nn_colblk(name, a, b, out_dtype=F32):
    M, K = a.shape
    J, _, nb = b.shape
    tm = _row_tile(M)
    grid = (M // tm, J)
    return _matmul(name, [(a, b)], [(pl.BlockSpec((tm, K), lambda i, j: (i, 0)), pl.BlockSpec((None, K, nb), lambda i, j: (j, 0, 0)))],
                   NN, grid, [_sds((M, J * nb), out_dtype)], [pl.BlockSpec((tm, nb), lambda i, j: (i, j))])[0]


def mm_nt(name, a, w, tr, out_dtype=F32):
    M, C = a.shape
    R = w.shape[0]
    tm = _row_tile(M)
    grid = (M // tm, R // tr)
    return _matmul(name, [(a, w)], [(pl.BlockSpec((tm, C), lambda i, r: (i, 0)), pl.BlockSpec((tr, C), lambda i, r: (r, 0)))],
                   NT, grid, [_sds((M, R), out_dtype)], [pl.BlockSpec((tm, tr), lambda i, r: (i, r))])[0]


def mm_nt_red(name, a, w, tr, tk, out_dtype=F32):
    M, C = a.shape
    R = w.shape[0]
    tm = _row_tile(M)
    nk = C // tk
    grid = (M // tm, R // tr, nk)
    return _matmul(name, [(a, w)], [(pl.BlockSpec((tm, tk), lambda i, r, k: (i, k)), pl.BlockSpec((tr, tk), lambda i, r, k: (r, k)))],
                   NT, grid, [_sds((M, R), out_dtype)], [pl.BlockSpec((tm, tr), lambda i, r, k: (i, r))],
                   nred=nk, acc_shape=(tm, tr))[0]


def mm_nn_red(name, a, b, tn, tk, out_dtype=F32, residual=None, rows=None, prev=None, tm=None):
    M, K = a.shape
    N = b.shape[1]
    tm = min(tm or _row_tile(M), M)
    nk = K // tk
    i0, ni = (0, M // tm) if rows is None else rows
    grid = (ni, N // tn, nk)
    ospec = pl.BlockSpec((tm, tn), lambda i, n, k: (i + i0, n))
    extra, especs, epi = [], [], None
    if residual is not None:
        extra, especs, epi = [residual], [ospec], (lambda v, r, *_: (v + r,))
    alias = {}
    if prev is not None:
        alias = {2 + len(extra): 0}
        extra, especs = extra + [prev], especs + [_ANY]
        epi = epi or (lambda v, *_: (v,))
    return _matmul(name, [(a, b)], [(pl.BlockSpec((tm, tk), lambda i, n, k: (i + i0, k)), pl.BlockSpec((tk, tn), lambda i, n, k: (k, n)))],
                   NN, grid, [_sds((M, N), out_dtype)], [ospec], nred=nk, acc_shape=(tm, tn),
                   extra=extra, extra_specs=especs, epilogue=epi, alias=alias)[0]


def mm_tn(name, x, dy, tr, tc, out_dtype=BF16):
    M, R = x.shape
    C = dy.shape[1]
    grid = (R // tr, C // tc)
    return _matmul(name, [(x, dy)], [(pl.BlockSpec((M, tr), lambda r, c: (0, r)), pl.BlockSpec((M, tc), lambda r, c: (0, c)))],
                   TN, grid, [_sds((R, C), out_dtype)], [pl.BlockSpec((tr, tc), lambda r, c: (r, c))])[0]


def mm_tn_colblk(name, x, dy, nb, out_dtype=BF16):
    M, R = x.shape
    J = dy.shape[1] // nb
    grid = (J,)
    return _matmul(name, [(x, dy)], [(pl.BlockSpec((M, R), lambda j: (0, 0)), pl.BlockSpec((M, nb), lambda j: (0, j)))],
                   TN, grid, [_sds((J, R, nb), out_dtype)], [pl.BlockSpec((None, R, nb), lambda j: (j, 0, 0))])[0]


def _rows(T):
    return min(256, T)


def rms_fwd(name, x, g, eps=NORM_EPS):
    T, D = x.shape
    tm = _rows(T)

    def body(x_ref, g_ref, o_ref):
        xv = x_ref[...]
        r = lax.rsqrt(jnp.mean(xv * xv, axis=-1, keepdims=True) + eps)
        o_ref[...] = (xv * r * g_ref[...]).astype(BF16)

    return _pcall(body, name=name, grid=(T // tm,),
                  in_specs=[pl.BlockSpec((tm, D), lambda i: (i, 0)), pl.BlockSpec((1, D), lambda i: (0, 0))],
                  out_specs=pl.BlockSpec((tm, D), lambda i: (i, 0)), out_shape=_sds((T, D), BF16),
                  compiler_params=_cparams(("arbitrary",)))(x, g)


def rms_bwd(name, x, g, dy, dres, eps=NORM_EPS):
    T, D = x.shape
    tm = _rows(T)

    def body(x_ref, g_ref, dy_ref, dr_ref, dx_ref, dxb_ref, dg_ref):
        i = pl.program_id(0)
        xv = x_ref[...]
        r = lax.rsqrt(jnp.mean(xv * xv, axis=-1, keepdims=True) + eps)
        xh = xv * r
        dyv = dy_ref[...]
        gd = dyv * g_ref[...]
        dx = r * (gd - xh * jnp.mean(gd * xh, axis=-1, keepdims=True)) + dr_ref[...]
        dx_ref[...] = dx
        dxb_ref[...] = dx.astype(BF16)

        @pl.when(i == 0)
        def _():
            dg_ref[...] = jnp.zeros_like(dg_ref)

        dg_ref[...] += jnp.sum(dyv * xh, axis=0, keepdims=True)

    row = pl.BlockSpec((tm, D), lambda i: (i, 0))
    vec = pl.BlockSpec((1, D), lambda i: (0, 0))
    return _pcall(body, name=name, grid=(T // tm,), in_specs=[row, vec, row, row], out_specs=[row, row, vec],
                  out_shape=[_sds((T, D), F32), _sds((T, D), BF16), _sds((1, D), F32)],
                  compiler_params=_cparams(("arbitrary",)))(x, g, dy, dres)


def gnorm_fwd(y, projp, g):
    T, D = y.shape
    tm = _rows(T)

    def body(y_ref, z_ref, g_ref, o_ref):
        yz = y_ref[...] * _silu(z_ref[...])
        r = lax.rsqrt(jnp.mean(yz * yz, axis=-1, keepdims=True) + SSM_NORM_EPS)
        o_ref[...] = (yz * r * g_ref[...]).astype(BF16)

    row = pl.BlockSpec((tm, D), lambda i: (i, 0))
    return _pcall(body, name="gnorm_fwd", grid=(T // tm,),
                  in_specs=[row, pl.BlockSpec((tm, D), lambda i: (i, OFF_Z // D)), pl.BlockSpec((1, D), lambda i: (0, 0))],
                  out_specs=row, out_shape=_sds((T, D), BF16), compiler_params=_cparams(("arbitrary",)))(y, projp, g)


def gnorm_bwd(y, projp, g, dyn):
    T, D = y.shape
    tm = _rows(T)

    def body(y_ref, z_ref, g_ref, dyn_ref, dy_ref, dz_ref, dg_ref):
        i = pl.program_id(0)
        yv, zv = y_ref[...], z_ref[...]
        sz = _silu(zv)
        yz = yv * sz
        r = lax.rsqrt(jnp.mean(yz * yz, axis=-1, keepdims=True) + SSM_NORM_EPS)
        xh = yz * r
        dv = dyn_ref[...]
        gd = dv * g_ref[...]
        dyz = r * (gd - xh * jnp.mean(gd * xh, axis=-1, keepdims=True))
        dy_ref[...] = dyz * sz
        dz_ref[...] = (dyz * yv * _dsilu(zv)).astype(BF16)

        @pl.when(i == 0)
        def _():
            dg_ref[...] = jnp.zeros_like(dg_ref)

        dg_ref[...] += jnp.sum(dv * xh, axis=0, keepdims=True)

    row = pl.BlockSpec((tm, D), lambda i: (i, 0))
    vec = pl.BlockSpec((1, D), lambda i: (0, 0))
    return _pcall(body, name="gnorm_bwd", grid=(T // tm,),
                  in_specs=[row, pl.BlockSpec((tm, D), lambda i: (i, OFF_Z // D)), vec, row], out_specs=[row, row, vec],
                  out_shape=[_sds((T, D), F32), _sds((T, D), BF16), _sds((1, D), F32)],
                  compiler_params=_cparams(("arbitrary",)))(y, projp, g, dyn)


def merge_fwd(projp, out_a, out_s):
    T, D = out_a.shape
    tm = _rows(T)

    def body(ga_ref, gs_ref, a_ref, s_ref, o_ref):
        o_ref[...] = (_sigmoid(ga_ref[...]) * a_ref[...] + _sigmoid(gs_ref[...]) * s_ref[...]).astype(BF16)

    row = pl.BlockSpec((tm, D), lambda i: (i, 0))
    return _pcall(body, name="merge_fwd", grid=(T // tm,),
                  in_specs=[pl.BlockSpec((tm, D), lambda i: (i, OFF_GA // D)), pl.BlockSpec((tm, D), lambda i: (i, OFF_GS // D)), row, row],
                  out_specs=row, out_shape=_sds((T, D), BF16), compiler_params=_cparams(("arbitrary",)))(projp, projp, out_a, out_s)


def merge_bwd(projp, out_a, out_s, dmerged):
    T, D = out_a.shape
    tm = _rows(T)

    def body(ga_ref, gs_ref, a_ref, s_ref, dm_ref, da_ref, ds_ref, dga_ref, dgs_ref):
        dm = dm_ref[...]
        sa, ss = _sigmoid(ga_ref[...]), _sigmoid(gs_ref[...])
        da_ref[...] = (dm * sa).astype(BF16)
        ds_ref[...] = (dm * ss).astype(BF16)
        dga_ref[...] = (dm * a_ref[...] * sa * (1.0 - sa)).astype(BF16)
        dgs_ref[...] = (dm * s_ref[...] * ss * (1.0 - ss)).astype(BF16)

    row = pl.BlockSpec((tm, D), lambda i: (i, 0))
    return _pcall(body, name="merge_bwd", grid=(T // tm,),
                  in_specs=[pl.BlockSpec((tm, D), lambda i: (i, OFF_GA // D)), pl.BlockSpec((tm, D), lambda i: (i, OFF_GS // D)), row, row, row],
                  out_specs=[row] * 4, out_shape=[_sds((T, D), BF16)] * 4,
                  compiler_params=_cparams(("arbitrary",)))(projp, projp, out_a, out_s, dmerged)


def head_fwd_bwd(h2, pg, pp, g_final, target):
    T, D = h2.shape
    tm = _rows(T)

    def body(h_ref, pg_ref, pp_ref, g_ref, t_ref, loss_ref, dh_ref, dpg_ref, dpp_ref, dg_ref):
        i = pl.program_id(0)
        s = _sigmoid(pg_ref[...])
        ppv = pp_ref[...]
        h3 = h_ref[...] + s * ppv
        r = lax.rsqrt(jnp.mean(h3 * h3, axis=-1, keepdims=True) + NORM_EPS)
        xh = h3 * r
        gv = g_ref[...]
        e = xh * gv - t_ref[...]
        dyo = e * (1.0 / D)
        gd = dyo * gv
        dh = r * (gd - xh * jnp.mean(gd * xh, axis=-1, keepdims=True))
        dh_ref[...] = dh
        dpg_ref[...] = (dh * ppv * s * (1.0 - s)).astype(BF16)
        dpp_ref[...] = (dh * s).astype(BF16)

        @pl.when(i == 0)
        def _():
            dg_ref[...] = jnp.zeros_like(dg_ref)
            loss_ref[...] = jnp.zeros_like(loss_ref)

        dg_ref[...] += jnp.sum(dyo * xh, axis=0, keepdims=True)
        part = 0.5 * jnp.sum(jnp.mean(e * e, axis=-1, keepdims=True), axis=0, keepdims=True)
        loss_ref[...] += jnp.broadcast_to(part, loss_ref.shape)

    row = pl.BlockSpec((tm, D), lambda i: (i, 0))
    vec = pl.BlockSpec((1, D), lambda i: (0, 0))
    return _pcall(body, name="head_fwd_bwd", grid=(T // tm,), in_specs=[row, row, row, vec, row],
                  out_specs=[pl.BlockSpec((1, 128), lambda i: (0, 0)), row, row, row, vec],
                  out_shape=[_sds((1, 128), F32), _sds((T, D), F32), _sds((T, D), BF16), _sds((T, D), BF16), _sds((1, D), F32)],
                  compiler_params=_cparams(("arbitrary",)))(h2, pg, pp, g_final, target)


FFN_TILE = 512


def ffn_up(f, wgt, wut):
    T, D = f.shape
    H = wgt.shape[0]
    tm = _row_tile(T)

    def body(f_ref, wg_ref, wu_ref, g_ref, u_ref, a_ref):
        fv = f_ref[...]
        g = _dot(fv, wg_ref[...], NT)
        u = _dot(fv, wu_ref[...], NT)
        g_ref[...] = g
        u_ref[...] = u
        a_ref[...] = (_silu(g) * u).astype(BF16)

    wspec = pl.BlockSpec((FFN_TILE, D), lambda i, j: (j, 0))
    ospec = pl.BlockSpec((tm, FFN_TILE), lambda i, j: (i, j))
    return _pcall(body, name="ffn_up", grid=(T // tm, H // FFN_TILE), in_specs=[pl.BlockSpec((tm, D), lambda i, j: (i, 0)), wspec, wspec],
                  out_specs=[ospec] * 3, out_shape=[_sds((T, H), F32), _sds((T, H), F32), _sds((T, H), BF16)],
                  compiler_params=_cparams(("arbitrary", "arbitrary")))(f, wgt, wut)


def ffn_down_bwd(dh2b, wd, gate, up):
    T, D = dh2b.shape
    H = wd.shape[0]
    tm = _row_tile(T)
    ospec = pl.BlockSpec((tm, FFN_TILE), lambda i, j: (i, j))

    def epi(da, g, u):
        return (da * u * _dsilu(g), da * _silu(g))

    return _matmul("ffn_down_bwd", [(dh2b, wd)],
                   [(pl.BlockSpec((tm, D), lambda i, j: (i, 0)), pl.BlockSpec((FFN_TILE, D), lambda i, j: (j, 0)))],
                   NT, (T // tm, H // FFN_TILE), [_sds((T, H), BF16)] * 2, [ospec, ospec],
                   extra=(gate, up), extra_specs=(ospec, ospec), epilogue=epi)


def ffn_up_bwd(dgate, dup, wgt, wut):
    T, H = dgate.shape
    D = wgt.shape[1]
    tm = min(512, T)
    tn = 512
    aspec = pl.BlockSpec((tm, H), lambda i, n: (i, 0))
    wspec = pl.BlockSpec((H, tn), lambda i, n: (0, n))
    return _matmul("ffn_up_bwd", [(dgate, wgt), (dup, wut)], [(aspec, wspec), (aspec, wspec)], NN, (T // tm, D // tn),
                   [_sds((T, D), F32)], [pl.BlockSpec((tm, tn), lambda i, n: (i, n))])[0]


def attn_br_bwd(dout_a, wab):
    T, D = dout_a.shape
    J, R, nb = wab.shape
    tm = _row_tile(T)
    return _matmul("attn_br_bwd", [(dout_a, wab)],
                   [(pl.BlockSpec((tm, nb), lambda i, j: (i, j)), pl.BlockSpec((None, R, nb), lambda i, j: (j, 0, 0)))],
                   NT, (T // tm, J), [_sds((T, R), BF16)], [pl.BlockSpec((tm, R), lambda i, j: (i, 0))], nred=J, acc_shape=(tm, R))[0]


def _adam_math(w, g, m, v):
    m2 = ADAM_B1 * m + (1.0 - ADAM_B1) * g
    v2 = ADAM_B2 * v + (1.0 - ADAM_B2) * (g * g)
    m_hat = m2 / (1.0 - ADAM_B1 ** ADAM_STEP)
    v_hat = v2 / (1.0 - ADAM_B2 ** ADAM_STEP)
    delta = -ADAM_LR * (m_hat / (jnp.sqrt(v_hat) + ADAM_EPS) + ADAM_WD * w)
    return delta, m2, v2


def _sum_partials(own, parts):
    g = None if own is None else own.astype(F32)
    if parts is not None:
        for s in range(parts.shape[0]):
            t = parts[s].astype(F32)
            g = t if g is None else g + t
    return g


def adamw(name, parts, w, m, v, own=None, own_slot=None):
    R, C = w.shape
    tr, tc = R, C
    for cand in (256, 176, 128, 64, 32, 16, 8):
        if R % cand == 0 and R > cand:
            tr = cand
            break
    if tr == R and R > 256:
        tc = 256
    given = [a for a in (parts, own) if a is not None]
    pre = own_slot is not None

    def body(*refs):
        refs = refs[1:] if pre else refs
        p_ref = refs[0] if parts is not None else None
        o_ref = refs[len(given) - 1] if own is not None else None
        w_ref, m_ref, v_ref, g_ref, d_ref, m2_ref, v2_ref = refs[-7:]
        g = _sum_partials(None if o_ref is None else o_ref[...], p_ref)
        d, m2, v2 = _adam_math(w_ref[...], g, m_ref[...], v_ref[...])
        g_ref[...] = g
        d_ref[...] = d
        m2_ref[...] = m2
        v2_ref[...] = v2

    blk = pl.BlockSpec((tr, tc), lambda i, j, *s: (i, j))
    specs = [] if parts is None else [pl.BlockSpec((parts.shape[0], tr, tc), lambda i, j, *s: (0, i, j))]
    if own is not None:
        specs.append(pl.BlockSpec((None, tr, tc), lambda i, j, s: (s[0], i, j)) if pre else blk)
    specs += [blk] * 3
    grid = (R // tr, C // tc)
    out_shape = [_sds((R, C), F32)] * 4
    params = _cparams(("arbitrary", "arbitrary"))
    if not pre:
        return _pcall(body, name=name, grid=grid, in_specs=specs, out_specs=[blk] * 4, out_shape=out_shape,
                      compiler_params=params)(*given, w, m, v)
    spec = pltpu.PrefetchScalarGridSpec(num_scalar_prefetch=1, grid=grid, in_specs=specs, out_specs=[blk] * 4)
    return _pcall(body, name=name, grid_spec=spec, out_shape=out_shape,
                  compiler_params=params)(jnp.asarray(own_slot, jnp.int32).reshape(1), *given, w, m, v)


_HBM = pl.BlockSpec(memory_space=pltpu.HBM)
_SEM = pl.BlockSpec(memory_space=pltpu.SEMAPHORE)
_ANY = pl.BlockSpec(memory_space=pl.ANY)
_SPLIT_PARAMS = dict(compiler_params=pltpu.CompilerParams(has_side_effects=pltpu.SideEffectType.DATAFLOW_SIDE_EFFECTING))
ICI_SAME_CORE = (2, 4, 6)
ALL_PEERS = (1, 2, 3, 4, 5, 6, 7)
LAND_SLOTS = {"gather": N_DEV, "scatter": N_DEV - 1, "pair": 4, "scatter_core": 3}


def _mesh_pos():
    x, y, c = lax.axis_index("x"), lax.axis_index("y"), lax.axis_index("c")
    return x, y, c, 4 * x + 2 * y + c


def _peer_of(k, x, y, c):
    px = 1 - x if k & 4 else x
    py = 1 - y if k & 2 else y
    pc = 1 - c if k & 1 else c
    return (px, py, pc), 4 * px + 2 * py + pc


def _split_copies(mode, ks, srcs, lands, send_sems, recv_sems):
    x, y, c, me = _mesh_pos()
    pairs = []
    for a in range(len(lands)):
        for j, k in enumerate(ks):
            dev, peer = _peer_of(k, x, y, c)
            i = a * len(ks) + j
            if mode == "gather":
                s_out, d_out, d_in = lands[a].at[me], lands[a].at[me], lands[a].at[peer]
            elif mode == "scatter":
                s_out, d_out, d_in = srcs[a].at[peer], lands[a].at[k - 1], lands[a].at[k - 1]
            elif mode == "pair":
                dev, _ = _peer_of(1, x, y, c)
                _, theirs = _peer_of(k | 1, x, y, c)
                s_out, d_out, d_in = srcs[a].at[theirs], lands[a].at[j], lands[a].at[j]
            elif mode == "scatter_core":
                s_out, d_out, d_in = srcs[a].at[j + 1], lands[a].at[j], lands[a].at[j]
            else:
                dev, _ = _peer_of(1, x, y, c)
                _, theirs = _peer_of(k | 1, x, y, c)
                s_out, d_out, d_in = lands[a].at[peer], lands[a].at[peer], lands[a].at[theirs]
            both = [pltpu.make_async_remote_copy(src_ref=s_out, dst_ref=d, send_sem=send_sems.at[i], recv_sem=recv_sems.at[i],
                                                 device_id=dev, device_id_type=pl.DeviceIdType.MESH) for d in (d_out, d_in)]
            pairs.append(tuple(both))
    return pairs


def split_start(name, mode, ks, srcs, lands=None, after=None):
    n, nk = len(srcs) if lands is None else len(lands), len(ks)
    srcs = [pltpu.with_memory_space_constraint(s, pltpu.HBM) for s in srcs]
    if lands is None:
        shapes = [((N_DEV,) + s.shape) if mode == "gather" else ((LAND_SLOTS[mode],) + s.shape[1:]) for s in srcs]
        lands = [lax.empty(shp, s.dtype) for shp, s in zip(shapes, srcs)]
    lands = [pltpu.with_memory_space_constraint(l, pltpu.HBM) for l in lands]
    both = srcs + lands
    extra = [] if after is None else [after]

    def body(*refs):
        src_refs, land_refs = refs[:len(srcs)], refs[len(srcs):len(both)]
        send_sems, recv_sems = refs[len(both) + len(extra)], refs[len(both) + len(extra) + 1]
        token = refs[-1]
        for out, _ in _split_copies(mode, ks, src_refs, land_refs, send_sems, recv_sems):
            out.start()
        token[...] = jnp.zeros_like(token)

    out_shape = (pltpu.SemaphoreType.DMA((n * nk,)), pltpu.SemaphoreType.DMA((n * nk,)),
                 *[pltpu.HBM(a.shape, a.dtype) for a in both], _sds((8, 128), F32))
    res = _raw_call(body, name=name, out_shape=out_shape, in_specs=[_HBM] * len(both) + [_ANY] * len(extra),
                    out_specs=(_SEM, _SEM, *[_HBM] * len(both), pl.BlockSpec(memory_space=pltpu.VMEM)),
                    input_output_aliases={i: 2 + i for i in range(len(both))}, **_SPLIT_PARAMS)(*both, *extra)
    _PENDING.append(res[-1])
    return dict(mode=mode, ks=ks, sems=(res[0], res[1]), srcs=list(res[2:2 + len(srcs)]),
                lands=list(res[2 + len(srcs):2 + len(both)]), token=res[-1])


def split_wait(name, h, after):
    ns = len(h["srcs"])
    both = h["srcs"] + h["lands"]
    after = list(after) if isinstance(after, (list, tuple)) else [after]

    def body(*refs):
        src_refs, land_refs = refs[:ns], refs[ns:len(both)]
        send_sems, recv_sems = refs[len(both)], refs[len(both) + 1]
        for out, arriving in _split_copies(h["mode"], h["ks"], src_refs, land_refs, send_sems, recv_sems):
            out.wait_send()
            arriving.wait_recv()

    res = _raw_call(body, name=name, out_shape=tuple(pltpu.HBM(a.shape, a.dtype) for a in both),
                    in_specs=[_HBM] * len(both) + [_SEM, _SEM] + [_ANY] * len(after), out_specs=tuple([_HBM] * len(both)),
                    input_output_aliases={i: i for i in range(len(both))}, **_SPLIT_PARAMS)(*both, *h["sems"], *after)
    return list(res[:ns]), list(res[ns:])


FORWARD_BLOCKS = (0, 2, 4, 6)


def pair_sum(name, mine, slots, theirs):
    P, R, C = theirs.shape
    tc = 512

    def body(s_ref, a_ref, b_ref, o_ref):
        o_ref[...] = (a_ref[...].astype(F32) + b_ref[...].astype(F32)).astype(o_ref.dtype)

    blk = pl.BlockSpec((None, R, tc), lambda p, i, s: (p, 0, i))
    spec = pltpu.PrefetchScalarGridSpec(num_scalar_prefetch=1, grid=(P, C // tc),
                                        in_specs=[pl.BlockSpec((None, R, tc), lambda p, i, s: (s[p], 0, i)), blk], out_specs=blk)
    return _pcall(body, name=name, grid_spec=spec, out_shape=_sds((P, R, C), theirs.dtype),
                  compiler_params=_cparams(("arbitrary", "arbitrary")))(slots, mine, theirs)


def _rope_parts(pos_ref, inv_ref):
    ang = pos_ref[...] * inv_ref[...]
    return jnp.cos(ang), jnp.sin(ang)


def _rot_half(t):
    lane = lax.broadcasted_iota(jnp.int32, t.shape, 1)
    return jnp.where((lane % HEAD_DIM) < HEAD_DIM // 2, -pltpu.roll(t, 128 - HEAD_DIM // 2, 1), pltpu.roll(t, HEAD_DIM // 2, 1))


def _attn_mask(n):
    row = lax.broadcasted_iota(jnp.int32, (BLK, 2 * BLK), 0)
    col = lax.broadcasted_iota(jnp.int32, (BLK, 2 * BLK), 1)
    dist = row + BLK - col
    return (dist >= 0) & (dist < BLK) & ((n * BLK - BLK + col) >= 0)


def _attn_specs(T):
    prev = lambda n: jnp.maximum(n - 1, 0)
    kc = pl.BlockSpec((BLK, KV_DIM), lambda n: (n, OFF_K // KV_DIM))
    kp = pl.BlockSpec((BLK, KV_DIM), lambda n: (prev(n), OFF_K // KV_DIM))
    vc = pl.BlockSpec((BLK, KV_DIM), lambda n: (n, OFF_V // KV_DIM))
    vp = pl.BlockSpec((BLK, KV_DIM), lambda n: (prev(n), OFF_V // KV_DIM))
    pc = pl.BlockSpec((BLK, 1), lambda n: (n, 0))
    pp = pl.BlockSpec((BLK, 1), lambda n: (prev(n), 0))
    inv = pl.BlockSpec((1, 128), lambda n: (0, 0))
    sink = pl.BlockSpec(memory_space=pltpu.SMEM)
    return kc, kp, vc, vp, pc, pp, inv, sink


def _softmax_sink(sc, valid, sink):
    sc = jnp.where(valid, sc * (HEAD_DIM ** -0.5), -1e30)
    m = jnp.maximum(jnp.max(sc, axis=1, keepdims=True), sink)
    e = jnp.exp(sc - m)
    es = jnp.exp(sink - m)
    den = jnp.sum(e, axis=1, keepdims=True) + es
    return e / den, es / den


def attn_fwd(projp, posf, inv128, sinks):
    T = projp.shape[0]
    kc, kp, vc, vp, pc, pp, inv, sink = _attn_specs(T)

    def body(q_ref, kc_ref, kp_ref, vc_ref, vp_ref, pc_ref, pp_ref, inv_ref, sink_ref, o_ref, qr_ref, kr_ref):
        n = pl.program_id(0)
        cos_c, sin_c = _rope_parts(pc_ref, inv_ref)
        cos_p, sin_p = _rope_parts(pp_ref, inv_ref)
        valid = _attn_mask(n)
        k_c, k_p = [], []
        for s in range(KV_DIM // 128):
            t = kc_ref[:, 128 * s:128 * (s + 1)]
            k_c.append((t * cos_c + _rot_half(t) * sin_c).astype(BF16))
            kr_ref[:, 128 * s:128 * (s + 1)] = k_c[s]
            t = kp_ref[:, 128 * s:128 * (s + 1)]
            k_p.append((t * cos_p + _rot_half(t) * sin_p).astype(BF16))
        kcat, vcat = [], []
        for hk in range(KV_HEADS):
            lo = HEAD_DIM * (hk % 2)
            kcat.append(jnp.concatenate([k_p[hk // 2][:, lo:lo + HEAD_DIM], k_c[hk // 2][:, lo:lo + HEAD_DIM]], axis=0))
            vcat.append(jnp.concatenate([vp_ref[:, HEAD_DIM * hk:HEAD_DIM * (hk + 1)], vc_ref[:, HEAD_DIM * hk:HEAD_DIM * (hk + 1)]], axis=0)
                        .astype(BF16))
        q_heads = []
        for s in range(Q_DIM // 128):
            t = q_ref[:, 128 * s:128 * (s + 1)]
            qs = (t * cos_c + _rot_half(t) * sin_c).astype(BF16)
            qr_ref[:, 128 * s:128 * (s + 1)] = qs
            q_heads += [qs[:, :HEAD_DIM], qs[:, HEAD_DIM:]]
        G = ATTN_HEADS // KV_HEADS
        scores = [_dot(q_heads[hq], kcat[hq // G], NT) for hq in range(ATTN_HEADS)]
        probs = [_softmax_sink(scores[hq], valid, sink_ref[0, hq])[0] for hq in range(ATTN_HEADS)]
        outs = [_dot(probs[hq], vcat[hq // G], NN) for hq in range(ATTN_HEADS)]
        for s in range(Q_DIM // 128):
            o_ref[:, 128 * s:128 * (s + 1)] = jnp.concatenate([outs[2 * s], outs[2 * s + 1]], axis=1).astype(BF16)

    qspec = pl.BlockSpec((BLK, Q_DIM), lambda n: (n, OFF_Q // Q_DIM))
    orow = pl.BlockSpec((BLK, Q_DIM), lambda n: (n, 0))
    krow = pl.BlockSpec((BLK, KV_DIM), lambda n: (n, 0))
    return _pcall(body, name="attn_fwd", grid=(T // BLK,), in_specs=[qspec, kc, kp, vc, vp, pc, pp, inv, sink],
                  out_specs=[orow, orow, krow], out_shape=[_sds((T, Q_DIM), BF16), _sds((T, Q_DIM), BF16), _sds((T, KV_DIM), BF16)],
                  compiler_params=_cparams(("arbitrary",)))(projp, projp, projp, projp, projp, posf, posf, inv128, sinks)


def attn_bwd(qr, kr, projp, dattn, posf, inv128, sinks):
    T = projp.shape[0]
    _, _, vc, vp, pc, pp, inv, sink = _attn_specs(T)
    G = ATTN_HEADS // KV_HEADS

    def body(qr_ref, krc_ref, krp_ref, vc_ref, vp_ref, do_ref, pc_ref, pp_ref, inv_ref, sink_ref, dq_ref, dk_ref, dv_ref, dsk_ref):
        n = pl.program_id(0)

        @pl.when(n == 0)
        def _():
            dk_ref[...] = jnp.zeros_like(dk_ref)
            dv_ref[...] = jnp.zeros_like(dv_ref)
            dsk_ref[...] = jnp.zeros_like(dsk_ref)

        cos_c, sin_c = _rope_parts(pc_ref, inv_ref)
        cos_p, sin_p = _rope_parts(pp_ref, inv_ref)
        valid = _attn_mask(n)
        lane = lax.broadcasted_iota(jnp.int32, (1, 128), 1)
        kcat, vcat = [], []
        for hk in range(KV_HEADS):
            ksl = slice(HEAD_DIM * hk, HEAD_DIM * (hk + 1))
            kcat.append(jnp.concatenate([krp_ref[:, ksl], krc_ref[:, ksl]], axis=0))
            vcat.append(jnp.concatenate([vp_ref[:, ksl], vc_ref[:, ksl]], axis=0).astype(BF16))
        H = range(ATTN_HEADS)
        q_heads = [qr_ref[:, HEAD_DIM * hq:HEAD_DIM * (hq + 1)] for hq in H]
        do_heads = [do_ref[:, HEAD_DIM * hq:HEAD_DIM * (hq + 1)] for hq in H]
        soft = [_softmax_sink(_dot(q_heads[hq], kcat[hq // G], NT), valid, sink_ref[0, hq]) for hq in H]
        dps = [_dot(do_heads[hq], vcat[hq // G], NT) for hq in H]
        deltas = [jnp.sum(soft[hq][0] * dps[hq], axis=1, keepdims=True) for hq in H]
        dss = [(soft[hq][0] * (dps[hq] - deltas[hq]) * (HEAD_DIM ** -0.5)).astype(BF16) for hq in H]
        pbs = [soft[hq][0].astype(BF16) for hq in H]
        dsk = jnp.zeros((1, 128), F32)
        for hq in H:
            dsk = dsk + jnp.where(lane == hq, -jnp.sum(soft[hq][1] * deltas[hq], axis=0, keepdims=True), 0.0)
        dsk_ref[...] += dsk
        dq_heads = [_dot(dss[hq], kcat[hq // G], NN) for hq in H]
        dk_parts = [_dot(dss[hq], q_heads[hq], TN) for hq in H]
        dv_parts = [_dot(pbs[hq], do_heads[hq], TN) for hq in H]
        dk_heads = [sum(dk_parts[G * hk + 1:G * (hk + 1)], dk_parts[G * hk]) for hk in range(KV_HEADS)]
        dv_heads = [sum(dv_parts[G * hk + 1:G * (hk + 1)], dv_parts[G * hk]) for hk in range(KV_HEADS)]
        for s in range(Q_DIM // 128):
            t = jnp.concatenate([dq_heads[2 * s], dq_heads[2 * s + 1]], axis=1)
            dq_ref[:, 128 * s:128 * (s + 1)] = (t * cos_c - _rot_half(t) * sin_c).astype(BF16)
        cur = pl.ds(pl.multiple_of(n * BLK, BLK), BLK)
        prv = pl.ds(pl.multiple_of(jnp.maximum(n - 1, 0) * BLK, BLK), BLK)
        for s in range(KV_DIM // 128):
            tc = jnp.concatenate([dk_heads[2 * s][BLK:], dk_heads[2 * s + 1][BLK:]], axis=1)
            tp = jnp.concatenate([dk_heads[2 * s][:BLK], dk_heads[2 * s + 1][:BLK]], axis=1)
            cols = slice(128 * s, 128 * (s + 1))
            dk_ref[cur, cols] += tc * cos_c - _rot_half(tc) * sin_c
            dk_ref[prv, cols] += tp * cos_p - _rot_half(tp) * sin_p
            dv_ref[cur, cols] += jnp.concatenate([dv_heads[2 * s][BLK:], dv_heads[2 * s + 1][BLK:]], axis=1)
            dv_ref[prv, cols] += jnp.concatenate([dv_heads[2 * s][:BLK], dv_heads[2 * s + 1][:BLK]], axis=1)

    qrow = pl.BlockSpec((BLK, Q_DIM), lambda n: (n, 0))
    krc = pl.BlockSpec((BLK, KV_DIM), lambda n: (n, 0))
    krp = pl.BlockSpec((BLK, KV_DIM), lambda n: (jnp.maximum(n - 1, 0), 0))
    whole = pl.BlockSpec((T, KV_DIM), lambda n: (0, 0))
    return _pcall(body, name="attn_bwd", grid=(T // BLK,), in_specs=[qrow, krc, krp, vc, vp, qrow, pc, pp, inv, sink],
                  out_specs=[qrow, whole, whole, pl.BlockSpec((1, 128), lambda n: (0, 0))],
                  out_shape=[_sds((T, Q_DIM), BF16), _sds((T, KV_DIM), F32), _sds((T, KV_DIM), F32), _sds((1, 128), F32)],
                  compiler_params=_cparams(("arbitrary",)))(qr, kr, kr, projp, projp, dattn, posf, posf, inv128, sinks)


CONV_CB = 256


def _shift_down(x, s):
    row = lax.broadcasted_iota(jnp.int32, x.shape, 0)
    return jnp.where(row >= s, pltpu.roll(x, s, 0), 0.0)


def _shift_up(x, s):
    T = x.shape[0]
    row = lax.broadcasted_iota(jnp.int32, x.shape, 0)
    return jnp.where(row < T - s, pltpu.roll(x, T - s, 0), 0.0)


def _conv_pre(x, w_ref, b_ref):
    acc = x * w_ref[CONV_WIDTH - 1:CONV_WIDTH, :] + b_ref[...]
    for s in range(1, CONV_WIDTH):
        acc = acc + _shift_down(x, s) * w_ref[CONV_WIDTH - 1 - s:CONV_WIDTH - s, :]
    return acc


def conv_fwd(projp, conv_w, conv_b):
    T = projp.shape[0]

    def body(x_ref, w_ref, b_ref, o_ref):
        o_ref[...] = _silu(_conv_pre(x_ref[...], w_ref, b_ref))

    return _pcall(body, name="conv_fwd", grid=(CONV_DIM // CONV_CB,),
                  in_specs=[pl.BlockSpec((T, CONV_CB), lambda c: (0, OFF_XBC // CONV_CB + c)),
                            pl.BlockSpec((CONV_WIDTH, CONV_CB), lambda c: (0, c)), pl.BlockSpec((1, CONV_CB), lambda c: (0, c))],
                  out_specs=pl.BlockSpec((T, CONV_CB), lambda c: (0, c)), out_shape=_sds((T, CONV_DIM), F32),
                  compiler_params=_cparams(("arbitrary",)))(projp, conv_w, conv_b)


def conv_bwd(name, projp, dact, conv_w, conv_b, col0):
    T, C = dact.shape
    c0 = col0 // CONV_CB

    def body(x_ref, da_ref, w_ref, b_ref, dx_ref, dw_ref, db_ref):
        x = x_ref[...]
        dpre = da_ref[...] * _dsilu(_conv_pre(x, w_ref, b_ref))
        dx = dpre * w_ref[CONV_WIDTH - 1:CONV_WIDTH, :]
        dw_ref[CONV_WIDTH - 1:CONV_WIDTH, :] = jnp.sum(dpre * x, axis=0, keepdims=True)
        for s in range(1, CONV_WIDTH):
            i = CONV_WIDTH - 1 - s
            dx = dx + _shift_up(dpre, s) * w_ref[i:i + 1, :]
            dw_ref[i:i + 1, :] = jnp.sum(dpre * _shift_down(x, s), axis=0, keepdims=True)
        dx_ref[...] = dx.astype(BF16)
        db_ref[...] = jnp.sum(dpre, axis=0, keepdims=True)

    return _pcall(body, name=name, grid=(C // CONV_CB,),
                  in_specs=[pl.BlockSpec((T, CONV_CB), lambda c: (0, OFF_XBC // CONV_CB + c0 + c)),
                            pl.BlockSpec((T, CONV_CB), lambda c: (0, c)),
                            pl.BlockSpec((CONV_WIDTH, CONV_CB), lambda c: (0, c0 + c)), pl.BlockSpec((1, CONV_CB), lambda c: (0, c0 + c))],
                  out_specs=[pl.BlockSpec((T, CONV_CB), lambda c: (0, c)), pl.BlockSpec((CONV_WIDTH, CONV_CB), lambda c: (0, c)),
                             pl.BlockSpec((1, CONV_CB), lambda c: (0, c))],
                  out_shape=[_sds((T, C), BF16), _sds((CONV_WIDTH, C), F32), _sds((1, C), F32)],
                  compiler_params=_cparams(("arbitrary",)))(projp, dact, conv_w, conv_b)


def _softplus(x):
    return jnp.maximum(x, 0.0) + jnp.log1p(jnp.exp(-jnp.abs(x)))


def _tri(lower):
    r = lax.broadcasted_iota(jnp.int32, (BLK, BLK), 0)
    c = lax.broadcasted_iota(jnp.int32, (BLK, BLK), 1)
    return (r >= c) if lower else (c >= r)


def _ssd_chunk_setup(dt_ref, dtb_ref, alog_ref):
    raw = dt_ref[...] + dtb_ref[...]
    dt = _softplus(raw)
    aneg = -jnp.exp(alog_ref[...])
    a = dt * aneg
    acs = jnp.dot(_tri(True).astype(F32), a, precision=lax.Precision.HIGHEST, preferred_element_type=F32)
    return raw, dt, aneg, acs, acs.T


def _ssd_specs(T, rev):
    nc = T // BLK
    ci = (lambda c: nc - 1 - c) if rev else (lambda c: c)
    xs = pl.BlockSpec((BLK, D_INNER), lambda c: (ci(c), 0))
    bm = pl.BlockSpec((BLK, SSM_GROUPS * D_STATE), lambda c: (ci(c), D_INNER // (SSM_GROUPS * D_STATE)))
    cm = pl.BlockSpec((BLK, SSM_GROUPS * D_STATE), lambda c: (ci(c), D_INNER // (SSM_GROUPS * D_STATE) + 1))
    dt = pl.BlockSpec((BLK, DT_PAD), lambda c: (ci(c), OFF_DT // DT_PAD))
    v128 = pl.BlockSpec((1, 128), lambda c: (0, 0))
    dfull = pl.BlockSpec((1, D_INNER), lambda c: (0, 0))
    st = pl.BlockSpec((None, SSM_HEADS, HEAD_DIM, D_STATE), lambda c: (ci(c), 0, 0, 0))
    return xs, bm, cm, dt, v128, dfull, st, ci


GW = HEADS_PER_GROUP * HEAD_DIM


def _expanders():
    e = np.zeros((SSM_GROUPS, 128, GW), np.float32)
    for g in range(SSM_GROUPS):
        for hh in range(HEADS_PER_GROUP):
            e[g, HEADS_PER_GROUP * g + hh, HEAD_DIM * hh:HEAD_DIM * (hh + 1)] = 1.0
    return jnp.asarray(e, BF16), jnp.asarray(np.transpose(e, (0, 2, 1)).copy(), BF16)


def _split2(v):
    hi = lax.bitcast_convert_type(lax.bitcast_convert_type(v, jnp.uint32) & jnp.uint32(0xFFFF0000), F32)
    return hi.astype(BF16), (v - hi).astype(BF16)


def _dotx(a, b):
    if a.dtype == BF16:
        hi, lo = _split2(b)
        return jnp.dot(a, hi, preferred_element_type=F32) + jnp.dot(a, lo, preferred_element_type=F32)
    hi, lo = _split2(a)
    return jnp.dot(hi, b, preferred_element_type=F32) + jnp.dot(lo, b, preferred_element_type=F32)


def _decay(acs, acsT, h, tril):
    return jnp.where(tril, jnp.exp(jnp.where(tril, acs[:, h:h + 1] - acsT[h:h + 1, :], 0.0)), 0.0)


def ssd_fwd(xbc, projp, dtb, alog, dfull):
    T = xbc.shape[0]
    nc = T // BLK
    xs, bm, cm, dts, v128, dfs, st, _ = _ssd_specs(T, False)
    E, _ = _expanders()

    def body(xs_ref, b_ref, c_ref, dt_ref, dtb_ref, alog_ref, d_ref, e_ref, y_ref, st_ref, h_scr):
        c = pl.program_id(0)

        @pl.when(c == 0)
        def _():
            h_scr[...] = jnp.zeros_like(h_scr)

        _, dt, _, acs, acsT = _ssd_chunk_setup(dt_ref, dtb_ref, alog_ref)
        tril = _tri(True)
        alast = acs[BLK - 1:BLK, :]
        eacs = jnp.exp(acs)
        wmat = jnp.exp(alast - acs)
        gam = jnp.exp(alast)
        for g in range(SSM_GROUPS):
            gl = slice(GW * g, GW * (g + 1))
            hsl = slice(HEADS_PER_GROUP * g, HEADS_PER_GROUP * (g + 1))
            heads = [HEADS_PER_GROUP * g + hh for hh in range(HEADS_PER_GROUP)]
            Eg = e_ref[g]
            B = b_ref[:, D_STATE * g:D_STATE * (g + 1)].astype(BF16)
            C = c_ref[:, D_STATE * g:D_STATE * (g + 1)].astype(BF16)
            cb = _dot(C, B, NT)
            x_g = xs_ref[:, gl]
            xd_g = x_g * _dotx(dt, Eg)
            hold = h_scr[hsl]
            st_ref[hsl] = hold
            hcat = hold.reshape(GW, D_STATE)
            yoff = _dotx(eacs, Eg) * _dot(C, hcat, NT)
            S = _dot(xd_g * _dotx(wmat, Eg), B, TN)
            Ms = [cb * _decay(acs, acsT, h, tril) for h in heads]
            ys = [_dot(Ms[hh], xd_g[:, HEAD_DIM * hh:HEAD_DIM * (hh + 1)], NN) for hh in range(HEADS_PER_GROUP)]
            for hh, h in enumerate(heads):
                h_scr[h] = gam[:, h:h + 1] * hold[hh] + S[HEAD_DIM * hh:HEAD_DIM * (hh + 1)]
            y_ref[:, gl] = jnp.concatenate(ys, axis=1) + yoff + d_ref[:, gl] * x_g

    espec = pl.BlockSpec((SSM_GROUPS, 128, GW), lambda c: (0, 0, 0))
    return _pcall(body, name="ssd_fwd", grid=(nc,), in_specs=[xs, bm, cm, dts, v128, v128, dfs, espec],
                  out_specs=[xs, st], out_shape=[_sds((T, D_INNER), F32), _sds((nc, SSM_HEADS, HEAD_DIM, D_STATE), F32)],
                  scratch_shapes=[pltpu.VMEM((SSM_HEADS, HEAD_DIM, D_STATE), F32)],
                  compiler_params=_cparams(("arbitrary",)))(xbc, xbc, xbc, projp, dtb, alog, dfull, E)


def ssd_bwd(xbc, projp, dtb, alog, dfull, states, dy):
    T = xbc.shape[0]
    nc = T // BLK
    xs, bm, cm, dts, v128, dfs, st, ci = _ssd_specs(T, True)
    gn = SSM_GROUPS * D_STATE
    E, ET = _expanders()

    def body(xs_ref, b_ref, c_ref, dt_ref, dtb_ref, alog_ref, d_ref, st_ref, dy_ref, e_ref, et_ref,
             dxs_ref, dB_ref, dC_ref, ddt_ref, dal_ref, dD_ref, ddtb_ref, dh_scr):
        i = pl.program_id(0)

        @pl.when(i == 0)
        def _():
            dh_scr[...] = jnp.zeros_like(dh_scr)
            dal_ref[...] = jnp.zeros_like(dal_ref)
            dD_ref[...] = jnp.zeros_like(dD_ref)
            ddtb_ref[...] = jnp.zeros_like(ddtb_ref)

        raw, dt, aneg, acs, acsT = _ssd_chunk_setup(dt_ref, dtb_ref, alog_ref)
        tril = _tri(True)
        lane = lax.broadcasted_iota(jnp.int32, (BLK, 128), 1)
        sub = lax.broadcasted_iota(jnp.int32, (BLK, 128), 0)
        alast = acs[BLK - 1:BLK, :]
        eacs = jnp.exp(acs)
        wmat = jnp.exp(alast - acs)
        gam = jnp.exp(alast)
        gcol = jnp.exp(acsT[:, BLK - 1:BLK])
        ds_col = jnp.zeros((BLK, 128), F32)
        ds_row = jnp.zeros((BLK, 128), F32)
        ddt_col = jnp.zeros((BLK, 128), F32)
        dDm = jnp.zeros((BLK, 128), F32)
        hl = [slice(HEAD_DIM * hh, HEAD_DIM * (hh + 1)) for hh in range(HEADS_PER_GROUP)]
        for g in range(SSM_GROUPS):
            gl = slice(GW * g, GW * (g + 1))
            gs = slice(D_STATE * g, D_STATE * (g + 1))
            hsl = slice(HEADS_PER_GROUP * g, HEADS_PER_GROUP * (g + 1))
            heads = [HEADS_PER_GROUP * g + hh for hh in range(HEADS_PER_GROUP)]
            Eg, ETg = e_ref[g], et_ref[g]
            B = b_ref[:, gs].astype(BF16)
            C = c_ref[:, gs].astype(BF16)
            cb = _dot(C, B, NT)
            x_g, dy_g = xs_ref[:, gl], dy_ref[:, gl]
            dt_x, w_x = _dotx(dt, Eg), _dotx(wmat, Eg)
            xd_g = x_g * dt_x
            dye = dy_g * _dotx(eacs, Eg)
            hcat = st_ref[hsl].reshape(GW, D_STATE)
            dSv = dh_scr[hsl]
            dScat = dSv.reshape(GW, D_STATE)
            dDm = dDm + _dotx(dy_g * x_g, ETg)
            dH_y = _dot(dye, C, TN)
            dC_g = _dot(dye, hcat, NN)
            ds_col = ds_col + _dotx(dye * _dot(C, hcat, NT), ETg)
            dxdw = _dot(B, dScat, NT)
            dB_g = _dot(xd_g * w_x, dScat, NN)
            dww = _dotx(xd_g * dxdw, ETg) * wmat
            ds_col = ds_col - dww + jnp.where(sub == BLK - 1, jnp.sum(dww, axis=0, keepdims=True), 0.0)
            hd = jnp.sum(_dotx(Eg, dScat * hcat), axis=1, keepdims=True) * gcol
            ds_row = ds_row - jnp.where(lane == BLK - 1, hd, 0.0)
            decays = [_decay(acs, acsT, h, tril) for h in heads]
            Ms = [cb * d for d in decays]
            dMs = [_dot(dy_g[:, hl[hh]], xd_g[:, hl[hh]], NT) for hh in range(HEADS_PER_GROUP)]
            dxd1 = [_dot(Ms[hh], dy_g[:, hl[hh]], TN) for hh in range(HEADS_PER_GROUP)]
            dG = jnp.zeros((BLK, BLK), F32)
            for hh, h in enumerate(heads):
                Q = dMs[hh] * Ms[hh]
                ds_col = ds_col + jnp.where(lane == h, jnp.sum(Q, axis=1, keepdims=True), 0.0)
                ds_row = ds_row + jnp.where(sub == h, jnp.sum(Q, axis=0, keepdims=True), 0.0)
                dG = dG + dMs[hh] * decays[hh]
            dxd_g = jnp.concatenate(dxd1, axis=1) + w_x * dxdw
            dxs_ref[:, gl] = d_ref[:, gl] * dy_g + dxd_g * dt_x
            ddt_col = ddt_col + _dotx(dxd_g * x_g, ETg)
            dC_ref[:, gs] = dC_g + _dot(dG, B, NN)
            dB_ref[:, gs] = dB_g + _dot(dG, C, TN)
            for hh, h in enumerate(heads):
                dh_scr[h] = gam[:, h:h + 1] * dSv[hh] + dH_y[hl[hh]]
        ds_all = ds_col - ds_row.T
        da = jnp.dot(_tri(False).astype(F32), ds_all, precision=lax.Precision.HIGHEST, preferred_element_type=F32)
        ddt = ddt_col + da * aneg
        draw = jnp.where(lane < SSM_HEADS, ddt * _sigmoid(raw), 0.0)
        ddt_ref[...] = draw.astype(BF16)
        dal_ref[...] += jnp.sum(da * dt, axis=0, keepdims=True) * aneg
        ddtb_ref[...] += jnp.sum(draw, axis=0, keepdims=True)
        dD_ref[...] += jnp.sum(dDm, axis=0, keepdims=True)

    gblk = pl.BlockSpec((BLK, gn), lambda c: (ci(c), 0))
    espec = pl.BlockSpec((SSM_GROUPS, 128, GW), lambda c: (0, 0, 0))
    etspec = pl.BlockSpec((SSM_GROUPS, GW, 128), lambda c: (0, 0, 0))
    return _pcall(body, name="ssd_bwd", grid=(nc,), in_specs=[xs, bm, cm, dts, v128, v128, dfs, st, xs, espec, etspec],
                  out_specs=[xs, gblk, gblk, pl.BlockSpec((BLK, DT_PAD), lambda c: (ci(c), 0)), v128, v128, v128],
                  out_shape=[_sds((T, D_INNER), F32), _sds((T, gn), F32), _sds((T, gn), F32), _sds((T, DT_PAD), BF16),
                             _sds((1, 128), F32), _sds((1, 128), F32), _sds((1, 128), F32)],
                  scratch_shapes=[pltpu.VMEM((SSM_HEADS, HEAD_DIM, D_STATE), F32)],
                  compiler_params=_cparams(("arbitrary",)))(xbc, xbc, xbc, projp, dtb, alog, dfull, states, dy, E, ET)


_WIN_ORDER = ("z", "ga", "gs", "xbc", "q", "k", "v", "dt")


PERM_TILE = 512


def _win_row_moves():
    off = dict(z=OFF_Z, ga=OFF_GA, gs=OFF_GS, xbc=OFF_XBC, q=OFF_Q, k=OFF_K, v=OFF_V, dt=OFF_DT)
    per = IN_DIM // N_DEV
    tiles = [[] for _ in range(-(-IN_PAD // PERM_TILE))]
    for nm in _WIN_ORDER:
        s, w = SEG[nm]
        d = off[nm]
        while w > 0:
            j, r = divmod(s, per)
            n = min(w, per - r, PERM_TILE - d % PERM_TILE)
            tiles[d // PERM_TILE].append((j, r, n, d % PERM_TILE))
            s, w, d = s + n, w - n, d + n
    return tiles


def _win_to_padded(win_g):
    per = IN_DIM // N_DEV
    moves = _win_row_moves()

    def body(w_ref, o_ref, slots, stage, in_sems, out_sems):
        loads = [pltpu.make_async_copy(w_ref.at[j], slots.at[j], in_sems.at[j]) for j in range(N_DEV)]
        for cp in loads:
            cp.start()
        arrived = [False] * N_DEV
        stores = [None, None]
        for t, pieces in enumerate(moves):
            rows = min(PERM_TILE, IN_PAD - PERM_TILE * t)
            b = t % 2
            if stores[b] is not None:
                stores[b].wait()
            filled = 0
            for j, r, n, d in pieces:
                if not arrived[j]:
                    loads[j].wait()
                    arrived[j] = True
                stage[b, pl.ds(d, n), :] = slots[j, pl.ds(r, n), :]
                filled = max(filled, d + n)
            if filled < rows:
                stage[b, pl.ds(filled, rows - filled), :] = jnp.zeros((rows - filled, D_MODEL), stage.dtype)
            stores[b] = pltpu.make_async_copy(stage.at[b, pl.ds(0, rows), :], o_ref.at[pl.ds(PERM_TILE * t, rows), :], out_sems.at[b])
            stores[b].start()
        for cp in stores:
            cp.wait()

    return _pcall(body, name="w_in_rows", in_specs=[_ANY], out_specs=_ANY, out_shape=_sds((IN_PAD, D_MODEL), win_g.dtype),
                  scratch_shapes=[pltpu.VMEM((N_DEV, per, D_MODEL), win_g.dtype), pltpu.VMEM((2, PERM_TILE, D_MODEL), win_g.dtype),
                                  pltpu.SemaphoreType.DMA((N_DEV,)), pltpu.SemaphoreType.DMA((2,))],
                  compiler_params=pltpu.CompilerParams(vmem_limit_bytes=60 * 1024 * 1024))(win_g)


def _padded_to_win(dw):
    off = dict(z=OFF_Z, ga=OFF_GA, gs=OFF_GS, xbc=OFF_XBC, q=OFF_Q, k=OFF_K, v=OFF_V, dt=OFF_DT)
    per = IN_DIM // N_DEV
    blocks = []
    for j in range(N_DEV):
        lo, hi, rows = j * per, (j + 1) * per, []
        for nm in ("q", "k", "v", "z", "xbc", "dt", "ga", "gs"):
            s, w = SEG[nm]
            a, b = max(lo, s), min(hi, s + w)
            if a < b:
                rows.append(dw[off[nm] + a - s:off[nm] + b - s])
        blocks.append(jnp.concatenate(rows, axis=0))
    return jnp.stack(blocks)


def _pad128(v):
    return jnp.pad(v, ((0, 0), (0, 128 - v.shape[1])))


_SMALL = (("loss", 128, 1), ("g_mix", 2048, 2048), ("conv_b", 3072, 3072), ("dt_bias", 128, 32), ("a_log", 128, 32),
          ("d_skip", 128, 32), ("g_ssd", 2048, 2048), ("sinks", 128, 16), ("g_ffn", 2048, 2048), ("g_ple", 2048, 2048),
          ("g_final", 2048, 2048))


def _small_vec(d):
    parts = []
    for nm, pw, w in _SMALL:
        v = d[nm].reshape(1, -1).astype(F32)
        parts.append(jnp.pad(v[:, :min(v.shape[1], pw)], ((0, 0), (0, pw - min(v.shape[1], pw)))))
    return jnp.concatenate(parts, axis=1)


def _small_split(vec):
    out, o = {}, 0
    for nm, pw, w in _SMALL:
        out[nm] = vec[0, o:o + w]
        o += pw
    return out


def kernel(x, p, positions, g_mix, w_in, conv_w, conv_b, dt_bias, a_log, d_skip, g_ssd, sinks, w_attn_br, w_ssd_br, w_o, g_ffn, w_gate, w_up, w_down, g_ple, w_ple_gate, w_ple_proj, g_final, loss_target, m_g_mix, m_w_in, m_conv_w, m_conv_b, m_dt_bias, m_a_log, m_d_skip, m_g_ssd, m_sinks, m_w_attn_br, m_w_ssd_br, m_w_o, m_g_ffn, m_w_gate, m_w_up, m_w_down, m_g_ple, m_w_ple_gate, m_w_ple_proj, m_g_final, v_g_mix, v_w_in, v_conv_w, v_conv_b, v_dt_bias, v_a_log, v_d_skip, v_g_ssd, v_sinks, v_w_attn_br, v_w_ssd_br, v_w_o, v_g_ffn, v_w_gate, v_w_up, v_w_down, v_g_ple, v_w_ple_gate, v_w_ple_proj, v_g_final):
    T = x.shape[1]
    D = D_MODEL
    W = dict(g_mix=g_mix, w_in=w_in, conv_w=conv_w, conv_b=conv_b, dt_bias=dt_bias, a_log=a_log, d_skip=d_skip, g_ssd=g_ssd,
             sinks=sinks, w_attn_br=w_attn_br, w_ssd_br=w_ssd_br, w_o=w_o, g_ffn=g_ffn, w_gate=w_gate, w_up=w_up, w_down=w_down,
             g_ple=g_ple, w_ple_gate=w_ple_gate, w_ple_proj=w_ple_proj, g_final=g_final)
    Mo = dict(g_mix=m_g_mix, w_in=m_w_in, conv_w=m_conv_w, conv_b=m_conv_b, dt_bias=m_dt_bias, a_log=m_a_log, d_skip=m_d_skip,
              g_ssd=m_g_ssd, sinks=m_sinks, w_attn_br=m_w_attn_br, w_ssd_br=m_w_ssd_br, w_o=m_w_o, g_ffn=m_g_ffn, w_gate=m_w_gate,
              w_up=m_w_up, w_down=m_w_down, g_ple=m_g_ple, w_ple_gate=m_w_ple_gate, w_ple_proj=m_w_ple_proj, g_final=m_g_final)
    Vo = dict(g_mix=v_g_mix, w_in=v_w_in, conv_w=v_conv_w, conv_b=v_conv_b, dt_bias=v_dt_bias, a_log=v_a_log, d_skip=v_d_skip,
              g_ssd=v_g_ssd, sinks=v_sinks, w_attn_br=v_w_attn_br, w_ssd_br=v_w_ssd_br, w_o=v_w_o, g_ffn=v_g_ffn, w_gate=v_w_gate,
              w_up=v_w_up, w_down=v_w_down, g_ple=v_g_ple, w_ple_gate=v_w_ple_gate, w_ple_proj=v_w_ple_proj, g_final=v_g_final)
    order = ["g_mix", "w_in", "conv_w", "conv_b", "dt_bias", "a_log", "d_skip", "g_ssd", "sinks", "w_attn_br", "w_ssd_br", "w_o",
             "g_ffn", "w_gate", "w_up", "w_down", "g_ple", "w_ple_gate", "w_ple_proj", "g_final"]
    big = ["w_in", "conv_w", "w_attn_br", "w_ssd_br", "w_o", "w_gate", "w_up", "w_down", "w_ple_gate", "w_ple_proj"]

    x2 = x.reshape(T, D)
    p2 = p.reshape(T, PLE_DIM)
    tgt = loss_target.reshape(T, D)
    posf = positions.reshape(T, 1).astype(F32)
    inv = ROPE_THETA ** (-np.arange(HEAD_DIM // 2, dtype=np.float32) * 2.0 / HEAD_DIM)
    inv128 = jnp.asarray(np.tile(inv, 128 // (HEAD_DIM // 2)).reshape(1, 128).astype(np.float32))
    transposed = ("w_in", "w_gate", "w_up")

    def shard2d(a, n):
        a = a.reshape(a.shape[-2:])
        return a.T if n in transposed else a

    sh = {n: shard2d(W[n], n) for n in big}

    del _PENDING[:]
    me = 4 * lax.axis_index("x") + 2 * lax.axis_index("y") + lax.axis_index("c")
    groups = (("w_in",), ("conv_w", "w_attn_br", "w_ssd_br", "w_o"), ("w_gate", "w_up", "w_down"), ("w_ple_gate", "w_ple_proj"))
    send = {n: sh[n] if n == "conv_w" else sh[n].astype(BF16) for n in big}
    started, prev = [], None
    for gi, grp in enumerate(groups):
        zones = [lax.dynamic_update_index_in_dim(lax.empty((N_DEV,) + send[n].shape, send[n].dtype), send[n], me, 0) for n in grp]
        h = split_start("gather_start_%d" % gi, "gather", ICI_SAME_CORE, [], lands=zones, after=prev)
        prev = h["token"]
        started.append(h)
    gathered, fwd = {}, {}

    def forward_start(gi, after):
        _, lands = split_wait("gather_wait_%d" % gi, started[gi], after)
        fwd[gi] = split_start("forward_start_%d" % gi, "forward", FORWARD_BLOCKS, [], lands=lands)

    def forward_wait(gi, after):
        _, full = split_wait("forward_wait_%d" % gi, fwd[gi], after)
        gathered.update(zip(groups[gi], full))

    u = rms_fwd("norm_mix", x2, g_mix)
    forward_start(0, u)
    forward_wait(0, u)
    forward_start(1, u)
    winp = _win_to_padded(gathered["w_in"])
    dtb = _pad128(dt_bias)
    alog = _pad128(a_log)
    dfull = jnp.repeat(d_skip.reshape(SSM_HEADS), HEAD_DIM).reshape(1, D_INNER)

    projp = mm_nt("in_proj", u, winp, 640)
    attn, qr, kr = attn_fwd(projp, posf, inv128, sinks)
    forward_wait(1, attn)
    convw = jnp.transpose(gathered["conv_w"], (1, 0, 2)).reshape(CONV_WIDTH, CONV_DIM)
    wab = gathered["w_attn_br"]
    wsb = gathered["w_ssd_br"].reshape(D, D)
    wo = gathered["w_o"].reshape(D, D)
    xbc = conv_fwd(projp, convw, conv_b)
    y, states = ssd_fwd(xbc, projp, dtb, alog, dfull)
    yn = gnorm_fwd(y, projp, g_ssd)
    out_a = mm_nn_colblk("attn_br", attn, wab)
    out_s = mm_nn("ssd_br", yn, wsb, 512)
    forward_start(2, out_s)
    merged = merge_fwd(projp, out_a, out_s)
    h1 = mm_nn("o_proj", merged, wo, 512, residual=x2)
    f = rms_fwd("norm_ffn", h1, g_ffn)
    forward_wait(2, f)
    wgt, wut, wd = (gathered[n].reshape(FFN_HIDDEN, D) for n in ("w_gate", "w_up", "w_down"))
    gate, up, act = ffn_up(f, wgt, wut)
    forward_start(3, act)
    h2 = mm_nn_red("ffn_down", act, wd, 512, FFN_HIDDEN, residual=h1)
    r = rms_fwd("norm_ple", h2, g_ple)
    forward_wait(3, r)
    wpg = gathered["w_ple_gate"].reshape(D, D)
    wpp = gathered["w_ple_proj"]
    pg = mm_nn("ple_gate", r, wpg, 512)
    pp = mm_nn_colblk("ple_proj", p2, wpp)
    loss_v, dh3, dpg, dpp, dg_final = head_fwd_bwd(h2, pg, pp, g_final.reshape(1, D), tgt)

    gw = {}
    scat = []

    def scatter_start(names):
        scat.append((names, split_start("scatter_start_%d" % len(scat), "scatter", ALL_PEERS, [gw[n] for n in names])))

    gw["w_ple_proj"] = mm_tn_colblk("dw_ple_proj", p2, dpp, PLE_DIM)
    dr = mm_nt("d_ple_gate", dpg, wpg, 512)
    gw["w_ple_gate"] = mm_tn("dw_ple_gate", r, dpg, 512, 1024).reshape(N_DEV, D // N_DEV, D)
    scatter_start(("w_ple_proj", "w_ple_gate"))
    dh2, dh2b, dg_ple = rms_bwd("norm_ple_bwd", h2, g_ple, dr, dh3)
    dgate, dup = ffn_down_bwd(dh2b, wd, gate, up)
    per = FFN_HIDDEN // N_DEV
    gw["w_down"] = mm_tn("dw_down", act, dh2b, FFN_TILE, 1024).reshape(N_DEV, per, D)
    gw["w_gate"] = mm_tn("dw_gate", dgate, f, FFN_TILE, 1024).reshape(N_DEV, per, D)
    gw["w_up"] = mm_tn("dw_up", dup, f, FFN_TILE, 1024).reshape(N_DEV, per, D)
    scatter_start(("w_down", "w_gate", "w_up"))
    df = ffn_up_bwd(dgate, dup, wgt, wut)
    dh1, dh1b, dg_ffn = rms_bwd("norm_ffn_bwd", h1, g_ffn, df, dh2)
    dmerged = mm_nt("d_o_proj", dh1b, wo, 512)
    gw["w_o"] = mm_tn("dw_o", merged, dh1b, 512, 1024).reshape(N_DEV, D // N_DEV, D)
    dout_a, dout_s, dga, dgs = merge_bwd(projp, out_a, out_s, dmerged)
    gw["w_ssd_br"] = mm_tn("dw_ssd_br", yn, dout_s, 512, 1024).reshape(N_DEV, D // N_DEV, D)
    gw["w_attn_br"] = mm_tn_colblk("dw_attn_br", attn, dout_a, D // N_DEV)
    scatter_start(("w_o", "w_ssd_br", "w_attn_br"))
    dyn = mm_nt("d_ssd_br", dout_s, wsb, 512)
    dattn = attn_br_bwd(dout_a, wab)
    dy, dz, dg_ssd = gnorm_bwd(y, projp, g_ssd, dyn)
    dxs, dbm, dcm, ddt, dal, ddsk, ddtb = ssd_bwd(xbc, projp, dtb, alog, dfull, states, dy)
    dx_x, dwc_x, dbc_x = conv_bwd("conv_bwd_x", projp, dxs, convw, conv_b, 0)
    dx_b, dwc_b, dbc_b = conv_bwd("conv_bwd_b", projp, dbm, convw, conv_b, D_INNER)
    dx_c, dwc_c, dbc_c = conv_bwd("conv_bwd_c", projp, dcm, convw, conv_b, D_INNER + SSM_GROUPS * D_STATE)
    dq, dk, dv, dsk = attn_bwd(qr, kr, projp, dattn, posf, inv128, sinks)
    dproj = jnp.concatenate([dz, dga, dgs, dx_x, dx_b, dx_c, dq, dk.astype(BF16), dv.astype(BF16), ddt], axis=1)
    gw_in = _padded_to_win(mm_tn("dw_in", dproj, u, 640, 1024))
    pair = split_start("pair_start", "pair", FORWARD_BLOCKS, [gw_in])
    dconvw = jnp.concatenate([dwc_x, dwc_b, dwc_c], axis=1)
    gw["conv_w"] = jnp.transpose(dconvw.reshape(CONV_WIDTH, N_DEV, CONV_DIM // N_DEV), (1, 0, 2))
    scatter_start(("conv_w",))
    du_tm = min(512, T)
    tiles = T // du_tm
    first = max(tiles // 2, 1)
    du = mm_nn_red("d_in_proj_a", dproj, winp, 512, IN_PAD, rows=(0, first), tm=du_tm)
    (gw_in,), (sibling_part,) = split_wait("pair_wait", pair, du)
    pair_slots = jnp.stack([jnp.bitwise_xor(me, k) for k in FORWARD_BLOCKS]).astype(jnp.int32)
    core = split_start("core_start", "scatter_core", ICI_SAME_CORE, [pair_sum("pair_sum_w_in", gw_in, pair_slots, sibling_part)])
    if first < tiles:
        du = mm_nn_red("d_in_proj_b", dproj, winp, 512, IN_PAD, rows=(first, tiles - first), prev=du, tm=du_tm)
    gx, _, dg_mix = rms_bwd("norm_mix_bwd", x2, g_mix, du, dh1)

    small_g = dict(loss=loss_v[:, :1], g_mix=dg_mix, conv_b=jnp.concatenate([dbc_x, dbc_b, dbc_c], axis=1), dt_bias=ddtb,
                   a_log=dal, d_skip=ddsk, g_ssd=dg_ssd, sinks=dsk, g_ffn=dg_ffn, g_ple=dg_ple, g_final=dg_final)
    vec = _small_vec(small_g)
    small = split_start("small_start", "gather", ALL_PEERS, [],
                        lands=[lax.dynamic_update_index_in_dim(lax.empty((N_DEV,) + vec.shape, F32), vec, me, 0)])

    res = {}
    after = [gx]
    for si, (names, h) in enumerate(scat):
        srcs, lands = split_wait("scatter_wait_%d" % si, h, after)
        for n, mine, arrived in zip(names, srcs, lands):
            res[n] = adamw("adamw_" + n, arrived, sh[n], shard2d(Mo[n], n), shard2d(Vo[n], n), own=mine, own_slot=me)
        after = [res[n][0] for n in names]
    zero = jnp.zeros((1, 1), F32)
    _, (vec_parts,) = split_wait("small_wait", small, [res[n][0] for n in res])
    sres = adamw("adamw_small", vec_parts, _small_vec({**W, "loss": zero}), _small_vec({**Mo, "loss": zero}),
                 _small_vec({**Vo, "loss": zero}))
    ssplit = [_small_split(a) for a in sres]

    (pair_sums,), (arrived,) = split_wait("core_wait", core, [sres[0]])
    res["w_in"] = adamw("adamw_w_in", arrived, sh["w_in"], shard2d(Mo["w_in"], "w_in"), shard2d(Vo["w_in"], "w_in"),
                        own=pair_sums, own_slot=0)
    loss = ssplit[0]["loss"].reshape(())
    for n in order:
        if n not in res:
            res[n] = tuple(s[n].reshape(W[n].shape) for s in ssplit)
        else:
            res[n] = tuple((a.T if n in transposed else a).reshape(W[n].shape) for a in res[n])
    outs = [loss, gx.reshape(x.shape)]
    for k in range(4):
        outs += [res[n][k] for n in order]
    return tuple(outs)
```

```python
import functools

import numpy as np
import jax
import jax.numpy as jnp
from jax import lax
from jax.experimental import pallas as pl
from jax.experimental.pallas import tpu as pltpu

F32 = jnp.float32
BF16 = jnp.bfloat16

N_DEV = 8
D_MODEL = 2048
HEAD_DIM = 64
ATTN_HEADS = 16
KV_HEADS = 4
Q_DIM = 1024
KV_DIM = 256
BLK = 128
D_INNER = 2048
SSM_HEADS = 32
SSM_GROUPS = 4
HEADS_PER_GROUP = 8
D_STATE = 128
CONV_WIDTH = 4
CONV_DIM = 3072
FFN_HIDDEN = 5632
PLE_DIM = 256
IN_DIM = 10784
NORM_EPS = 1e-6
SSM_NORM_EPS = 1e-5
ROPE_THETA = 10000.0

OFF_Z, OFF_GA, OFF_GS, OFF_XBC, OFF_Q, OFF_K, OFF_V, OFF_DT = 0, 2048, 4096, 6144, 9216, 10240, 10496, 10752
IN_PAD = 10880
DT_PAD = 128
SEG = dict(q=(0, 1024), k=(1024, 256), v=(1280, 256), z=(1536, 2048), xbc=(3584, 3072), dt=(6656, 32),
           ga=(6688, 2048), gs=(8736, 2048))

ADAM_LR, ADAM_B1, ADAM_B2, ADAM_EPS, ADAM_WD, ADAM_STEP = 0.001, 0.9, 0.999, 1e-08, 0.01, 10

VMEM_LIMIT = 56 * 1024 * 1024

NN = (((1,), (0,)), ((), ()))
NT = (((1,), (1,)), ((), ()))
TN = (((0,), (0,)), ((), ()))


_PENDING = []


def _raw_call(body, **kw):
    return pl.pallas_call(body, **kw)


def _pcall(body, **kw):
    if "in_specs" not in kw:
        return _raw_call(body, **kw)
    deps = list(_PENDING)
    del _PENDING[:]
    if not deps:
        return _raw_call(body, **kw)
    n_in = len(kw["in_specs"])

    def tied(*refs):
        return body(*refs[:n_in], *refs[n_in + len(deps):])

    kw["in_specs"] = list(kw["in_specs"]) + [pl.BlockSpec(memory_space=pl.ANY)] * len(deps)
    call = _raw_call(tied, **kw)
    return lambda *ops: call(*ops, *deps)


def _cparams(sem=None):
    if sem is None:
        return pltpu.CompilerParams(vmem_limit_bytes=VMEM_LIMIT)
    return pltpu.CompilerParams(vmem_limit_bytes=VMEM_LIMIT, dimension_semantics=sem)


def _dot(a, b, dn):
    return lax.dot_general(a.astype(BF16), b.astype(BF16), dn, preferred_element_type=F32)


def _sigmoid(x):
    return 1.0 / (1.0 + jnp.exp(-x))


def _silu(x):
    return x * _sigmoid(x)


def _dsilu(x):
    s = _sigmoid(x)
    return s * (1.0 + x * (1.0 - s))


def _matmul(name, pairs, pair_specs, dn, grid, out_shapes, out_specs, nred=1, extra=(), extra_specs=(),
            epilogue=None, acc_shape=None, alias=None):
    n_in = 2 * len(pairs) + len(extra)
    n_out = len(out_shapes)

    def body(*refs):
        ins = refs[:2 * len(pairs)]
        ex = [r for r, sp in zip(refs[2 * len(pairs):n_in], extra_specs) if sp.memory_space != pl.ANY]
        outs = refs[n_in:n_in + n_out]

        def prod():
            s = None
            for p in range(len(pairs)):
                d = _dot(ins[2 * p][...], ins[2 * p + 1][...], dn)
                s = d if s is None else s + d
            return s

        def finish(val):
            if epilogue is None:
                outs[0][...] = val.astype(outs[0].dtype)
            else:
                res = epilogue(val, *[e[...] for e in ex])
                for o, r in zip(outs, res):
                    o[...] = r.astype(o.dtype)

        if nred == 1:
            finish(prod())
        else:
            acc = refs[n_in + n_out]
            k = pl.program_id(len(grid) - 1)

            @pl.when(k == 0)
            def _():
                acc[...] = jnp.zeros_like(acc)

            acc[...] += prod()

            @pl.when(k == nred - 1)
            def _():
                finish(acc[...])

    operands = []
    specs = []
    for (a, b), (sa, sb) in zip(pairs, pair_specs):
        operands += [a, b]
        specs += [sa, sb]
    operands += list(extra)
    specs += list(extra_specs)
    scratch = [pltpu.VMEM(acc_shape, F32)] if nred > 1 else []
    sem = ("arbitrary",) * len(grid)
    res = _pcall(body, name=name, grid=grid, in_specs=specs, out_specs=list(out_specs), input_output_aliases=dict(alias or {}),
                 out_shape=list(out_shapes), scratch_shapes=scratch, compiler_params=_cparams(sem))(*operands)
    return res


def _sds(shape, dtype):
    return jax.ShapeDtypeStruct(shape, dtype)


def _row_tile(T):
    return min(1024, T)


def mm_nn(name, a, b, tn, out_dtype=F32, residual=None):
    M, K = a.shape
    N = b.shape[1]
    tm = _row_tile(M)
    grid = (M // tm, N // tn)
    extra, especs, epi = (), (), None
    if residual is not None:
        extra = (residual,)
        especs = (pl.BlockSpec((tm, tn), lambda i, n: (i, n)),)
        epi = lambda v, r: (v + r,)
    return _matmul(name, [(a, b)], [(pl.BlockSpec((tm, K), lambda i, n: (i, 0)), pl.BlockSpec((K, tn), lambda i, n: (0, n)))],
                   NN, grid, [_sds((M, N), out_dtype)], [pl.BlockSpec((tm, tn), lambda i, n: (i, n))],
                   extra=extra, extra_specs=especs, epilogue=epi)[0]


def mm_nn_colblk(name, a, b, out_dtype=F32):
    M, K = a.shape
    J, _, nb = b.shape
    tm = _row_tile(M)
    grid = (M // tm, J)
    return _matmul(name, [(a, b)], [(pl.BlockSpec((tm, K), lambda i, j: (i, 0)), pl.BlockSpec((None, K, nb), lambda i, j: (j, 0, 0)))],
                   NN, grid, [_sds((M, J * nb), out_dtype)], [pl.BlockSpec((tm, nb), lambda i, j: (i, j))])[0]


def mm_nt(name, a, w, tr, out_dtype=F32):
    M, C = a.shape
    R = w.shape[0]
    tm = _row_tile(M)
    grid = (M // tm, R // tr)
    return _matmul(name, [(a, w)], [(pl.BlockSpec((tm, C), lambda i, r: (i, 0)), pl.BlockSpec((tr, C), lambda i, r: (r, 0)))],
                   NT, grid, [_sds((M, R), out_dtype)], [pl.BlockSpec((tm, tr), lambda i, r: (i, r))])[0]


def mm_nt_red(name, a, w, tr, tk, out_dtype=F32):
    M, C = a.shape
    R = w.shape[0]
    tm = _row_tile(M)
    nk = C // tk
    grid = (M // tm, R // tr, nk)
    return _matmul(name, [(a, w)], [(pl.BlockSpec((tm, tk), lambda i, r, k: (i, k)), pl.BlockSpec((tr, tk), lambda i, r, k: (r, k)))],
                   NT, grid, [_sds((M, R), out_dtype)], [pl.BlockSpec((tm, tr), lambda i, r, k: (i, r))],
                   nred=nk, acc_shape=(tm, tr))[0]


def mm_nn_red(name, a, b, tn, tk, out_dtype=F32, residual=None, rows=None, prev=None, tm=None):
    M, K = a.shape
    N = b.shape[1]
    tm = min(tm or _row_tile(M), M)
    nk = K // tk
    i0, ni = (0, M // tm) if rows is None else rows
    grid = (ni, N // tn, nk)
    ospec = pl.BlockSpec((tm, tn), lambda i, n, k: (i + i0, n))
    extra, especs, epi = [], [], None
    if residual is not None:
        extra, especs, epi = [residual], [ospec], (lambda v, r, *_: (v + r,))
    alias = {}
    if prev is not None:
        alias = {2 + len(extra): 0}
        extra, especs = extra + [prev], especs + [_ANY]
        epi = epi or (lambda v, *_: (v,))
    return _matmul(name, [(a, b)], [(pl.BlockSpec((tm, tk), lambda i, n, k: (i + i0, k)), pl.BlockSpec((tk, tn), lambda i, n, k: (k, n)))],
                   NN, grid, [_sds((M, N), out_dtype)], [ospec], nred=nk, acc_shape=(tm, tn),
                   extra=extra, extra_specs=especs, epilogue=epi, alias=alias)[0]


def mm_tn(name, x, dy, tr, tc, out_dtype=BF16):
    M, R = x.shape
    C = dy.shape[1]
    grid = (R // tr, C // tc)
    return _matmul(name, [(x, dy)], [(pl.BlockSpec((M, tr), lambda r, c: (0, r)), pl.BlockSpec((M, tc), lambda r, c: (0, c)))],
                   TN, grid, [_sds((R, C), out_dtype)], [pl.BlockSpec((tr, tc), lambda r, c: (r, c))])[0]


def mm_tn_colblk(name, x, dy, nb, out_dtype=BF16):
    M, R = x.shape
    J = dy.shape[1] // nb
    grid = (J,)
    return _matmul(name, [(x, dy)], [(pl.BlockSpec((M, R), lambda j: (0, 0)), pl.BlockSpec((M, nb), lambda j: (0, j)))],
                   TN, grid, [_sds((J, R, nb), out_dtype)], [pl.BlockSpec((None, R, nb), lambda j: (j, 0, 0))])[0]


def _rows(T):
    return min(256, T)


def rms_fwd(name, x, g, eps=NORM_EPS):
    T, D = x.shape
    tm = _rows(T)

    def body(x_ref, g_ref, o_ref):
        xv = x_ref[...]
        r = lax.rsqrt(jnp.mean(xv * xv, axis=-1, keepdims=True) + eps)
        o_ref[...] = (xv * r * g_ref[...]).astype(BF16)

    return _pcall(body, name=name, grid=(T // tm,),
                  in_specs=[pl.BlockSpec((tm, D), lambda i: (i, 0)), pl.BlockSpec((1, D), lambda i: (0, 0))],
                  out_specs=pl.BlockSpec((tm, D), lambda i: (i, 0)), out_shape=_sds((T, D), BF16),
                  compiler_params=_cparams(("arbitrary",)))(x, g)


def rms_bwd(name, x, g, dy, dres, eps=NORM_EPS):
    T, D = x.shape
    tm = _rows(T)

    def body(x_ref, g_ref, dy_ref, dr_ref, dx_ref, dxb_ref, dg_ref):
        i = pl.program_id(0)
        xv = x_ref[...]
        r = lax.rsqrt(jnp.mean(xv * xv, axis=-1, keepdims=True) + eps)
        xh = xv * r
        dyv = dy_ref[...]
        gd = dyv * g_ref[...]
        dx = r * (gd - xh * jnp.mean(gd * xh, axis=-1, keepdims=True)) + dr_ref[...]
        dx_ref[...] = dx
        dxb_ref[...] = dx.astype(BF16)

        @pl.when(i == 0)
        def _():
            dg_ref[...] = jnp.zeros_like(dg_ref)

        dg_ref[...] += jnp.sum(dyv * xh, axis=0, keepdims=True)

    row = pl.BlockSpec((tm, D), lambda i: (i, 0))
    vec = pl.BlockSpec((1, D), lambda i: (0, 0))
    return _pcall(body, name=name, grid=(T // tm,), in_specs=[row, vec, row, row], out_specs=[row, row, vec],
                  out_shape=[_sds((T, D), F32), _sds((T, D), BF16), _sds((1, D), F32)],
                  compiler_params=_cparams(("arbitrary",)))(x, g, dy, dres)


def gnorm_fwd(y, projp, g):
    T, D = y.shape
    tm = _rows(T)

    def body(y_ref, z_ref, g_ref, o_ref):
        yz = y_ref[...] * _silu(z_ref[...])
        r = lax.rsqrt(jnp.mean(yz * yz, axis=-1, keepdims=True) + SSM_NORM_EPS)
        o_ref[...] = (yz * r * g_ref[...]).astype(BF16)

    row = pl.BlockSpec((tm, D), lambda i: (i, 0))
    return _pcall(body, name="gnorm_fwd", grid=(T // tm,),
                  in_specs=[row, pl.BlockSpec((tm, D), lambda i: (i, OFF_Z // D)), pl.BlockSpec((1, D), lambda i: (0, 0))],
                  out_specs=row, out_shape=_sds((T, D), BF16), compiler_params=_cparams(("arbitrary",)))(y, projp, g)


def gnorm_bwd(y, projp, g, dyn):
    T, D = y.shape
    tm = _rows(T)

    def body(y_ref, z_ref, g_ref, dyn_ref, dy_ref, dz_ref, dg_ref):
        i = pl.program_id(0)
        yv, zv = y_ref[...], z_ref[...]
        sz = _silu(zv)
        yz = yv * sz
        r = lax.rsqrt(jnp.mean(yz * yz, axis=-1, keepdims=True) + SSM_NORM_EPS)
        xh = yz * r
        dv = dyn_ref[...]
        gd = dv * g_ref[...]
        dyz = r * (gd - xh * jnp.mean(gd * xh, axis=-1, keepdims=True))
        dy_ref[...] = dyz * sz
        dz_ref[...] = (dyz * yv * _dsilu(zv)).astype(BF16)

        @pl.when(i == 0)
        def _():
            dg_ref[...] = jnp.zeros_like(dg_ref)

        dg_ref[...] += jnp.sum(dv * xh, axis=0, keepdims=True)

    row = pl.BlockSpec((tm, D), lambda i: (i, 0))
    vec = pl.BlockSpec((1, D), lambda i: (0, 0))
    return _pcall(body, name="gnorm_bwd", grid=(T // tm,),
                  in_specs=[row, pl.BlockSpec((tm, D), lambda i: (i, OFF_Z // D)), vec, row], out_specs=[row, row, vec],
                  out_shape=[_sds((T, D), F32), _sds((T, D), BF16), _sds((1, D), F32)],
                  compiler_params=_cparams(("arbitrary",)))(y, projp, g, dyn)


def merge_fwd(projp, out_a, out_s):
    T, D = out_a.shape
    tm = _rows(T)

    def body(ga_ref, gs_ref, a_ref, s_ref, o_ref):
        o_ref[...] = (_sigmoid(ga_ref[...]) * a_ref[...] + _sigmoid(gs_ref[...]) * s_ref[...]).astype(BF16)

    row = pl.BlockSpec((tm, D), lambda i: (i, 0))
    return _pcall(body, name="merge_fwd", grid=(T // tm,),
                  in_specs=[pl.BlockSpec((tm, D), lambda i: (i, OFF_GA // D)), pl.BlockSpec((tm, D), lambda i: (i, OFF_GS // D)), row, row],
                  out_specs=row, out_shape=_sds((T, D), BF16), compiler_params=_cparams(("arbitrary",)))(projp, projp, out_a, out_s)


def merge_bwd(projp, out_a, out_s, dmerged):
    T, D = out_a.shape
    tm = _rows(T)

    def body(ga_ref, gs_ref, a_ref, s_ref, dm_ref, da_ref, ds_ref, dga_ref, dgs_ref):
        dm = dm_ref[...]
        sa, ss = _sigmoid(ga_ref[...]), _sigmoid(gs_ref[...])
        da_ref[...] = (dm * sa).astype(BF16)
        ds_ref[...] = (dm * ss).astype(BF16)
        dga_ref[...] = (dm * a_ref[...] * sa * (1.0 - sa)).astype(BF16)
        dgs_ref[...] = (dm * s_ref[...] * ss * (1.0 - ss)).astype(BF16)

    row = pl.BlockSpec((tm, D), lambda i: (i, 0))
    return _pcall(body, name="merge_bwd", grid=(T // tm,),
                  in_specs=[pl.BlockSpec((tm, D), lambda i: (i, OFF_GA // D)), pl.BlockSpec((tm, D), lambda i: (i, OFF_GS // D)), row, row, row],
                  out_specs=[row] * 4, out_shape=[_sds((T, D), BF16)] * 4,
                  compiler_params=_cparams(("arbitrary",)))(projp, projp, out_a, out_s, dmerged)


def head_fwd_bwd(h2, pg, pp, g_final, target):
    T, D = h2.shape
    tm = _rows(T)

    def body(h_ref, pg_ref, pp_ref, g_ref, t_ref, loss_ref, dh_ref, dpg_ref, dpp_ref, dg_ref):
        i = pl.program_id(0)
        s = _sigmoid(pg_ref[...])
        ppv = pp_ref[...]
        h3 = h_ref[...] + s * ppv
        r = lax.rsqrt(jnp.mean(h3 * h3, axis=-1, keepdims=True) + NORM_EPS)
        xh = h3 * r
        gv = g_ref[...]
        e = xh * gv - t_ref[...]
        dyo = e * (1.0 / D)
        gd = dyo * gv
        dh = r * (gd - xh * jnp.mean(gd * xh, axis=-1, keepdims=True))
        dh_ref[...] = dh
        dpg_ref[...] = (dh * ppv * s * (1.0 - s)).astype(BF16)
        dpp_ref[...] = (dh * s).astype(BF16)

        @pl.when(i == 0)
        def _():
            dg_ref[...] = jnp.zeros_like(dg_ref)
            loss_ref[...] = jnp.zeros_like(loss_ref)

        dg_ref[...] += jnp.sum(dyo * xh, axis=0, keepdims=True)
        part = 0.5 * jnp.sum(jnp.mean(e * e, axis=-1, keepdims=True), axis=0, keepdims=True)
        loss_ref[...] += jnp.broadcast_to(part, loss_ref.shape)

    row = pl.BlockSpec((tm, D), lambda i: (i, 0))
    vec = pl.BlockSpec((1, D), lambda i: (0, 0))
    return _pcall(body, name="head_fwd_bwd", grid=(T // tm,), in_specs=[row, row, row, vec, row],
                  out_specs=[pl.BlockSpec((1, 128), lambda i: (0, 0)), row, row, row, vec],
                  out_shape=[_sds((1, 128), F32), _sds((T, D), F32), _sds((T, D), BF16), _sds((T, D), BF16), _sds((1, D), F32)],
                  compiler_params=_cparams(("arbitrary",)))(h2, pg, pp, g_final, target)


FFN_TILE = 512


def ffn_up(f, wgt, wut):
    T, D = f.shape
    H = wgt.shape[0]
    tm = _row_tile(T)

    def body(f_ref, wg_ref, wu_ref, g_ref, u_ref, a_ref):
        fv = f_ref[...]
        g = _dot(fv, wg_ref[...], NT)
        u = _dot(fv, wu_ref[...], NT)
        g_ref[...] = g
        u_ref[...] = u
        a_ref[...] = (_silu(g) * u).astype(BF16)

    wspec = pl.BlockSpec((FFN_TILE, D), lambda i, j: (j, 0))
    ospec = pl.BlockSpec((tm, FFN_TILE), lambda i, j: (i, j))
    return _pcall(body, name="ffn_up", grid=(T // tm, H // FFN_TILE), in_specs=[pl.BlockSpec((tm, D), lambda i, j: (i, 0)), wspec, wspec],
                  out_specs=[ospec] * 3, out_shape=[_sds((T, H), F32), _sds((T, H), F32), _sds((T, H), BF16)],
                  compiler_params=_cparams(("arbitrary", "arbitrary")))(f, wgt, wut)


def ffn_down_bwd(dh2b, wd, gate, up):
    T, D = dh2b.shape
    H = wd.shape[0]
    tm = _row_tile(T)
    ospec = pl.BlockSpec((tm, FFN_TILE), lambda i, j: (i, j))

    def epi(da, g, u):
        return (da * u * _dsilu(g), da * _silu(g))

    return _matmul("ffn_down_bwd", [(dh2b, wd)],
                   [(pl.BlockSpec((tm, D), lambda i, j: (i, 0)), pl.BlockSpec((FFN_TILE, D), lambda i, j: (j, 0)))],
                   NT, (T // tm, H // FFN_TILE), [_sds((T, H), BF16)] * 2, [ospec, ospec],
                   extra=(gate, up), extra_specs=(ospec, ospec), epilogue=epi)


def ffn_up_bwd(dgate, dup, wgt, wut):
    T, H = dgate.shape
    D = wgt.shape[1]
    tm = min(512, T)
    tn = 512
    aspec = pl.BlockSpec((tm, H), lambda i, n: (i, 0))
    wspec = pl.BlockSpec((H, tn), lambda i, n: (0, n))
    return _matmul("ffn_up_bwd", [(dgate, wgt), (dup, wut)], [(aspec, wspec), (aspec, wspec)], NN, (T // tm, D // tn),
                   [_sds((T, D), F32)], [pl.BlockSpec((tm, tn), lambda i, n: (i, n))])[0]


def attn_br_bwd(dout_a, wab):
    T, D = dout_a.shape
    J, R, nb = wab.shape
    tm = _row_tile(T)
    return _matmul("attn_br_bwd", [(dout_a, wab)],
                   [(pl.BlockSpec((tm, nb), lambda i, j: (i, j)), pl.BlockSpec((None, R, nb), lambda i, j: (j, 0, 0)))],
                   NT, (T // tm, J), [_sds((T, R), BF16)], [pl.BlockSpec((tm, R), lambda i, j: (i, 0))], nred=J, acc_shape=(tm, R))[0]


def _adam_math(w, g, m, v):
    m2 = ADAM_B1 * m + (1.0 - ADAM_B1) * g
    v2 = ADAM_B2 * v + (1.0 - ADAM_B2) * (g * g)
    m_hat = m2 / (1.0 - ADAM_B1 ** ADAM_STEP)
    v_hat = v2 / (1.0 - ADAM_B2 ** ADAM_STEP)
    delta = -ADAM_LR * (m_hat / (jnp.sqrt(v_hat) + ADAM_EPS) + ADAM_WD * w)
    return delta, m2, v2


def _sum_partials(own, parts):
    g = None if own is None else own.astype(F32)
    if parts is not None:
        for s in range(parts.shape[0]):
            t = parts[s].astype(F32)
            g = t if g is None else g + t
    return g


def adamw(name, parts, w, m, v, own=None, own_slot=None):
    R, C = w.shape
    tr, tc = R, C
    for cand in (256, 176, 128, 64, 32, 16, 8):
        if R % cand == 0 and R > cand:
            tr = cand
            break
    if tr == R and R > 256:
        tc = 256
    given = [a for a in (parts, own) if a is not None]
    pre = own_slot is not None

    def body(*refs):
        refs = refs[1:] if pre else refs
        p_ref = refs[0] if parts is not None else None
        o_ref = refs[len(given) - 1] if own is not None else None
        w_ref, m_ref, v_ref, g_ref, d_ref, m2_ref, v2_ref = refs[-7:]
        g = _sum_partials(None if o_ref is None else o_ref[...], p_ref)
        d, m2, v2 = _adam_math(w_ref[...], g, m_ref[...], v_ref[...])
        g_ref[...] = g
        d_ref[...] = d
        m2_ref[...] = m2
        v2_ref[...] = v2

    blk = pl.BlockSpec((tr, tc), lambda i, j, *s: (i, j))
    specs = [] if parts is None else [pl.BlockSpec((parts.shape[0], tr, tc), lambda i, j, *s: (0, i, j))]
    if own is not None:
        specs.append(pl.BlockSpec((None, tr, tc), lambda i, j, s: (s[0], i, j)) if pre else blk)
    specs += [blk] * 3
    grid = (R // tr, C // tc)
    out_shape = [_sds((R, C), F32)] * 4
    params = _cparams(("arbitrary", "arbitrary"))
    if not pre:
        return _pcall(body, name=name, grid=grid, in_specs=specs, out_specs=[blk] * 4, out_shape=out_shape,
                      compiler_params=params)(*given, w, m, v)
    spec = pltpu.PrefetchScalarGridSpec(num_scalar_prefetch=1, grid=grid, in_specs=specs, out_specs=[blk] * 4)
    return _pcall(body, name=name, grid_spec=spec, out_shape=out_shape,
                  compiler_params=params)(jnp.asarray(own_slot, jnp.int32).reshape(1), *given, w, m, v)


_HBM = pl.BlockSpec(memory_space=pltpu.HBM)
_SEM = pl.BlockSpec(memory_space=pltpu.SEMAPHORE)
_ANY = pl.BlockSpec(memory_space=pl.ANY)
_SPLIT_PARAMS = dict(compiler_params=pltpu.CompilerParams(has_side_effects=pltpu.SideEffectType.DATAFLOW_SIDE_EFFECTING))
ICI_SAME_CORE = (2, 4, 6)
ALL_PEERS = (1, 2, 3, 4, 5, 6, 7)
LAND_SLOTS = {"gather": N_DEV, "scatter": N_DEV - 1, "pair": 4, "scatter_core": 3}


def _mesh_pos():
    x, y, c = lax.axis_index("x"), lax.axis_index("y"), lax.axis_index("c")
    return x, y, c, 4 * x + 2 * y + c


def _peer_of(k, x, y, c):
    px = 1 - x if k & 4 else x
    py = 1 - y if k & 2 else y
    pc = 1 - c if k & 1 else c
    return (px, py, pc), 4 * px + 2 * py + pc


def _split_copies(mode, ks, srcs, lands, send_sems, recv_sems):
    x, y, c, me = _mesh_pos()
    pairs = []
    for a in range(len(lands)):
        for j, k in enumerate(ks):
            dev, peer = _peer_of(k, x, y, c)
            i = a * len(ks) + j
            if mode == "gather":
                s_out, d_out, d_in = lands[a].at[me], lands[a].at[me], lands[a].at[peer]
            elif mode == "scatter":
                s_out, d_out, d_in = srcs[a].at[peer], lands[a].at[k - 1], lands[a].at[k - 1]
            elif mode == "pair":
                dev, _ = _peer_of(1, x, y, c)
                _, theirs = _peer_of(k | 1, x, y, c)
                s_out, d_out, d_in = srcs[a].at[theirs], lands[a].at[j], lands[a].at[j]
            elif mode == "scatter_core":
                s_out, d_out, d_in = srcs[a].at[j + 1], lands[a].at[j], lands[a].at[j]
            else:
                dev, _ = _peer_of(1, x, y, c)
                _, theirs = _peer_of(k | 1, x, y, c)
                s_out, d_out, d_in = lands[a].at[peer], lands[a].at[peer], lands[a].at[theirs]
            both = [pltpu.make_async_remote_copy(src_ref=s_out, dst_ref=d, send_sem=send_sems.at[i], recv_sem=recv_sems.at[i],
                                                 device_id=dev, device_id_type=pl.DeviceIdType.MESH) for d in (d_out, d_in)]
            pairs.append(tuple(both))
    return pairs


def split_start(name, mode, ks, srcs, lands=None, after=None):
    n, nk = len(srcs) if lands is None else len(lands), len(ks)
    srcs = [pltpu.with_memory_space_constraint(s, pltpu.HBM) for s in srcs]
    if lands is None:
        shapes = [((N_DEV,) + s.shape) if mode == "gather" else ((LAND_SLOTS[mode],) + s.shape[1:]) for s in srcs]
        lands = [lax.empty(shp, s.dtype) for shp, s in zip(shapes, srcs)]
    lands = [pltpu.with_memory_space_constraint(l, pltpu.HBM) for l in lands]
    both = srcs + lands
    extra = [] if after is None else [after]

    def body(*refs):
        src_refs, land_refs = refs[:len(srcs)], refs[len(srcs):len(both)]
        send_sems, recv_sems = refs[len(both) + len(extra)], refs[len(both) + len(extra) + 1]
        token = refs[-1]
        for out, _ in _split_copies(mode, ks, src_refs, land_refs, send_sems, recv_sems):
            out.start()
        token[...] = jnp.zeros_like(token)

    out_shape = (pltpu.SemaphoreType.DMA((n * nk,)), pltpu.SemaphoreType.DMA((n * nk,)),
                 *[pltpu.HBM(a.shape, a.dtype) for a in both], _sds((8, 128), F32))
    res = _raw_call(body, name=name, out_shape=out_shape, in_specs=[_HBM] * len(both) + [_ANY] * len(extra),
                    out_specs=(_SEM, _SEM, *[_HBM] * len(both), pl.BlockSpec(memory_space=pltpu.VMEM)),
                    input_output_aliases={i: 2 + i for i in range(len(both))}, **_SPLIT_PARAMS)(*both, *extra)
    _PENDING.append(res[-1])
    return dict(mode=mode, ks=ks, sems=(res[0], res[1]), srcs=list(res[2:2 + len(srcs)]),
                lands=list(res[2 + len(srcs):2 + len(both)]), token=res[-1])


def split_wait(name, h, after):
    ns = len(h["srcs"])
    both = h["srcs"] + h["lands"]
    after = list(after) if isinstance(after, (list, tuple)) else [after]

    def body(*refs):
        src_refs, land_refs = refs[:ns], refs[ns:len(both)]
        send_sems, recv_sems = refs[len(both)], refs[len(both) + 1]
        for out, arriving in _split_copies(h["mode"], h["ks"], src_refs, land_refs, send_sems, recv_sems):
            out.wait_send()
            arriving.wait_recv()

    res = _raw_call(body, name=name, out_shape=tuple(pltpu.HBM(a.shape, a.dtype) for a in both),
                    in_specs=[_HBM] * len(both) + [_SEM, _SEM] + [_ANY] * len(after), out_specs=tuple([_HBM] * len(both)),
                    input_output_aliases={i: i for i in range(len(both))}, **_SPLIT_PARAMS)(*both, *h["sems"], *after)
    return list(res[:ns]), list(res[ns:])


FORWARD_BLOCKS = (0, 2, 4, 6)


def pair_sum(name, mine, slots, theirs):
    P, R, C = theirs.shape
    tc = 512

    def body(s_ref, a_ref, b_ref, o_ref):
        o_ref[...] = (a_ref[...].astype(F32) + b_ref[...].astype(F32)).astype(o_ref.dtype)

    blk = pl.BlockSpec((None, R, tc), lambda p, i, s: (p, 0, i))
    spec = pltpu.PrefetchScalarGridSpec(num_scalar_prefetch=1, grid=(P, C // tc),
                                        in_specs=[pl.BlockSpec((None, R, tc), lambda p, i, s: (s[p], 0, i)), blk], out_specs=blk)
    return _pcall(body, name=name, grid_spec=spec, out_shape=_sds((P, R, C), theirs.dtype),
                  compiler_params=_cparams(("arbitrary", "arbitrary")))(slots, mine, theirs)


def _rope_parts(pos_ref, inv_ref):
    ang = pos_ref[...] * inv_ref[...]
    return jnp.cos(ang), jnp.sin(ang)


def _rot_half(t):
    lane = lax.broadcasted_iota(jnp.int32, t.shape, 1)
    return jnp.where((lane % HEAD_DIM) < HEAD_DIM // 2, -pltpu.roll(t, 128 - HEAD_DIM // 2, 1), pltpu.roll(t, HEAD_DIM // 2, 1))


def _attn_mask(n):
    row = lax.broadcasted_iota(jnp.int32, (BLK, 2 * BLK), 0)
    col = lax.broadcasted_iota(jnp.int32, (BLK, 2 * BLK), 1)
    dist = row + BLK - col
    return (dist >= 0) & (dist < BLK) & ((n * BLK - BLK + col) >= 0)


def _attn_specs(T):
    prev = lambda n: jnp.maximum(n - 1, 0)
    kc = pl.BlockSpec((BLK, KV_DIM), lambda n: (n, OFF_K // KV_DIM))
    kp = pl.BlockSpec((BLK, KV_DIM), lambda n: (prev(n), OFF_K // KV_DIM))
    vc = pl.BlockSpec((BLK, KV_DIM), lambda n: (n, OFF_V // KV_DIM))
    vp = pl.BlockSpec((BLK, KV_DIM), lambda n: (prev(n), OFF_V // KV_DIM))
    pc = pl.BlockSpec((BLK, 1), lambda n: (n, 0))
    pp = pl.BlockSpec((BLK, 1), lambda n: (prev(n), 0))
    inv = pl.BlockSpec((1, 128), lambda n: (0, 0))
    sink = pl.BlockSpec(memory_space=pltpu.SMEM)
    return kc, kp, vc, vp, pc, pp, inv, sink


def _softmax_sink(sc, valid, sink):
    sc = jnp.where(valid, sc * (HEAD_DIM ** -0.5), -1e30)
    m = jnp.maximum(jnp.max(sc, axis=1, keepdims=True), sink)
    e = jnp.exp(sc - m)
    es = jnp.exp(sink - m)
    den = jnp.sum(e, axis=1, keepdims=True) + es
    return e / den, es / den


def attn_fwd(projp, posf, inv128, sinks):
    T = projp.shape[0]
    kc, kp, vc, vp, pc, pp, inv, sink = _attn_specs(T)

    def body(q_ref, kc_ref, kp_ref, vc_ref, vp_ref, pc_ref, pp_ref, inv_ref, sink_ref, o_ref, qr_ref, kr_ref):
        n = pl.program_id(0)
        cos_c, sin_c = _rope_parts(pc_ref, inv_ref)
        cos_p, sin_p = _rope_parts(pp_ref, inv_ref)
        valid = _attn_mask(n)
        k_c, k_p = [], []
        for s in range(KV_DIM // 128):
            t = kc_ref[:, 128 * s:128 * (s + 1)]
            k_c.append((t * cos_c + _rot_half(t) * sin_c).astype(BF16))
            kr_ref[:, 128 * s:128 * (s + 1)] = k_c[s]
            t = kp_ref[:, 128 * s:128 * (s + 1)]
            k_p.append((t * cos_p + _rot_half(t) * sin_p).astype(BF16))
        kcat, vcat = [], []
        for hk in range(KV_HEADS):
            lo = HEAD_DIM * (hk % 2)
            kcat.append(jnp.concatenate([k_p[hk // 2][:, lo:lo + HEAD_DIM], k_c[hk // 2][:, lo:lo + HEAD_DIM]], axis=0))
            vcat.append(jnp.concatenate([vp_ref[:, HEAD_DIM * hk:HEAD_DIM * (hk + 1)], vc_ref[:, HEAD_DIM * hk:HEAD_DIM * (hk + 1)]], axis=0)
                        .astype(BF16))
        q_heads = []
        for s in range(Q_DIM // 128):
            t = q_ref[:, 128 * s:128 * (s + 1)]
            qs = (t * cos_c + _rot_half(t) * sin_c).astype(BF16)
            qr_ref[:, 128 * s:128 * (s + 1)] = qs
            q_heads += [qs[:, :HEAD_DIM], qs[:, HEAD_DIM:]]
        G = ATTN_HEADS // KV_HEADS
        scores = [_dot(q_heads[hq], kcat[hq // G], NT) for hq in range(ATTN_HEADS)]
        probs = [_softmax_sink(scores[hq], valid, sink_ref[0, hq])[0] for hq in range(ATTN_HEADS)]
        outs = [_dot(probs[hq], vcat[hq // G], NN) for hq in range(ATTN_HEADS)]
        for s in range(Q_DIM // 128):
            o_ref[:, 128 * s:128 * (s + 1)] = jnp.concatenate([outs[2 * s], outs[2 * s + 1]], axis=1).astype(BF16)

    qspec = pl.BlockSpec((BLK, Q_DIM), lambda n: (n, OFF_Q // Q_DIM))
    orow = pl.BlockSpec((BLK, Q_DIM), lambda n: (n, 0))
    krow = pl.BlockSpec((BLK, KV_DIM), lambda n: (n, 0))
    return _pcall(body, name="attn_fwd", grid=(T // BLK,), in_specs=[qspec, kc, kp, vc, vp, pc, pp, inv, sink],
                  out_specs=[orow, orow, krow], out_shape=[_sds((T, Q_DIM), BF16), _sds((T, Q_DIM), BF16), _sds((T, KV_DIM), BF16)],
                  compiler_params=_cparams(("arbitrary",)))(projp, projp, projp, projp, projp, posf, posf, inv128, sinks)


def attn_bwd(qr, kr, projp, dattn, posf, inv128, sinks):
    T = projp.shape[0]
    _, _, vc, vp, pc, pp, inv, sink = _attn_specs(T)
    G = ATTN_HEADS // KV_HEADS

    def body(qr_ref, krc_ref, krp_ref, vc_ref, vp_ref, do_ref, pc_ref, pp_ref, inv_ref, sink_ref, dq_ref, dk_ref, dv_ref, dsk_ref):
        n = pl.program_id(0)

        @pl.when(n == 0)
        def _():
            dk_ref[...] = jnp.zeros_like(dk_ref)
            dv_ref[...] = jnp.zeros_like(dv_ref)
            dsk_ref[...] = jnp.zeros_like(dsk_ref)

        cos_c, sin_c = _rope_parts(pc_ref, inv_ref)
        cos_p, sin_p = _rope_parts(pp_ref, inv_ref)
        valid = _attn_mask(n)
        lane = lax.broadcasted_iota(jnp.int32, (1, 128), 1)
        kcat, vcat = [], []
        for hk in range(KV_HEADS):
            ksl = slice(HEAD_DIM * hk, HEAD_DIM * (hk + 1))
            kcat.append(jnp.concatenate([krp_ref[:, ksl], krc_ref[:, ksl]], axis=0))
            vcat.append(jnp.concatenate([vp_ref[:, ksl], vc_ref[:, ksl]], axis=0).astype(BF16))
        H = range(ATTN_HEADS)
        q_heads = [qr_ref[:, HEAD_DIM * hq:HEAD_DIM * (hq + 1)] for hq in H]
        do_heads = [do_ref[:, HEAD_DIM * hq:HEAD_DIM * (hq + 1)] for hq in H]
        soft = [_softmax_sink(_dot(q_heads[hq], kcat[hq // G], NT), valid, sink_ref[0, hq]) for hq in H]
        dps = [_dot(do_heads[hq], vcat[hq // G], NT) for hq in H]
        deltas = [jnp.sum(soft[hq][0] * dps[hq], axis=1, keepdims=True) for hq in H]
        dss = [(soft[hq][0] * (dps[hq] - deltas[hq]) * (HEAD_DIM ** -0.5)).astype(BF16) for hq in H]
        pbs = [soft[hq][0].astype(BF16) for hq in H]
        dsk = jnp.zeros((1, 128), F32)
        for hq in H:
            dsk = dsk + jnp.where(lane == hq, -jnp.sum(soft[hq][1] * deltas[hq], axis=0, keepdims=True), 0.0)
        dsk_ref[...] += dsk
        dq_heads = [_dot(dss[hq], kcat[hq // G], NN) for hq in H]
        dk_parts = [_dot(dss[hq], q_heads[hq], TN) for hq in H]
        dv_parts = [_dot(pbs[hq], do_heads[hq], TN) for hq in H]
        dk_heads = [sum(dk_parts[G * hk + 1:G * (hk + 1)], dk_parts[G * hk]) for hk in range(KV_HEADS)]
        dv_heads = [sum(dv_parts[G * hk + 1:G * (hk + 1)], dv_parts[G * hk]) for hk in range(KV_HEADS)]
        for s in range(Q_DIM // 128):
            t = jnp.concatenate([dq_heads[2 * s], dq_heads[2 * s + 1]], axis=1)
            dq_ref[:, 128 * s:128 * (s + 1)] = (t * cos_c - _rot_half(t) * sin_c).astype(BF16)
        cur = pl.ds(pl.multiple_of(n * BLK, BLK), BLK)
        prv = pl.ds(pl.multiple_of(jnp.maximum(n - 1, 0) * BLK, BLK), BLK)
        for s in range(KV_DIM // 128):
            tc = jnp.concatenate([dk_heads[2 * s][BLK:], dk_heads[2 * s + 1][BLK:]], axis=1)
            tp = jnp.concatenate([dk_heads[2 * s][:BLK], dk_heads[2 * s + 1][:BLK]], axis=1)
            cols = slice(128 * s, 128 * (s + 1))
            dk_ref[cur, cols] += tc * cos_c - _rot_half(tc) * sin_c
            dk_ref[prv, cols] += tp * cos_p - _rot_half(tp) * sin_p
            dv_ref[cur, cols] += jnp.concatenate([dv_heads[2 * s][BLK:], dv_heads[2 * s + 1][BLK:]], axis=1)
            dv_ref[prv, cols] += jnp.concatenate([dv_heads[2 * s][:BLK], dv_heads[2 * s + 1][:BLK]], axis=1)

    qrow = pl.BlockSpec((BLK, Q_DIM), lambda n: (n, 0))
    krc = pl.BlockSpec((BLK, KV_DIM), lambda n: (n, 0))
    krp = pl.BlockSpec((BLK, KV_DIM), lambda n: (jnp.maximum(n - 1, 0), 0))
    whole = pl.BlockSpec((T, KV_DIM), lambda n: (0, 0))
    return _pcall(body, name="attn_bwd", grid=(T // BLK,), in_specs=[qrow, krc, krp, vc, vp, qrow, pc, pp, inv, sink],
                  out_specs=[qrow, whole, whole, pl.BlockSpec((1, 128), lambda n: (0, 0))],
                  out_shape=[_sds((T, Q_DIM), BF16), _sds((T, KV_DIM), F32), _sds((T, KV_DIM), F32), _sds((1, 128), F32)],
                  compiler_params=_cparams(("arbitrary",)))(qr, kr, kr, projp, projp, dattn, posf, posf, inv128, sinks)


CONV_CB = 256


def _shift_down(x, s):
    row = lax.broadcasted_iota(jnp.int32, x.shape, 0)
    return jnp.where(row >= s, pltpu.roll(x, s, 0), 0.0)


def _shift_up(x, s):
    T = x.shape[0]
    row = lax.broadcasted_iota(jnp.int32, x.shape, 0)
    return jnp.where(row < T - s, pltpu.roll(x, T - s, 0), 0.0)


def _conv_pre(x, w_ref, b_ref):
    acc = x * w_ref[CONV_WIDTH - 1:CONV_WIDTH, :] + b_ref[...]
    for s in range(1, CONV_WIDTH):
        acc = acc + _shift_down(x, s) * w_ref[CONV_WIDTH - 1 - s:CONV_WIDTH - s, :]
    return acc


def conv_fwd(projp, conv_w, conv_b):
    T = projp.shape[0]

    def body(x_ref, w_ref, b_ref, o_ref):
        o_ref[...] = _silu(_conv_pre(x_ref[...], w_ref, b_ref))

    return _pcall(body, name="conv_fwd", grid=(CONV_DIM // CONV_CB,),
                  in_specs=[pl.BlockSpec((T, CONV_CB), lambda c: (0, OFF_XBC // CONV_CB + c)),
                            pl.BlockSpec((CONV_WIDTH, CONV_CB), lambda c: (0, c)), pl.BlockSpec((1, CONV_CB), lambda c: (0, c))],
                  out_specs=pl.BlockSpec((T, CONV_CB), lambda c: (0, c)), out_shape=_sds((T, CONV_DIM), F32),
                  compiler_params=_cparams(("arbitrary",)))(projp, conv_w, conv_b)


def conv_bwd(name, projp, dact, conv_w, conv_b, col0):
    T, C = dact.shape
    c0 = col0 // CONV_CB

    def body(x_ref, da_ref, w_ref, b_ref, dx_ref, dw_ref, db_ref):
        x = x_ref[...]
        dpre = da_ref[...] * _dsilu(_conv_pre(x, w_ref, b_ref))
        dx = dpre * w_ref[CONV_WIDTH - 1:CONV_WIDTH, :]
        dw_ref[CONV_WIDTH - 1:CONV_WIDTH, :] = jnp.sum(dpre * x, axis=0, keepdims=True)
        for s in range(1, CONV_WIDTH):
            i = CONV_WIDTH - 1 - s
            dx = dx + _shift_up(dpre, s) * w_ref[i:i + 1, :]
            dw_ref[i:i + 1, :] = jnp.sum(dpre * _shift_down(x, s), axis=0, keepdims=True)
        dx_ref[...] = dx.astype(BF16)
        db_ref[...] = jnp.sum(dpre, axis=0, keepdims=True)

    return _pcall(body, name=name, grid=(C // CONV_CB,),
                  in_specs=[pl.BlockSpec((T, CONV_CB), lambda c: (0, OFF_XBC // CONV_CB + c0 + c)),
                            pl.BlockSpec((T, CONV_CB), lambda c: (0, c)),
                            pl.BlockSpec((CONV_WIDTH, CONV_CB), lambda c: (0, c0 + c)), pl.BlockSpec((1, CONV_CB), lambda c: (0, c0 + c))],
                  out_specs=[pl.BlockSpec((T, CONV_CB), lambda c: (0, c)), pl.BlockSpec((CONV_WIDTH, CONV_CB), lambda c: (0, c)),
                             pl.BlockSpec((1, CONV_CB), lambda c: (0, c))],
                  out_shape=[_sds((T, C), BF16), _sds((CONV_WIDTH, C), F32), _sds((1, C), F32)],
                  compiler_params=_cparams(("arbitrary",)))(projp, dact, conv_w, conv_b)


def _softplus(x):
    return jnp.maximum(x, 0.0) + jnp.log1p(jnp.exp(-jnp.abs(x)))


def _tri(lower):
    r = lax.broadcasted_iota(jnp.int32, (BLK, BLK), 0)
    c = lax.broadcasted_iota(jnp.int32, (BLK, BLK), 1)
    return (r >= c) if lower else (c >= r)


def _ssd_chunk_setup(dt_ref, dtb_ref, alog_ref):
    raw = dt_ref[...] + dtb_ref[...]
    dt = _softplus(raw)
    aneg = -jnp.exp(alog_ref[...])
    a = dt * aneg
    acs = jnp.dot(_tri(True).astype(F32), a, precision=lax.Precision.HIGHEST, preferred_element_type=F32)
    return raw, dt, aneg, acs, acs.T


def _ssd_specs(T, rev):
    nc = T // BLK
    ci = (lambda c: nc - 1 - c) if rev else (lambda c: c)
    xs = pl.BlockSpec((BLK, D_INNER), lambda c: (ci(c), 0))
    bm = pl.BlockSpec((BLK, SSM_GROUPS * D_STATE), lambda c: (ci(c), D_INNER // (SSM_GROUPS * D_STATE)))
    cm = pl.BlockSpec((BLK, SSM_GROUPS * D_STATE), lambda c: (ci(c), D_INNER // (SSM_GROUPS * D_STATE) + 1))
    dt = pl.BlockSpec((BLK, DT_PAD), lambda c: (ci(c), OFF_DT // DT_PAD))
    v128 = pl.BlockSpec((1, 128), lambda c: (0, 0))
    dfull = pl.BlockSpec((1, D_INNER), lambda c: (0, 0))
    st = pl.BlockSpec((None, SSM_HEADS, HEAD_DIM, D_STATE), lambda c: (ci(c), 0, 0, 0))
    return xs, bm, cm, dt, v128, dfull, st, ci


GW = HEADS_PER_GROUP * HEAD_DIM


def _expanders():
    e = np.zeros((SSM_GROUPS, 128, GW), np.float32)
    for g in range(SSM_GROUPS):
        for hh in range(HEADS_PER_GROUP):
            e[g, HEADS_PER_GROUP * g + hh, HEAD_DIM * hh:HEAD_DIM * (hh + 1)] = 1.0
    return jnp.asarray(e, BF16), jnp.asarray(np.transpose(e, (0, 2, 1)).copy(), BF16)


def _split2(v):
    hi = lax.bitcast_convert_type(lax.bitcast_convert_type(v, jnp.uint32) & jnp.uint32(0xFFFF0000), F32)
    return hi.astype(BF16), (v - hi).astype(BF16)


def _dotx(a, b):
    if a.dtype == BF16:
        hi, lo = _split2(b)
        return jnp.dot(a, hi, preferred_element_type=F32) + jnp.dot(a, lo, preferred_element_type=F32)
    hi, lo = _split2(a)
    return jnp.dot(hi, b, preferred_element_type=F32) + jnp.dot(lo, b, preferred_element_type=F32)


def _decay(acs, acsT, h, tril):
    return jnp.where(tril, jnp.exp(jnp.where(tril, acs[:, h:h + 1] - acsT[h:h + 1, :], 0.0)), 0.0)


def ssd_fwd(xbc, projp, dtb, alog, dfull):
    T = xbc.shape[0]
    nc = T // BLK
    xs, bm, cm, dts, v128, dfs, st, _ = _ssd_specs(T, False)
    E, _ = _expanders()

    def body(xs_ref, b_ref, c_ref, dt_ref, dtb_ref, alog_ref, d_ref, e_ref, y_ref, st_ref, h_scr):
        c = pl.program_id(0)

        @pl.when(c == 0)
        def _():
            h_scr[...] = jnp.zeros_like(h_scr)

        _, dt, _, acs, acsT = _ssd_chunk_setup(dt_ref, dtb_ref, alog_ref)
        tril = _tri(True)
        alast = acs[BLK - 1:BLK, :]
        eacs = jnp.exp(acs)
        wmat = jnp.exp(alast - acs)
        gam = jnp.exp(alast)
        for g in range(SSM_GROUPS):
            gl = slice(GW * g, GW * (g + 1))
            hsl = slice(HEADS_PER_GROUP * g, HEADS_PER_GROUP * (g + 1))
            heads = [HEADS_PER_GROUP * g + hh for hh in range(HEADS_PER_GROUP)]
            Eg = e_ref[g]
            B = b_ref[:, D_STATE * g:D_STATE * (g + 1)].astype(BF16)
            C = c_ref[:, D_STATE * g:D_STATE * (g + 1)].astype(BF16)
            cb = _dot(C, B, NT)
            x_g = xs_ref[:, gl]
            xd_g = x_g * _dotx(dt, Eg)
            hold = h_scr[hsl]
            st_ref[hsl] = hold
            hcat = hold.reshape(GW, D_STATE)
            yoff = _dotx(eacs, Eg) * _dot(C, hcat, NT)
            S = _dot(xd_g * _dotx(wmat, Eg), B, TN)
            Ms = [cb * _decay(acs, acsT, h, tril) for h in heads]
            ys = [_dot(Ms[hh], xd_g[:, HEAD_DIM * hh:HEAD_DIM * (hh + 1)], NN) for hh in range(HEADS_PER_GROUP)]
            for hh, h in enumerate(heads):
                h_scr[h] = gam[:, h:h + 1] * hold[hh] + S[HEAD_DIM * hh:HEAD_DIM * (hh + 1)]
            y_ref[:, gl] = jnp.concatenate(ys, axis=1) + yoff + d_ref[:, gl] * x_g

    espec = pl.BlockSpec((SSM_GROUPS, 128, GW), lambda c: (0, 0, 0))
    return _pcall(body, name="ssd_fwd", grid=(nc,), in_specs=[xs, bm, cm, dts, v128, v128, dfs, espec],
                  out_specs=[xs, st], out_shape=[_sds((T, D_INNER), F32), _sds((nc, SSM_HEADS, HEAD_DIM, D_STATE), F32)],
                  scratch_shapes=[pltpu.VMEM((SSM_HEADS, HEAD_DIM, D_STATE), F32)],
                  compiler_params=_cparams(("arbitrary",)))(xbc, xbc, xbc, projp, dtb, alog, dfull, E)


def ssd_bwd(xbc, projp, dtb, alog, dfull, states, dy):
    T = xbc.shape[0]
    nc = T // BLK
    xs, bm, cm, dts, v128, dfs, st, ci = _ssd_specs(T, True)
    gn = SSM_GROUPS * D_STATE
    E, ET = _expanders()

    def body(xs_ref, b_ref, c_ref, dt_ref, dtb_ref, alog_ref, d_ref, st_ref, dy_ref, e_ref, et_ref,
             dxs_ref, dB_ref, dC_ref, ddt_ref, dal_ref, dD_ref, ddtb_ref, dh_scr):
        i = pl.program_id(0)

        @pl.when(i == 0)
        def _():
            dh_scr[...] = jnp.zeros_like(dh_scr)
            dal_ref[...] = jnp.zeros_like(dal_ref)
            dD_ref[...] = jnp.zeros_like(dD_ref)
            ddtb_ref[...] = jnp.zeros_like(ddtb_ref)

        raw, dt, aneg, acs, acsT = _ssd_chunk_setup(dt_ref, dtb_ref, alog_ref)
        tril = _tri(True)
        lane = lax.broadcasted_iota(jnp.int32, (BLK, 128), 1)
        sub = lax.broadcasted_iota(jnp.int32, (BLK, 128), 0)
        alast = acs[BLK - 1:BLK, :]
        eacs = jnp.exp(acs)
        wmat = jnp.exp(alast - acs)
        gam = jnp.exp(alast)
        gcol = jnp.exp(acsT[:, BLK - 1:BLK])
        ds_col = jnp.zeros((BLK, 128), F32)
        ds_row = jnp.zeros((BLK, 128), F32)
        ddt_col = jnp.zeros((BLK, 128), F32)
        dDm = jnp.zeros((BLK, 128), F32)
        hl = [slice(HEAD_DIM * hh, HEAD_DIM * (hh + 1)) for hh in range(HEADS_PER_GROUP)]
        for g in range(SSM_GROUPS):
            gl = slice(GW * g, GW * (g + 1))
            gs = slice(D_STATE * g, D_STATE * (g + 1))
            hsl = slice(HEADS_PER_GROUP * g, HEADS_PER_GROUP * (g + 1))
            heads = [HEADS_PER_GROUP * g + hh for hh in range(HEADS_PER_GROUP)]
            Eg, ETg = e_ref[g], et_ref[g]
            B = b_ref[:, gs].astype(BF16)
            C = c_ref[:, gs].astype(BF16)
            cb = _dot(C, B, NT)
            x_g, dy_g = xs_ref[:, gl], dy_ref[:, gl]
            dt_x, w_x = _dotx(dt, Eg), _dotx(wmat, Eg)
            xd_g = x_g * dt_x
            dye = dy_g * _dotx(eacs, Eg)
            hcat = st_ref[hsl].reshape(GW, D_STATE)
            dSv = dh_scr[hsl]
            dScat = dSv.reshape(GW, D_STATE)
            dDm = dDm + _dotx(dy_g * x_g, ETg)
            dH_y = _dot(dye, C, TN)
            dC_g = _dot(dye, hcat, NN)
            ds_col = ds_col + _dotx(dye * _dot(C, hcat, NT), ETg)
            dxdw = _dot(B, dScat, NT)
            dB_g = _dot(xd_g * w_x, dScat, NN)
            dww = _dotx(xd_g * dxdw, ETg) * wmat
            ds_col = ds_col - dww + jnp.where(sub == BLK - 1, jnp.sum(dww, axis=0, keepdims=True), 0.0)
            hd = jnp.sum(_dotx(Eg, dScat * hcat), axis=1, keepdims=True) * gcol
            ds_row = ds_row - jnp.where(lane == BLK - 1, hd, 0.0)
            decays = [_decay(acs, acsT, h, tril) for h in heads]
            Ms = [cb * d for d in decays]
            dMs = [_dot(dy_g[:, hl[hh]], xd_g[:, hl[hh]], NT) for hh in range(HEADS_PER_GROUP)]
            dxd1 = [_dot(Ms[hh], dy_g[:, hl[hh]], TN) for hh in range(HEADS_PER_GROUP)]
            dG = jnp.zeros((BLK, BLK), F32)
            for hh, h in enumerate(heads):
                Q = dMs[hh] * Ms[hh]
                ds_col = ds_col + jnp.where(lane == h, jnp.sum(Q, axis=1, keepdims=True), 0.0)
                ds_row = ds_row + jnp.where(sub == h, jnp.sum(Q, axis=0, keepdims=True), 0.0)
                dG = dG + dMs[hh] * decays[hh]
            dxd_g = jnp.concatenate(dxd1, axis=1) + w_x * dxdw
            dxs_ref[:, gl] = d_ref[:, gl] * dy_g + dxd_g * dt_x
            ddt_col = ddt_col + _dotx(dxd_g * x_g, ETg)
            dC_ref[:, gs] = dC_g + _dot(dG, B, NN)
            dB_ref[:, gs] = dB_g + _dot(dG, C, TN)
            for hh, h in enumerate(heads):
                dh_scr[h] = gam[:, h:h + 1] * dSv[hh] + dH_y[hl[hh]]
        ds_all = ds_col - ds_row.T
        da = jnp.dot(_tri(False).astype(F32), ds_all, precision=lax.Precision.HIGHEST, preferred_element_type=F32)
        ddt = ddt_col + da * aneg
        draw = jnp.where(lane < SSM_HEADS, ddt * _sigmoid(raw), 0.0)
        ddt_ref[...] = draw.astype(BF16)
        dal_ref[...] += jnp.sum(da * dt, axis=0, keepdims=True) * aneg
        ddtb_ref[...] += jnp.sum(draw, axis=0, keepdims=True)
        dD_ref[...] += jnp.sum(dDm, axis=0, keepdims=True)

    gblk = pl.BlockSpec((BLK, gn), lambda c: (ci(c), 0))
    espec = pl.BlockSpec((SSM_GROUPS, 128, GW), lambda c: (0, 0, 0))
    etspec = pl.BlockSpec((SSM_GROUPS, GW, 128), lambda c: (0, 0, 0))
    return _pcall(body, name="ssd_bwd", grid=(nc,), in_specs=[xs, bm, cm, dts, v128, v128, dfs, st, xs, espec, etspec],
                  out_specs=[xs, gblk, gblk, pl.BlockSpec((BLK, DT_PAD), lambda c: (ci(c), 0)), v128, v128, v128],
                  out_shape=[_sds((T, D_INNER), F32), _sds((T, gn), F32), _sds((T, gn), F32), _sds((T, DT_PAD), BF16),
                             _sds((1, 128), F32), _sds((1, 128), F32), _sds((1, 128), F32)],
                  scratch_shapes=[pltpu.VMEM((SSM_HEADS, HEAD_DIM, D_STATE), F32)],
                  compiler_params=_cparams(("arbitrary",)))(xbc, xbc, xbc, projp, dtb, alog, dfull, states, dy, E, ET)


_WIN_ORDER = ("z", "ga", "gs", "xbc", "q", "k", "v", "dt")


PERM_TILE = 512


def _win_row_moves():
    off = dict(z=OFF_Z, ga=OFF_GA, gs=OFF_GS, xbc=OFF_XBC, q=OFF_Q, k=OFF_K, v=OFF_V, dt=OFF_DT)
    per = IN_DIM // N_DEV
    tiles = [[] for _ in range(-(-IN_PAD // PERM_TILE))]
    for nm in _WIN_ORDER:
        s, w = SEG[nm]
        d = off[nm]
        while w > 0:
            j, r = divmod(s, per)
            n = min(w, per - r, PERM_TILE - d % PERM_TILE)
            tiles[d // PERM_TILE].append((j, r, n, d % PERM_TILE))
            s, w, d = s + n, w - n, d + n
    return tiles


def _win_to_padded(win_g):
    per = IN_DIM // N_DEV
    moves = _win_row_moves()

    def body(w_ref, o_ref, slots, stage, in_sems, out_sems):
        loads = [pltpu.make_async_copy(w_ref.at[j], slots.at[j], in_sems.at[j]) for j in range(N_DEV)]
        for cp in loads:
            cp.start()
        arrived = [False] * N_DEV
        stores = [None, None]
        for t, pieces in enumerate(moves):
            rows = min(PERM_TILE, IN_PAD - PERM_TILE * t)
            b = t % 2
            if stores[b] is not None:
                stores[b].wait()
            filled = 0
            for j, r, n, d in pieces:
                if not arrived[j]:
                    loads[j].wait()
                    arrived[j] = True
                stage[b, pl.ds(d, n), :] = slots[j, pl.ds(r, n), :]
                filled = max(filled, d + n)
            if filled < rows:
                stage[b, pl.ds(filled, rows - filled), :] = jnp.zeros((rows - filled, D_MODEL), stage.dtype)
            stores[b] = pltpu.make_async_copy(stage.at[b, pl.ds(0, rows), :], o_ref.at[pl.ds(PERM_TILE * t, rows), :], out_sems.at[b])
            stores[b].start()
        for cp in stores:
            cp.wait()

    return _pcall(body, name="w_in_rows", in_specs=[_ANY], out_specs=_ANY, out_shape=_sds((IN_PAD, D_MODEL), win_g.dtype),
                  scratch_shapes=[pltpu.VMEM((N_DEV, per, D_MODEL), win_g.dtype), pltpu.VMEM((2, PERM_TILE, D_MODEL), win_g.dtype),
                                  pltpu.SemaphoreType.DMA((N_DEV,)), pltpu.SemaphoreType.DMA((2,))],
                  compiler_params=pltpu.CompilerParams(vmem_limit_bytes=60 * 1024 * 1024))(win_g)


def _padded_to_win(dw):
    off = dict(z=OFF_Z, ga=OFF_GA, gs=OFF_GS, xbc=OFF_XBC, q=OFF_Q, k=OFF_K, v=OFF_V, dt=OFF_DT)
    per = IN_DIM // N_DEV
    blocks = []
    for j in range(N_DEV):
        lo, hi, rows = j * per, (j + 1) * per, []
        for nm in ("q", "k", "v", "z", "xbc", "dt", "ga", "gs"):
            s, w = SEG[nm]
            a, b = max(lo, s), min(hi, s + w)
            if a < b:
                rows.append(dw[off[nm] + a - s:off[nm] + b - s])
        blocks.append(jnp.concatenate(rows, axis=0))
    return jnp.stack(blocks)


def _pad128(v):
    return jnp.pad(v, ((0, 0), (0, 128 - v.shape[1])))


_SMALL = (("loss", 128, 1), ("g_mix", 2048, 2048), ("conv_b", 3072, 3072), ("dt_bias", 128, 32), ("a_log", 128, 32),
          ("d_skip", 128, 32), ("g_ssd", 2048, 2048), ("sinks", 128, 16), ("g_ffn", 2048, 2048), ("g_ple", 2048, 2048),
          ("g_final", 2048, 2048))


def _small_vec(d):
    parts = []
    for nm, pw, w in _SMALL:
        v = d[nm].reshape(1, -1).astype(F32)
        parts.append(jnp.pad(v[:, :min(v.shape[1], pw)], ((0, 0), (0, pw - min(v.shape[1], pw)))))
    return jnp.concatenate(parts, axis=1)


def _small_split(vec):
    out, o = {}, 0
    for nm, pw, w in _SMALL:
        out[nm] = vec[0, o:o + w]
        o += pw
    return out


def kernel(x, p, positions, g_mix, w_in, conv_w, conv_b, dt_bias, a_log, d_skip, g_ssd, sinks, w_attn_br, w_ssd_br, w_o, g_ffn, w_gate, w_up, w_down, g_ple, w_ple_gate, w_ple_proj, g_final, loss_target, m_g_mix, m_w_in, m_conv_w, m_conv_b, m_dt_bias, m_a_log, m_d_skip, m_g_ssd, m_sinks, m_w_attn_br, m_w_ssd_br, m_w_o, m_g_ffn, m_w_gate, m_w_up, m_w_down, m_g_ple, m_w_ple_gate, m_w_ple_proj, m_g_final, v_g_mix, v_w_in, v_conv_w, v_conv_b, v_dt_bias, v_a_log, v_d_skip, v_g_ssd, v_sinks, v_w_attn_br, v_w_ssd_br, v_w_o, v_g_ffn, v_w_gate, v_w_up, v_w_down, v_g_ple, v_w_ple_gate, v_w_ple_proj, v_g_final):
    T = x.shape[1]
    D = D_MODEL
    W = dict(g_mix=g_mix, w_in=w_in, conv_w=conv_w, conv_b=conv_b, dt_bias=dt_bias, a_log=a_log, d_skip=d_skip, g_ssd=g_ssd,
             sinks=sinks, w_attn_br=w_attn_br, w_ssd_br=w_ssd_br, w_o=w_o, g_ffn=g_ffn, w_gate=w_gate, w_up=w_up, w_down=w_down,
             g_ple=g_ple, w_ple_gate=w_ple_gate, w_ple_proj=w_ple_proj, g_final=g_final)
    Mo = dict(g_mix=m_g_mix, w_in=m_w_in, conv_w=m_conv_w, conv_b=m_conv_b, dt_bias=m_dt_bias, a_log=m_a_log, d_skip=m_d_skip,
              g_ssd=m_g_ssd, sinks=m_sinks, w_attn_br=m_w_attn_br, w_ssd_br=m_w_ssd_br, w_o=m_w_o, g_ffn=m_g_ffn, w_gate=m_w_gate,
              w_up=m_w_up, w_down=m_w_down, g_ple=m_g_ple, w_ple_gate=m_w_ple_gate, w_ple_proj=m_w_ple_proj, g_final=m_g_final)
    Vo = dict(g_mix=v_g_mix, w_in=v_w_in, conv_w=v_conv_w, conv_b=v_conv_b, dt_bias=v_dt_bias, a_log=v_a_log, d_skip=v_d_skip,
              g_ssd=v_g_ssd, sinks=v_sinks, w_attn_br=v_w_attn_br, w_ssd_br=v_w_ssd_br, w_o=v_w_o, g_ffn=v_g_ffn, w_gate=v_w_gate,
              w_up=v_w_up, w_down=v_w_down, g_ple=v_g_ple, w_ple_gate=v_w_ple_gate, w_ple_proj=v_w_ple_proj, g_final=v_g_final)
    order = ["g_mix", "w_in", "conv_w", "conv_b", "dt_bias", "a_log", "d_skip", "g_ssd", "sinks", "w_attn_br", "w_ssd_br", "w_o",
             "g_ffn", "w_gate", "w_up", "w_down", "g_ple", "w_ple_gate", "w_ple_proj", "g_final"]
    big = ["w_in", "conv_w", "w_attn_br", "w_ssd_br", "w_o", "w_gate", "w_up", "w_down", "w_ple_gate", "w_ple_proj"]

    x2 = x.reshape(T, D)
    p2 = p.reshape(T, PLE_DIM)
    tgt = loss_target.reshape(T, D)
    posf = positions.reshape(T, 1).astype(F32)
    inv = ROPE_THETA ** (-np.arange(HEAD_DIM // 2, dtype=np.float32) * 2.0 / HEAD_DIM)
    inv128 = jnp.asarray(np.tile(inv, 128 // (HEAD_DIM // 2)).reshape(1, 128).astype(np.float32))
    transposed = ("w_in", "w_gate", "w_up")

    def shard2d(a, n):
        a = a.reshape(a.shape[-2:])
        return a.T if n in transposed else a

    sh = {n: shard2d(W[n], n) for n in big}

    del _PENDING[:]
    me = 4 * lax.axis_index("x") + 2 * lax.axis_index("y") + lax.axis_index("c")
    groups = (("w_in",), ("conv_w", "w_attn_br", "w_ssd_br", "w_o"), ("w_gate", "w_up", "w_down"), ("w_ple_gate", "w_ple_proj"))
    send = {n: sh[n] if n == "conv_w" else sh[n].astype(BF16) for n in big}
    started, prev = [], None
    for gi, grp in enumerate(groups):
        zones = [lax.dynamic_update_index_in_dim(lax.empty((N_DEV,) + send[n].shape, send[n].dtype), send[n], me, 0) for n in grp]
        h = split_start("gather_start_%d" % gi, "gather", ICI_SAME_CORE, [], lands=zones, after=prev)
        prev = h["token"]
        started.append(h)
    gathered, fwd = {}, {}

    def forward_start(gi, after):
        _, lands = split_wait("gather_wait_%d" % gi, started[gi], after)
        fwd[gi] = split_start("forward_start_%d" % gi, "forward", FORWARD_BLOCKS, [], lands=lands)

    def forward_wait(gi, after):
        _, full = split_wait("forward_wait_%d" % gi, fwd[gi], after)
        gathered.update(zip(groups[gi], full))

    u = rms_fwd("norm_mix", x2, g_mix)
    forward_start(0, u)
    forward_wait(0, u)
    winp = _win_to_padded(gathered["w_in"])
    dtb = _pad128(dt_bias)
    alog = _pad128(a_log)
    dfull = jnp.repeat(d_skip.reshape(SSM_HEADS), HEAD_DIM).reshape(1, D_INNER)

    projp = mm_nt("in_proj", u, winp, 640)
    forward_start(1, projp)
    attn, qr, kr = attn_fwd(projp, posf, inv128, sinks)
    forward_wait(1, attn)
    convw = jnp.transpose(gathered["conv_w"], (1, 0, 2)).reshape(CONV_WIDTH, CONV_DIM)
    wab = gathered["w_attn_br"]
    wsb = gathered["w_ssd_br"].reshape(D, D)
    wo = gathered["w_o"].reshape(D, D)
    xbc = conv_fwd(projp, convw, conv_b)
    y, states = ssd_fwd(xbc, projp, dtb, alog, dfull)
    yn = gnorm_fwd(y, projp, g_ssd)
    out_a = mm_nn_colblk("attn_br", attn, wab)
    out_s = mm_nn("ssd_br", yn, wsb, 512)
    forward_start(2, out_s)
    merged = merge_fwd(projp, out_a, out_s)
    h1 = mm_nn("o_proj", merged, wo, 512, residual=x2)
    f = rms_fwd("norm_ffn", h1, g_ffn)
    forward_wait(2, f)
    wgt, wut, wd = (gathered[n].reshape(FFN_HIDDEN, D) for n in ("w_gate", "w_up", "w_down"))
    gate, up, act = ffn_up(f, wgt, wut)
    forward_start(3, act)
    h2 = mm_nn_red("ffn_down", act, wd, 512, FFN_HIDDEN, residual=h1)
    r = rms_fwd("norm_ple", h2, g_ple)
    forward_wait(3, r)
    wpg = gathered["w_ple_gate"].reshape(D, D)
    wpp = gathered["w_ple_proj"]
    pg = mm_nn("ple_gate", r, wpg, 512)
    pp = mm_nn_colblk("ple_proj", p2, wpp)
    loss_v, dh3, dpg, dpp, dg_final = head_fwd_bwd(h2, pg, pp, g_final.reshape(1, D), tgt)

    gw = {}
    scat = []

    def scatter_start(names):
        scat.append((names, split_start("scatter_start_%d" % len(scat), "scatter", ALL_PEERS, [gw[n] for n in names])))

    gw["w_ple_proj"] = mm_tn_colblk("dw_ple_proj", p2, dpp, PLE_DIM)
    dr = mm_nt("d_ple_gate", dpg, wpg, 512)
    gw["w_ple_gate"] = mm_tn("dw_ple_gate", r, dpg, 512, D).reshape(N_DEV, D // N_DEV, D)
    scatter_start(("w_ple_proj", "w_ple_gate"))
    dh2, dh2b, dg_ple = rms_bwd("norm_ple_bwd", h2, g_ple, dr, dh3)
    dgate, dup = ffn_down_bwd(dh2b, wd, gate, up)
    per = FFN_HIDDEN // N_DEV
    gw["w_down"] = mm_tn("dw_down", act, dh2b, FFN_TILE, D).reshape(N_DEV, per, D)
    gw["w_gate"] = mm_tn("dw_gate", dgate, f, FFN_TILE, D).reshape(N_DEV, per, D)
    gw["w_up"] = mm_tn("dw_up", dup, f, FFN_TILE, D).reshape(N_DEV, per, D)
    scatter_start(("w_down", "w_gate", "w_up"))
    df = ffn_up_bwd(dgate, dup, wgt, wut)
    dh1, dh1b, dg_ffn = rms_bwd("norm_ffn_bwd", h1, g_ffn, df, dh2)
    dmerged = mm_nt("d_o_proj", dh1b, wo, 512)
    gw["w_o"] = mm_tn("dw_o", merged, dh1b, 512, D).reshape(N_DEV, D // N_DEV, D)
    dout_a, dout_s, dga, dgs = merge_bwd(projp, out_a, out_s, dmerged)
    gw["w_ssd_br"] = mm_tn("dw_ssd_br", yn, dout_s, 512, D).reshape(N_DEV, D // N_DEV, D)
    gw["w_attn_br"] = mm_tn_colblk("dw_attn_br", attn, dout_a, D // N_DEV)
    scatter_start(("w_o", "w_ssd_br", "w_attn_br"))
    dyn = mm_nt("d_ssd_br", dout_s, wsb, 512)
    dattn = attn_br_bwd(dout_a, wab)
    dy, dz, dg_ssd = gnorm_bwd(y, projp, g_ssd, dyn)
    dxs, dbm, dcm, ddt, dal, ddsk, ddtb = ssd_bwd(xbc, projp, dtb, alog, dfull, states, dy)
    dx_x, dwc_x, dbc_x = conv_bwd("conv_bwd_x", projp, dxs, convw, conv_b, 0)
    dx_b, dwc_b, dbc_b = conv_bwd("conv_bwd_b", projp, dbm, convw, conv_b, D_INNER)
    dx_c, dwc_c, dbc_c = conv_bwd("conv_bwd_c", projp, dcm, convw, conv_b, D_INNER + SSM_GROUPS * D_STATE)
    dq, dk, dv, dsk = attn_bwd(qr, kr, projp, dattn, posf, inv128, sinks)
    dproj = jnp.concatenate([dz, dga, dgs, dx_x, dx_b, dx_c, dq, dk.astype(BF16), dv.astype(BF16), ddt], axis=1)
    gw_in = _padded_to_win(mm_tn("dw_in", dproj, u, 640, D))
    pair = split_start("pair_start", "pair", FORWARD_BLOCKS, [gw_in])
    dconvw = jnp.concatenate([dwc_x, dwc_b, dwc_c], axis=1)
    gw["conv_w"] = jnp.transpose(dconvw.reshape(CONV_WIDTH, N_DEV, CONV_DIM // N_DEV), (1, 0, 2))
    scatter_start(("conv_w",))
    du_tm = min(512, T)
    tiles = T // du_tm
    first = max(tiles // 4, 1)
    du = mm_nn_red("d_in_proj_a", dproj, winp, 512, IN_PAD, rows=(0, first), tm=du_tm)
    (gw_in,), (sibling_part,) = split_wait("pair_wait", pair, du)
    pair_slots = jnp.stack([jnp.bitwise_xor(me, k) for k in FORWARD_BLOCKS]).astype(jnp.int32)
    core = split_start("core_start", "scatter_core", ICI_SAME_CORE, [pair_sum("pair_sum_w_in", gw_in, pair_slots, sibling_part)])
    if first < tiles:
        du = mm_nn_red("d_in_proj_b", dproj, winp, 512, IN_PAD, rows=(first, tiles - first), prev=du, tm=du_tm)
    gx, _, dg_mix = rms_bwd("norm_mix_bwd", x2, g_mix, du, dh1)

    small_g = dict(loss=loss_v[:, :1], g_mix=dg_mix, conv_b=jnp.concatenate([dbc_x, dbc_b, dbc_c], axis=1), dt_bias=ddtb,
                   a_log=dal, d_skip=ddsk, g_ssd=dg_ssd, sinks=dsk, g_ffn=dg_ffn, g_ple=dg_ple, g_final=dg_final)
    vec = _small_vec(small_g)
    small = split_start("small_start", "gather", ALL_PEERS, [],
                        lands=[lax.dynamic_update_index_in_dim(lax.empty((N_DEV,) + vec.shape, F32), vec, me, 0)])

    res = {}
    after = [gx]
    for si, (names, h) in enumerate(scat):
        srcs, lands = split_wait("scatter_wait_%d" % si, h, after)
        for n, mine, arrived in zip(names, srcs, lands):
            res[n] = adamw("adamw_" + n, arrived, sh[n], shard2d(Mo[n], n), shard2d(Vo[n], n), own=mine, own_slot=me)
        after = [res[n][0] for n in names]
    zero = jnp.zeros((1, 1), F32)
    _, (vec_parts,) = split_wait("small_wait", small, [res[n][0] for n in res])
    sres = adamw("adamw_small", vec_parts, _small_vec({**W, "loss": zero}), _small_vec({**Mo, "loss": zero}),
                 _small_vec({**Vo, "loss": zero}))
    ssplit = [_small_split(a) for a in sres]

    (pair_sums,), (arrived,) = split_wait("core_wait", core, [sres[0]])
    res["w_in"] = adamw("adamw_w_in", arrived, sh["w_in"], shard2d(Mo["w_in"], "w_in"), shard2d(Vo["w_in"], "w_in"),
                        own=pair_sums, own_slot=0)
    loss = ssplit[0]["loss"].reshape(())
    for n in order:
        if n not in res:
            res[n] = tuple(s[n].reshape(W[n].shape) for s in ssplit)
        else:
            res[n] = tuple((a.T if n in transposed else a).reshape(W[n].shape) for a in res[n])
    outs = [loss, gx.reshape(x.shape)]
    for k in range(4):
        outs += [res[n][k] for n in order]
    return tuple(outs)
```

```python
import functools

import numpy as np
import jax
import jax.numpy as jnp
from jax import lax
from jax.experimental import pallas as pl
from jax.experimental.pallas import tpu as pltpu

F32 = jnp.float32
BF16 = jnp.bfloat16

N_DEV = 8
D_MODEL = 2048
HEAD_DIM = 64
ATTN_HEADS = 16
KV_HEADS = 4
Q_DIM = 1024
KV_DIM = 256
BLK = 128
D_INNER = 2048
SSM_HEADS = 32
SSM_GROUPS = 4
HEADS_PER_GROUP = 8
D_STATE = 128
CONV_WIDTH = 4
CONV_DIM = 3072
FFN_HIDDEN = 5632
PLE_DIM = 256
IN_DIM = 10784
NORM_EPS = 1e-6
SSM_NORM_EPS = 1e-5
ROPE_THETA = 10000.0

OFF_Z, OFF_GA, OFF_GS, OFF_XBC, OFF_Q, OFF_K, OFF_V, OFF_DT = 0, 2048, 4096, 6144, 9216, 10240, 10496, 10752
IN_PAD = 10880
DT_PAD = 128
SEG = dict(q=(0, 1024), k=(1024, 256), v=(1280, 256), z=(1536, 2048), xbc=(3584, 3072), dt=(6656, 32),
           ga=(6688, 2048), gs=(8736, 2048))

ADAM_LR, ADAM_B1, ADAM_B2, ADAM_EPS, ADAM_WD, ADAM_STEP = 0.001, 0.9, 0.999, 1e-08, 0.01, 10

VMEM_LIMIT = 56 * 1024 * 1024

NN = (((1,), (0,)), ((), ()))
NT = (((1,), (1,)), ((), ()))
TN = (((0,), (0,)), ((), ()))


_PENDING = []


def _raw_call(body, **kw):
    return pl.pallas_call(body, **kw)


def _pcall(body, **kw):
    if "in_specs" not in kw:
        return _raw_call(body, **kw)
    deps = list(_PENDING)
    del _PENDING[:]
    if not deps:
        return _raw_call(body, **kw)
    n_in = len(kw["in_specs"])

    def tied(*refs):
        return body(*refs[:n_in], *refs[n_in + len(deps):])

    kw["in_specs"] = list(kw["in_specs"]) + [pl.BlockSpec(memory_space=pl.ANY)] * len(deps)
    call = _raw_call(tied, **kw)
    return lambda *ops: call(*ops, *deps)


def _cparams(sem=None):
    if sem is None:
        return pltpu.CompilerParams(vmem_limit_bytes=VMEM_LIMIT)
    return pltpu.CompilerParams(vmem_limit_bytes=VMEM_LIMIT, dimension_semantics=sem)


def _dot(a, b, dn):
    return lax.dot_general(a.astype(BF16), b.astype(BF16), dn, preferred_element_type=F32)


def _sigmoid(x):
    return 1.0 / (1.0 + jnp.exp(-x))


def _silu(x):
    return x * _sigmoid(x)


def _dsilu(x):
    s = _sigmoid(x)
    return s * (1.0 + x * (1.0 - s))


def _matmul(name, pairs, pair_specs, dn, grid, out_shapes, out_specs, nred=1, extra=(), extra_specs=(),
            epilogue=None, acc_shape=None, alias=None):
    n_in = 2 * len(pairs) + len(extra)
    n_out = len(out_shapes)

    def body(*refs):
        ins = refs[:2 * len(pairs)]
        ex = [r for r, sp in zip(refs[2 * len(pairs):n_in], extra_specs) if sp.memory_space != pl.ANY]
        outs = refs[n_in:n_in + n_out]

        def prod():
            s = None
            for p in range(len(pairs)):
                d = _dot(ins[2 * p][...], ins[2 * p + 1][...], dn)
                s = d if s is None else s + d
            return s

        def finish(val):
            if epilogue is None:
                outs[0][...] = val.astype(outs[0].dtype)
            else:
                res = epilogue(val, *[e[...] for e in ex])
                for o, r in zip(outs, res):
                    o[...] = r.astype(o.dtype)

        if nred == 1:
            finish(prod())
        else:
            acc = refs[n_in + n_out]
            k = pl.program_id(len(grid) - 1)

            @pl.when(k == 0)
            def _():
                acc[...] = jnp.zeros_like(acc)

            acc[...] += prod()

            @pl.when(k == nred - 1)
            def _():
                finish(acc[...])

    operands = []
    specs = []
    for (a, b), (sa, sb) in zip(pairs, pair_specs):
        operands += [a, b]
        specs += [sa, sb]
    operands += list(extra)
    specs += list(extra_specs)
    scratch = [pltpu.VMEM(acc_shape, F32)] if nred > 1 else []
    sem = ("arbitrary",) * len(grid)
    res = _pcall(body, name=name, grid=grid, in_specs=specs, out_specs=list(out_specs), input_output_aliases=dict(alias or {}),
                 out_shape=list(out_shapes), scratch_shapes=scratch, compiler_params=_cparams(sem))(*operands)
    return res


def _sds(shape, dtype):
    return jax.ShapeDtypeStruct(shape, dtype)


def _row_tile(T):
    return min(1024, T)


def mm_nn(name, a, b, tn, out_dtype=F32, residual=None):
    M, K = a.shape
    N = b.shape[1]
    tm = _row_tile(M)
    grid = (M // tm, N // tn)
    extra, especs, epi = (), (), None
    if residual is not None:
        extra = (residual,)
        especs = (pl.BlockSpec((tm, tn), lambda i, n: (i, n)),)
        epi = lambda v, r: (v + r,)
    return _matmul(name, [(a, b)], [(pl.BlockSpec((tm, K), lambda i, n: (i, 0)), pl.BlockSpec((K, tn), lambda i, n: (0, n)))],
                   NN, grid, [_sds((M, N), out_dtype)], [pl.BlockSpec((tm, tn), lambda i, n: (i, n))],
                   extra=extra, extra_specs=especs, epilogue=epi)[0]


def mm_nn_colblk(name, a, b, out_dtype=F32):
    M, K = a.shape
    J, _, nb = b.shape
    tm = _row_tile(M)
    grid = (M // tm, J)
    return _matmul(name, [(a, b)], [(pl.BlockSpec((tm, K), lambda i, j: (i, 0)), pl.BlockSpec((None, K, nb), lambda i, j: (j, 0, 0)))],
                   NN, grid, [_sds((M, J * nb), out_dtype)], [pl.BlockSpec((tm, nb), lambda i, j: (i, j))])[0]


def mm_nt(name, a, w, tr, out_dtype=F32):
    M, C = a.shape
    R = w.shape[0]
    tm = _row_tile(M)
    grid = (M // tm, R // tr)
    return _matmul(name, [(a, w)], [(pl.BlockSpec((tm, C), lambda i, r: (i, 0)), pl.BlockSpec((tr, C), lambda i, r: (r, 0)))],
                   NT, grid, [_sds((M, R), out_dtype)], [pl.BlockSpec((tm, tr), lambda i, r: (i, r))])[0]


def mm_nt_red(name, a, w, tr, tk, out_dtype=F32):
    M, C = a.shape
    R = w.shape[0]
    tm = _row_tile(M)
    nk = C // tk
    grid = (M // tm, R // tr, nk)
    return _matmul(name, [(a, w)], [(pl.BlockSpec((tm, tk), lambda i, r, k: (i, k)), pl.BlockSpec((tr, tk), lambda i, r, k: (r, k)))],
                   NT, grid, [_sds((M, R), out_dtype)], [pl.BlockSpec((tm, tr), lambda i, r, k: (i, r))],
                   nred=nk, acc_shape=(tm, tr))[0]


def mm_nn_red(name, a, b, tn, tk, out_dtype=F32, residual=None, rows=None, prev=None, tm=None):
    M, K = a.shape
    N = b.shape[1]
    tm = min(tm or _row_tile(M), M)
    nk = K // tk
    i0, ni = (0, M // tm) if rows is None else rows
    grid = (ni, N // tn, nk)
    ospec = pl.BlockSpec((tm, tn), lambda i, n, k: (i + i0, n))
    extra, especs, epi = [], [], None
    if residual is not None:
        extra, especs, epi = [residual], [ospec], (lambda v, r, *_: (v + r,))
    alias = {}
    if prev is not None:
        alias = {2 + len(extra): 0}
        extra, especs = extra + [prev], especs + [_ANY]
        epi = epi or (lambda v, *_: (v,))
    return _matmul(name, [(a, b)], [(pl.BlockSpec((tm, tk), lambda i, n, k: (i + i0, k)), pl.BlockSpec((tk, tn), lambda i, n, k: (k, n)))],
                   NN, grid, [_sds((M, N), out_dtype)], [ospec], nred=nk, acc_shape=(tm, tn),
                   extra=extra, extra_specs=especs, epilogue=epi, alias=alias)[0]


def mm_tn(name, x, dy, tr, tc, out_dtype=BF16):
    M, R = x.shape
    C = dy.shape[1]
    grid = (R // tr, C // tc)
    return _matmul(name, [(x, dy)], [(pl.BlockSpec((M, tr), lambda r, c: (0, r)), pl.BlockSpec((M, tc), lambda r, c: (0, c)))],
                   TN, grid, [_sds((R, C), out_dtype)], [pl.BlockSpec((tr, tc), lambda r, c: (r, c))])[0]


def mm_tn_colblk(name, x, dy, nb, out_dtype=BF16):
    M, R = x.shape
    J = dy.shape[1] // nb
    grid = (J,)
    return _matmul(name, [(x, dy)], [(pl.BlockSpec((M, R), lambda j: (0, 0)), pl.BlockSpec((M, nb), lambda j: (0, j)))],
                   TN, grid, [_sds((J, R, nb), out_dtype)], [pl.BlockSpec((None, R, nb), lambda j: (j, 0, 0))])[0]


def _rows(T):
    return min(256, T)


def rms_fwd(name, x, g, eps=NORM_EPS):
    T, D = x.shape
    tm = _rows(T)

    def body(x_ref, g_ref, o_ref):
        xv = x_ref[...]
        r = lax.rsqrt(jnp.mean(xv * xv, axis=-1, keepdims=True) + eps)
        o_ref[...] = (xv * r * g_ref[...]).astype(BF16)

    return _pcall(body, name=name, grid=(T // tm,),
                  in_specs=[pl.BlockSpec((tm, D), lambda i: (i, 0)), pl.BlockSpec((1, D), lambda i: (0, 0))],
                  out_specs=pl.BlockSpec((tm, D), lambda i: (i, 0)), out_shape=_sds((T, D), BF16),
                  compiler_params=_cparams(("arbitrary",)))(x, g)


def rms_bwd(name, x, g, dy, dres, eps=NORM_EPS):
    T, D = x.shape
    tm = _rows(T)

    def body(x_ref, g_ref, dy_ref, dr_ref, dx_ref, dxb_ref, dg_ref):
        i = pl.program_id(0)
        xv = x_ref[...]
        r = lax.rsqrt(jnp.mean(xv * xv, axis=-1, keepdims=True) + eps)
        xh = xv * r
        dyv = dy_ref[...]
        gd = dyv * g_ref[...]
        dx = r * (gd - xh * jnp.mean(gd * xh, axis=-1, keepdims=True)) + dr_ref[...]
        dx_ref[...] = dx
        dxb_ref[...] = dx.astype(BF16)

        @pl.when(i == 0)
        def _():
            dg_ref[...] = jnp.zeros_like(dg_ref)

        dg_ref[...] += jnp.sum(dyv * xh, axis=0, keepdims=True)

    row = pl.BlockSpec((tm, D), lambda i: (i, 0))
    vec = pl.BlockSpec((1, D), lambda i: (0, 0))
    return _pcall(body, name=name, grid=(T // tm,), in_specs=[row, vec, row, row], out_specs=[row, row, vec],
                  out_shape=[_sds((T, D), F32), _sds((T, D), BF16), _sds((1, D), F32)],
                  compiler_params=_cparams(("arbitrary",)))(x, g, dy, dres)


def gnorm_fwd(y, projp, g):
    T, D = y.shape
    tm = _rows(T)

    def body(y_ref, z_ref, g_ref, o_ref):
        yz = y_ref[...] * _silu(z_ref[...])
        r = lax.rsqrt(jnp.mean(yz * yz, axis=-1, keepdims=True) + SSM_NORM_EPS)
        o_ref[...] = (yz * r * g_ref[...]).astype(BF16)

    row = pl.BlockSpec((tm, D), lambda i: (i, 0))
    return _pcall(body, name="gnorm_fwd", grid=(T // tm,),
                  in_specs=[row, pl.BlockSpec((tm, D), lambda i: (i, OFF_Z // D)), pl.BlockSpec((1, D), lambda i: (0, 0))],
                  out_specs=row, out_shape=_sds((T, D), BF16), compiler_params=_cparams(("arbitrary",)))(y, projp, g)


def gnorm_bwd(y, projp, g, dyn):
    T, D = y.shape
    tm = _rows(T)

    def body(y_ref, z_ref, g_ref, dyn_ref, dy_ref, dz_ref, dg_ref):
        i = pl.program_id(0)
        yv, zv = y_ref[...], z_ref[...]
        sz = _silu(zv)
        yz = yv * sz
        r = lax.rsqrt(jnp.mean(yz * yz, axis=-1, keepdims=True) + SSM_NORM_EPS)
        xh = yz * r
        dv = dyn_ref[...]
        gd = dv * g_ref[...]
        dyz = r * (gd - xh * jnp.mean(gd * xh, axis=-1, keepdims=True))
        dy_ref[...] = dyz * sz
        dz_ref[...] = (dyz * yv * _dsilu(zv)).astype(BF16)

        @pl.when(i == 0)
        def _():
            dg_ref[...] = jnp.zeros_like(dg_ref)

        dg_ref[...] += jnp.sum(dv * xh, axis=0, keepdims=True)

    row = pl.BlockSpec((tm, D), lambda i: (i, 0))
    vec = pl.BlockSpec((1, D), lambda i: (0, 0))
    return _pcall(body, name="gnorm_bwd", grid=(T // tm,),
                  in_specs=[row, pl.BlockSpec((tm, D), lambda i: (i, OFF_Z // D)), vec, row], out_specs=[row, row, vec],
                  out_shape=[_sds((T, D), F32), _sds((T, D), BF16), _sds((1, D), F32)],
                  compiler_params=_cparams(("arbitrary",)))(y, projp, g, dyn)


def merge_fwd(projp, out_a, out_s):
    T, D = out_a.shape
    tm = _rows(T)

    def body(ga_ref, gs_ref, a_ref, s_ref, o_ref):
        o_ref[...] = (_sigmoid(ga_ref[...]) * a_ref[...] + _sigmoid(gs_ref[...]) * s_ref[...]).astype(BF16)

    row = pl.BlockSpec((tm, D), lambda i: (i, 0))
    return _pcall(body, name="merge_fwd", grid=(T // tm,),
                  in_specs=[pl.BlockSpec((tm, D), lambda i: (i, OFF_GA // D)), pl.BlockSpec((tm, D), lambda i: (i, OFF_GS // D)), row, row],
                  out_specs=row, out_shape=_sds((T, D), BF16), compiler_params=_cparams(("arbitrary",)))(projp, projp, out_a, out_s)


def merge_bwd(projp, out_a, out_s, dmerged):
    T, D = out_a.shape
    tm = _rows(T)

    def body(ga_ref, gs_ref, a_ref, s_ref, dm_ref, da_ref, ds_ref, dga_ref, dgs_ref):
        dm = dm_ref[...]
        sa, ss = _sigmoid(ga_ref[...]), _sigmoid(gs_ref[...])
        da_ref[...] = (dm * sa).astype(BF16)
        ds_ref[...] = (dm * ss).astype(BF16)
        dga_ref[...] = (dm * a_ref[...] * sa * (1.0 - sa)).astype(BF16)
        dgs_ref[...] = (dm * s_ref[...] * ss * (1.0 - ss)).astype(BF16)

    row = pl.BlockSpec((tm, D), lambda i: (i, 0))
    return _pcall(body, name="merge_bwd", grid=(T // tm,),
                  in_specs=[pl.BlockSpec((tm, D), lambda i: (i, OFF_GA // D)), pl.BlockSpec((tm, D), lambda i: (i, OFF_GS // D)), row, row, row],
                  out_specs=[row] * 4, out_shape=[_sds((T, D), BF16)] * 4,
                  compiler_params=_cparams(("arbitrary",)))(projp, projp, out_a, out_s, dmerged)


def head_fwd_bwd(h2, pg, pp, g_final, target):
    T, D = h2.shape
    tm = _rows(T)

    def body(h_ref, pg_ref, pp_ref, g_ref, t_ref, loss_ref, dh_ref, dpg_ref, dpp_ref, dg_ref):
        i = pl.program_id(0)
        s = _sigmoid(pg_ref[...])
        ppv = pp_ref[...]
        h3 = h_ref[...] + s * ppv
        r = lax.rsqrt(jnp.mean(h3 * h3, axis=-1, keepdims=True) + NORM_EPS)
        xh = h3 * r
        gv = g_ref[...]
        e = xh * gv - t_ref[...]
        dyo = e * (1.0 / D)
        gd = dyo * gv
        dh = r * (gd - xh * jnp.mean(gd * xh, axis=-1, keepdims=True))
        dh_ref[...] = dh
        dpg_ref[...] = (dh * ppv * s * (1.0 - s)).astype(BF16)
        dpp_ref[...] = (dh * s).astype(BF16)

        @pl.when(i == 0)
        def _():
            dg_ref[...] = jnp.zeros_like(dg_ref)
            loss_ref[...] = jnp.zeros_like(loss_ref)

        dg_ref[...] += jnp.sum(dyo * xh, axis=0, keepdims=True)
        part = 0.5 * jnp.sum(jnp.mean(e * e, axis=-1, keepdims=True), axis=0, keepdims=True)
        loss_ref[...] += jnp.broadcast_to(part, loss_ref.shape)

    row = pl.BlockSpec((tm, D), lambda i: (i, 0))
    vec = pl.BlockSpec((1, D), lambda i: (0, 0))
    return _pcall(body, name="head_fwd_bwd", grid=(T // tm,), in_specs=[row, row, row, vec, row],
                  out_specs=[pl.BlockSpec((1, 128), lambda i: (0, 0)), row, row, row, vec],
                  out_shape=[_sds((1, 128), F32), _sds((T, D), F32), _sds((T, D), BF16), _sds((T, D), BF16), _sds((1, D), F32)],
                  compiler_params=_cparams(("arbitrary",)))(h2, pg, pp, g_final, target)


FFN_TILE = 512


def ffn_up(f, wgt, wut):
    T, D = f.shape
    H = wgt.shape[0]
    tm = _row_tile(T)

    def body(f_ref, wg_ref, wu_ref, g_ref, u_ref, a_ref):
        fv = f_ref[...]
        g = _dot(fv, wg_ref[...], NT)
        u = _dot(fv, wu_ref[...], NT)
        g_ref[...] = g
        u_ref[...] = u
        a_ref[...] = (_silu(g) * u).astype(BF16)

    wspec = pl.BlockSpec((FFN_TILE, D), lambda i, j: (j, 0))
    ospec = pl.BlockSpec((tm, FFN_TILE), lambda i, j: (i, j))
    return _pcall(body, name="ffn_up", grid=(T // tm, H // FFN_TILE), in_specs=[pl.BlockSpec((tm, D), lambda i, j: (i, 0)), wspec, wspec],
                  out_specs=[ospec] * 3, out_shape=[_sds((T, H), F32), _sds((T, H), F32), _sds((T, H), BF16)],
                  compiler_params=_cparams(("arbitrary", "arbitrary")))(f, wgt, wut)


def ffn_down_bwd(dh2b, wd, gate, up):
    T, D = dh2b.shape
    H = wd.shape[0]
    tm = _row_tile(T)
    ospec = pl.BlockSpec((tm, FFN_TILE), lambda i, j: (i, j))

    def epi(da, g, u):
        return (da * u * _dsilu(g), da * _silu(g))

    return _matmul("ffn_down_bwd", [(dh2b, wd)],
                   [(pl.BlockSpec((tm, D), lambda i, j: (i, 0)), pl.BlockSpec((FFN_TILE, D), lambda i, j: (j, 0)))],
                   NT, (T // tm, H // FFN_TILE), [_sds((T, H), BF16)] * 2, [ospec, ospec],
                   extra=(gate, up), extra_specs=(ospec, ospec), epilogue=epi)


def ffn_up_bwd(dgate, dup, wgt, wut):
    T, H = dgate.shape
    D = wgt.shape[1]
    tm = min(512, T)
    tn = 512
    aspec = pl.BlockSpec((tm, H), lambda i, n: (i, 0))
    wspec = pl.BlockSpec((H, tn), lambda i, n: (0, n))
    return _matmul("ffn_up_bwd", [(dgate, wgt), (dup, wut)], [(aspec, wspec), (aspec, wspec)], NN, (T // tm, D // tn),
                   [_sds((T, D), F32)], [pl.BlockSpec((tm, tn), lambda i, n: (i, n))])[0]


def attn_br_bwd(dout_a, wab):
    T, D = dout_a.shape
    J, R, nb = wab.shape
    tm = _row_tile(T)
    return _matmul("attn_br_bwd", [(dout_a, wab)],
                   [(pl.BlockSpec((tm, nb), lambda i, j: (i, j)), pl.BlockSpec((None, R, nb), lambda i, j: (j, 0, 0)))],
                   NT, (T // tm, J), [_sds((T, R), BF16)], [pl.BlockSpec((tm, R), lambda i, j: (i, 0))], nred=J, acc_shape=(tm, R))[0]


def _adam_math(w, g, m, v):
    m2 = ADAM_B1 * m + (1.0 - ADAM_B1) * g
    v2 = ADAM_B2 * v + (1.0 - ADAM_B2) * (g * g)
    m_hat = m2 / (1.0 - ADAM_B1 ** ADAM_STEP)
    v_hat = v2 / (1.0 - ADAM_B2 ** ADAM_STEP)
    delta = -ADAM_LR * (m_hat / (jnp.sqrt(v_hat) + ADAM_EPS) + ADAM_WD * w)
    return delta, m2, v2


def _sum_partials(own, parts):
    g = None if own is None else own.astype(F32)
    if parts is not None:
        for s in range(parts.shape[0]):
            t = parts[s].astype(F32)
            g = t if g is None else g + t
    return g


def adamw(name, parts, w, m, v, own=None, own_slot=None):
    R, C = w.shape
    tr, tc = R, C
    for cand in (256, 176, 128, 64, 32, 16, 8):
        if R % cand == 0 and R > cand:
            tr = cand
            break
    if tr == R and R > 256:
        tc = 256
    given = [a for a in (parts, own) if a is not None]
    pre = own_slot is not None

    def body(*refs):
        refs = refs[1:] if pre else refs
        p_ref = refs[0] if parts is not None else None
        o_ref = refs[len(given) - 1] if own is not None else None
        w_ref, m_ref, v_ref, g_ref, d_ref, m2_ref, v2_ref = refs[-7:]
        g = _sum_partials(None if o_ref is None else o_ref[...], p_ref)
        d, m2, v2 = _adam_math(w_ref[...], g, m_ref[...], v_ref[...])
        g_ref[...] = g
        d_ref[...] = d
        m2_ref[...] = m2
        v2_ref[...] = v2

    blk = pl.BlockSpec((tr, tc), lambda i, j, *s: (i, j))
    specs = [] if parts is None else [pl.BlockSpec((parts.shape[0], tr, tc), lambda i, j, *s: (0, i, j))]
    if own is not None:
        specs.append(pl.BlockSpec((None, tr, tc), lambda i, j, s: (s[0], i, j)) if pre else blk)
    specs += [blk] * 3
    grid = (R // tr, C // tc)
    out_shape = [_sds((R, C), F32)] * 4
    params = _cparams(("arbitrary", "arbitrary"))
    if not pre:
        return _pcall(body, name=name, grid=grid, in_specs=specs, out_specs=[blk] * 4, out_shape=out_shape,
                      compiler_params=params)(*given, w, m, v)
    spec = pltpu.PrefetchScalarGridSpec(num_scalar_prefetch=1, grid=grid, in_specs=specs, out_specs=[blk] * 4)
    return _pcall(body, name=name, grid_spec=spec, out_shape=out_shape,
                  compiler_params=params)(jnp.asarray(own_slot, jnp.int32).reshape(1), *given, w, m, v)


_HBM = pl.BlockSpec(memory_space=pltpu.HBM)
_SEM = pl.BlockSpec(memory_space=pltpu.SEMAPHORE)
_ANY = pl.BlockSpec(memory_space=pl.ANY)
_SPLIT_PARAMS = dict(compiler_params=pltpu.CompilerParams(has_side_effects=pltpu.SideEffectType.DATAFLOW_SIDE_EFFECTING))
ICI_SAME_CORE = (2, 4, 6)
ALL_PEERS = (1, 2, 3, 4, 5, 6, 7)
LAND_SLOTS = {"gather": N_DEV, "scatter": N_DEV - 1, "pair": 4, "scatter_core": 3}


def _mesh_pos():
    x, y, c = lax.axis_index("x"), lax.axis_index("y"), lax.axis_index("c")
    return x, y, c, 4 * x + 2 * y + c


def _peer_of(k, x, y, c):
    px = 1 - x if k & 4 else x
    py = 1 - y if k & 2 else y
    pc = 1 - c if k & 1 else c
    return (px, py, pc), 4 * px + 2 * py + pc


def _split_copies(mode, ks, srcs, lands, send_sems, recv_sems):
    x, y, c, me = _mesh_pos()
    pairs = []
    for a in range(len(lands)):
        for j, k in enumerate(ks):
            dev, peer = _peer_of(k if mode != "route" else k[0], x, y, c)
            i = a * len(ks) + j
            if mode == "gather":
                s_out, d_out, d_in = lands[a].at[me], lands[a].at[me], lands[a].at[peer]
            elif mode == "scatter":
                s_out, d_out, d_in = srcs[a].at[peer], lands[a].at[k - 1], lands[a].at[k - 1]
            elif mode == "pair":
                dev, _ = _peer_of(1, x, y, c)
                _, theirs = _peer_of(k | 1, x, y, c)
                s_out, d_out, d_in = srcs[a].at[theirs], lands[a].at[j], lands[a].at[j]
            elif mode == "scatter_core":
                s_out, d_out, d_in = srcs[a].at[j + 1], lands[a].at[j], lands[a].at[j]
            elif mode == "route":
                k, rel, half = k
                dev, _ = _peer_of(k, x, y, c)
                _, held = _peer_of(rel, x, y, c)
                _, theirs = _peer_of(k ^ rel, x, y, c)
                cols = lands[a].shape[-1] // 2
                cut = (slice(None), slice(None)) if half is None else (slice(None), pl.ds(half * cols, cols))
                s_out, d_out, d_in = lands[a].at[held].at[cut], lands[a].at[held].at[cut], lands[a].at[theirs].at[cut]
            else:
                dev, _ = _peer_of(1, x, y, c)
                _, theirs = _peer_of(k | 1, x, y, c)
                s_out, d_out, d_in = lands[a].at[peer], lands[a].at[peer], lands[a].at[theirs]
            both = [pltpu.make_async_remote_copy(src_ref=s_out, dst_ref=d, send_sem=send_sems.at[i], recv_sem=recv_sems.at[i],
                                                 device_id=dev, device_id_type=pl.DeviceIdType.MESH) for d in (d_out, d_in)]
            pairs.append(tuple(both))
    return pairs


def split_start(name, mode, ks, srcs, lands=None, after=None):
    n, nk = len(srcs) if lands is None else len(lands), len(ks)
    srcs = [pltpu.with_memory_space_constraint(s, pltpu.HBM) for s in srcs]
    if lands is None:
        shapes = [((N_DEV,) + s.shape) if mode == "gather" else ((LAND_SLOTS[mode],) + s.shape[1:]) for s in srcs]
        lands = [lax.empty(shp, s.dtype) for shp, s in zip(shapes, srcs)]
    lands = [pltpu.with_memory_space_constraint(l, pltpu.HBM) for l in lands]
    both = srcs + lands
    extra = [] if after is None else [after]

    def body(*refs):
        src_refs, land_refs = refs[:len(srcs)], refs[len(srcs):len(both)]
        send_sems, recv_sems = refs[len(both) + len(extra)], refs[len(both) + len(extra) + 1]
        token = refs[-1]
        for out, _ in _split_copies(mode, ks, src_refs, land_refs, send_sems, recv_sems):
            out.start()
        token[...] = jnp.zeros_like(token)

    out_shape = (pltpu.SemaphoreType.DMA((n * nk,)), pltpu.SemaphoreType.DMA((n * nk,)),
                 *[pltpu.HBM(a.shape, a.dtype) for a in both], _sds((8, 128), F32))
    res = _raw_call(body, name=name, out_shape=out_shape, in_specs=[_HBM] * len(both) + [_ANY] * len(extra),
                    out_specs=(_SEM, _SEM, *[_HBM] * len(both), pl.BlockSpec(memory_space=pltpu.VMEM)),
                    input_output_aliases={i: 2 + i for i in range(len(both))}, **_SPLIT_PARAMS)(*both, *extra)
    _PENDING.append(res[-1])
    return dict(mode=mode, ks=ks, sems=(res[0], res[1]), srcs=list(res[2:2 + len(srcs)]),
                lands=list(res[2 + len(srcs):2 + len(both)]), token=res[-1])


def split_wait(name, h, after):
    ns = len(h["srcs"])
    both = h["srcs"] + h["lands"]
    after = list(after) if isinstance(after, (list, tuple)) else [after]

    def body(*refs):
        src_refs, land_refs = refs[:ns], refs[ns:len(both)]
        send_sems, recv_sems = refs[len(both)], refs[len(both) + 1]
        for out, arriving in _split_copies(h["mode"], h["ks"], src_refs, land_refs, send_sems, recv_sems):
            out.wait_send()
            arriving.wait_recv()

    res = _raw_call(body, name=name, out_shape=tuple(pltpu.HBM(a.shape, a.dtype) for a in both),
                    in_specs=[_HBM] * len(both) + [_SEM, _SEM] + [_ANY] * len(after), out_specs=tuple([_HBM] * len(both)),
                    input_output_aliases={i: i for i in range(len(both))}, **_SPLIT_PARAMS)(*both, *h["sems"], *after)
    return list(res[:ns]), list(res[ns:])


FORWARD_BLOCKS = (0, 2, 4, 6)


def pair_sum(name, mine, slots, theirs):
    P, R, C = theirs.shape
    tc = 512

    def body(s_ref, a_ref, b_ref, o_ref):
        o_ref[...] = (a_ref[...].astype(F32) + b_ref[...].astype(F32)).astype(o_ref.dtype)

    blk = pl.BlockSpec((None, R, tc), lambda p, i, s: (p, 0, i))
    spec = pltpu.PrefetchScalarGridSpec(num_scalar_prefetch=1, grid=(P, C // tc),
                                        in_specs=[pl.BlockSpec((None, R, tc), lambda p, i, s: (s[p], 0, i)), blk], out_specs=blk)
    return _pcall(body, name=name, grid_spec=spec, out_shape=_sds((P, R, C), theirs.dtype),
                  compiler_params=_cparams(("arbitrary", "arbitrary")))(slots, mine, theirs)


def _rope_parts(pos_ref, inv_ref):
    ang = pos_ref[...] * inv_ref[...]
    return jnp.cos(ang), jnp.sin(ang)


def _rot_half(t):
    lane = lax.broadcasted_iota(jnp.int32, t.shape, 1)
    return jnp.where((lane % HEAD_DIM) < HEAD_DIM // 2, -pltpu.roll(t, 128 - HEAD_DIM // 2, 1), pltpu.roll(t, HEAD_DIM // 2, 1))


def _attn_mask(n):
    row = lax.broadcasted_iota(jnp.int32, (BLK, 2 * BLK), 0)
    col = lax.broadcasted_iota(jnp.int32, (BLK, 2 * BLK), 1)
    dist = row + BLK - col
    return (dist >= 0) & (dist < BLK) & ((n * BLK - BLK + col) >= 0)


def _attn_specs(T):
    prev = lambda n: jnp.maximum(n - 1, 0)
    kc = pl.BlockSpec((BLK, KV_DIM), lambda n: (n, OFF_K // KV_DIM))
    kp = pl.BlockSpec((BLK, KV_DIM), lambda n: (prev(n), OFF_K // KV_DIM))
    vc = pl.BlockSpec((BLK, KV_DIM), lambda n: (n, OFF_V // KV_DIM))
    vp = pl.BlockSpec((BLK, KV_DIM), lambda n: (prev(n), OFF_V // KV_DIM))
    pc = pl.BlockSpec((BLK, 1), lambda n: (n, 0))
    pp = pl.BlockSpec((BLK, 1), lambda n: (prev(n), 0))
    inv = pl.BlockSpec((1, 128), lambda n: (0, 0))
    sink = pl.BlockSpec(memory_space=pltpu.SMEM)
    return kc, kp, vc, vp, pc, pp, inv, sink


def _softmax_sink(sc, valid, sink):
    sc = jnp.where(valid, sc * (HEAD_DIM ** -0.5), -1e30)
    m = jnp.maximum(jnp.max(sc, axis=1, keepdims=True), sink)
    e = jnp.exp(sc - m)
    es = jnp.exp(sink - m)
    den = jnp.sum(e, axis=1, keepdims=True) + es
    return e / den, es / den


def attn_fwd(projp, posf, inv128, sinks):
    T = projp.shape[0]
    kc, kp, vc, vp, pc, pp, inv, sink = _attn_specs(T)

    def body(q_ref, kc_ref, kp_ref, vc_ref, vp_ref, pc_ref, pp_ref, inv_ref, sink_ref, o_ref, qr_ref, kr_ref):
        n = pl.program_id(0)
        cos_c, sin_c = _rope_parts(pc_ref, inv_ref)
        cos_p, sin_p = _rope_parts(pp_ref, inv_ref)
        valid = _attn_mask(n)
        k_c, k_p = [], []
        for s in range(KV_DIM // 128):
            t = kc_ref[:, 128 * s:128 * (s + 1)]
            k_c.append((t * cos_c + _rot_half(t) * sin_c).astype(BF16))
            kr_ref[:, 128 * s:128 * (s + 1)] = k_c[s]
            t = kp_ref[:, 128 * s:128 * (s + 1)]
            k_p.append((t * cos_p + _rot_half(t) * sin_p).astype(BF16))
        kcat, vcat = [], []
        for hk in range(KV_HEADS):
            lo = HEAD_DIM * (hk % 2)
            kcat.append(jnp.concatenate([k_p[hk // 2][:, lo:lo + HEAD_DIM], k_c[hk // 2][:, lo:lo + HEAD_DIM]], axis=0))
            vcat.append(jnp.concatenate([vp_ref[:, HEAD_DIM * hk:HEAD_DIM * (hk + 1)], vc_ref[:, HEAD_DIM * hk:HEAD_DIM * (hk + 1)]], axis=0)
                        .astype(BF16))
        q_heads = []
        for s in range(Q_DIM // 128):
            t = q_ref[:, 128 * s:128 * (s + 1)]
            qs = (t * cos_c + _rot_half(t) * sin_c).astype(BF16)
            qr_ref[:, 128 * s:128 * (s + 1)] = qs
            q_heads += [qs[:, :HEAD_DIM], qs[:, HEAD_DIM:]]
        G = ATTN_HEADS // KV_HEADS
        scores = [_dot(q_heads[hq], kcat[hq // G], NT) for hq in range(ATTN_HEADS)]
        probs = [_softmax_sink(scores[hq], valid, sink_ref[0, hq])[0] for hq in range(ATTN_HEADS)]
        outs = [_dot(probs[hq], vcat[hq // G], NN) for hq in range(ATTN_HEADS)]
        for s in range(Q_DIM // 128):
            o_ref[:, 128 * s:128 * (s + 1)] = jnp.concatenate([outs[2 * s], outs[2 * s + 1]], axis=1).astype(BF16)

    qspec = pl.BlockSpec((BLK, Q_DIM), lambda n: (n, OFF_Q // Q_DIM))
    orow = pl.BlockSpec((BLK, Q_DIM), lambda n: (n, 0))
    krow = pl.BlockSpec((BLK, KV_DIM), lambda n: (n, 0))
    return _pcall(body, name="attn_fwd", grid=(T // BLK,), in_specs=[qspec, kc, kp, vc, vp, pc, pp, inv, sink],
                  out_specs=[orow, orow, krow], out_shape=[_sds((T, Q_DIM), BF16), _sds((T, Q_DIM), BF16), _sds((T, KV_DIM), BF16)],
                  compiler_params=_cparams(("arbitrary",)))(projp, projp, projp, projp, projp, posf, posf, inv128, sinks)


def attn_bwd(qr, kr, projp, dattn, posf, inv128, sinks):
    T = projp.shape[0]
    _, _, vc, vp, pc, pp, inv, sink = _attn_specs(T)
    G = ATTN_HEADS // KV_HEADS

    def body(qr_ref, krc_ref, krp_ref, vc_ref, vp_ref, do_ref, pc_ref, pp_ref, inv_ref, sink_ref, dq_ref, dk_ref, dv_ref, dsk_ref):
        n = pl.program_id(0)

        @pl.when(n == 0)
        def _():
            dk_ref[...] = jnp.zeros_like(dk_ref)
            dv_ref[...] = jnp.zeros_like(dv_ref)
            dsk_ref[...] = jnp.zeros_like(dsk_ref)

        cos_c, sin_c = _rope_parts(pc_ref, inv_ref)
        cos_p, sin_p = _rope_parts(pp_ref, inv_ref)
        valid = _attn_mask(n)
        lane = lax.broadcasted_iota(jnp.int32, (1, 128), 1)
        kcat, vcat = [], []
        for hk in range(KV_HEADS):
            ksl = slice(HEAD_DIM * hk, HEAD_DIM * (hk + 1))
            kcat.append(jnp.concatenate([krp_ref[:, ksl], krc_ref[:, ksl]], axis=0))
            vcat.append(jnp.concatenate([vp_ref[:, ksl], vc_ref[:, ksl]], axis=0).astype(BF16))
        H = range(ATTN_HEADS)
        q_heads = [qr_ref[:, HEAD_DIM * hq:HEAD_DIM * (hq + 1)] for hq in H]
        do_heads = [do_ref[:, HEAD_DIM * hq:HEAD_DIM * (hq + 1)] for hq in H]
        soft = [_softmax_sink(_dot(q_heads[hq], kcat[hq // G], NT), valid, sink_ref[0, hq]) for hq in H]
        dps = [_dot(do_heads[hq], vcat[hq // G], NT) for hq in H]
        deltas = [jnp.sum(soft[hq][0] * dps[hq], axis=1, keepdims=True) for hq in H]
        dss = [(soft[hq][0] * (dps[hq] - deltas[hq]) * (HEAD_DIM ** -0.5)).astype(BF16) for hq in H]
        pbs = [soft[hq][0].astype(BF16) for hq in H]
        dsk = jnp.zeros((1, 128), F32)
        for hq in H:
            dsk = dsk + jnp.where(lane == hq, -jnp.sum(soft[hq][1] * deltas[hq], axis=0, keepdims=True), 0.0)
        dsk_ref[...] += dsk
        dq_heads = [_dot(dss[hq], kcat[hq // G], NN) for hq in H]
        dk_parts = [_dot(dss[hq], q_heads[hq], TN) for hq in H]
        dv_parts = [_dot(pbs[hq], do_heads[hq], TN) for hq in H]
        dk_heads = [sum(dk_parts[G * hk + 1:G * (hk + 1)], dk_parts[G * hk]) for hk in range(KV_HEADS)]
        dv_heads = [sum(dv_parts[G * hk + 1:G * (hk + 1)], dv_parts[G * hk]) for hk in range(KV_HEADS)]
        for s in range(Q_DIM // 128):
            t = jnp.concatenate([dq_heads[2 * s], dq_heads[2 * s + 1]], axis=1)
            dq_ref[:, 128 * s:128 * (s + 1)] = (t * cos_c - _rot_half(t) * sin_c).astype(BF16)
        cur = pl.ds(pl.multiple_of(n * BLK, BLK), BLK)
        prv = pl.ds(pl.multiple_of(jnp.maximum(n - 1, 0) * BLK, BLK), BLK)
        for s in range(KV_DIM // 128):
            tc = jnp.concatenate([dk_heads[2 * s][BLK:], dk_heads[2 * s + 1][BLK:]], axis=1)
            tp = jnp.concatenate([dk_heads[2 * s][:BLK], dk_heads[2 * s + 1][:BLK]], axis=1)
            cols = slice(128 * s, 128 * (s + 1))
            dk_ref[cur, cols] += tc * cos_c - _rot_half(tc) * sin_c
            dk_ref[prv, cols] += tp * cos_p - _rot_half(tp) * sin_p
            dv_ref[cur, cols] += jnp.concatenate([dv_heads[2 * s][BLK:], dv_heads[2 * s + 1][BLK:]], axis=1)
            dv_ref[prv, cols] += jnp.concatenate([dv_heads[2 * s][:BLK], dv_heads[2 * s + 1][:BLK]], axis=1)

    qrow = pl.BlockSpec((BLK, Q_DIM), lambda n: (n, 0))
    krc = pl.BlockSpec((BLK, KV_DIM), lambda n: (n, 0))
    krp = pl.BlockSpec((BLK, KV_DIM), lambda n: (jnp.maximum(n - 1, 0), 0))
    whole = pl.BlockSpec((T, KV_DIM), lambda n: (0, 0))
    return _pcall(body, name="attn_bwd", grid=(T // BLK,), in_specs=[qrow, krc, krp, vc, vp, qrow, pc, pp, inv, sink],
                  out_specs=[qrow, whole, whole, pl.BlockSpec((1, 128), lambda n: (0, 0))],
                  out_shape=[_sds((T, Q_DIM), BF16), _sds((T, KV_DIM), F32), _sds((T, KV_DIM), F32), _sds((1, 128), F32)],
                  compiler_params=_cparams(("arbitrary",)))(qr, kr, kr, projp, projp, dattn, posf, posf, inv128, sinks)


CONV_CB = 256


def _shift_down(x, s):
    row = lax.broadcasted_iota(jnp.int32, x.shape, 0)
    return jnp.where(row >= s, pltpu.roll(x, s, 0), 0.0)


def _shift_up(x, s):
    T = x.shape[0]
    row = lax.broadcasted_iota(jnp.int32, x.shape, 0)
    return jnp.where(row < T - s, pltpu.roll(x, T - s, 0), 0.0)


def _conv_pre(x, w_ref, b_ref):
    acc = x * w_ref[CONV_WIDTH - 1:CONV_WIDTH, :] + b_ref[...]
    for s in range(1, CONV_WIDTH):
        acc = acc + _shift_down(x, s) * w_ref[CONV_WIDTH - 1 - s:CONV_WIDTH - s, :]
    return acc


def conv_fwd(projp, conv_w, conv_b):
    T = projp.shape[0]

    def body(x_ref, w_ref, b_ref, o_ref):
        o_ref[...] = _silu(_conv_pre(x_ref[...], w_ref, b_ref))

    return _pcall(body, name="conv_fwd", grid=(CONV_DIM // CONV_CB,),
                  in_specs=[pl.BlockSpec((T, CONV_CB), lambda c: (0, OFF_XBC // CONV_CB + c)),
                            pl.BlockSpec((CONV_WIDTH, CONV_CB), lambda c: (0, c)), pl.BlockSpec((1, CONV_CB), lambda c: (0, c))],
                  out_specs=pl.BlockSpec((T, CONV_CB), lambda c: (0, c)), out_shape=_sds((T, CONV_DIM), F32),
                  compiler_params=_cparams(("arbitrary",)))(projp, conv_w, conv_b)


def conv_bwd(name, projp, dact, conv_w, conv_b, col0):
    T, C = dact.shape
    c0 = col0 // CONV_CB

    def body(x_ref, da_ref, w_ref, b_ref, dx_ref, dw_ref, db_ref):
        x = x_ref[...]
        dpre = da_ref[...] * _dsilu(_conv_pre(x, w_ref, b_ref))
        dx = dpre * w_ref[CONV_WIDTH - 1:CONV_WIDTH, :]
        dw_ref[CONV_WIDTH - 1:CONV_WIDTH, :] = jnp.sum(dpre * x, axis=0, keepdims=True)
        for s in range(1, CONV_WIDTH):
            i = CONV_WIDTH - 1 - s
            dx = dx + _shift_up(dpre, s) * w_ref[i:i + 1, :]
            dw_ref[i:i + 1, :] = jnp.sum(dpre * _shift_down(x, s), axis=0, keepdims=True)
        dx_ref[...] = dx.astype(BF16)
        db_ref[...] = jnp.sum(dpre, axis=0, keepdims=True)

    return _pcall(body, name=name, grid=(C // CONV_CB,),
                  in_specs=[pl.BlockSpec((T, CONV_CB), lambda c: (0, OFF_XBC // CONV_CB + c0 + c)),
                            pl.BlockSpec((T, CONV_CB), lambda c: (0, c)),
                            pl.BlockSpec((CONV_WIDTH, CONV_CB), lambda c: (0, c0 + c)), pl.BlockSpec((1, CONV_CB), lambda c: (0, c0 + c))],
                  out_specs=[pl.BlockSpec((T, CONV_CB), lambda c: (0, c)), pl.BlockSpec((CONV_WIDTH, CONV_CB), lambda c: (0, c)),
                             pl.BlockSpec((1, CONV_CB), lambda c: (0, c))],
                  out_shape=[_sds((T, C), BF16), _sds((CONV_WIDTH, C), F32), _sds((1, C), F32)],
                  compiler_params=_cparams(("arbitrary",)))(projp, dact, conv_w, conv_b)


def _softplus(x):
    return jnp.maximum(x, 0.0) + jnp.log1p(jnp.exp(-jnp.abs(x)))


def _tri(lower):
    r = lax.broadcasted_iota(jnp.int32, (BLK, BLK), 0)
    c = lax.broadcasted_iota(jnp.int32, (BLK, BLK), 1)
    return (r >= c) if lower else (c >= r)


def _ssd_chunk_setup(dt_ref, dtb_ref, alog_ref):
    raw = dt_ref[...] + dtb_ref[...]
    dt = _softplus(raw)
    aneg = -jnp.exp(alog_ref[...])
    a = dt * aneg
    acs = jnp.dot(_tri(True).astype(F32), a, precision=lax.Precision.HIGHEST, preferred_element_type=F32)
    return raw, dt, aneg, acs, acs.T


def _ssd_specs(T, rev):
    nc = T // BLK
    ci = (lambda c: nc - 1 - c) if rev else (lambda c: c)
    xs = pl.BlockSpec((BLK, D_INNER), lambda c: (ci(c), 0))
    bm = pl.BlockSpec((BLK, SSM_GROUPS * D_STATE), lambda c: (ci(c), D_INNER // (SSM_GROUPS * D_STATE)))
    cm = pl.BlockSpec((BLK, SSM_GROUPS * D_STATE), lambda c: (ci(c), D_INNER // (SSM_GROUPS * D_STATE) + 1))
    dt = pl.BlockSpec((BLK, DT_PAD), lambda c: (ci(c), OFF_DT // DT_PAD))
    v128 = pl.BlockSpec((1, 128), lambda c: (0, 0))
    dfull = pl.BlockSpec((1, D_INNER), lambda c: (0, 0))
    st = pl.BlockSpec((None, SSM_HEADS, HEAD_DIM, D_STATE), lambda c: (ci(c), 0, 0, 0))
    return xs, bm, cm, dt, v128, dfull, st, ci


GW = HEADS_PER_GROUP * HEAD_DIM


def _expanders():
    e = np.zeros((SSM_GROUPS, 128, GW), np.float32)
    for g in range(SSM_GROUPS):
        for hh in range(HEADS_PER_GROUP):
            e[g, HEADS_PER_GROUP * g + hh, HEAD_DIM * hh:HEAD_DIM * (hh + 1)] = 1.0
    return jnp.asarray(e, BF16), jnp.asarray(np.transpose(e, (0, 2, 1)).copy(), BF16)


def _split2(v):
    hi = lax.bitcast_convert_type(lax.bitcast_convert_type(v, jnp.uint32) & jnp.uint32(0xFFFF0000), F32)
    return hi.astype(BF16), (v - hi).astype(BF16)


def _dotx(a, b):
    if a.dtype == BF16:
        hi, lo = _split2(b)
        return jnp.dot(a, hi, preferred_element_type=F32) + jnp.dot(a, lo, preferred_element_type=F32)
    hi, lo = _split2(a)
    return jnp.dot(hi, b, preferred_element_type=F32) + jnp.dot(lo, b, preferred_element_type=F32)


def _decay(acs, acsT, h, tril):
    return jnp.where(tril, jnp.exp(jnp.where(tril, acs[:, h:h + 1] - acsT[h:h + 1, :], 0.0)), 0.0)


def ssd_fwd(xbc, projp, dtb, alog, dfull):
    T = xbc.shape[0]
    nc = T // BLK
    xs, bm, cm, dts, v128, dfs, st, _ = _ssd_specs(T, False)
    E, _ = _expanders()

    def body(xs_ref, b_ref, c_ref, dt_ref, dtb_ref, alog_ref, d_ref, e_ref, y_ref, st_ref, h_scr):
        c = pl.program_id(0)

        @pl.when(c == 0)
        def _():
            h_scr[...] = jnp.zeros_like(h_scr)

        _, dt, _, acs, acsT = _ssd_chunk_setup(dt_ref, dtb_ref, alog_ref)
        tril = _tri(True)
        alast = acs[BLK - 1:BLK, :]
        eacs = jnp.exp(acs)
        wmat = jnp.exp(alast - acs)
        gam = jnp.exp(alast)
        for g in range(SSM_GROUPS):
            gl = slice(GW * g, GW * (g + 1))
            hsl = slice(HEADS_PER_GROUP * g, HEADS_PER_GROUP * (g + 1))
            heads = [HEADS_PER_GROUP * g + hh for hh in range(HEADS_PER_GROUP)]
            Eg = e_ref[g]
            B = b_ref[:, D_STATE * g:D_STATE * (g + 1)].astype(BF16)
            C = c_ref[:, D_STATE * g:D_STATE * (g + 1)].astype(BF16)
            cb = _dot(C, B, NT)
            x_g = xs_ref[:, gl]
            xd_g = x_g * _dotx(dt, Eg)
            hold = h_scr[hsl]
            st_ref[hsl] = hold
            hcat = hold.reshape(GW, D_STATE)
            yoff = _dotx(eacs, Eg) * _dot(C, hcat, NT)
            S = _dot(xd_g * _dotx(wmat, Eg), B, TN)
            Ms = [cb * _decay(acs, acsT, h, tril) for h in heads]
            ys = [_dot(Ms[hh], xd_g[:, HEAD_DIM * hh:HEAD_DIM * (hh + 1)], NN) for hh in range(HEADS_PER_GROUP)]
            for hh, h in enumerate(heads):
                h_scr[h] = gam[:, h:h + 1] * hold[hh] + S[HEAD_DIM * hh:HEAD_DIM * (hh + 1)]
            y_ref[:, gl] = jnp.concatenate(ys, axis=1) + yoff + d_ref[:, gl] * x_g

    espec = pl.BlockSpec((SSM_GROUPS, 128, GW), lambda c: (0, 0, 0))
    return _pcall(body, name="ssd_fwd", grid=(nc,), in_specs=[xs, bm, cm, dts, v128, v128, dfs, espec],
                  out_specs=[xs, st], out_shape=[_sds((T, D_INNER), F32), _sds((nc, SSM_HEADS, HEAD_DIM, D_STATE), F32)],
                  scratch_shapes=[pltpu.VMEM((SSM_HEADS, HEAD_DIM, D_STATE), F32)],
                  compiler_params=_cparams(("arbitrary",)))(xbc, xbc, xbc, projp, dtb, alog, dfull, E)


def ssd_bwd(xbc, projp, dtb, alog, dfull, states, dy):
    T = xbc.shape[0]
    nc = T // BLK
    xs, bm, cm, dts, v128, dfs, st, ci = _ssd_specs(T, True)
    gn = SSM_GROUPS * D_STATE
    E, ET = _expanders()

    def body(xs_ref, b_ref, c_ref, dt_ref, dtb_ref, alog_ref, d_ref, st_ref, dy_ref, e_ref, et_ref,
             dxs_ref, dB_ref, dC_ref, ddt_ref, dal_ref, dD_ref, ddtb_ref, dh_scr):
        i = pl.program_id(0)

        @pl.when(i == 0)
        def _():
            dh_scr[...] = jnp.zeros_like(dh_scr)
            dal_ref[...] = jnp.zeros_like(dal_ref)
            dD_ref[...] = jnp.zeros_like(dD_ref)
            ddtb_ref[...] = jnp.zeros_like(ddtb_ref)

        raw, dt, aneg, acs, acsT = _ssd_chunk_setup(dt_ref, dtb_ref, alog_ref)
        tril = _tri(True)
        lane = lax.broadcasted_iota(jnp.int32, (BLK, 128), 1)
        sub = lax.broadcasted_iota(jnp.int32, (BLK, 128), 0)
        alast = acs[BLK - 1:BLK, :]
        eacs = jnp.exp(acs)
        wmat = jnp.exp(alast - acs)
        gam = jnp.exp(alast)
        gcol = jnp.exp(acsT[:, BLK - 1:BLK])
        ds_col = jnp.zeros((BLK, 128), F32)
        ds_row = jnp.zeros((BLK, 128), F32)
        ddt_col = jnp.zeros((BLK, 128), F32)
        dDm = jnp.zeros((BLK, 128), F32)
        hl = [slice(HEAD_DIM * hh, HEAD_DIM * (hh + 1)) for hh in range(HEADS_PER_GROUP)]
        for g in range(SSM_GROUPS):
            gl = slice(GW * g, GW * (g + 1))
            gs = slice(D_STATE * g, D_STATE * (g + 1))
            hsl = slice(HEADS_PER_GROUP * g, HEADS_PER_GROUP * (g + 1))
            heads = [HEADS_PER_GROUP * g + hh for hh in range(HEADS_PER_GROUP)]
            Eg, ETg = e_ref[g], et_ref[g]
            B = b_ref[:, gs].astype(BF16)
            C = c_ref[:, gs].astype(BF16)
            cb = _dot(C, B, NT)
            x_g, dy_g = xs_ref[:, gl], dy_ref[:, gl]
            dt_x, w_x = _dotx(dt, Eg), _dotx(wmat, Eg)
            xd_g = x_g * dt_x
            dye = dy_g * _dotx(eacs, Eg)
            hcat = st_ref[hsl].reshape(GW, D_STATE)
            dSv = dh_scr[hsl]
            dScat = dSv.reshape(GW, D_STATE)
            dDm = dDm + _dotx(dy_g * x_g, ETg)
            dH_y = _dot(dye, C, TN)
            dC_g = _dot(dye, hcat, NN)
            ds_col = ds_col + _dotx(dye * _dot(C, hcat, NT), ETg)
            dxdw = _dot(B, dScat, NT)
            dB_g = _dot(xd_g * w_x, dScat, NN)
            dww = _dotx(xd_g * dxdw, ETg) * wmat
            ds_col = ds_col - dww + jnp.where(sub == BLK - 1, jnp.sum(dww, axis=0, keepdims=True), 0.0)
            hd = jnp.sum(_dotx(Eg, dScat * hcat), axis=1, keepdims=True) * gcol
            ds_row = ds_row - jnp.where(lane == BLK - 1, hd, 0.0)
            decays = [_decay(acs, acsT, h, tril) for h in heads]
            Ms = [cb * d for d in decays]
            dMs = [_dot(dy_g[:, hl[hh]], xd_g[:, hl[hh]], NT) for hh in range(HEADS_PER_GROUP)]
            dxd1 = [_dot(Ms[hh], dy_g[:, hl[hh]], TN) for hh in range(HEADS_PER_GROUP)]
            dG = jnp.zeros((BLK, BLK), F32)
            for hh, h in enumerate(heads):
                Q = dMs[hh] * Ms[hh]
                ds_col = ds_col + jnp.where(lane == h, jnp.sum(Q, axis=1, keepdims=True), 0.0)
                ds_row = ds_row + jnp.where(sub == h, jnp.sum(Q, axis=0, keepdims=True), 0.0)
                dG = dG + dMs[hh] * decays[hh]
            dxd_g = jnp.concatenate(dxd1, axis=1) + w_x * dxdw
            dxs_ref[:, gl] = d_ref[:, gl] * dy_g + dxd_g * dt_x
            ddt_col = ddt_col + _dotx(dxd_g * x_g, ETg)
            dC_ref[:, gs] = dC_g + _dot(dG, B, NN)
            dB_ref[:, gs] = dB_g + _dot(dG, C, TN)
            for hh, h in enumerate(heads):
                dh_scr[h] = gam[:, h:h + 1] * dSv[hh] + dH_y[hl[hh]]
        ds_all = ds_col - ds_row.T
        da = jnp.dot(_tri(False).astype(F32), ds_all, precision=lax.Precision.HIGHEST, preferred_element_type=F32)
        ddt = ddt_col + da * aneg
        draw = jnp.where(lane < SSM_HEADS, ddt * _sigmoid(raw), 0.0)
        ddt_ref[...] = draw.astype(BF16)
        dal_ref[...] += jnp.sum(da * dt, axis=0, keepdims=True) * aneg
        ddtb_ref[...] += jnp.sum(draw, axis=0, keepdims=True)
        dD_ref[...] += jnp.sum(dDm, axis=0, keepdims=True)

    gblk = pl.BlockSpec((BLK, gn), lambda c: (ci(c), 0))
    espec = pl.BlockSpec((SSM_GROUPS, 128, GW), lambda c: (0, 0, 0))
    etspec = pl.BlockSpec((SSM_GROUPS, GW, 128), lambda c: (0, 0, 0))
    return _pcall(body, name="ssd_bwd", grid=(nc,), in_specs=[xs, bm, cm, dts, v128, v128, dfs, st, xs, espec, etspec],
                  out_specs=[xs, gblk, gblk, pl.BlockSpec((BLK, DT_PAD), lambda c: (ci(c), 0)), v128, v128, v128],
                  out_shape=[_sds((T, D_INNER), F32), _sds((T, gn), F32), _sds((T, gn), F32), _sds((T, DT_PAD), BF16),
                             _sds((1, 128), F32), _sds((1, 128), F32), _sds((1, 128), F32)],
                  scratch_shapes=[pltpu.VMEM((SSM_HEADS, HEAD_DIM, D_STATE), F32)],
                  compiler_params=_cparams(("arbitrary",)))(xbc, xbc, xbc, projp, dtb, alog, dfull, states, dy, E, ET)


_WIN_ORDER = ("z", "ga", "gs", "xbc", "q", "k", "v", "dt")


PERM_TILE = 512


def _win_row_moves():
    off = dict(z=OFF_Z, ga=OFF_GA, gs=OFF_GS, xbc=OFF_XBC, q=OFF_Q, k=OFF_K, v=OFF_V, dt=OFF_DT)
    per = IN_DIM // N_DEV
    tiles = [[] for _ in range(-(-IN_PAD // PERM_TILE))]
    for nm in _WIN_ORDER:
        s, w = SEG[nm]
        d = off[nm]
        while w > 0:
            j, r = divmod(s, per)
            n = min(w, per - r, PERM_TILE - d % PERM_TILE)
            tiles[d // PERM_TILE].append((j, r, n, d % PERM_TILE))
            s, w, d = s + n, w - n, d + n
    return tiles


def _win_to_padded(win_g):
    per = IN_DIM // N_DEV
    moves = _win_row_moves()

    def body(w_ref, o_ref, slots, stage, in_sems, out_sems):
        loads = [pltpu.make_async_copy(w_ref.at[j], slots.at[j], in_sems.at[j]) for j in range(N_DEV)]
        for cp in loads:
            cp.start()
        arrived = [False] * N_DEV
        stores = [None, None]
        for t, pieces in enumerate(moves):
            rows = min(PERM_TILE, IN_PAD - PERM_TILE * t)
            b = t % 2
            if stores[b] is not None:
                stores[b].wait()
            filled = 0
            for j, r, n, d in pieces:
                if not arrived[j]:
                    loads[j].wait()
                    arrived[j] = True
                stage[b, pl.ds(d, n), :] = slots[j, pl.ds(r, n), :]
                filled = max(filled, d + n)
            if filled < rows:
                stage[b, pl.ds(filled, rows - filled), :] = jnp.zeros((rows - filled, D_MODEL), stage.dtype)
            stores[b] = pltpu.make_async_copy(stage.at[b, pl.ds(0, rows), :], o_ref.at[pl.ds(PERM_TILE * t, rows), :], out_sems.at[b])
            stores[b].start()
        for cp in stores:
            cp.wait()

    return _pcall(body, name="w_in_rows", in_specs=[_ANY], out_specs=_ANY, out_shape=_sds((IN_PAD, D_MODEL), win_g.dtype),
                  scratch_shapes=[pltpu.VMEM((N_DEV, per, D_MODEL), win_g.dtype), pltpu.VMEM((2, PERM_TILE, D_MODEL), win_g.dtype),
                                  pltpu.SemaphoreType.DMA((N_DEV,)), pltpu.SemaphoreType.DMA((2,))],
                  compiler_params=pltpu.CompilerParams(vmem_limit_bytes=60 * 1024 * 1024))(win_g)


def _padded_to_win(dw):
    off = dict(z=OFF_Z, ga=OFF_GA, gs=OFF_GS, xbc=OFF_XBC, q=OFF_Q, k=OFF_K, v=OFF_V, dt=OFF_DT)
    per = IN_DIM // N_DEV
    blocks = []
    for j in range(N_DEV):
        lo, hi, rows = j * per, (j + 1) * per, []
        for nm in ("q", "k", "v", "z", "xbc", "dt", "ga", "gs"):
            s, w = SEG[nm]
            a, b = max(lo, s), min(hi, s + w)
            if a < b:
                rows.append(dw[off[nm] + a - s:off[nm] + b - s])
        blocks.append(jnp.concatenate(rows, axis=0))
    return jnp.stack(blocks)


def _pad128(v):
    return jnp.pad(v, ((0, 0), (0, 128 - v.shape[1])))


_SMALL = (("loss", 128, 1), ("g_mix", 2048, 2048), ("conv_b", 3072, 3072), ("dt_bias", 128, 32), ("a_log", 128, 32),
          ("d_skip", 128, 32), ("g_ssd", 2048, 2048), ("sinks", 128, 16), ("g_ffn", 2048, 2048), ("g_ple", 2048, 2048),
          ("g_final", 2048, 2048))


def _small_vec(d):
    parts = []
    for nm, pw, w in _SMALL:
        v = d[nm].reshape(1, -1).astype(F32)
        parts.append(jnp.pad(v[:, :min(v.shape[1], pw)], ((0, 0), (0, pw - min(v.shape[1], pw)))))
    return jnp.concatenate(parts, axis=1)


def _small_split(vec):
    out, o = {}, 0
    for nm, pw, w in _SMALL:
        out[nm] = vec[0, o:o + w]
        o += pw
    return out


def kernel(x, p, positions, g_mix, w_in, conv_w, conv_b, dt_bias, a_log, d_skip, g_ssd, sinks, w_attn_br, w_ssd_br, w_o, g_ffn, w_gate, w_up, w_down, g_ple, w_ple_gate, w_ple_proj, g_final, loss_target, m_g_mix, m_w_in, m_conv_w, m_conv_b, m_dt_bias, m_a_log, m_d_skip, m_g_ssd, m_sinks, m_w_attn_br, m_w_ssd_br, m_w_o, m_g_ffn, m_w_gate, m_w_up, m_w_down, m_g_ple, m_w_ple_gate, m_w_ple_proj, m_g_final, v_g_mix, v_w_in, v_conv_w, v_conv_b, v_dt_bias, v_a_log, v_d_skip, v_g_ssd, v_sinks, v_w_attn_br, v_w_ssd_br, v_w_o, v_g_ffn, v_w_gate, v_w_up, v_w_down, v_g_ple, v_w_ple_gate, v_w_ple_proj, v_g_final):
    T = x.shape[1]
    D = D_MODEL
    W = dict(g_mix=g_mix, w_in=w_in, conv_w=conv_w, conv_b=conv_b, dt_bias=dt_bias, a_log=a_log, d_skip=d_skip, g_ssd=g_ssd,
             sinks=sinks, w_attn_br=w_attn_br, w_ssd_br=w_ssd_br, w_o=w_o, g_ffn=g_ffn, w_gate=w_gate, w_up=w_up, w_down=w_down,
             g_ple=g_ple, w_ple_gate=w_ple_gate, w_ple_proj=w_ple_proj, g_final=g_final)
    Mo = dict(g_mix=m_g_mix, w_in=m_w_in, conv_w=m_conv_w, conv_b=m_conv_b, dt_bias=m_dt_bias, a_log=m_a_log, d_skip=m_d_skip,
              g_ssd=m_g_ssd, sinks=m_sinks, w_attn_br=m_w_attn_br, w_ssd_br=m_w_ssd_br, w_o=m_w_o, g_ffn=m_g_ffn, w_gate=m_w_gate,
              w_up=m_w_up, w_down=m_w_down, g_ple=m_g_ple, w_ple_gate=m_w_ple_gate, w_ple_proj=m_w_ple_proj, g_final=m_g_final)
    Vo = dict(g_mix=v_g_mix, w_in=v_w_in, conv_w=v_conv_w, conv_b=v_conv_b, dt_bias=v_dt_bias, a_log=v_a_log, d_skip=v_d_skip,
              g_ssd=v_g_ssd, sinks=v_sinks, w_attn_br=v_w_attn_br, w_ssd_br=v_w_ssd_br, w_o=v_w_o, g_ffn=v_g_ffn, w_gate=v_w_gate,
              w_up=v_w_up, w_down=v_w_down, g_ple=v_g_ple, w_ple_gate=v_w_ple_gate, w_ple_proj=v_w_ple_proj, g_final=v_g_final)
    order = ["g_mix", "w_in", "conv_w", "conv_b", "dt_bias", "a_log", "d_skip", "g_ssd", "sinks", "w_attn_br", "w_ssd_br", "w_o",
             "g_ffn", "w_gate", "w_up", "w_down", "g_ple", "w_ple_gate", "w_ple_proj", "g_final"]
    big = ["w_in", "conv_w", "w_attn_br", "w_ssd_br", "w_o", "w_gate", "w_up", "w_down", "w_ple_gate", "w_ple_proj"]

    x2 = x.reshape(T, D)
    p2 = p.reshape(T, PLE_DIM)
    tgt = loss_target.reshape(T, D)
    posf = positions.reshape(T, 1).astype(F32)
    inv = ROPE_THETA ** (-np.arange(HEAD_DIM // 2, dtype=np.float32) * 2.0 / HEAD_DIM)
    inv128 = jnp.asarray(np.tile(inv, 128 // (HEAD_DIM // 2)).reshape(1, 128).astype(np.float32))
    transposed = ("w_in", "w_gate", "w_up")

    def shard2d(a, n):
        a = a.reshape(a.shape[-2:])
        return a.T if n in transposed else a

    sh = {n: shard2d(W[n], n) for n in big}

    del _PENDING[:]
    me = 4 * lax.axis_index("x") + 2 * lax.axis_index("y") + lax.axis_index("c")
    groups = (("w_in",), ("conv_w", "w_attn_br", "w_ssd_br", "w_o"), ("w_gate", "w_up"), ("w_down",), ("w_ple_gate", "w_ple_proj"))
    send = {n: sh[n] if n == "conv_w" else sh[n].astype(BF16) for n in big}
    started, prev = [], None
    for gi, grp in enumerate(groups):
        zones = [lax.dynamic_update_index_in_dim(lax.empty((N_DEV,) + send[n].shape, send[n].dtype), send[n], me, 0) for n in grp]
        if gi == 0:
            ring_a = split_start("ring_a_start", "route", ((4, 0, 0), (2, 0, 1)), [], lands=zones)
            h = split_start("ring_b_start", "route", ((4, 0, 1), (2, 0, 0)), [], lands=ring_a["lands"], after=ring_a["token"])
        else:
            h = split_start("gather_start_%d" % gi, "gather", ICI_SAME_CORE, [], lands=zones, after=prev)
        prev = h["token"]
        started.append(h)
    gathered, fwd = {}, {}

    def forward_start(gi, after, lands=None):
        if lands is None:
            _, lands = split_wait("gather_wait_%d" % gi, started[gi], after)
        fwd[gi] = split_start("forward_start_%d" % gi, "forward", FORWARD_BLOCKS, [], lands=lands)

    def forward_wait(gi, after):
        _, full = split_wait("forward_wait_%d" % gi, fwd[gi], after)
        gathered.update(zip(groups[gi], full))

    u = rms_fwd("norm_mix", x2, g_mix)
    _, zone = split_wait("ring_a_wait", dict(ring_a, lands=started[0]["lands"]), u)
    ring_c = split_start("ring_c_start", "route", ((2, 4, 0), (4, 2, 1)), [], lands=zone)
    _, zone = split_wait("ring_b_wait", dict(started[0], lands=ring_c["lands"]), u)
    _, zone = split_wait("ring_c_wait", dict(ring_c, lands=zone), u)
    forward_start(0, u, lands=zone)
    forward_wait(0, u)
    winp = _win_to_padded(gathered["w_in"])
    dtb = _pad128(dt_bias)
    alog = _pad128(a_log)
    dfull = jnp.repeat(d_skip.reshape(SSM_HEADS), HEAD_DIM).reshape(1, D_INNER)

    projp = mm_nt("in_proj", u, winp, 640)
    forward_start(1, projp)
    attn, qr, kr = attn_fwd(projp, posf, inv128, sinks)
    forward_wait(1, attn)
    convw = jnp.transpose(gathered["conv_w"], (1, 0, 2)).reshape(CONV_WIDTH, CONV_DIM)
    wab = gathered["w_attn_br"]
    wsb = gathered["w_ssd_br"].reshape(D, D)
    wo = gathered["w_o"].reshape(D, D)
    xbc = conv_fwd(projp, convw, conv_b)
    y, states = ssd_fwd(xbc, projp, dtb, alog, dfull)
    yn = gnorm_fwd(y, projp, g_ssd)
    out_a = mm_nn_colblk("attn_br", attn, wab)
    out_s = mm_nn("ssd_br", yn, wsb, 512)
    forward_start(2, out_s)
    merged = merge_fwd(projp, out_a, out_s)
    h1 = mm_nn("o_proj", merged, wo, 512, residual=x2)
    f = rms_fwd("norm_ffn", h1, g_ffn)
    forward_wait(2, f)
    forward_start(3, f)
    wgt, wut = (gathered[n].reshape(FFN_HIDDEN, D) for n in ("w_gate", "w_up"))
    gate, up, act = ffn_up(f, wgt, wut)
    forward_wait(3, act)
    forward_start(4, act)
    wd = gathered["w_down"].reshape(FFN_HIDDEN, D)
    h2 = mm_nn_red("ffn_down", act, wd, 512, FFN_HIDDEN, residual=h1)
    r = rms_fwd("norm_ple", h2, g_ple)
    forward_wait(4, r)
    wpg = gathered["w_ple_gate"].reshape(D, D)
    wpp = gathered["w_ple_proj"]
    pg = mm_nn("ple_gate", r, wpg, 512)
    pp = mm_nn_colblk("ple_proj", p2, wpp)
    loss_v, dh3, dpg, dpp, dg_final = head_fwd_bwd(h2, pg, pp, g_final.reshape(1, D), tgt)

    gw = {}
    scat = []

    def scatter_start(names):
        scat.append((names, split_start("scatter_start_%d" % len(scat), "scatter", ALL_PEERS, [gw[n] for n in names])))

    gw["w_ple_proj"] = mm_tn_colblk("dw_ple_proj", p2, dpp, PLE_DIM)
    dr = mm_nt("d_ple_gate", dpg, wpg, 512)
    gw["w_ple_gate"] = mm_tn("dw_ple_gate", r, dpg, 512, D).reshape(N_DEV, D // N_DEV, D)
    scatter_start(("w_ple_proj", "w_ple_gate"))
    dh2, dh2b, dg_ple = rms_bwd("norm_ple_bwd", h2, g_ple, dr, dh3)
    dgate, dup = ffn_down_bwd(dh2b, wd, gate, up)
    per = FFN_HIDDEN // N_DEV
    gw["w_down"] = mm_tn("dw_down", act, dh2b, FFN_TILE, D).reshape(N_DEV, per, D)
    gw["w_gate"] = mm_tn("dw_gate", dgate, f, FFN_TILE, D).reshape(N_DEV, per, D)
    gw["w_up"] = mm_tn("dw_up", dup, f, FFN_TILE, D).reshape(N_DEV, per, D)
    scatter_start(("w_down", "w_gate", "w_up"))
    df = ffn_up_bwd(dgate, dup, wgt, wut)
    dh1, dh1b, dg_ffn = rms_bwd("norm_ffn_bwd", h1, g_ffn, df, dh2)
    dmerged = mm_nt("d_o_proj", dh1b, wo, 512)
    gw["w_o"] = mm_tn("dw_o", merged, dh1b, 512, D).reshape(N_DEV, D // N_DEV, D)
    dout_a, dout_s, dga, dgs = merge_bwd(projp, out_a, out_s, dmerged)
    gw["w_ssd_br"] = mm_tn("dw_ssd_br", yn, dout_s, 512, D).reshape(N_DEV, D // N_DEV, D)
    gw["w_attn_br"] = mm_tn_colblk("dw_attn_br", attn, dout_a, D // N_DEV)
    scatter_start(("w_o", "w_ssd_br", "w_attn_br"))
    dyn = mm_nt("d_ssd_br", dout_s, wsb, 512)
    dattn = attn_br_bwd(dout_a, wab)
    dy, dz, dg_ssd = gnorm_bwd(y, projp, g_ssd, dyn)
    dxs, dbm, dcm, ddt, dal, ddsk, ddtb = ssd_bwd(xbc, projp, dtb, alog, dfull, states, dy)
    dx_x, dwc_x, dbc_x = conv_bwd("conv_bwd_x", projp, dxs, convw, conv_b, 0)
    dx_b, dwc_b, dbc_b = conv_bwd("conv_bwd_b", projp, dbm, convw, conv_b, D_INNER)
    dx_c, dwc_c, dbc_c = conv_bwd("conv_bwd_c", projp, dcm, convw, conv_b, D_INNER + SSM_GROUPS * D_STATE)
    dq, dk, dv, dsk = attn_bwd(qr, kr, projp, dattn, posf, inv128, sinks)
    dproj = jnp.concatenate([dz, dga, dgs, dx_x, dx_b, dx_c, dq, dk.astype(BF16), dv.astype(BF16), ddt], axis=1)
    gw_in = _padded_to_win(mm_tn("dw_in", dproj, u, 640, D))
    pair = split_start("pair_start", "pair", FORWARD_BLOCKS, [gw_in])
    dconvw = jnp.concatenate([dwc_x, dwc_b, dwc_c], axis=1)
    gw["conv_w"] = jnp.transpose(dconvw.reshape(CONV_WIDTH, N_DEV, CONV_DIM // N_DEV), (1, 0, 2))
    scatter_start(("conv_w",))
    du_tm = min(512, T)
    tiles = T // du_tm
    first = max(tiles // 4, 1)
    du = mm_nn_red("d_in_proj_a", dproj, winp, 512, IN_PAD, rows=(0, first), tm=du_tm)
    (gw_in,), (sibling_part,) = split_wait("pair_wait", pair, du)
    pair_slots = jnp.stack([jnp.bitwise_xor(me, k) for k in FORWARD_BLOCKS]).astype(jnp.int32)
    core = split_start("core_start", "scatter_core", ICI_SAME_CORE, [pair_sum("pair_sum_w_in", gw_in, pair_slots, sibling_part)])
    if first < tiles:
        du = mm_nn_red("d_in_proj_b", dproj, winp, 512, IN_PAD, rows=(first, tiles - first), prev=du, tm=du_tm)
    gx, _, dg_mix = rms_bwd("norm_mix_bwd", x2, g_mix, du, dh1)

    small_g = dict(loss=loss_v[:, :1], g_mix=dg_mix, conv_b=jnp.concatenate([dbc_x, dbc_b, dbc_c], axis=1), dt_bias=ddtb,
                   a_log=dal, d_skip=ddsk, g_ssd=dg_ssd, sinks=dsk, g_ffn=dg_ffn, g_ple=dg_ple, g_final=dg_final)
    vec = _small_vec(small_g)
    small = split_start("small_start", "gather", ALL_PEERS, [],
                        lands=[lax.dynamic_update_index_in_dim(lax.empty((N_DEV,) + vec.shape, F32), vec, me, 0)])

    res = {}
    after = [gx]
    for si, (names, h) in enumerate(scat):
        srcs, lands = split_wait("scatter_wait_%d" % si, h, after)
        for n, mine, arrived in zip(names, srcs, lands):
            res[n] = adamw("adamw_" + n, arrived, sh[n], shard2d(Mo[n], n), shard2d(Vo[n], n), own=mine, own_slot=me)
        after = [res[n][0] for n in names]
    zero = jnp.zeros((1, 1), F32)
    _, (vec_parts,) = split_wait("small_wait", small, [res[n][0] for n in res])
    sres = adamw("adamw_small", vec_parts, _small_vec({**W, "loss": zero}), _small_vec({**Mo, "loss": zero}),
                 _small_vec({**Vo, "loss": zero}))
    ssplit = [_small_split(a) for a in sres]

    (pair_sums,), (arrived,) = split_wait("core_wait", core, [sres[0]])
    res["w_in"] = adamw("adamw_w_in", arrived, sh["w_in"], shard2d(Mo["w_in"], "w_in"), shard2d(Vo["w_in"], "w_in"),
                        own=pair_sums, own_slot=0)
    loss = ssplit[0]["loss"].reshape(())
    for n in order:
        if n not in res:
            res[n] = tuple(s[n].reshape(W[n].shape) for s in ssplit)
        else:
            res[n] = tuple((a.T if n in transposed else a).reshape(W[n].shape) for a in res[n])
    outs = [loss, gx.reshape(x.shape)]
    for k in range(4):
        outs += [res[n][k] for n in order]
    return tuple(outs)
```

```python
import functools

import numpy as np
import jax
import jax.numpy as jnp
from jax import lax
from jax.experimental import pallas as pl
from jax.experimental.pallas import tpu as pltpu

F32 = jnp.float32
BF16 = jnp.bfloat16

N_DEV = 8
D_MODEL = 2048
HEAD_DIM = 64
ATTN_HEADS = 16
KV_HEADS = 4
Q_DIM = 1024
KV_DIM = 256
BLK = 128
D_INNER = 2048
SSM_HEADS = 32
SSM_GROUPS = 4
HEADS_PER_GROUP = 8
D_STATE = 128
CONV_WIDTH = 4
CONV_DIM = 3072
FFN_HIDDEN = 5632
PLE_DIM = 256
IN_DIM = 10784
NORM_EPS = 1e-6
SSM_NORM_EPS = 1e-5
ROPE_THETA = 10000.0

OFF_Z, OFF_GA, OFF_GS, OFF_XBC, OFF_Q, OFF_K, OFF_V, OFF_DT = 0, 2048, 4096, 6144, 9216, 10240, 10496, 10752
IN_PAD = 10880
DT_PAD = 128
SEG = dict(q=(0, 1024), k=(1024, 256), v=(1280, 256), z=(1536, 2048), xbc=(3584, 3072), dt=(6656, 32),
           ga=(6688, 2048), gs=(8736, 2048))

ADAM_LR, ADAM_B1, ADAM_B2, ADAM_EPS, ADAM_WD, ADAM_STEP = 0.001, 0.9, 0.999, 1e-08, 0.01, 10

VMEM_LIMIT = 56 * 1024 * 1024

NN = (((1,), (0,)), ((), ()))
NT = (((1,), (1,)), ((), ()))
TN = (((0,), (0,)), ((), ()))


_PENDING = []


def _raw_call(body, **kw):
    return pl.pallas_call(body, **kw)


def _pcall(body, **kw):
    if "in_specs" not in kw:
        return _raw_call(body, **kw)
    deps = list(_PENDING)
    del _PENDING[:]
    if not deps:
        return _raw_call(body, **kw)
    n_in = len(kw["in_specs"])

    def tied(*refs):
        return body(*refs[:n_in], *refs[n_in + len(deps):])

    kw["in_specs"] = list(kw["in_specs"]) + [pl.BlockSpec(memory_space=pl.ANY)] * len(deps)
    call = _raw_call(tied, **kw)
    return lambda *ops: call(*ops, *deps)


def _cparams(sem=None):
    if sem is None:
        return pltpu.CompilerParams(vmem_limit_bytes=VMEM_LIMIT)
    return pltpu.CompilerParams(vmem_limit_bytes=VMEM_LIMIT, dimension_semantics=sem)


def _dot(a, b, dn):
    return lax.dot_general(a.astype(BF16), b.astype(BF16), dn, preferred_element_type=F32)


def _sigmoid(x):
    return 1.0 / (1.0 + jnp.exp(-x))


def _silu(x):
    return x * _sigmoid(x)


def _dsilu(x):
    s = _sigmoid(x)
    return s * (1.0 + x * (1.0 - s))


def _matmul(name, pairs, pair_specs, dn, grid, out_shapes, out_specs, nred=1, extra=(), extra_specs=(),
            epilogue=None, acc_shape=None, alias=None):
    n_in = 2 * len(pairs) + len(extra)
    n_out = len(out_shapes)

    def body(*refs):
        ins = refs[:2 * len(pairs)]
        ex = [r for r, sp in zip(refs[2 * len(pairs):n_in], extra_specs) if sp.memory_space != pl.ANY]
        outs = refs[n_in:n_in + n_out]

        def prod():
            s = None
            for p in range(len(pairs)):
                d = _dot(ins[2 * p][...], ins[2 * p + 1][...], dn)
                s = d if s is None else s + d
            return s

        def finish(val):
            if epilogue is None:
                outs[0][...] = val.astype(outs[0].dtype)
            else:
                res = epilogue(val, *[e[...] for e in ex])
                for o, r in zip(outs, res):
                    o[...] = r.astype(o.dtype)

        if nred == 1:
            finish(prod())
        else:
            acc = refs[n_in + n_out]
            k = pl.program_id(len(grid) - 1)

            @pl.when(k == 0)
            def _():
                acc[...] = jnp.zeros_like(acc)

            acc[...] += prod()

            @pl.when(k == nred - 1)
            def _():
                finish(acc[...])

    operands = []
    specs = []
    for (a, b), (sa, sb) in zip(pairs, pair_specs):
        operands += [a, b]
        specs += [sa, sb]
    operands += list(extra)
    specs += list(extra_specs)
    scratch = [pltpu.VMEM(acc_shape, F32)] if nred > 1 else []
    sem = ("arbitrary",) * len(grid)
    res = _pcall(body, name=name, grid=grid, in_specs=specs, out_specs=list(out_specs), input_output_aliases=dict(alias or {}),
                 out_shape=list(out_shapes), scratch_shapes=scratch, compiler_params=_cparams(sem))(*operands)
    return res


def _sds(shape, dtype):
    return jax.ShapeDtypeStruct(shape, dtype)


def _row_tile(T):
    return min(1024, T)


def mm_nn(name, a, b, tn, out_dtype=F32, residual=None):
    M, K = a.shape
    N = b.shape[1]
    tm = _row_tile(M)
    grid = (M // tm, N // tn)
    extra, especs, epi = (), (), None
    if residual is not None:
        extra = (residual,)
        especs = (pl.BlockSpec((tm, tn), lambda i, n: (i, n)),)
        epi = lambda v, r: (v + r,)
    return _matmul(name, [(a, b)], [(pl.BlockSpec((tm, K), lambda i, n: (i, 0)), pl.BlockSpec((K, tn), lambda i, n: (0, n)))],
                   NN, grid, [_sds((M, N), out_dtype)], [pl.BlockSpec((tm, tn), lambda i, n: (i, n))],
                   extra=extra, extra_specs=especs, epilogue=epi)[0]


def mm_nn_colblk(name, a, b, out_dtype=F32):
    M, K = a.shape
    J, _, nb = b.shape
    tm = _row_tile(M)
    grid = (M // tm, J)
    return _matmul(name, [(a, b)], [(pl.BlockSpec((tm, K), lambda i, j: (i, 0)), pl.BlockSpec((None, K, nb), lambda i, j: (j, 0, 0)))],
                   NN, grid, [_sds((M, J * nb), out_dtype)], [pl.BlockSpec((tm, nb), lambda i, j: (i, j))])[0]


def mm_nt(name, a, w, tr, out_dtype=F32):
    M, C = a.shape
    R = w.shape[0]
    tm = _row_tile(M)
    grid = (M // tm, R // tr)
    return _matmul(name, [(a, w)], [(pl.BlockSpec((tm, C), lambda i, r: (i, 0)), pl.BlockSpec((tr, C), lambda i, r: (r, 0)))],
                   NT, grid, [_sds((M, R), out_dtype)], [pl.BlockSpec((tm, tr), lambda i, r: (i, r))])[0]


def mm_nt_red(name, a, w, tr, tk, out_dtype=F32):
    M, C = a.shape
    R = w.shape[0]
    tm = _row_tile(M)
    nk = C // tk
    grid = (M // tm, R // tr, nk)
    return _matmul(name, [(a, w)], [(pl.BlockSpec((tm, tk), lambda i, r, k: (i, k)), pl.BlockSpec((tr, tk), lambda i, r, k: (r, k)))],
                   NT, grid, [_sds((M, R), out_dtype)], [pl.BlockSpec((tm, tr), lambda i, r, k: (i, r))],
                   nred=nk, acc_shape=(tm, tr))[0]


def mm_nn_red(name, a, b, tn, tk, out_dtype=F32, residual=None, rows=None, prev=None, tm=None):
    M, K = a.shape
    N = b.shape[1]
    tm = min(tm or _row_tile(M), M)
    nk = K // tk
    i0, ni = (0, M // tm) if rows is None else rows
    grid = (ni, N // tn, nk)
    ospec = pl.BlockSpec((tm, tn), lambda i, n, k: (i + i0, n))
    extra, especs, epi = [], [], None
    if residual is not None:
        extra, especs, epi = [residual], [ospec], (lambda v, r, *_: (v + r,))
    alias = {}
    if prev is not None:
        alias = {2 + len(extra): 0}
        extra, especs = extra + [prev], especs + [_ANY]
        epi = epi or (lambda v, *_: (v,))
    return _matmul(name, [(a, b)], [(pl.BlockSpec((tm, tk), lambda i, n, k: (i + i0, k)), pl.BlockSpec((tk, tn), lambda i, n, k: (k, n)))],
                   NN, grid, [_sds((M, N), out_dtype)], [ospec], nred=nk, acc_shape=(tm, tn),
                   extra=extra, extra_specs=especs, epilogue=epi, alias=alias)[0]


def mm_tn(name, x, dy, tr, tc, out_dtype=BF16):
    M, R = x.shape
    C = dy.shape[1]
    grid = (R // tr, C // tc)
    return _matmul(name, [(x, dy)], [(pl.BlockSpec((M, tr), lambda r, c: (0, r)), pl.BlockSpec((M, tc), lambda r, c: (0, c)))],
                   TN, grid, [_sds((R, C), out_dtype)], [pl.BlockSpec((tr, tc), lambda r, c: (r, c))])[0]


def mm_tn_colblk(name, x, dy, nb, out_dtype=BF16):
    M, R = x.shape
    J = dy.shape[1] // nb
    grid = (J,)
    return _matmul(name, [(x, dy)], [(pl.BlockSpec((M, R), lambda j: (0, 0)), pl.BlockSpec((M, nb), lambda j: (0, j)))],
                   TN, grid, [_sds((J, R, nb), out_dtype)], [pl.BlockSpec((None, R, nb), lambda j: (j, 0, 0))])[0]


def _rows(T):
    return min(256, T)


def rms_fwd(name, x, g, eps=NORM_EPS):
    T, D = x.shape
    tm = _rows(T)

    def body(x_ref, g_ref, o_ref):
        xv = x_ref[...]
        r = lax.rsqrt(jnp.mean(xv * xv, axis=-1, keepdims=True) + eps)
        o_ref[...] = (xv * r * g_ref[...]).astype(BF16)

    return _pcall(body, name=name, grid=(T // tm,),
                  in_specs=[pl.BlockSpec((tm, D), lambda i: (i, 0)), pl.BlockSpec((1, D), lambda i: (0, 0))],
                  out_specs=pl.BlockSpec((tm, D), lambda i: (i, 0)), out_shape=_sds((T, D), BF16),
                  compiler_params=_cparams(("arbitrary",)))(x, g)


def rms_bwd(name, x, g, dy, dres, eps=NORM_EPS):
    T, D = x.shape
    tm = _rows(T)

    def body(x_ref, g_ref, dy_ref, dr_ref, dx_ref, dxb_ref, dg_ref):
        i = pl.program_id(0)
        xv = x_ref[...]
        r = lax.rsqrt(jnp.mean(xv * xv, axis=-1, keepdims=True) + eps)
        xh = xv * r
        dyv = dy_ref[...]
        gd = dyv * g_ref[...]
        dx = r * (gd - xh * jnp.mean(gd * xh, axis=-1, keepdims=True)) + dr_ref[...]
        dx_ref[...] = dx
        dxb_ref[...] = dx.astype(BF16)

        @pl.when(i == 0)
        def _():
            dg_ref[...] = jnp.zeros_like(dg_ref)

        dg_ref[...] += jnp.sum(dyv * xh, axis=0, keepdims=True)

    row = pl.BlockSpec((tm, D), lambda i: (i, 0))
    vec = pl.BlockSpec((1, D), lambda i: (0, 0))
    return _pcall(body, name=name, grid=(T // tm,), in_specs=[row, vec, row, row], out_specs=[row, row, vec],
                  out_shape=[_sds((T, D), F32), _sds((T, D), BF16), _sds((1, D), F32)],
                  compiler_params=_cparams(("arbitrary",)))(x, g, dy, dres)


def gnorm_fwd(y, projp, g):
    T, D = y.shape
    tm = _rows(T)

    def body(y_ref, z_ref, g_ref, o_ref):
        yz = y_ref[...] * _silu(z_ref[...])
        r = lax.rsqrt(jnp.mean(yz * yz, axis=-1, keepdims=True) + SSM_NORM_EPS)
        o_ref[...] = (yz * r * g_ref[...]).astype(BF16)

    row = pl.BlockSpec((tm, D), lambda i: (i, 0))
    return _pcall(body, name="gnorm_fwd", grid=(T // tm,),
                  in_specs=[row, pl.BlockSpec((tm, D), lambda i: (i, OFF_Z // D)), pl.BlockSpec((1, D), lambda i: (0, 0))],
                  out_specs=row, out_shape=_sds((T, D), BF16), compiler_params=_cparams(("arbitrary",)))(y, projp, g)


def gnorm_bwd(y, projp, g, dyn):
    T, D = y.shape
    tm = _rows(T)

    def body(y_ref, z_ref, g_ref, dyn_ref, dy_ref, dz_ref, dg_ref):
        i = pl.program_id(0)
        yv, zv = y_ref[...], z_ref[...]
        sz = _silu(zv)
        yz = yv * sz
        r = lax.rsqrt(jnp.mean(yz * yz, axis=-1, keepdims=True) + SSM_NORM_EPS)
        xh = yz * r
        dv = dyn_ref[...]
        gd = dv * g_ref[...]
        dyz = r * (gd - xh * jnp.mean(gd * xh, axis=-1, keepdims=True))
        dy_ref[...] = dyz * sz
        dz_ref[...] = (dyz * yv * _dsilu(zv)).astype(BF16)

        @pl.when(i == 0)
        def _():
            dg_ref[...] = jnp.zeros_like(dg_ref)

        dg_ref[...] += jnp.sum(dv * xh, axis=0, keepdims=True)

    row = pl.BlockSpec((tm, D), lambda i: (i, 0))
    vec = pl.BlockSpec((1, D), lambda i: (0, 0))
    return _pcall(body, name="gnorm_bwd", grid=(T // tm,),
                  in_specs=[row, pl.BlockSpec((tm, D), lambda i: (i, OFF_Z // D)), vec, row], out_specs=[row, row, vec],
                  out_shape=[_sds((T, D), F32), _sds((T, D), BF16), _sds((1, D), F32)],
                  compiler_params=_cparams(("arbitrary",)))(y, projp, g, dyn)


def merge_fwd(projp, out_a, out_s):
    T, D = out_a.shape
    tm = _rows(T)

    def body(ga_ref, gs_ref, a_ref, s_ref, o_ref):
        o_ref[...] = (_sigmoid(ga_ref[...]) * a_ref[...] + _sigmoid(gs_ref[...]) * s_ref[...]).astype(BF16)

    row = pl.BlockSpec((tm, D), lambda i: (i, 0))
    return _pcall(body, name="merge_fwd", grid=(T // tm,),
                  in_specs=[pl.BlockSpec((tm, D), lambda i: (i, OFF_GA // D)), pl.BlockSpec((tm, D), lambda i: (i, OFF_GS // D)), row, row],
                  out_specs=row, out_shape=_sds((T, D), BF16), compiler_params=_cparams(("arbitrary",)))(projp, projp, out_a, out_s)


def merge_bwd(projp, out_a, out_s, dmerged):
    T, D = out_a.shape
    tm = _rows(T)

    def body(ga_ref, gs_ref, a_ref, s_ref, dm_ref, da_ref, ds_ref, dga_ref, dgs_ref):
        dm = dm_ref[...]
        sa, ss = _sigmoid(ga_ref[...]), _sigmoid(gs_ref[...])
        da_ref[...] = (dm * sa).astype(BF16)
        ds_ref[...] = (dm * ss).astype(BF16)
        dga_ref[...] = (dm * a_ref[...] * sa * (1.0 - sa)).astype(BF16)
        dgs_ref[...] = (dm * s_ref[...] * ss * (1.0 - ss)).astype(BF16)

    row = pl.BlockSpec((tm, D), lambda i: (i, 0))
    return _pcall(body, name="merge_bwd", grid=(T // tm,),
                  in_specs=[pl.BlockSpec((tm, D), lambda i: (i, OFF_GA // D)), pl.BlockSpec((tm, D), lambda i: (i, OFF_GS // D)), row, row, row],
                  out_specs=[row] * 4, out_shape=[_sds((T, D), BF16)] * 4,
                  compiler_params=_cparams(("arbitrary",)))(projp, projp, out_a, out_s, dmerged)


def head_fwd_bwd(h2, pg, pp, g_final, target):
    T, D = h2.shape
    tm = _rows(T)

    def body(h_ref, pg_ref, pp_ref, g_ref, t_ref, loss_ref, dh_ref, dpg_ref, dpp_ref, dg_ref):
        i = pl.program_id(0)
        s = _sigmoid(pg_ref[...])
        ppv = pp_ref[...]
        h3 = h_ref[...] + s * ppv
        r = lax.rsqrt(jnp.mean(h3 * h3, axis=-1, keepdims=True) + NORM_EPS)
        xh = h3 * r
        gv = g_ref[...]
        e = xh * gv - t_ref[...]
        dyo = e * (1.0 / D)
        gd = dyo * gv
        dh = r * (gd - xh * jnp.mean(gd * xh, axis=-1, keepdims=True))
        dh_ref[...] = dh
        dpg_ref[...] = (dh * ppv * s * (1.0 - s)).astype(BF16)
        dpp_ref[...] = (dh * s).astype(BF16)

        @pl.when(i == 0)
        def _():
            dg_ref[...] = jnp.zeros_like(dg_ref)
            loss_ref[...] = jnp.zeros_like(loss_ref)

        dg_ref[...] += jnp.sum(dyo * xh, axis=0, keepdims=True)
        part = 0.5 * jnp.sum(jnp.mean(e * e, axis=-1, keepdims=True), axis=0, keepdims=True)
        loss_ref[...] += jnp.broadcast_to(part, loss_ref.shape)

    row = pl.BlockSpec((tm, D), lambda i: (i, 0))
    vec = pl.BlockSpec((1, D), lambda i: (0, 0))
    return _pcall(body, name="head_fwd_bwd", grid=(T // tm,), in_specs=[row, row, row, vec, row],
                  out_specs=[pl.BlockSpec((1, 128), lambda i: (0, 0)), row, row, row, vec],
                  out_shape=[_sds((1, 128), F32), _sds((T, D), F32), _sds((T, D), BF16), _sds((T, D), BF16), _sds((1, D), F32)],
                  compiler_params=_cparams(("arbitrary",)))(h2, pg, pp, g_final, target)


FFN_TILE = 512


def ffn_up(f, wgt, wut):
    T, D = f.shape
    H = wgt.shape[0]
    tm = _row_tile(T)

    def body(f_ref, wg_ref, wu_ref, g_ref, u_ref, a_ref):
        fv = f_ref[...]
        g = _dot(fv, wg_ref[...], NT)
        u = _dot(fv, wu_ref[...], NT)
        g_ref[...] = g
        u_ref[...] = u
        a_ref[...] = (_silu(g) * u).astype(BF16)

    wspec = pl.BlockSpec((FFN_TILE, D), lambda i, j: (j, 0))
    ospec = pl.BlockSpec((tm, FFN_TILE), lambda i, j: (i, j))
    return _pcall(body, name="ffn_up", grid=(T // tm, H // FFN_TILE), in_specs=[pl.BlockSpec((tm, D), lambda i, j: (i, 0)), wspec, wspec],
                  out_specs=[ospec] * 3, out_shape=[_sds((T, H), F32), _sds((T, H), F32), _sds((T, H), BF16)],
                  compiler_params=_cparams(("arbitrary", "arbitrary")))(f, wgt, wut)


def ffn_down_bwd(dh2b, wd, gate, up):
    T, D = dh2b.shape
    H = wd.shape[0]
    tm = _row_tile(T)
    ospec = pl.BlockSpec((tm, FFN_TILE), lambda i, j: (i, j))

    def epi(da, g, u):
        return (da * u * _dsilu(g), da * _silu(g))

    return _matmul("ffn_down_bwd", [(dh2b, wd)],
                   [(pl.BlockSpec((tm, D), lambda i, j: (i, 0)), pl.BlockSpec((FFN_TILE, D), lambda i, j: (j, 0)))],
                   NT, (T // tm, H // FFN_TILE), [_sds((T, H), BF16)] * 2, [ospec, ospec],
                   extra=(gate, up), extra_specs=(ospec, ospec), epilogue=epi)


def ffn_up_bwd(dgate, dup, wgt, wut):
    T, H = dgate.shape
    D = wgt.shape[1]
    tm = min(512, T)
    tn = 512
    aspec = pl.BlockSpec((tm, H), lambda i, n: (i, 0))
    wspec = pl.BlockSpec((H, tn), lambda i, n: (0, n))
    return _matmul("ffn_up_bwd", [(dgate, wgt), (dup, wut)], [(aspec, wspec), (aspec, wspec)], NN, (T // tm, D // tn),
                   [_sds((T, D), F32)], [pl.BlockSpec((tm, tn), lambda i, n: (i, n))])[0]


def attn_br_bwd(dout_a, wab):
    T, D = dout_a.shape
    J, R, nb = wab.shape
    tm = _row_tile(T)
    return _matmul("attn_br_bwd", [(dout_a, wab)],
                   [(pl.BlockSpec((tm, nb), lambda i, j: (i, j)), pl.BlockSpec((None, R, nb), lambda i, j: (j, 0, 0)))],
                   NT, (T // tm, J), [_sds((T, R), BF16)], [pl.BlockSpec((tm, R), lambda i, j: (i, 0))], nred=J, acc_shape=(tm, R))[0]


def _adam_math(w, g, m, v):
    m2 = ADAM_B1 * m + (1.0 - ADAM_B1) * g
    v2 = ADAM_B2 * v + (1.0 - ADAM_B2) * (g * g)
    m_hat = m2 / (1.0 - ADAM_B1 ** ADAM_STEP)
    v_hat = v2 / (1.0 - ADAM_B2 ** ADAM_STEP)
    delta = -ADAM_LR * (m_hat / (jnp.sqrt(v_hat) + ADAM_EPS) + ADAM_WD * w)
    return delta, m2, v2


def _sum_partials(own, parts):
    g = None if own is None else own.astype(F32)
    if parts is not None:
        for s in range(parts.shape[0]):
            t = parts[s].astype(F32)
            g = t if g is None else g + t
    return g


def adamw(name, parts, w, m, v, own=None, own_slot=None):
    R, C = w.shape
    tr, tc = R, C
    for cand in (256, 176, 128, 64, 32, 16, 8):
        if R % cand == 0 and R > cand:
            tr = cand
            break
    if tr == R and R > 256:
        tc = 256
    given = [a for a in (parts, own) if a is not None]
    pre = own_slot is not None

    def body(*refs):
        refs = refs[1:] if pre else refs
        p_ref = refs[0] if parts is not None else None
        o_ref = refs[len(given) - 1] if own is not None else None
        w_ref, m_ref, v_ref, g_ref, d_ref, m2_ref, v2_ref = refs[-7:]
        g = _sum_partials(None if o_ref is None else o_ref[...], p_ref)
        d, m2, v2 = _adam_math(w_ref[...], g, m_ref[...], v_ref[...])
        g_ref[...] = g
        d_ref[...] = d
        m2_ref[...] = m2
        v2_ref[...] = v2

    blk = pl.BlockSpec((tr, tc), lambda i, j, *s: (i, j))
    specs = [] if parts is None else [pl.BlockSpec((parts.shape[0], tr, tc), lambda i, j, *s: (0, i, j))]
    if own is not None:
        specs.append(pl.BlockSpec((None, tr, tc), lambda i, j, s: (s[0], i, j)) if pre else blk)
    specs += [blk] * 3
    grid = (R // tr, C // tc)
    out_shape = [_sds((R, C), F32)] * 4
    params = _cparams(("arbitrary", "arbitrary"))
    if not pre:
        return _pcall(body, name=name, grid=grid, in_specs=specs, out_specs=[blk] * 4, out_shape=out_shape,
                      compiler_params=params)(*given, w, m, v)
    spec = pltpu.PrefetchScalarGridSpec(num_scalar_prefetch=1, grid=grid, in_specs=specs, out_specs=[blk] * 4)
    return _pcall(body, name=name, grid_spec=spec, out_shape=out_shape,
                  compiler_params=params)(jnp.asarray(own_slot, jnp.int32).reshape(1), *given, w, m, v)


_HBM = pl.BlockSpec(memory_space=pltpu.HBM)
_SEM = pl.BlockSpec(memory_space=pltpu.SEMAPHORE)
_ANY = pl.BlockSpec(memory_space=pl.ANY)
_SPLIT_PARAMS = dict(compiler_params=pltpu.CompilerParams(has_side_effects=pltpu.SideEffectType.DATAFLOW_SIDE_EFFECTING))
ICI_SAME_CORE = (2, 4, 6)
ALL_PEERS = (1, 2, 3, 4, 5, 6, 7)
LAND_SLOTS = {"gather": N_DEV, "scatter": N_DEV - 1, "pair": 4, "scatter_core": 3}


def _mesh_pos():
    x, y, c = lax.axis_index("x"), lax.axis_index("y"), lax.axis_index("c")
    return x, y, c, 4 * x + 2 * y + c


def _peer_of(k, x, y, c):
    px = 1 - x if k & 4 else x
    py = 1 - y if k & 2 else y
    pc = 1 - c if k & 1 else c
    return (px, py, pc), 4 * px + 2 * py + pc


def _split_copies(mode, ks, srcs, lands, send_sems, recv_sems):
    x, y, c, me = _mesh_pos()
    pairs = []
    for a in range(len(lands)):
        for j, k in enumerate(ks):
            dev, peer = _peer_of(k if mode != "route" else k[0], x, y, c)
            i = a * len(ks) + j
            if mode == "gather":
                s_out, d_out, d_in = lands[a].at[me], lands[a].at[me], lands[a].at[peer]
            elif mode == "scatter":
                s_out, d_out, d_in = srcs[a].at[peer], lands[a].at[k - 1], lands[a].at[k - 1]
            elif mode == "pair":
                dev, _ = _peer_of(1, x, y, c)
                _, theirs = _peer_of(k | 1, x, y, c)
                s_out, d_out, d_in = srcs[a].at[theirs], lands[a].at[j], lands[a].at[j]
            elif mode == "scatter_core":
                s_out, d_out, d_in = srcs[a].at[j + 1], lands[a].at[j], lands[a].at[j]
            elif mode == "route":
                k, rel, half = k
                dev, _ = _peer_of(k, x, y, c)
                _, held = _peer_of(rel, x, y, c)
                _, theirs = _peer_of(k ^ rel, x, y, c)
                cols = lands[a].shape[-1] // 2
                cut = (slice(None), slice(None)) if half is None else (slice(None), pl.ds(half * cols, cols))
                s_out, d_out, d_in = lands[a].at[held].at[cut], lands[a].at[held].at[cut], lands[a].at[theirs].at[cut]
            else:
                dev, _ = _peer_of(1, x, y, c)
                _, theirs = _peer_of(k | 1, x, y, c)
                s_out, d_out, d_in = lands[a].at[peer], lands[a].at[peer], lands[a].at[theirs]
            both = [pltpu.make_async_remote_copy(src_ref=s_out, dst_ref=d, send_sem=send_sems.at[i], recv_sem=recv_sems.at[i],
                                                 device_id=dev, device_id_type=pl.DeviceIdType.MESH) for d in (d_out, d_in)]
            pairs.append(tuple(both))
    return pairs


def split_start(name, mode, ks, srcs, lands=None, after=None):
    n, nk = len(srcs) if lands is None else len(lands), len(ks)
    srcs = [pltpu.with_memory_space_constraint(s, pltpu.HBM) for s in srcs]
    if lands is None:
        shapes = [((N_DEV,) + s.shape) if mode == "gather" else ((LAND_SLOTS[mode],) + s.shape[1:]) for s in srcs]
        lands = [lax.empty(shp, s.dtype) for shp, s in zip(shapes, srcs)]
    lands = [pltpu.with_memory_space_constraint(l, pltpu.HBM) for l in lands]
    both = srcs + lands
    extra = [] if after is None else [after]

    def body(*refs):
        src_refs, land_refs = refs[:len(srcs)], refs[len(srcs):len(both)]
        send_sems, recv_sems = refs[len(both) + len(extra)], refs[len(both) + len(extra) + 1]
        token = refs[-1]
        for out, _ in _split_copies(mode, ks, src_refs, land_refs, send_sems, recv_sems):
            out.start()
        token[...] = jnp.zeros_like(token)

    out_shape = (pltpu.SemaphoreType.DMA((n * nk,)), pltpu.SemaphoreType.DMA((n * nk,)),
                 *[pltpu.HBM(a.shape, a.dtype) for a in both], _sds((8, 128), F32))
    res = _raw_call(body, name=name, out_shape=out_shape, in_specs=[_HBM] * len(both) + [_ANY] * len(extra),
                    out_specs=(_SEM, _SEM, *[_HBM] * len(both), pl.BlockSpec(memory_space=pltpu.VMEM)),
                    input_output_aliases={i: 2 + i for i in range(len(both))}, **_SPLIT_PARAMS)(*both, *extra)
    _PENDING.append(res[-1])
    return dict(mode=mode, ks=ks, sems=(res[0], res[1]), srcs=list(res[2:2 + len(srcs)]),
                lands=list(res[2 + len(srcs):2 + len(both)]), token=res[-1])


def split_wait(name, h, after):
    ns = len(h["srcs"])
    both = h["srcs"] + h["lands"]
    after = list(after) if isinstance(after, (list, tuple)) else [after]

    def body(*refs):
        src_refs, land_refs = refs[:ns], refs[ns:len(both)]
        send_sems, recv_sems = refs[len(both)], refs[len(both) + 1]
        for out, arriving in _split_copies(h["mode"], h["ks"], src_refs, land_refs, send_sems, recv_sems):
            out.wait_send()
            arriving.wait_recv()

    res = _raw_call(body, name=name, out_shape=tuple(pltpu.HBM(a.shape, a.dtype) for a in both),
                    in_specs=[_HBM] * len(both) + [_SEM, _SEM] + [_ANY] * len(after), out_specs=tuple([_HBM] * len(both)),
                    input_output_aliases={i: i for i in range(len(both))}, **_SPLIT_PARAMS)(*both, *h["sems"], *after)
    return list(res[:ns]), list(res[ns:])


FORWARD_BLOCKS = (0, 2, 4, 6)


def pair_sum(name, mine, slots, theirs):
    P, R, C = theirs.shape
    tc = 512

    def body(s_ref, a_ref, b_ref, o_ref):
        o_ref[...] = (a_ref[...].astype(F32) + b_ref[...].astype(F32)).astype(o_ref.dtype)

    blk = pl.BlockSpec((None, R, tc), lambda p, i, s: (p, 0, i))
    spec = pltpu.PrefetchScalarGridSpec(num_scalar_prefetch=1, grid=(P, C // tc),
                                        in_specs=[pl.BlockSpec((None, R, tc), lambda p, i, s: (s[p], 0, i)), blk], out_specs=blk)
    return _pcall(body, name=name, grid_spec=spec, out_shape=_sds((P, R, C), theirs.dtype),
                  compiler_params=_cparams(("arbitrary", "arbitrary")))(slots, mine, theirs)


def _rope_parts(pos_ref, inv_ref):
    ang = pos_ref[...] * inv_ref[...]
    return jnp.cos(ang), jnp.sin(ang)


def _rot_half(t):
    lane = lax.broadcasted_iota(jnp.int32, t.shape, 1)
    return jnp.where((lane % HEAD_DIM) < HEAD_DIM // 2, -pltpu.roll(t, 128 - HEAD_DIM // 2, 1), pltpu.roll(t, HEAD_DIM // 2, 1))


def _attn_mask(n):
    row = lax.broadcasted_iota(jnp.int32, (BLK, 2 * BLK), 0)
    col = lax.broadcasted_iota(jnp.int32, (BLK, 2 * BLK), 1)
    dist = row + BLK - col
    return (dist >= 0) & (dist < BLK) & ((n * BLK - BLK + col) >= 0)


def _attn_specs(T):
    prev = lambda n: jnp.maximum(n - 1, 0)
    kc = pl.BlockSpec((BLK, KV_DIM), lambda n: (n, OFF_K // KV_DIM))
    kp = pl.BlockSpec((BLK, KV_DIM), lambda n: (prev(n), OFF_K // KV_DIM))
    vc = pl.BlockSpec((BLK, KV_DIM), lambda n: (n, OFF_V // KV_DIM))
    vp = pl.BlockSpec((BLK, KV_DIM), lambda n: (prev(n), OFF_V // KV_DIM))
    pc = pl.BlockSpec((BLK, 1), lambda n: (n, 0))
    pp = pl.BlockSpec((BLK, 1), lambda n: (prev(n), 0))
    inv = pl.BlockSpec((1, 128), lambda n: (0, 0))
    sink = pl.BlockSpec(memory_space=pltpu.SMEM)
    return kc, kp, vc, vp, pc, pp, inv, sink


def _softmax_sink(sc, valid, sink):
    sc = jnp.where(valid, sc * (HEAD_DIM ** -0.5), -1e30)
    m = jnp.maximum(jnp.max(sc, axis=1, keepdims=True), sink)
    e = jnp.exp(sc - m)
    es = jnp.exp(sink - m)
    den = jnp.sum(e, axis=1, keepdims=True) + es
    return e / den, es / den


def attn_fwd(projp, posf, inv128, sinks):
    T = projp.shape[0]
    kc, kp, vc, vp, pc, pp, inv, sink = _attn_specs(T)

    def body(q_ref, kc_ref, kp_ref, vc_ref, vp_ref, pc_ref, pp_ref, inv_ref, sink_ref, o_ref, qr_ref, kr_ref):
        n = pl.program_id(0)
        cos_c, sin_c = _rope_parts(pc_ref, inv_ref)
        cos_p, sin_p = _rope_parts(pp_ref, inv_ref)
        valid = _attn_mask(n)
        k_c, k_p = [], []
        for s in range(KV_DIM // 128):
            t = kc_ref[:, 128 * s:128 * (s + 1)]
            k_c.append((t * cos_c + _rot_half(t) * sin_c).astype(BF16))
            kr_ref[:, 128 * s:128 * (s + 1)] = k_c[s]
            t = kp_ref[:, 128 * s:128 * (s + 1)]
            k_p.append((t * cos_p + _rot_half(t) * sin_p).astype(BF16))
        kcat, vcat = [], []
        for hk in range(KV_HEADS):
            lo = HEAD_DIM * (hk % 2)
            kcat.append(jnp.concatenate([k_p[hk // 2][:, lo:lo + HEAD_DIM], k_c[hk // 2][:, lo:lo + HEAD_DIM]], axis=0))
            vcat.append(jnp.concatenate([vp_ref[:, HEAD_DIM * hk:HEAD_DIM * (hk + 1)], vc_ref[:, HEAD_DIM * hk:HEAD_DIM * (hk + 1)]], axis=0)
                        .astype(BF16))
        q_heads = []
        for s in range(Q_DIM // 128):
            t = q_ref[:, 128 * s:128 * (s + 1)]
            qs = (t * cos_c + _rot_half(t) * sin_c).astype(BF16)
            qr_ref[:, 128 * s:128 * (s + 1)] = qs
            q_heads += [qs[:, :HEAD_DIM], qs[:, HEAD_DIM:]]
        G = ATTN_HEADS // KV_HEADS
        scores = [_dot(q_heads[hq], kcat[hq // G], NT) for hq in range(ATTN_HEADS)]
        probs = [_softmax_sink(scores[hq], valid, sink_ref[0, hq])[0] for hq in range(ATTN_HEADS)]
        outs = [_dot(probs[hq], vcat[hq // G], NN) for hq in range(ATTN_HEADS)]
        for s in range(Q_DIM // 128):
            o_ref[:, 128 * s:128 * (s + 1)] = jnp.concatenate([outs[2 * s], outs[2 * s + 1]], axis=1).astype(BF16)

    qspec = pl.BlockSpec((BLK, Q_DIM), lambda n: (n, OFF_Q // Q_DIM))
    orow = pl.BlockSpec((BLK, Q_DIM), lambda n: (n, 0))
    krow = pl.BlockSpec((BLK, KV_DIM), lambda n: (n, 0))
    return _pcall(body, name="attn_fwd", grid=(T // BLK,), in_specs=[qspec, kc, kp, vc, vp, pc, pp, inv, sink],
                  out_specs=[orow, orow, krow], out_shape=[_sds((T, Q_DIM), BF16), _sds((T, Q_DIM), BF16), _sds((T, KV_DIM), BF16)],
                  compiler_params=_cparams(("arbitrary",)))(projp, projp, projp, projp, projp, posf, posf, inv128, sinks)


def attn_bwd(qr, kr, projp, dattn, posf, inv128, sinks):
    T = projp.shape[0]
    _, _, vc, vp, pc, pp, inv, sink = _attn_specs(T)
    G = ATTN_HEADS // KV_HEADS

    def body(qr_ref, krc_ref, krp_ref, vc_ref, vp_ref, do_ref, pc_ref, pp_ref, inv_ref, sink_ref, dq_ref, dk_ref, dv_ref, dsk_ref):
        n = pl.program_id(0)

        @pl.when(n == 0)
        def _():
            dk_ref[...] = jnp.zeros_like(dk_ref)
            dv_ref[...] = jnp.zeros_like(dv_ref)
            dsk_ref[...] = jnp.zeros_like(dsk_ref)

        cos_c, sin_c = _rope_parts(pc_ref, inv_ref)
        cos_p, sin_p = _rope_parts(pp_ref, inv_ref)
        valid = _attn_mask(n)
        lane = lax.broadcasted_iota(jnp.int32, (1, 128), 1)
        kcat, vcat = [], []
        for hk in range(KV_HEADS):
            ksl = slice(HEAD_DIM * hk, HEAD_DIM * (hk + 1))
            kcat.append(jnp.concatenate([krp_ref[:, ksl], krc_ref[:, ksl]], axis=0))
            vcat.append(jnp.concatenate([vp_ref[:, ksl], vc_ref[:, ksl]], axis=0).astype(BF16))
        H = range(ATTN_HEADS)
        q_heads = [qr_ref[:, HEAD_DIM * hq:HEAD_DIM * (hq + 1)] for hq in H]
        do_heads = [do_ref[:, HEAD_DIM * hq:HEAD_DIM * (hq + 1)] for hq in H]
        soft = [_softmax_sink(_dot(q_heads[hq], kcat[hq // G], NT), valid, sink_ref[0, hq]) for hq in H]
        dps = [_dot(do_heads[hq], vcat[hq // G], NT) for hq in H]
        deltas = [jnp.sum(soft[hq][0] * dps[hq], axis=1, keepdims=True) for hq in H]
        dss = [(soft[hq][0] * (dps[hq] - deltas[hq]) * (HEAD_DIM ** -0.5)).astype(BF16) for hq in H]
        pbs = [soft[hq][0].astype(BF16) for hq in H]
        dsk = jnp.zeros((1, 128), F32)
        for hq in H:
            dsk = dsk + jnp.where(lane == hq, -jnp.sum(soft[hq][1] * deltas[hq], axis=0, keepdims=True), 0.0)
        dsk_ref[...] += dsk
        dq_heads = [_dot(dss[hq], kcat[hq // G], NN) for hq in H]
        dk_parts = [_dot(dss[hq], q_heads[hq], TN) for hq in H]
        dv_parts = [_dot(pbs[hq], do_heads[hq], TN) for hq in H]
        dk_heads = [sum(dk_parts[G * hk + 1:G * (hk + 1)], dk_parts[G * hk]) for hk in range(KV_HEADS)]
        dv_heads = [sum(dv_parts[G * hk + 1:G * (hk + 1)], dv_parts[G * hk]) for hk in range(KV_HEADS)]
        for s in range(Q_DIM // 128):
            t = jnp.concatenate([dq_heads[2 * s], dq_heads[2 * s + 1]], axis=1)
            dq_ref[:, 128 * s:128 * (s + 1)] = (t * cos_c - _rot_half(t) * sin_c).astype(BF16)
        cur = pl.ds(pl.multiple_of(n * BLK, BLK), BLK)
        prv = pl.ds(pl.multiple_of(jnp.maximum(n - 1, 0) * BLK, BLK), BLK)
        for s in range(KV_DIM // 128):
            tc = jnp.concatenate([dk_heads[2 * s][BLK:], dk_heads[2 * s + 1][BLK:]], axis=1)
            tp = jnp.concatenate([dk_heads[2 * s][:BLK], dk_heads[2 * s + 1][:BLK]], axis=1)
            cols = slice(128 * s, 128 * (s + 1))
            dk_ref[cur, cols] += tc * cos_c - _rot_half(tc) * sin_c
            dk_ref[prv, cols] += tp * cos_p - _rot_half(tp) * sin_p
            dv_ref[cur, cols] += jnp.concatenate([dv_heads[2 * s][BLK:], dv_heads[2 * s + 1][BLK:]], axis=1)
            dv_ref[prv, cols] += jnp.concatenate([dv_heads[2 * s][:BLK], dv_heads[2 * s + 1][:BLK]], axis=1)

    qrow = pl.BlockSpec((BLK, Q_DIM), lambda n: (n, 0))
    krc = pl.BlockSpec((BLK, KV_DIM), lambda n: (n, 0))
    krp = pl.BlockSpec((BLK, KV_DIM), lambda n: (jnp.maximum(n - 1, 0), 0))
    whole = pl.BlockSpec((T, KV_DIM), lambda n: (0, 0))
    return _pcall(body, name="attn_bwd", grid=(T // BLK,), in_specs=[qrow, krc, krp, vc, vp, qrow, pc, pp, inv, sink],
                  out_specs=[qrow, whole, whole, pl.BlockSpec((1, 128), lambda n: (0, 0))],
                  out_shape=[_sds((T, Q_DIM), BF16), _sds((T, KV_DIM), F32), _sds((T, KV_DIM), F32), _sds((1, 128), F32)],
                  compiler_params=_cparams(("arbitrary",)))(qr, kr, kr, projp, projp, dattn, posf, posf, inv128, sinks)


CONV_CB = 256


def _shift_down(x, s):
    row = lax.broadcasted_iota(jnp.int32, x.shape, 0)
    return jnp.where(row >= s, pltpu.roll(x, s, 0), 0.0)


def _shift_up(x, s):
    T = x.shape[0]
    row = lax.broadcasted_iota(jnp.int32, x.shape, 0)
    return jnp.where(row < T - s, pltpu.roll(x, T - s, 0), 0.0)


def _conv_pre(x, w_ref, b_ref):
    acc = x * w_ref[CONV_WIDTH - 1:CONV_WIDTH, :] + b_ref[...]
    for s in range(1, CONV_WIDTH):
        acc = acc + _shift_down(x, s) * w_ref[CONV_WIDTH - 1 - s:CONV_WIDTH - s, :]
    return acc


def conv_fwd(projp, conv_w, conv_b):
    T = projp.shape[0]

    def body(x_ref, w_ref, b_ref, o_ref):
        o_ref[...] = _silu(_conv_pre(x_ref[...], w_ref, b_ref))

    return _pcall(body, name="conv_fwd", grid=(CONV_DIM // CONV_CB,),
                  in_specs=[pl.BlockSpec((T, CONV_CB), lambda c: (0, OFF_XBC // CONV_CB + c)),
                            pl.BlockSpec((CONV_WIDTH, CONV_CB), lambda c: (0, c)), pl.BlockSpec((1, CONV_CB), lambda c: (0, c))],
                  out_specs=pl.BlockSpec((T, CONV_CB), lambda c: (0, c)), out_shape=_sds((T, CONV_DIM), F32),
                  compiler_params=_cparams(("arbitrary",)))(projp, conv_w, conv_b)


def conv_bwd(name, projp, dact, conv_w, conv_b, col0):
    T, C = dact.shape
    c0 = col0 // CONV_CB

    def body(x_ref, da_ref, w_ref, b_ref, dx_ref, dw_ref, db_ref):
        x = x_ref[...]
        dpre = da_ref[...] * _dsilu(_conv_pre(x, w_ref, b_ref))
        dx = dpre * w_ref[CONV_WIDTH - 1:CONV_WIDTH, :]
        dw_ref[CONV_WIDTH - 1:CONV_WIDTH, :] = jnp.sum(dpre * x, axis=0, keepdims=True)
        for s in range(1, CONV_WIDTH):
            i = CONV_WIDTH - 1 - s
            dx = dx + _shift_up(dpre, s) * w_ref[i:i + 1, :]
            dw_ref[i:i + 1, :] = jnp.sum(dpre * _shift_down(x, s), axis=0, keepdims=True)
        dx_ref[...] = dx.astype(BF16)
        db_ref[...] = jnp.sum(dpre, axis=0, keepdims=True)

    return _pcall(body, name=name, grid=(C // CONV_CB,),
                  in_specs=[pl.BlockSpec((T, CONV_CB), lambda c: (0, OFF_XBC // CONV_CB + c0 + c)),
                            pl.BlockSpec((T, CONV_CB), lambda c: (0, c)),
                            pl.BlockSpec((CONV_WIDTH, CONV_CB), lambda c: (0, c0 + c)), pl.BlockSpec((1, CONV_CB), lambda c: (0, c0 + c))],
                  out_specs=[pl.BlockSpec((T, CONV_CB), lambda c: (0, c)), pl.BlockSpec((CONV_WIDTH, CONV_CB), lambda c: (0, c)),
                             pl.BlockSpec((1, CONV_CB), lambda c: (0, c))],
                  out_shape=[_sds((T, C), BF16), _sds((CONV_WIDTH, C), F32), _sds((1, C), F32)],
                  compiler_params=_cparams(("arbitrary",)))(projp, dact, conv_w, conv_b)


def _softplus(x):
    return jnp.maximum(x, 0.0) + jnp.log1p(jnp.exp(-jnp.abs(x)))


def _tri(lower):
    r = lax.broadcasted_iota(jnp.int32, (BLK, BLK), 0)
    c = lax.broadcasted_iota(jnp.int32, (BLK, BLK), 1)
    return (r >= c) if lower else (c >= r)


def _ssd_chunk_setup(dt_ref, dtb_ref, alog_ref):
    raw = dt_ref[...] + dtb_ref[...]
    dt = _softplus(raw)
    aneg = -jnp.exp(alog_ref[...])
    a = dt * aneg
    acs = jnp.dot(_tri(True).astype(F32), a, precision=lax.Precision.HIGHEST, preferred_element_type=F32)
    return raw, dt, aneg, acs, acs.T


def _ssd_specs(T, rev):
    nc = T // BLK
    ci = (lambda c: nc - 1 - c) if rev else (lambda c: c)
    xs = pl.BlockSpec((BLK, D_INNER), lambda c: (ci(c), 0))
    bm = pl.BlockSpec((BLK, SSM_GROUPS * D_STATE), lambda c: (ci(c), D_INNER // (SSM_GROUPS * D_STATE)))
    cm = pl.BlockSpec((BLK, SSM_GROUPS * D_STATE), lambda c: (ci(c), D_INNER // (SSM_GROUPS * D_STATE) + 1))
    dt = pl.BlockSpec((BLK, DT_PAD), lambda c: (ci(c), OFF_DT // DT_PAD))
    v128 = pl.BlockSpec((1, 128), lambda c: (0, 0))
    dfull = pl.BlockSpec((1, D_INNER), lambda c: (0, 0))
    st = pl.BlockSpec((None, SSM_HEADS, HEAD_DIM, D_STATE), lambda c: (ci(c), 0, 0, 0))
    return xs, bm, cm, dt, v128, dfull, st, ci


GW = HEADS_PER_GROUP * HEAD_DIM


def _expanders():
    e = np.zeros((SSM_GROUPS, 128, GW), np.float32)
    for g in range(SSM_GROUPS):
        for hh in range(HEADS_PER_GROUP):
            e[g, HEADS_PER_GROUP * g + hh, HEAD_DIM * hh:HEAD_DIM * (hh + 1)] = 1.0
    return jnp.asarray(e, BF16), jnp.asarray(np.transpose(e, (0, 2, 1)).copy(), BF16)


def _split2(v):
    hi = lax.bitcast_convert_type(lax.bitcast_convert_type(v, jnp.uint32) & jnp.uint32(0xFFFF0000), F32)
    return hi.astype(BF16), (v - hi).astype(BF16)


def _dotx(a, b):
    if a.dtype == BF16:
        hi, lo = _split2(b)
        return jnp.dot(a, hi, preferred_element_type=F32) + jnp.dot(a, lo, preferred_element_type=F32)
    hi, lo = _split2(a)
    return jnp.dot(hi, b, preferred_element_type=F32) + jnp.dot(lo, b, preferred_element_type=F32)


def _decay(acs, acsT, h, tril):
    return jnp.where(tril, jnp.exp(jnp.where(tril, acs[:, h:h + 1] - acsT[h:h + 1, :], 0.0)), 0.0)


def ssd_fwd(xbc, projp, dtb, alog, dfull):
    T = xbc.shape[0]
    nc = T // BLK
    xs, bm, cm, dts, v128, dfs, st, _ = _ssd_specs(T, False)
    E, _ = _expanders()

    def body(xs_ref, b_ref, c_ref, dt_ref, dtb_ref, alog_ref, d_ref, e_ref, y_ref, st_ref, h_scr):
        c = pl.program_id(0)

        @pl.when(c == 0)
        def _():
            h_scr[...] = jnp.zeros_like(h_scr)

        _, dt, _, acs, acsT = _ssd_chunk_setup(dt_ref, dtb_ref, alog_ref)
        tril = _tri(True)
        alast = acs[BLK - 1:BLK, :]
        eacs = jnp.exp(acs)
        wmat = jnp.exp(alast - acs)
        gam = jnp.exp(alast)
        for g in range(SSM_GROUPS):
            gl = slice(GW * g, GW * (g + 1))
            hsl = slice(HEADS_PER_GROUP * g, HEADS_PER_GROUP * (g + 1))
            heads = [HEADS_PER_GROUP * g + hh for hh in range(HEADS_PER_GROUP)]
            Eg = e_ref[g]
            B = b_ref[:, D_STATE * g:D_STATE * (g + 1)].astype(BF16)
            C = c_ref[:, D_STATE * g:D_STATE * (g + 1)].astype(BF16)
            cb = _dot(C, B, NT)
            x_g = xs_ref[:, gl]
            xd_g = x_g * _dotx(dt, Eg)
            hold = h_scr[hsl]
            st_ref[hsl] = hold
            hcat = hold.reshape(GW, D_STATE)
            yoff = _dotx(eacs, Eg) * _dot(C, hcat, NT)
            S = _dot(xd_g * _dotx(wmat, Eg), B, TN)
            Ms = [cb * _decay(acs, acsT, h, tril) for h in heads]
            ys = [_dot(Ms[hh], xd_g[:, HEAD_DIM * hh:HEAD_DIM * (hh + 1)], NN) for hh in range(HEADS_PER_GROUP)]
            for hh, h in enumerate(heads):
                h_scr[h] = gam[:, h:h + 1] * hold[hh] + S[HEAD_DIM * hh:HEAD_DIM * (hh + 1)]
            y_ref[:, gl] = jnp.concatenate(ys, axis=1) + yoff + d_ref[:, gl] * x_g

    espec = pl.BlockSpec((SSM_GROUPS, 128, GW), lambda c: (0, 0, 0))
    return _pcall(body, name="ssd_fwd", grid=(nc,), in_specs=[xs, bm, cm, dts, v128, v128, dfs, espec],
                  out_specs=[xs, st], out_shape=[_sds((T, D_INNER), F32), _sds((nc, SSM_HEADS, HEAD_DIM, D_STATE), F32)],
                  scratch_shapes=[pltpu.VMEM((SSM_HEADS, HEAD_DIM, D_STATE), F32)],
                  compiler_params=_cparams(("arbitrary",)))(xbc, xbc, xbc, projp, dtb, alog, dfull, E)


def ssd_bwd(xbc, projp, dtb, alog, dfull, states, dy):
    T = xbc.shape[0]
    nc = T // BLK
    xs, bm, cm, dts, v128, dfs, st, ci = _ssd_specs(T, True)
    gn = SSM_GROUPS * D_STATE
    E, ET = _expanders()

    def body(xs_ref, b_ref, c_ref, dt_ref, dtb_ref, alog_ref, d_ref, st_ref, dy_ref, e_ref, et_ref,
             dxs_ref, dB_ref, dC_ref, ddt_ref, dal_ref, dD_ref, ddtb_ref, dh_scr):
        i = pl.program_id(0)

        @pl.when(i == 0)
        def _():
            dh_scr[...] = jnp.zeros_like(dh_scr)
            dal_ref[...] = jnp.zeros_like(dal_ref)
            dD_ref[...] = jnp.zeros_like(dD_ref)
            ddtb_ref[...] = jnp.zeros_like(ddtb_ref)

        raw, dt, aneg, acs, acsT = _ssd_chunk_setup(dt_ref, dtb_ref, alog_ref)
        tril = _tri(True)
        lane = lax.broadcasted_iota(jnp.int32, (BLK, 128), 1)
        sub = lax.broadcasted_iota(jnp.int32, (BLK, 128), 0)
        alast = acs[BLK - 1:BLK, :]
        eacs = jnp.exp(acs)
        wmat = jnp.exp(alast - acs)
        gam = jnp.exp(alast)
        gcol = jnp.exp(acsT[:, BLK - 1:BLK])
        ds_col = jnp.zeros((BLK, 128), F32)
        ds_row = jnp.zeros((BLK, 128), F32)
        ddt_col = jnp.zeros((BLK, 128), F32)
        dDm = jnp.zeros((BLK, 128), F32)
        hl = [slice(HEAD_DIM * hh, HEAD_DIM * (hh + 1)) for hh in range(HEADS_PER_GROUP)]
        for g in range(SSM_GROUPS):
            gl = slice(GW * g, GW * (g + 1))
            gs = slice(D_STATE * g, D_STATE * (g + 1))
            hsl = slice(HEADS_PER_GROUP * g, HEADS_PER_GROUP * (g + 1))
            heads = [HEADS_PER_GROUP * g + hh for hh in range(HEADS_PER_GROUP)]
            Eg, ETg = e_ref[g], et_ref[g]
            B = b_ref[:, gs].astype(BF16)
            C = c_ref[:, gs].astype(BF16)
            cb = _dot(C, B, NT)
            x_g, dy_g = xs_ref[:, gl], dy_ref[:, gl]
            dt_x, w_x = _dotx(dt, Eg), _dotx(wmat, Eg)
            xd_g = x_g * dt_x
            dye = dy_g * _dotx(eacs, Eg)
            hcat = st_ref[hsl].reshape(GW, D_STATE)
            dSv = dh_scr[hsl]
            dScat = dSv.reshape(GW, D_STATE)
            dDm = dDm + _dotx(dy_g * x_g, ETg)
            dH_y = _dot(dye, C, TN)
            dC_g = _dot(dye, hcat, NN)
            ds_col = ds_col + _dotx(dye * _dot(C, hcat, NT), ETg)
            dxdw = _dot(B, dScat, NT)
            dB_g = _dot(xd_g * w_x, dScat, NN)
            dww = _dotx(xd_g * dxdw, ETg) * wmat
            ds_col = ds_col - dww + jnp.where(sub == BLK - 1, jnp.sum(dww, axis=0, keepdims=True), 0.0)
            hd = jnp.sum(_dotx(Eg, dScat * hcat), axis=1, keepdims=True) * gcol
            ds_row = ds_row - jnp.where(lane == BLK - 1, hd, 0.0)
            decays = [_decay(acs, acsT, h, tril) for h in heads]
            Ms = [cb * d for d in decays]
            dMs = [_dot(dy_g[:, hl[hh]], xd_g[:, hl[hh]], NT) for hh in range(HEADS_PER_GROUP)]
            dxd1 = [_dot(Ms[hh], dy_g[:, hl[hh]], TN) for hh in range(HEADS_PER_GROUP)]
            dG = jnp.zeros((BLK, BLK), F32)
            for hh, h in enumerate(heads):
                Q = dMs[hh] * Ms[hh]
                ds_col = ds_col + jnp.where(lane == h, jnp.sum(Q, axis=1, keepdims=True), 0.0)
                ds_row = ds_row + jnp.where(sub == h, jnp.sum(Q, axis=0, keepdims=True), 0.0)
                dG = dG + dMs[hh] * decays[hh]
            dxd_g = jnp.concatenate(dxd1, axis=1) + w_x * dxdw
            dxs_ref[:, gl] = d_ref[:, gl] * dy_g + dxd_g * dt_x
            ddt_col = ddt_col + _dotx(dxd_g * x_g, ETg)
            dC_ref[:, gs] = dC_g + _dot(dG, B, NN)
            dB_ref[:, gs] = dB_g + _dot(dG, C, TN)
            for hh, h in enumerate(heads):
                dh_scr[h] = gam[:, h:h + 1] * dSv[hh] + dH_y[hl[hh]]
        ds_all = ds_col - ds_row.T
        da = jnp.dot(_tri(False).astype(F32), ds_all, precision=lax.Precision.HIGHEST, preferred_element_type=F32)
        ddt = ddt_col + da * aneg
        draw = jnp.where(lane < SSM_HEADS, ddt * _sigmoid(raw), 0.0)
        ddt_ref[...] = draw.astype(BF16)
        dal_ref[...] += jnp.sum(da * dt, axis=0, keepdims=True) * aneg
        ddtb_ref[...] += jnp.sum(draw, axis=0, keepdims=True)
        dD_ref[...] += jnp.sum(dDm, axis=0, keepdims=True)

    gblk = pl.BlockSpec((BLK, gn), lambda c: (ci(c), 0))
    espec = pl.BlockSpec((SSM_GROUPS, 128, GW), lambda c: (0, 0, 0))
    etspec = pl.BlockSpec((SSM_GROUPS, GW, 128), lambda c: (0, 0, 0))
    return _pcall(body, name="ssd_bwd", grid=(nc,), in_specs=[xs, bm, cm, dts, v128, v128, dfs, st, xs, espec, etspec],
                  out_specs=[xs, gblk, gblk, pl.BlockSpec((BLK, DT_PAD), lambda c: (ci(c), 0)), v128, v128, v128],
                  out_shape=[_sds((T, D_INNER), F32), _sds((T, gn), F32), _sds((T, gn), F32), _sds((T, DT_PAD), BF16),
                             _sds((1, 128), F32), _sds((1, 128), F32), _sds((1, 128), F32)],
                  scratch_shapes=[pltpu.VMEM((SSM_HEADS, HEAD_DIM, D_STATE), F32)],
                  compiler_params=_cparams(("arbitrary",)))(xbc, xbc, xbc, projp, dtb, alog, dfull, states, dy, E, ET)


_WIN_ORDER = ("z", "ga", "gs", "xbc", "q", "k", "v", "dt")


PERM_TILE = 512


def _win_row_moves():
    off = dict(z=OFF_Z, ga=OFF_GA, gs=OFF_GS, xbc=OFF_XBC, q=OFF_Q, k=OFF_K, v=OFF_V, dt=OFF_DT)
    per = IN_DIM // N_DEV
    tiles = [[] for _ in range(-(-IN_PAD // PERM_TILE))]
    for nm in _WIN_ORDER:
        s, w = SEG[nm]
        d = off[nm]
        while w > 0:
            j, r = divmod(s, per)
            n = min(w, per - r, PERM_TILE - d % PERM_TILE)
            tiles[d // PERM_TILE].append((j, r, n, d % PERM_TILE))
            s, w, d = s + n, w - n, d + n
    return tiles


def _win_to_padded(win_g):
    per = IN_DIM // N_DEV
    moves = _win_row_moves()

    def body(w_ref, o_ref, slots, stage, in_sems, out_sems):
        loads = [pltpu.make_async_copy(w_ref.at[j], slots.at[j], in_sems.at[j]) for j in range(N_DEV)]
        for cp in loads:
            cp.start()
        arrived = [False] * N_DEV
        stores = [None, None]
        for t, pieces in enumerate(moves):
            rows = min(PERM_TILE, IN_PAD - PERM_TILE * t)
            b = t % 2
            if stores[b] is not None:
                stores[b].wait()
            filled = 0
            for j, r, n, d in pieces:
                if not arrived[j]:
                    loads[j].wait()
                    arrived[j] = True
                stage[b, pl.ds(d, n), :] = slots[j, pl.ds(r, n), :]
                filled = max(filled, d + n)
            if filled < rows:
                stage[b, pl.ds(filled, rows - filled), :] = jnp.zeros((rows - filled, D_MODEL), stage.dtype)
            stores[b] = pltpu.make_async_copy(stage.at[b, pl.ds(0, rows), :], o_ref.at[pl.ds(PERM_TILE * t, rows), :], out_sems.at[b])
            stores[b].start()
        for cp in stores:
            cp.wait()

    return _pcall(body, name="w_in_rows", in_specs=[_ANY], out_specs=_ANY, out_shape=_sds((IN_PAD, D_MODEL), win_g.dtype),
                  scratch_shapes=[pltpu.VMEM((N_DEV, per, D_MODEL), win_g.dtype), pltpu.VMEM((2, PERM_TILE, D_MODEL), win_g.dtype),
                                  pltpu.SemaphoreType.DMA((N_DEV,)), pltpu.SemaphoreType.DMA((2,))],
                  compiler_params=pltpu.CompilerParams(vmem_limit_bytes=60 * 1024 * 1024))(win_g)


def _padded_to_win(dw):
    off = dict(z=OFF_Z, ga=OFF_GA, gs=OFF_GS, xbc=OFF_XBC, q=OFF_Q, k=OFF_K, v=OFF_V, dt=OFF_DT)
    per = IN_DIM // N_DEV
    blocks = []
    for j in range(N_DEV):
        lo, hi, rows = j * per, (j + 1) * per, []
        for nm in ("q", "k", "v", "z", "xbc", "dt", "ga", "gs"):
            s, w = SEG[nm]
            a, b = max(lo, s), min(hi, s + w)
            if a < b:
                rows.append(dw[off[nm] + a - s:off[nm] + b - s])
        blocks.append(jnp.concatenate(rows, axis=0))
    return jnp.stack(blocks)


def _pad128(v):
    return jnp.pad(v, ((0, 0), (0, 128 - v.shape[1])))


_SMALL = (("loss", 128, 1), ("g_mix", 2048, 2048), ("conv_b", 3072, 3072), ("dt_bias", 128, 32), ("a_log", 128, 32),
          ("d_skip", 128, 32), ("g_ssd", 2048, 2048), ("sinks", 128, 16), ("g_ffn", 2048, 2048), ("g_ple", 2048, 2048),
          ("g_final", 2048, 2048))


def _small_vec(d):
    parts = []
    for nm, pw, w in _SMALL:
        v = d[nm].reshape(1, -1).astype(F32)
        parts.append(jnp.pad(v[:, :min(v.shape[1], pw)], ((0, 0), (0, pw - min(v.shape[1], pw)))))
    return jnp.concatenate(parts, axis=1)


def _small_split(vec):
    out, o = {}, 0
    for nm, pw, w in _SMALL:
        out[nm] = vec[0, o:o + w]
        o += pw
    return out


def kernel(x, p, positions, g_mix, w_in, conv_w, conv_b, dt_bias, a_log, d_skip, g_ssd, sinks, w_attn_br, w_ssd_br, w_o, g_ffn, w_gate, w_up, w_down, g_ple, w_ple_gate, w_ple_proj, g_final, loss_target, m_g_mix, m_w_in, m_conv_w, m_conv_b, m_dt_bias, m_a_log, m_d_skip, m_g_ssd, m_sinks, m_w_attn_br, m_w_ssd_br, m_w_o, m_g_ffn, m_w_gate, m_w_up, m_w_down, m_g_ple, m_w_ple_gate, m_w_ple_proj, m_g_final, v_g_mix, v_w_in, v_conv_w, v_conv_b, v_dt_bias, v_a_log, v_d_skip, v_g_ssd, v_sinks, v_w_attn_br, v_w_ssd_br, v_w_o, v_g_ffn, v_w_gate, v_w_up, v_w_down, v_g_ple, v_w_ple_gate, v_w_ple_proj, v_g_final):
    T = x.shape[1]
    D = D_MODEL
    W = dict(g_mix=g_mix, w_in=w_in, conv_w=conv_w, conv_b=conv_b, dt_bias=dt_bias, a_log=a_log, d_skip=d_skip, g_ssd=g_ssd,
             sinks=sinks, w_attn_br=w_attn_br, w_ssd_br=w_ssd_br, w_o=w_o, g_ffn=g_ffn, w_gate=w_gate, w_up=w_up, w_down=w_down,
             g_ple=g_ple, w_ple_gate=w_ple_gate, w_ple_proj=w_ple_proj, g_final=g_final)
    Mo = dict(g_mix=m_g_mix, w_in=m_w_in, conv_w=m_conv_w, conv_b=m_conv_b, dt_bias=m_dt_bias, a_log=m_a_log, d_skip=m_d_skip,
              g_ssd=m_g_ssd, sinks=m_sinks, w_attn_br=m_w_attn_br, w_ssd_br=m_w_ssd_br, w_o=m_w_o, g_ffn=m_g_ffn, w_gate=m_w_gate,
              w_up=m_w_up, w_down=m_w_down, g_ple=m_g_ple, w_ple_gate=m_w_ple_gate, w_ple_proj=m_w_ple_proj, g_final=m_g_final)
    Vo = dict(g_mix=v_g_mix, w_in=v_w_in, conv_w=v_conv_w, conv_b=v_conv_b, dt_bias=v_dt_bias, a_log=v_a_log, d_skip=v_d_skip,
              g_ssd=v_g_ssd, sinks=v_sinks, w_attn_br=v_w_attn_br, w_ssd_br=v_w_ssd_br, w_o=v_w_o, g_ffn=v_g_ffn, w_gate=v_w_gate,
              w_up=v_w_up, w_down=v_w_down, g_ple=v_g_ple, w_ple_gate=v_w_ple_gate, w_ple_proj=v_w_ple_proj, g_final=v_g_final)
    order = ["g_mix", "w_in", "conv_w", "conv_b", "dt_bias", "a_log", "d_skip", "g_ssd", "sinks", "w_attn_br", "w_ssd_br", "w_o",
             "g_ffn", "w_gate", "w_up", "w_down", "g_ple", "w_ple_gate", "w_ple_proj", "g_final"]
    big = ["w_in", "conv_w", "w_attn_br", "w_ssd_br", "w_o", "w_gate", "w_up", "w_down", "w_ple_gate", "w_ple_proj"]

    x2 = x.reshape(T, D)
    p2 = p.reshape(T, PLE_DIM)
    tgt = loss_target.reshape(T, D)
    posf = positions.reshape(T, 1).astype(F32)
    inv = ROPE_THETA ** (-np.arange(HEAD_DIM // 2, dtype=np.float32) * 2.0 / HEAD_DIM)
    inv128 = jnp.asarray(np.tile(inv, 128 // (HEAD_DIM // 2)).reshape(1, 128).astype(np.float32))
    transposed = ("w_in", "w_gate", "w_up")

    def shard2d(a, n):
        a = a.reshape(a.shape[-2:])
        return a.T if n in transposed else a

    sh = {n: shard2d(W[n], n) for n in big}

    del _PENDING[:]
    me = 4 * lax.axis_index("x") + 2 * lax.axis_index("y") + lax.axis_index("c")
    groups = (("w_in",), ("conv_w", "w_attn_br", "w_ssd_br", "w_o"), ("w_gate", "w_up"), ("w_down",), ("w_ple_gate", "w_ple_proj"))
    send = {n: sh[n] if n == "conv_w" else sh[n].astype(BF16) for n in big}
    zones = [[lax.dynamic_update_index_in_dim(lax.empty((N_DEV,) + send[n].shape, send[n].dtype), send[n], me, 0) for n in grp]
             for grp in groups]
    ring_a = split_start("ring_a_start", "route", ((4, 0, 0), (2, 0, 1)), [], lands=zones[0])
    ring_b = split_start("ring_b_start", "route", ((4, 0, 1), (2, 0, 0)), [], lands=ring_a["lands"], after=ring_a["token"])
    gathered, fwd = {}, {}

    def forward_start(gi, after, lands=None):
        if lands is None:
            _, lands = split_wait("gather_wait_%d" % gi, started[gi], after)
        fwd[gi] = split_start("forward_start_%d" % gi, "forward", FORWARD_BLOCKS, [], lands=lands)

    def forward_wait(gi, after):
        _, full = split_wait("forward_wait_%d" % gi, fwd[gi], after)
        gathered.update(zip(groups[gi], full))

    u = rms_fwd("norm_mix", x2, g_mix)
    _, zone = split_wait("ring_a_wait", dict(ring_a, lands=ring_b["lands"]), u)
    ring_c = split_start("ring_c_start", "route", ((2, 4, 0), (4, 2, 1)), [], lands=zone)
    started, prev = [None], ring_c["token"]
    for gi in range(1, len(groups)):
        started.append(split_start("gather_start_%d" % gi, "gather", ICI_SAME_CORE, [], lands=zones[gi], after=prev))
        prev = started[-1]["token"]
    _, zone = split_wait("ring_b_wait", dict(ring_b, lands=ring_c["lands"]), u)
    _, zone = split_wait("ring_c_wait", dict(ring_c, lands=zone), u)
    forward_start(0, u, lands=zone)
    forward_wait(0, u)
    winp = _win_to_padded(gathered["w_in"])
    dtb = _pad128(dt_bias)
    alog = _pad128(a_log)
    dfull = jnp.repeat(d_skip.reshape(SSM_HEADS), HEAD_DIM).reshape(1, D_INNER)

    projp = mm_nt("in_proj", u, winp, 640)
    forward_start(1, projp)
    attn, qr, kr = attn_fwd(projp, posf, inv128, sinks)
    forward_wait(1, attn)
    convw = jnp.transpose(gathered["conv_w"], (1, 0, 2)).reshape(CONV_WIDTH, CONV_DIM)
    wab = gathered["w_attn_br"]
    wsb = gathered["w_ssd_br"].reshape(D, D)
    wo = gathered["w_o"].reshape(D, D)
    xbc = conv_fwd(projp, convw, conv_b)
    y, states = ssd_fwd(xbc, projp, dtb, alog, dfull)
    yn = gnorm_fwd(y, projp, g_ssd)
    out_a = mm_nn_colblk("attn_br", attn, wab)
    out_s = mm_nn("ssd_br", yn, wsb, 512)
    forward_start(2, out_s)
    merged = merge_fwd(projp, out_a, out_s)
    h1 = mm_nn("o_proj", merged, wo, 512, residual=x2)
    f = rms_fwd("norm_ffn", h1, g_ffn)
    forward_wait(2, f)
    forward_start(3, f)
    wgt, wut = (gathered[n].reshape(FFN_HIDDEN, D) for n in ("w_gate", "w_up"))
    gate, up, act = ffn_up(f, wgt, wut)
    forward_wait(3, act)
    forward_start(4, act)
    wd = gathered["w_down"].reshape(FFN_HIDDEN, D)
    h2 = mm_nn_red("ffn_down", act, wd, 512, FFN_HIDDEN, residual=h1)
    r = rms_fwd("norm_ple", h2, g_ple)
    forward_wait(4, r)
    wpg = gathered["w_ple_gate"].reshape(D, D)
    wpp = gathered["w_ple_proj"]
    pg = mm_nn("ple_gate", r, wpg, 512)
    pp = mm_nn_colblk("ple_proj", p2, wpp)
    loss_v, dh3, dpg, dpp, dg_final = head_fwd_bwd(h2, pg, pp, g_final.reshape(1, D), tgt)

    gw = {}
    scat = []

    def scatter_start(names):
        scat.append((names, split_start("scatter_start_%d" % len(scat), "scatter", ALL_PEERS, [gw[n] for n in names])))

    gw["w_ple_proj"] = mm_tn_colblk("dw_ple_proj", p2, dpp, PLE_DIM)
    dr = mm_nt("d_ple_gate", dpg, wpg, 512)
    gw["w_ple_gate"] = mm_tn("dw_ple_gate", r, dpg, 512, D).reshape(N_DEV, D // N_DEV, D)
    scatter_start(("w_ple_proj", "w_ple_gate"))
    dh2, dh2b, dg_ple = rms_bwd("norm_ple_bwd", h2, g_ple, dr, dh3)
    dgate, dup = ffn_down_bwd(dh2b, wd, gate, up)
    per = FFN_HIDDEN // N_DEV
    gw["w_down"] = mm_tn("dw_down", act, dh2b, FFN_TILE, D).reshape(N_DEV, per, D)
    gw["w_gate"] = mm_tn("dw_gate", dgate, f, FFN_TILE, D).reshape(N_DEV, per, D)
    gw["w_up"] = mm_tn("dw_up", dup, f, FFN_TILE, D).reshape(N_DEV, per, D)
    scatter_start(("w_down", "w_gate", "w_up"))
    df = ffn_up_bwd(dgate, dup, wgt, wut)
    dh1, dh1b, dg_ffn = rms_bwd("norm_ffn_bwd", h1, g_ffn, df, dh2)
    dmerged = mm_nt("d_o_proj", dh1b, wo, 512)
    gw["w_o"] = mm_tn("dw_o", merged, dh1b, 512, D).reshape(N_DEV, D // N_DEV, D)
    dout_a, dout_s, dga, dgs = merge_bwd(projp, out_a, out_s, dmerged)
    gw["w_ssd_br"] = mm_tn("dw_ssd_br", yn, dout_s, 512, D).reshape(N_DEV, D // N_DEV, D)
    gw["w_attn_br"] = mm_tn_colblk("dw_attn_br", attn, dout_a, D // N_DEV)
    scatter_start(("w_o", "w_ssd_br", "w_attn_br"))
    dyn = mm_nt("d_ssd_br", dout_s, wsb, 512)
    dattn = attn_br_bwd(dout_a, wab)
    dy, dz, dg_ssd = gnorm_bwd(y, projp, g_ssd, dyn)
    dxs, dbm, dcm, ddt, dal, ddsk, ddtb = ssd_bwd(xbc, projp, dtb, alog, dfull, states, dy)
    dx_x, dwc_x, dbc_x = conv_bwd("conv_bwd_x", projp, dxs, convw, conv_b, 0)
    dx_b, dwc_b, dbc_b = conv_bwd("conv_bwd_b", projp, dbm, convw, conv_b, D_INNER)
    dx_c, dwc_c, dbc_c = conv_bwd("conv_bwd_c", projp, dcm, convw, conv_b, D_INNER + SSM_GROUPS * D_STATE)
    dq, dk, dv, dsk = attn_bwd(qr, kr, projp, dattn, posf, inv128, sinks)
    dproj = jnp.concatenate([dz, dga, dgs, dx_x, dx_b, dx_c, dq, dk.astype(BF16), dv.astype(BF16), ddt], axis=1)
    gw_in = _padded_to_win(mm_tn("dw_in", dproj, u, 640, D))
    pair = split_start("pair_start", "pair", FORWARD_BLOCKS, [gw_in])
    dconvw = jnp.concatenate([dwc_x, dwc_b, dwc_c], axis=1)
    gw["conv_w"] = jnp.transpose(dconvw.reshape(CONV_WIDTH, N_DEV, CONV_DIM // N_DEV), (1, 0, 2))
    scatter_start(("conv_w",))
    du_tm = min(512, T)
    tiles = T // du_tm
    first = max(tiles // 4, 1)
    du = mm_nn_red("d_in_proj_a", dproj, winp, 512, IN_PAD, rows=(0, first), tm=du_tm)
    (gw_in,), (sibling_part,) = split_wait("pair_wait", pair, du)
    pair_slots = jnp.stack([jnp.bitwise_xor(me, k) for k in FORWARD_BLOCKS]).astype(jnp.int32)
    core = split_start("core_start", "scatter_core", ICI_SAME_CORE, [pair_sum("pair_sum_w_in", gw_in, pair_slots, sibling_part)])
    if first < tiles:
        du = mm_nn_red("d_in_proj_b", dproj, winp, 512, IN_PAD, rows=(first, tiles - first), prev=du, tm=du_tm)
    gx, _, dg_mix = rms_bwd("norm_mix_bwd", x2, g_mix, du, dh1)

    small_g = dict(loss=loss_v[:, :1], g_mix=dg_mix, conv_b=jnp.concatenate([dbc_x, dbc_b, dbc_c], axis=1), dt_bias=ddtb,
                   a_log=dal, d_skip=ddsk, g_ssd=dg_ssd, sinks=dsk, g_ffn=dg_ffn, g_ple=dg_ple, g_final=dg_final)
    vec = _small_vec(small_g)
    small = split_start("small_start", "gather", ALL_PEERS, [],
                        lands=[lax.dynamic_update_index_in_dim(lax.empty((N_DEV,) + vec.shape, F32), vec, me, 0)])

    res = {}
    after = [gx]
    for si, (names, h) in enumerate(scat):
        srcs, lands = split_wait("scatter_wait_%d" % si, h, after)
        for n, mine, arrived in zip(names, srcs, lands):
            res[n] = adamw("adamw_" + n, arrived, sh[n], shard2d(Mo[n], n), shard2d(Vo[n], n), own=mine, own_slot=me)
        after = [res[n][0] for n in names]
    zero = jnp.zeros((1, 1), F32)
    _, (vec_parts,) = split_wait("small_wait", small, [res[n][0] for n in res])
    sres = adamw("adamw_small", vec_parts, _small_vec({**W, "loss": zero}), _small_vec({**Mo, "loss": zero}),
                 _small_vec({**Vo, "loss": zero}))
    ssplit = [_small_split(a) for a in sres]

    (pair_sums,), (arrived,) = split_wait("core_wait", core, [sres[0]])
    res["w_in"] = adamw("adamw_w_in", arrived, sh["w_in"], shard2d(Mo["w_in"], "w_in"), shard2d(Vo["w_in"], "w_in"),
                        own=pair_sums, own_slot=0)
    loss = ssplit[0]["loss"].reshape(())
    for n in order:
        if n not in res:
            res[n] = tuple(s[n].reshape(W[n].shape) for s in ssplit)
        else:
            res[n] = tuple((a.T if n in transposed else a).reshape(W[n].shape) for a in res[n])
    outs = [loss, gx.reshape(x.shape)]
    for k in range(4):
        outs += [res[n][k] for n in order]
    return tuple(outs)
```

```python
import functools

import numpy as np
import jax
import jax.numpy as jnp
from jax import lax
from jax.experimental import pallas as pl
from jax.experimental.pallas import tpu as pltpu

F32 = jnp.float32
BF16 = jnp.bfloat16

N_DEV = 8
D_MODEL = 2048
HEAD_DIM = 64
ATTN_HEADS = 16
KV_HEADS = 4
Q_DIM = 1024
KV_DIM = 256
BLK = 128
D_INNER = 2048
SSM_HEADS = 32
SSM_GROUPS = 4
HEADS_PER_GROUP = 8
D_STATE = 128
CONV_WIDTH = 4
CONV_DIM = 3072
FFN_HIDDEN = 5632
PLE_DIM = 256
IN_DIM = 10784
NORM_EPS = 1e-6
SSM_NORM_EPS = 1e-5
ROPE_THETA = 10000.0

OFF_Z, OFF_GA, OFF_GS, OFF_XBC, OFF_Q, OFF_K, OFF_V, OFF_DT = 0, 2048, 4096, 6144, 9216, 10240, 10496, 10752
IN_PAD = 10880
DT_PAD = 128
SEG = dict(q=(0, 1024), k=(1024, 256), v=(1280, 256), z=(1536, 2048), xbc=(3584, 3072), dt=(6656, 32),
           ga=(6688, 2048), gs=(8736, 2048))

ADAM_LR, ADAM_B1, ADAM_B2, ADAM_EPS, ADAM_WD, ADAM_STEP = 0.001, 0.9, 0.999, 1e-08, 0.01, 10

VMEM_LIMIT = 56 * 1024 * 1024

NN = (((1,), (0,)), ((), ()))
NT = (((1,), (1,)), ((), ()))
TN = (((0,), (0,)), ((), ()))


_PENDING = []


def _raw_call(body, **kw):
    return pl.pallas_call(body, **kw)


def _pcall(body, **kw):
    if "in_specs" not in kw:
        return _raw_call(body, **kw)
    deps = list(_PENDING)
    del _PENDING[:]
    if not deps:
        return _raw_call(body, **kw)
    n_in = len(kw["in_specs"])

    def tied(*refs):
        return body(*refs[:n_in], *refs[n_in + len(deps):])

    kw["in_specs"] = list(kw["in_specs"]) + [pl.BlockSpec(memory_space=pl.ANY)] * len(deps)
    call = _raw_call(tied, **kw)
    return lambda *ops: call(*ops, *deps)


def _cparams(sem=None):
    if sem is None:
        return pltpu.CompilerParams(vmem_limit_bytes=VMEM_LIMIT)
    return pltpu.CompilerParams(vmem_limit_bytes=VMEM_LIMIT, dimension_semantics=sem)


def _dot(a, b, dn):
    return lax.dot_general(a.astype(BF16), b.astype(BF16), dn, preferred_element_type=F32)


def _sigmoid(x):
    return 1.0 / (1.0 + jnp.exp(-x))


def _silu(x):
    return x * _sigmoid(x)


def _dsilu(x):
    s = _sigmoid(x)
    return s * (1.0 + x * (1.0 - s))


def _matmul(name, pairs, pair_specs, dn, grid, out_shapes, out_specs, nred=1, extra=(), extra_specs=(),
            epilogue=None, acc_shape=None, alias=None):
    n_in = 2 * len(pairs) + len(extra)
    n_out = len(out_shapes)

    def body(*refs):
        ins = refs[:2 * len(pairs)]
        ex = [r for r, sp in zip(refs[2 * len(pairs):n_in], extra_specs) if sp.memory_space != pl.ANY]
        outs = refs[n_in:n_in + n_out]

        def prod():
            s = None
            for p in range(len(pairs)):
                d = _dot(ins[2 * p][...], ins[2 * p + 1][...], dn)
                s = d if s is None else s + d
            return s

        def finish(val):
            if epilogue is None:
                outs[0][...] = val.astype(outs[0].dtype)
            else:
                res = epilogue(val, *[e[...] for e in ex])
                for o, r in zip(outs, res):
                    o[...] = r.astype(o.dtype)

        if nred == 1:
            finish(prod())
        else:
            acc = refs[n_in + n_out]
            k = pl.program_id(len(grid) - 1)

            @pl.when(k == 0)
            def _():
                acc[...] = jnp.zeros_like(acc)

            acc[...] += prod()

            @pl.when(k == nred - 1)
            def _():
                finish(acc[...])

    operands = []
    specs = []
    for (a, b), (sa, sb) in zip(pairs, pair_specs):
        operands += [a, b]
        specs += [sa, sb]
    operands += list(extra)
    specs += list(extra_specs)
    scratch = [pltpu.VMEM(acc_shape, F32)] if nred > 1 else []
    sem = ("arbitrary",) * len(grid)
    res = _pcall(body, name=name, grid=grid, in_specs=specs, out_specs=list(out_specs), input_output_aliases=dict(alias or {}),
                 out_shape=list(out_shapes), scratch_shapes=scratch, compiler_params=_cparams(sem))(*operands)
    return res


def _sds(shape, dtype):
    return jax.ShapeDtypeStruct(shape, dtype)


def _row_tile(T):
    return min(1024, T)


def mm_nn(name, a, b, tn, out_dtype=F32, residual=None):
    M, K = a.shape
    N = b.shape[1]
    tm = _row_tile(M)
    grid = (M // tm, N // tn)
    extra, especs, epi = (), (), None
    if residual is not None:
        extra = (residual,)
        especs = (pl.BlockSpec((tm, tn), lambda i, n: (i, n)),)
        epi = lambda v, r: (v + r,)
    return _matmul(name, [(a, b)], [(pl.BlockSpec((tm, K), lambda i, n: (i, 0)), pl.BlockSpec((K, tn), lambda i, n: (0, n)))],
                   NN, grid, [_sds((M, N), out_dtype)], [pl.BlockSpec((tm, tn), lambda i, n: (i, n))],
                   extra=extra, extra_specs=especs, epilogue=epi)[0]


def mm_nn_colblk(name, a, b, out_dtype=F32):
    M, K = a.shape
    J, _, nb = b.shape
    tm = _row_tile(M)
    grid = (M // tm, J)
    return _matmul(name, [(a, b)], [(pl.BlockSpec((tm, K), lambda i, j: (i, 0)), pl.BlockSpec((None, K, nb), lambda i, j: (j, 0, 0)))],
                   NN, grid, [_sds((M, J * nb), out_dtype)], [pl.BlockSpec((tm, nb), lambda i, j: (i, j))])[0]


def mm_nt(name, a, w, tr, out_dtype=F32):
    M, C = a.shape
    R = w.shape[0]
    tm = _row_tile(M)
    grid = (M // tm, R // tr)
    return _matmul(name, [(a, w)], [(pl.BlockSpec((tm, C), lambda i, r: (i, 0)), pl.BlockSpec((tr, C), lambda i, r: (r, 0)))],
                   NT, grid, [_sds((M, R), out_dtype)], [pl.BlockSpec((tm, tr), lambda i, r: (i, r))])[0]


def mm_nt_red(name, a, w, tr, tk, out_dtype=F32):
    M, C = a.shape
    R = w.shape[0]
    tm = _row_tile(M)
    nk = C // tk
    grid = (M // tm, R // tr, nk)
    return _matmul(name, [(a, w)], [(pl.BlockSpec((tm, tk), lambda i, r, k: (i, k)), pl.BlockSpec((tr, tk), lambda i, r, k: (r, k)))],
                   NT, grid, [_sds((M, R), out_dtype)], [pl.BlockSpec((tm, tr), lambda i, r, k: (i, r))],
                   nred=nk, acc_shape=(tm, tr))[0]


def mm_nn_red(name, a, b, tn, tk, out_dtype=F32, residual=None, rows=None, prev=None, tm=None):
    M, K = a.shape
    N = b.shape[1]
    tm = min(tm or _row_tile(M), M)
    nk = K // tk
    i0, ni = (0, M // tm) if rows is None else rows
    grid = (ni, N // tn, nk)
    ospec = pl.BlockSpec((tm, tn), lambda i, n, k: (i + i0, n))
    extra, especs, epi = [], [], None
    if residual is not None:
        extra, especs, epi = [residual], [ospec], (lambda v, r, *_: (v + r,))
    alias = {}
    if prev is not None:
        alias = {2 + len(extra): 0}
        extra, especs = extra + [prev], especs + [_ANY]
        epi = epi or (lambda v, *_: (v,))
    return _matmul(name, [(a, b)], [(pl.BlockSpec((tm, tk), lambda i, n, k: (i + i0, k)), pl.BlockSpec((tk, tn), lambda i, n, k: (k, n)))],
                   NN, grid, [_sds((M, N), out_dtype)], [ospec], nred=nk, acc_shape=(tm, tn),
                   extra=extra, extra_specs=especs, epilogue=epi, alias=alias)[0]


def mm_tn(name, x, dy, tr, tc, out_dtype=BF16):
    M, R = x.shape
    C = dy.shape[1]
    grid = (R // tr, C // tc)
    return _matmul(name, [(x, dy)], [(pl.BlockSpec((M, tr), lambda r, c: (0, r)), pl.BlockSpec((M, tc), lambda r, c: (0, c)))],
                   TN, grid, [_sds((R, C), out_dtype)], [pl.BlockSpec((tr, tc), lambda r, c: (r, c))])[0]


def mm_tn_colblk(name, x, dy, nb, out_dtype=BF16):
    M, R = x.shape
    J = dy.shape[1] // nb
    grid = (J,)
    return _matmul(name, [(x, dy)], [(pl.BlockSpec((M, R), lambda j: (0, 0)), pl.BlockSpec((M, nb), lambda j: (0, j)))],
                   TN, grid, [_sds((J, R, nb), out_dtype)], [pl.BlockSpec((None, R, nb), lambda j: (j, 0, 0))])[0]


def _rows(T):
    return min(256, T)


def rms_fwd(name, x, g, eps=NORM_EPS):
    T, D = x.shape
    tm = _rows(T)

    def body(x_ref, g_ref, o_ref):
        xv = x_ref[...]
        r = lax.rsqrt(jnp.mean(xv * xv, axis=-1, keepdims=True) + eps)
        o_ref[...] = (xv * r * g_ref[...]).astype(BF16)

    return _pcall(body, name=name, grid=(T // tm,),
                  in_specs=[pl.BlockSpec((tm, D), lambda i: (i, 0)), pl.BlockSpec((1, D), lambda i: (0, 0))],
                  out_specs=pl.BlockSpec((tm, D), lambda i: (i, 0)), out_shape=_sds((T, D), BF16),
                  compiler_params=_cparams(("arbitrary",)))(x, g)


def rms_bwd(name, x, g, dy, dres, eps=NORM_EPS):
    T, D = x.shape
    tm = _rows(T)

    def body(x_ref, g_ref, dy_ref, dr_ref, dx_ref, dxb_ref, dg_ref):
        i = pl.program_id(0)
        xv = x_ref[...]
        r = lax.rsqrt(jnp.mean(xv * xv, axis=-1, keepdims=True) + eps)
        xh = xv * r
        dyv = dy_ref[...]
        gd = dyv * g_ref[...]
        dx = r * (gd - xh * jnp.mean(gd * xh, axis=-1, keepdims=True)) + dr_ref[...]
        dx_ref[...] = dx
        dxb_ref[...] = dx.astype(BF16)

        @pl.when(i == 0)
        def _():
            dg_ref[...] = jnp.zeros_like(dg_ref)

        dg_ref[...] += jnp.sum(dyv * xh, axis=0, keepdims=True)

    row = pl.BlockSpec((tm, D), lambda i: (i, 0))
    vec = pl.BlockSpec((1, D), lambda i: (0, 0))
    return _pcall(body, name=name, grid=(T // tm,), in_specs=[row, vec, row, row], out_specs=[row, row, vec],
                  out_shape=[_sds((T, D), F32), _sds((T, D), BF16), _sds((1, D), F32)],
                  compiler_params=_cparams(("arbitrary",)))(x, g, dy, dres)


def gnorm_fwd(y, projp, g):
    T, D = y.shape
    tm = _rows(T)

    def body(y_ref, z_ref, g_ref, o_ref):
        yz = y_ref[...] * _silu(z_ref[...])
        r = lax.rsqrt(jnp.mean(yz * yz, axis=-1, keepdims=True) + SSM_NORM_EPS)
        o_ref[...] = (yz * r * g_ref[...]).astype(BF16)

    row = pl.BlockSpec((tm, D), lambda i: (i, 0))
    return _pcall(body, name="gnorm_fwd", grid=(T // tm,),
                  in_specs=[row, pl.BlockSpec((tm, D), lambda i: (i, OFF_Z // D)), pl.BlockSpec((1, D), lambda i: (0, 0))],
                  out_specs=row, out_shape=_sds((T, D), BF16), compiler_params=_cparams(("arbitrary",)))(y, projp, g)


def gnorm_bwd(y, projp, g, dyn):
    T, D = y.shape
    tm = _rows(T)

    def body(y_ref, z_ref, g_ref, dyn_ref, dy_ref, dz_ref, dg_ref):
        i = pl.program_id(0)
        yv, zv = y_ref[...], z_ref[...]
        sz = _silu(zv)
        yz = yv * sz
        r = lax.rsqrt(jnp.mean(yz * yz, axis=-1, keepdims=True) + SSM_NORM_EPS)
        xh = yz * r
        dv = dyn_ref[...]
        gd = dv * g_ref[...]
        dyz = r * (gd - xh * jnp.mean(gd * xh, axis=-1, keepdims=True))
        dy_ref[...] = dyz * sz
        dz_ref[...] = (dyz * yv * _dsilu(zv)).astype(BF16)

        @pl.when(i == 0)
        def _():
            dg_ref[...] = jnp.zeros_like(dg_ref)

        dg_ref[...] += jnp.sum(dv * xh, axis=0, keepdims=True)

    row = pl.BlockSpec((tm, D), lambda i: (i, 0))
    vec = pl.BlockSpec((1, D), lambda i: (0, 0))
    return _pcall(body, name="gnorm_bwd", grid=(T // tm,),
                  in_specs=[row, pl.BlockSpec((tm, D), lambda i: (i, OFF_Z // D)), vec, row], out_specs=[row, row, vec],
                  out_shape=[_sds((T, D), F32), _sds((T, D), BF16), _sds((1, D), F32)],
                  compiler_params=_cparams(("arbitrary",)))(y, projp, g, dyn)


def merge_fwd(projp, out_a, out_s):
    T, D = out_a.shape
    tm = _rows(T)

    def body(ga_ref, gs_ref, a_ref, s_ref, o_ref):
        o_ref[...] = (_sigmoid(ga_ref[...]) * a_ref[...] + _sigmoid(gs_ref[...]) * s_ref[...]).astype(BF16)

    row = pl.BlockSpec((tm, D), lambda i: (i, 0))
    return _pcall(body, name="merge_fwd", grid=(T // tm,),
                  in_specs=[pl.BlockSpec((tm, D), lambda i: (i, OFF_GA // D)), pl.BlockSpec((tm, D), lambda i: (i, OFF_GS // D)), row, row],
                  out_specs=row, out_shape=_sds((T, D), BF16), compiler_params=_cparams(("arbitrary",)))(projp, projp, out_a, out_s)


def merge_bwd(projp, out_a, out_s, dmerged):
    T, D = out_a.shape
    tm = _rows(T)

    def body(ga_ref, gs_ref, a_ref, s_ref, dm_ref, da_ref, ds_ref, dga_ref, dgs_ref):
        dm = dm_ref[...]
        sa, ss = _sigmoid(ga_ref[...]), _sigmoid(gs_ref[...])
        da_ref[...] = (dm * sa).astype(BF16)
        ds_ref[...] = (dm * ss).astype(BF16)
        dga_ref[...] = (dm * a_ref[...] * sa * (1.0 - sa)).astype(BF16)
        dgs_ref[...] = (dm * s_ref[...] * ss * (1.0 - ss)).astype(BF16)

    row = pl.BlockSpec((tm, D), lambda i: (i, 0))
    return _pcall(body, name="merge_bwd", grid=(T // tm,),
                  in_specs=[pl.BlockSpec((tm, D), lambda i: (i, OFF_GA // D)), pl.BlockSpec((tm, D), lambda i: (i, OFF_GS // D)), row, row, row],
                  out_specs=[row] * 4, out_shape=[_sds((T, D), BF16)] * 4,
                  compiler_params=_cparams(("arbitrary",)))(projp, projp, out_a, out_s, dmerged)


def head_fwd_bwd(h2, pg, pp, g_final, target):
    T, D = h2.shape
    tm = _rows(T)

    def body(h_ref, pg_ref, pp_ref, g_ref, t_ref, loss_ref, dh_ref, dpg_ref, dpp_ref, dg_ref):
        i = pl.program_id(0)
        s = _sigmoid(pg_ref[...])
        ppv = pp_ref[...]
        h3 = h_ref[...] + s * ppv
        r = lax.rsqrt(jnp.mean(h3 * h3, axis=-1, keepdims=True) + NORM_EPS)
        xh = h3 * r
        gv = g_ref[...]
        e = xh * gv - t_ref[...]
        dyo = e * (1.0 / D)
        gd = dyo * gv
        dh = r * (gd - xh * jnp.mean(gd * xh, axis=-1, keepdims=True))
        dh_ref[...] = dh
        dpg_ref[...] = (dh * ppv * s * (1.0 - s)).astype(BF16)
        dpp_ref[...] = (dh * s).astype(BF16)

        @pl.when(i == 0)
        def _():
            dg_ref[...] = jnp.zeros_like(dg_ref)
            loss_ref[...] = jnp.zeros_like(loss_ref)

        dg_ref[...] += jnp.sum(dyo * xh, axis=0, keepdims=True)
        part = 0.5 * jnp.sum(jnp.mean(e * e, axis=-1, keepdims=True), axis=0, keepdims=True)
        loss_ref[...] += jnp.broadcast_to(part, loss_ref.shape)

    row = pl.BlockSpec((tm, D), lambda i: (i, 0))
    vec = pl.BlockSpec((1, D), lambda i: (0, 0))
    return _pcall(body, name="head_fwd_bwd", grid=(T // tm,), in_specs=[row, row, row, vec, row],
                  out_specs=[pl.BlockSpec((1, 128), lambda i: (0, 0)), row, row, row, vec],
                  out_shape=[_sds((1, 128), F32), _sds((T, D), F32), _sds((T, D), BF16), _sds((T, D), BF16), _sds((1, D), F32)],
                  compiler_params=_cparams(("arbitrary",)))(h2, pg, pp, g_final, target)


FFN_TILE = 512


def ffn_up(f, wgt, wut):
    T, D = f.shape
    H = wgt.shape[0]
    tm = _row_tile(T)

    def body(f_ref, wg_ref, wu_ref, g_ref, u_ref, a_ref):
        fv = f_ref[...]
        g = _dot(fv, wg_ref[...], NT)
        u = _dot(fv, wu_ref[...], NT)
        g_ref[...] = g.astype(BF16)
        u_ref[...] = u.astype(BF16)
        a_ref[...] = (_silu(g) * u).astype(BF16)

    wspec = pl.BlockSpec((FFN_TILE, D), lambda i, j: (j, 0))
    ospec = pl.BlockSpec((tm, FFN_TILE), lambda i, j: (i, j))
    return _pcall(body, name="ffn_up", grid=(T // tm, H // FFN_TILE), in_specs=[pl.BlockSpec((tm, D), lambda i, j: (i, 0)), wspec, wspec],
                  out_specs=[ospec] * 3, out_shape=[_sds((T, H), BF16)] * 3,
                  compiler_params=_cparams(("arbitrary", "arbitrary")))(f, wgt, wut)


def ffn_down_bwd(dh2b, wd, gate, up):
    T, D = dh2b.shape
    H = wd.shape[0]
    tm = _row_tile(T)
    ospec = pl.BlockSpec((tm, FFN_TILE), lambda i, j: (i, j))

    def epi(da, g, u):
        g, u = g.astype(F32), u.astype(F32)
        return (da * u * _dsilu(g), da * _silu(g))

    return _matmul("ffn_down_bwd", [(dh2b, wd)],
                   [(pl.BlockSpec((tm, D), lambda i, j: (i, 0)), pl.BlockSpec((FFN_TILE, D), lambda i, j: (j, 0)))],
                   NT, (T // tm, H // FFN_TILE), [_sds((T, H), BF16)] * 2, [ospec, ospec],
                   extra=(gate, up), extra_specs=(ospec, ospec), epilogue=epi)


def ffn_up_bwd(dgate, dup, wgt, wut):
    T, H = dgate.shape
    D = wgt.shape[1]
    tm = min(512, T)
    tn = 512
    aspec = pl.BlockSpec((tm, H), lambda i, n: (i, 0))
    wspec = pl.BlockSpec((H, tn), lambda i, n: (0, n))
    return _matmul("ffn_up_bwd", [(dgate, wgt), (dup, wut)], [(aspec, wspec), (aspec, wspec)], NN, (T // tm, D // tn),
                   [_sds((T, D), F32)], [pl.BlockSpec((tm, tn), lambda i, n: (i, n))])[0]


def attn_br_bwd(dout_a, wab):
    T, D = dout_a.shape
    J, R, nb = wab.shape
    tm = _row_tile(T)
    return _matmul("attn_br_bwd", [(dout_a, wab)],
                   [(pl.BlockSpec((tm, nb), lambda i, j: (i, j)), pl.BlockSpec((None, R, nb), lambda i, j: (j, 0, 0)))],
                   NT, (T // tm, J), [_sds((T, R), BF16)], [pl.BlockSpec((tm, R), lambda i, j: (i, 0))], nred=J, acc_shape=(tm, R))[0]


def _adam_math(w, g, m, v):
    m2 = ADAM_B1 * m + (1.0 - ADAM_B1) * g
    v2 = ADAM_B2 * v + (1.0 - ADAM_B2) * (g * g)
    m_hat = m2 / (1.0 - ADAM_B1 ** ADAM_STEP)
    v_hat = v2 / (1.0 - ADAM_B2 ** ADAM_STEP)
    delta = -ADAM_LR * (m_hat / (jnp.sqrt(v_hat) + ADAM_EPS) + ADAM_WD * w)
    return delta, m2, v2


def _sum_partials(own, parts):
    g = None if own is None else own.astype(F32)
    if parts is not None:
        for s in range(parts.shape[0]):
            t = parts[s].astype(F32)
            g = t if g is None else g + t
    return g


def adamw(name, parts, w, m, v, own=None, own_slot=None):
    R, C = w.shape
    tr, tc = R, C
    for cand in (256, 176, 128, 64, 32, 16, 8):
        if R % cand == 0 and R > cand:
            tr = cand
            break
    if tr == R and R > 256:
        tc = 256
    given = [a for a in (parts, own) if a is not None]
    pre = own_slot is not None

    def body(*refs):
        refs = refs[1:] if pre else refs
        p_ref = refs[0] if parts is not None else None
        o_ref = refs[len(given) - 1] if own is not None else None
        w_ref, m_ref, v_ref, g_ref, d_ref, m2_ref, v2_ref = refs[-7:]
        g = _sum_partials(None if o_ref is None else o_ref[...], p_ref)
        d, m2, v2 = _adam_math(w_ref[...], g, m_ref[...], v_ref[...])
        g_ref[...] = g
        d_ref[...] = d
        m2_ref[...] = m2
        v2_ref[...] = v2

    blk = pl.BlockSpec((tr, tc), lambda i, j, *s: (i, j))
    specs = [] if parts is None else [pl.BlockSpec((parts.shape[0], tr, tc), lambda i, j, *s: (0, i, j))]
    if own is not None:
        specs.append(pl.BlockSpec((None, tr, tc), lambda i, j, s: (s[0], i, j)) if pre else blk)
    specs += [blk] * 3
    grid = (R // tr, C // tc)
    out_shape = [_sds((R, C), F32)] * 4
    params = _cparams(("arbitrary", "arbitrary"))
    if not pre:
        return _pcall(body, name=name, grid=grid, in_specs=specs, out_specs=[blk] * 4, out_shape=out_shape,
                      compiler_params=params)(*given, w, m, v)
    spec = pltpu.PrefetchScalarGridSpec(num_scalar_prefetch=1, grid=grid, in_specs=specs, out_specs=[blk] * 4)
    return _pcall(body, name=name, grid_spec=spec, out_shape=out_shape,
                  compiler_params=params)(jnp.asarray(own_slot, jnp.int32).reshape(1), *given, w, m, v)


_HBM = pl.BlockSpec(memory_space=pltpu.HBM)
_SEM = pl.BlockSpec(memory_space=pltpu.SEMAPHORE)
_ANY = pl.BlockSpec(memory_space=pl.ANY)
_SPLIT_PARAMS = dict(compiler_params=pltpu.CompilerParams(has_side_effects=pltpu.SideEffectType.DATAFLOW_SIDE_EFFECTING))
ICI_SAME_CORE = (2, 4, 6)
ALL_PEERS = (1, 2, 3, 4, 5, 6, 7)
LAND_SLOTS = {"gather": N_DEV, "scatter": N_DEV - 1, "pair": 4, "scatter_core": 3}


def _mesh_pos():
    x, y, c = lax.axis_index("x"), lax.axis_index("y"), lax.axis_index("c")
    return x, y, c, 4 * x + 2 * y + c


def _peer_of(k, x, y, c):
    px = 1 - x if k & 4 else x
    py = 1 - y if k & 2 else y
    pc = 1 - c if k & 1 else c
    return (px, py, pc), 4 * px + 2 * py + pc


def _split_copies(mode, ks, srcs, lands, send_sems, recv_sems):
    x, y, c, me = _mesh_pos()
    pairs = []
    for a in range(len(lands)):
        for j, k in enumerate(ks):
            dev, peer = _peer_of(k if mode != "route" else k[0], x, y, c)
            i = a * len(ks) + j
            if mode == "gather":
                s_out, d_out, d_in = lands[a].at[me], lands[a].at[me], lands[a].at[peer]
            elif mode == "scatter":
                s_out, d_out, d_in = srcs[a].at[peer], lands[a].at[k - 1], lands[a].at[k - 1]
            elif mode == "pair":
                dev, _ = _peer_of(1, x, y, c)
                _, theirs = _peer_of(k | 1, x, y, c)
                s_out, d_out, d_in = srcs[a].at[theirs], lands[a].at[j], lands[a].at[j]
            elif mode == "scatter_core":
                s_out, d_out, d_in = srcs[a].at[j + 1], lands[a].at[j], lands[a].at[j]
            elif mode == "route":
                k, rel, half = k
                dev, _ = _peer_of(k, x, y, c)
                _, held = _peer_of(rel, x, y, c)
                _, theirs = _peer_of(k ^ rel, x, y, c)
                cols = lands[a].shape[-1] // 2
                cut = (slice(None), slice(None)) if half is None else (slice(None), pl.ds(half * cols, cols))
                s_out, d_out, d_in = lands[a].at[held].at[cut], lands[a].at[held].at[cut], lands[a].at[theirs].at[cut]
            else:
                dev, _ = _peer_of(1, x, y, c)
                _, theirs = _peer_of(k | 1, x, y, c)
                s_out, d_out, d_in = lands[a].at[peer], lands[a].at[peer], lands[a].at[theirs]
            both = [pltpu.make_async_remote_copy(src_ref=s_out, dst_ref=d, send_sem=send_sems.at[i], recv_sem=recv_sems.at[i],
                                                 device_id=dev, device_id_type=pl.DeviceIdType.MESH) for d in (d_out, d_in)]
            pairs.append(tuple(both))
    return pairs


def split_start(name, mode, ks, srcs, lands=None, after=None):
    n, nk = len(srcs) if lands is None else len(lands), len(ks)
    srcs = [pltpu.with_memory_space_constraint(s, pltpu.HBM) for s in srcs]
    if lands is None:
        shapes = [((N_DEV,) + s.shape) if mode == "gather" else ((LAND_SLOTS[mode],) + s.shape[1:]) for s in srcs]
        lands = [lax.empty(shp, s.dtype) for shp, s in zip(shapes, srcs)]
    lands = [pltpu.with_memory_space_constraint(l, pltpu.HBM) for l in lands]
    both = srcs + lands
    extra = [] if after is None else [after]

    def body(*refs):
        src_refs, land_refs = refs[:len(srcs)], refs[len(srcs):len(both)]
        send_sems, recv_sems = refs[len(both) + len(extra)], refs[len(both) + len(extra) + 1]
        token = refs[-1]
        for out, _ in _split_copies(mode, ks, src_refs, land_refs, send_sems, recv_sems):
            out.start()
        token[...] = jnp.zeros_like(token)

    out_shape = (pltpu.SemaphoreType.DMA((n * nk,)), pltpu.SemaphoreType.DMA((n * nk,)),
                 *[pltpu.HBM(a.shape, a.dtype) for a in both], _sds((8, 128), F32))
    res = _raw_call(body, name=name, out_shape=out_shape, in_specs=[_HBM] * len(both) + [_ANY] * len(extra),
                    out_specs=(_SEM, _SEM, *[_HBM] * len(both), pl.BlockSpec(memory_space=pltpu.VMEM)),
                    input_output_aliases={i: 2 + i for i in range(len(both))}, **_SPLIT_PARAMS)(*both, *extra)
    _PENDING.append(res[-1])
    return dict(mode=mode, ks=ks, sems=(res[0], res[1]), srcs=list(res[2:2 + len(srcs)]),
                lands=list(res[2 + len(srcs):2 + len(both)]), token=res[-1])


def split_wait(name, h, after):
    ns = len(h["srcs"])
    both = h["srcs"] + h["lands"]
    after = list(after) if isinstance(after, (list, tuple)) else [after]

    def body(*refs):
        src_refs, land_refs = refs[:ns], refs[ns:len(both)]
        send_sems, recv_sems = refs[len(both)], refs[len(both) + 1]
        for out, arriving in _split_copies(h["mode"], h["ks"], src_refs, land_refs, send_sems, recv_sems):
            out.wait_send()
            arriving.wait_recv()

    res = _raw_call(body, name=name, out_shape=tuple(pltpu.HBM(a.shape, a.dtype) for a in both),
                    in_specs=[_HBM] * len(both) + [_SEM, _SEM] + [_ANY] * len(after), out_specs=tuple([_HBM] * len(both)),
                    input_output_aliases={i: i for i in range(len(both))}, **_SPLIT_PARAMS)(*both, *h["sems"], *after)
    return list(res[:ns]), list(res[ns:])


FORWARD_BLOCKS = (0, 2, 4, 6)


def pair_sum(name, mine, slots, theirs):
    P, R, C = theirs.shape
    tc = 512

    def body(s_ref, a_ref, b_ref, o_ref):
        o_ref[...] = (a_ref[...].astype(F32) + b_ref[...].astype(F32)).astype(o_ref.dtype)

    blk = pl.BlockSpec((None, R, tc), lambda p, i, s: (p, 0, i))
    spec = pltpu.PrefetchScalarGridSpec(num_scalar_prefetch=1, grid=(P, C // tc),
                                        in_specs=[pl.BlockSpec((None, R, tc), lambda p, i, s: (s[p], 0, i)), blk], out_specs=blk)
    return _pcall(body, name=name, grid_spec=spec, out_shape=_sds((P, R, C), theirs.dtype),
                  compiler_params=_cparams(("arbitrary", "arbitrary")))(slots, mine, theirs)


def _rope_parts(pos_ref, inv_ref):
    ang = pos_ref[...] * inv_ref[...]
    return jnp.cos(ang), jnp.sin(ang)


def _rot_half(t):
    lane = lax.broadcasted_iota(jnp.int32, t.shape, 1)
    return jnp.where((lane % HEAD_DIM) < HEAD_DIM // 2, -pltpu.roll(t, 128 - HEAD_DIM // 2, 1), pltpu.roll(t, HEAD_DIM // 2, 1))


def _attn_mask(n):
    row = lax.broadcasted_iota(jnp.int32, (BLK, 2 * BLK), 0)
    col = lax.broadcasted_iota(jnp.int32, (BLK, 2 * BLK), 1)
    dist = row + BLK - col
    return (dist >= 0) & (dist < BLK) & ((n * BLK - BLK + col) >= 0)


def _attn_specs(T):
    prev = lambda n: jnp.maximum(n - 1, 0)
    kc = pl.BlockSpec((BLK, KV_DIM), lambda n: (n, OFF_K // KV_DIM))
    kp = pl.BlockSpec((BLK, KV_DIM), lambda n: (prev(n), OFF_K // KV_DIM))
    vc = pl.BlockSpec((BLK, KV_DIM), lambda n: (n, OFF_V // KV_DIM))
    vp = pl.BlockSpec((BLK, KV_DIM), lambda n: (prev(n), OFF_V // KV_DIM))
    pc = pl.BlockSpec((BLK, 1), lambda n: (n, 0))
    pp = pl.BlockSpec((BLK, 1), lambda n: (prev(n), 0))
    inv = pl.BlockSpec((1, 128), lambda n: (0, 0))
    sink = pl.BlockSpec(memory_space=pltpu.SMEM)
    return kc, kp, vc, vp, pc, pp, inv, sink


def _softmax_sink(sc, valid, sink):
    sc = jnp.where(valid, sc * (HEAD_DIM ** -0.5), -1e30)
    m = jnp.maximum(jnp.max(sc, axis=1, keepdims=True), sink)
    e = jnp.exp(sc - m)
    es = jnp.exp(sink - m)
    den = jnp.sum(e, axis=1, keepdims=True) + es
    return e / den, es / den


def attn_fwd(projp, posf, inv128, sinks):
    T = projp.shape[0]
    kc, kp, vc, vp, pc, pp, inv, sink = _attn_specs(T)

    def body(q_ref, kc_ref, kp_ref, vc_ref, vp_ref, pc_ref, pp_ref, inv_ref, sink_ref, o_ref, qr_ref, kr_ref):
        n = pl.program_id(0)
        cos_c, sin_c = _rope_parts(pc_ref, inv_ref)
        cos_p, sin_p = _rope_parts(pp_ref, inv_ref)
        valid = _attn_mask(n)
        k_c, k_p = [], []
        for s in range(KV_DIM // 128):
            t = kc_ref[:, 128 * s:128 * (s + 1)]
            k_c.append((t * cos_c + _rot_half(t) * sin_c).astype(BF16))
            kr_ref[:, 128 * s:128 * (s + 1)] = k_c[s]
            t = kp_ref[:, 128 * s:128 * (s + 1)]
            k_p.append((t * cos_p + _rot_half(t) * sin_p).astype(BF16))
        kcat, vcat = [], []
        for hk in range(KV_HEADS):
            lo = HEAD_DIM * (hk % 2)
            kcat.append(jnp.concatenate([k_p[hk // 2][:, lo:lo + HEAD_DIM], k_c[hk // 2][:, lo:lo + HEAD_DIM]], axis=0))
            vcat.append(jnp.concatenate([vp_ref[:, HEAD_DIM * hk:HEAD_DIM * (hk + 1)], vc_ref[:, HEAD_DIM * hk:HEAD_DIM * (hk + 1)]], axis=0)
                        .astype(BF16))
        q_heads = []
        for s in range(Q_DIM // 128):
            t = q_ref[:, 128 * s:128 * (s + 1)]
            qs = (t * cos_c + _rot_half(t) * sin_c).astype(BF16)
            qr_ref[:, 128 * s:128 * (s + 1)] = qs
            q_heads += [qs[:, :HEAD_DIM], qs[:, HEAD_DIM:]]
        G = ATTN_HEADS // KV_HEADS
        scores = [_dot(q_heads[hq], kcat[hq // G], NT) for hq in range(ATTN_HEADS)]
        probs = [_softmax_sink(scores[hq], valid, sink_ref[0, hq])[0] for hq in range(ATTN_HEADS)]
        outs = [_dot(probs[hq], vcat[hq // G], NN) for hq in range(ATTN_HEADS)]
        for s in range(Q_DIM // 128):
            o_ref[:, 128 * s:128 * (s + 1)] = jnp.concatenate([outs[2 * s], outs[2 * s + 1]], axis=1).astype(BF16)

    qspec = pl.BlockSpec((BLK, Q_DIM), lambda n: (n, OFF_Q // Q_DIM))
    orow = pl.BlockSpec((BLK, Q_DIM), lambda n: (n, 0))
    krow = pl.BlockSpec((BLK, KV_DIM), lambda n: (n, 0))
    return _pcall(body, name="attn_fwd", grid=(T // BLK,), in_specs=[qspec, kc, kp, vc, vp, pc, pp, inv, sink],
                  out_specs=[orow, orow, krow], out_shape=[_sds((T, Q_DIM), BF16), _sds((T, Q_DIM), BF16), _sds((T, KV_DIM), BF16)],
                  compiler_params=_cparams(("arbitrary",)))(projp, projp, projp, projp, projp, posf, posf, inv128, sinks)


def attn_bwd(qr, kr, projp, dattn, posf, inv128, sinks):
    T = projp.shape[0]
    _, _, vc, vp, pc, pp, inv, sink = _attn_specs(T)
    G = ATTN_HEADS // KV_HEADS

    def body(qr_ref, krc_ref, krp_ref, vc_ref, vp_ref, do_ref, pc_ref, pp_ref, inv_ref, sink_ref, dq_ref, dk_ref, dv_ref, dsk_ref):
        n = pl.program_id(0)

        @pl.when(n == 0)
        def _():
            dk_ref[...] = jnp.zeros_like(dk_ref)
            dv_ref[...] = jnp.zeros_like(dv_ref)
            dsk_ref[...] = jnp.zeros_like(dsk_ref)

        cos_c, sin_c = _rope_parts(pc_ref, inv_ref)
        cos_p, sin_p = _rope_parts(pp_ref, inv_ref)
        valid = _attn_mask(n)
        lane = lax.broadcasted_iota(jnp.int32, (1, 128), 1)
        kcat, vcat = [], []
        for hk in range(KV_HEADS):
            ksl = slice(HEAD_DIM * hk, HEAD_DIM * (hk + 1))
            kcat.append(jnp.concatenate([krp_ref[:, ksl], krc_ref[:, ksl]], axis=0))
            vcat.append(jnp.concatenate([vp_ref[:, ksl], vc_ref[:, ksl]], axis=0).astype(BF16))
        H = range(ATTN_HEADS)
        q_heads = [qr_ref[:, HEAD_DIM * hq:HEAD_DIM * (hq + 1)] for hq in H]
        do_heads = [do_ref[:, HEAD_DIM * hq:HEAD_DIM * (hq + 1)] for hq in H]
        soft = [_softmax_sink(_dot(q_heads[hq], kcat[hq // G], NT), valid, sink_ref[0, hq]) for hq in H]
        dps = [_dot(do_heads[hq], vcat[hq // G], NT) for hq in H]
        deltas = [jnp.sum(soft[hq][0] * dps[hq], axis=1, keepdims=True) for hq in H]
        dss = [(soft[hq][0] * (dps[hq] - deltas[hq]) * (HEAD_DIM ** -0.5)).astype(BF16) for hq in H]
        pbs = [soft[hq][0].astype(BF16) for hq in H]
        dsk = jnp.zeros((1, 128), F32)
        for hq in H:
            dsk = dsk + jnp.where(lane == hq, -jnp.sum(soft[hq][1] * deltas[hq], axis=0, keepdims=True), 0.0)
        dsk_ref[...] += dsk
        dq_heads = [_dot(dss[hq], kcat[hq // G], NN) for hq in H]
        dk_parts = [_dot(dss[hq], q_heads[hq], TN) for hq in H]
        dv_parts = [_dot(pbs[hq], do_heads[hq], TN) for hq in H]
        dk_heads = [sum(dk_parts[G * hk + 1:G * (hk + 1)], dk_parts[G * hk]) for hk in range(KV_HEADS)]
        dv_heads = [sum(dv_parts[G * hk + 1:G * (hk + 1)], dv_parts[G * hk]) for hk in range(KV_HEADS)]
        for s in range(Q_DIM // 128):
            t = jnp.concatenate([dq_heads[2 * s], dq_heads[2 * s + 1]], axis=1)
            dq_ref[:, 128 * s:128 * (s + 1)] = (t * cos_c - _rot_half(t) * sin_c).astype(BF16)
        cur = pl.ds(pl.multiple_of(n * BLK, BLK), BLK)
        prv = pl.ds(pl.multiple_of(jnp.maximum(n - 1, 0) * BLK, BLK), BLK)
        for s in range(KV_DIM // 128):
            tc = jnp.concatenate([dk_heads[2 * s][BLK:], dk_heads[2 * s + 1][BLK:]], axis=1)
            tp = jnp.concatenate([dk_heads[2 * s][:BLK], dk_heads[2 * s + 1][:BLK]], axis=1)
            cols = slice(128 * s, 128 * (s + 1))
            dk_ref[cur, cols] += tc * cos_c - _rot_half(tc) * sin_c
            dk_ref[prv, cols] += tp * cos_p - _rot_half(tp) * sin_p
            dv_ref[cur, cols] += jnp.concatenate([dv_heads[2 * s][BLK:], dv_heads[2 * s + 1][BLK:]], axis=1)
            dv_ref[prv, cols] += jnp.concatenate([dv_heads[2 * s][:BLK], dv_heads[2 * s + 1][:BLK]], axis=1)

    qrow = pl.BlockSpec((BLK, Q_DIM), lambda n: (n, 0))
    krc = pl.BlockSpec((BLK, KV_DIM), lambda n: (n, 0))
    krp = pl.BlockSpec((BLK, KV_DIM), lambda n: (jnp.maximum(n - 1, 0), 0))
    whole = pl.BlockSpec((T, KV_DIM), lambda n: (0, 0))
    return _pcall(body, name="attn_bwd", grid=(T // BLK,), in_specs=[qrow, krc, krp, vc, vp, qrow, pc, pp, inv, sink],
                  out_specs=[qrow, whole, whole, pl.BlockSpec((1, 128), lambda n: (0, 0))],
                  out_shape=[_sds((T, Q_DIM), BF16), _sds((T, KV_DIM), F32), _sds((T, KV_DIM), F32), _sds((1, 128), F32)],
                  compiler_params=_cparams(("arbitrary",)))(qr, kr, kr, projp, projp, dattn, posf, posf, inv128, sinks)


CONV_CB = 256


def _shift_down(x, s):
    row = lax.broadcasted_iota(jnp.int32, x.shape, 0)
    return jnp.where(row >= s, pltpu.roll(x, s, 0), 0.0)


def _shift_up(x, s):
    T = x.shape[0]
    row = lax.broadcasted_iota(jnp.int32, x.shape, 0)
    return jnp.where(row < T - s, pltpu.roll(x, T - s, 0), 0.0)


def _conv_pre(x, w_ref, b_ref):
    acc = x * w_ref[CONV_WIDTH - 1:CONV_WIDTH, :] + b_ref[...]
    for s in range(1, CONV_WIDTH):
        acc = acc + _shift_down(x, s) * w_ref[CONV_WIDTH - 1 - s:CONV_WIDTH - s, :]
    return acc


def conv_fwd(projp, conv_w, conv_b):
    T = projp.shape[0]

    def body(x_ref, w_ref, b_ref, o_ref):
        o_ref[...] = _silu(_conv_pre(x_ref[...], w_ref, b_ref))

    return _pcall(body, name="conv_fwd", grid=(CONV_DIM // CONV_CB,),
                  in_specs=[pl.BlockSpec((T, CONV_CB), lambda c: (0, OFF_XBC // CONV_CB + c)),
                            pl.BlockSpec((CONV_WIDTH, CONV_CB), lambda c: (0, c)), pl.BlockSpec((1, CONV_CB), lambda c: (0, c))],
                  out_specs=pl.BlockSpec((T, CONV_CB), lambda c: (0, c)), out_shape=_sds((T, CONV_DIM), F32),
                  compiler_params=_cparams(("arbitrary",)))(projp, conv_w, conv_b)


def conv_bwd(name, projp, dact, conv_w, conv_b, col0):
    T, C = dact.shape
    c0 = col0 // CONV_CB

    def body(x_ref, da_ref, w_ref, b_ref, dx_ref, dw_ref, db_ref):
        x = x_ref[...]
        dpre = da_ref[...] * _dsilu(_conv_pre(x, w_ref, b_ref))
        dx = dpre * w_ref[CONV_WIDTH - 1:CONV_WIDTH, :]
        dw_ref[CONV_WIDTH - 1:CONV_WIDTH, :] = jnp.sum(dpre * x, axis=0, keepdims=True)
        for s in range(1, CONV_WIDTH):
            i = CONV_WIDTH - 1 - s
            dx = dx + _shift_up(dpre, s) * w_ref[i:i + 1, :]
            dw_ref[i:i + 1, :] = jnp.sum(dpre * _shift_down(x, s), axis=0, keepdims=True)
        dx_ref[...] = dx.astype(BF16)
        db_ref[...] = jnp.sum(dpre, axis=0, keepdims=True)

    return _pcall(body, name=name, grid=(C // CONV_CB,),
                  in_specs=[pl.BlockSpec((T, CONV_CB), lambda c: (0, OFF_XBC // CONV_CB + c0 + c)),
                            pl.BlockSpec((T, CONV_CB), lambda c: (0, c)),
                            pl.BlockSpec((CONV_WIDTH, CONV_CB), lambda c: (0, c0 + c)), pl.BlockSpec((1, CONV_CB), lambda c: (0, c0 + c))],
                  out_specs=[pl.BlockSpec((T, CONV_CB), lambda c: (0, c)), pl.BlockSpec((CONV_WIDTH, CONV_CB), lambda c: (0, c)),
                             pl.BlockSpec((1, CONV_CB), lambda c: (0, c))],
                  out_shape=[_sds((T, C), BF16), _sds((CONV_WIDTH, C), F32), _sds((1, C), F32)],
                  compiler_params=_cparams(("arbitrary",)))(projp, dact, conv_w, conv_b)


def _softplus(x):
    return jnp.maximum(x, 0.0) + jnp.log1p(jnp.exp(-jnp.abs(x)))


def _tri(lower):
    r = lax.broadcasted_iota(jnp.int32, (BLK, BLK), 0)
    c = lax.broadcasted_iota(jnp.int32, (BLK, BLK), 1)
    return (r >= c) if lower else (c >= r)


def _ssd_chunk_setup(dt_ref, dtb_ref, alog_ref):
    raw = dt_ref[...] + dtb_ref[...]
    dt = _softplus(raw)
    aneg = -jnp.exp(alog_ref[...])
    a = dt * aneg
    acs = jnp.dot(_tri(True).astype(F32), a, precision=lax.Precision.HIGHEST, preferred_element_type=F32)
    return raw, dt, aneg, acs, acs.T


def _ssd_specs(T, rev):
    nc = T // BLK
    ci = (lambda c: nc - 1 - c) if rev else (lambda c: c)
    xs = pl.BlockSpec((BLK, D_INNER), lambda c: (ci(c), 0))
    bm = pl.BlockSpec((BLK, SSM_GROUPS * D_STATE), lambda c: (ci(c), D_INNER // (SSM_GROUPS * D_STATE)))
    cm = pl.BlockSpec((BLK, SSM_GROUPS * D_STATE), lambda c: (ci(c), D_INNER // (SSM_GROUPS * D_STATE) + 1))
    dt = pl.BlockSpec((BLK, DT_PAD), lambda c: (ci(c), OFF_DT // DT_PAD))
    v128 = pl.BlockSpec((1, 128), lambda c: (0, 0))
    dfull = pl.BlockSpec((1, D_INNER), lambda c: (0, 0))
    st = pl.BlockSpec((None, SSM_HEADS, HEAD_DIM, D_STATE), lambda c: (ci(c), 0, 0, 0))
    return xs, bm, cm, dt, v128, dfull, st, ci


GW = HEADS_PER_GROUP * HEAD_DIM


def _expanders():
    e = np.zeros((SSM_GROUPS, 128, GW), np.float32)
    for g in range(SSM_GROUPS):
        for hh in range(HEADS_PER_GROUP):
            e[g, HEADS_PER_GROUP * g + hh, HEAD_DIM * hh:HEAD_DIM * (hh + 1)] = 1.0
    return jnp.asarray(e, BF16), jnp.asarray(np.transpose(e, (0, 2, 1)).copy(), BF16)


def _split2(v):
    hi = lax.bitcast_convert_type(lax.bitcast_convert_type(v, jnp.uint32) & jnp.uint32(0xFFFF0000), F32)
    return hi.astype(BF16), (v - hi).astype(BF16)


def _dotx(a, b):
    if a.dtype == BF16:
        hi, lo = _split2(b)
        return jnp.dot(a, hi, preferred_element_type=F32) + jnp.dot(a, lo, preferred_element_type=F32)
    hi, lo = _split2(a)
    return jnp.dot(hi, b, preferred_element_type=F32) + jnp.dot(lo, b, preferred_element_type=F32)


def _decay(acs, acsT, h, tril):
    return jnp.where(tril, jnp.exp(jnp.where(tril, acs[:, h:h + 1] - acsT[h:h + 1, :], 0.0)), 0.0)


def ssd_fwd(xbc, projp, dtb, alog, dfull):
    T = xbc.shape[0]
    nc = T // BLK
    xs, bm, cm, dts, v128, dfs, st, _ = _ssd_specs(T, False)
    E, _ = _expanders()

    def body(xs_ref, b_ref, c_ref, dt_ref, dtb_ref, alog_ref, d_ref, e_ref, y_ref, st_ref, h_scr):
        c = pl.program_id(0)

        @pl.when(c == 0)
        def _():
            h_scr[...] = jnp.zeros_like(h_scr)

        _, dt, _, acs, acsT = _ssd_chunk_setup(dt_ref, dtb_ref, alog_ref)
        tril = _tri(True)
        alast = acs[BLK - 1:BLK, :]
        eacs = jnp.exp(acs)
        wmat = jnp.exp(alast - acs)
        gam = jnp.exp(alast)
        for g in range(SSM_GROUPS):
            gl = slice(GW * g, GW * (g + 1))
            hsl = slice(HEADS_PER_GROUP * g, HEADS_PER_GROUP * (g + 1))
            heads = [HEADS_PER_GROUP * g + hh for hh in range(HEADS_PER_GROUP)]
            Eg = e_ref[g]
            B = b_ref[:, D_STATE * g:D_STATE * (g + 1)].astype(BF16)
            C = c_ref[:, D_STATE * g:D_STATE * (g + 1)].astype(BF16)
            cb = _dot(C, B, NT)
            x_g = xs_ref[:, gl]
            xd_g = x_g * _dotx(dt, Eg)
            hold = h_scr[hsl]
            st_ref[hsl] = hold
            hcat = hold.reshape(GW, D_STATE)
            yoff = _dotx(eacs, Eg) * _dot(C, hcat, NT)
            S = _dot(xd_g * _dotx(wmat, Eg), B, TN)
            Ms = [cb * _decay(acs, acsT, h, tril) for h in heads]
            ys = [_dot(Ms[hh], xd_g[:, HEAD_DIM * hh:HEAD_DIM * (hh + 1)], NN) for hh in range(HEADS_PER_GROUP)]
            for hh, h in enumerate(heads):
                h_scr[h] = gam[:, h:h + 1] * hold[hh] + S[HEAD_DIM * hh:HEAD_DIM * (hh + 1)]
            y_ref[:, gl] = jnp.concatenate(ys, axis=1) + yoff + d_ref[:, gl] * x_g

    espec = pl.BlockSpec((SSM_GROUPS, 128, GW), lambda c: (0, 0, 0))
    return _pcall(body, name="ssd_fwd", grid=(nc,), in_specs=[xs, bm, cm, dts, v128, v128, dfs, espec],
                  out_specs=[xs, st], out_shape=[_sds((T, D_INNER), F32), _sds((nc, SSM_HEADS, HEAD_DIM, D_STATE), F32)],
                  scratch_shapes=[pltpu.VMEM((SSM_HEADS, HEAD_DIM, D_STATE), F32)],
                  compiler_params=_cparams(("arbitrary",)))(xbc, xbc, xbc, projp, dtb, alog, dfull, E)


def ssd_bwd(xbc, projp, dtb, alog, dfull, states, dy):
    T = xbc.shape[0]
    nc = T // BLK
    xs, bm, cm, dts, v128, dfs, st, ci = _ssd_specs(T, True)
    gn = SSM_GROUPS * D_STATE
    E, ET = _expanders()

    def body(xs_ref, b_ref, c_ref, dt_ref, dtb_ref, alog_ref, d_ref, st_ref, dy_ref, e_ref, et_ref,
             dxs_ref, dB_ref, dC_ref, ddt_ref, dal_ref, dD_ref, ddtb_ref, dh_scr):
        i = pl.program_id(0)

        @pl.when(i == 0)
        def _():
            dh_scr[...] = jnp.zeros_like(dh_scr)
            dal_ref[...] = jnp.zeros_like(dal_ref)
            dD_ref[...] = jnp.zeros_like(dD_ref)
            ddtb_ref[...] = jnp.zeros_like(ddtb_ref)

        raw, dt, aneg, acs, acsT = _ssd_chunk_setup(dt_ref, dtb_ref, alog_ref)
        tril = _tri(True)
        lane = lax.broadcasted_iota(jnp.int32, (BLK, 128), 1)
        sub = lax.broadcasted_iota(jnp.int32, (BLK, 128), 0)
        alast = acs[BLK - 1:BLK, :]
        eacs = jnp.exp(acs)
        wmat = jnp.exp(alast - acs)
        gam = jnp.exp(alast)
        gcol = jnp.exp(acsT[:, BLK - 1:BLK])
        ds_col = jnp.zeros((BLK, 128), F32)
        ds_row = jnp.zeros((BLK, 128), F32)
        ddt_col = jnp.zeros((BLK, 128), F32)
        dDm = jnp.zeros((BLK, 128), F32)
        hl = [slice(HEAD_DIM * hh, HEAD_DIM * (hh + 1)) for hh in range(HEADS_PER_GROUP)]
        for g in range(SSM_GROUPS):
            gl = slice(GW * g, GW * (g + 1))
            gs = slice(D_STATE * g, D_STATE * (g + 1))
            hsl = slice(HEADS_PER_GROUP * g, HEADS_PER_GROUP * (g + 1))
            heads = [HEADS_PER_GROUP * g + hh for hh in range(HEADS_PER_GROUP)]
            Eg, ETg = e_ref[g], et_ref[g]
            B = b_ref[:, gs].astype(BF16)
            C = c_ref[:, gs].astype(BF16)
            cb = _dot(C, B, NT)
            x_g, dy_g = xs_ref[:, gl], dy_ref[:, gl]
            dt_x, w_x = _dotx(dt, Eg), _dotx(wmat, Eg)
            xd_g = x_g * dt_x
            dye = dy_g * _dotx(eacs, Eg)
            hcat = st_ref[hsl].reshape(GW, D_STATE)
            dSv = dh_scr[hsl]
            dScat = dSv.reshape(GW, D_STATE)
            dDm = dDm + _dotx(dy_g * x_g, ETg)
            dH_y = _dot(dye, C, TN)
            dC_g = _dot(dye, hcat, NN)
            ds_col = ds_col + _dotx(dye * _dot(C, hcat, NT), ETg)
            dxdw = _dot(B, dScat, NT)
            dB_g = _dot(xd_g * w_x, dScat, NN)
            dww = _dotx(xd_g * dxdw, ETg) * wmat
            ds_col = ds_col - dww + jnp.where(sub == BLK - 1, jnp.sum(dww, axis=0, keepdims=True), 0.0)
            hd = jnp.sum(_dotx(Eg, dScat * hcat), axis=1, keepdims=True) * gcol
            ds_row = ds_row - jnp.where(lane == BLK - 1, hd, 0.0)
            decays = [_decay(acs, acsT, h, tril) for h in heads]
            Ms = [cb * d for d in decays]
            dMs = [_dot(dy_g[:, hl[hh]], xd_g[:, hl[hh]], NT) for hh in range(HEADS_PER_GROUP)]
            dxd1 = [_dot(Ms[hh], dy_g[:, hl[hh]], TN) for hh in range(HEADS_PER_GROUP)]
            dG = jnp.zeros((BLK, BLK), F32)
            for hh, h in enumerate(heads):
                Q = dMs[hh] * Ms[hh]
                ds_col = ds_col + jnp.where(lane == h, jnp.sum(Q, axis=1, keepdims=True), 0.0)
                ds_row = ds_row + jnp.where(sub == h, jnp.sum(Q, axis=0, keepdims=True), 0.0)
                dG = dG + dMs[hh] * decays[hh]
            dxd_g = jnp.concatenate(dxd1, axis=1) + w_x * dxdw
            dxs_ref[:, gl] = d_ref[:, gl] * dy_g + dxd_g * dt_x
            ddt_col = ddt_col + _dotx(dxd_g * x_g, ETg)
            dC_ref[:, gs] = dC_g + _dot(dG, B, NN)
            dB_ref[:, gs] = dB_g + _dot(dG, C, TN)
            for hh, h in enumerate(heads):
                dh_scr[h] = gam[:, h:h + 1] * dSv[hh] + dH_y[hl[hh]]
        ds_all = ds_col - ds_row.T
        da = jnp.dot(_tri(False).astype(F32), ds_all, precision=lax.Precision.HIGHEST, preferred_element_type=F32)
        ddt = ddt_col + da * aneg
        draw = jnp.where(lane < SSM_HEADS, ddt * _sigmoid(raw), 0.0)
        ddt_ref[...] = draw.astype(BF16)
        dal_ref[...] += jnp.sum(da * dt, axis=0, keepdims=True) * aneg
        ddtb_ref[...] += jnp.sum(draw, axis=0, keepdims=True)
        dD_ref[...] += jnp.sum(dDm, axis=0, keepdims=True)

    gblk = pl.BlockSpec((BLK, gn), lambda c: (ci(c), 0))
    espec = pl.BlockSpec((SSM_GROUPS, 128, GW), lambda c: (0, 0, 0))
    etspec = pl.BlockSpec((SSM_GROUPS, GW, 128), lambda c: (0, 0, 0))
    return _pcall(body, name="ssd_bwd", grid=(nc,), in_specs=[xs, bm, cm, dts, v128, v128, dfs, st, xs, espec, etspec],
                  out_specs=[xs, gblk, gblk, pl.BlockSpec((BLK, DT_PAD), lambda c: (ci(c), 0)), v128, v128, v128],
                  out_shape=[_sds((T, D_INNER), F32), _sds((T, gn), F32), _sds((T, gn), F32), _sds((T, DT_PAD), BF16),
                             _sds((1, 128), F32), _sds((1, 128), F32), _sds((1, 128), F32)],
                  scratch_shapes=[pltpu.VMEM((SSM_HEADS, HEAD_DIM, D_STATE), F32)],
                  compiler_params=_cparams(("arbitrary",)))(xbc, xbc, xbc, projp, dtb, alog, dfull, states, dy, E, ET)


_WIN_ORDER = ("z", "ga", "gs", "xbc", "q", "k", "v", "dt")


PERM_TILE = 512


def _win_row_moves():
    off = dict(z=OFF_Z, ga=OFF_GA, gs=OFF_GS, xbc=OFF_XBC, q=OFF_Q, k=OFF_K, v=OFF_V, dt=OFF_DT)
    per = IN_DIM // N_DEV
    tiles = [[] for _ in range(-(-IN_PAD // PERM_TILE))]
    for nm in _WIN_ORDER:
        s, w = SEG[nm]
        d = off[nm]
        while w > 0:
            j, r = divmod(s, per)
            n = min(w, per - r, PERM_TILE - d % PERM_TILE)
            tiles[d // PERM_TILE].append((j, r, n, d % PERM_TILE))
            s, w, d = s + n, w - n, d + n
    return tiles


def _win_to_padded(win_g):
    per = IN_DIM // N_DEV
    moves = _win_row_moves()

    def body(w_ref, o_ref, slots, stage, in_sems, out_sems):
        loads = [pltpu.make_async_copy(w_ref.at[j], slots.at[j], in_sems.at[j]) for j in range(N_DEV)]
        for cp in loads:
            cp.start()
        arrived = [False] * N_DEV
        stores = [None, None]
        for t, pieces in enumerate(moves):
            rows = min(PERM_TILE, IN_PAD - PERM_TILE * t)
            b = t % 2
            if stores[b] is not None:
                stores[b].wait()
            filled = 0
            for j, r, n, d in pieces:
                if not arrived[j]:
                    loads[j].wait()
                    arrived[j] = True
                stage[b, pl.ds(d, n), :] = slots[j, pl.ds(r, n), :]
                filled = max(filled, d + n)
            if filled < rows:
                stage[b, pl.ds(filled, rows - filled), :] = jnp.zeros((rows - filled, D_MODEL), stage.dtype)
            stores[b] = pltpu.make_async_copy(stage.at[b, pl.ds(0, rows), :], o_ref.at[pl.ds(PERM_TILE * t, rows), :], out_sems.at[b])
            stores[b].start()
        for cp in stores:
            cp.wait()

    return _pcall(body, name="w_in_rows", in_specs=[_ANY], out_specs=_ANY, out_shape=_sds((IN_PAD, D_MODEL), win_g.dtype),
                  scratch_shapes=[pltpu.VMEM((N_DEV, per, D_MODEL), win_g.dtype), pltpu.VMEM((2, PERM_TILE, D_MODEL), win_g.dtype),
                                  pltpu.SemaphoreType.DMA((N_DEV,)), pltpu.SemaphoreType.DMA((2,))],
                  compiler_params=pltpu.CompilerParams(vmem_limit_bytes=60 * 1024 * 1024))(win_g)


def _padded_to_win(dw):
    per = IN_DIM // N_DEV
    moves = _win_row_moves()

    def body(d_ref, o_ref, slots, stage, in_sems, out_sems):
        def load(t):
            rows = min(PERM_TILE, IN_PAD - PERM_TILE * t)
            return pltpu.make_async_copy(d_ref.at[pl.ds(PERM_TILE * t, rows), :], stage.at[t % 2, pl.ds(0, rows), :], in_sems.at[t % 2])

        pending = load(0)
        pending.start()
        for t, pieces in enumerate(moves):
            pending.wait()
            if t + 1 < len(moves):
                pending = load(t + 1)
                pending.start()
            for j, r, n, d in pieces:
                slots[j, pl.ds(r, n), :] = stage[t % 2, pl.ds(d, n), :]
        stores = [pltpu.make_async_copy(slots.at[j], o_ref.at[j], out_sems.at[j]) for j in range(N_DEV)]
        for cp in stores:
            cp.start()
        for cp in stores:
            cp.wait()

    return _pcall(body, name="d_w_in_rows", in_specs=[_ANY], out_specs=_ANY, out_shape=_sds((N_DEV, per, D_MODEL), dw.dtype),
                  scratch_shapes=[pltpu.VMEM((N_DEV, per, D_MODEL), dw.dtype), pltpu.VMEM((2, PERM_TILE, D_MODEL), dw.dtype),
                                  pltpu.SemaphoreType.DMA((2,)), pltpu.SemaphoreType.DMA((N_DEV,))],
                  compiler_params=pltpu.CompilerParams(vmem_limit_bytes=60 * 1024 * 1024))(dw)


def _pad128(v):
    return jnp.pad(v, ((0, 0), (0, 128 - v.shape[1])))


_SMALL = (("loss", 128, 1), ("g_mix", 2048, 2048), ("conv_b", 3072, 3072), ("dt_bias", 128, 32), ("a_log", 128, 32),
          ("d_skip", 128, 32), ("g_ssd", 2048, 2048), ("sinks", 128, 16), ("g_ffn", 2048, 2048), ("g_ple", 2048, 2048),
          ("g_final", 2048, 2048))


def _small_vec(d):
    parts = []
    for nm, pw, w in _SMALL:
        v = d[nm].reshape(1, -1).astype(F32)
        parts.append(jnp.pad(v[:, :min(v.shape[1], pw)], ((0, 0), (0, pw - min(v.shape[1], pw)))))
    return jnp.concatenate(parts, axis=1)


def _small_split(vec):
    out, o = {}, 0
    for nm, pw, w in _SMALL:
        out[nm] = vec[0, o:o + w]
        o += pw
    return out


def kernel(x, p, positions, g_mix, w_in, conv_w, conv_b, dt_bias, a_log, d_skip, g_ssd, sinks, w_attn_br, w_ssd_br, w_o, g_ffn, w_gate, w_up, w_down, g_ple, w_ple_gate, w_ple_proj, g_final, loss_target, m_g_mix, m_w_in, m_conv_w, m_conv_b, m_dt_bias, m_a_log, m_d_skip, m_g_ssd, m_sinks, m_w_attn_br, m_w_ssd_br, m_w_o, m_g_ffn, m_w_gate, m_w_up, m_w_down, m_g_ple, m_w_ple_gate, m_w_ple_proj, m_g_final, v_g_mix, v_w_in, v_conv_w, v_conv_b, v_dt_bias, v_a_log, v_d_skip, v_g_ssd, v_sinks, v_w_attn_br, v_w_ssd_br, v_w_o, v_g_ffn, v_w_gate, v_w_up, v_w_down, v_g_ple, v_w_ple_gate, v_w_ple_proj, v_g_final):
    T = x.shape[1]
    D = D_MODEL
    W = dict(g_mix=g_mix, w_in=w_in, conv_w=conv_w, conv_b=conv_b, dt_bias=dt_bias, a_log=a_log, d_skip=d_skip, g_ssd=g_ssd,
             sinks=sinks, w_attn_br=w_attn_br, w_ssd_br=w_ssd_br, w_o=w_o, g_ffn=g_ffn, w_gate=w_gate, w_up=w_up, w_down=w_down,
             g_ple=g_ple, w_ple_gate=w_ple_gate, w_ple_proj=w_ple_proj, g_final=g_final)
    Mo = dict(g_mix=m_g_mix, w_in=m_w_in, conv_w=m_conv_w, conv_b=m_conv_b, dt_bias=m_dt_bias, a_log=m_a_log, d_skip=m_d_skip,
              g_ssd=m_g_ssd, sinks=m_sinks, w_attn_br=m_w_attn_br, w_ssd_br=m_w_ssd_br, w_o=m_w_o, g_ffn=m_g_ffn, w_gate=m_w_gate,
              w_up=m_w_up, w_down=m_w_down, g_ple=m_g_ple, w_ple_gate=m_w_ple_gate, w_ple_proj=m_w_ple_proj, g_final=m_g_final)
    Vo = dict(g_mix=v_g_mix, w_in=v_w_in, conv_w=v_conv_w, conv_b=v_conv_b, dt_bias=v_dt_bias, a_log=v_a_log, d_skip=v_d_skip,
              g_ssd=v_g_ssd, sinks=v_sinks, w_attn_br=v_w_attn_br, w_ssd_br=v_w_ssd_br, w_o=v_w_o, g_ffn=v_g_ffn, w_gate=v_w_gate,
              w_up=v_w_up, w_down=v_w_down, g_ple=v_g_ple, w_ple_gate=v_w_ple_gate, w_ple_proj=v_w_ple_proj, g_final=v_g_final)
    order = ["g_mix", "w_in", "conv_w", "conv_b", "dt_bias", "a_log", "d_skip", "g_ssd", "sinks", "w_attn_br", "w_ssd_br", "w_o",
             "g_ffn", "w_gate", "w_up", "w_down", "g_ple", "w_ple_gate", "w_ple_proj", "g_final"]
    big = ["w_in", "conv_w", "w_attn_br", "w_ssd_br", "w_o", "w_gate", "w_up", "w_down", "w_ple_gate", "w_ple_proj"]

    x2 = x.reshape(T, D)
    p2 = p.reshape(T, PLE_DIM)
    tgt = loss_target.reshape(T, D)
    posf = positions.reshape(T, 1).astype(F32)
    inv = ROPE_THETA ** (-np.arange(HEAD_DIM // 2, dtype=np.float32) * 2.0 / HEAD_DIM)
    inv128 = jnp.asarray(np.tile(inv, 128 // (HEAD_DIM // 2)).reshape(1, 128).astype(np.float32))
    transposed = ("w_in", "w_gate", "w_up")

    def shard2d(a, n):
        a = a.reshape(a.shape[-2:])
        return a.T if n in transposed else a

    sh = {n: shard2d(W[n], n) for n in big}

    del _PENDING[:]
    me = 4 * lax.axis_index("x") + 2 * lax.axis_index("y") + lax.axis_index("c")
    groups = (("w_in",), ("conv_w", "w_attn_br", "w_ssd_br", "w_o"), ("w_gate", "w_up"), ("w_down",), ("w_ple_gate", "w_ple_proj"))
    send = {n: sh[n] if n == "conv_w" else sh[n].astype(BF16) for n in big}
    zones = [[lax.dynamic_update_index_in_dim(lax.empty((N_DEV,) + send[n].shape, send[n].dtype), send[n], me, 0) for n in grp]
             for grp in groups]
    ring_a = split_start("ring_a_start", "route", ((4, 0, 0), (2, 0, 1)), [], lands=zones[0])
    ring_b = split_start("ring_b_start", "route", ((4, 0, 1), (2, 0, 0)), [], lands=ring_a["lands"], after=ring_a["token"])
    gathered, fwd = {}, {}

    def forward_start(gi, after, lands=None):
        if lands is None:
            _, lands = split_wait("gather_wait_%d" % gi, started[gi], after)
        fwd[gi] = split_start("forward_start_%d" % gi, "forward", FORWARD_BLOCKS, [], lands=lands)

    def forward_wait(gi, after):
        _, full = split_wait("forward_wait_%d" % gi, fwd[gi], after)
        gathered.update(zip(groups[gi], full))

    u = rms_fwd("norm_mix", x2, g_mix)
    _, zone = split_wait("ring_a_wait", dict(ring_a, lands=ring_b["lands"]), [u] + [z for grp in zones[1:] for z in grp])
    ring_c = split_start("ring_c_start", "route", ((2, 4, 0), (4, 2, 1)), [], lands=zone)
    started, prev = [None], ring_c["token"]
    for gi in range(1, len(groups)):
        started.append(split_start("gather_start_%d" % gi, "gather", ICI_SAME_CORE, [], lands=zones[gi], after=prev))
        prev = started[-1]["token"]
    _, zone = split_wait("ring_b_wait", dict(ring_b, lands=ring_c["lands"]), u)
    _, zone = split_wait("ring_c_wait", dict(ring_c, lands=zone), u)
    forward_start(0, u, lands=zone)
    forward_wait(0, u)
    winp = _win_to_padded(gathered["w_in"])
    dtb = _pad128(dt_bias)
    alog = _pad128(a_log)
    dfull = jnp.repeat(d_skip.reshape(SSM_HEADS), HEAD_DIM).reshape(1, D_INNER)

    projp = mm_nt("in_proj", u, winp, 640)
    forward_start(1, projp)
    attn, qr, kr = attn_fwd(projp, posf, inv128, sinks)
    forward_wait(1, attn)
    convw = jnp.transpose(gathered["conv_w"], (1, 0, 2)).reshape(CONV_WIDTH, CONV_DIM)
    wab = gathered["w_attn_br"]
    wsb = gathered["w_ssd_br"].reshape(D, D)
    wo = gathered["w_o"].reshape(D, D)
    xbc = conv_fwd(projp, convw, conv_b)
    y, states = ssd_fwd(xbc, projp, dtb, alog, dfull)
    yn = gnorm_fwd(y, projp, g_ssd)
    out_a = mm_nn_colblk("attn_br", attn, wab)
    out_s = mm_nn("ssd_br", yn, wsb, 512)
    forward_start(2, out_s)
    merged = merge_fwd(projp, out_a, out_s)
    h1 = mm_nn("o_proj", merged, wo, 512, residual=x2)
    f = rms_fwd("norm_ffn", h1, g_ffn)
    forward_wait(2, f)
    forward_start(3, f)
    wgt, wut = (gathered[n].reshape(FFN_HIDDEN, D) for n in ("w_gate", "w_up"))
    gate, up, act = ffn_up(f, wgt, wut)
    forward_wait(3, act)
    forward_start(4, act)
    wd = gathered["w_down"].reshape(FFN_HIDDEN, D)
    h2 = mm_nn_red("ffn_down", act, wd, 512, FFN_HIDDEN, residual=h1)
    r = rms_fwd("norm_ple", h2, g_ple)
    forward_wait(4, r)
    wpg = gathered["w_ple_gate"].reshape(D, D)
    wpp = gathered["w_ple_proj"]
    pg = mm_nn("ple_gate", r, wpg, 512)
    pp = mm_nn_colblk("ple_proj", p2, wpp)
    loss_v, dh3, dpg, dpp, dg_final = head_fwd_bwd(h2, pg, pp, g_final.reshape(1, D), tgt)

    gw = {}
    scat = []

    def scatter_start(names):
        scat.append((names, split_start("scatter_start_%d" % len(scat), "scatter", ALL_PEERS, [gw[n] for n in names])))

    gw["w_ple_proj"] = mm_tn_colblk("dw_ple_proj", p2, dpp, PLE_DIM)
    dr = mm_nt("d_ple_gate", dpg, wpg, 512)
    gw["w_ple_gate"] = mm_tn("dw_ple_gate", r, dpg, 512, D).reshape(N_DEV, D // N_DEV, D)
    scatter_start(("w_ple_proj", "w_ple_gate"))
    dh2, dh2b, dg_ple = rms_bwd("norm_ple_bwd", h2, g_ple, dr, dh3)
    dgate, dup = ffn_down_bwd(dh2b, wd, gate, up)
    per = FFN_HIDDEN // N_DEV
    gw["w_down"] = mm_tn("dw_down", act, dh2b, FFN_TILE, D).reshape(N_DEV, per, D)
    gw["w_gate"] = mm_tn("dw_gate", dgate, f, FFN_TILE, D).reshape(N_DEV, per, D)
    gw["w_up"] = mm_tn("dw_up", dup, f, FFN_TILE, D).reshape(N_DEV, per, D)
    scatter_start(("w_down", "w_gate", "w_up"))
    df = ffn_up_bwd(dgate, dup, wgt, wut)
    dh1, dh1b, dg_ffn = rms_bwd("norm_ffn_bwd", h1, g_ffn, df, dh2)
    dmerged = mm_nt("d_o_proj", dh1b, wo, 512)
    gw["w_o"] = mm_tn("dw_o", merged, dh1b, 512, D).reshape(N_DEV, D // N_DEV, D)
    dout_a, dout_s, dga, dgs = merge_bwd(projp, out_a, out_s, dmerged)
    gw["w_ssd_br"] = mm_tn("dw_ssd_br", yn, dout_s, 512, D).reshape(N_DEV, D // N_DEV, D)
    gw["w_attn_br"] = mm_tn_colblk("dw_attn_br", attn, dout_a, D // N_DEV)
    scatter_start(("w_o", "w_ssd_br", "w_attn_br"))
    dyn = mm_nt("d_ssd_br", dout_s, wsb, 512)
    dattn = attn_br_bwd(dout_a, wab)
    dy, dz, dg_ssd = gnorm_bwd(y, projp, g_ssd, dyn)
    dxs, dbm, dcm, ddt, dal, ddsk, ddtb = ssd_bwd(xbc, projp, dtb, alog, dfull, states, dy)
    dx_x, dwc_x, dbc_x = conv_bwd("conv_bwd_x", projp, dxs, convw, conv_b, 0)
    dx_b, dwc_b, dbc_b = conv_bwd("conv_bwd_b", projp, dbm, convw, conv_b, D_INNER)
    dx_c, dwc_c, dbc_c = conv_bwd("conv_bwd_c", projp, dcm, convw, conv_b, D_INNER + SSM_GROUPS * D_STATE)
    dq, dk, dv, dsk = attn_bwd(qr, kr, projp, dattn, posf, inv128, sinks)
    dproj = jnp.concatenate([dz, dga, dgs, dx_x, dx_b, dx_c, dq, dk.astype(BF16), dv.astype(BF16), ddt], axis=1)
    gw_in = _padded_to_win(mm_tn("dw_in", dproj, u, 640, D))
    pair = split_start("pair_start", "pair", FORWARD_BLOCKS, [gw_in])
    dconvw = jnp.concatenate([dwc_x, dwc_b, dwc_c], axis=1)
    gw["conv_w"] = jnp.transpose(dconvw.reshape(CONV_WIDTH, N_DEV, CONV_DIM // N_DEV), (1, 0, 2))
    scatter_start(("conv_w",))
    du_tm = min(512, T)
    tiles = T // du_tm
    first = max(tiles // 4, 1)
    du = mm_nn_red("d_in_proj_a", dproj, winp, 512, IN_PAD, rows=(0, first), tm=du_tm)
    (gw_in,), (sibling_part,) = split_wait("pair_wait", pair, du)
    pair_slots = jnp.stack([jnp.bitwise_xor(me, k) for k in FORWARD_BLOCKS]).astype(jnp.int32)
    core = split_start("core_start", "scatter_core", ICI_SAME_CORE, [pair_sum("pair_sum_w_in", gw_in, pair_slots, sibling_part)])
    if first < tiles:
        du = mm_nn_red("d_in_proj_b", dproj, winp, 512, IN_PAD, rows=(first, tiles - first), prev=du, tm=du_tm)
    gx, _, dg_mix = rms_bwd("norm_mix_bwd", x2, g_mix, du, dh1)

    small_g = dict(loss=loss_v[:, :1], g_mix=dg_mix, conv_b=jnp.concatenate([dbc_x, dbc_b, dbc_c], axis=1), dt_bias=ddtb,
                   a_log=dal, d_skip=ddsk, g_ssd=dg_ssd, sinks=dsk, g_ffn=dg_ffn, g_ple=dg_ple, g_final=dg_final)
    vec = _small_vec(small_g)
    small = split_start("small_start", "gather", ALL_PEERS, [],
                        lands=[lax.dynamic_update_index_in_dim(lax.empty((N_DEV,) + vec.shape, F32), vec, me, 0)])

    res = {}
    after = [gx]
    for si, (names, h) in enumerate(scat):
        srcs, lands = split_wait("scatter_wait_%d" % si, h, after)
        for n, mine, arrived in zip(names, srcs, lands):
            res[n] = adamw("adamw_" + n, arrived, sh[n], shard2d(Mo[n], n), shard2d(Vo[n], n), own=mine, own_slot=me)
        after = [res[n][0] for n in names]
    zero = jnp.zeros((1, 1), F32)
    _, (vec_parts,) = split_wait("small_wait", small, [res[n][0] for n in res])
    sres = adamw("adamw_small", vec_parts, _small_vec({**W, "loss": zero}), _small_vec({**Mo, "loss": zero}),
                 _small_vec({**Vo, "loss": zero}))
    ssplit = [_small_split(a) for a in sres]

    (pair_sums,), (arrived,) = split_wait("core_wait", core, [sres[0]])
    res["w_in"] = adamw("adamw_w_in", arrived, sh["w_in"], shard2d(Mo["w_in"], "w_in"), shard2d(Vo["w_in"], "w_in"),
                        own=pair_sums, own_slot=0)
    loss = ssplit[0]["loss"].reshape(())
    for n in order:
        if n not in res:
            res[n] = tuple(s[n].reshape(W[n].shape) for s in ssplit)
        else:
            res[n] = tuple((a.T if n in transposed else a).reshape(W[n].shape) for a in res[n])
    outs = [loss, gx.reshape(x.shape)]
    for k in range(4):
        outs += [res[n][k] for n in order]
    return tuple(outs)
```

```python
import numpy as np
import jax
import jax.numpy as jnp
from jax import lax
from jax.experimental import pallas as pl
from jax.experimental.pallas import tpu as pltpu

F32 = jnp.float32
BF16 = jnp.bfloat16

N_DEV = 8
D_MODEL = 2048
HEAD_DIM = 64
ATTN_HEADS = 16
KV_HEADS = 4
Q_DIM = 1024
KV_DIM = 256
BLK = 128
D_INNER = 2048
SSM_HEADS = 32
SSM_GROUPS = 4
HEADS_PER_GROUP = 8
D_STATE = 128
CONV_WIDTH = 4
CONV_DIM = 3072
FFN_HIDDEN = 5632
PLE_DIM = 256
IN_DIM = 10784
NORM_EPS = 1e-6
SSM_NORM_EPS = 1e-5
ROPE_THETA = 10000.0

OFF_GA, OFF_GS, OFF_Z, OFF_XBC, OFF_Q, OFF_K, OFF_V, OFF_DT = 0, 2048, 4096, 6144, 9216, 10240, 10496, 10752
IN_PAD = 10880
DT_PAD = 128
SEG = dict(q=(0, 1024), k=(1024, 256), v=(1280, 256), z=(1536, 2048), xbc=(3584, 3072), dt=(6656, 32),
           ga=(6688, 2048), gs=(8736, 2048))

ADAM_LR, ADAM_B1, ADAM_B2, ADAM_EPS, ADAM_WD, ADAM_STEP = 0.001, 0.9, 0.999, 1e-08, 0.01, 10

VMEM_LIMIT = 56 * 1024 * 1024
VMEM_LIMIT_ROWS = 60 * 1024 * 1024

NN = (((1,), (0,)), ((), ()))
NT = (((1,), (1,)), ((), ()))
TN = (((0,), (0,)), ((), ()))


_PENDING = []


def _raw_call(body, **kw):
    return pl.pallas_call(body, **kw)


def _pcall(body, **kw):
    if "in_specs" not in kw:
        return _raw_call(body, **kw)
    deps = list(_PENDING)
    del _PENDING[:]
    if not deps:
        return _raw_call(body, **kw)
    n_in = len(kw["in_specs"])

    def tied(*refs):
        return body(*refs[:n_in], *refs[n_in + len(deps):])

    kw["in_specs"] = list(kw["in_specs"]) + [pl.BlockSpec(memory_space=pl.ANY)] * len(deps)
    call = _raw_call(tied, **kw)
    return lambda *ops: call(*ops, *deps)


def _cparams(sem=None):
    if sem is None:
        return pltpu.CompilerParams(vmem_limit_bytes=VMEM_LIMIT)
    return pltpu.CompilerParams(vmem_limit_bytes=VMEM_LIMIT, dimension_semantics=sem)


def _dot(a, b, dn):
    return lax.dot_general(a.astype(BF16), b.astype(BF16), dn, preferred_element_type=F32)


def _sigmoid(x):
    return 1.0 / (1.0 + jnp.exp(-x))


def _silu(x):
    return x * _sigmoid(x)


def _dsilu(x):
    s = _sigmoid(x)
    return s * (1.0 + x * (1.0 - s))


def _matmul(name, pairs, pair_specs, dn, grid, out_shapes, out_specs, nred=1, extra=(), extra_specs=(),
            epilogue=None, acc_shape=None, alias=None):
    n_in = 2 * len(pairs) + len(extra)
    n_out = len(out_shapes)

    def body(*refs):
        ins = refs[:2 * len(pairs)]
        ex = [r for r, sp in zip(refs[2 * len(pairs):n_in], extra_specs) if sp.memory_space != pl.ANY]
        outs = refs[n_in:n_in + n_out]

        def prod():
            s = None
            for p in range(len(pairs)):
                d = _dot(ins[2 * p][...], ins[2 * p + 1][...], dn)
                s = d if s is None else s + d
            return s

        def finish(val):
            if epilogue is None:
                outs[0][...] = val.astype(outs[0].dtype)
            else:
                res = epilogue(val, *[e[...] for e in ex])
                for o, r in zip(outs, res):
                    o[...] = r.astype(o.dtype)

        if nred == 1:
            finish(prod())
        else:
            acc = refs[n_in + n_out]
            k = pl.program_id(len(grid) - 1)

            @pl.when(k == 0)
            def _():
                acc[...] = jnp.zeros_like(acc)

            acc[...] += prod()

            @pl.when(k == nred - 1)
            def _():
                finish(acc[...])

    operands = []
    specs = []
    for (a, b), (sa, sb) in zip(pairs, pair_specs):
        operands += [a, b]
        specs += [sa, sb]
    operands += list(extra)
    specs += list(extra_specs)
    scratch = [pltpu.VMEM(acc_shape, F32)] if nred > 1 else []
    sem = ("arbitrary",) * len(grid)
    res = _pcall(body, name=name, grid=grid, in_specs=specs, out_specs=list(out_specs), input_output_aliases=dict(alias or {}),
                 out_shape=list(out_shapes), scratch_shapes=scratch, compiler_params=_cparams(sem))(*operands)
    return res


def _sds(shape, dtype):
    return jax.ShapeDtypeStruct(shape, dtype)


def _row_tile(T):
    return min(1024, T)


def mm_nn(name, a, b, tn, out_dtype=F32, residual=None):
    M, K = a.shape
    N = b.shape[1]
    tm = _row_tile(M)
    grid = (M // tm, N // tn)
    extra, especs, epi = (), (), None
    if residual is not None:
        extra = (residual,)
        especs = (pl.BlockSpec((tm, tn), lambda i, n: (i, n)),)
        epi = lambda v, r: (v + r,)
    return _matmul(name, [(a, b)], [(pl.BlockSpec((tm, K), lambda i, n: (i, 0)), pl.BlockSpec((K, tn), lambda i, n: (0, n)))],
                   NN, grid, [_sds((M, N), out_dtype)], [pl.BlockSpec((tm, tn), lambda i, n: (i, n))],
                   extra=extra, extra_specs=especs, epilogue=epi)[0]


def mm_nn_colblk(name, a, b, out_dtype=F32):
    M, K = a.shape
    J, _, nb = b.shape
    tm = _row_tile(M)
    grid = (M // tm, J)
    return _matmul(name, [(a, b)], [(pl.BlockSpec((tm, K), lambda i, j: (i, 0)), pl.BlockSpec((None, K, nb), lambda i, j: (j, 0, 0)))],
                   NN, grid, [_sds((M, J * nb), out_dtype)], [pl.BlockSpec((tm, nb), lambda i, j: (i, j))])[0]


def mm_nt(name, a, w, tr, out_dtype=F32):
    M, C = a.shape
    R = w.shape[0]
    tm = _row_tile(M)
    grid = (M // tm, R // tr)
    return _matmul(name, [(a, w)], [(pl.BlockSpec((tm, C), lambda i, r: (i, 0)), pl.BlockSpec((tr, C), lambda i, r: (r, 0)))],
                   NT, grid, [_sds((M, R), out_dtype)], [pl.BlockSpec((tm, tr), lambda i, r: (i, r))])[0]


def mm_nn_red(name, a, b, tn, tk, out_dtype=F32, residual=None, rows=None, prev=None, tm=None):
    M, K = a.shape
    N = b.shape[1]
    tm = min(tm or _row_tile(M), M)
    nk = K // tk
    i0, ni = (0, M // tm) if rows is None else rows
    grid = (ni, N // tn, nk)
    ospec = pl.BlockSpec((tm, tn), lambda i, n, k: (i + i0, n))
    extra, especs, epi = [], [], None
    if residual is not None:
        extra, especs, epi = [residual], [ospec], (lambda v, r, *_: (v + r,))
    alias = {}
    if prev is not None:
        alias = {2 + len(extra): 0}
        extra, especs = extra + [prev], especs + [_ANY]
        epi = epi or (lambda v, *_: (v,))
    return _matmul(name, [(a, b)], [(pl.BlockSpec((tm, tk), lambda i, n, k: (i + i0, k)), pl.BlockSpec((tk, tn), lambda i, n, k: (k, n)))],
                   NN, grid, [_sds((M, N), out_dtype)], [ospec], nred=nk, acc_shape=(tm, tn),
                   extra=extra, extra_specs=especs, epilogue=epi, alias=alias)[0]


def mm_tn(name, x, dy, tr, tc, out_dtype=BF16):
    M, R = x.shape
    C = dy.shape[1]
    grid = (R // tr, C // tc)
    return _matmul(name, [(x, dy)], [(pl.BlockSpec((M, tr), lambda r, c: (0, r)), pl.BlockSpec((M, tc), lambda r, c: (0, c)))],
                   TN, grid, [_sds((R, C), out_dtype)], [pl.BlockSpec((tr, tc), lambda r, c: (r, c))])[0]


def mm_tn_colblk(name, x, dy, nb, out_dtype=BF16):
    M, R = x.shape
    J = dy.shape[1] // nb
    grid = (J,)
    return _matmul(name, [(x, dy)], [(pl.BlockSpec((M, R), lambda j: (0, 0)), pl.BlockSpec((M, nb), lambda j: (0, j)))],
                   TN, grid, [_sds((J, R, nb), out_dtype)], [pl.BlockSpec((None, R, nb), lambda j: (j, 0, 0))])[0]


def _rows(T):
    return min(256, T)


def rms_fwd(name, x, g, eps=NORM_EPS):
    T, D = x.shape
    tm = _rows(T)

    def body(x_ref, g_ref, o_ref):
        xv = x_ref[...]
        r = lax.rsqrt(jnp.mean(xv * xv, axis=-1, keepdims=True) + eps)
        o_ref[...] = (xv * r * g_ref[...]).astype(BF16)

    return _pcall(body, name=name, grid=(T // tm,),
                  in_specs=[pl.BlockSpec((tm, D), lambda i: (i, 0)), pl.BlockSpec((1, D), lambda i: (0, 0))],
                  out_specs=pl.BlockSpec((tm, D), lambda i: (i, 0)), out_shape=_sds((T, D), BF16),
                  compiler_params=_cparams(("arbitrary",)))(x, g)


def rms_bwd(name, x, g, dy, dres, eps=NORM_EPS):
    T, D = x.shape
    tm = _rows(T)

    def body(x_ref, g_ref, dy_ref, dr_ref, dx_ref, dxb_ref, dg_ref):
        i = pl.program_id(0)
        xv = x_ref[...]
        r = lax.rsqrt(jnp.mean(xv * xv, axis=-1, keepdims=True) + eps)
        xh = xv * r
        dyv = dy_ref[...]
        gd = dyv * g_ref[...]
        dx = r * (gd - xh * jnp.mean(gd * xh, axis=-1, keepdims=True)) + dr_ref[...]
        dx_ref[...] = dx
        dxb_ref[...] = dx.astype(BF16)

        @pl.when(i == 0)
        def _():
            dg_ref[...] = jnp.zeros_like(dg_ref)

        dg_ref[...] += jnp.sum(dyv * xh, axis=0, keepdims=True)

    row = pl.BlockSpec((tm, D), lambda i: (i, 0))
    vec = pl.BlockSpec((1, D), lambda i: (0, 0))
    return _pcall(body, name=name, grid=(T // tm,), in_specs=[row, vec, row, row], out_specs=[row, row, vec],
                  out_shape=[_sds((T, D), F32), _sds((T, D), BF16), _sds((1, D), F32)],
                  compiler_params=_cparams(("arbitrary",)))(x, g, dy, dres)


def gnorm_fwd(y, projp, g):
    T, D = y.shape
    tm = _rows(T)

    def body(y_ref, z_ref, g_ref, o_ref):
        yz = y_ref[...] * _silu(z_ref[...])
        r = lax.rsqrt(jnp.mean(yz * yz, axis=-1, keepdims=True) + SSM_NORM_EPS)
        o_ref[...] = (yz * r * g_ref[...]).astype(BF16)

    row = pl.BlockSpec((tm, D), lambda i: (i, 0))
    return _pcall(body, name="gnorm_fwd", grid=(T // tm,),
                  in_specs=[row, pl.BlockSpec((tm, D), lambda i: (i, OFF_Z // D)), pl.BlockSpec((1, D), lambda i: (0, 0))],
                  out_specs=row, out_shape=_sds((T, D), BF16), compiler_params=_cparams(("arbitrary",)))(y, projp, g)


def gnorm_bwd(y, projp, g, dyn, dproj):
    T, D = y.shape
    tm = _rows(T)

    def body(y_ref, z_ref, g_ref, dyn_ref, _, dy_ref, dz_ref, dg_ref):
        i = pl.program_id(0)
        yv, zv = y_ref[...], z_ref[...]
        sz = _silu(zv)
        yz = yv * sz
        r = lax.rsqrt(jnp.mean(yz * yz, axis=-1, keepdims=True) + SSM_NORM_EPS)
        xh = yz * r
        dv = dyn_ref[...]
        gd = dv * g_ref[...]
        dyz = r * (gd - xh * jnp.mean(gd * xh, axis=-1, keepdims=True))
        dy_ref[...] = dyz * sz
        dz_ref[...] = (dyz * yv * _dsilu(zv)).astype(BF16)

        @pl.when(i == 0)
        def _():
            dg_ref[...] = jnp.zeros_like(dg_ref)

        dg_ref[...] += jnp.sum(dv * xh, axis=0, keepdims=True)

    row = pl.BlockSpec((tm, D), lambda i: (i, 0))
    vec = pl.BlockSpec((1, D), lambda i: (0, 0))
    return _pcall(body, name="gnorm_bwd", grid=(T // tm,),
                  in_specs=[row, pl.BlockSpec((tm, D), lambda i: (i, OFF_Z // D)), vec, row, _ANY],
                  out_specs=[row, pl.BlockSpec((tm, D), lambda i: (i, OFF_Z // D)), vec],
                  out_shape=[_sds((T, D), F32), _sds(dproj.shape, dproj.dtype), _sds((1, D), F32)], input_output_aliases={4: 1},
                  compiler_params=_cparams(("arbitrary",)))(y, projp, g, dyn, dproj)


def merge_fwd(projp, out_a, out_s):
    T, D = out_a.shape
    tm = _rows(T)

    def body(ga_ref, gs_ref, a_ref, s_ref, o_ref):
        o_ref[...] = (_sigmoid(ga_ref[...]) * a_ref[...] + _sigmoid(gs_ref[...]) * s_ref[...]).astype(BF16)

    row = pl.BlockSpec((tm, D), lambda i: (i, 0))
    return _pcall(body, name="merge_fwd", grid=(T // tm,),
                  in_specs=[pl.BlockSpec((tm, D), lambda i: (i, OFF_GA // D)), pl.BlockSpec((tm, D), lambda i: (i, OFF_GS // D)), row, row],
                  out_specs=row, out_shape=_sds((T, D), BF16), compiler_params=_cparams(("arbitrary",)))(projp, projp, out_a, out_s)


def merge_bwd(projp, out_a, out_s, dmerged):
    T, D = out_a.shape
    tm = _rows(T)
    assert OFF_GA == 0 and OFF_GS == D

    def body(ga_ref, gs_ref, a_ref, s_ref, dm_ref, da_ref, ds_ref, dp_ref):
        dm = dm_ref[...]
        sa, ss = _sigmoid(ga_ref[...]), _sigmoid(gs_ref[...])
        da_ref[...] = (dm * sa).astype(BF16)
        ds_ref[...] = (dm * ss).astype(BF16)
        dp_ref[:, :D] = (dm * a_ref[...] * sa * (1.0 - sa)).astype(BF16)
        dp_ref[:, D:] = (dm * s_ref[...] * ss * (1.0 - ss)).astype(BF16)

    row = pl.BlockSpec((tm, D), lambda i: (i, 0))
    return _pcall(body, name="merge_bwd", grid=(T // tm,),
                  in_specs=[pl.BlockSpec((tm, D), lambda i: (i, OFF_GA // D)), pl.BlockSpec((tm, D), lambda i: (i, OFF_GS // D)), row, row, row],
                  out_specs=[row, row, pl.BlockSpec((tm, 2 * D), lambda i: (i, 0))],
                  out_shape=[_sds((T, D), BF16), _sds((T, D), BF16), _sds((T, IN_PAD), BF16)],
                  compiler_params=_cparams(("arbitrary",)))(projp, projp, out_a, out_s, dmerged)


def head_fwd_bwd(h2, pg, pp, g_final, target):
    T, D = h2.shape
    tm = _rows(T)

    def body(h_ref, pg_ref, pp_ref, g_ref, t_ref, loss_ref, dh_ref, dpg_ref, dpp_ref, dg_ref):
        i = pl.program_id(0)
        s = _sigmoid(pg_ref[...])
        ppv = pp_ref[...]
        h3 = h_ref[...] + s * ppv
        r = lax.rsqrt(jnp.mean(h3 * h3, axis=-1, keepdims=True) + NORM_EPS)
        xh = h3 * r
        gv = g_ref[...]
        e = xh * gv - t_ref[...]
        dyo = e * (1.0 / D)
        gd = dyo * gv
        dh = r * (gd - xh * jnp.mean(gd * xh, axis=-1, keepdims=True))
        dh_ref[...] = dh
        dpg_ref[...] = (dh * ppv * s * (1.0 - s)).astype(BF16)
        dpp_ref[...] = (dh * s).astype(BF16)

        @pl.when(i == 0)
        def _():
            dg_ref[...] = jnp.zeros_like(dg_ref)
            loss_ref[...] = jnp.zeros_like(loss_ref)

        dg_ref[...] += jnp.sum(dyo * xh, axis=0, keepdims=True)
        part = 0.5 * jnp.sum(jnp.mean(e * e, axis=-1, keepdims=True), axis=0, keepdims=True)
        loss_ref[...] += jnp.broadcast_to(part, loss_ref.shape)

    row = pl.BlockSpec((tm, D), lambda i: (i, 0))
    vec = pl.BlockSpec((1, D), lambda i: (0, 0))
    return _pcall(body, name="head_fwd_bwd", grid=(T // tm,), in_specs=[row, row, row, vec, row],
                  out_specs=[pl.BlockSpec((1, 128), lambda i: (0, 0)), row, row, row, vec],
                  out_shape=[_sds((1, 128), F32), _sds((T, D), F32), _sds((T, D), BF16), _sds((T, D), BF16), _sds((1, D), F32)],
                  compiler_params=_cparams(("arbitrary",)))(h2, pg, pp, g_final, target)


FFN_TILE = 512


def ffn_up(f, wgt, wut):
    T, D = f.shape
    H = wgt.shape[0]
    tm = _row_tile(T)

    def body(f_ref, wg_ref, wu_ref, g_ref, u_ref, a_ref):
        fv = f_ref[...]
        g = _dot(fv, wg_ref[...], NT)
        u = _dot(fv, wu_ref[...], NT)
        g_ref[...] = g.astype(BF16)
        u_ref[...] = u.astype(BF16)
        a_ref[...] = (_silu(g) * u).astype(BF16)

    wspec = pl.BlockSpec((FFN_TILE, D), lambda i, j: (j, 0))
    ospec = pl.BlockSpec((tm, FFN_TILE), lambda i, j: (i, j))
    return _pcall(body, name="ffn_up", grid=(T // tm, H // FFN_TILE), in_specs=[pl.BlockSpec((tm, D), lambda i, j: (i, 0)), wspec, wspec],
                  out_specs=[ospec] * 3, out_shape=[_sds((T, H), BF16)] * 3,
                  compiler_params=_cparams(("arbitrary", "arbitrary")))(f, wgt, wut)


def ffn_down_bwd(dh2b, wd, gate, up):
    T, D = dh2b.shape
    H = wd.shape[0]
    tm = _row_tile(T)
    ospec = pl.BlockSpec((tm, FFN_TILE), lambda i, j: (i, j))

    def epi(da, g, u):
        g, u = g.astype(F32), u.astype(F32)
        return (da * u * _dsilu(g), da * _silu(g))

    return _matmul("ffn_down_bwd", [(dh2b, wd)],
                   [(pl.BlockSpec((tm, D), lambda i, j: (i, 0)), pl.BlockSpec((FFN_TILE, D), lambda i, j: (j, 0)))],
                   NT, (T // tm, H // FFN_TILE), [_sds((T, H), BF16)] * 2, [ospec, ospec],
                   extra=(gate, up), extra_specs=(ospec, ospec), epilogue=epi)


def ffn_up_bwd(dgate, dup, wgt, wut):
    T, H = dgate.shape
    D = wgt.shape[1]
    tm = min(512, T)
    tn = 512
    aspec = pl.BlockSpec((tm, H), lambda i, n: (i, 0))
    wspec = pl.BlockSpec((H, tn), lambda i, n: (0, n))
    return _matmul("ffn_up_bwd", [(dgate, wgt), (dup, wut)], [(aspec, wspec), (aspec, wspec)], NN, (T // tm, D // tn),
                   [_sds((T, D), F32)], [pl.BlockSpec((tm, tn), lambda i, n: (i, n))])[0]


def attn_br_bwd(dout_a, wab):
    T, D = dout_a.shape
    J, R, nb = wab.shape
    tm = _row_tile(T)
    return _matmul("attn_br_bwd", [(dout_a, wab)],
                   [(pl.BlockSpec((tm, nb), lambda i, j: (i, j)), pl.BlockSpec((None, R, nb), lambda i, j: (j, 0, 0)))],
                   NT, (T // tm, J), [_sds((T, R), BF16)], [pl.BlockSpec((tm, R), lambda i, j: (i, 0))], nred=J, acc_shape=(tm, R))[0]


def _adam_math(w, g, m, v):
    m2 = ADAM_B1 * m + (1.0 - ADAM_B1) * g
    v2 = ADAM_B2 * v + (1.0 - ADAM_B2) * (g * g)
    m_hat = m2 / (1.0 - ADAM_B1 ** ADAM_STEP)
    v_hat = v2 / (1.0 - ADAM_B2 ** ADAM_STEP)
    delta = -ADAM_LR * (m_hat / (jnp.sqrt(v_hat) + ADAM_EPS) + ADAM_WD * w)
    return delta, m2, v2


def _sum_partials(own, parts):
    g = None if own is None else own.astype(F32)
    if parts is not None:
        for s in range(parts.shape[0]):
            t = parts[s].astype(F32)
            g = t if g is None else g + t
    return g


def adamw(name, parts, w, m, v, own=None, own_slot=None):
    R, C = w.shape
    tr, tc = R, C
    for cand in (256, 176, 128, 64, 32, 16, 8):
        if R % cand == 0 and R > cand:
            tr = cand
            break
    if tr == R and R > 256:
        tc = 256
    given = [a for a in (parts, own) if a is not None]
    pre = own_slot is not None

    def body(*refs):
        refs = refs[1:] if pre else refs
        p_ref = refs[0] if parts is not None else None
        o_ref = refs[len(given) - 1] if own is not None else None
        w_ref, m_ref, v_ref, g_ref, d_ref, m2_ref, v2_ref = refs[-7:]
        g = _sum_partials(None if o_ref is None else o_ref[...], p_ref)
        d, m2, v2 = _adam_math(w_ref[...], g, m_ref[...], v_ref[...])
        g_ref[...] = g
        d_ref[...] = d
        m2_ref[...] = m2
        v2_ref[...] = v2

    blk = pl.BlockSpec((tr, tc), lambda i, j, *s: (i, j))
    specs = [] if parts is None else [pl.BlockSpec((parts.shape[0], tr, tc), lambda i, j, *s: (0, i, j))]
    if own is not None:
        specs.append(pl.BlockSpec((None, tr, tc), lambda i, j, s: (s[0], i, j)) if pre else blk)
    specs += [blk] * 3
    grid = (R // tr, C // tc)
    out_shape = [_sds((R, C), F32)] * 4
    params = _cparams(("arbitrary", "arbitrary"))
    if not pre:
        return _pcall(body, name=name, grid=grid, in_specs=specs, out_specs=[blk] * 4, out_shape=out_shape,
                      compiler_params=params)(*given, w, m, v)
    spec = pltpu.PrefetchScalarGridSpec(num_scalar_prefetch=1, grid=grid, in_specs=specs, out_specs=[blk] * 4)
    return _pcall(body, name=name, grid_spec=spec, out_shape=out_shape,
                  compiler_params=params)(jnp.asarray(own_slot, jnp.int32).reshape(1), *given, w, m, v)


_HBM = pl.BlockSpec(memory_space=pltpu.HBM)
_SEM = pl.BlockSpec(memory_space=pltpu.SEMAPHORE)
_ANY = pl.BlockSpec(memory_space=pl.ANY)
_SPLIT_PARAMS = dict(compiler_params=pltpu.CompilerParams(has_side_effects=pltpu.SideEffectType.DATAFLOW_SIDE_EFFECTING))
ICI_SAME_CORE = (2, 4, 6)
ALL_PEERS = (1, 2, 3, 4, 5, 6, 7)
LAND_SLOTS = {"gather": N_DEV, "scatter": N_DEV - 1, "pair": 4, "scatter_core": 3}


def _mesh_pos():
    x, y, c = lax.axis_index("x"), lax.axis_index("y"), lax.axis_index("c")
    return x, y, c, 4 * x + 2 * y + c


def _peer_of(k, x, y, c):
    px = 1 - x if k & 4 else x
    py = 1 - y if k & 2 else y
    pc = 1 - c if k & 1 else c
    return (px, py, pc), 4 * px + 2 * py + pc


def _split_copies(mode, ks, srcs, lands, send_sems, recv_sems):
    x, y, c, me = _mesh_pos()
    pairs = []
    for a in range(len(lands)):
        for j, k in enumerate(ks):
            dev, peer = _peer_of(k if mode != "route" else k[0], x, y, c)
            i = a * len(ks) + j
            if mode == "gather":
                s_out, d_out, d_in = lands[a].at[me], lands[a].at[me], lands[a].at[peer]
            elif mode == "scatter":
                s_out, d_out, d_in = srcs[a].at[peer], lands[a].at[k - 1], lands[a].at[k - 1]
            elif mode == "pair":
                dev, _ = _peer_of(1, x, y, c)
                _, theirs = _peer_of(k | 1, x, y, c)
                s_out, d_out, d_in = srcs[a].at[theirs], lands[a].at[j], lands[a].at[j]
            elif mode == "scatter_core":
                s_out, d_out, d_in = srcs[a].at[j + 1], lands[a].at[j], lands[a].at[j]
            elif mode == "route":
                k, rel, half = k
                dev, _ = _peer_of(k, x, y, c)
                _, held = _peer_of(rel, x, y, c)
                _, theirs = _peer_of(k ^ rel, x, y, c)
                cols = lands[a].shape[-1] // 2
                cut = (slice(None), slice(None)) if half is None else (slice(None), pl.ds(half * cols, cols))
                s_out, d_out, d_in = lands[a].at[held].at[cut], lands[a].at[held].at[cut], lands[a].at[theirs].at[cut]
            else:
                dev, _ = _peer_of(1, x, y, c)
                _, theirs = _peer_of(k | 1, x, y, c)
                s_out, d_out, d_in = lands[a].at[peer], lands[a].at[peer], lands[a].at[theirs]
            both = [pltpu.make_async_remote_copy(src_ref=s_out, dst_ref=d, send_sem=send_sems.at[i], recv_sem=recv_sems.at[i],
                                                 device_id=dev, device_id_type=pl.DeviceIdType.MESH) for d in (d_out, d_in)]
            pairs.append(tuple(both))
    return pairs


def split_start(name, mode, ks, srcs, lands=None, after=None):
    n, nk = len(srcs) if lands is None else len(lands), len(ks)
    srcs = [pltpu.with_memory_space_constraint(s, pltpu.HBM) for s in srcs]
    if lands is None:
        shapes = [((N_DEV,) + s.shape) if mode == "gather" else ((LAND_SLOTS[mode],) + s.shape[1:]) for s in srcs]
        lands = [lax.empty(shp, s.dtype) for shp, s in zip(shapes, srcs)]
    lands = [pltpu.with_memory_space_constraint(l, pltpu.HBM) for l in lands]
    both = srcs + lands
    extra = [] if after is None else [after]

    def body(*refs):
        src_refs, land_refs = refs[:len(srcs)], refs[len(srcs):len(both)]
        send_sems, recv_sems = refs[len(both) + len(extra)], refs[len(both) + len(extra) + 1]
        token = refs[-1]
        for out, _ in _split_copies(mode, ks, src_refs, land_refs, send_sems, recv_sems):
            out.start()
        token[...] = jnp.zeros_like(token)

    out_shape = (pltpu.SemaphoreType.DMA((n * nk,)), pltpu.SemaphoreType.DMA((n * nk,)),
                 *[pltpu.HBM(a.shape, a.dtype) for a in both], _sds((8, 128), F32))
    res = _raw_call(body, name=name, out_shape=out_shape, in_specs=[_HBM] * len(both) + [_ANY] * len(extra),
                    out_specs=(_SEM, _SEM, *[_HBM] * len(both), pl.BlockSpec(memory_space=pltpu.VMEM)),
                    input_output_aliases={i: 2 + i for i in range(len(both))}, **_SPLIT_PARAMS)(*both, *extra)
    _PENDING.append(res[-1])
    return dict(mode=mode, ks=ks, sems=(res[0], res[1]), srcs=list(res[2:2 + len(srcs)]),
                lands=list(res[2 + len(srcs):2 + len(both)]), token=res[-1])


def split_wait(name, h, after):
    ns = len(h["srcs"])
    both = h["srcs"] + h["lands"]
    after = list(after) if isinstance(after, (list, tuple)) else [after]

    def body(*refs):
        src_refs, land_refs = refs[:ns], refs[ns:len(both)]
        send_sems, recv_sems = refs[len(both)], refs[len(both) + 1]
        for out, arriving in _split_copies(h["mode"], h["ks"], src_refs, land_refs, send_sems, recv_sems):
            out.wait_send()
            arriving.wait_recv()

    res = _raw_call(body, name=name, out_shape=tuple(pltpu.HBM(a.shape, a.dtype) for a in both),
                    in_specs=[_HBM] * len(both) + [_SEM, _SEM] + [_ANY] * len(after), out_specs=tuple([_HBM] * len(both)),
                    input_output_aliases={i: i for i in range(len(both))}, **_SPLIT_PARAMS)(*both, *h["sems"], *after)
    return list(res[:ns]), list(res[ns:])


FORWARD_BLOCKS = (0, 2, 4, 6)


def pair_sum(name, mine, slots, theirs):
    P, R, C = theirs.shape
    tc = 512

    def body(s_ref, a_ref, b_ref, o_ref):
        o_ref[...] = (a_ref[...].astype(F32) + b_ref[...].astype(F32)).astype(o_ref.dtype)

    blk = pl.BlockSpec((None, R, tc), lambda p, i, s: (p, 0, i))
    spec = pltpu.PrefetchScalarGridSpec(num_scalar_prefetch=1, grid=(P, C // tc),
                                        in_specs=[pl.BlockSpec((None, R, tc), lambda p, i, s: (s[p], 0, i)), blk], out_specs=blk)
    return _pcall(body, name=name, grid_spec=spec, out_shape=_sds((P, R, C), theirs.dtype),
                  compiler_params=_cparams(("arbitrary", "arbitrary")))(slots, mine, theirs)


def _rope_parts(pos_ref, inv_ref):
    ang = pos_ref[...] * inv_ref[...]
    return jnp.cos(ang), jnp.sin(ang)


def _rot_half(t):
    lane = lax.broadcasted_iota(jnp.int32, t.shape, 1)
    return jnp.where((lane % HEAD_DIM) < HEAD_DIM // 2, -pltpu.roll(t, 128 - HEAD_DIM // 2, 1), pltpu.roll(t, HEAD_DIM // 2, 1))


def _attn_mask(n):
    row = lax.broadcasted_iota(jnp.int32, (BLK, 2 * BLK), 0)
    col = lax.broadcasted_iota(jnp.int32, (BLK, 2 * BLK), 1)
    dist = row + BLK - col
    return (dist >= 0) & (dist < BLK) & ((n * BLK - BLK + col) >= 0)


def _attn_specs(T):
    prev = lambda n: jnp.maximum(n - 1, 0)
    kc = pl.BlockSpec((BLK, KV_DIM), lambda n: (n, OFF_K // KV_DIM))
    kp = pl.BlockSpec((BLK, KV_DIM), lambda n: (prev(n), OFF_K // KV_DIM))
    vc = pl.BlockSpec((BLK, KV_DIM), lambda n: (n, OFF_V // KV_DIM))
    vp = pl.BlockSpec((BLK, KV_DIM), lambda n: (prev(n), OFF_V // KV_DIM))
    pc = pl.BlockSpec((BLK, 1), lambda n: (n, 0))
    pp = pl.BlockSpec((BLK, 1), lambda n: (prev(n), 0))
    inv = pl.BlockSpec((1, 128), lambda n: (0, 0))
    sink = pl.BlockSpec(memory_space=pltpu.SMEM)
    return kc, kp, vc, vp, pc, pp, inv, sink


def _softmax_sink(sc, valid, sink):
    sc = jnp.where(valid, sc * (HEAD_DIM ** -0.5), -1e30)
    m = jnp.maximum(jnp.max(sc, axis=1, keepdims=True), sink)
    e = jnp.exp(sc - m)
    es = jnp.exp(sink - m)
    den = jnp.sum(e, axis=1, keepdims=True) + es
    return e / den, es / den


def attn_fwd(projp, posf, inv128, sinks):
    T = projp.shape[0]
    kc, kp, vc, vp, pc, pp, inv, sink = _attn_specs(T)

    def body(q_ref, kc_ref, kp_ref, vc_ref, vp_ref, pc_ref, pp_ref, inv_ref, sink_ref, o_ref, qr_ref, kr_ref):
        n = pl.program_id(0)
        cos_c, sin_c = _rope_parts(pc_ref, inv_ref)
        cos_p, sin_p = _rope_parts(pp_ref, inv_ref)
        valid = _attn_mask(n)
        k_c, k_p = [], []
        for s in range(KV_DIM // 128):
            t = kc_ref[:, 128 * s:128 * (s + 1)]
            k_c.append((t * cos_c + _rot_half(t) * sin_c).astype(BF16))
            kr_ref[:, 128 * s:128 * (s + 1)] = k_c[s]
            t = kp_ref[:, 128 * s:128 * (s + 1)]
            k_p.append((t * cos_p + _rot_half(t) * sin_p).astype(BF16))
        kcat, vcat = [], []
        for hk in range(KV_HEADS):
            lo = HEAD_DIM * (hk % 2)
            kcat.append(jnp.concatenate([k_p[hk // 2][:, lo:lo + HEAD_DIM], k_c[hk // 2][:, lo:lo + HEAD_DIM]], axis=0))
            vcat.append(jnp.concatenate([vp_ref[:, HEAD_DIM * hk:HEAD_DIM * (hk + 1)], vc_ref[:, HEAD_DIM * hk:HEAD_DIM * (hk + 1)]], axis=0)
                        .astype(BF16))
        q_heads = []
        for s in range(Q_DIM // 128):
            t = q_ref[:, 128 * s:128 * (s + 1)]
            qs = (t * cos_c + _rot_half(t) * sin_c).astype(BF16)
            qr_ref[:, 128 * s:128 * (s + 1)] = qs
            q_heads += [qs[:, :HEAD_DIM], qs[:, HEAD_DIM:]]
        G = ATTN_HEADS // KV_HEADS
        scores = [_dot(q_heads[hq], kcat[hq // G], NT) for hq in range(ATTN_HEADS)]
        probs = [_softmax_sink(scores[hq], valid, sink_ref[0, hq])[0] for hq in range(ATTN_HEADS)]
        outs = [_dot(probs[hq], vcat[hq // G], NN) for hq in range(ATTN_HEADS)]
        for s in range(Q_DIM // 128):
            o_ref[:, 128 * s:128 * (s + 1)] = jnp.concatenate([outs[2 * s], outs[2 * s + 1]], axis=1).astype(BF16)

    qspec = pl.BlockSpec((BLK, Q_DIM), lambda n: (n, OFF_Q // Q_DIM))
    orow = pl.BlockSpec((BLK, Q_DIM), lambda n: (n, 0))
    krow = pl.BlockSpec((BLK, KV_DIM), lambda n: (n, 0))
    return _pcall(body, name="attn_fwd", grid=(T // BLK,), in_specs=[qspec, kc, kp, vc, vp, pc, pp, inv, sink],
                  out_specs=[orow, orow, krow], out_shape=[_sds((T, Q_DIM), BF16), _sds((T, Q_DIM), BF16), _sds((T, KV_DIM), BF16)],
                  compiler_params=_cparams(("arbitrary",)))(projp, projp, projp, projp, projp, posf, posf, inv128, sinks)


def attn_bwd(qr, kr, projp, dattn, posf, inv128, sinks, dproj):
    T = projp.shape[0]
    _, _, vc, vp, pc, pp, inv, sink = _attn_specs(T)
    G = ATTN_HEADS // KV_HEADS

    def body(qr_ref, krc_ref, krp_ref, vc_ref, vp_ref, do_ref, pc_ref, pp_ref, inv_ref, sink_ref, _, dq_ref, dk_ref, dv_ref, dsk_ref):
        n = pl.program_id(0)

        @pl.when(n == 0)
        def _():
            dk_ref[...] = jnp.zeros_like(dk_ref)
            dv_ref[...] = jnp.zeros_like(dv_ref)
            dsk_ref[...] = jnp.zeros_like(dsk_ref)

        cos_c, sin_c = _rope_parts(pc_ref, inv_ref)
        cos_p, sin_p = _rope_parts(pp_ref, inv_ref)
        valid = _attn_mask(n)
        lane = lax.broadcasted_iota(jnp.int32, (1, 128), 1)
        kcat, vcat = [], []
        for hk in range(KV_HEADS):
            ksl = slice(HEAD_DIM * hk, HEAD_DIM * (hk + 1))
            kcat.append(jnp.concatenate([krp_ref[:, ksl], krc_ref[:, ksl]], axis=0))
            vcat.append(jnp.concatenate([vp_ref[:, ksl], vc_ref[:, ksl]], axis=0).astype(BF16))
        H = range(ATTN_HEADS)
        q_heads = [qr_ref[:, HEAD_DIM * hq:HEAD_DIM * (hq + 1)] for hq in H]
        do_heads = [do_ref[:, HEAD_DIM * hq:HEAD_DIM * (hq + 1)] for hq in H]
        soft = [_softmax_sink(_dot(q_heads[hq], kcat[hq // G], NT), valid, sink_ref[0, hq]) for hq in H]
        dps = [_dot(do_heads[hq], vcat[hq // G], NT) for hq in H]
        deltas = [jnp.sum(soft[hq][0] * dps[hq], axis=1, keepdims=True) for hq in H]
        dss = [(soft[hq][0] * (dps[hq] - deltas[hq]) * (HEAD_DIM ** -0.5)).astype(BF16) for hq in H]
        pbs = [soft[hq][0].astype(BF16) for hq in H]
        dsk = jnp.zeros((1, 128), F32)
        for hq in H:
            dsk = dsk + jnp.where(lane == hq, -jnp.sum(soft[hq][1] * deltas[hq], axis=0, keepdims=True), 0.0)
        dsk_ref[...] += dsk
        dq_heads = [_dot(dss[hq], kcat[hq // G], NN) for hq in H]
        dk_parts = [_dot(dss[hq], q_heads[hq], TN) for hq in H]
        dv_parts = [_dot(pbs[hq], do_heads[hq], TN) for hq in H]
        dk_heads = [sum(dk_parts[G * hk + 1:G * (hk + 1)], dk_parts[G * hk]) for hk in range(KV_HEADS)]
        dv_heads = [sum(dv_parts[G * hk + 1:G * (hk + 1)], dv_parts[G * hk]) for hk in range(KV_HEADS)]
        for s in range(Q_DIM // 128):
            t = jnp.concatenate([dq_heads[2 * s], dq_heads[2 * s + 1]], axis=1)
            dq_ref[:, 128 * s:128 * (s + 1)] = (t * cos_c - _rot_half(t) * sin_c).astype(BF16)
        cur = pl.ds(pl.multiple_of(n * BLK, BLK), BLK)
        prv = pl.ds(pl.multiple_of(jnp.maximum(n - 1, 0) * BLK, BLK), BLK)
        for s in range(KV_DIM // 128):
            tc = jnp.concatenate([dk_heads[2 * s][BLK:], dk_heads[2 * s + 1][BLK:]], axis=1)
            tp = jnp.concatenate([dk_heads[2 * s][:BLK], dk_heads[2 * s + 1][:BLK]], axis=1)
            cols = slice(128 * s, 128 * (s + 1))
            dk_ref[cur, cols] += tc * cos_c - _rot_half(tc) * sin_c
            dk_ref[prv, cols] += tp * cos_p - _rot_half(tp) * sin_p
            dv_ref[cur, cols] += jnp.concatenate([dv_heads[2 * s][BLK:], dv_heads[2 * s + 1][BLK:]], axis=1)
            dv_ref[prv, cols] += jnp.concatenate([dv_heads[2 * s][:BLK], dv_heads[2 * s + 1][:BLK]], axis=1)

    qrow = pl.BlockSpec((BLK, Q_DIM), lambda n: (n, 0))
    krc = pl.BlockSpec((BLK, KV_DIM), lambda n: (n, 0))
    krp = pl.BlockSpec((BLK, KV_DIM), lambda n: (jnp.maximum(n - 1, 0), 0))
    whole = pl.BlockSpec((T, KV_DIM), lambda n: (0, 0))
    return _pcall(body, name="attn_bwd", grid=(T // BLK,), in_specs=[qrow, krc, krp, vc, vp, qrow, pc, pp, inv, sink, _ANY],
                  out_specs=[pl.BlockSpec((BLK, Q_DIM), lambda n: (n, OFF_Q // Q_DIM)), whole, whole, pl.BlockSpec((1, 128), lambda n: (0, 0))],
                  out_shape=[_sds(dproj.shape, dproj.dtype), _sds((T, KV_DIM), F32), _sds((T, KV_DIM), F32), _sds((1, 128), F32)],
                  input_output_aliases={10: 0},
                  compiler_params=_cparams(("arbitrary",)))(qr, kr, kr, projp, projp, dattn, posf, posf, inv128, sinks, dproj)


CONV_CB = 256


def _shift_down(x, s):
    row = lax.broadcasted_iota(jnp.int32, x.shape, 0)
    return jnp.where(row >= s, pltpu.roll(x, s, 0), 0.0)


def _shift_up(x, s):
    T = x.shape[0]
    row = lax.broadcasted_iota(jnp.int32, x.shape, 0)
    return jnp.where(row < T - s, pltpu.roll(x, T - s, 0), 0.0)


def _conv_pre(x, w_ref, b_ref):
    acc = x * w_ref[CONV_WIDTH - 1:CONV_WIDTH, :] + b_ref[...]
    for s in range(1, CONV_WIDTH):
        acc = acc + _shift_down(x, s) * w_ref[CONV_WIDTH - 1 - s:CONV_WIDTH - s, :]
    return acc


def conv_fwd(projp, conv_w, conv_b):
    T = projp.shape[0]

    def body(x_ref, w_ref, b_ref, o_ref):
        o_ref[...] = _silu(_conv_pre(x_ref[...], w_ref, b_ref))

    return _pcall(body, name="conv_fwd", grid=(CONV_DIM // CONV_CB,),
                  in_specs=[pl.BlockSpec((T, CONV_CB), lambda c: (0, OFF_XBC // CONV_CB + c)),
                            pl.BlockSpec((CONV_WIDTH, CONV_CB), lambda c: (0, c)), pl.BlockSpec((1, CONV_CB), lambda c: (0, c))],
                  out_specs=pl.BlockSpec((T, CONV_CB), lambda c: (0, c)), out_shape=_sds((T, CONV_DIM), F32),
                  compiler_params=_cparams(("arbitrary",)))(projp, conv_w, conv_b)


def conv_bwd(name, projp, dact, conv_w, conv_b, col0, dproj):
    T, C = dact.shape
    c0 = col0 // CONV_CB

    def body(x_ref, da_ref, w_ref, b_ref, _, dx_ref, dw_ref, db_ref):
        x = x_ref[...]
        dpre = da_ref[...] * _dsilu(_conv_pre(x, w_ref, b_ref))
        dx = dpre * w_ref[CONV_WIDTH - 1:CONV_WIDTH, :]
        dw_ref[CONV_WIDTH - 1:CONV_WIDTH, :] = jnp.sum(dpre * x, axis=0, keepdims=True)
        for s in range(1, CONV_WIDTH):
            i = CONV_WIDTH - 1 - s
            dx = dx + _shift_up(dpre, s) * w_ref[i:i + 1, :]
            dw_ref[i:i + 1, :] = jnp.sum(dpre * _shift_down(x, s), axis=0, keepdims=True)
        dx_ref[...] = dx.astype(BF16)
        db_ref[...] = jnp.sum(dpre, axis=0, keepdims=True)

    return _pcall(body, name=name, grid=(C // CONV_CB,),
                  in_specs=[pl.BlockSpec((T, CONV_CB), lambda c: (0, OFF_XBC // CONV_CB + c0 + c)),
                            pl.BlockSpec((T, CONV_CB), lambda c: (0, c)),
                            pl.BlockSpec((CONV_WIDTH, CONV_CB), lambda c: (0, c0 + c)), pl.BlockSpec((1, CONV_CB), lambda c: (0, c0 + c)), _ANY],
                  out_specs=[pl.BlockSpec((T, CONV_CB), lambda c: (0, OFF_XBC // CONV_CB + c0 + c)),
                             pl.BlockSpec((CONV_WIDTH, CONV_CB), lambda c: (0, c)), pl.BlockSpec((1, CONV_CB), lambda c: (0, c))],
                  out_shape=[_sds(dproj.shape, dproj.dtype), _sds((CONV_WIDTH, C), F32), _sds((1, C), F32)],
                  input_output_aliases={4: 0}, compiler_params=_cparams(("arbitrary",)))(projp, dact, conv_w, conv_b, dproj)


def _softplus(x):
    return jnp.maximum(x, 0.0) + jnp.log1p(jnp.exp(-jnp.abs(x)))


def _tri(lower):
    r = lax.broadcasted_iota(jnp.int32, (BLK, BLK), 0)
    c = lax.broadcasted_iota(jnp.int32, (BLK, BLK), 1)
    return (r >= c) if lower else (c >= r)


def _ssd_chunk_setup(dt_ref, dtb_ref, alog_ref):
    raw = dt_ref[...] + dtb_ref[...]
    dt = _softplus(raw)
    aneg = -jnp.exp(alog_ref[...])
    a = dt * aneg
    acs = jnp.dot(_tri(True).astype(F32), a, precision=lax.Precision.HIGHEST, preferred_element_type=F32)
    return raw, dt, aneg, acs, acs.T


def _ssd_specs(T, rev):
    nc = T // BLK
    ci = (lambda c: nc - 1 - c) if rev else (lambda c: c)
    xs = pl.BlockSpec((BLK, D_INNER), lambda c: (ci(c), 0))
    bm = pl.BlockSpec((BLK, SSM_GROUPS * D_STATE), lambda c: (ci(c), D_INNER // (SSM_GROUPS * D_STATE)))
    cm = pl.BlockSpec((BLK, SSM_GROUPS * D_STATE), lambda c: (ci(c), D_INNER // (SSM_GROUPS * D_STATE) + 1))
    dt = pl.BlockSpec((BLK, DT_PAD), lambda c: (ci(c), OFF_DT // DT_PAD))
    v128 = pl.BlockSpec((1, 128), lambda c: (0, 0))
    dfull = pl.BlockSpec((1, D_INNER), lambda c: (0, 0))
    st = pl.BlockSpec((None, SSM_HEADS, HEAD_DIM, D_STATE), lambda c: (ci(c), 0, 0, 0))
    return xs, bm, cm, dt, v128, dfull, st, ci


GW = HEADS_PER_GROUP * HEAD_DIM


def _expanders():
    e = np.zeros((SSM_GROUPS, 128, GW), np.float32)
    for g in range(SSM_GROUPS):
        for hh in range(HEADS_PER_GROUP):
            e[g, HEADS_PER_GROUP * g + hh, HEAD_DIM * hh:HEAD_DIM * (hh + 1)] = 1.0
    return jnp.asarray(e, BF16), jnp.asarray(np.transpose(e, (0, 2, 1)).copy(), BF16)


def _split2(v):
    hi = lax.bitcast_convert_type(lax.bitcast_convert_type(v, jnp.uint32) & jnp.uint32(0xFFFF0000), F32)
    return hi.astype(BF16), (v - hi).astype(BF16)


def _dotx(a, b):
    if a.dtype == BF16:
        hi, lo = _split2(b)
        return jnp.dot(a, hi, preferred_element_type=F32) + jnp.dot(a, lo, preferred_element_type=F32)
    hi, lo = _split2(a)
    return jnp.dot(hi, b, preferred_element_type=F32) + jnp.dot(lo, b, preferred_element_type=F32)


def _decay(acs, acsT, h, tril):
    return jnp.where(tril, jnp.exp(jnp.where(tril, acs[:, h:h + 1] - acsT[h:h + 1, :], 0.0)), 0.0)


def ssd_fwd(xbc, projp, dtb, alog, dfull):
    T = xbc.shape[0]
    nc = T // BLK
    xs, bm, cm, dts, v128, dfs, st, _ = _ssd_specs(T, False)
    E, _ = _expanders()

    def body(xs_ref, b_ref, c_ref, dt_ref, dtb_ref, alog_ref, d_ref, e_ref, y_ref, st_ref, h_scr):
        c = pl.program_id(0)

        @pl.when(c == 0)
        def _():
            h_scr[...] = jnp.zeros_like(h_scr)

        _, dt, _, acs, acsT = _ssd_chunk_setup(dt_ref, dtb_ref, alog_ref)
        tril = _tri(True)
        alast = acs[BLK - 1:BLK, :]
        eacs = jnp.exp(acs)
        wmat = jnp.exp(alast - acs)
        gam = jnp.exp(alast)
        for g in range(SSM_GROUPS):
            gl = slice(GW * g, GW * (g + 1))
            hsl = slice(HEADS_PER_GROUP * g, HEADS_PER_GROUP * (g + 1))
            heads = [HEADS_PER_GROUP * g + hh for hh in range(HEADS_PER_GROUP)]
            Eg = e_ref[g]
            B = b_ref[:, D_STATE * g:D_STATE * (g + 1)].astype(BF16)
            C = c_ref[:, D_STATE * g:D_STATE * (g + 1)].astype(BF16)
            cb = _dot(C, B, NT)
            x_g = xs_ref[:, gl]
            xd_g = x_g * _dotx(dt, Eg)
            hold = h_scr[hsl]
            st_ref[hsl] = hold
            hcat = hold.reshape(GW, D_STATE)
            yoff = _dotx(eacs, Eg) * _dot(C, hcat, NT)
            S = _dot(xd_g * _dotx(wmat, Eg), B, TN)
            Ms = [cb * _decay(acs, acsT, h, tril) for h in heads]
            ys = [_dot(Ms[hh], xd_g[:, HEAD_DIM * hh:HEAD_DIM * (hh + 1)], NN) for hh in range(HEADS_PER_GROUP)]
            for hh, h in enumerate(heads):
                h_scr[h] = gam[:, h:h + 1] * hold[hh] + S[HEAD_DIM * hh:HEAD_DIM * (hh + 1)]
            y_ref[:, gl] = jnp.concatenate(ys, axis=1) + yoff + d_ref[:, gl] * x_g

    espec = pl.BlockSpec((SSM_GROUPS, 128, GW), lambda c: (0, 0, 0))
    return _pcall(body, name="ssd_fwd", grid=(nc,), in_specs=[xs, bm, cm, dts, v128, v128, dfs, espec],
                  out_specs=[xs, st], out_shape=[_sds((T, D_INNER), F32), _sds((nc, SSM_HEADS, HEAD_DIM, D_STATE), F32)],
                  scratch_shapes=[pltpu.VMEM((SSM_HEADS, HEAD_DIM, D_STATE), F32)],
                  compiler_params=_cparams(("arbitrary",)))(xbc, xbc, xbc, projp, dtb, alog, dfull, E)


def ssd_bwd(xbc, projp, dtb, alog, dfull, states, dy, dproj):
    T = xbc.shape[0]
    nc = T // BLK
    xs, bm, cm, dts, v128, dfs, st, ci = _ssd_specs(T, True)
    gn = SSM_GROUPS * D_STATE
    E, ET = _expanders()

    def body(xs_ref, b_ref, c_ref, dt_ref, dtb_ref, alog_ref, d_ref, st_ref, dy_ref, e_ref, et_ref, _,
             dxs_ref, dB_ref, dC_ref, ddt_ref, dal_ref, dD_ref, ddtb_ref, dh_scr):
        i = pl.program_id(0)

        @pl.when(i == 0)
        def _():
            dh_scr[...] = jnp.zeros_like(dh_scr)
            dal_ref[...] = jnp.zeros_like(dal_ref)
            dD_ref[...] = jnp.zeros_like(dD_ref)
            ddtb_ref[...] = jnp.zeros_like(ddtb_ref)

        raw, dt, aneg, acs, acsT = _ssd_chunk_setup(dt_ref, dtb_ref, alog_ref)
        tril = _tri(True)
        lane = lax.broadcasted_iota(jnp.int32, (BLK, 128), 1)
        sub = lax.broadcasted_iota(jnp.int32, (BLK, 128), 0)
        alast = acs[BLK - 1:BLK, :]
        eacs = jnp.exp(acs)
        wmat = jnp.exp(alast - acs)
        gam = jnp.exp(alast)
        gcol = jnp.exp(acsT[:, BLK - 1:BLK])
        ds_col = jnp.zeros((BLK, 128), F32)
        ds_row = jnp.zeros((BLK, 128), F32)
        ddt_col = jnp.zeros((BLK, 128), F32)
        dDm = jnp.zeros((BLK, 128), F32)
        hl = [slice(HEAD_DIM * hh, HEAD_DIM * (hh + 1)) for hh in range(HEADS_PER_GROUP)]
        for g in range(SSM_GROUPS):
            gl = slice(GW * g, GW * (g + 1))
            gs = slice(D_STATE * g, D_STATE * (g + 1))
            hsl = slice(HEADS_PER_GROUP * g, HEADS_PER_GROUP * (g + 1))
            heads = [HEADS_PER_GROUP * g + hh for hh in range(HEADS_PER_GROUP)]
            Eg, ETg = e_ref[g], et_ref[g]
            B = b_ref[:, gs].astype(BF16)
            C = c_ref[:, gs].astype(BF16)
            cb = _dot(C, B, NT)
            x_g, dy_g = xs_ref[:, gl], dy_ref[:, gl]
            dt_x, w_x = _dotx(dt, Eg), _dotx(wmat, Eg)
            xd_g = x_g * dt_x
            dye = dy_g * _dotx(eacs, Eg)
            hcat = st_ref[hsl].reshape(GW, D_STATE)
            dSv = dh_scr[hsl]
            dScat = dSv.reshape(GW, D_STATE)
            dDm = dDm + _dotx(dy_g * x_g, ETg)
            dH_y = _dot(dye, C, TN)
            dC_g = _dot(dye, hcat, NN)
            ds_col = ds_col + _dotx(dye * _dot(C, hcat, NT), ETg)
            dxdw = _dot(B, dScat, NT)
            dB_g = _dot(xd_g * w_x, dScat, NN)
            dww = _dotx(xd_g * dxdw, ETg) * wmat
            ds_col = ds_col - dww + jnp.where(sub == BLK - 1, jnp.sum(dww, axis=0, keepdims=True), 0.0)
            hd = jnp.sum(_dotx(Eg, dScat * hcat), axis=1, keepdims=True) * gcol
            ds_row = ds_row - jnp.where(lane == BLK - 1, hd, 0.0)
            decays = [_decay(acs, acsT, h, tril) for h in heads]
            Ms = [cb * d for d in decays]
            dMs = [_dot(dy_g[:, hl[hh]], xd_g[:, hl[hh]], NT) for hh in range(HEADS_PER_GROUP)]
            dxd1 = [_dot(Ms[hh], dy_g[:, hl[hh]], TN) for hh in range(HEADS_PER_GROUP)]
            dG = jnp.zeros((BLK, BLK), F32)
            for hh, h in enumerate(heads):
                Q = dMs[hh] * Ms[hh]
                ds_col = ds_col + jnp.where(lane == h, jnp.sum(Q, axis=1, keepdims=True), 0.0)
                ds_row = ds_row + jnp.where(sub == h, jnp.sum(Q, axis=0, keepdims=True), 0.0)
                dG = dG + dMs[hh] * decays[hh]
            dxd_g = jnp.concatenate(dxd1, axis=1) + w_x * dxdw
            dxs_ref[:, gl] = d_ref[:, gl] * dy_g + dxd_g * dt_x
            ddt_col = ddt_col + _dotx(dxd_g * x_g, ETg)
            dC_ref[:, gs] = dC_g + _dot(dG, B, NN)
            dB_ref[:, gs] = dB_g + _dot(dG, C, TN)
            for hh, h in enumerate(heads):
                dh_scr[h] = gam[:, h:h + 1] * dSv[hh] + dH_y[hl[hh]]
        ds_all = ds_col - ds_row.T
        da = jnp.dot(_tri(False).astype(F32), ds_all, precision=lax.Precision.HIGHEST, preferred_element_type=F32)
        ddt = ddt_col + da * aneg
        draw = jnp.where(lane < SSM_HEADS, ddt * _sigmoid(raw), 0.0)
        ddt_ref[...] = draw.astype(BF16)
        dal_ref[...] += jnp.sum(da * dt, axis=0, keepdims=True) * aneg
        ddtb_ref[...] += jnp.sum(draw, axis=0, keepdims=True)
        dD_ref[...] += jnp.sum(dDm, axis=0, keepdims=True)

    gblk = pl.BlockSpec((BLK, gn), lambda c: (ci(c), 0))
    espec = pl.BlockSpec((SSM_GROUPS, 128, GW), lambda c: (0, 0, 0))
    etspec = pl.BlockSpec((SSM_GROUPS, GW, 128), lambda c: (0, 0, 0))
    return _pcall(body, name="ssd_bwd", grid=(nc,), in_specs=[xs, bm, cm, dts, v128, v128, dfs, st, xs, espec, etspec, _ANY],
                  out_specs=[xs, gblk, gblk, pl.BlockSpec((BLK, DT_PAD), lambda c: (ci(c), OFF_DT // DT_PAD)), v128, v128, v128],
                  out_shape=[_sds((T, D_INNER), F32), _sds((T, gn), F32), _sds((T, gn), F32), _sds(dproj.shape, dproj.dtype),
                             _sds((1, 128), F32), _sds((1, 128), F32), _sds((1, 128), F32)],
                  scratch_shapes=[pltpu.VMEM((SSM_HEADS, HEAD_DIM, D_STATE), F32)], input_output_aliases={11: 3},
                  compiler_params=_cparams(("arbitrary",)))(xbc, xbc, xbc, projp, dtb, alog, dfull, states, dy, E, ET, dproj)


_WIN_ORDER = ("ga", "gs", "z", "xbc", "q", "k", "v", "dt")


PERM_TILE = 512


def _win_row_moves():
    off = dict(z=OFF_Z, ga=OFF_GA, gs=OFF_GS, xbc=OFF_XBC, q=OFF_Q, k=OFF_K, v=OFF_V, dt=OFF_DT)
    per = IN_DIM // N_DEV
    tiles = [[] for _ in range(-(-IN_PAD // PERM_TILE))]
    for nm in _WIN_ORDER:
        s, w = SEG[nm]
        d = off[nm]
        while w > 0:
            j, r = divmod(s, per)
            n = min(w, per - r, PERM_TILE - d % PERM_TILE)
            tiles[d // PERM_TILE].append((j, r, n, d % PERM_TILE))
            s, w, d = s + n, w - n, d + n
    return tiles


def _win_to_padded(win_g):
    per = IN_DIM // N_DEV
    moves = _win_row_moves()

    def body(w_ref, o_ref, slots, stage, in_sems, out_sems):
        loads = [pltpu.make_async_copy(w_ref.at[j], slots.at[j], in_sems.at[j]) for j in range(N_DEV)]
        for cp in loads:
            cp.start()
        arrived = [False] * N_DEV
        stores = [None, None]
        for t, pieces in enumerate(moves):
            rows = min(PERM_TILE, IN_PAD - PERM_TILE * t)
            b = t % 2
            if stores[b] is not None:
                stores[b].wait()
            filled = 0
            for j, r, n, d in pieces:
                if not arrived[j]:
                    loads[j].wait()
                    arrived[j] = True
                stage[b, pl.ds(d, n), :] = slots[j, pl.ds(r, n), :]
                filled = max(filled, d + n)
            if filled < rows:
                stage[b, pl.ds(filled, rows - filled), :] = jnp.zeros((rows - filled, D_MODEL), stage.dtype)
            stores[b] = pltpu.make_async_copy(stage.at[b, pl.ds(0, rows), :], o_ref.at[pl.ds(PERM_TILE * t, rows), :], out_sems.at[b])
            stores[b].start()
        for cp in stores:
            cp.wait()

    return _pcall(body, name="w_in_rows", in_specs=[_ANY], out_specs=_ANY, out_shape=_sds((IN_PAD, D_MODEL), win_g.dtype),
                  scratch_shapes=[pltpu.VMEM((N_DEV, per, D_MODEL), win_g.dtype), pltpu.VMEM((2, PERM_TILE, D_MODEL), win_g.dtype),
                                  pltpu.SemaphoreType.DMA((N_DEV,)), pltpu.SemaphoreType.DMA((2,))],
                  compiler_params=pltpu.CompilerParams(vmem_limit_bytes=VMEM_LIMIT_ROWS))(win_g)


def _padded_to_win(dw):
    per = IN_DIM // N_DEV
    moves = _win_row_moves()

    def body(d_ref, o_ref, slots, stage, in_sems, out_sems):
        def load(t):
            rows = min(PERM_TILE, IN_PAD - PERM_TILE * t)
            return pltpu.make_async_copy(d_ref.at[pl.ds(PERM_TILE * t, rows), :], stage.at[t % 2, pl.ds(0, rows), :], in_sems.at[t % 2])

        pending = load(0)
        pending.start()
        for t, pieces in enumerate(moves):
            pending.wait()
            if t + 1 < len(moves):
                pending = load(t + 1)
                pending.start()
            for j, r, n, d in pieces:
                slots[j, pl.ds(r, n), :] = stage[t % 2, pl.ds(d, n), :]
        stores = [pltpu.make_async_copy(slots.at[j], o_ref.at[j], out_sems.at[j]) for j in range(N_DEV)]
        for cp in stores:
            cp.start()
        for cp in stores:
            cp.wait()

    return _pcall(body, name="d_w_in_rows", in_specs=[_ANY], out_specs=_ANY, out_shape=_sds((N_DEV, per, D_MODEL), dw.dtype),
                  scratch_shapes=[pltpu.VMEM((N_DEV, per, D_MODEL), dw.dtype), pltpu.VMEM((2, PERM_TILE, D_MODEL), dw.dtype),
                                  pltpu.SemaphoreType.DMA((2,)), pltpu.SemaphoreType.DMA((N_DEV,))],
                  compiler_params=pltpu.CompilerParams(vmem_limit_bytes=VMEM_LIMIT_ROWS))(dw)


def _pad128(v):
    return jnp.pad(v, ((0, 0), (0, 128 - v.shape[1])))


_SMALL = (("loss", 128, 1), ("g_mix", 2048, 2048), ("conv_b", 3072, 3072), ("dt_bias", 128, 32), ("a_log", 128, 32),
          ("d_skip", 128, 32), ("g_ssd", 2048, 2048), ("sinks", 128, 16), ("g_ffn", 2048, 2048), ("g_ple", 2048, 2048),
          ("g_final", 2048, 2048))


def _small_vec(d):
    parts = []
    for nm, pw, w in _SMALL:
        v = d[nm].reshape(1, -1).astype(F32)
        parts.append(jnp.pad(v[:, :min(v.shape[1], pw)], ((0, 0), (0, pw - min(v.shape[1], pw)))))
    return jnp.concatenate(parts, axis=1)


def _small_split(vec):
    out, o = {}, 0
    for nm, pw, w in _SMALL:
        out[nm] = vec[0, o:o + w]
        o += pw
    return out


def kernel(x, p, positions, g_mix, w_in, conv_w, conv_b, dt_bias, a_log, d_skip, g_ssd, sinks, w_attn_br, w_ssd_br, w_o, g_ffn, w_gate, w_up, w_down, g_ple, w_ple_gate, w_ple_proj, g_final, loss_target, m_g_mix, m_w_in, m_conv_w, m_conv_b, m_dt_bias, m_a_log, m_d_skip, m_g_ssd, m_sinks, m_w_attn_br, m_w_ssd_br, m_w_o, m_g_ffn, m_w_gate, m_w_up, m_w_down, m_g_ple, m_w_ple_gate, m_w_ple_proj, m_g_final, v_g_mix, v_w_in, v_conv_w, v_conv_b, v_dt_bias, v_a_log, v_d_skip, v_g_ssd, v_sinks, v_w_attn_br, v_w_ssd_br, v_w_o, v_g_ffn, v_w_gate, v_w_up, v_w_down, v_g_ple, v_w_ple_gate, v_w_ple_proj, v_g_final):
    T = x.shape[1]
    D = D_MODEL
    W = dict(g_mix=g_mix, w_in=w_in, conv_w=conv_w, conv_b=conv_b, dt_bias=dt_bias, a_log=a_log, d_skip=d_skip, g_ssd=g_ssd,
             sinks=sinks, w_attn_br=w_attn_br, w_ssd_br=w_ssd_br, w_o=w_o, g_ffn=g_ffn, w_gate=w_gate, w_up=w_up, w_down=w_down,
             g_ple=g_ple, w_ple_gate=w_ple_gate, w_ple_proj=w_ple_proj, g_final=g_final)
    Mo = dict(g_mix=m_g_mix, w_in=m_w_in, conv_w=m_conv_w, conv_b=m_conv_b, dt_bias=m_dt_bias, a_log=m_a_log, d_skip=m_d_skip,
              g_ssd=m_g_ssd, sinks=m_sinks, w_attn_br=m_w_attn_br, w_ssd_br=m_w_ssd_br, w_o=m_w_o, g_ffn=m_g_ffn, w_gate=m_w_gate,
              w_up=m_w_up, w_down=m_w_down, g_ple=m_g_ple, w_ple_gate=m_w_ple_gate, w_ple_proj=m_w_ple_proj, g_final=m_g_final)
    Vo = dict(g_mix=v_g_mix, w_in=v_w_in, conv_w=v_conv_w, conv_b=v_conv_b, dt_bias=v_dt_bias, a_log=v_a_log, d_skip=v_d_skip,
              g_ssd=v_g_ssd, sinks=v_sinks, w_attn_br=v_w_attn_br, w_ssd_br=v_w_ssd_br, w_o=v_w_o, g_ffn=v_g_ffn, w_gate=v_w_gate,
              w_up=v_w_up, w_down=v_w_down, g_ple=v_g_ple, w_ple_gate=v_w_ple_gate, w_ple_proj=v_w_ple_proj, g_final=v_g_final)
    order = ["g_mix", "w_in", "conv_w", "conv_b", "dt_bias", "a_log", "d_skip", "g_ssd", "sinks", "w_attn_br", "w_ssd_br", "w_o",
             "g_ffn", "w_gate", "w_up", "w_down", "g_ple", "w_ple_gate", "w_ple_proj", "g_final"]
    big = ["w_in", "conv_w", "w_attn_br", "w_ssd_br", "w_o", "w_gate", "w_up", "w_down", "w_ple_gate", "w_ple_proj"]

    x2 = x.reshape(T, D)
    p2 = p.reshape(T, PLE_DIM)
    tgt = loss_target.reshape(T, D)
    posf = positions.reshape(T, 1).astype(F32)
    inv = ROPE_THETA ** (-np.arange(HEAD_DIM // 2, dtype=np.float32) * 2.0 / HEAD_DIM)
    inv128 = jnp.asarray(np.tile(inv, 128 // (HEAD_DIM // 2)).reshape(1, 128).astype(np.float32))
    transposed = ("w_in", "w_gate", "w_up")

    def shard2d(a, n):
        a = a.reshape(a.shape[-2:])
        return a.T if n in transposed else a

    sh = {n: shard2d(W[n], n) for n in big}

    del _PENDING[:]
    me = 4 * lax.axis_index("x") + 2 * lax.axis_index("y") + lax.axis_index("c")
    groups = (("w_in",), ("conv_w", "w_attn_br", "w_ssd_br", "w_o"), ("w_gate", "w_up"), ("w_down",), ("w_ple_gate", "w_ple_proj"))
    send = {n: sh[n] if n == "conv_w" else sh[n].astype(BF16) for n in big}
    zones = [[lax.dynamic_update_index_in_dim(lax.empty((N_DEV,) + send[n].shape, send[n].dtype), send[n], me, 0) for n in grp]
             for grp in groups]
    ring_a = split_start("ring_a_start", "route", ((4, 0, 0), (2, 0, 1)), [], lands=zones[0])
    ring_b = split_start("ring_b_start", "route", ((4, 0, 1), (2, 0, 0)), [], lands=ring_a["lands"], after=ring_a["token"])
    gathered, fwd = {}, {}

    def forward_start(gi, after, lands=None):
        if lands is None:
            _, lands = split_wait("gather_wait_%d" % gi, started[gi], after)
        fwd[gi] = split_start("forward_start_%d" % gi, "forward", FORWARD_BLOCKS, [], lands=lands)

    def forward_wait(gi, after):
        _, full = split_wait("forward_wait_%d" % gi, fwd[gi], after)
        gathered.update(zip(groups[gi], full))

    u = rms_fwd("norm_mix", x2, g_mix)
    _, zone = split_wait("ring_a_wait", dict(ring_a, lands=ring_b["lands"]), [u] + [z for grp in zones[1:] for z in grp])
    ring_c = split_start("ring_c_start", "route", ((2, 4, 0), (4, 2, 1)), [], lands=zone)
    started, prev = [None], ring_c["token"]
    for gi in range(1, len(groups)):
        started.append(split_start("gather_start_%d" % gi, "gather", ICI_SAME_CORE, [], lands=zones[gi], after=prev))
        prev = started[-1]["token"]
    _, zone = split_wait("ring_b_wait", dict(ring_b, lands=ring_c["lands"]), u)
    _, zone = split_wait("ring_c_wait", dict(ring_c, lands=zone), u)
    forward_start(0, u, lands=zone)
    forward_wait(0, u)
    winp = _win_to_padded(gathered["w_in"])
    dtb = _pad128(dt_bias)
    alog = _pad128(a_log)
    dfull = jnp.repeat(d_skip.reshape(SSM_HEADS), HEAD_DIM).reshape(1, D_INNER)

    projp = mm_nt("in_proj", u, winp, 640)
    forward_start(1, projp)
    attn, qr, kr = attn_fwd(projp, posf, inv128, sinks)
    forward_wait(1, attn)
    convw = jnp.transpose(gathered["conv_w"], (1, 0, 2)).reshape(CONV_WIDTH, CONV_DIM)
    wab = gathered["w_attn_br"]
    wsb = gathered["w_ssd_br"].reshape(D, D)
    wo = gathered["w_o"].reshape(D, D)
    xbc = conv_fwd(projp, convw, conv_b)
    y, states = ssd_fwd(xbc, projp, dtb, alog, dfull)
    yn = gnorm_fwd(y, projp, g_ssd)
    out_a = mm_nn_colblk("attn_br", attn, wab)
    out_s = mm_nn("ssd_br", yn, wsb, 512)
    forward_start(2, out_s)
    merged = merge_fwd(projp, out_a, out_s)
    h1 = mm_nn("o_proj", merged, wo, 512, residual=x2)
    f = rms_fwd("norm_ffn", h1, g_ffn)
    forward_wait(2, f)
    forward_start(3, f)
    wgt, wut = (gathered[n].reshape(FFN_HIDDEN, D) for n in ("w_gate", "w_up"))
    gate, up, act = ffn_up(f, wgt, wut)
    forward_wait(3, act)
    forward_start(4, act)
    wd = gathered["w_down"].reshape(FFN_HIDDEN, D)
    h2 = mm_nn_red("ffn_down", act, wd, 512, FFN_HIDDEN, residual=h1)
    r = rms_fwd("norm_ple", h2, g_ple)
    forward_wait(4, r)
    wpg = gathered["w_ple_gate"].reshape(D, D)
    wpp = gathered["w_ple_proj"]
    pg = mm_nn("ple_gate", r, wpg, 512)
    pp = mm_nn_colblk("ple_proj", p2, wpp)
    loss_v, dh3, dpg, dpp, dg_final = head_fwd_bwd(h2, pg, pp, g_final.reshape(1, D), tgt)

    gw = {}
    scat = []

    def scatter_start(names):
        scat.append((names, split_start("scatter_start_%d" % len(scat), "scatter", ALL_PEERS, [gw[n] for n in names])))

    gw["w_ple_proj"] = mm_tn_colblk("dw_ple_proj", p2, dpp, PLE_DIM)
    dr = mm_nt("d_ple_gate", dpg, wpg, 512)
    gw["w_ple_gate"] = mm_tn("dw_ple_gate", r, dpg, 512, D).reshape(N_DEV, D // N_DEV, D)
    scatter_start(("w_ple_proj", "w_ple_gate"))
    dh2, dh2b, dg_ple = rms_bwd("norm_ple_bwd", h2, g_ple, dr, dh3)
    dgate, dup = ffn_down_bwd(dh2b, wd, gate, up)
    per = FFN_HIDDEN // N_DEV
    gw["w_down"] = mm_tn("dw_down", act, dh2b, FFN_TILE, D).reshape(N_DEV, per, D)
    gw["w_gate"] = mm_tn("dw_gate", dgate, f, FFN_TILE, D).reshape(N_DEV, per, D)
    gw["w_up"] = mm_tn("dw_up", dup, f, FFN_TILE, D).reshape(N_DEV, per, D)
    scatter_start(("w_down", "w_gate", "w_up"))
    df = ffn_up_bwd(dgate, dup, wgt, wut)
    dh1, dh1b, dg_ffn = rms_bwd("norm_ffn_bwd", h1, g_ffn, df, dh2)
    dmerged = mm_nt("d_o_proj", dh1b, wo, 512)
    gw["w_o"] = mm_tn("dw_o", merged, dh1b, 512, D).reshape(N_DEV, D // N_DEV, D)
    dout_a, dout_s, dproj = merge_bwd(projp, out_a, out_s, dmerged)
    gw["w_ssd_br"] = mm_tn("dw_ssd_br", yn, dout_s, 512, D).reshape(N_DEV, D // N_DEV, D)
    gw["w_attn_br"] = mm_tn_colblk("dw_attn_br", attn, dout_a, D // N_DEV)
    scatter_start(("w_o", "w_ssd_br", "w_attn_br"))
    dyn = mm_nt("d_ssd_br", dout_s, wsb, 512)
    dattn = attn_br_bwd(dout_a, wab)
    dy, dproj, dg_ssd = gnorm_bwd(y, projp, g_ssd, dyn, dproj)
    dxs, dbm, dcm, dproj, dal, ddsk, ddtb = ssd_bwd(xbc, projp, dtb, alog, dfull, states, dy, dproj)
    dproj, dwc_x, dbc_x = conv_bwd("conv_bwd_x", projp, dxs, convw, conv_b, 0, dproj)
    dproj, dwc_b, dbc_b = conv_bwd("conv_bwd_b", projp, dbm, convw, conv_b, D_INNER, dproj)
    dproj, dwc_c, dbc_c = conv_bwd("conv_bwd_c", projp, dcm, convw, conv_b, D_INNER + SSM_GROUPS * D_STATE, dproj)
    dproj, dk, dv, dsk = attn_bwd(qr, kr, projp, dattn, posf, inv128, sinks, dproj)
    dproj = lax.dynamic_update_slice(dproj, jnp.concatenate([dk, dv], axis=1).astype(BF16), (0, OFF_K))
    gw_in = _padded_to_win(mm_tn("dw_in", dproj, u, 640, D))
    pair = split_start("pair_start", "pair", FORWARD_BLOCKS, [gw_in])
    dconvw = jnp.concatenate([dwc_x, dwc_b, dwc_c], axis=1)
    gw["conv_w"] = jnp.transpose(dconvw.reshape(CONV_WIDTH, N_DEV, CONV_DIM // N_DEV), (1, 0, 2))
    scatter_start(("conv_w",))
    du_tm = min(512, T)
    tiles = T // du_tm
    first = max(tiles // 4, 1)
    du = mm_nn_red("d_in_proj_a", dproj, winp, 512, IN_PAD, rows=(0, first), tm=du_tm)
    (gw_in,), (sibling_part,) = split_wait("pair_wait", pair, du)
    pair_slots = jnp.stack([jnp.bitwise_xor(me, k) for k in FORWARD_BLOCKS]).astype(jnp.int32)
    core = split_start("core_start", "scatter_core", ICI_SAME_CORE, [pair_sum("pair_sum_w_in", gw_in, pair_slots, sibling_part)])
    if first < tiles:
        du = mm_nn_red("d_in_proj_b", dproj, winp, 512, IN_PAD, rows=(first, tiles - first), prev=du, tm=du_tm)
    gx, _, dg_mix = rms_bwd("norm_mix_bwd", x2, g_mix, du, dh1)

    small_g = dict(loss=loss_v[:, :1], g_mix=dg_mix, conv_b=jnp.concatenate([dbc_x, dbc_b, dbc_c], axis=1), dt_bias=ddtb,
                   a_log=dal, d_skip=ddsk, g_ssd=dg_ssd, sinks=dsk, g_ffn=dg_ffn, g_ple=dg_ple, g_final=dg_final)
    vec = _small_vec(small_g)
    small = split_start("small_start", "gather", ALL_PEERS, [],
                        lands=[lax.dynamic_update_index_in_dim(lax.empty((N_DEV,) + vec.shape, F32), vec, me, 0)])

    res = {}
    after = [gx]
    for si, (names, h) in enumerate(scat):
        srcs, lands = split_wait("scatter_wait_%d" % si, h, after)
        for n, mine, arrived in zip(names, srcs, lands):
            res[n] = adamw("adamw_" + n, arrived, sh[n], shard2d(Mo[n], n), shard2d(Vo[n], n), own=mine, own_slot=me)
        after = [res[n][0] for n in names]
    zero = jnp.zeros((1, 1), F32)
    _, (vec_parts,) = split_wait("small_wait", small, [res[n][0] for n in res])
    sres = adamw("adamw_small", vec_parts, _small_vec({**W, "loss": zero}), _small_vec({**Mo, "loss": zero}),
                 _small_vec({**Vo, "loss": zero}))
    ssplit = [_small_split(a) for a in sres]

    (pair_sums,), (arrived,) = split_wait("core_wait", core, [sres[0]])
    res["w_in"] = adamw("adamw_w_in", arrived, sh["w_in"], shard2d(Mo["w_in"], "w_in"), shard2d(Vo["w_in"], "w_in"),
                        own=pair_sums, own_slot=0)
    loss = ssplit[0]["loss"].reshape(())
    for n in order:
        if n not in res:
            res[n] = tuple(s[n].reshape(W[n].shape) for s in ssplit)
        else:
            res[n] = tuple((a.T if n in transposed else a).reshape(W[n].shape) for a in res[n])
    outs = [loss, gx.reshape(x.shape)]
    for k in range(4):
        outs += [res[n][k] for n in order]
    return tuple(outs)
```

```python
import numpy as np
import jax
import jax.numpy as jnp
from jax import lax
from jax.experimental import pallas as pl
from jax.experimental.pallas import tpu as pltpu

F32 = jnp.float32
BF16 = jnp.bfloat16

N_DEV = 8
D_MODEL = 2048
HEAD_DIM = 64
ATTN_HEADS = 16
KV_HEADS = 4
Q_DIM = 1024
KV_DIM = 256
BLK = 128
D_INNER = 2048
SSM_HEADS = 32
SSM_GROUPS = 4
HEADS_PER_GROUP = 8
D_STATE = 128
CONV_WIDTH = 4
CONV_DIM = 3072
FFN_HIDDEN = 5632
PLE_DIM = 256
IN_DIM = 10784
NORM_EPS = 1e-6
SSM_NORM_EPS = 1e-5
ROPE_THETA = 10000.0

OFF_GA, OFF_GS, OFF_Z, OFF_XBC, OFF_Q, OFF_K, OFF_V, OFF_DT = 0, 2048, 4096, 6144, 9216, 10240, 10496, 10752
IN_PAD = 10880
DT_PAD = 128
SEG = dict(q=(0, 1024), k=(1024, 256), v=(1280, 256), z=(1536, 2048), xbc=(3584, 3072), dt=(6656, 32),
           ga=(6688, 2048), gs=(8736, 2048))

ADAM_LR, ADAM_B1, ADAM_B2, ADAM_EPS, ADAM_WD, ADAM_STEP = 0.001, 0.9, 0.999, 1e-08, 0.01, 10

VMEM_LIMIT = 56 * 1024 * 1024
VMEM_LIMIT_ROWS = 60 * 1024 * 1024

NN = (((1,), (0,)), ((), ()))
NT = (((1,), (1,)), ((), ()))
TN = (((0,), (0,)), ((), ()))


_PENDING = []


def _raw_call(body, **kw):
    return pl.pallas_call(body, **kw)


def _pcall(body, **kw):
    if "in_specs" not in kw:
        return _raw_call(body, **kw)
    deps = list(_PENDING)
    del _PENDING[:]
    if not deps:
        return _raw_call(body, **kw)
    n_in = len(kw["in_specs"])

    def tied(*refs):
        return body(*refs[:n_in], *refs[n_in + len(deps):])

    kw["in_specs"] = list(kw["in_specs"]) + [pl.BlockSpec(memory_space=pl.ANY)] * len(deps)
    call = _raw_call(tied, **kw)
    return lambda *ops: call(*ops, *deps)


def _cparams(sem=None):
    if sem is None:
        return pltpu.CompilerParams(vmem_limit_bytes=VMEM_LIMIT)
    return pltpu.CompilerParams(vmem_limit_bytes=VMEM_LIMIT, dimension_semantics=sem)


def _dot(a, b, dn):
    return lax.dot_general(a.astype(BF16), b.astype(BF16), dn, preferred_element_type=F32)


def _sigmoid(x):
    return 1.0 / (1.0 + jnp.exp(-x))


def _silu(x):
    return x * _sigmoid(x)


def _dsilu(x):
    s = _sigmoid(x)
    return s * (1.0 + x * (1.0 - s))


def _matmul(name, pairs, pair_specs, dn, grid, out_shapes, out_specs, nred=1, extra=(), extra_specs=(),
            epilogue=None, acc_shape=None, alias=None):
    n_in = 2 * len(pairs) + len(extra)
    n_out = len(out_shapes)

    def body(*refs):
        ins = refs[:2 * len(pairs)]
        ex = [r for r, sp in zip(refs[2 * len(pairs):n_in], extra_specs) if sp.memory_space != pl.ANY]
        outs = refs[n_in:n_in + n_out]

        def prod():
            s = None
            for p in range(len(pairs)):
                d = _dot(ins[2 * p][...], ins[2 * p + 1][...], dn)
                s = d if s is None else s + d
            return s

        def finish(val):
            if epilogue is None:
                outs[0][...] = val.astype(outs[0].dtype)
            else:
                res = epilogue(val, *[e[...] for e in ex])
                for o, r in zip(outs, res):
                    o[...] = r.astype(o.dtype)

        if nred == 1:
            finish(prod())
        else:
            acc = refs[n_in + n_out]
            k = pl.program_id(len(grid) - 1)

            @pl.when(k == 0)
            def _():
                acc[...] = jnp.zeros_like(acc)

            acc[...] += prod()

            @pl.when(k == nred - 1)
            def _():
                finish(acc[...])

    operands = []
    specs = []
    for (a, b), (sa, sb) in zip(pairs, pair_specs):
        operands += [a, b]
        specs += [sa, sb]
    operands += list(extra)
    specs += list(extra_specs)
    scratch = [pltpu.VMEM(acc_shape, F32)] if nred > 1 else []
    sem = ("arbitrary",) * len(grid)
    res = _pcall(body, name=name, grid=grid, in_specs=specs, out_specs=list(out_specs), input_output_aliases=dict(alias or {}),
                 out_shape=list(out_shapes), scratch_shapes=scratch, compiler_params=_cparams(sem))(*operands)
    return res


def _sds(shape, dtype):
    return jax.ShapeDtypeStruct(shape, dtype)


def _row_tile(T):
    return min(1024, T)


def mm_nn(name, a, b, tn, out_dtype=F32, residual=None):
    M, K = a.shape
    N = b.shape[1]
    tm = _row_tile(M)
    grid = (M // tm, N // tn)
    extra, especs, epi = (), (), None
    if residual is not None:
        extra = (residual,)
        especs = (pl.BlockSpec((tm, tn), lambda i, n: (i, n)),)
        epi = lambda v, r: (v + r,)
    return _matmul(name, [(a, b)], [(pl.BlockSpec((tm, K), lambda i, n: (i, 0)), pl.BlockSpec((K, tn), lambda i, n: (0, n)))],
                   NN, grid, [_sds((M, N), out_dtype)], [pl.BlockSpec((tm, tn), lambda i, n: (i, n))],
                   extra=extra, extra_specs=especs, epilogue=epi)[0]


def mm_nn_colblk(name, a, b, out_dtype=F32):
    M, K = a.shape
    J, _, nb = b.shape
    tm = _row_tile(M)
    grid = (M // tm, J)
    return _matmul(name, [(a, b)], [(pl.BlockSpec((tm, K), lambda i, j: (i, 0)), pl.BlockSpec((None, K, nb), lambda i, j: (j, 0, 0)))],
                   NN, grid, [_sds((M, J * nb), out_dtype)], [pl.BlockSpec((tm, nb), lambda i, j: (i, j))])[0]


def mm_nt(name, a, w, tr, out_dtype=F32):
    M, C = a.shape
    R = w.shape[0]
    tm = _row_tile(M)
    grid = (M // tm, R // tr)
    return _matmul(name, [(a, w)], [(pl.BlockSpec((tm, C), lambda i, r: (i, 0)), pl.BlockSpec((tr, C), lambda i, r: (r, 0)))],
                   NT, grid, [_sds((M, R), out_dtype)], [pl.BlockSpec((tm, tr), lambda i, r: (i, r))])[0]


def mm_nn_red(name, a, b, tn, tk, out_dtype=F32, residual=None, rows=None, prev=None, tm=None):
    M, K = a.shape
    N = b.shape[1]
    tm = min(tm or _row_tile(M), M)
    nk = K // tk
    i0, ni = (0, M // tm) if rows is None else rows
    grid = (ni, N // tn, nk)
    ospec = pl.BlockSpec((tm, tn), lambda i, n, k: (i + i0, n))
    extra, especs, epi = [], [], None
    if residual is not None:
        extra, especs, epi = [residual], [ospec], (lambda v, r, *_: (v + r,))
    alias = {}
    if prev is not None:
        alias = {2 + len(extra): 0}
        extra, especs = extra + [prev], especs + [_ANY]
        epi = epi or (lambda v, *_: (v,))
    return _matmul(name, [(a, b)], [(pl.BlockSpec((tm, tk), lambda i, n, k: (i + i0, k)), pl.BlockSpec((tk, tn), lambda i, n, k: (k, n)))],
                   NN, grid, [_sds((M, N), out_dtype)], [ospec], nred=nk, acc_shape=(tm, tn),
                   extra=extra, extra_specs=especs, epilogue=epi, alias=alias)[0]


def mm_tn(name, x, dy, tr, tc, out_dtype=BF16):
    M, R = x.shape
    C = dy.shape[1]
    grid = (R // tr, C // tc)
    return _matmul(name, [(x, dy)], [(pl.BlockSpec((M, tr), lambda r, c: (0, r)), pl.BlockSpec((M, tc), lambda r, c: (0, c)))],
                   TN, grid, [_sds((R, C), out_dtype)], [pl.BlockSpec((tr, tc), lambda r, c: (r, c))])[0]


def mm_tn_colblk(name, x, dy, nb, out_dtype=BF16):
    M, R = x.shape
    J = dy.shape[1] // nb
    grid = (J,)
    return _matmul(name, [(x, dy)], [(pl.BlockSpec((M, R), lambda j: (0, 0)), pl.BlockSpec((M, nb), lambda j: (0, j)))],
                   TN, grid, [_sds((J, R, nb), out_dtype)], [pl.BlockSpec((None, R, nb), lambda j: (j, 0, 0))])[0]


def _rows(T):
    return min(256, T)


def rms_fwd(name, x, g, eps=NORM_EPS):
    T, D = x.shape
    tm = _rows(T)

    def body(x_ref, g_ref, o_ref):
        xv = x_ref[...]
        r = lax.rsqrt(jnp.mean(xv * xv, axis=-1, keepdims=True) + eps)
        o_ref[...] = (xv * r * g_ref[...]).astype(BF16)

    return _pcall(body, name=name, grid=(T // tm,),
                  in_specs=[pl.BlockSpec((tm, D), lambda i: (i, 0)), pl.BlockSpec((1, D), lambda i: (0, 0))],
                  out_specs=pl.BlockSpec((tm, D), lambda i: (i, 0)), out_shape=_sds((T, D), BF16),
                  compiler_params=_cparams(("arbitrary",)))(x, g)


def rms_bwd(name, x, g, dy, dres, eps=NORM_EPS):
    T, D = x.shape
    tm = _rows(T)

    def body(x_ref, g_ref, dy_ref, dr_ref, dx_ref, dxb_ref, dg_ref):
        i = pl.program_id(0)
        xv = x_ref[...]
        r = lax.rsqrt(jnp.mean(xv * xv, axis=-1, keepdims=True) + eps)
        xh = xv * r
        dyv = dy_ref[...]
        gd = dyv * g_ref[...]
        dx = r * (gd - xh * jnp.mean(gd * xh, axis=-1, keepdims=True)) + dr_ref[...]
        dx_ref[...] = dx
        dxb_ref[...] = dx.astype(BF16)

        @pl.when(i == 0)
        def _():
            dg_ref[...] = jnp.zeros_like(dg_ref)

        dg_ref[...] += jnp.sum(dyv * xh, axis=0, keepdims=True)

    row = pl.BlockSpec((tm, D), lambda i: (i, 0))
    vec = pl.BlockSpec((1, D), lambda i: (0, 0))
    return _pcall(body, name=name, grid=(T // tm,), in_specs=[row, vec, row, row], out_specs=[row, row, vec],
                  out_shape=[_sds((T, D), F32), _sds((T, D), BF16), _sds((1, D), F32)],
                  compiler_params=_cparams(("arbitrary",)))(x, g, dy, dres)


def gnorm_fwd(y, projp, g):
    T, D = y.shape
    tm = _rows(T)

    def body(y_ref, z_ref, g_ref, o_ref):
        yz = y_ref[...] * _silu(z_ref[...])
        r = lax.rsqrt(jnp.mean(yz * yz, axis=-1, keepdims=True) + SSM_NORM_EPS)
        o_ref[...] = (yz * r * g_ref[...]).astype(BF16)

    row = pl.BlockSpec((tm, D), lambda i: (i, 0))
    return _pcall(body, name="gnorm_fwd", grid=(T // tm,),
                  in_specs=[row, pl.BlockSpec((tm, D), lambda i: (i, OFF_Z // D)), pl.BlockSpec((1, D), lambda i: (0, 0))],
                  out_specs=row, out_shape=_sds((T, D), BF16), compiler_params=_cparams(("arbitrary",)))(y, projp, g)


def gnorm_bwd(y, projp, g, dyn, dproj):
    T, D = y.shape
    tm = _rows(T)

    def body(y_ref, z_ref, g_ref, dyn_ref, _, dy_ref, dz_ref, dg_ref):
        i = pl.program_id(0)
        yv, zv = y_ref[...], z_ref[...]
        sz = _silu(zv)
        yz = yv * sz
        r = lax.rsqrt(jnp.mean(yz * yz, axis=-1, keepdims=True) + SSM_NORM_EPS)
        xh = yz * r
        dv = dyn_ref[...]
        gd = dv * g_ref[...]
        dyz = r * (gd - xh * jnp.mean(gd * xh, axis=-1, keepdims=True))
        dy_ref[...] = dyz * sz
        dz_ref[...] = (dyz * yv * _dsilu(zv)).astype(BF16)

        @pl.when(i == 0)
        def _():
            dg_ref[...] = jnp.zeros_like(dg_ref)

        dg_ref[...] += jnp.sum(dv * xh, axis=0, keepdims=True)

    row = pl.BlockSpec((tm, D), lambda i: (i, 0))
    vec = pl.BlockSpec((1, D), lambda i: (0, 0))
    return _pcall(body, name="gnorm_bwd", grid=(T // tm,),
                  in_specs=[row, pl.BlockSpec((tm, D), lambda i: (i, OFF_Z // D)), vec, row, _ANY],
                  out_specs=[row, pl.BlockSpec((tm, D), lambda i: (i, OFF_Z // D)), vec],
                  out_shape=[_sds((T, D), F32), _sds(dproj.shape, dproj.dtype), _sds((1, D), F32)], input_output_aliases={4: 1},
                  compiler_params=_cparams(("arbitrary",)))(y, projp, g, dyn, dproj)


def merge_fwd(projp, out_a, out_s):
    T, D = out_a.shape
    tm = _rows(T)

    def body(ga_ref, gs_ref, a_ref, s_ref, o_ref):
        o_ref[...] = (_sigmoid(ga_ref[...]) * a_ref[...] + _sigmoid(gs_ref[...]) * s_ref[...]).astype(BF16)

    row = pl.BlockSpec((tm, D), lambda i: (i, 0))
    return _pcall(body, name="merge_fwd", grid=(T // tm,),
                  in_specs=[pl.BlockSpec((tm, D), lambda i: (i, OFF_GA // D)), pl.BlockSpec((tm, D), lambda i: (i, OFF_GS // D)), row, row],
                  out_specs=row, out_shape=_sds((T, D), BF16), compiler_params=_cparams(("arbitrary",)))(projp, projp, out_a, out_s)


def merge_bwd(projp, out_a, out_s, dmerged):
    T, D = out_a.shape
    tm = _rows(T)
    assert OFF_GA == 0 and OFF_GS == D

    def body(ga_ref, gs_ref, a_ref, s_ref, dm_ref, da_ref, ds_ref, dp_ref):
        dm = dm_ref[...]
        sa, ss = _sigmoid(ga_ref[...]), _sigmoid(gs_ref[...])
        da_ref[...] = (dm * sa).astype(BF16)
        ds_ref[...] = (dm * ss).astype(BF16)
        dp_ref[:, :D] = (dm * a_ref[...] * sa * (1.0 - sa)).astype(BF16)
        dp_ref[:, D:] = (dm * s_ref[...] * ss * (1.0 - ss)).astype(BF16)

    row = pl.BlockSpec((tm, D), lambda i: (i, 0))
    return _pcall(body, name="merge_bwd", grid=(T // tm,),
                  in_specs=[pl.BlockSpec((tm, D), lambda i: (i, OFF_GA // D)), pl.BlockSpec((tm, D), lambda i: (i, OFF_GS // D)), row, row, row],
                  out_specs=[row, row, pl.BlockSpec((tm, 2 * D), lambda i: (i, 0))],
                  out_shape=[_sds((T, D), BF16), _sds((T, D), BF16), _sds((T, IN_PAD), BF16)],
                  compiler_params=_cparams(("arbitrary",)))(projp, projp, out_a, out_s, dmerged)


def head_fwd_bwd(h2, pg, pp, g_final, target):
    T, D = h2.shape
    tm = _rows(T)

    def body(h_ref, pg_ref, pp_ref, g_ref, t_ref, loss_ref, dh_ref, dpg_ref, dpp_ref, dg_ref):
        i = pl.program_id(0)
        s = _sigmoid(pg_ref[...])
        ppv = pp_ref[...]
        h3 = h_ref[...] + s * ppv
        r = lax.rsqrt(jnp.mean(h3 * h3, axis=-1, keepdims=True) + NORM_EPS)
        xh = h3 * r
        gv = g_ref[...]
        e = xh * gv - t_ref[...]
        dyo = e * (1.0 / D)
        gd = dyo * gv
        dh = r * (gd - xh * jnp.mean(gd * xh, axis=-1, keepdims=True))
        dh_ref[...] = dh
        dpg_ref[...] = (dh * ppv * s * (1.0 - s)).astype(BF16)
        dpp_ref[...] = (dh * s).astype(BF16)

        @pl.when(i == 0)
        def _():
            dg_ref[...] = jnp.zeros_like(dg_ref)
            loss_ref[...] = jnp.zeros_like(loss_ref)

        dg_ref[...] += jnp.sum(dyo * xh, axis=0, keepdims=True)
        part = 0.5 * jnp.sum(jnp.mean(e * e, axis=-1, keepdims=True), axis=0, keepdims=True)
        loss_ref[...] += jnp.broadcast_to(part, loss_ref.shape)

    row = pl.BlockSpec((tm, D), lambda i: (i, 0))
    vec = pl.BlockSpec((1, D), lambda i: (0, 0))
    return _pcall(body, name="head_fwd_bwd", grid=(T // tm,), in_specs=[row, row, row, vec, row],
                  out_specs=[pl.BlockSpec((1, 128), lambda i: (0, 0)), row, row, row, vec],
                  out_shape=[_sds((1, 128), F32), _sds((T, D), F32), _sds((T, D), BF16), _sds((T, D), BF16), _sds((1, D), F32)],
                  compiler_params=_cparams(("arbitrary",)))(h2, pg, pp, g_final, target)


FFN_TILE = 512


def ffn_up(f, wgt, wut):
    T, D = f.shape
    H = wgt.shape[0]
    tm = _row_tile(T)

    def body(f_ref, wg_ref, wu_ref, g_ref, u_ref, a_ref):
        fv = f_ref[...]
        g = _dot(fv, wg_ref[...], NT)
        u = _dot(fv, wu_ref[...], NT)
        g_ref[...] = g.astype(BF16)
        u_ref[...] = u.astype(BF16)
        a_ref[...] = (_silu(g) * u).astype(BF16)

    wspec = pl.BlockSpec((FFN_TILE, D), lambda i, j: (j, 0))
    ospec = pl.BlockSpec((tm, FFN_TILE), lambda i, j: (i, j))
    return _pcall(body, name="ffn_up", grid=(T // tm, H // FFN_TILE), in_specs=[pl.BlockSpec((tm, D), lambda i, j: (i, 0)), wspec, wspec],
                  out_specs=[ospec] * 3, out_shape=[_sds((T, H), BF16)] * 3,
                  compiler_params=_cparams(("arbitrary", "arbitrary")))(f, wgt, wut)


def ffn_down_bwd(dh2b, wd, gate, up):
    T, D = dh2b.shape
    H = wd.shape[0]
    tm = _row_tile(T)
    ospec = pl.BlockSpec((tm, FFN_TILE), lambda i, j: (i, j))

    def epi(da, g, u):
        g, u = g.astype(F32), u.astype(F32)
        return (da * u * _dsilu(g), da * _silu(g))

    return _matmul("ffn_down_bwd", [(dh2b, wd)],
                   [(pl.BlockSpec((tm, D), lambda i, j: (i, 0)), pl.BlockSpec((FFN_TILE, D), lambda i, j: (j, 0)))],
                   NT, (T // tm, H // FFN_TILE), [_sds((T, H), BF16)] * 2, [ospec, ospec],
                   extra=(gate, up), extra_specs=(ospec, ospec), epilogue=epi)


def ffn_up_bwd(dgate, dup, wgt, wut):
    T, H = dgate.shape
    D = wgt.shape[1]
    tm = min(512, T)
    tn = 512
    aspec = pl.BlockSpec((tm, H), lambda i, n: (i, 0))
    wspec = pl.BlockSpec((H, tn), lambda i, n: (0, n))
    return _matmul("ffn_up_bwd", [(dgate, wgt), (dup, wut)], [(aspec, wspec), (aspec, wspec)], NN, (T // tm, D // tn),
                   [_sds((T, D), F32)], [pl.BlockSpec((tm, tn), lambda i, n: (i, n))])[0]


def attn_br_bwd(dout_a, wab):
    T, D = dout_a.shape
    J, R, nb = wab.shape
    tm = _row_tile(T)
    return _matmul("attn_br_bwd", [(dout_a, wab)],
                   [(pl.BlockSpec((tm, nb), lambda i, j: (i, j)), pl.BlockSpec((None, R, nb), lambda i, j: (j, 0, 0)))],
                   NT, (T // tm, J), [_sds((T, R), BF16)], [pl.BlockSpec((tm, R), lambda i, j: (i, 0))], nred=J, acc_shape=(tm, R))[0]


def _adam_math(w, g, m, v):
    m2 = ADAM_B1 * m + (1.0 - ADAM_B1) * g
    v2 = ADAM_B2 * v + (1.0 - ADAM_B2) * (g * g)
    m_hat = m2 / (1.0 - ADAM_B1 ** ADAM_STEP)
    v_hat = v2 / (1.0 - ADAM_B2 ** ADAM_STEP)
    delta = -ADAM_LR * (m_hat / (jnp.sqrt(v_hat) + ADAM_EPS) + ADAM_WD * w)
    return delta, m2, v2


def _sum_partials(own, parts):
    g = None if own is None else own.astype(F32)
    if parts is not None:
        for s in range(parts.shape[0]):
            t = parts[s].astype(F32)
            g = t if g is None else g + t
    return g


def adamw(name, parts, w, m, v, own=None, own_slot=None):
    R, C = w.shape
    tr, tc = R, C
    for cand in (256, 176, 128, 64, 32, 16, 8):
        if R % cand == 0 and R > cand:
            tr = cand
            break
    if tr == R and R > 256:
        tc = 256
    given = [a for a in (parts, own) if a is not None]
    pre = own_slot is not None

    def body(*refs):
        refs = refs[1:] if pre else refs
        p_ref = refs[0] if parts is not None else None
        o_ref = refs[len(given) - 1] if own is not None else None
        w_ref, m_ref, v_ref, g_ref, d_ref, m2_ref, v2_ref = refs[-7:]
        g = _sum_partials(None if o_ref is None else o_ref[...], p_ref)
        d, m2, v2 = _adam_math(w_ref[...], g, m_ref[...], v_ref[...])
        g_ref[...] = g
        d_ref[...] = d
        m2_ref[...] = m2
        v2_ref[...] = v2

    blk = pl.BlockSpec((tr, tc), lambda i, j, *s: (i, j))
    specs = [] if parts is None else [pl.BlockSpec((parts.shape[0], tr, tc), lambda i, j, *s: (0, i, j))]
    if own is not None:
        specs.append(pl.BlockSpec((None, tr, tc), lambda i, j, s: (s[0], i, j)) if pre else blk)
    specs += [blk] * 3
    grid = (R // tr, C // tc)
    out_shape = [_sds((R, C), F32)] * 4
    params = _cparams(("arbitrary", "arbitrary"))
    if not pre:
        return _pcall(body, name=name, grid=grid, in_specs=specs, out_specs=[blk] * 4, out_shape=out_shape,
                      compiler_params=params)(*given, w, m, v)
    spec = pltpu.PrefetchScalarGridSpec(num_scalar_prefetch=1, grid=grid, in_specs=specs, out_specs=[blk] * 4)
    return _pcall(body, name=name, grid_spec=spec, out_shape=out_shape,
                  compiler_params=params)(jnp.asarray(own_slot, jnp.int32).reshape(1), *given, w, m, v)


_HBM = pl.BlockSpec(memory_space=pltpu.HBM)
_SEM = pl.BlockSpec(memory_space=pltpu.SEMAPHORE)
_ANY = pl.BlockSpec(memory_space=pl.ANY)
_SPLIT_PARAMS = dict(compiler_params=pltpu.CompilerParams(has_side_effects=pltpu.SideEffectType.DATAFLOW_SIDE_EFFECTING))
ICI_SAME_CORE = (2, 4, 6)
ALL_PEERS = (1, 2, 3, 4, 5, 6, 7)
LAND_SLOTS = {"gather": N_DEV, "scatter": N_DEV - 1, "pair": 4, "scatter_core": 3}


def _mesh_pos():
    x, y, c = lax.axis_index("x"), lax.axis_index("y"), lax.axis_index("c")
    return x, y, c, 4 * x + 2 * y + c


def _peer_of(k, x, y, c):
    px = 1 - x if k & 4 else x
    py = 1 - y if k & 2 else y
    pc = 1 - c if k & 1 else c
    return (px, py, pc), 4 * px + 2 * py + pc


def _split_copies(mode, ks, srcs, lands, send_sems, recv_sems):
    x, y, c, me = _mesh_pos()
    pairs = []
    for a in range(len(lands)):
        for j, k in enumerate(ks):
            dev, peer = _peer_of(k if mode != "route" else k[0], x, y, c)
            i = a * len(ks) + j
            if mode == "gather":
                s_out, d_out, d_in = lands[a].at[me], lands[a].at[me], lands[a].at[peer]
            elif mode == "scatter":
                s_out, d_out, d_in = srcs[a].at[peer], lands[a].at[k - 1], lands[a].at[k - 1]
            elif mode == "pair":
                dev, _ = _peer_of(1, x, y, c)
                _, theirs = _peer_of(k | 1, x, y, c)
                s_out, d_out, d_in = srcs[a].at[theirs], lands[a].at[j], lands[a].at[j]
            elif mode == "scatter_core":
                s_out, d_out, d_in = srcs[a].at[j + 1], lands[a].at[j], lands[a].at[j]
            elif mode == "route":
                k, rel, half = k
                dev, _ = _peer_of(k, x, y, c)
                _, held = _peer_of(rel, x, y, c)
                _, theirs = _peer_of(k ^ rel, x, y, c)
                cols = lands[a].shape[-1] // 2
                cut = (slice(None), slice(None)) if half is None else (slice(None), pl.ds(half * cols, cols))
                s_out, d_out, d_in = lands[a].at[held].at[cut], lands[a].at[held].at[cut], lands[a].at[theirs].at[cut]
            else:
                dev, _ = _peer_of(1, x, y, c)
                _, theirs = _peer_of(k | 1, x, y, c)
                s_out, d_out, d_in = lands[a].at[peer], lands[a].at[peer], lands[a].at[theirs]
            both = [pltpu.make_async_remote_copy(src_ref=s_out, dst_ref=d, send_sem=send_sems.at[i], recv_sem=recv_sems.at[i],
                                                 device_id=dev, device_id_type=pl.DeviceIdType.MESH) for d in (d_out, d_in)]
            pairs.append(tuple(both))
    return pairs


def split_start(name, mode, ks, srcs, lands=None, after=None):
    n, nk = len(srcs) if lands is None else len(lands), len(ks)
    srcs = [pltpu.with_memory_space_constraint(s, pltpu.HBM) for s in srcs]
    if lands is None:
        shapes = [((N_DEV,) + s.shape) if mode == "gather" else ((LAND_SLOTS[mode],) + s.shape[1:]) for s in srcs]
        lands = [lax.empty(shp, s.dtype) for shp, s in zip(shapes, srcs)]
    lands = [pltpu.with_memory_space_constraint(l, pltpu.HBM) for l in lands]
    both = srcs + lands
    extra = [] if after is None else [after]

    def body(*refs):
        src_refs, land_refs = refs[:len(srcs)], refs[len(srcs):len(both)]
        send_sems, recv_sems = refs[len(both) + len(extra)], refs[len(both) + len(extra) + 1]
        token = refs[-1]
        for out, _ in _split_copies(mode, ks, src_refs, land_refs, send_sems, recv_sems):
            out.start()
        token[...] = jnp.zeros_like(token)

    out_shape = (pltpu.SemaphoreType.DMA((n * nk,)), pltpu.SemaphoreType.DMA((n * nk,)),
                 *[pltpu.HBM(a.shape, a.dtype) for a in both], _sds((8, 128), F32))
    res = _raw_call(body, name=name, out_shape=out_shape, in_specs=[_HBM] * len(both) + [_ANY] * len(extra),
                    out_specs=(_SEM, _SEM, *[_HBM] * len(both), pl.BlockSpec(memory_space=pltpu.VMEM)),
                    input_output_aliases={i: 2 + i for i in range(len(both))}, **_SPLIT_PARAMS)(*both, *extra)
    _PENDING.append(res[-1])
    return dict(mode=mode, ks=ks, sems=(res[0], res[1]), srcs=list(res[2:2 + len(srcs)]),
                lands=list(res[2 + len(srcs):2 + len(both)]), token=res[-1])


def split_wait(name, h, after):
    ns = len(h["srcs"])
    both = h["srcs"] + h["lands"]
    after = list(after) if isinstance(after, (list, tuple)) else [after]

    def body(*refs):
        src_refs, land_refs = refs[:ns], refs[ns:len(both)]
        send_sems, recv_sems = refs[len(both)], refs[len(both) + 1]
        for out, arriving in _split_copies(h["mode"], h["ks"], src_refs, land_refs, send_sems, recv_sems):
            out.wait_send()
            arriving.wait_recv()

    res = _raw_call(body, name=name, out_shape=tuple(pltpu.HBM(a.shape, a.dtype) for a in both),
                    in_specs=[_HBM] * len(both) + [_SEM, _SEM] + [_ANY] * len(after), out_specs=tuple([_HBM] * len(both)),
                    input_output_aliases={i: i for i in range(len(both))}, **_SPLIT_PARAMS)(*both, *h["sems"], *after)
    return list(res[:ns]), list(res[ns:])


FORWARD_BLOCKS = (0, 2, 4, 6)


def pair_sum(name, mine, slots, theirs):
    P, R, C = theirs.shape
    tc = 512

    def body(s_ref, a_ref, b_ref, o_ref):
        o_ref[...] = (a_ref[...].astype(F32) + b_ref[...].astype(F32)).astype(o_ref.dtype)

    blk = pl.BlockSpec((None, R, tc), lambda p, i, s: (p, 0, i))
    spec = pltpu.PrefetchScalarGridSpec(num_scalar_prefetch=1, grid=(P, C // tc),
                                        in_specs=[pl.BlockSpec((None, R, tc), lambda p, i, s: (s[p], 0, i)), blk], out_specs=blk)
    return _pcall(body, name=name, grid_spec=spec, out_shape=_sds((P, R, C), theirs.dtype),
                  compiler_params=_cparams(("arbitrary", "arbitrary")))(slots, mine, theirs)


def _rope_parts(pos_ref, inv_ref):
    ang = pos_ref[...] * inv_ref[...]
    return jnp.cos(ang), jnp.sin(ang)


def _rot_half(t):
    lane = lax.broadcasted_iota(jnp.int32, t.shape, 1)
    return jnp.where((lane % HEAD_DIM) < HEAD_DIM // 2, -pltpu.roll(t, 128 - HEAD_DIM // 2, 1), pltpu.roll(t, HEAD_DIM // 2, 1))


def _attn_mask(n):
    row = lax.broadcasted_iota(jnp.int32, (BLK, 2 * BLK), 0)
    col = lax.broadcasted_iota(jnp.int32, (BLK, 2 * BLK), 1)
    dist = row + BLK - col
    return (dist >= 0) & (dist < BLK) & ((n * BLK - BLK + col) >= 0)


def _attn_specs(T):
    prev = lambda n: jnp.maximum(n - 1, 0)
    kc = pl.BlockSpec((BLK, KV_DIM), lambda n: (n, OFF_K // KV_DIM))
    kp = pl.BlockSpec((BLK, KV_DIM), lambda n: (prev(n), OFF_K // KV_DIM))
    vc = pl.BlockSpec((BLK, KV_DIM), lambda n: (n, OFF_V // KV_DIM))
    vp = pl.BlockSpec((BLK, KV_DIM), lambda n: (prev(n), OFF_V // KV_DIM))
    pc = pl.BlockSpec((BLK, 1), lambda n: (n, 0))
    pp = pl.BlockSpec((BLK, 1), lambda n: (prev(n), 0))
    inv = pl.BlockSpec((1, 128), lambda n: (0, 0))
    sink = pl.BlockSpec(memory_space=pltpu.SMEM)
    return kc, kp, vc, vp, pc, pp, inv, sink


def _softmax_sink(sc, valid, sink):
    sc = jnp.where(valid, sc * (HEAD_DIM ** -0.5), -1e30)
    m = jnp.maximum(jnp.max(sc, axis=1, keepdims=True), sink)
    e = jnp.exp(sc - m)
    es = jnp.exp(sink - m)
    den = jnp.sum(e, axis=1, keepdims=True) + es
    return e / den, es / den


def attn_fwd(projp, posf, inv128, sinks):
    T = projp.shape[0]
    kc, kp, vc, vp, pc, pp, inv, sink = _attn_specs(T)

    def body(q_ref, kc_ref, kp_ref, vc_ref, vp_ref, pc_ref, pp_ref, inv_ref, sink_ref, o_ref, qr_ref, kr_ref):
        n = pl.program_id(0)
        cos_c, sin_c = _rope_parts(pc_ref, inv_ref)
        cos_p, sin_p = _rope_parts(pp_ref, inv_ref)
        valid = _attn_mask(n)
        k_c, k_p = [], []
        for s in range(KV_DIM // 128):
            t = kc_ref[:, 128 * s:128 * (s + 1)]
            k_c.append((t * cos_c + _rot_half(t) * sin_c).astype(BF16))
            kr_ref[:, 128 * s:128 * (s + 1)] = k_c[s]
            t = kp_ref[:, 128 * s:128 * (s + 1)]
            k_p.append((t * cos_p + _rot_half(t) * sin_p).astype(BF16))
        kcat, vcat = [], []
        for hk in range(KV_HEADS):
            lo = HEAD_DIM * (hk % 2)
            kcat.append(jnp.concatenate([k_p[hk // 2][:, lo:lo + HEAD_DIM], k_c[hk // 2][:, lo:lo + HEAD_DIM]], axis=0))
            vcat.append(jnp.concatenate([vp_ref[:, HEAD_DIM * hk:HEAD_DIM * (hk + 1)], vc_ref[:, HEAD_DIM * hk:HEAD_DIM * (hk + 1)]], axis=0)
                        .astype(BF16))
        q_heads = []
        for s in range(Q_DIM // 128):
            t = q_ref[:, 128 * s:128 * (s + 1)]
            qs = (t * cos_c + _rot_half(t) * sin_c).astype(BF16)
            qr_ref[:, 128 * s:128 * (s + 1)] = qs
            q_heads += [qs[:, :HEAD_DIM], qs[:, HEAD_DIM:]]
        G = ATTN_HEADS // KV_HEADS
        scores = [_dot(q_heads[hq], kcat[hq // G], NT) for hq in range(ATTN_HEADS)]
        probs = [_softmax_sink(scores[hq], valid, sink_ref[0, hq])[0] for hq in range(ATTN_HEADS)]
        outs = [_dot(probs[hq], vcat[hq // G], NN) for hq in range(ATTN_HEADS)]
        for s in range(Q_DIM // 128):
            o_ref[:, 128 * s:128 * (s + 1)] = jnp.concatenate([outs[2 * s], outs[2 * s + 1]], axis=1).astype(BF16)

    qspec = pl.BlockSpec((BLK, Q_DIM), lambda n: (n, OFF_Q // Q_DIM))
    orow = pl.BlockSpec((BLK, Q_DIM), lambda n: (n, 0))
    krow = pl.BlockSpec((BLK, KV_DIM), lambda n: (n, 0))
    return _pcall(body, name="attn_fwd", grid=(T // BLK,), in_specs=[qspec, kc, kp, vc, vp, pc, pp, inv, sink],
                  out_specs=[orow, orow, krow], out_shape=[_sds((T, Q_DIM), BF16), _sds((T, Q_DIM), BF16), _sds((T, KV_DIM), BF16)],
                  compiler_params=_cparams(("arbitrary",)))(projp, projp, projp, projp, projp, posf, posf, inv128, sinks)


def attn_bwd(qr, kr, projp, dattn, posf, inv128, sinks, dproj):
    T = projp.shape[0]
    _, _, vc, vp, pc, pp, inv, sink = _attn_specs(T)
    G = ATTN_HEADS // KV_HEADS

    def body(qr_ref, krc_ref, krp_ref, vc_ref, vp_ref, do_ref, pc_ref, pp_ref, inv_ref, sink_ref, _, dq_ref, dk_ref, dv_ref, dsk_ref):
        n = pl.program_id(0)

        @pl.when(n == 0)
        def _():
            dk_ref[...] = jnp.zeros_like(dk_ref)
            dv_ref[...] = jnp.zeros_like(dv_ref)
            dsk_ref[...] = jnp.zeros_like(dsk_ref)

        cos_c, sin_c = _rope_parts(pc_ref, inv_ref)
        cos_p, sin_p = _rope_parts(pp_ref, inv_ref)
        valid = _attn_mask(n)
        lane = lax.broadcasted_iota(jnp.int32, (1, 128), 1)
        kcat, vcat = [], []
        for hk in range(KV_HEADS):
            ksl = slice(HEAD_DIM * hk, HEAD_DIM * (hk + 1))
            kcat.append(jnp.concatenate([krp_ref[:, ksl], krc_ref[:, ksl]], axis=0))
            vcat.append(jnp.concatenate([vp_ref[:, ksl], vc_ref[:, ksl]], axis=0).astype(BF16))
        H = range(ATTN_HEADS)
        q_heads = [qr_ref[:, HEAD_DIM * hq:HEAD_DIM * (hq + 1)] for hq in H]
        do_heads = [do_ref[:, HEAD_DIM * hq:HEAD_DIM * (hq + 1)] for hq in H]
        soft = [_softmax_sink(_dot(q_heads[hq], kcat[hq // G], NT), valid, sink_ref[0, hq]) for hq in H]
        dps = [_dot(do_heads[hq], vcat[hq // G], NT) for hq in H]
        deltas = [jnp.sum(soft[hq][0] * dps[hq], axis=1, keepdims=True) for hq in H]
        dss = [(soft[hq][0] * (dps[hq] - deltas[hq]) * (HEAD_DIM ** -0.5)).astype(BF16) for hq in H]
        pbs = [soft[hq][0].astype(BF16) for hq in H]
        dsk = jnp.zeros((1, 128), F32)
        for hq in H:
            dsk = dsk + jnp.where(lane == hq, -jnp.sum(soft[hq][1] * deltas[hq], axis=0, keepdims=True), 0.0)
        dsk_ref[...] += dsk
        dq_heads = [_dot(dss[hq], kcat[hq // G], NN) for hq in H]
        dk_parts = [_dot(dss[hq], q_heads[hq], TN) for hq in H]
        dv_parts = [_dot(pbs[hq], do_heads[hq], TN) for hq in H]
        dk_heads = [sum(dk_parts[G * hk + 1:G * (hk + 1)], dk_parts[G * hk]) for hk in range(KV_HEADS)]
        dv_heads = [sum(dv_parts[G * hk + 1:G * (hk + 1)], dv_parts[G * hk]) for hk in range(KV_HEADS)]
        for s in range(Q_DIM // 128):
            t = jnp.concatenate([dq_heads[2 * s], dq_heads[2 * s + 1]], axis=1)
            dq_ref[:, 128 * s:128 * (s + 1)] = (t * cos_c - _rot_half(t) * sin_c).astype(BF16)
        cur = pl.ds(pl.multiple_of(n * BLK, BLK), BLK)
        prv = pl.ds(pl.multiple_of(jnp.maximum(n - 1, 0) * BLK, BLK), BLK)
        for s in range(KV_DIM // 128):
            tc = jnp.concatenate([dk_heads[2 * s][BLK:], dk_heads[2 * s + 1][BLK:]], axis=1)
            tp = jnp.concatenate([dk_heads[2 * s][:BLK], dk_heads[2 * s + 1][:BLK]], axis=1)
            cols = slice(128 * s, 128 * (s + 1))
            dk_ref[cur, cols] += tc * cos_c - _rot_half(tc) * sin_c
            dk_ref[prv, cols] += tp * cos_p - _rot_half(tp) * sin_p
            dv_ref[cur, cols] += jnp.concatenate([dv_heads[2 * s][BLK:], dv_heads[2 * s + 1][BLK:]], axis=1)
            dv_ref[prv, cols] += jnp.concatenate([dv_heads[2 * s][:BLK], dv_heads[2 * s + 1][:BLK]], axis=1)

    qrow = pl.BlockSpec((BLK, Q_DIM), lambda n: (n, 0))
    krc = pl.BlockSpec((BLK, KV_DIM), lambda n: (n, 0))
    krp = pl.BlockSpec((BLK, KV_DIM), lambda n: (jnp.maximum(n - 1, 0), 0))
    whole = pl.BlockSpec((T, KV_DIM), lambda n: (0, 0))
    return _pcall(body, name="attn_bwd", grid=(T // BLK,), in_specs=[qrow, krc, krp, vc, vp, qrow, pc, pp, inv, sink, _ANY],
                  out_specs=[pl.BlockSpec((BLK, Q_DIM), lambda n: (n, OFF_Q // Q_DIM)), whole, whole, pl.BlockSpec((1, 128), lambda n: (0, 0))],
                  out_shape=[_sds(dproj.shape, dproj.dtype), _sds((T, KV_DIM), F32), _sds((T, KV_DIM), F32), _sds((1, 128), F32)],
                  input_output_aliases={10: 0},
                  compiler_params=_cparams(("arbitrary",)))(qr, kr, kr, projp, projp, dattn, posf, posf, inv128, sinks, dproj)


CONV_CB = 256


def _shift_down(x, s):
    row = lax.broadcasted_iota(jnp.int32, x.shape, 0)
    return jnp.where(row >= s, pltpu.roll(x, s, 0), 0.0)


def _shift_up(x, s):
    T = x.shape[0]
    row = lax.broadcasted_iota(jnp.int32, x.shape, 0)
    return jnp.where(row < T - s, pltpu.roll(x, T - s, 0), 0.0)


def _conv_pre(x, w_ref, b_ref):
    acc = x * w_ref[CONV_WIDTH - 1:CONV_WIDTH, :] + b_ref[...]
    for s in range(1, CONV_WIDTH):
        acc = acc + _shift_down(x, s) * w_ref[CONV_WIDTH - 1 - s:CONV_WIDTH - s, :]
    return acc


def conv_fwd(projp, conv_w, conv_b):
    T = projp.shape[0]

    def body(x_ref, w_ref, b_ref, o_ref):
        o_ref[...] = _silu(_conv_pre(x_ref[...], w_ref, b_ref))

    return _pcall(body, name="conv_fwd", grid=(CONV_DIM // CONV_CB,),
                  in_specs=[pl.BlockSpec((T, CONV_CB), lambda c: (0, OFF_XBC // CONV_CB + c)),
                            pl.BlockSpec((CONV_WIDTH, CONV_CB), lambda c: (0, c)), pl.BlockSpec((1, CONV_CB), lambda c: (0, c))],
                  out_specs=pl.BlockSpec((T, CONV_CB), lambda c: (0, c)), out_shape=_sds((T, CONV_DIM), F32),
                  compiler_params=_cparams(("arbitrary",)))(projp, conv_w, conv_b)


def conv_bwd(name, projp, dact, conv_w, conv_b, col0, dproj):
    T, C = dact.shape
    c0 = col0 // CONV_CB

    def body(x_ref, da_ref, w_ref, b_ref, _, dx_ref, dw_ref, db_ref):
        x = x_ref[...]
        dpre = da_ref[...] * _dsilu(_conv_pre(x, w_ref, b_ref))
        dx = dpre * w_ref[CONV_WIDTH - 1:CONV_WIDTH, :]
        dw_ref[CONV_WIDTH - 1:CONV_WIDTH, :] = jnp.sum(dpre * x, axis=0, keepdims=True)
        for s in range(1, CONV_WIDTH):
            i = CONV_WIDTH - 1 - s
            dx = dx + _shift_up(dpre, s) * w_ref[i:i + 1, :]
            dw_ref[i:i + 1, :] = jnp.sum(dpre * _shift_down(x, s), axis=0, keepdims=True)
        dx_ref[...] = dx.astype(BF16)
        db_ref[...] = jnp.sum(dpre, axis=0, keepdims=True)

    return _pcall(body, name=name, grid=(C // CONV_CB,),
                  in_specs=[pl.BlockSpec((T, CONV_CB), lambda c: (0, OFF_XBC // CONV_CB + c0 + c)),
                            pl.BlockSpec((T, CONV_CB), lambda c: (0, c)),
                            pl.BlockSpec((CONV_WIDTH, CONV_CB), lambda c: (0, c0 + c)), pl.BlockSpec((1, CONV_CB), lambda c: (0, c0 + c)), _ANY],
                  out_specs=[pl.BlockSpec((T, CONV_CB), lambda c: (0, OFF_XBC // CONV_CB + c0 + c)),
                             pl.BlockSpec((CONV_WIDTH, CONV_CB), lambda c: (0, c)), pl.BlockSpec((1, CONV_CB), lambda c: (0, c))],
                  out_shape=[_sds(dproj.shape, dproj.dtype), _sds((CONV_WIDTH, C), F32), _sds((1, C), F32)],
                  input_output_aliases={4: 0}, compiler_params=_cparams(("arbitrary",)))(projp, dact, conv_w, conv_b, dproj)


def _softplus(x):
    return jnp.maximum(x, 0.0) + jnp.log1p(jnp.exp(-jnp.abs(x)))


def _tri(lower):
    r = lax.broadcasted_iota(jnp.int32, (BLK, BLK), 0)
    c = lax.broadcasted_iota(jnp.int32, (BLK, BLK), 1)
    return (r >= c) if lower else (c >= r)


def _ssd_chunk_setup(dt_ref, dtb_ref, alog_ref):
    raw = dt_ref[...] + dtb_ref[...]
    dt = _softplus(raw)
    aneg = -jnp.exp(alog_ref[...])
    a = dt * aneg
    acs = jnp.dot(_tri(True).astype(F32), a, precision=lax.Precision.HIGHEST, preferred_element_type=F32)
    return raw, dt, aneg, acs, acs.T


def _ssd_specs(T, rev):
    nc = T // BLK
    ci = (lambda c: nc - 1 - c) if rev else (lambda c: c)
    xs = pl.BlockSpec((BLK, D_INNER), lambda c: (ci(c), 0))
    bm = pl.BlockSpec((BLK, SSM_GROUPS * D_STATE), lambda c: (ci(c), D_INNER // (SSM_GROUPS * D_STATE)))
    cm = pl.BlockSpec((BLK, SSM_GROUPS * D_STATE), lambda c: (ci(c), D_INNER // (SSM_GROUPS * D_STATE) + 1))
    dt = pl.BlockSpec((BLK, DT_PAD), lambda c: (ci(c), OFF_DT // DT_PAD))
    v128 = pl.BlockSpec((1, 128), lambda c: (0, 0))
    dfull = pl.BlockSpec((1, D_INNER), lambda c: (0, 0))
    st = pl.BlockSpec((None, SSM_HEADS, HEAD_DIM, D_STATE), lambda c: (ci(c), 0, 0, 0))
    return xs, bm, cm, dt, v128, dfull, st, ci


GW = HEADS_PER_GROUP * HEAD_DIM


def _expanders():
    e = np.zeros((SSM_GROUPS, 128, GW), np.float32)
    for g in range(SSM_GROUPS):
        for hh in range(HEADS_PER_GROUP):
            e[g, HEADS_PER_GROUP * g + hh, HEAD_DIM * hh:HEAD_DIM * (hh + 1)] = 1.0
    return jnp.asarray(e, BF16), jnp.asarray(np.transpose(e, (0, 2, 1)).copy(), BF16)


def _split2(v):
    hi = lax.bitcast_convert_type(lax.bitcast_convert_type(v, jnp.uint32) & jnp.uint32(0xFFFF0000), F32)
    return hi.astype(BF16), (v - hi).astype(BF16)


def _dotx(a, b):
    if a.dtype == BF16:
        hi, lo = _split2(b)
        return jnp.dot(a, hi, preferred_element_type=F32) + jnp.dot(a, lo, preferred_element_type=F32)
    hi, lo = _split2(a)
    return jnp.dot(hi, b, preferred_element_type=F32) + jnp.dot(lo, b, preferred_element_type=F32)


def _decay(acs, acsT, h, tril):
    return jnp.where(tril, jnp.exp(jnp.where(tril, acs[:, h:h + 1] - acsT[h:h + 1, :], 0.0)), 0.0)


def ssd_fwd(xbc, projp, dtb, alog, dfull):
    T = xbc.shape[0]
    nc = T // BLK
    xs, bm, cm, dts, v128, dfs, st, _ = _ssd_specs(T, False)
    E, _ = _expanders()

    def body(xs_ref, b_ref, c_ref, dt_ref, dtb_ref, alog_ref, d_ref, e_ref, y_ref, st_ref, h_scr):
        c = pl.program_id(0)

        @pl.when(c == 0)
        def _():
            h_scr[...] = jnp.zeros_like(h_scr)

        _, dt, _, acs, acsT = _ssd_chunk_setup(dt_ref, dtb_ref, alog_ref)
        tril = _tri(True)
        alast = acs[BLK - 1:BLK, :]
        eacs = jnp.exp(acs)
        wmat = jnp.exp(alast - acs)
        gam = jnp.exp(alast)
        for g in range(SSM_GROUPS):
            gl = slice(GW * g, GW * (g + 1))
            hsl = slice(HEADS_PER_GROUP * g, HEADS_PER_GROUP * (g + 1))
            heads = [HEADS_PER_GROUP * g + hh for hh in range(HEADS_PER_GROUP)]
            Eg = e_ref[g]
            B = b_ref[:, D_STATE * g:D_STATE * (g + 1)].astype(BF16)
            C = c_ref[:, D_STATE * g:D_STATE * (g + 1)].astype(BF16)
            cb = _dot(C, B, NT)
            x_g = xs_ref[:, gl]
            xd_g = x_g * _dotx(dt, Eg)
            hold = h_scr[hsl]
            st_ref[hsl] = hold
            hcat = hold.reshape(GW, D_STATE)
            yoff = _dotx(eacs, Eg) * _dot(C, hcat, NT)
            S = _dot(xd_g * _dotx(wmat, Eg), B, TN)
            Ms = [cb * _decay(acs, acsT, h, tril) for h in heads]
            ys = [_dot(Ms[hh], xd_g[:, HEAD_DIM * hh:HEAD_DIM * (hh + 1)], NN) for hh in range(HEADS_PER_GROUP)]
            for hh, h in enumerate(heads):
                h_scr[h] = gam[:, h:h + 1] * hold[hh] + S[HEAD_DIM * hh:HEAD_DIM * (hh + 1)]
            y_ref[:, gl] = jnp.concatenate(ys, axis=1) + yoff + d_ref[:, gl] * x_g

    espec = pl.BlockSpec((SSM_GROUPS, 128, GW), lambda c: (0, 0, 0))
    return _pcall(body, name="ssd_fwd", grid=(nc,), in_specs=[xs, bm, cm, dts, v128, v128, dfs, espec],
                  out_specs=[xs, st], out_shape=[_sds((T, D_INNER), F32), _sds((nc, SSM_HEADS, HEAD_DIM, D_STATE), F32)],
                  scratch_shapes=[pltpu.VMEM((SSM_HEADS, HEAD_DIM, D_STATE), F32)],
                  compiler_params=_cparams(("arbitrary",)))(xbc, xbc, xbc, projp, dtb, alog, dfull, E)


def ssd_bwd(xbc, projp, dtb, alog, dfull, states, dy, dproj):
    T = xbc.shape[0]
    nc = T // BLK
    xs, bm, cm, dts, v128, dfs, st, ci = _ssd_specs(T, True)
    gn = SSM_GROUPS * D_STATE
    E, ET = _expanders()

    def body(xs_ref, b_ref, c_ref, dt_ref, dtb_ref, alog_ref, d_ref, st_ref, dy_ref, e_ref, et_ref, _,
             dxs_ref, dB_ref, dC_ref, ddt_ref, dal_ref, dD_ref, ddtb_ref, dh_scr):
        i = pl.program_id(0)

        @pl.when(i == 0)
        def _():
            dh_scr[...] = jnp.zeros_like(dh_scr)
            dal_ref[...] = jnp.zeros_like(dal_ref)
            dD_ref[...] = jnp.zeros_like(dD_ref)
            ddtb_ref[...] = jnp.zeros_like(ddtb_ref)

        raw, dt, aneg, acs, acsT = _ssd_chunk_setup(dt_ref, dtb_ref, alog_ref)
        tril = _tri(True)
        lane = lax.broadcasted_iota(jnp.int32, (BLK, 128), 1)
        sub = lax.broadcasted_iota(jnp.int32, (BLK, 128), 0)
        alast = acs[BLK - 1:BLK, :]
        eacs = jnp.exp(acs)
        wmat = jnp.exp(alast - acs)
        gam = jnp.exp(alast)
        gcol = jnp.exp(acsT[:, BLK - 1:BLK])
        ds_col = jnp.zeros((BLK, 128), F32)
        ds_row = jnp.zeros((BLK, 128), F32)
        ddt_col = jnp.zeros((BLK, 128), F32)
        dDm = jnp.zeros((BLK, 128), F32)
        hl = [slice(HEAD_DIM * hh, HEAD_DIM * (hh + 1)) for hh in range(HEADS_PER_GROUP)]
        for g in range(SSM_GROUPS):
            gl = slice(GW * g, GW * (g + 1))
            gs = slice(D_STATE * g, D_STATE * (g + 1))
            hsl = slice(HEADS_PER_GROUP * g, HEADS_PER_GROUP * (g + 1))
            heads = [HEADS_PER_GROUP * g + hh for hh in range(HEADS_PER_GROUP)]
            Eg, ETg = e_ref[g], et_ref[g]
            B = b_ref[:, gs].astype(BF16)
            C = c_ref[:, gs].astype(BF16)
            cb = _dot(C, B, NT)
            x_g, dy_g = xs_ref[:, gl], dy_ref[:, gl]
            dt_x, w_x = _dotx(dt, Eg), _dotx(wmat, Eg)
            xd_g = x_g * dt_x
            dye = dy_g * _dotx(eacs, Eg)
            hcat = st_ref[hsl].reshape(GW, D_STATE)
            dSv = dh_scr[hsl]
            dScat = dSv.reshape(GW, D_STATE)
            dDm = dDm + _dotx(dy_g * x_g, ETg)
            dH_y = _dot(dye, C, TN)
            dC_g = _dot(dye, hcat, NN)
            ds_col = ds_col + _dotx(dye * _dot(C, hcat, NT), ETg)
            dxdw = _dot(B, dScat, NT)
            dB_g = _dot(xd_g * w_x, dScat, NN)
            dww = _dotx(xd_g * dxdw, ETg) * wmat
            ds_col = ds_col - dww + jnp.where(sub == BLK - 1, jnp.sum(dww, axis=0, keepdims=True), 0.0)
            hd = jnp.sum(_dotx(Eg, dScat * hcat), axis=1, keepdims=True) * gcol
            ds_row = ds_row - jnp.where(lane == BLK - 1, hd, 0.0)
            decays = [_decay(acs, acsT, h, tril) for h in heads]
            Ms = [cb * d for d in decays]
            dMs = [_dot(dy_g[:, hl[hh]], xd_g[:, hl[hh]], NT) for hh in range(HEADS_PER_GROUP)]
            dxd1 = [_dot(Ms[hh], dy_g[:, hl[hh]], TN) for hh in range(HEADS_PER_GROUP)]
            dG = jnp.zeros((BLK, BLK), F32)
            for hh, h in enumerate(heads):
                Q = dMs[hh] * Ms[hh]
                ds_col = ds_col + jnp.where(lane == h, jnp.sum(Q, axis=1, keepdims=True), 0.0)
                ds_row = ds_row + jnp.where(sub == h, jnp.sum(Q, axis=0, keepdims=True), 0.0)
                dG = dG + dMs[hh] * decays[hh]
            dxd_g = jnp.concatenate(dxd1, axis=1) + w_x * dxdw
            dxs_ref[:, gl] = d_ref[:, gl] * dy_g + dxd_g * dt_x
            ddt_col = ddt_col + _dotx(dxd_g * x_g, ETg)
            dC_ref[:, gs] = dC_g + _dot(dG, B, NN)
            dB_ref[:, gs] = dB_g + _dot(dG, C, TN)
            for hh, h in enumerate(heads):
                dh_scr[h] = gam[:, h:h + 1] * dSv[hh] + dH_y[hl[hh]]
        ds_all = ds_col - ds_row.T
        da = jnp.dot(_tri(False).astype(F32), ds_all, precision=lax.Precision.HIGHEST, preferred_element_type=F32)
        ddt = ddt_col + da * aneg
        draw = jnp.where(lane < SSM_HEADS, ddt * _sigmoid(raw), 0.0)
        ddt_ref[...] = draw.astype(BF16)
        dal_ref[...] += jnp.sum(da * dt, axis=0, keepdims=True) * aneg
        ddtb_ref[...] += jnp.sum(draw, axis=0, keepdims=True)
        dD_ref[...] += jnp.sum(dDm, axis=0, keepdims=True)

    gblk = pl.BlockSpec((BLK, gn), lambda c: (ci(c), 0))
    espec = pl.BlockSpec((SSM_GROUPS, 128, GW), lambda c: (0, 0, 0))
    etspec = pl.BlockSpec((SSM_GROUPS, GW, 128), lambda c: (0, 0, 0))
    return _pcall(body, name="ssd_bwd", grid=(nc,), in_specs=[xs, bm, cm, dts, v128, v128, dfs, st, xs, espec, etspec, _ANY],
                  out_specs=[xs, gblk, gblk, pl.BlockSpec((BLK, DT_PAD), lambda c: (ci(c), OFF_DT // DT_PAD)), v128, v128, v128],
                  out_shape=[_sds((T, D_INNER), F32), _sds((T, gn), F32), _sds((T, gn), F32), _sds(dproj.shape, dproj.dtype),
                             _sds((1, 128), F32), _sds((1, 128), F32), _sds((1, 128), F32)],
                  scratch_shapes=[pltpu.VMEM((SSM_HEADS, HEAD_DIM, D_STATE), F32)], input_output_aliases={11: 3},
                  compiler_params=_cparams(("arbitrary",)))(xbc, xbc, xbc, projp, dtb, alog, dfull, states, dy, E, ET, dproj)


_WIN_ORDER = ("ga", "gs", "z", "xbc", "q", "k", "v", "dt")


PERM_TILE = 512


def _win_row_moves():
    off = dict(z=OFF_Z, ga=OFF_GA, gs=OFF_GS, xbc=OFF_XBC, q=OFF_Q, k=OFF_K, v=OFF_V, dt=OFF_DT)
    per = IN_DIM // N_DEV
    tiles = [[] for _ in range(-(-IN_PAD // PERM_TILE))]
    for nm in _WIN_ORDER:
        s, w = SEG[nm]
        d = off[nm]
        while w > 0:
            j, r = divmod(s, per)
            n = min(w, per - r, PERM_TILE - d % PERM_TILE)
            tiles[d // PERM_TILE].append((j, r, n, d % PERM_TILE))
            s, w, d = s + n, w - n, d + n
    return tiles


def _win_to_padded(win_g):
    per = IN_DIM // N_DEV
    moves = _win_row_moves()

    def body(w_ref, o_ref, slots, stage, in_sems, out_sems):
        loads = [pltpu.make_async_copy(w_ref.at[j], slots.at[j], in_sems.at[j]) for j in range(N_DEV)]
        for cp in loads:
            cp.start()
        arrived = [False] * N_DEV
        stores = [None, None]
        for t, pieces in enumerate(moves):
            rows = min(PERM_TILE, IN_PAD - PERM_TILE * t)
            b = t % 2
            if stores[b] is not None:
                stores[b].wait()
            filled = 0
            for j, r, n, d in pieces:
                if not arrived[j]:
                    loads[j].wait()
                    arrived[j] = True
                stage[b, pl.ds(d, n), :] = slots[j, pl.ds(r, n), :]
                filled = max(filled, d + n)
            if filled < rows:
                stage[b, pl.ds(filled, rows - filled), :] = jnp.zeros((rows - filled, D_MODEL), stage.dtype)
            stores[b] = pltpu.make_async_copy(stage.at[b, pl.ds(0, rows), :], o_ref.at[pl.ds(PERM_TILE * t, rows), :], out_sems.at[b])
            stores[b].start()
        for cp in stores:
            cp.wait()

    return _pcall(body, name="w_in_rows", in_specs=[_ANY], out_specs=_ANY, out_shape=_sds((IN_PAD, D_MODEL), win_g.dtype),
                  scratch_shapes=[pltpu.VMEM((N_DEV, per, D_MODEL), win_g.dtype), pltpu.VMEM((2, PERM_TILE, D_MODEL), win_g.dtype),
                                  pltpu.SemaphoreType.DMA((N_DEV,)), pltpu.SemaphoreType.DMA((2,))],
                  compiler_params=pltpu.CompilerParams(vmem_limit_bytes=VMEM_LIMIT_ROWS))(win_g)


def _padded_to_win(dw):
    per = IN_DIM // N_DEV
    moves = _win_row_moves()

    def body(d_ref, o_ref, slots, stage, in_sems, out_sems):
        def load(t):
            rows = min(PERM_TILE, IN_PAD - PERM_TILE * t)
            return pltpu.make_async_copy(d_ref.at[pl.ds(PERM_TILE * t, rows), :], stage.at[t % 2, pl.ds(0, rows), :], in_sems.at[t % 2])

        pending = load(0)
        pending.start()
        for t, pieces in enumerate(moves):
            pending.wait()
            if t + 1 < len(moves):
                pending = load(t + 1)
                pending.start()
            for j, r, n, d in pieces:
                slots[j, pl.ds(r, n), :] = stage[t % 2, pl.ds(d, n), :]
        stores = [pltpu.make_async_copy(slots.at[j], o_ref.at[j], out_sems.at[j]) for j in range(N_DEV)]
        for cp in stores:
            cp.start()
        for cp in stores:
            cp.wait()

    return _pcall(body, name="d_w_in_rows", in_specs=[_ANY], out_specs=_ANY, out_shape=_sds((N_DEV, per, D_MODEL), dw.dtype),
                  scratch_shapes=[pltpu.VMEM((N_DEV, per, D_MODEL), dw.dtype), pltpu.VMEM((2, PERM_TILE, D_MODEL), dw.dtype),
                                  pltpu.SemaphoreType.DMA((2,)), pltpu.SemaphoreType.DMA((N_DEV,))],
                  compiler_params=pltpu.CompilerParams(vmem_limit_bytes=VMEM_LIMIT_ROWS))(dw)


def _pad128(v):
    return jnp.pad(v, ((0, 0), (0, 128 - v.shape[1])))


_SMALL = (("loss", 128, 1), ("g_mix", 2048, 2048), ("conv_b", 3072, 3072), ("dt_bias", 128, 32), ("a_log", 128, 32),
          ("d_skip", 128, 32), ("g_ssd", 2048, 2048), ("sinks", 128, 16), ("g_ffn", 2048, 2048), ("g_ple", 2048, 2048),
          ("g_final", 2048, 2048))


def _small_vec(d):
    parts = []
    for nm, pw, w in _SMALL:
        v = d[nm].reshape(1, -1).astype(F32)
        parts.append(jnp.pad(v[:, :min(v.shape[1], pw)], ((0, 0), (0, pw - min(v.shape[1], pw)))))
    return jnp.concatenate(parts, axis=1)


def _small_split(vec):
    out, o = {}, 0
    for nm, pw, w in _SMALL:
        out[nm] = vec[0, o:o + w]
        o += pw
    return out


def kernel(x, p, positions, g_mix, w_in, conv_w, conv_b, dt_bias, a_log, d_skip, g_ssd, sinks, w_attn_br, w_ssd_br, w_o, g_ffn, w_gate, w_up, w_down, g_ple, w_ple_gate, w_ple_proj, g_final, loss_target, m_g_mix, m_w_in, m_conv_w, m_conv_b, m_dt_bias, m_a_log, m_d_skip, m_g_ssd, m_sinks, m_w_attn_br, m_w_ssd_br, m_w_o, m_g_ffn, m_w_gate, m_w_up, m_w_down, m_g_ple, m_w_ple_gate, m_w_ple_proj, m_g_final, v_g_mix, v_w_in, v_conv_w, v_conv_b, v_dt_bias, v_a_log, v_d_skip, v_g_ssd, v_sinks, v_w_attn_br, v_w_ssd_br, v_w_o, v_g_ffn, v_w_gate, v_w_up, v_w_down, v_g_ple, v_w_ple_gate, v_w_ple_proj, v_g_final):
    T = x.shape[1]
    D = D_MODEL
    W = dict(g_mix=g_mix, w_in=w_in, conv_w=conv_w, conv_b=conv_b, dt_bias=dt_bias, a_log=a_log, d_skip=d_skip, g_ssd=g_ssd,
             sinks=sinks, w_attn_br=w_attn_br, w_ssd_br=w_ssd_br, w_o=w_o, g_ffn=g_ffn, w_gate=w_gate, w_up=w_up, w_down=w_down,
             g_ple=g_ple, w_ple_gate=w_ple_gate, w_ple_proj=w_ple_proj, g_final=g_final)
    Mo = dict(g_mix=m_g_mix, w_in=m_w_in, conv_w=m_conv_w, conv_b=m_conv_b, dt_bias=m_dt_bias, a_log=m_a_log, d_skip=m_d_skip,
              g_ssd=m_g_ssd, sinks=m_sinks, w_attn_br=m_w_attn_br, w_ssd_br=m_w_ssd_br, w_o=m_w_o, g_ffn=m_g_ffn, w_gate=m_w_gate,
              w_up=m_w_up, w_down=m_w_down, g_ple=m_g_ple, w_ple_gate=m_w_ple_gate, w_ple_proj=m_w_ple_proj, g_final=m_g_final)
    Vo = dict(g_mix=v_g_mix, w_in=v_w_in, conv_w=v_conv_w, conv_b=v_conv_b, dt_bias=v_dt_bias, a_log=v_a_log, d_skip=v_d_skip,
              g_ssd=v_g_ssd, sinks=v_sinks, w_attn_br=v_w_attn_br, w_ssd_br=v_w_ssd_br, w_o=v_w_o, g_ffn=v_g_ffn, w_gate=v_w_gate,
              w_up=v_w_up, w_down=v_w_down, g_ple=v_g_ple, w_ple_gate=v_w_ple_gate, w_ple_proj=v_w_ple_proj, g_final=v_g_final)
    order = ["g_mix", "w_in", "conv_w", "conv_b", "dt_bias", "a_log", "d_skip", "g_ssd", "sinks", "w_attn_br", "w_ssd_br", "w_o",
             "g_ffn", "w_gate", "w_up", "w_down", "g_ple", "w_ple_gate", "w_ple_proj", "g_final"]
    big = ["w_in", "conv_w", "w_attn_br", "w_ssd_br", "w_o", "w_gate", "w_up", "w_down", "w_ple_gate", "w_ple_proj"]

    x2 = x.reshape(T, D)
    p2 = p.reshape(T, PLE_DIM)
    tgt = loss_target.reshape(T, D)
    posf = positions.reshape(T, 1).astype(F32)
    inv = ROPE_THETA ** (-np.arange(HEAD_DIM // 2, dtype=np.float32) * 2.0 / HEAD_DIM)
    inv128 = jnp.asarray(np.tile(inv, 128 // (HEAD_DIM // 2)).reshape(1, 128).astype(np.float32))
    transposed = ("w_in", "w_gate", "w_up")

    def shard2d(a, n):
        a = a.reshape(a.shape[-2:])
        return a.T if n in transposed else a

    sh = {n: shard2d(W[n], n) for n in big}

    del _PENDING[:]
    me = 4 * lax.axis_index("x") + 2 * lax.axis_index("y") + lax.axis_index("c")
    groups = (("w_in",), ("conv_w", "w_attn_br", "w_ssd_br", "w_o"), ("w_gate", "w_up"), ("w_down",), ("w_ple_gate", "w_ple_proj"))
    send = {n: sh[n] if n == "conv_w" else sh[n].astype(BF16) for n in big}
    zones = [[lax.dynamic_update_index_in_dim(lax.empty((N_DEV,) + send[n].shape, send[n].dtype), send[n], me, 0) for n in grp]
             for grp in groups]
    ring_a = split_start("ring_a_start", "route", ((4, 0, 0), (2, 0, 1)), [], lands=zones[0])
    ring_b = split_start("ring_b_start", "route", ((4, 0, 1), (2, 0, 0)), [], lands=ring_a["lands"], after=ring_a["token"])
    gathered, fwd = {}, {}

    def forward_start(gi, after, lands=None):
        if lands is None:
            _, lands = split_wait("gather_wait_%d" % gi, started[gi], after)
        fwd[gi] = split_start("forward_start_%d" % gi, "forward", FORWARD_BLOCKS, [], lands=lands)

    def forward_wait(gi, after):
        _, full = split_wait("forward_wait_%d" % gi, fwd[gi], after)
        gathered.update(zip(groups[gi], full))

    u = rms_fwd("norm_mix", x2, g_mix)
    _, zone = split_wait("ring_a_wait", dict(ring_a, lands=ring_b["lands"]), [u] + [z for grp in zones[1:] for z in grp])
    ring_c = split_start("ring_c_start", "route", ((2, 4, 0), (4, 2, 1)), [], lands=zone)
    started, prev = [None], ring_c["token"]
    for gi in range(1, len(groups)):
        started.append(split_start("gather_start_%d" % gi, "gather", ICI_SAME_CORE, [], lands=zones[gi], after=prev))
        prev = started[-1]["token"]
    _, zone = split_wait("ring_b_wait", dict(ring_b, lands=ring_c["lands"]), u)
    near = split_start("forward_start_0", "route", ((1, 0, None), (1, 2, None), (1, 4, None)), [], lands=zone)
    _, zone = split_wait("ring_c_wait", dict(ring_c, lands=near["lands"]), u)
    far = split_start("forward_far_start_0", "route", ((1, 6, None),), [], lands=zone)
    _, zone = split_wait("forward_wait_0", dict(near, lands=far["lands"]), u)
    _, zone = split_wait("forward_far_wait_0", dict(far, lands=zone), u)
    winp = _win_to_padded(zone[0])
    dtb = _pad128(dt_bias)
    alog = _pad128(a_log)
    dfull = jnp.repeat(d_skip.reshape(SSM_HEADS), HEAD_DIM).reshape(1, D_INNER)

    projp = mm_nt("in_proj", u, winp, IN_PAD // 5)
    forward_start(1, projp)
    attn, qr, kr = attn_fwd(projp, posf, inv128, sinks)
    forward_wait(1, attn)
    convw = jnp.transpose(gathered["conv_w"], (1, 0, 2)).reshape(CONV_WIDTH, CONV_DIM)
    wab = gathered["w_attn_br"]
    wsb = gathered["w_ssd_br"].reshape(D, D)
    wo = gathered["w_o"].reshape(D, D)
    xbc = conv_fwd(projp, convw, conv_b)
    y, states = ssd_fwd(xbc, projp, dtb, alog, dfull)
    yn = gnorm_fwd(y, projp, g_ssd)
    out_a = mm_nn_colblk("attn_br", attn, wab)
    out_s = mm_nn("ssd_br", yn, wsb, 512)
    forward_start(2, out_s)
    merged = merge_fwd(projp, out_a, out_s)
    h1 = mm_nn("o_proj", merged, wo, 512, residual=x2)
    f = rms_fwd("norm_ffn", h1, g_ffn)
    forward_wait(2, f)
    forward_start(3, f)
    wgt, wut = (gathered[n].reshape(FFN_HIDDEN, D) for n in ("w_gate", "w_up"))
    gate, up, act = ffn_up(f, wgt, wut)
    forward_wait(3, act)
    forward_start(4, act)
    wd = gathered["w_down"].reshape(FFN_HIDDEN, D)
    h2 = mm_nn_red("ffn_down", act, wd, 512, FFN_HIDDEN, residual=h1)
    r = rms_fwd("norm_ple", h2, g_ple)
    forward_wait(4, r)
    wpg = gathered["w_ple_gate"].reshape(D, D)
    wpp = gathered["w_ple_proj"]
    pg = mm_nn("ple_gate", r, wpg, 512)
    pp = mm_nn_colblk("ple_proj", p2, wpp)
    loss_v, dh3, dpg, dpp, dg_final = head_fwd_bwd(h2, pg, pp, g_final.reshape(1, D), tgt)

    gw = {}
    scat = []

    def scatter_start(names):
        scat.append((names, split_start("scatter_start_%d" % len(scat), "scatter", ALL_PEERS, [gw[n] for n in names])))

    gw["w_ple_proj"] = mm_tn_colblk("dw_ple_proj", p2, dpp, PLE_DIM)
    dr = mm_nt("d_ple_gate", dpg, wpg, 512)
    gw["w_ple_gate"] = mm_tn("dw_ple_gate", r, dpg, 512, D).reshape(N_DEV, D // N_DEV, D)
    scatter_start(("w_ple_proj", "w_ple_gate"))
    dh2, dh2b, dg_ple = rms_bwd("norm_ple_bwd", h2, g_ple, dr, dh3)
    dgate, dup = ffn_down_bwd(dh2b, wd, gate, up)
    per = FFN_HIDDEN // N_DEV
    gw["w_down"] = mm_tn("dw_down", act, dh2b, FFN_TILE, D).reshape(N_DEV, per, D)
    gw["w_gate"] = mm_tn("dw_gate", dgate, f, FFN_TILE, D).reshape(N_DEV, per, D)
    gw["w_up"] = mm_tn("dw_up", dup, f, FFN_TILE, D).reshape(N_DEV, per, D)
    scatter_start(("w_down", "w_gate", "w_up"))
    df = ffn_up_bwd(dgate, dup, wgt, wut)
    dh1, dh1b, dg_ffn = rms_bwd("norm_ffn_bwd", h1, g_ffn, df, dh2)
    dmerged = mm_nt("d_o_proj", dh1b, wo, 512)
    gw["w_o"] = mm_tn("dw_o", merged, dh1b, 512, D).reshape(N_DEV, D // N_DEV, D)
    dout_a, dout_s, dproj = merge_bwd(projp, out_a, out_s, dmerged)
    gw["w_ssd_br"] = mm_tn("dw_ssd_br", yn, dout_s, 512, D).reshape(N_DEV, D // N_DEV, D)
    gw["w_attn_br"] = mm_tn_colblk("dw_attn_br", attn, dout_a, D // N_DEV)
    scatter_start(("w_o", "w_ssd_br", "w_attn_br"))
    dyn = mm_nt("d_ssd_br", dout_s, wsb, 512)
    dattn = attn_br_bwd(dout_a, wab)
    dy, dproj, dg_ssd = gnorm_bwd(y, projp, g_ssd, dyn, dproj)
    dxs, dbm, dcm, dproj, dal, ddsk, ddtb = ssd_bwd(xbc, projp, dtb, alog, dfull, states, dy, dproj)
    dproj, dwc_x, dbc_x = conv_bwd("conv_bwd_x", projp, dxs, convw, conv_b, 0, dproj)
    dproj, dwc_b, dbc_b = conv_bwd("conv_bwd_b", projp, dbm, convw, conv_b, D_INNER, dproj)
    dproj, dwc_c, dbc_c = conv_bwd("conv_bwd_c", projp, dcm, convw, conv_b, D_INNER + SSM_GROUPS * D_STATE, dproj)
    dproj, dk, dv, dsk = attn_bwd(qr, kr, projp, dattn, posf, inv128, sinks, dproj)
    dproj = lax.dynamic_update_slice(dproj, jnp.concatenate([dk, dv], axis=1).astype(BF16), (0, OFF_K))
    gw_in = _padded_to_win(mm_tn("dw_in", dproj, u, 640, D))
    pair = split_start("pair_start", "pair", FORWARD_BLOCKS, [gw_in])
    dconvw = jnp.concatenate([dwc_x, dwc_b, dwc_c], axis=1)
    gw["conv_w"] = jnp.transpose(dconvw.reshape(CONV_WIDTH, N_DEV, CONV_DIM // N_DEV), (1, 0, 2))
    scatter_start(("conv_w",))
    du_tm = min(512, T)
    tiles = T // du_tm
    first = max(tiles // 4, 1)
    du = mm_nn_red("d_in_proj_a", dproj, winp, 512, IN_PAD, rows=(0, first), tm=du_tm)
    (gw_in,), (sibling_part,) = split_wait("pair_wait", pair, du)
    pair_slots = jnp.stack([jnp.bitwise_xor(me, k) for k in FORWARD_BLOCKS]).astype(jnp.int32)
    core = split_start("core_start", "scatter_core", ICI_SAME_CORE, [pair_sum("pair_sum_w_in", gw_in, pair_slots, sibling_part)])
    if first < tiles:
        du = mm_nn_red("d_in_proj_b", dproj, winp, 512, IN_PAD, rows=(first, tiles - first), prev=du, tm=du_tm)
    gx, _, dg_mix = rms_bwd("norm_mix_bwd", x2, g_mix, du, dh1)

    small_g = dict(loss=loss_v[:, :1], g_mix=dg_mix, conv_b=jnp.concatenate([dbc_x, dbc_b, dbc_c], axis=1), dt_bias=ddtb,
                   a_log=dal, d_skip=ddsk, g_ssd=dg_ssd, sinks=dsk, g_ffn=dg_ffn, g_ple=dg_ple, g_final=dg_final)
    vec = _small_vec(small_g)
    small = split_start("small_start", "gather", ALL_PEERS, [],
                        lands=[lax.dynamic_update_index_in_dim(lax.empty((N_DEV,) + vec.shape, F32), vec, me, 0)])

    res = {}
    after = [gx]
    for si, (names, h) in enumerate(scat):
        srcs, lands = split_wait("scatter_wait_%d" % si, h, after)
        for n, mine, arrived in zip(names, srcs, lands):
            res[n] = adamw("adamw_" + n, arrived, sh[n], shard2d(Mo[n], n), shard2d(Vo[n], n), own=mine, own_slot=me)
        after = [res[n][0] for n in names]
    zero = jnp.zeros((1, 1), F32)
    _, (vec_parts,) = split_wait("small_wait", small, [res[n][0] for n in res])
    sres = adamw("adamw_small", vec_parts, _small_vec({**W, "loss": zero}), _small_vec({**Mo, "loss": zero}),
                 _small_vec({**Vo, "loss": zero}))
    ssplit = [_small_split(a) for a in sres]

    (pair_sums,), (arrived,) = split_wait("core_wait", core, [sres[0]])
    res["w_in"] = adamw("adamw_w_in", arrived, sh["w_in"], shard2d(Mo["w_in"], "w_in"), shard2d(Vo["w_in"], "w_in"),
                        own=pair_sums, own_slot=0)
    loss = ssplit[0]["loss"].reshape(())
    for n in order:
        if n not in res:
            res[n] = tuple(s[n].reshape(W[n].shape) for s in ssplit)
        else:
            res[n] = tuple((a.T if n in transposed else a).reshape(W[n].shape) for a in res[n])
    outs = [loss, gx.reshape(x.shape)]
    for k in range(4):
        outs += [res[n][k] for n in order]
    return tuple(outs)
```

```python
import numpy as np
import jax
import jax.numpy as jnp
from jax import lax
from jax.experimental import pallas as pl
from jax.experimental.pallas import tpu as pltpu

F32 = jnp.float32
BF16 = jnp.bfloat16

N_DEV = 8
D_MODEL = 2048
HEAD_DIM = 64
ATTN_HEADS = 16
KV_HEADS = 4
Q_DIM = 1024
KV_DIM = 256
BLK = 128
D_INNER = 2048
SSM_HEADS = 32
SSM_GROUPS = 4
HEADS_PER_GROUP = 8
D_STATE = 128
CONV_WIDTH = 4
CONV_DIM = 3072
FFN_HIDDEN = 5632
PLE_DIM = 256
IN_DIM = 10784
NORM_EPS = 1e-6
SSM_NORM_EPS = 1e-5
ROPE_THETA = 10000.0

OFF_GA, OFF_GS, OFF_Z, OFF_XBC, OFF_Q, OFF_K, OFF_V, OFF_DT = 0, 2048, 4096, 6144, 9216, 10240, 10496, 10752
IN_PAD = 10880
DT_PAD = 128
SEG = dict(q=(0, 1024), k=(1024, 256), v=(1280, 256), z=(1536, 2048), xbc=(3584, 3072), dt=(6656, 32),
           ga=(6688, 2048), gs=(8736, 2048))

ADAM_LR, ADAM_B1, ADAM_B2, ADAM_EPS, ADAM_WD, ADAM_STEP = 0.001, 0.9, 0.999, 1e-08, 0.01, 10

VMEM_LIMIT = 56 * 1024 * 1024
VMEM_LIMIT_ROWS = 60 * 1024 * 1024

NN = (((1,), (0,)), ((), ()))
NT = (((1,), (1,)), ((), ()))
TN = (((0,), (0,)), ((), ()))


_PENDING = []


def _raw_call(body, **kw):
    return pl.pallas_call(body, **kw)


def _pcall(body, **kw):
    if "in_specs" not in kw:
        return _raw_call(body, **kw)
    deps = list(_PENDING)
    del _PENDING[:]
    if not deps:
        return _raw_call(body, **kw)
    n_in = len(kw["in_specs"])

    def tied(*refs):
        return body(*refs[:n_in], *refs[n_in + len(deps):])

    kw["in_specs"] = list(kw["in_specs"]) + [pl.BlockSpec(memory_space=pl.ANY)] * len(deps)
    call = _raw_call(tied, **kw)
    return lambda *ops: call(*ops, *deps)


def _cparams(sem=None):
    if sem is None:
        return pltpu.CompilerParams(vmem_limit_bytes=VMEM_LIMIT)
    return pltpu.CompilerParams(vmem_limit_bytes=VMEM_LIMIT, dimension_semantics=sem)


def _dot(a, b, dn):
    return lax.dot_general(a.astype(BF16), b.astype(BF16), dn, preferred_element_type=F32)


def _sigmoid(x):
    return 1.0 / (1.0 + jnp.exp(-x))


def _silu(x):
    return x * _sigmoid(x)


def _dsilu(x):
    s = _sigmoid(x)
    return s * (1.0 + x * (1.0 - s))


def _matmul(name, pairs, pair_specs, dn, grid, out_shapes, out_specs, nred=1, extra=(), extra_specs=(),
            epilogue=None, acc_shape=None, alias=None):
    n_in = 2 * len(pairs) + len(extra)
    n_out = len(out_shapes)

    def body(*refs):
        ins = refs[:2 * len(pairs)]
        ex = [r for r, sp in zip(refs[2 * len(pairs):n_in], extra_specs) if sp.memory_space != pl.ANY]
        outs = refs[n_in:n_in + n_out]

        def prod():
            s = None
            for p in range(len(pairs)):
                d = _dot(ins[2 * p][...], ins[2 * p + 1][...], dn)
                s = d if s is None else s + d
            return s

        def finish(val):
            if epilogue is None:
                outs[0][...] = val.astype(outs[0].dtype)
            else:
                res = epilogue(val, *[e[...] for e in ex])
                for o, r in zip(outs, res):
                    o[...] = r.astype(o.dtype)

        if nred == 1:
            finish(prod())
        else:
            acc = refs[n_in + n_out]
            k = pl.program_id(len(grid) - 1)

            @pl.when(k == 0)
            def _():
                acc[...] = jnp.zeros_like(acc)

            acc[...] += prod()

            @pl.when(k == nred - 1)
            def _():
                finish(acc[...])

    operands = []
    specs = []
    for (a, b), (sa, sb) in zip(pairs, pair_specs):
        operands += [a, b]
        specs += [sa, sb]
    operands += list(extra)
    specs += list(extra_specs)
    scratch = [pltpu.VMEM(acc_shape, F32)] if nred > 1 else []
    sem = ("arbitrary",) * len(grid)
    res = _pcall(body, name=name, grid=grid, in_specs=specs, out_specs=list(out_specs), input_output_aliases=dict(alias or {}),
                 out_shape=list(out_shapes), scratch_shapes=scratch, compiler_params=_cparams(sem))(*operands)
    return res


def _sds(shape, dtype):
    return jax.ShapeDtypeStruct(shape, dtype)


def _row_tile(T):
    return min(1024, T)


def mm_nn(name, a, b, tn, out_dtype=F32, residual=None):
    M, K = a.shape
    N = b.shape[1]
    tm = _row_tile(M)
    grid = (M // tm, N // tn)
    extra, especs, epi = (), (), None
    if residual is not None:
        extra = (residual,)
        especs = (pl.BlockSpec((tm, tn), lambda i, n: (i, n)),)
        epi = lambda v, r: (v + r,)
    return _matmul(name, [(a, b)], [(pl.BlockSpec((tm, K), lambda i, n: (i, 0)), pl.BlockSpec((K, tn), lambda i, n: (0, n)))],
                   NN, grid, [_sds((M, N), out_dtype)], [pl.BlockSpec((tm, tn), lambda i, n: (i, n))],
                   extra=extra, extra_specs=especs, epilogue=epi)[0]


def mm_nn_colblk(name, a, b, out_dtype=F32):
    M, K = a.shape
    J, _, nb = b.shape
    tm = _row_tile(M)
    grid = (M // tm, J)
    return _matmul(name, [(a, b)], [(pl.BlockSpec((tm, K), lambda i, j: (i, 0)), pl.BlockSpec((None, K, nb), lambda i, j: (j, 0, 0)))],
                   NN, grid, [_sds((M, J * nb), out_dtype)], [pl.BlockSpec((tm, nb), lambda i, j: (i, j))])[0]


def mm_nt(name, a, w, tr, out_dtype=F32):
    M, C = a.shape
    R = w.shape[0]
    tm = _row_tile(M)
    grid = (M // tm, R // tr)
    return _matmul(name, [(a, w)], [(pl.BlockSpec((tm, C), lambda i, r: (i, 0)), pl.BlockSpec((tr, C), lambda i, r: (r, 0)))],
                   NT, grid, [_sds((M, R), out_dtype)], [pl.BlockSpec((tm, tr), lambda i, r: (i, r))])[0]


def mm_nn_red(name, a, b, tn, tk, out_dtype=F32, residual=None, rows=None, prev=None, tm=None):
    M, K = a.shape
    N = b.shape[1]
    tm = min(tm or _row_tile(M), M)
    nk = K // tk
    i0, ni = (0, M // tm) if rows is None else rows
    grid = (ni, N // tn, nk)
    ospec = pl.BlockSpec((tm, tn), lambda i, n, k: (i + i0, n))
    extra, especs, epi = [], [], None
    if residual is not None:
        extra, especs, epi = [residual], [ospec], (lambda v, r, *_: (v + r,))
    alias = {}
    if prev is not None:
        alias = {2 + len(extra): 0}
        extra, especs = extra + [prev], especs + [_ANY]
        epi = epi or (lambda v, *_: (v,))
    return _matmul(name, [(a, b)], [(pl.BlockSpec((tm, tk), lambda i, n, k: (i + i0, k)), pl.BlockSpec((tk, tn), lambda i, n, k: (k, n)))],
                   NN, grid, [_sds((M, N), out_dtype)], [ospec], nred=nk, acc_shape=(tm, tn),
                   extra=extra, extra_specs=especs, epilogue=epi, alias=alias)[0]


def mm_tn(name, x, dy, tr, tc, out_dtype=BF16):
    M, R = x.shape
    C = dy.shape[1]
    grid = (R // tr, C // tc)
    return _matmul(name, [(x, dy)], [(pl.BlockSpec((M, tr), lambda r, c: (0, r)), pl.BlockSpec((M, tc), lambda r, c: (0, c)))],
                   TN, grid, [_sds((R, C), out_dtype)], [pl.BlockSpec((tr, tc), lambda r, c: (r, c))])[0]


def mm_tn_colblk(name, x, dy, nb, out_dtype=BF16):
    M, R = x.shape
    J = dy.shape[1] // nb
    grid = (J,)
    return _matmul(name, [(x, dy)], [(pl.BlockSpec((M, R), lambda j: (0, 0)), pl.BlockSpec((M, nb), lambda j: (0, j)))],
                   TN, grid, [_sds((J, R, nb), out_dtype)], [pl.BlockSpec((None, R, nb), lambda j: (j, 0, 0))])[0]


def _rows(T):
    return min(256, T)


def rms_fwd(name, x, g, eps=NORM_EPS):
    T, D = x.shape
    tm = _rows(T)

    def body(x_ref, g_ref, o_ref):
        xv = x_ref[...]
        r = lax.rsqrt(jnp.mean(xv * xv, axis=-1, keepdims=True) + eps)
        o_ref[...] = (xv * r * g_ref[...]).astype(BF16)

    return _pcall(body, name=name, grid=(T // tm,),
                  in_specs=[pl.BlockSpec((tm, D), lambda i: (i, 0)), pl.BlockSpec((1, D), lambda i: (0, 0))],
                  out_specs=pl.BlockSpec((tm, D), lambda i: (i, 0)), out_shape=_sds((T, D), BF16),
                  compiler_params=_cparams(("arbitrary",)))(x, g)


def rms_bwd(name, x, g, dy, dres, eps=NORM_EPS):
    T, D = x.shape
    tm = _rows(T)

    def body(x_ref, g_ref, dy_ref, dr_ref, dx_ref, dxb_ref, dg_ref):
        i = pl.program_id(0)
        xv = x_ref[...]
        r = lax.rsqrt(jnp.mean(xv * xv, axis=-1, keepdims=True) + eps)
        xh = xv * r
        dyv = dy_ref[...]
        gd = dyv * g_ref[...]
        dx = r * (gd - xh * jnp.mean(gd * xh, axis=-1, keepdims=True)) + dr_ref[...]
        dx_ref[...] = dx
        dxb_ref[...] = dx.astype(BF16)

        @pl.when(i == 0)
        def _():
            dg_ref[...] = jnp.zeros_like(dg_ref)

        dg_ref[...] += jnp.sum(dyv * xh, axis=0, keepdims=True)

    row = pl.BlockSpec((tm, D), lambda i: (i, 0))
    vec = pl.BlockSpec((1, D), lambda i: (0, 0))
    return _pcall(body, name=name, grid=(T // tm,), in_specs=[row, vec, row, row], out_specs=[row, row, vec],
                  out_shape=[_sds((T, D), F32), _sds((T, D), BF16), _sds((1, D), F32)],
                  compiler_params=_cparams(("arbitrary",)))(x, g, dy, dres)


def gnorm_fwd(y, projp, g):
    T, D = y.shape
    tm = _rows(T)

    def body(y_ref, z_ref, g_ref, o_ref):
        yz = y_ref[...] * _silu(z_ref[...])
        r = lax.rsqrt(jnp.mean(yz * yz, axis=-1, keepdims=True) + SSM_NORM_EPS)
        o_ref[...] = (yz * r * g_ref[...]).astype(BF16)

    row = pl.BlockSpec((tm, D), lambda i: (i, 0))
    return _pcall(body, name="gnorm_fwd", grid=(T // tm,),
                  in_specs=[row, pl.BlockSpec((tm, D), lambda i: (i, OFF_Z // D)), pl.BlockSpec((1, D), lambda i: (0, 0))],
                  out_specs=row, out_shape=_sds((T, D), BF16), compiler_params=_cparams(("arbitrary",)))(y, projp, g)


def gnorm_bwd(y, projp, g, dyn, dproj):
    T, D = y.shape
    tm = _rows(T)

    def body(y_ref, z_ref, g_ref, dyn_ref, _, dy_ref, dz_ref, dg_ref):
        i = pl.program_id(0)
        yv, zv = y_ref[...], z_ref[...]
        sz = _silu(zv)
        yz = yv * sz
        r = lax.rsqrt(jnp.mean(yz * yz, axis=-1, keepdims=True) + SSM_NORM_EPS)
        xh = yz * r
        dv = dyn_ref[...]
        gd = dv * g_ref[...]
        dyz = r * (gd - xh * jnp.mean(gd * xh, axis=-1, keepdims=True))
        dy_ref[...] = dyz * sz
        dz_ref[...] = (dyz * yv * _dsilu(zv)).astype(BF16)

        @pl.when(i == 0)
        def _():
            dg_ref[...] = jnp.zeros_like(dg_ref)

        dg_ref[...] += jnp.sum(dv * xh, axis=0, keepdims=True)

    row = pl.BlockSpec((tm, D), lambda i: (i, 0))
    vec = pl.BlockSpec((1, D), lambda i: (0, 0))
    return _pcall(body, name="gnorm_bwd", grid=(T // tm,),
                  in_specs=[row, pl.BlockSpec((tm, D), lambda i: (i, OFF_Z // D)), vec, row, _ANY],
                  out_specs=[row, pl.BlockSpec((tm, D), lambda i: (i, OFF_Z // D)), vec],
                  out_shape=[_sds((T, D), F32), _sds(dproj.shape, dproj.dtype), _sds((1, D), F32)], input_output_aliases={4: 1},
                  compiler_params=_cparams(("arbitrary",)))(y, projp, g, dyn, dproj)


def merge_fwd(projp, out_a, out_s):
    T, D = out_a.shape
    tm = _rows(T)

    def body(ga_ref, gs_ref, a_ref, s_ref, o_ref):
        o_ref[...] = (_sigmoid(ga_ref[...]) * a_ref[...] + _sigmoid(gs_ref[...]) * s_ref[...]).astype(BF16)

    row = pl.BlockSpec((tm, D), lambda i: (i, 0))
    return _pcall(body, name="merge_fwd", grid=(T // tm,),
                  in_specs=[pl.BlockSpec((tm, D), lambda i: (i, OFF_GA // D)), pl.BlockSpec((tm, D), lambda i: (i, OFF_GS // D)), row, row],
                  out_specs=row, out_shape=_sds((T, D), BF16), compiler_params=_cparams(("arbitrary",)))(projp, projp, out_a, out_s)


def merge_bwd(projp, out_a, out_s, dmerged):
    T, D = out_a.shape
    tm = _rows(T)
    assert OFF_GA == 0 and OFF_GS == D

    def body(ga_ref, gs_ref, a_ref, s_ref, dm_ref, da_ref, ds_ref, dp_ref):
        dm = dm_ref[...]
        sa, ss = _sigmoid(ga_ref[...]), _sigmoid(gs_ref[...])
        da_ref[...] = (dm * sa).astype(BF16)
        ds_ref[...] = (dm * ss).astype(BF16)
        dp_ref[:, :D] = (dm * a_ref[...] * sa * (1.0 - sa)).astype(BF16)
        dp_ref[:, D:] = (dm * s_ref[...] * ss * (1.0 - ss)).astype(BF16)

    row = pl.BlockSpec((tm, D), lambda i: (i, 0))
    return _pcall(body, name="merge_bwd", grid=(T // tm,),
                  in_specs=[pl.BlockSpec((tm, D), lambda i: (i, OFF_GA // D)), pl.BlockSpec((tm, D), lambda i: (i, OFF_GS // D)), row, row, row],
                  out_specs=[row, row, pl.BlockSpec((tm, 2 * D), lambda i: (i, 0))],
                  out_shape=[_sds((T, D), BF16), _sds((T, D), BF16), _sds((T, IN_PAD), BF16)],
                  compiler_params=_cparams(("arbitrary",)))(projp, projp, out_a, out_s, dmerged)


def head_fwd_bwd(h2, pg, pp, g_final, target):
    T, D = h2.shape
    tm = _rows(T)

    def body(h_ref, pg_ref, pp_ref, g_ref, t_ref, loss_ref, dh_ref, dpg_ref, dpp_ref, dg_ref):
        i = pl.program_id(0)
        s = _sigmoid(pg_ref[...])
        ppv = pp_ref[...]
        h3 = h_ref[...] + s * ppv
        r = lax.rsqrt(jnp.mean(h3 * h3, axis=-1, keepdims=True) + NORM_EPS)
        xh = h3 * r
        gv = g_ref[...]
        e = xh * gv - t_ref[...]
        dyo = e * (1.0 / D)
        gd = dyo * gv
        dh = r * (gd - xh * jnp.mean(gd * xh, axis=-1, keepdims=True))
        dh_ref[...] = dh
        dpg_ref[...] = (dh * ppv * s * (1.0 - s)).astype(BF16)
        dpp_ref[...] = (dh * s).astype(BF16)

        @pl.when(i == 0)
        def _():
            dg_ref[...] = jnp.zeros_like(dg_ref)
            loss_ref[...] = jnp.zeros_like(loss_ref)

        dg_ref[...] += jnp.sum(dyo * xh, axis=0, keepdims=True)
        part = 0.5 * jnp.sum(jnp.mean(e * e, axis=-1, keepdims=True), axis=0, keepdims=True)
        loss_ref[...] += jnp.broadcast_to(part, loss_ref.shape)

    row = pl.BlockSpec((tm, D), lambda i: (i, 0))
    vec = pl.BlockSpec((1, D), lambda i: (0, 0))
    return _pcall(body, name="head_fwd_bwd", grid=(T // tm,), in_specs=[row, row, row, vec, row],
                  out_specs=[pl.BlockSpec((1, 128), lambda i: (0, 0)), row, row, row, vec],
                  out_shape=[_sds((1, 128), F32), _sds((T, D), F32), _sds((T, D), BF16), _sds((T, D), BF16), _sds((1, D), F32)],
                  compiler_params=_cparams(("arbitrary",)))(h2, pg, pp, g_final, target)


FFN_TILE = 1408


def ffn_up(f, wgt, wut):
    T, D = f.shape
    H = wgt.shape[0]
    tm = _row_tile(T)

    def body(f_ref, wg_ref, wu_ref, g_ref, u_ref, a_ref):
        fv = f_ref[...]
        g = _dot(fv, wg_ref[...], NT)
        u = _dot(fv, wu_ref[...], NT)
        g_ref[...] = g.astype(BF16)
        u_ref[...] = u.astype(BF16)
        a_ref[...] = (_silu(g) * u).astype(BF16)

    th = 512
    wspec = pl.BlockSpec((th, D), lambda i, j: (j, 0))
    ospec = pl.BlockSpec((tm, th), lambda i, j: (i, j))
    return _pcall(body, name="ffn_up", grid=(T // tm, H // th), in_specs=[pl.BlockSpec((tm, D), lambda i, j: (i, 0)), wspec, wspec],
                  out_specs=[ospec] * 3, out_shape=[_sds((T, H), BF16)] * 3,
                  compiler_params=_cparams(("arbitrary", "arbitrary")))(f, wgt, wut)


def ffn_down_bwd(dh2b, wd, gate, up):
    T, D = dh2b.shape
    H = wd.shape[0]
    tm = _row_tile(T)
    ospec = pl.BlockSpec((tm, FFN_TILE), lambda i, j: (i, j))

    def epi(da, g, u):
        g, u = g.astype(F32), u.astype(F32)
        return (da * u * _dsilu(g), da * _silu(g))

    return _matmul("ffn_down_bwd", [(dh2b, wd)],
                   [(pl.BlockSpec((tm, D), lambda i, j: (i, 0)), pl.BlockSpec((FFN_TILE, D), lambda i, j: (j, 0)))],
                   NT, (T // tm, H // FFN_TILE), [_sds((T, H), BF16)] * 2, [ospec, ospec],
                   extra=(gate, up), extra_specs=(ospec, ospec), epilogue=epi)


def ffn_up_bwd(dgate, dup, wgt, wut):
    T, H = dgate.shape
    D = wgt.shape[1]
    tm = min(512, T)
    tn = 512
    aspec = pl.BlockSpec((tm, H), lambda i, n: (i, 0))
    wspec = pl.BlockSpec((H, tn), lambda i, n: (0, n))
    return _matmul("ffn_up_bwd", [(dgate, wgt), (dup, wut)], [(aspec, wspec), (aspec, wspec)], NN, (T // tm, D // tn),
                   [_sds((T, D), F32)], [pl.BlockSpec((tm, tn), lambda i, n: (i, n))])[0]


def attn_br_bwd(dout_a, wab):
    T, D = dout_a.shape
    J, R, nb = wab.shape
    tm = _row_tile(T)
    return _matmul("attn_br_bwd", [(dout_a, wab)],
                   [(pl.BlockSpec((tm, nb), lambda i, j: (i, j)), pl.BlockSpec((None, R, nb), lambda i, j: (j, 0, 0)))],
                   NT, (T // tm, J), [_sds((T, R), BF16)], [pl.BlockSpec((tm, R), lambda i, j: (i, 0))], nred=J, acc_shape=(tm, R))[0]


def _adam_math(w, g, m, v):
    m2 = ADAM_B1 * m + (1.0 - ADAM_B1) * g
    v2 = ADAM_B2 * v + (1.0 - ADAM_B2) * (g * g)
    m_hat = m2 / (1.0 - ADAM_B1 ** ADAM_STEP)
    v_hat = v2 / (1.0 - ADAM_B2 ** ADAM_STEP)
    delta = -ADAM_LR * (m_hat / (jnp.sqrt(v_hat) + ADAM_EPS) + ADAM_WD * w)
    return delta, m2, v2


def _sum_partials(own, parts):
    g = None if own is None else own.astype(F32)
    if parts is not None:
        for s in range(parts.shape[0]):
            t = parts[s].astype(F32)
            g = t if g is None else g + t
    return g


def adamw(name, parts, w, m, v, own=None, own_slot=None):
    R, C = w.shape
    tr, tc = R, C
    for cand in (256, 176, 128, 64, 32, 16, 8):
        if R % cand == 0 and R > cand:
            tr = cand
            break
    if tr == R and R > 256:
        tc = 256
    given = [a for a in (parts, own) if a is not None]
    pre = own_slot is not None

    def body(*refs):
        refs = refs[1:] if pre else refs
        p_ref = refs[0] if parts is not None else None
        o_ref = refs[len(given) - 1] if own is not None else None
        w_ref, m_ref, v_ref, g_ref, d_ref, m2_ref, v2_ref = refs[-7:]
        g = _sum_partials(None if o_ref is None else o_ref[...], p_ref)
        d, m2, v2 = _adam_math(w_ref[...], g, m_ref[...], v_ref[...])
        g_ref[...] = g
        d_ref[...] = d
        m2_ref[...] = m2
        v2_ref[...] = v2

    blk = pl.BlockSpec((tr, tc), lambda i, j, *s: (i, j))
    specs = [] if parts is None else [pl.BlockSpec((parts.shape[0], tr, tc), lambda i, j, *s: (0, i, j))]
    if own is not None:
        specs.append(pl.BlockSpec((None, tr, tc), lambda i, j, s: (s[0], i, j)) if pre else blk)
    specs += [blk] * 3
    grid = (R // tr, C // tc)
    out_shape = [_sds((R, C), F32)] * 4
    params = _cparams(("arbitrary", "arbitrary"))
    if not pre:
        return _pcall(body, name=name, grid=grid, in_specs=specs, out_specs=[blk] * 4, out_shape=out_shape,
                      compiler_params=params)(*given, w, m, v)
    spec = pltpu.PrefetchScalarGridSpec(num_scalar_prefetch=1, grid=grid, in_specs=specs, out_specs=[blk] * 4)
    return _pcall(body, name=name, grid_spec=spec, out_shape=out_shape,
                  compiler_params=params)(jnp.asarray(own_slot, jnp.int32).reshape(1), *given, w, m, v)


_HBM = pl.BlockSpec(memory_space=pltpu.HBM)
_SEM = pl.BlockSpec(memory_space=pltpu.SEMAPHORE)
_ANY = pl.BlockSpec(memory_space=pl.ANY)
_SPLIT_PARAMS = dict(compiler_params=pltpu.CompilerParams(has_side_effects=pltpu.SideEffectType.DATAFLOW_SIDE_EFFECTING))
ICI_SAME_CORE = (2, 4, 6)
ALL_PEERS = (1, 2, 3, 4, 5, 6, 7)
LAND_SLOTS = {"gather": N_DEV, "scatter": N_DEV - 1, "pair": 4, "scatter_core": 3}


def _mesh_pos():
    x, y, c = lax.axis_index("x"), lax.axis_index("y"), lax.axis_index("c")
    return x, y, c, 4 * x + 2 * y + c


def _peer_of(k, x, y, c):
    px = 1 - x if k & 4 else x
    py = 1 - y if k & 2 else y
    pc = 1 - c if k & 1 else c
    return (px, py, pc), 4 * px + 2 * py + pc


def _split_copies(mode, ks, srcs, lands, send_sems, recv_sems):
    x, y, c, me = _mesh_pos()
    pairs = []
    for a in range(len(lands)):
        for j, k in enumerate(ks):
            dev, peer = _peer_of(k if mode != "route" else k[0], x, y, c)
            i = a * len(ks) + j
            if mode == "gather":
                s_out, d_out, d_in = lands[a].at[me], lands[a].at[me], lands[a].at[peer]
            elif mode == "scatter":
                s_out, d_out, d_in = srcs[a].at[peer], lands[a].at[k - 1], lands[a].at[k - 1]
            elif mode == "pair":
                dev, _ = _peer_of(1, x, y, c)
                _, theirs = _peer_of(k | 1, x, y, c)
                s_out, d_out, d_in = srcs[a].at[theirs], lands[a].at[j], lands[a].at[j]
            elif mode == "scatter_core":
                s_out, d_out, d_in = srcs[a].at[j + 1], lands[a].at[j], lands[a].at[j]
            elif mode == "route":
                k, rel, half = k
                dev, _ = _peer_of(k, x, y, c)
                _, held = _peer_of(rel, x, y, c)
                _, theirs = _peer_of(k ^ rel, x, y, c)
                cols = lands[a].shape[-1] // 2
                cut = (slice(None), slice(None)) if half is None else (slice(None), pl.ds(half * cols, cols))
                s_out, d_out, d_in = lands[a].at[held].at[cut], lands[a].at[held].at[cut], lands[a].at[theirs].at[cut]
            else:
                dev, _ = _peer_of(1, x, y, c)
                _, theirs = _peer_of(k | 1, x, y, c)
                s_out, d_out, d_in = lands[a].at[peer], lands[a].at[peer], lands[a].at[theirs]
            both = [pltpu.make_async_remote_copy(src_ref=s_out, dst_ref=d, send_sem=send_sems.at[i], recv_sem=recv_sems.at[i],
                                                 device_id=dev, device_id_type=pl.DeviceIdType.MESH) for d in (d_out, d_in)]
            pairs.append(tuple(both))
    return pairs


def split_start(name, mode, ks, srcs, lands=None, after=None):
    n, nk = len(srcs) if lands is None else len(lands), len(ks)
    srcs = [pltpu.with_memory_space_constraint(s, pltpu.HBM) for s in srcs]
    if lands is None:
        shapes = [((N_DEV,) + s.shape) if mode == "gather" else ((LAND_SLOTS[mode],) + s.shape[1:]) for s in srcs]
        lands = [lax.empty(shp, s.dtype) for shp, s in zip(shapes, srcs)]
    lands = [pltpu.with_memory_space_constraint(l, pltpu.HBM) for l in lands]
    both = srcs + lands
    extra = [] if after is None else [after]

    def body(*refs):
        src_refs, land_refs = refs[:len(srcs)], refs[len(srcs):len(both)]
        send_sems, recv_sems = refs[len(both) + len(extra)], refs[len(both) + len(extra) + 1]
        token = refs[-1]
        for out, _ in _split_copies(mode, ks, src_refs, land_refs, send_sems, recv_sems):
            out.start()
        token[...] = jnp.zeros_like(token)

    out_shape = (pltpu.SemaphoreType.DMA((n * nk,)), pltpu.SemaphoreType.DMA((n * nk,)),
                 *[pltpu.HBM(a.shape, a.dtype) for a in both], _sds((8, 128), F32))
    res = _raw_call(body, name=name, out_shape=out_shape, in_specs=[_HBM] * len(both) + [_ANY] * len(extra),
                    out_specs=(_SEM, _SEM, *[_HBM] * len(both), pl.BlockSpec(memory_space=pltpu.VMEM)),
                    input_output_aliases={i: 2 + i for i in range(len(both))}, **_SPLIT_PARAMS)(*both, *extra)
    _PENDING.append(res[-1])
    return dict(mode=mode, ks=ks, sems=(res[0], res[1]), srcs=list(res[2:2 + len(srcs)]),
                lands=list(res[2 + len(srcs):2 + len(both)]), token=res[-1])


def split_wait(name, h, after):
    ns = len(h["srcs"])
    both = h["srcs"] + h["lands"]
    after = list(after) if isinstance(after, (list, tuple)) else [after]

    def body(*refs):
        src_refs, land_refs = refs[:ns], refs[ns:len(both)]
        send_sems, recv_sems = refs[len(both)], refs[len(both) + 1]
        for out, arriving in _split_copies(h["mode"], h["ks"], src_refs, land_refs, send_sems, recv_sems):
            out.wait_send()
            arriving.wait_recv()

    res = _raw_call(body, name=name, out_shape=tuple(pltpu.HBM(a.shape, a.dtype) for a in both),
                    in_specs=[_HBM] * len(both) + [_SEM, _SEM] + [_ANY] * len(after), out_specs=tuple([_HBM] * len(both)),
                    input_output_aliases={i: i for i in range(len(both))}, **_SPLIT_PARAMS)(*both, *h["sems"], *after)
    return list(res[:ns]), list(res[ns:])


FORWARD_BLOCKS = (0, 2, 4, 6)


def pair_sum(name, mine, slots, theirs):
    P, R, C = theirs.shape
    tc = 512

    def body(s_ref, a_ref, b_ref, o_ref):
        o_ref[...] = (a_ref[...].astype(F32) + b_ref[...].astype(F32)).astype(o_ref.dtype)

    blk = pl.BlockSpec((None, R, tc), lambda p, i, s: (p, 0, i))
    spec = pltpu.PrefetchScalarGridSpec(num_scalar_prefetch=1, grid=(P, C // tc),
                                        in_specs=[pl.BlockSpec((None, R, tc), lambda p, i, s: (s[p], 0, i)), blk], out_specs=blk)
    return _pcall(body, name=name, grid_spec=spec, out_shape=_sds((P, R, C), theirs.dtype),
                  compiler_params=_cparams(("arbitrary", "arbitrary")))(slots, mine, theirs)


def _rope_parts(pos_ref, inv_ref):
    ang = pos_ref[...] * inv_ref[...]
    return jnp.cos(ang), jnp.sin(ang)


def _rot_half(t):
    lane = lax.broadcasted_iota(jnp.int32, t.shape, 1)
    return jnp.where((lane % HEAD_DIM) < HEAD_DIM // 2, -pltpu.roll(t, 128 - HEAD_DIM // 2, 1), pltpu.roll(t, HEAD_DIM // 2, 1))


def _attn_mask(n):
    row = lax.broadcasted_iota(jnp.int32, (BLK, 2 * BLK), 0)
    col = lax.broadcasted_iota(jnp.int32, (BLK, 2 * BLK), 1)
    dist = row + BLK - col
    return (dist >= 0) & (dist < BLK) & ((n * BLK - BLK + col) >= 0)


def _attn_specs(T):
    prev = lambda n: jnp.maximum(n - 1, 0)
    kc = pl.BlockSpec((BLK, KV_DIM), lambda n: (n, OFF_K // KV_DIM))
    kp = pl.BlockSpec((BLK, KV_DIM), lambda n: (prev(n), OFF_K // KV_DIM))
    vc = pl.BlockSpec((BLK, KV_DIM), lambda n: (n, OFF_V // KV_DIM))
    vp = pl.BlockSpec((BLK, KV_DIM), lambda n: (prev(n), OFF_V // KV_DIM))
    pc = pl.BlockSpec((BLK, 1), lambda n: (n, 0))
    pp = pl.BlockSpec((BLK, 1), lambda n: (prev(n), 0))
    inv = pl.BlockSpec((1, 128), lambda n: (0, 0))
    sink = pl.BlockSpec(memory_space=pltpu.SMEM)
    return kc, kp, vc, vp, pc, pp, inv, sink


def _softmax_sink(sc, valid, sink):
    sc = jnp.where(valid, sc * (HEAD_DIM ** -0.5), -1e30)
    m = jnp.maximum(jnp.max(sc, axis=1, keepdims=True), sink)
    e = jnp.exp(sc - m)
    es = jnp.exp(sink - m)
    den = jnp.sum(e, axis=1, keepdims=True) + es
    return e / den, es / den


def attn_fwd(projp, posf, inv128, sinks):
    T = projp.shape[0]
    kc, kp, vc, vp, pc, pp, inv, sink = _attn_specs(T)

    def body(q_ref, kc_ref, kp_ref, vc_ref, vp_ref, pc_ref, pp_ref, inv_ref, sink_ref, o_ref, qr_ref, kr_ref):
        n = pl.program_id(0)
        cos_c, sin_c = _rope_parts(pc_ref, inv_ref)
        cos_p, sin_p = _rope_parts(pp_ref, inv_ref)
        valid = _attn_mask(n)
        k_c, k_p = [], []
        for s in range(KV_DIM // 128):
            t = kc_ref[:, 128 * s:128 * (s + 1)]
            k_c.append((t * cos_c + _rot_half(t) * sin_c).astype(BF16))
            kr_ref[:, 128 * s:128 * (s + 1)] = k_c[s]
            t = kp_ref[:, 128 * s:128 * (s + 1)]
            k_p.append((t * cos_p + _rot_half(t) * sin_p).astype(BF16))
        kcat, vcat = [], []
        for hk in range(KV_HEADS):
            lo = HEAD_DIM * (hk % 2)
            kcat.append(jnp.concatenate([k_p[hk // 2][:, lo:lo + HEAD_DIM], k_c[hk // 2][:, lo:lo + HEAD_DIM]], axis=0))
            vcat.append(jnp.concatenate([vp_ref[:, HEAD_DIM * hk:HEAD_DIM * (hk + 1)], vc_ref[:, HEAD_DIM * hk:HEAD_DIM * (hk + 1)]], axis=0)
                        .astype(BF16))
        q_heads = []
        for s in range(Q_DIM // 128):
            t = q_ref[:, 128 * s:128 * (s + 1)]
            qs = (t * cos_c + _rot_half(t) * sin_c).astype(BF16)
            qr_ref[:, 128 * s:128 * (s + 1)] = qs
            q_heads += [qs[:, :HEAD_DIM], qs[:, HEAD_DIM:]]
        G = ATTN_HEADS // KV_HEADS
        scores = [_dot(q_heads[hq], kcat[hq // G], NT) for hq in range(ATTN_HEADS)]
        probs = [_softmax_sink(scores[hq], valid, sink_ref[0, hq])[0] for hq in range(ATTN_HEADS)]
        outs = [_dot(probs[hq], vcat[hq // G], NN) for hq in range(ATTN_HEADS)]
        for s in range(Q_DIM // 128):
            o_ref[:, 128 * s:128 * (s + 1)] = jnp.concatenate([outs[2 * s], outs[2 * s + 1]], axis=1).astype(BF16)

    qspec = pl.BlockSpec((BLK, Q_DIM), lambda n: (n, OFF_Q // Q_DIM))
    orow = pl.BlockSpec((BLK, Q_DIM), lambda n: (n, 0))
    krow = pl.BlockSpec((BLK, KV_DIM), lambda n: (n, 0))
    return _pcall(body, name="attn_fwd", grid=(T // BLK,), in_specs=[qspec, kc, kp, vc, vp, pc, pp, inv, sink],
                  out_specs=[orow, orow, krow], out_shape=[_sds((T, Q_DIM), BF16), _sds((T, Q_DIM), BF16), _sds((T, KV_DIM), BF16)],
                  compiler_params=_cparams(("arbitrary",)))(projp, projp, projp, projp, projp, posf, posf, inv128, sinks)


def attn_bwd(qr, kr, projp, dattn, posf, inv128, sinks, dproj):
    T = projp.shape[0]
    _, _, vc, vp, pc, pp, inv, sink = _attn_specs(T)
    G = ATTN_HEADS // KV_HEADS

    def body(qr_ref, krc_ref, krp_ref, vc_ref, vp_ref, do_ref, pc_ref, pp_ref, inv_ref, sink_ref, _, dq_ref, dk_ref, dv_ref, dsk_ref):
        n = pl.program_id(0)

        @pl.when(n == 0)
        def _():
            dk_ref[...] = jnp.zeros_like(dk_ref)
            dv_ref[...] = jnp.zeros_like(dv_ref)
            dsk_ref[...] = jnp.zeros_like(dsk_ref)

        cos_c, sin_c = _rope_parts(pc_ref, inv_ref)
        cos_p, sin_p = _rope_parts(pp_ref, inv_ref)
        valid = _attn_mask(n)
        lane = lax.broadcasted_iota(jnp.int32, (1, 128), 1)
        kcat, vcat = [], []
        for hk in range(KV_HEADS):
            ksl = slice(HEAD_DIM * hk, HEAD_DIM * (hk + 1))
            kcat.append(jnp.concatenate([krp_ref[:, ksl], krc_ref[:, ksl]], axis=0))
            vcat.append(jnp.concatenate([vp_ref[:, ksl], vc_ref[:, ksl]], axis=0).astype(BF16))
        H = range(ATTN_HEADS)
        q_heads = [qr_ref[:, HEAD_DIM * hq:HEAD_DIM * (hq + 1)] for hq in H]
        do_heads = [do_ref[:, HEAD_DIM * hq:HEAD_DIM * (hq + 1)] for hq in H]
        soft = [_softmax_sink(_dot(q_heads[hq], kcat[hq // G], NT), valid, sink_ref[0, hq]) for hq in H]
        dps = [_dot(do_heads[hq], vcat[hq // G], NT) for hq in H]
        deltas = [jnp.sum(soft[hq][0] * dps[hq], axis=1, keepdims=True) for hq in H]
        dss = [(soft[hq][0] * (dps[hq] - deltas[hq]) * (HEAD_DIM ** -0.5)).astype(BF16) for hq in H]
        pbs = [soft[hq][0].astype(BF16) for hq in H]
        dsk = jnp.zeros((1, 128), F32)
        for hq in H:
            dsk = dsk + jnp.where(lane == hq, -jnp.sum(soft[hq][1] * deltas[hq], axis=0, keepdims=True), 0.0)
        dsk_ref[...] += dsk
        dq_heads = [_dot(dss[hq], kcat[hq // G], NN) for hq in H]
        dk_parts = [_dot(dss[hq], q_heads[hq], TN) for hq in H]
        dv_parts = [_dot(pbs[hq], do_heads[hq], TN) for hq in H]
        dk_heads = [sum(dk_parts[G * hk + 1:G * (hk + 1)], dk_parts[G * hk]) for hk in range(KV_HEADS)]
        dv_heads = [sum(dv_parts[G * hk + 1:G * (hk + 1)], dv_parts[G * hk]) for hk in range(KV_HEADS)]
        for s in range(Q_DIM // 128):
            t = jnp.concatenate([dq_heads[2 * s], dq_heads[2 * s + 1]], axis=1)
            dq_ref[:, 128 * s:128 * (s + 1)] = (t * cos_c - _rot_half(t) * sin_c).astype(BF16)
        cur = pl.ds(pl.multiple_of(n * BLK, BLK), BLK)
        prv = pl.ds(pl.multiple_of(jnp.maximum(n - 1, 0) * BLK, BLK), BLK)
        for s in range(KV_DIM // 128):
            tc = jnp.concatenate([dk_heads[2 * s][BLK:], dk_heads[2 * s + 1][BLK:]], axis=1)
            tp = jnp.concatenate([dk_heads[2 * s][:BLK], dk_heads[2 * s + 1][:BLK]], axis=1)
            cols = slice(128 * s, 128 * (s + 1))
            dk_ref[cur, cols] += tc * cos_c - _rot_half(tc) * sin_c
            dk_ref[prv, cols] += tp * cos_p - _rot_half(tp) * sin_p
            dv_ref[cur, cols] += jnp.concatenate([dv_heads[2 * s][BLK:], dv_heads[2 * s + 1][BLK:]], axis=1)
            dv_ref[prv, cols] += jnp.concatenate([dv_heads[2 * s][:BLK], dv_heads[2 * s + 1][:BLK]], axis=1)

    qrow = pl.BlockSpec((BLK, Q_DIM), lambda n: (n, 0))
    krc = pl.BlockSpec((BLK, KV_DIM), lambda n: (n, 0))
    krp = pl.BlockSpec((BLK, KV_DIM), lambda n: (jnp.maximum(n - 1, 0), 0))
    whole = pl.BlockSpec((T, KV_DIM), lambda n: (0, 0))
    return _pcall(body, name="attn_bwd", grid=(T // BLK,), in_specs=[qrow, krc, krp, vc, vp, qrow, pc, pp, inv, sink, _ANY],
                  out_specs=[pl.BlockSpec((BLK, Q_DIM), lambda n: (n, OFF_Q // Q_DIM)), whole, whole, pl.BlockSpec((1, 128), lambda n: (0, 0))],
                  out_shape=[_sds(dproj.shape, dproj.dtype), _sds((T, KV_DIM), F32), _sds((T, KV_DIM), F32), _sds((1, 128), F32)],
                  input_output_aliases={10: 0},
                  compiler_params=_cparams(("arbitrary",)))(qr, kr, kr, projp, projp, dattn, posf, posf, inv128, sinks, dproj)


CONV_CB = 256


def _shift_down(x, s):
    row = lax.broadcasted_iota(jnp.int32, x.shape, 0)
    return jnp.where(row >= s, pltpu.roll(x, s, 0), 0.0)


def _shift_up(x, s):
    T = x.shape[0]
    row = lax.broadcasted_iota(jnp.int32, x.shape, 0)
    return jnp.where(row < T - s, pltpu.roll(x, T - s, 0), 0.0)


def _conv_pre(x, w_ref, b_ref):
    acc = x * w_ref[CONV_WIDTH - 1:CONV_WIDTH, :] + b_ref[...]
    for s in range(1, CONV_WIDTH):
        acc = acc + _shift_down(x, s) * w_ref[CONV_WIDTH - 1 - s:CONV_WIDTH - s, :]
    return acc


def conv_fwd(projp, conv_w, conv_b):
    T = projp.shape[0]

    def body(x_ref, w_ref, b_ref, o_ref):
        o_ref[...] = _silu(_conv_pre(x_ref[...], w_ref, b_ref))

    return _pcall(body, name="conv_fwd", grid=(CONV_DIM // CONV_CB,),
                  in_specs=[pl.BlockSpec((T, CONV_CB), lambda c: (0, OFF_XBC // CONV_CB + c)),
                            pl.BlockSpec((CONV_WIDTH, CONV_CB), lambda c: (0, c)), pl.BlockSpec((1, CONV_CB), lambda c: (0, c))],
                  out_specs=pl.BlockSpec((T, CONV_CB), lambda c: (0, c)), out_shape=_sds((T, CONV_DIM), F32),
                  compiler_params=_cparams(("arbitrary",)))(projp, conv_w, conv_b)


def conv_bwd(name, projp, dact, conv_w, conv_b, col0, dproj):
    T, C = dact.shape
    c0 = col0 // CONV_CB

    def body(x_ref, da_ref, w_ref, b_ref, _, dx_ref, dw_ref, db_ref):
        x = x_ref[...]
        dpre = da_ref[...] * _dsilu(_conv_pre(x, w_ref, b_ref))
        dx = dpre * w_ref[CONV_WIDTH - 1:CONV_WIDTH, :]
        dw_ref[CONV_WIDTH - 1:CONV_WIDTH, :] = jnp.sum(dpre * x, axis=0, keepdims=True)
        for s in range(1, CONV_WIDTH):
            i = CONV_WIDTH - 1 - s
            dx = dx + _shift_up(dpre, s) * w_ref[i:i + 1, :]
            dw_ref[i:i + 1, :] = jnp.sum(dpre * _shift_down(x, s), axis=0, keepdims=True)
        dx_ref[...] = dx.astype(BF16)
        db_ref[...] = jnp.sum(dpre, axis=0, keepdims=True)

    return _pcall(body, name=name, grid=(C // CONV_CB,),
                  in_specs=[pl.BlockSpec((T, CONV_CB), lambda c: (0, OFF_XBC // CONV_CB + c0 + c)),
                            pl.BlockSpec((T, CONV_CB), lambda c: (0, c)),
                            pl.BlockSpec((CONV_WIDTH, CONV_CB), lambda c: (0, c0 + c)), pl.BlockSpec((1, CONV_CB), lambda c: (0, c0 + c)), _ANY],
                  out_specs=[pl.BlockSpec((T, CONV_CB), lambda c: (0, OFF_XBC // CONV_CB + c0 + c)),
                             pl.BlockSpec((CONV_WIDTH, CONV_CB), lambda c: (0, c)), pl.BlockSpec((1, CONV_CB), lambda c: (0, c))],
                  out_shape=[_sds(dproj.shape, dproj.dtype), _sds((CONV_WIDTH, C), F32), _sds((1, C), F32)],
                  input_output_aliases={4: 0}, compiler_params=_cparams(("arbitrary",)))(projp, dact, conv_w, conv_b, dproj)


def _softplus(x):
    return jnp.maximum(x, 0.0) + jnp.log1p(jnp.exp(-jnp.abs(x)))


def _tri(lower):
    r = lax.broadcasted_iota(jnp.int32, (BLK, BLK), 0)
    c = lax.broadcasted_iota(jnp.int32, (BLK, BLK), 1)
    return (r >= c) if lower else (c >= r)


def _ssd_chunk_setup(dt_ref, dtb_ref, alog_ref):
    raw = dt_ref[...] + dtb_ref[...]
    dt = _softplus(raw)
    aneg = -jnp.exp(alog_ref[...])
    a = dt * aneg
    acs = jnp.dot(_tri(True).astype(F32), a, precision=lax.Precision.HIGHEST, preferred_element_type=F32)
    return raw, dt, aneg, acs, acs.T


def _ssd_specs(T, rev):
    nc = T // BLK
    ci = (lambda c: nc - 1 - c) if rev else (lambda c: c)
    xs = pl.BlockSpec((BLK, D_INNER), lambda c: (ci(c), 0))
    bm = pl.BlockSpec((BLK, SSM_GROUPS * D_STATE), lambda c: (ci(c), D_INNER // (SSM_GROUPS * D_STATE)))
    cm = pl.BlockSpec((BLK, SSM_GROUPS * D_STATE), lambda c: (ci(c), D_INNER // (SSM_GROUPS * D_STATE) + 1))
    dt = pl.BlockSpec((BLK, DT_PAD), lambda c: (ci(c), OFF_DT // DT_PAD))
    v128 = pl.BlockSpec((1, 128), lambda c: (0, 0))
    dfull = pl.BlockSpec((1, D_INNER), lambda c: (0, 0))
    st = pl.BlockSpec((None, SSM_HEADS, HEAD_DIM, D_STATE), lambda c: (ci(c), 0, 0, 0))
    return xs, bm, cm, dt, v128, dfull, st, ci


GW = HEADS_PER_GROUP * HEAD_DIM


def _expanders():
    e = np.zeros((SSM_GROUPS, 128, GW), np.float32)
    for g in range(SSM_GROUPS):
        for hh in range(HEADS_PER_GROUP):
            e[g, HEADS_PER_GROUP * g + hh, HEAD_DIM * hh:HEAD_DIM * (hh + 1)] = 1.0
    return jnp.asarray(e, BF16), jnp.asarray(np.transpose(e, (0, 2, 1)).copy(), BF16)


def _split2(v):
    hi = lax.bitcast_convert_type(lax.bitcast_convert_type(v, jnp.uint32) & jnp.uint32(0xFFFF0000), F32)
    return hi.astype(BF16), (v - hi).astype(BF16)


def _dotx(a, b):
    if a.dtype == BF16:
        hi, lo = _split2(b)
        return jnp.dot(a, hi, preferred_element_type=F32) + jnp.dot(a, lo, preferred_element_type=F32)
    hi, lo = _split2(a)
    return jnp.dot(hi, b, preferred_element_type=F32) + jnp.dot(lo, b, preferred_element_type=F32)


def _decay(acs, acsT, h, tril):
    return jnp.where(tril, jnp.exp(jnp.where(tril, acs[:, h:h + 1] - acsT[h:h + 1, :], 0.0)), 0.0)


def ssd_fwd(xbc, projp, dtb, alog, dfull):
    T = xbc.shape[0]
    nc = T // BLK
    xs, bm, cm, dts, v128, dfs, st, _ = _ssd_specs(T, False)
    E, _ = _expanders()

    def body(xs_ref, b_ref, c_ref, dt_ref, dtb_ref, alog_ref, d_ref, e_ref, y_ref, st_ref, h_scr):
        c = pl.program_id(0)

        @pl.when(c == 0)
        def _():
            h_scr[...] = jnp.zeros_like(h_scr)

        _, dt, _, acs, acsT = _ssd_chunk_setup(dt_ref, dtb_ref, alog_ref)
        tril = _tri(True)
        alast = acs[BLK - 1:BLK, :]
        eacs = jnp.exp(acs)
        wmat = jnp.exp(alast - acs)
        gam = jnp.exp(alast)
        for g in range(SSM_GROUPS):
            gl = slice(GW * g, GW * (g + 1))
            hsl = slice(HEADS_PER_GROUP * g, HEADS_PER_GROUP * (g + 1))
            heads = [HEADS_PER_GROUP * g + hh for hh in range(HEADS_PER_GROUP)]
            Eg = e_ref[g]
            B = b_ref[:, D_STATE * g:D_STATE * (g + 1)].astype(BF16)
            C = c_ref[:, D_STATE * g:D_STATE * (g + 1)].astype(BF16)
            cb = _dot(C, B, NT)
            x_g = xs_ref[:, gl]
            xd_g = x_g * _dotx(dt, Eg)
            hold = h_scr[hsl]
            st_ref[hsl] = hold
            hcat = hold.reshape(GW, D_STATE)
            yoff = _dotx(eacs, Eg) * _dot(C, hcat, NT)
            S = _dot(xd_g * _dotx(wmat, Eg), B, TN)
            Ms = [cb * _decay(acs, acsT, h, tril) for h in heads]
            ys = [_dot(Ms[hh], xd_g[:, HEAD_DIM * hh:HEAD_DIM * (hh + 1)], NN) for hh in range(HEADS_PER_GROUP)]
            for hh, h in enumerate(heads):
                h_scr[h] = gam[:, h:h + 1] * hold[hh] + S[HEAD_DIM * hh:HEAD_DIM * (hh + 1)]
            y_ref[:, gl] = jnp.concatenate(ys, axis=1) + yoff + d_ref[:, gl] * x_g

    espec = pl.BlockSpec((SSM_GROUPS, 128, GW), lambda c: (0, 0, 0))
    return _pcall(body, name="ssd_fwd", grid=(nc,), in_specs=[xs, bm, cm, dts, v128, v128, dfs, espec],
                  out_specs=[xs, st], out_shape=[_sds((T, D_INNER), F32), _sds((nc, SSM_HEADS, HEAD_DIM, D_STATE), F32)],
                  scratch_shapes=[pltpu.VMEM((SSM_HEADS, HEAD_DIM, D_STATE), F32)],
                  compiler_params=_cparams(("arbitrary",)))(xbc, xbc, xbc, projp, dtb, alog, dfull, E)


def ssd_bwd(xbc, projp, dtb, alog, dfull, states, dy, dproj):
    T = xbc.shape[0]
    nc = T // BLK
    xs, bm, cm, dts, v128, dfs, st, ci = _ssd_specs(T, True)
    gn = SSM_GROUPS * D_STATE
    E, ET = _expanders()

    def body(xs_ref, b_ref, c_ref, dt_ref, dtb_ref, alog_ref, d_ref, st_ref, dy_ref, e_ref, et_ref, _,
             dxs_ref, dB_ref, dC_ref, ddt_ref, dal_ref, dD_ref, ddtb_ref, dh_scr):
        i = pl.program_id(0)

        @pl.when(i == 0)
        def _():
            dh_scr[...] = jnp.zeros_like(dh_scr)
            dal_ref[...] = jnp.zeros_like(dal_ref)
            dD_ref[...] = jnp.zeros_like(dD_ref)
            ddtb_ref[...] = jnp.zeros_like(ddtb_ref)

        raw, dt, aneg, acs, acsT = _ssd_chunk_setup(dt_ref, dtb_ref, alog_ref)
        tril = _tri(True)
        lane = lax.broadcasted_iota(jnp.int32, (BLK, 128), 1)
        sub = lax.broadcasted_iota(jnp.int32, (BLK, 128), 0)
        alast = acs[BLK - 1:BLK, :]
        eacs = jnp.exp(acs)
        wmat = jnp.exp(alast - acs)
        gam = jnp.exp(alast)
        gcol = jnp.exp(acsT[:, BLK - 1:BLK])
        ds_col = jnp.zeros((BLK, 128), F32)
        ds_row = jnp.zeros((BLK, 128), F32)
        ddt_col = jnp.zeros((BLK, 128), F32)
        dDm = jnp.zeros((BLK, 128), F32)
        hl = [slice(HEAD_DIM * hh, HEAD_DIM * (hh + 1)) for hh in range(HEADS_PER_GROUP)]
        for g in range(SSM_GROUPS):
            gl = slice(GW * g, GW * (g + 1))
            gs = slice(D_STATE * g, D_STATE * (g + 1))
            hsl = slice(HEADS_PER_GROUP * g, HEADS_PER_GROUP * (g + 1))
            heads = [HEADS_PER_GROUP * g + hh for hh in range(HEADS_PER_GROUP)]
            Eg, ETg = e_ref[g], et_ref[g]
            B = b_ref[:, gs].astype(BF16)
            C = c_ref[:, gs].astype(BF16)
            cb = _dot(C, B, NT)
            x_g, dy_g = xs_ref[:, gl], dy_ref[:, gl]
            dt_x, w_x = _dotx(dt, Eg), _dotx(wmat, Eg)
            xd_g = x_g * dt_x
            dye = dy_g * _dotx(eacs, Eg)
            hcat = st_ref[hsl].reshape(GW, D_STATE)
            dSv = dh_scr[hsl]
            dScat = dSv.reshape(GW, D_STATE)
            dDm = dDm + _dotx(dy_g * x_g, ETg)
            dH_y = _dot(dye, C, TN)
            dC_g = _dot(dye, hcat, NN)
            ds_col = ds_col + _dotx(dye * _dot(C, hcat, NT), ETg)
            dxdw = _dot(B, dScat, NT)
            dB_g = _dot(xd_g * w_x, dScat, NN)
            dww = _dotx(xd_g * dxdw, ETg) * wmat
            ds_col = ds_col - dww + jnp.where(sub == BLK - 1, jnp.sum(dww, axis=0, keepdims=True), 0.0)
            hd = jnp.sum(_dotx(Eg, dScat * hcat), axis=1, keepdims=True) * gcol
            ds_row = ds_row - jnp.where(lane == BLK - 1, hd, 0.0)
            decays = [_decay(acs, acsT, h, tril) for h in heads]
            Ms = [cb * d for d in decays]
            dMs = [_dot(dy_g[:, hl[hh]], xd_g[:, hl[hh]], NT) for hh in range(HEADS_PER_GROUP)]
            dxd1 = [_dot(Ms[hh], dy_g[:, hl[hh]], TN) for hh in range(HEADS_PER_GROUP)]
            dG = jnp.zeros((BLK, BLK), F32)
            for hh, h in enumerate(heads):
                Q = dMs[hh] * Ms[hh]
                ds_col = ds_col + jnp.where(lane == h, jnp.sum(Q, axis=1, keepdims=True), 0.0)
                ds_row = ds_row + jnp.where(sub == h, jnp.sum(Q, axis=0, keepdims=True), 0.0)
                dG = dG + dMs[hh] * decays[hh]
            dxd_g = jnp.concatenate(dxd1, axis=1) + w_x * dxdw
            dxs_ref[:, gl] = d_ref[:, gl] * dy_g + dxd_g * dt_x
            ddt_col = ddt_col + _dotx(dxd_g * x_g, ETg)
            dC_ref[:, gs] = dC_g + _dot(dG, B, NN)
            dB_ref[:, gs] = dB_g + _dot(dG, C, TN)
            for hh, h in enumerate(heads):
                dh_scr[h] = gam[:, h:h + 1] * dSv[hh] + dH_y[hl[hh]]
        ds_all = ds_col - ds_row.T
        da = jnp.dot(_tri(False).astype(F32), ds_all, precision=lax.Precision.HIGHEST, preferred_element_type=F32)
        ddt = ddt_col + da * aneg
        draw = jnp.where(lane < SSM_HEADS, ddt * _sigmoid(raw), 0.0)
        ddt_ref[...] = draw.astype(BF16)
        dal_ref[...] += jnp.sum(da * dt, axis=0, keepdims=True) * aneg
        ddtb_ref[...] += jnp.sum(draw, axis=0, keepdims=True)
        dD_ref[...] += jnp.sum(dDm, axis=0, keepdims=True)

    gblk = pl.BlockSpec((BLK, gn), lambda c: (ci(c), 0))
    espec = pl.BlockSpec((SSM_GROUPS, 128, GW), lambda c: (0, 0, 0))
    etspec = pl.BlockSpec((SSM_GROUPS, GW, 128), lambda c: (0, 0, 0))
    return _pcall(body, name="ssd_bwd", grid=(nc,), in_specs=[xs, bm, cm, dts, v128, v128, dfs, st, xs, espec, etspec, _ANY],
                  out_specs=[xs, gblk, gblk, pl.BlockSpec((BLK, DT_PAD), lambda c: (ci(c), OFF_DT // DT_PAD)), v128, v128, v128],
                  out_shape=[_sds((T, D_INNER), F32), _sds((T, gn), F32), _sds((T, gn), F32), _sds(dproj.shape, dproj.dtype),
                             _sds((1, 128), F32), _sds((1, 128), F32), _sds((1, 128), F32)],
                  scratch_shapes=[pltpu.VMEM((SSM_HEADS, HEAD_DIM, D_STATE), F32)], input_output_aliases={11: 3},
                  compiler_params=_cparams(("arbitrary",)))(xbc, xbc, xbc, projp, dtb, alog, dfull, states, dy, E, ET, dproj)


_WIN_ORDER = ("ga", "gs", "z", "xbc", "q", "k", "v", "dt")


PERM_TILE = 512


def _win_row_moves():
    off = dict(z=OFF_Z, ga=OFF_GA, gs=OFF_GS, xbc=OFF_XBC, q=OFF_Q, k=OFF_K, v=OFF_V, dt=OFF_DT)
    per = IN_DIM // N_DEV
    tiles = [[] for _ in range(-(-IN_PAD // PERM_TILE))]
    for nm in _WIN_ORDER:
        s, w = SEG[nm]
        d = off[nm]
        while w > 0:
            j, r = divmod(s, per)
            n = min(w, per - r, PERM_TILE - d % PERM_TILE)
            tiles[d // PERM_TILE].append((j, r, n, d % PERM_TILE))
            s, w, d = s + n, w - n, d + n
    return tiles


def _win_to_padded(win_g):
    per = IN_DIM // N_DEV
    moves = _win_row_moves()

    def body(w_ref, o_ref, slots, stage, in_sems, out_sems):
        loads = [pltpu.make_async_copy(w_ref.at[j], slots.at[j], in_sems.at[j]) for j in range(N_DEV)]
        for cp in loads:
            cp.start()
        arrived = [False] * N_DEV
        stores = [None, None]
        for t, pieces in enumerate(moves):
            rows = min(PERM_TILE, IN_PAD - PERM_TILE * t)
            b = t % 2
            if stores[b] is not None:
                stores[b].wait()
            filled = 0
            for j, r, n, d in pieces:
                if not arrived[j]:
                    loads[j].wait()
                    arrived[j] = True
                stage[b, pl.ds(d, n), :] = slots[j, pl.ds(r, n), :]
                filled = max(filled, d + n)
            if filled < rows:
                stage[b, pl.ds(filled, rows - filled), :] = jnp.zeros((rows - filled, D_MODEL), stage.dtype)
            stores[b] = pltpu.make_async_copy(stage.at[b, pl.ds(0, rows), :], o_ref.at[pl.ds(PERM_TILE * t, rows), :], out_sems.at[b])
            stores[b].start()
        for cp in stores:
            cp.wait()

    return _pcall(body, name="w_in_rows", in_specs=[_ANY], out_specs=_ANY, out_shape=_sds((IN_PAD, D_MODEL), win_g.dtype),
                  scratch_shapes=[pltpu.VMEM((N_DEV, per, D_MODEL), win_g.dtype), pltpu.VMEM((2, PERM_TILE, D_MODEL), win_g.dtype),
                                  pltpu.SemaphoreType.DMA((N_DEV,)), pltpu.SemaphoreType.DMA((2,))],
                  compiler_params=pltpu.CompilerParams(vmem_limit_bytes=VMEM_LIMIT_ROWS))(win_g)


def _padded_to_win(dw):
    per = IN_DIM // N_DEV
    moves = _win_row_moves()

    def body(d_ref, o_ref, slots, stage, in_sems, out_sems):
        def load(t):
            rows = min(PERM_TILE, IN_PAD - PERM_TILE * t)
            return pltpu.make_async_copy(d_ref.at[pl.ds(PERM_TILE * t, rows), :], stage.at[t % 2, pl.ds(0, rows), :], in_sems.at[t % 2])

        pending = load(0)
        pending.start()
        for t, pieces in enumerate(moves):
            pending.wait()
            if t + 1 < len(moves):
                pending = load(t + 1)
                pending.start()
            for j, r, n, d in pieces:
                slots[j, pl.ds(r, n), :] = stage[t % 2, pl.ds(d, n), :]
        stores = [pltpu.make_async_copy(slots.at[j], o_ref.at[j], out_sems.at[j]) for j in range(N_DEV)]
        for cp in stores:
            cp.start()
        for cp in stores:
            cp.wait()

    return _pcall(body, name="d_w_in_rows", in_specs=[_ANY], out_specs=_ANY, out_shape=_sds((N_DEV, per, D_MODEL), dw.dtype),
                  scratch_shapes=[pltpu.VMEM((N_DEV, per, D_MODEL), dw.dtype), pltpu.VMEM((2, PERM_TILE, D_MODEL), dw.dtype),
                                  pltpu.SemaphoreType.DMA((2,)), pltpu.SemaphoreType.DMA((N_DEV,))],
                  compiler_params=pltpu.CompilerParams(vmem_limit_bytes=VMEM_LIMIT_ROWS))(dw)


def _pad128(v):
    return jnp.pad(v, ((0, 0), (0, 128 - v.shape[1])))


_SMALL = (("loss", 128, 1), ("g_mix", 2048, 2048), ("conv_b", 3072, 3072), ("dt_bias", 128, 32), ("a_log", 128, 32),
          ("d_skip", 128, 32), ("g_ssd", 2048, 2048), ("sinks", 128, 16), ("g_ffn", 2048, 2048), ("g_ple", 2048, 2048),
          ("g_final", 2048, 2048))


def _small_vec(d):
    parts = []
    for nm, pw, w in _SMALL:
        v = d[nm].reshape(1, -1).astype(F32)
        parts.append(jnp.pad(v[:, :min(v.shape[1], pw)], ((0, 0), (0, pw - min(v.shape[1], pw)))))
    return jnp.concatenate(parts, axis=1)


def _small_split(vec):
    out, o = {}, 0
    for nm, pw, w in _SMALL:
        out[nm] = vec[0, o:o + w]
        o += pw
    return out


def kernel(x, p, positions, g_mix, w_in, conv_w, conv_b, dt_bias, a_log, d_skip, g_ssd, sinks, w_attn_br, w_ssd_br, w_o, g_ffn, w_gate, w_up, w_down, g_ple, w_ple_gate, w_ple_proj, g_final, loss_target, m_g_mix, m_w_in, m_conv_w, m_conv_b, m_dt_bias, m_a_log, m_d_skip, m_g_ssd, m_sinks, m_w_attn_br, m_w_ssd_br, m_w_o, m_g_ffn, m_w_gate, m_w_up, m_w_down, m_g_ple, m_w_ple_gate, m_w_ple_proj, m_g_final, v_g_mix, v_w_in, v_conv_w, v_conv_b, v_dt_bias, v_a_log, v_d_skip, v_g_ssd, v_sinks, v_w_attn_br, v_w_ssd_br, v_w_o, v_g_ffn, v_w_gate, v_w_up, v_w_down, v_g_ple, v_w_ple_gate, v_w_ple_proj, v_g_final):
    T = x.shape[1]
    D = D_MODEL
    W = dict(g_mix=g_mix, w_in=w_in, conv_w=conv_w, conv_b=conv_b, dt_bias=dt_bias, a_log=a_log, d_skip=d_skip, g_ssd=g_ssd,
             sinks=sinks, w_attn_br=w_attn_br, w_ssd_br=w_ssd_br, w_o=w_o, g_ffn=g_ffn, w_gate=w_gate, w_up=w_up, w_down=w_down,
             g_ple=g_ple, w_ple_gate=w_ple_gate, w_ple_proj=w_ple_proj, g_final=g_final)
    Mo = dict(g_mix=m_g_mix, w_in=m_w_in, conv_w=m_conv_w, conv_b=m_conv_b, dt_bias=m_dt_bias, a_log=m_a_log, d_skip=m_d_skip,
              g_ssd=m_g_ssd, sinks=m_sinks, w_attn_br=m_w_attn_br, w_ssd_br=m_w_ssd_br, w_o=m_w_o, g_ffn=m_g_ffn, w_gate=m_w_gate,
              w_up=m_w_up, w_down=m_w_down, g_ple=m_g_ple, w_ple_gate=m_w_ple_gate, w_ple_proj=m_w_ple_proj, g_final=m_g_final)
    Vo = dict(g_mix=v_g_mix, w_in=v_w_in, conv_w=v_conv_w, conv_b=v_conv_b, dt_bias=v_dt_bias, a_log=v_a_log, d_skip=v_d_skip,
              g_ssd=v_g_ssd, sinks=v_sinks, w_attn_br=v_w_attn_br, w_ssd_br=v_w_ssd_br, w_o=v_w_o, g_ffn=v_g_ffn, w_gate=v_w_gate,
              w_up=v_w_up, w_down=v_w_down, g_ple=v_g_ple, w_ple_gate=v_w_ple_gate, w_ple_proj=v_w_ple_proj, g_final=v_g_final)
    order = ["g_mix", "w_in", "conv_w", "conv_b", "dt_bias", "a_log", "d_skip", "g_ssd", "sinks", "w_attn_br", "w_ssd_br", "w_o",
             "g_ffn", "w_gate", "w_up", "w_down", "g_ple", "w_ple_gate", "w_ple_proj", "g_final"]
    big = ["w_in", "conv_w", "w_attn_br", "w_ssd_br", "w_o", "w_gate", "w_up", "w_down", "w_ple_gate", "w_ple_proj"]

    x2 = x.reshape(T, D)
    p2 = p.reshape(T, PLE_DIM)
    tgt = loss_target.reshape(T, D)
    posf = positions.reshape(T, 1).astype(F32)
    inv = ROPE_THETA ** (-np.arange(HEAD_DIM // 2, dtype=np.float32) * 2.0 / HEAD_DIM)
    inv128 = jnp.asarray(np.tile(inv, 128 // (HEAD_DIM // 2)).reshape(1, 128).astype(np.float32))
    transposed = ("w_in", "w_gate", "w_up")

    def shard2d(a, n):
        a = a.reshape(a.shape[-2:])
        return a.T if n in transposed else a

    sh = {n: shard2d(W[n], n) for n in big}

    del _PENDING[:]
    me = 4 * lax.axis_index("x") + 2 * lax.axis_index("y") + lax.axis_index("c")
    groups = (("w_in",), ("conv_w", "w_attn_br", "w_ssd_br", "w_o"), ("w_gate", "w_up"), ("w_down",), ("w_ple_gate", "w_ple_proj"))
    send = {n: sh[n] if n == "conv_w" else sh[n].astype(BF16) for n in big}
    zones = [[lax.dynamic_update_index_in_dim(lax.empty((N_DEV,) + send[n].shape, send[n].dtype), send[n], me, 0) for n in grp]
             for grp in groups]
    ring_a = split_start("ring_a_start", "route", ((4, 0, 0), (2, 0, 1)), [], lands=zones[0])
    ring_b = split_start("ring_b_start", "route", ((4, 0, 1), (2, 0, 0)), [], lands=ring_a["lands"], after=ring_a["token"])
    gathered, fwd = {}, {}

    def forward_start(gi, after, lands=None):
        if lands is None:
            _, lands = split_wait("gather_wait_%d" % gi, started[gi], after)
        fwd[gi] = split_start("forward_start_%d" % gi, "forward", FORWARD_BLOCKS, [], lands=lands)

    def forward_wait(gi, after):
        _, full = split_wait("forward_wait_%d" % gi, fwd[gi], after)
        gathered.update(zip(groups[gi], full))

    u = rms_fwd("norm_mix", x2, g_mix)
    _, zone = split_wait("ring_a_wait", dict(ring_a, lands=ring_b["lands"]), [u] + [z for grp in zones[1:] for z in grp])
    ring_c = split_start("ring_c_start", "route", ((2, 4, 0), (4, 2, 1)), [], lands=zone)
    started, prev = [None], ring_c["token"]
    for gi in range(1, len(groups)):
        started.append(split_start("gather_start_%d" % gi, "gather", ICI_SAME_CORE, [], lands=zones[gi], after=prev))
        prev = started[-1]["token"]
    _, zone = split_wait("ring_b_wait", dict(ring_b, lands=ring_c["lands"]), u)
    near = split_start("forward_start_0", "route", ((1, 0, None), (1, 2, None), (1, 4, None)), [], lands=zone)
    _, zone = split_wait("ring_c_wait", dict(ring_c, lands=near["lands"]), u)
    far = split_start("forward_far_start_0", "route", ((1, 6, None),), [], lands=zone)
    _, zone = split_wait("forward_wait_0", dict(near, lands=far["lands"]), u)
    _, zone = split_wait("forward_far_wait_0", dict(far, lands=zone), u)
    winp = _win_to_padded(zone[0])
    dtb = _pad128(dt_bias)
    alog = _pad128(a_log)
    dfull = jnp.repeat(d_skip.reshape(SSM_HEADS), HEAD_DIM).reshape(1, D_INNER)

    projp = mm_nt("in_proj", u, winp, IN_PAD // 5)
    forward_start(1, projp)
    attn, qr, kr = attn_fwd(projp, posf, inv128, sinks)
    forward_wait(1, attn)
    convw = jnp.transpose(gathered["conv_w"], (1, 0, 2)).reshape(CONV_WIDTH, CONV_DIM)
    wab = gathered["w_attn_br"]
    wsb = gathered["w_ssd_br"].reshape(D, D)
    wo = gathered["w_o"].reshape(D, D)
    xbc = conv_fwd(projp, convw, conv_b)
    y, states = ssd_fwd(xbc, projp, dtb, alog, dfull)
    yn = gnorm_fwd(y, projp, g_ssd)
    out_a = mm_nn_colblk("attn_br", attn, wab)
    out_s = mm_nn("ssd_br", yn, wsb, 1024)
    forward_start(2, out_s)
    merged = merge_fwd(projp, out_a, out_s)
    h1 = mm_nn("o_proj", merged, wo, 1024, residual=x2)
    f = rms_fwd("norm_ffn", h1, g_ffn)
    forward_wait(2, f)
    forward_start(3, f)
    wgt, wut = (gathered[n].reshape(FFN_HIDDEN, D) for n in ("w_gate", "w_up"))
    gate, up, act = ffn_up(f, wgt, wut)
    forward_wait(3, act)
    forward_start(4, act)
    wd = gathered["w_down"].reshape(FFN_HIDDEN, D)
    h2 = mm_nn_red("ffn_down", act, wd, 512, FFN_HIDDEN, residual=h1)
    r = rms_fwd("norm_ple", h2, g_ple)
    forward_wait(4, r)
    wpg = gathered["w_ple_gate"].reshape(D, D)
    wpp = gathered["w_ple_proj"]
    pg = mm_nn("ple_gate", r, wpg, 1024)
    pp = mm_nn_colblk("ple_proj", p2, wpp)
    loss_v, dh3, dpg, dpp, dg_final = head_fwd_bwd(h2, pg, pp, g_final.reshape(1, D), tgt)

    gw = {}
    scat = []

    def scatter_start(names):
        scat.append((names, split_start("scatter_start_%d" % len(scat), "scatter", ALL_PEERS, [gw[n] for n in names])))

    gw["w_ple_proj"] = mm_tn_colblk("dw_ple_proj", p2, dpp, PLE_DIM)
    dr = mm_nt("d_ple_gate", dpg, wpg, 1024)
    gw["w_ple_gate"] = mm_tn("dw_ple_gate", r, dpg, 512, D).reshape(N_DEV, D // N_DEV, D)
    scatter_start(("w_ple_proj", "w_ple_gate"))
    dh2, dh2b, dg_ple = rms_bwd("norm_ple_bwd", h2, g_ple, dr, dh3)
    dgate, dup = ffn_down_bwd(dh2b, wd, gate, up)
    per = FFN_HIDDEN // N_DEV
    gw["w_down"] = mm_tn("dw_down", act, dh2b, FFN_TILE, D).reshape(N_DEV, per, D)
    gw["w_gate"] = mm_tn("dw_gate", dgate, f, FFN_TILE, D).reshape(N_DEV, per, D)
    gw["w_up"] = mm_tn("dw_up", dup, f, FFN_TILE, D).reshape(N_DEV, per, D)
    scatter_start(("w_down", "w_gate", "w_up"))
    df = ffn_up_bwd(dgate, dup, wgt, wut)
    dh1, dh1b, dg_ffn = rms_bwd("norm_ffn_bwd", h1, g_ffn, df, dh2)
    dmerged = mm_nt("d_o_proj", dh1b, wo, 1024)
    gw["w_o"] = mm_tn("dw_o", merged, dh1b, 512, D).reshape(N_DEV, D // N_DEV, D)
    dout_a, dout_s, dproj = merge_bwd(projp, out_a, out_s, dmerged)
    gw["w_ssd_br"] = mm_tn("dw_ssd_br", yn, dout_s, 512, D).reshape(N_DEV, D // N_DEV, D)
    gw["w_attn_br"] = mm_tn_colblk("dw_attn_br", attn, dout_a, D // N_DEV)
    scatter_start(("w_o", "w_ssd_br", "w_attn_br"))
    dyn = mm_nt("d_ssd_br", dout_s, wsb, 1024)
    dattn = attn_br_bwd(dout_a, wab)
    dy, dproj, dg_ssd = gnorm_bwd(y, projp, g_ssd, dyn, dproj)
    dxs, dbm, dcm, dproj, dal, ddsk, ddtb = ssd_bwd(xbc, projp, dtb, alog, dfull, states, dy, dproj)
    dproj, dwc_x, dbc_x = conv_bwd("conv_bwd_x", projp, dxs, convw, conv_b, 0, dproj)
    dproj, dwc_b, dbc_b = conv_bwd("conv_bwd_b", projp, dbm, convw, conv_b, D_INNER, dproj)
    dproj, dwc_c, dbc_c = conv_bwd("conv_bwd_c", projp, dcm, convw, conv_b, D_INNER + SSM_GROUPS * D_STATE, dproj)
    dproj, dk, dv, dsk = attn_bwd(qr, kr, projp, dattn, posf, inv128, sinks, dproj)
    dproj = lax.dynamic_update_slice(dproj, jnp.concatenate([dk, dv], axis=1).astype(BF16), (0, OFF_K))
    gw_in = _padded_to_win(mm_tn("dw_in", dproj, u, 640, D))
    pair = split_start("pair_start", "pair", FORWARD_BLOCKS, [gw_in])
    dconvw = jnp.concatenate([dwc_x, dwc_b, dwc_c], axis=1)
    gw["conv_w"] = jnp.transpose(dconvw.reshape(CONV_WIDTH, N_DEV, CONV_DIM // N_DEV), (1, 0, 2))
    scatter_start(("conv_w",))
    du_tm = min(512, T)
    tiles = T // du_tm
    first = max(tiles // 4, 1)
    du = mm_nn_red("d_in_proj_a", dproj, winp, 512, IN_PAD, rows=(0, first), tm=du_tm)
    (gw_in,), (sibling_part,) = split_wait("pair_wait", pair, du)
    pair_slots = jnp.stack([jnp.bitwise_xor(me, k) for k in FORWARD_BLOCKS]).astype(jnp.int32)
    core = split_start("core_start", "scatter_core", ICI_SAME_CORE, [pair_sum("pair_sum_w_in", gw_in, pair_slots, sibling_part)])
    if first < tiles:
        du = mm_nn_red("d_in_proj_b", dproj, winp, 512, IN_PAD, rows=(first, tiles - first), prev=du, tm=du_tm)
    gx, _, dg_mix = rms_bwd("norm_mix_bwd", x2, g_mix, du, dh1)

    small_g = dict(loss=loss_v[:, :1], g_mix=dg_mix, conv_b=jnp.concatenate([dbc_x, dbc_b, dbc_c], axis=1), dt_bias=ddtb,
                   a_log=dal, d_skip=ddsk, g_ssd=dg_ssd, sinks=dsk, g_ffn=dg_ffn, g_ple=dg_ple, g_final=dg_final)
    vec = _small_vec(small_g)
    small = split_start("small_start", "gather", ALL_PEERS, [],
                        lands=[lax.dynamic_update_index_in_dim(lax.empty((N_DEV,) + vec.shape, F32), vec, me, 0)])

    res = {}
    after = [gx]
    for si, (names, h) in enumerate(scat):
        srcs, lands = split_wait("scatter_wait_%d" % si, h, after)
        for n, mine, arrived in zip(names, srcs, lands):
            res[n] = adamw("adamw_" + n, arrived, sh[n], shard2d(Mo[n], n), shard2d(Vo[n], n), own=mine, own_slot=me)
        after = [res[n][0] for n in names]
    zero = jnp.zeros((1, 1), F32)
    _, (vec_parts,) = split_wait("small_wait", small, [res[n][0] for n in res])
    sres = adamw("adamw_small", vec_parts, _small_vec({**W, "loss": zero}), _small_vec({**Mo, "loss": zero}),
                 _small_vec({**Vo, "loss": zero}))
    ssplit = [_small_split(a) for a in sres]

    (pair_sums,), (arrived,) = split_wait("core_wait", core, [sres[0]])
    res["w_in"] = adamw("adamw_w_in", arrived, sh["w_in"], shard2d(Mo["w_in"], "w_in"), shard2d(Vo["w_in"], "w_in"),
                        own=pair_sums, own_slot=0)
    loss = ssplit[0]["loss"].reshape(())
    for n in order:
        if n not in res:
            res[n] = tuple(s[n].reshape(W[n].shape) for s in ssplit)
        else:
            res[n] = tuple((a.T if n in transposed else a).reshape(W[n].shape) for a in res[n])
    outs = [loss, gx.reshape(x.shape)]
    for k in range(4):
        outs += [res[n][k] for n in order]
    return tuple(outs)
```
